```python
import math
import jax, jax.numpy as jnp
from jax import lax
import numpy as np

D_MODEL = 1024
BATCH = 2
SEQ = 8192
DEPTH = 1

CTX_LEN = 256
GRID_W = 64
EPS = 1e-6

HG_HEADS = 4
HG_DK = 128
HG_DV = 128
HG_KDIM = HG_HEADS * HG_DK
HG_WIDTH = HG_HEADS * HG_DV
CHUNK = 32

HY_WIDTH = 512
HY_SHORT = 3
HY_EMB = 33
HY_FILTER_HIDDEN = 64
HY_FILTER_SCALE = 0.05
HY_DECAY_TARGET = 1e-2
HY_FAST_PCT = 0.3
HY_SLOW_PCT = 1.5

SPLIT_POINTS = (HG_KDIM, 2 * HG_KDIM, 3 * HG_KDIM, 3 * HG_KDIM + HG_WIDTH,
                3 * HG_KDIM + 2 * HG_WIDTH, 3 * HG_KDIM + 2 * HG_WIDTH + 3 * HY_WIDTH)
IN_WIDTH = 3 * HG_KDIM + 2 * HG_WIDTH + 3 * HY_WIDTH + 2 * D_MODEL

N_GROUPS = 4
EXPERTS_PER_GROUP = 8
TOP_K = 2
D_EXPERT = 512

kernel_name = "hybrid_hgrn2_hyena_hmoe_dit_block"

F32 = jnp.float32


def rms_norm(x, g):
    xf = x.astype(F32)
    y = xf * lax.rsqrt(jnp.mean(xf * xf, axis=-1, keepdims=True) + EPS)
    return (y * g.astype(F32)).astype(x.dtype)


def adaln(cvec, w, b):
    m = jax.nn.silu(cvec) @ w + b
    return [t[..., None, :] for t in jnp.split(m, 6, axis=-1)]


def modulate(h, shift, scale):
    return h * (1.0 + scale) + shift


def sincos_2d(n_tokens):
    rows = n_tokens // GRID_W
    r = jnp.repeat(jnp.arange(rows, dtype=F32), GRID_W)
    col = jnp.tile(jnp.arange(GRID_W, dtype=F32), rows)
    quarter = D_MODEL // 4
    omega = 1.0 / (10000.0 ** (jnp.arange(quarter, dtype=F32) / quarter))

    def axis_emb(pos):
        a = pos[:, None] * omega[None, :]
        return jnp.concatenate([jnp.sin(a), jnp.cos(a)], axis=-1)

    return jnp.concatenate([axis_emb(r), axis_emb(col)], axis=-1)


def lower_bounds(lb_param):
    p = jax.nn.softmax(lb_param.astype(F32), axis=0)
    return jnp.cumsum(p, axis=0)[:DEPTH]


def heads(t, d):
    B, L, _ = t.shape
    return t.reshape(B, L, HG_HEADS, d).transpose(0, 2, 1, 3)


def flip(t):
    return jnp.flip(t, axis=2)


def hgrn2_keys(z, lb):
    z = z.astype(F32)
    log_f = jnp.log(lb + (1.0 - lb) * jax.nn.sigmoid(z))
    k = (1.0 - lb) * jax.nn.sigmoid(-z)
    return heads(k, HG_DK), heads(log_f, HG_DK)


def gla_chunk_scan(q, k, v, log_f, s0):
    B, H, L, DK = q.shape
    DV = v.shape[-1]
    n = L // CHUNK

    def to_chunks(t):
        return t.reshape(B, H, n, CHUNK, t.shape[-1]).transpose(2, 0, 1, 3, 4)

    b = to_chunks(lax.cumsum(log_f.reshape(B, H, n, CHUNK, DK), axis=3).reshape(B, H, L, DK))
    mask = jnp.tril(jnp.ones((CHUNK, CHUNK), dtype=bool))
    mid = CHUNK // 2

    def step(S, xs):
        qc, kc, vc, bc = xs
        total = bc[:, :, -1:, :]
        ref = bc[:, :, mid:mid + 1, :]
        scores = jnp.einsum('bhtk,bhsk->bhts', qc * jnp.exp(bc - ref), kc * jnp.exp(ref - bc))
        scores = jnp.where(mask, scores, 0.0)
        o = (jnp.einsum('bhtk,bhkv->bhtv', qc * jnp.exp(bc), S)
             + jnp.einsum('bhts,bhsv->bhtv', scores, vc))
        S_new = (jnp.exp(total[:, :, 0, :])[..., None] * S
                 + jnp.einsum('bhsk,bhsv->bhkv', kc * jnp.exp(total - bc), vc))
        return S_new, o

    s_fin, o = lax.scan(step, s0, (to_chunks(q), to_chunks(k), to_chunks(v), b))
    return o.transpose(1, 2, 0, 3, 4).reshape(B, H, L, DV), s_fin


def gla_final_state(k, v, log_f):
    decay = jnp.exp(lax.cumsum(log_f, axis=2, reverse=True) - log_f)
    return jnp.einsum('bhlk,bhlv->bhkv', k * decay, v)


def hgrn2_bidir(q, v, fwd, bwd, s0_f, s0_b):
    o_f, s_f = gla_chunk_scan(q, fwd[0], v, fwd[1], s0_f)
    o_b, s_b = gla_chunk_scan(flip(q), flip(bwd[0]), flip(v), flip(bwd[1]), s0_b)
    return o_f + flip(o_b), s_f, s_b


def hgrn2_readout(o, g, norm_g, dtype):
    o = o * lax.rsqrt(jnp.mean(o * o, axis=-1, keepdims=True) + EPS) * norm_g.astype(F32)
    B, H, L, DV = o.shape
    o = o.transpose(0, 2, 1, 3).reshape(B, L, H * DV)
    return (o * jax.nn.silu(g.astype(F32))).astype(dtype)


def short_conv3(u, w, b):
    up = jnp.pad(u, ((0, 0), (1, 1), (0, 0)))
    return up[:, :-2] * w[0] + up[:, 1:-1] * w[1] + up[:, 2:] * w[2] + b


def hyena_filters(L, w1, b1, fr1, w2, b2, fr2, w3):
    t = jnp.linspace(0.0, 1.0, L, dtype=F32)[:, None]
    bands = (HY_EMB - 1) // 2
    ang = (2.0 * math.pi * jnp.arange(L, dtype=F32) / L)[:, None] * \
        jnp.linspace(1e-4, bands - 1, bands, dtype=F32)[None, :]
    z = jnp.concatenate([t, jnp.cos(ang), -jnp.sin(ang)], axis=-1)
    h = jnp.sin(fr1.astype(F32) * (z @ w1.astype(F32) + b1.astype(F32)))
    h = jnp.sin(fr2.astype(F32) * (h @ w2.astype(F32) + b2.astype(F32)))
    h = h @ w3.astype(F32)
    deltas = jnp.abs(jnp.linspace(math.log(HY_DECAY_TARGET) / HY_SLOW_PCT,
                                  math.log(HY_DECAY_TARGET) / HY_FAST_PCT, HY_WIDTH, dtype=F32))
    h = h.reshape(L, 2, HY_WIDTH) * jnp.exp(-t * deltas)[:, None, :]
    return h[:, 0], h[:, 1]


def long_conv_bidir(u, h_fwd, h_bwd, d_skip):
    B, L, C = u.shape
    kern = jnp.concatenate([h_fwd, jnp.zeros((1, C), F32), jnp.flip(h_bwd[1:], axis=0)], axis=0)
    y = jnp.fft.irfft(jnp.fft.rfft(u, n=2 * L, axis=1) * jnp.fft.rfft(kern, axis=0)[None],
                      n=2 * L, axis=1)[:, :L]
    return y + u * d_skip


def hyena_branch(p_hy, conv_w, conv_b, filt, d_skip):
    L = p_hy.shape[1]
    u = short_conv3(p_hy, conv_w, conv_b).astype(F32)
    v, x1, x0 = jnp.split(u, 3, axis=-1)
    h_fwd, h_bwd = hyena_filters(L, *filt)
    y = x0 * long_conv_bidir(v * x1, h_fwd, h_bwd, d_skip.astype(F32))
    return y.astype(p_hy.dtype)


def merge_branches(y_a, y_b, g_logits, w_pa, w_pb, w_o):
    g_a, g_b = jnp.split(g_logits, 2, axis=-1)
    mixed = jax.nn.sigmoid(g_a) * (y_a @ w_pa) + jax.nn.sigmoid(g_b) * (y_b @ w_pb)
    return mixed @ w_o


def token_mixer(p, lb, s0_f, s0_b, hg_norm_g, conv_w, conv_b, filt, d_skip, w_pa, w_pb, w_o):
    q, z_f, z_b, vi, g, p_hy, g_logits = jnp.split(p, SPLIT_POINTS, axis=-1)
    fwd = hgrn2_keys(z_f, lb[0])
    bwd = hgrn2_keys(z_b, lb[1])
    o, s_f, s_b = hgrn2_bidir(heads(q.astype(F32), HG_DK), heads(vi.astype(F32), HG_DV),
                              fwd, bwd, s0_f, s0_b)
    y_a = hgrn2_readout(o, g, hg_norm_g, p.dtype)
    y_b = hyena_branch(p_hy, conv_w, conv_b, filt, d_skip)
    return merge_branches(y_a, y_b, g_logits, w_pa, w_pb, w_o), s_f, s_b


def context_states(p_c, lb):
    z_f, z_b, vi = jnp.split(p_c, (HG_KDIM, 2 * HG_KDIM), axis=-1)
    kf, lf = hgrn2_keys(z_f, lb[0])
    kb, lb_ = hgrn2_keys(z_b, lb[1])
    v = heads(vi.astype(F32), HG_DV)
    return gla_final_state(kf, v, lf), gla_final_state(flip(kb), flip(v), flip(lb_))


def hier_moe(h, wg, bg, we, be, w_gate, w_up, w_down):
    B, L, D = h.shape
    t = h.reshape(B * L, D)
    p_group = jax.nn.softmax((t @ wg + bg).astype(F32), axis=-1)
    p_top_g, g_idx = lax.top_k(p_group, 1)
    e_logits = (jnp.einsum('td,gde->tge', t, we) + be).astype(F32)
    e_logits = jnp.take_along_axis(e_logits, g_idx[:, :, None], axis=1)[:, 0]
    p_top_e, e_idx = lax.top_k(jax.nn.softmax(e_logits, axis=-1), TOP_K)
    w_sel = p_top_g * p_top_e / jnp.sum(p_top_e, axis=-1, keepdims=True)
    combine = (jnp.einsum('tk,tke->te', w_sel, jax.nn.one_hot(e_idx, EXPERTS_PER_GROUP, dtype=F32))[:, None, :]
               * jax.nn.one_hot(g_idx[:, 0], N_GROUPS, dtype=F32)[:, :, None])
    out = jnp.zeros((B * L, D), F32)
    for gi in range(N_GROUPS):
        hid = (jax.nn.silu(jnp.einsum('td,edf->tef', t, w_gate[gi]))
               * jnp.einsum('td,edf->tef', t, w_up[gi])
               * combine[:, gi, :, None].astype(t.dtype))
        out = out + jnp.einsum('tef,efd->td', hid, w_down[gi])
    return out.reshape(B, L, D).astype(h.dtype)


def setup_inputs(seed: int = 0) -> dict:
    key = jax.random.key(seed)
    ks = jax.random.split(key, 32)

    def nrm(k, shape, scale):
        return scale * jax.random.normal(k, shape, F32)

    G, E, F = N_GROUPS, EXPERTS_PER_GROUP, D_EXPERT
    return {
        "x": nrm(ks[0], (BATCH, SEQ, D_MODEL), 1.0),
        "c": nrm(ks[1], (BATCH, D_MODEL), 1.0),
        "ctx": nrm(ks[2], (BATCH, CTX_LEN, D_MODEL), 1.0),
        "c_ctx": nrm(ks[3], (D_MODEL,), 1.0),
        "ada_w": nrm(ks[4], (DEPTH, D_MODEL, 6 * D_MODEL), 0.5 * D_MODEL ** -0.5),
        "ada_b": nrm(ks[5], (DEPTH, 6 * D_MODEL), 0.02),
        "norm1_g": 1.0 + nrm(ks[6], (DEPTH, D_MODEL), 0.05),
        "norm2_g": 1.0 + nrm(ks[7], (DEPTH, D_MODEL), 0.05),
        "w_in": nrm(ks[8], (DEPTH, D_MODEL, IN_WIDTH), D_MODEL ** -0.5),
        "hgrn_lb": nrm(ks[9], (DEPTH + 1, 2, HG_KDIM), 0.1),
        "hgrn_norm_g": 1.0 + nrm(ks[10], (DEPTH, HG_DV), 0.05),
        "hy_conv_w": nrm(ks[11], (DEPTH, HY_SHORT, 3 * HY_WIDTH), HY_SHORT ** -0.5),
        "hy_conv_b": nrm(ks[12], (DEPTH, 3 * HY_WIDTH), 0.02),
        "hy_filt_w1": nrm(ks[13], (DEPTH, HY_EMB, HY_FILTER_HIDDEN), HY_EMB ** -0.5),
        "hy_filt_b1": nrm(ks[14], (DEPTH, HY_FILTER_HIDDEN), 0.1),
        "hy_filt_freq1": 1.0 + nrm(ks[15], (DEPTH, HY_FILTER_HIDDEN), 0.1),
        "hy_filt_w2": nrm(ks[16], (DEPTH, HY_FILTER_HIDDEN, HY_FILTER_HIDDEN), HY_FILTER_HIDDEN ** -0.5),
        "hy_filt_b2": nrm(ks[17], (DEPTH, HY_FILTER_HIDDEN), 0.1),
        "hy_filt_freq2": 1.0 + nrm(ks[18], (DEPTH, HY_FILTER_HIDDEN), 0.1),
        "hy_filt_w3": nrm(ks[19], (DEPTH, HY_FILTER_HIDDEN, 2 * HY_WIDTH), HY_FILTER_SCALE * HY_FILTER_HIDDEN ** -0.5),
        "hy_d": nrm(ks[20], (DEPTH, HY_WIDTH), 0.1),
        "w_proj_a": nrm(ks[21], (DEPTH, HG_WIDTH, D_MODEL), HG_WIDTH ** -0.5),
        "w_proj_b": nrm(ks[22], (DEPTH, HY_WIDTH, D_MODEL), HY_WIDTH ** -0.5),
        "w_out": nrm(ks[23], (DEPTH, D_MODEL, D_MODEL), D_MODEL ** -0.5),
        "moe_router_g_w": nrm(ks[24], (DEPTH, D_MODEL, G), D_MODEL ** -0.5),
        "moe_router_g_b": nrm(ks[25], (DEPTH, G), 0.01),
        "moe_router_e_w": nrm(ks[26], (DEPTH, G, D_MODEL, E), D_MODEL ** -0.5),
        "moe_router_e_b": nrm(ks[27], (DEPTH, G, E), 0.01),
        "moe_w_gate": nrm(ks[28], (DEPTH, G, E, D_MODEL, F), D_MODEL ** -0.5),
        "moe_w_up": nrm(ks[29], (DEPTH, G, E, D_MODEL, F), D_MODEL ** -0.5),
        "moe_w_down": nrm(ks[30], (DEPTH, G, E, F, D_MODEL), F ** -0.5),
        "final_norm_g": 1.0 + nrm(ks[31], (D_MODEL,), 0.05),
    }


def reference(x, c, ctx, c_ctx, ada_w, ada_b, norm1_g, norm2_g, w_in, hgrn_lb, hgrn_norm_g,
              hy_conv_w, hy_conv_b, hy_filt_w1, hy_filt_b1, hy_filt_freq1, hy_filt_w2, hy_filt_b2,
              hy_filt_freq2, hy_filt_w3, hy_d, w_proj_a, w_proj_b, w_out, moe_router_g_w,
              moe_router_g_b, moe_router_e_w, moe_router_e_b, moe_w_gate, moe_w_up, moe_w_down,
              final_norm_g):
    B = x.shape[0]
    x = x + sincos_2d(x.shape[1]).astype(x.dtype)
    lbs = lower_bounds(hgrn_lb)
    zero_state = jnp.zeros((B, HG_HEADS, HG_DK, HG_DV), F32)
    for l in range(DEPTH):
        last = l == DEPTH - 1
        sh1, sc1, gt1, sh2, sc2, gt2 = adaln(c, ada_w[l], ada_b[l])
        csh1, csc1, cgt1, csh2, csc2, cgt2 = adaln(c_ctx, ada_w[l], ada_b[l])
        filt = (hy_filt_w1[l], hy_filt_b1[l], hy_filt_freq1[l], hy_filt_w2[l], hy_filt_b2[l],
                hy_filt_freq2[l], hy_filt_w3[l])
        mix_args = (hgrn_norm_g[l], hy_conv_w[l], hy_conv_b[l], filt, hy_d[l],
                    w_proj_a[l], w_proj_b[l], w_out[l])
        hc = modulate(rms_norm(ctx, norm1_g[l]), csh1, csc1)
        if last:
            s_f, s_b = context_states(hc @ w_in[l][:, HG_KDIM:3 * HG_KDIM + HG_WIDTH], lbs[l])
        else:
            ctx_mix, s_f, s_b = token_mixer(hc @ w_in[l], lbs[l], zero_state, zero_state, *mix_args)
        hx = modulate(rms_norm(x, norm1_g[l]), sh1, sc1)
        x_mix, _, _ = token_mixer(hx @ w_in[l], lbs[l], s_f, s_b, *mix_args)
        x = x + gt1 * x_mix
        moe_args = (moe_router_g_w[l], moe_router_g_b[l], moe_router_e_w[l], moe_router_e_b[l],
                    moe_w_gate[l], moe_w_up[l], moe_w_down[l])
        x = x + gt2 * hier_moe(modulate(rms_norm(x, norm2_g[l]), sh2, sc2), *moe_args)
        if not last:
            ctx = ctx + cgt1 * ctx_mix
            ctx = ctx + cgt2 * hier_moe(modulate(rms_norm(ctx, norm2_g[l]), csh2, csc2), *moe_args)
    return rms_norm(x, final_norm_g)
```

```python
import functools
import math

import numpy as np
import jax
import jax.numpy as jnp
from jax import lax
from jax.experimental import pallas as pl
from jax.experimental.pallas import tpu as pltpu

F32 = jnp.float32
BF16 = jnp.bfloat16
I32 = jnp.int32
HIGHEST = lax.Precision.HIGHEST

D = 1024
B = 2
L = 8192
T = B * L
CTX = 256
GRID_W = 64
EPS = 1e-6
H = 4
DK = 128
DV = 128
KD = H * DK
IN_W = 6144
HYW = 512
HY_EMB = 33
NGRP = 4
NEPG = 8
NEXP = NGRP * NEPG
DEXP = 512
HY_DECAY_TARGET = 1e-2
HY_FAST_PCT = 0.3
HY_SLOW_PCT = 1.5

V7X_LANES = 128
V7X_SUBLANES = 8
V7X_VMEM_BYTES = 64 * 1024 * 1024
VMEM_LIMIT = 48 * 1024 * 1024

FFT_N = 2 * L
FFT_P = 128
FFT_BB = 8

TM_IN = 1024
TN_IN = 1024
TH = 128
CB = 32
TM_HY = 1024
TM_MG = 256
TR = 512
TE = 256
NP_ROWS = 2 * T + NEXP * TE
NT_EXP = NP_ROWS // TE
TS = 512
TC = 256


def _cparams(n_axes, vmem=None):
    return pltpu.CompilerParams(dimension_semantics=("arbitrary",) * n_axes,
                                vmem_limit_bytes=vmem)


def _split3(x):
    hi = x.astype(BF16)
    r = x - hi.astype(F32)
    mid = r.astype(BF16)
    lo = (r - mid.astype(F32)).astype(BF16)
    return hi, mid, lo


def _dot01(m, x):
    hi, mid, lo = _split3(x)
    return (jnp.dot(m, hi, preferred_element_type=F32) + jnp.dot(m, mid, preferred_element_type=F32)
            + jnp.dot(m, lo, preferred_element_type=F32))


def _rms(x):
    return x * lax.rsqrt(jnp.mean(x * x, axis=-1, keepdims=True) + EPS)


def _lane_pick(x, lane, idx):
    return jnp.sum(jnp.where(lane == idx, x, 0.0), axis=-1, keepdims=True)


def _ada_kernel(c_ref, w_ref, b_ref, o_ref):
    c = c_ref[...]
    s = c * jax.nn.sigmoid(c)
    o_ref[...] = jnp.dot(s, w_ref[...], preferred_element_type=F32, precision=HIGHEST) + b_ref[...]


def _adaln(cvec, w, b):
    tn = 1536
    return pl.pallas_call(
        _ada_kernel,
        grid=(6 * D // tn,),
        in_specs=[pl.BlockSpec((8, D), lambda j: (0, 0)),
                  pl.BlockSpec((D, tn), lambda j: (0, j)),
                  pl.BlockSpec((1, tn), lambda j: (0, j))],
        out_specs=pl.BlockSpec((8, tn), lambda j: (0, j)),
        out_shape=jax.ShapeDtypeStruct((8, 6 * D), F32),
        compiler_params=_cparams(1, VMEM_LIMIT),
        name="adaln",
    )(cvec, w, b)


def _keys(z, lb):
    sig = jax.nn.sigmoid(z)
    logf = jnp.log(lb + (1.0 - lb) * sig)
    k = (1.0 - lb) * jax.nn.sigmoid(-z)
    return k, logf


def _ctx_kernel(ctx_ref, g_ref, sh_ref, sc_ref, w_ref, lb_ref, sf_ref, sb_ref):
    h = _rms(ctx_ref[...]) * g_ref[...]
    h = h * (1.0 + sc_ref[...]) + sh_ref[...]
    p = jnp.dot(h.astype(BF16), w_ref[...], preferred_element_type=F32)
    zf, zb, v = p[:, :KD], p[:, KD:2 * KD], p[:, 2 * KD:]
    kf, lf = _keys(zf, lb_ref[0:1, :])
    kb, lbk = _keys(zb, lb_ref[1:2, :])
    r = lax.broadcasted_iota(I32, (CTX, CTX), 0)
    c = lax.broadcasted_iota(I32, (CTX, CTX), 1)
    tril = jnp.where(c <= r, 1.0, 0.0).astype(BF16)
    cf = _dot01(tril, lf)
    cb = _dot01(tril, lbk)
    kfd = (kf * jnp.exp(cf[CTX - 1:CTX, :] - cf)).astype(BF16)
    kbd = (kb * jnp.exp(cb - lbk)).astype(BF16)
    vb = v.astype(BF16)
    tn = (((0,), (0,)), ((), ()))
    for hh in range(H):
        hs = slice(hh * DK, (hh + 1) * DK)
        sf_ref[hh] = lax.dot_general(vb[:, hs], kfd[:, hs], tn, preferred_element_type=F32)
        sb_ref[hh] = lax.dot_general(vb[:, hs], kbd[:, hs], tn, preferred_element_type=F32)


def _context_states(ctx, g1, csh1, csc1, w_ctx, lbs):
    st = jax.ShapeDtypeStruct((B, H, DV, DK), F32)
    return pl.pallas_call(
        _ctx_kernel,
        grid=(B,),
        in_specs=[pl.BlockSpec((None, CTX, D), lambda b: (b, 0, 0)),
                  pl.BlockSpec((1, D), lambda b: (0, 0)),
                  pl.BlockSpec((1, D), lambda b: (0, 0)),
                  pl.BlockSpec((1, D), lambda b: (0, 0)),
                  pl.BlockSpec((D, 3 * KD), lambda b: (0, 0)),
                  pl.BlockSpec((2, KD), lambda b: (0, 0))],
        out_specs=(pl.BlockSpec((None, H, DV, DK), lambda b: (b, 0, 0, 0)),
                   pl.BlockSpec((None, H, DV, DK), lambda b: (b, 0, 0, 0))),
        out_shape=(st, st),
        compiler_params=_cparams(1, VMEM_LIMIT),
        name="ctx_states",
    )(ctx, g1, csh1, csc1, w_ctx, lbs)


def _inproj_kernel(x_ref, pos_ref, g_ref, sh_ref, sc_ref, w_ref, o_ref, hx_ref):
    @pl.when(pl.program_id(1) == 0)
    def _():
        h = _rms(x_ref[...] + pos_ref[...]) * g_ref[...]
        hx_ref[...] = (h * (1.0 + sc_ref[...]) + sh_ref[...]).astype(BF16)

    o_ref[...] = jnp.dot(hx_ref[...], w_ref[...], preferred_element_type=F32)


def _in_projection(x2, pos, g1, sh1, sc1, w_bf):
    tiles_per_batch = L // TM_IN
    return pl.pallas_call(
        _inproj_kernel,
        grid=(T // TM_IN, IN_W // TN_IN),
        in_specs=[pl.BlockSpec((TM_IN, D), lambda i, j: (i, 0)),
                  pl.BlockSpec((TM_IN, D), lambda i, j: (i % tiles_per_batch, 0)),
                  pl.BlockSpec((1, D), lambda i, j: (0, 0)),
                  pl.BlockSpec((None, 1, D), lambda i, j: (i // tiles_per_batch, 0, 0)),
                  pl.BlockSpec((None, 1, D), lambda i, j: (i // tiles_per_batch, 0, 0)),
                  pl.BlockSpec((D, TN_IN), lambda i, j: (0, j))],
        out_specs=pl.BlockSpec((TM_IN, TN_IN), lambda i, j: (i, j)),
        out_shape=jax.ShapeDtypeStruct((T, IN_W), F32),
        scratch_shapes=[pltpu.VMEM((TM_IN, D), BF16)],
        compiler_params=_cparams(2, VMEM_LIMIT),
        name="in_proj",
    )(x2, pos, g1, sh1, sc1, w_bf)


def _hgrn_kernel(reverse, readout, *refs):
    if readout:
        q_ref, z_ref, v_ref, lb_ref, s0_ref, of_ref, g_ref, ng_ref, o_ref, st_ref = refs
    else:
        q_ref, z_ref, v_ref, lb_ref, s0_ref, o_ref, st_ref = refs

    @pl.when(pl.program_id(1) == 0)
    def _():
        st_ref[...] = s0_ref[...]

    q = q_ref[...]
    v = v_ref[...]
    k, logf = _keys(z_ref[...], lb_ref[...])
    r = lax.broadcasted_iota(I32, (TH, TH), 0)
    c = lax.broadcasted_iota(I32, (TH, TH), 1)
    cb_shift = CB.bit_length() - 1
    same = jnp.right_shift(r, cb_shift) == jnp.right_shift(c, cb_shift)
    tri = jnp.where(same & ((c >= r) if reverse else (c <= r)), 1.0, 0.0).astype(BF16)
    bl = _dot01(tri, logf)
    vb = v.astype(BF16)
    rr = lax.broadcasted_iota(I32, (CB, CB), 0)
    cc = lax.broadcasted_iota(I32, (CB, CB), 1)
    mask = (cc >= rr) if reverse else (cc <= rr)
    nt = (((1,), (1,)), ((), ()))
    tn = (((0,), (0,)), ((), ()))
    nblk = TH // CB
    order = range(nblk - 1, -1, -1) if reverse else range(nblk)
    e_row = 0 if reverse else CB - 1
    m_row = CB - 1 - CB // 2 if reverse else CB // 2
    for jb in order:
        rs = slice(jb * CB, (jb + 1) * CB)
        blj = bl[rs]
        tau = blj[e_row:e_row + 1]
        mid = blj[m_row:m_row + 1]
        qj = q[rs]
        kj = k[rs]
        qd0 = (qj * jnp.exp(blj - mid)).astype(BF16)
        kd0 = (kj * jnp.exp(mid - blj)).astype(BF16)
        qs = (qj * jnp.exp(blj)).astype(BF16)
        ke = (kj * jnp.exp(tau - blj)).astype(BF16)
        dec = jnp.exp(tau)
        for hh in range(H):
            hs = slice(hh * DK, (hh + 1) * DK)
            sc = lax.dot_general(qd0[:, hs], kd0[:, hs], nt, preferred_element_type=F32)
            sc = jnp.where(mask, sc, 0.0).astype(BF16)
            st = st_ref[hh]
            o_h = (lax.dot_general(qs[:, hs], st.astype(BF16), nt, preferred_element_type=F32)
                   + jnp.dot(sc, vb[rs, hs], preferred_element_type=F32))
            st_ref[hh] = st * dec[:, hs] + lax.dot_general(vb[rs, hs], ke[:, hs], tn,
                                                           preferred_element_type=F32)
            if readout:
                o_h = o_h + of_ref[rs, hs]
                o_h = _rms(o_h) * ng_ref[...]
                gh = g_ref[rs, hs]
                o_h = o_h * (gh * jax.nn.sigmoid(gh))
            o_ref[rs, hs] = o_h


def _hgrn_scan(p, lb_row, s0, reverse, z_col, o_f=None, norm_g=None):
    nch = L // TH
    if reverse:
        row = lambda b, c: b * nch + (nch - 1 - c)
    else:
        row = lambda b, c: b * nch + c
    col_spec = lambda j: pl.BlockSpec((TH, KD), lambda b, c: (row(b, c), j))
    in_specs = [col_spec(0), col_spec(z_col), col_spec(3),
                pl.BlockSpec((1, KD), lambda b, c: (0, 0)),
                pl.BlockSpec((None, H, DV, DK), lambda b, c: (b, 0, 0, 0))]
    args = [p, p, p, lb_row, s0]
    readout = o_f is not None
    if readout:
        in_specs += [pl.BlockSpec((TH, KD), lambda b, c: (row(b, c), 0)), col_spec(4),
                     pl.BlockSpec((1, DV), lambda b, c: (0, 0))]
        args += [o_f, p, norm_g]
    return pl.pallas_call(
        functools.partial(_hgrn_kernel, reverse, readout),
        grid=(B, nch),
        in_specs=in_specs,
        out_specs=pl.BlockSpec((TH, KD), lambda b, c: (row(b, c), 0)),
        out_shape=jax.ShapeDtypeStruct((T, KD), F32),
        scratch_shapes=[pltpu.VMEM((H, DV, DK), F32)],
        compiler_params=_cparams(2, VMEM_LIMIT),
        name="hgrn_bwd_readout" if readout else "hgrn_fwd",
    )(*args)


def _hy_pre_kernel(v_ref, x1_ref, x0_ref, vp_ref, x1p_ref, x0p_ref, vn_ref, x1n_ref, x0n_ref,
                   w_ref, b_ref, vx_ref, x0o_ref):
    i = pl.program_id(1)
    first = i == 0
    last = i == pl.num_programs(1) - 1
    row = lax.broadcasted_iota(I32, (TM_HY, 1), 0)

    def conv(c_ref, p_ref, n_ref, col):
        x = c_ref[...]
        prev_row = jnp.where(first, 0.0, p_ref[V7X_SUBLANES - 1:V7X_SUBLANES, :])
        next_row = jnp.where(last, 0.0, n_ref[0:1, :])
        xm = jnp.where(row == 0, prev_row, pltpu.roll(x, 1, axis=0))
        xp = jnp.where(row == TM_HY - 1, next_row, pltpu.roll(x, TM_HY - 1, axis=0))
        cs = slice(col * HYW, (col + 1) * HYW)
        return xm * w_ref[0:1, cs] + x * w_ref[1:2, cs] + xp * w_ref[2:3, cs] + b_ref[:, cs]

    v = conv(v_ref, vp_ref, vn_ref, 0)
    x1 = conv(x1_ref, x1p_ref, x1n_ref, 1)
    x0 = conv(x0_ref, x0p_ref, x0n_ref, 2)
    vx_ref[...] = v * x1
    x0o_ref[...] = x0


def _hyena_pre(p, conv_w, conv_b):
    nt = L // TM_HY
    hb = TM_HY // V7X_SUBLANES
    nhb = T // V7X_SUBLANES
    cur = lambda col: pl.BlockSpec((TM_HY, HYW), lambda b, i: (b * nt + i, col))
    prv = lambda col: pl.BlockSpec((V7X_SUBLANES, HYW),
                                   lambda b, i: (jnp.maximum((b * nt + i) * hb - 1, 0), col))
    nxt = lambda col: pl.BlockSpec((V7X_SUBLANES, HYW),
                                   lambda b, i: (jnp.minimum((b * nt + i + 1) * hb, nhb - 1), col))
    c0 = 5
    out = jax.ShapeDtypeStruct((T, HYW), F32)
    return pl.pallas_call(
        _hy_pre_kernel,
        grid=(B, nt),
        in_specs=[cur(c0), cur(c0 + 1), cur(c0 + 2), prv(c0), prv(c0 + 1), prv(c0 + 2),
                  nxt(c0), nxt(c0 + 1), nxt(c0 + 2),
                  pl.BlockSpec((3, 3 * HYW), lambda b, i: (0, 0)),
                  pl.BlockSpec((1, 3 * HYW), lambda b, i: (0, 0))],
        out_specs=(pl.BlockSpec((TM_HY, HYW), lambda b, i: (b * nt + i, 0)),
                   pl.BlockSpec((TM_HY, HYW), lambda b, i: (b * nt + i, 0))),
        out_shape=(out, out),
        compiler_params=_cparams(2, VMEM_LIMIT),
        name="hyena_pre",
    )(p, p, p, p, p, p, p, p, p, conv_w, conv_b)


def _filt_kernel(z_ref, w1_ref, b1_ref, f1_ref, w2_ref, b2_ref, f2_ref, w3_ref, dl_ref, o_ref):
    z = z_ref[...]
    lane = lax.broadcasted_iota(I32, z.shape, 1)
    t = _lane_pick(z, lane, 0)
    is_f = _lane_pick(z, lane, HY_EMB)
    is_b = _lane_pick(z, lane, HY_EMB + 1)
    dot = functools.partial(jnp.dot, preferred_element_type=F32, precision=HIGHEST)
    h = jnp.sin(f1_ref[...] * (dot(z, w1_ref[...]) + b1_ref[...]))
    h = jnp.sin(f2_ref[...] * (dot(h, w2_ref[...]) + b2_ref[...]))
    h = dot(h, w3_ref[...])
    win = jnp.exp(-t * dl_ref[...])
    o_ref[...] = (is_f * h[:, :HYW] + is_b * h[:, HYW:]) * win


def _filter_taps(zin, w1, b1, f1, w2, b2, f2, w3, deltas):
    tm = 2048
    full = lambda shape: pl.BlockSpec(shape, lambda i: (0, 0))
    return pl.pallas_call(
        _filt_kernel,
        grid=(FFT_N // tm,),
        in_specs=[pl.BlockSpec((tm, V7X_LANES), lambda i: (i, 0)),
                  full((V7X_LANES, V7X_LANES)), full((1, V7X_LANES)), full((1, V7X_LANES)),
                  full((V7X_LANES, V7X_LANES)), full((1, V7X_LANES)), full((1, V7X_LANES)),
                  full((V7X_LANES, 2 * HYW)), full((1, HYW))],
        out_specs=pl.BlockSpec((tm, HYW), lambda i: (i, 0)),
        out_shape=jax.ShapeDtypeStruct((FFT_N, HYW), F32),
        compiler_params=_cparams(1, VMEM_LIMIT),
        name="hyena_filter",
    )(zin, w1, b1, f1, w2, b2, f2, w3, deltas)


def _strided_dft_kernel(in_split, out_split, x_hbm, f_ref, o_hbm, xbuf, obuf, sem_in, sem_out):
    n = pl.program_id(0)
    j = pl.program_id(1)

    def src(jj):
        b = j * FFT_BB + jj
        return x_hbm.at[n, :, :, b, :] if in_split else x_hbm.at[n, :, b, :]

    def dst(jj):
        b = j * FFT_BB + jj
        return o_hbm.at[n, :, :, b, :] if out_split else o_hbm.at[n, :, b, :]

    for jj in range(FFT_BB):
        pltpu.make_async_copy(src(jj), xbuf.at[jj], sem_in).start()
    for jj in range(FFT_BB):
        pltpu.make_async_copy(src(jj), xbuf.at[jj], sem_in).wait()
    for jj in range(FFT_BB):
        x = xbuf[jj]
        if in_split:
            x = x.reshape(2 * FFT_P, HYW)
        r = jnp.dot(f_ref[...], x.astype(BF16), preferred_element_type=F32)
        if out_split:
            r = r.reshape(2, FFT_P, HYW)
        obuf[jj] = r
    for jj in range(FFT_BB):
        pltpu.make_async_copy(obuf.at[jj], dst(jj), sem_out).start()
    for jj in range(FFT_BB):
        pltpu.make_async_copy(obuf.at[jj], dst(jj), sem_out).wait()


def _dft_first(xv, fmat):
    n, kk = xv.shape[0], xv.shape[1]
    return pl.pallas_call(
        functools.partial(_strided_dft_kernel, False, True),
        grid=(n, FFT_P // FFT_BB),
        in_specs=[pl.BlockSpec(memory_space=pl.ANY),
                  pl.BlockSpec((2 * FFT_P, kk), lambda b, j: (0, 0))],
        out_specs=pl.BlockSpec(memory_space=pl.ANY),
        out_shape=jax.ShapeDtypeStruct((n, 2, FFT_P, FFT_P, HYW), F32),
        scratch_shapes=[pltpu.VMEM((FFT_BB, kk, HYW), F32), pltpu.VMEM((FFT_BB, 2, FFT_P, HYW), F32),
                        pltpu.SemaphoreType.DMA(()), pltpu.SemaphoreType.DMA(())],
        compiler_params=_cparams(2, VMEM_LIMIT),
        name="dft_first",
    )(xv, fmat)


def _dft_last(bp, gmat):
    n = bp.shape[0]
    na = L // FFT_P
    return pl.pallas_call(
        functools.partial(_strided_dft_kernel, True, False),
        grid=(n, FFT_P // FFT_BB),
        in_specs=[pl.BlockSpec(memory_space=pl.ANY),
                  pl.BlockSpec((na, 2 * FFT_P), lambda b, j: (0, 0))],
        out_specs=pl.BlockSpec(memory_space=pl.ANY),
        out_shape=jax.ShapeDtypeStruct((n, na, FFT_P, HYW), F32),
        scratch_shapes=[pltpu.VMEM((FFT_BB, 2, FFT_P, HYW), F32), pltpu.VMEM((FFT_BB, na, HYW), F32),
                        pltpu.SemaphoreType.DMA(()), pltpu.SemaphoreType.DMA(())],
        compiler_params=_cparams(2, VMEM_LIMIT),
        name="dft_last",
    )(bp, gmat)


def _cblock(mr, mi):
    return jnp.concatenate([jnp.concatenate([mr, -mi], axis=1), jnp.concatenate([mi, mr], axis=1)], axis=0)


def _kspec_kernel(a_ref, mr_ref, mi_ref, o_ref):
    rm = _cblock(mr_ref[...], mi_ref[...])
    a = a_ref[...].reshape(2 * FFT_P, HYW).astype(BF16)
    o_ref[...] = jnp.dot(rm, a, preferred_element_type=F32).reshape(2, FFT_P, HYW)


def _kernel_spectrum(ak, mr, mi):
    return pl.pallas_call(
        _kspec_kernel,
        grid=(FFT_P,),
        in_specs=[pl.BlockSpec((None, 2, None, FFT_P, HYW), lambda d: (0, 0, d, 0, 0)),
                  pl.BlockSpec((None, FFT_P, FFT_P), lambda d: (d, 0, 0)),
                  pl.BlockSpec((None, FFT_P, FFT_P), lambda d: (d, 0, 0))],
        out_specs=pl.BlockSpec((2, None, FFT_P, HYW), lambda d: (0, d, 0, 0)),
        out_shape=jax.ShapeDtypeStruct((2, FFT_P, FFT_P, HYW), F32),
        compiler_params=_cparams(1, VMEM_LIMIT),
        name="kernel_spectrum",
    )(ak, mr, mi)


def _mid_kernel(a_ref, kh_ref, mr_ref, mi_ref, mrt_ref, mit_ref, o_ref):
    rm = _cblock(mr_ref[...], mi_ref[...])
    rmt = _cblock(mrt_ref[...], -mit_ref[...])
    kr = kh_ref[0]
    ki = kh_ref[1]
    for n in range(B):
        a = a_ref[n].reshape(2 * FFT_P, HYW).astype(BF16)
        x = jnp.dot(rm, a, preferred_element_type=F32)
        xr, xi = x[:FFT_P], x[FFT_P:]
        y = jnp.concatenate([xr * kr - xi * ki, xr * ki + xi * kr], axis=0).astype(BF16)
        o_ref[n] = jnp.dot(rmt, y, preferred_element_type=F32).reshape(2, FFT_P, HYW)


def _fft_mid(au, kh, mr, mi, mrt, mit):
    msp = pl.BlockSpec((None, FFT_P, FFT_P), lambda d: (d, 0, 0))
    return pl.pallas_call(
        _mid_kernel,
        grid=(FFT_P,),
        in_specs=[pl.BlockSpec((B, 2, None, FFT_P, HYW), lambda d: (0, 0, d, 0, 0)),
                  pl.BlockSpec((2, None, FFT_P, HYW), lambda d: (0, d, 0, 0)),
                  msp, msp, msp, msp],
        out_specs=pl.BlockSpec((B, 2, None, FFT_P, HYW), lambda d: (0, 0, d, 0, 0)),
        out_shape=jax.ShapeDtypeStruct((B, 2, FFT_P, FFT_P, HYW), F32),
        compiler_params=_cparams(1, VMEM_LIMIT),
        name="fft_mid",
    )(au, kh, mr, mi, mrt, mit)


def _dft_tables():
    a = np.arange(FFT_P)
    ang1 = 2.0 * np.pi * np.outer(a, a) / FFT_P
    f_full = np.concatenate([np.cos(ang1), -np.sin(ang1)], axis=0)
    na = L // FFT_P
    g_last = np.concatenate([np.cos(ang1[:na]), -np.sin(ang1[:na])], axis=1)
    tb = lambda x: jnp.asarray(x.astype(np.float32)).astype(BF16)
    shp = (FFT_P, FFT_P, FFT_P)
    d = lax.broadcasted_iota(I32, shp, 0)
    c = lax.broadcasted_iota(I32, shp, 1)
    b = lax.broadcasted_iota(I32, shp, 2)
    m = (FFT_P * b * c + b * d) % FFT_N
    ang2 = m.astype(F32) * (2.0 * math.pi / FFT_N)
    mr = jnp.cos(ang2)
    mi = -jnp.sin(ang2)
    scale = 1.0 / FFT_N
    mrt = jnp.transpose(mr, (0, 2, 1)) * scale
    mit = jnp.transpose(mi, (0, 2, 1)) * scale
    return tb(f_full), tb(g_last), mr.astype(BF16), mi.astype(BF16), mrt.astype(BF16), mit.astype(BF16)


def _merge_kernel(ya_ref, cv_ref, vx_ref, x0_ref, ga_ref, gb_ref, x_ref, pos_ref, gt1_ref, hyd_ref,
                  wpa_ref, wpb_ref, wo_ref, g2_ref, sh2_ref, sc2_ref, wr_ref, br_ref,
                  x1_ref, t2_ref, lg_ref):
    vx = vx_ref[...]
    yb = x0_ref[...] * (cv_ref[...] + vx * hyd_ref[...])
    pa = jnp.dot(ya_ref[...].astype(BF16), wpa_ref[...], preferred_element_type=F32)
    pb = jnp.dot(yb.astype(BF16), wpb_ref[...], preferred_element_type=F32)
    mixed = jax.nn.sigmoid(ga_ref[...]) * pa + jax.nn.sigmoid(gb_ref[...]) * pb
    xm = jnp.dot(mixed.astype(BF16), wo_ref[...], preferred_element_type=F32)
    x1 = x_ref[...] + pos_ref[...] + gt1_ref[...] * xm
    x1_ref[...] = x1
    t2 = _rms(x1) * g2_ref[...]
    t2 = t2 * (1.0 + sc2_ref[...]) + sh2_ref[...]
    t2_ref[...] = t2
    lg_ref[...] = jnp.dot(t2, wr_ref[...], preferred_element_type=F32, precision=HIGHEST) + br_ref[...]


def _merge(ya, cv, vx, x0c, p, x2, pos, gt1, hyd, wpa, wpb, wo, g2, sh2, sc2, wr, br):
    tpb = L // TM_MG
    half = lambda: pl.BlockSpec((TM_MG, HYW), lambda i: (i, 0))
    full = lambda shape: pl.BlockSpec(shape, lambda i: tuple(0 for _ in shape))
    perb = lambda: pl.BlockSpec((None, 1, D), lambda i: (i // tpb, 0, 0))
    return pl.pallas_call(
        _merge_kernel,
        grid=(T // TM_MG,),
        in_specs=[half(), half(), half(), half(),
                  pl.BlockSpec((TM_MG, D), lambda i: (i, 4)),
                  pl.BlockSpec((TM_MG, D), lambda i: (i, 5)),
                  pl.BlockSpec((TM_MG, D), lambda i: (i, 0)),
                  pl.BlockSpec((TM_MG, D), lambda i: (i % tpb, 0)),
                  perb(), full((1, HYW)),
                  full((KD, D)), full((HYW, D)), full((D, D)),
                  full((1, D)), perb(), perb(),
                  full((D, V7X_LANES)), full((1, V7X_LANES))],
        out_specs=(pl.BlockSpec((TM_MG, D), lambda i: (i, 0)),
                   pl.BlockSpec((TM_MG, D), lambda i: (i, 0)),
                   pl.BlockSpec((TM_MG, V7X_LANES), lambda i: (i, 0))),
        out_shape=(jax.ShapeDtypeStruct((T, D), F32), jax.ShapeDtypeStruct((T, D), F32),
                   jax.ShapeDtypeStruct((T, V7X_LANES), F32)),
        compiler_params=_cparams(1, VMEM_LIMIT),
        name="merge",
    )(ya, cv, vx, x0c, p, p, x2, pos, gt1, hyd, wpa, wpb, wo, g2, sh2, sc2, wr, br)


def _route_kernel(lg_ref, info_ref, cnt_ref):
    @pl.when(pl.program_id(0) == 0)
    def _():
        cnt_ref[...] = jnp.zeros_like(cnt_ref)

    lg = lg_ref[...]
    lane = lax.broadcasted_iota(I32, lg.shape, 1)
    lanef = lane.astype(F32)
    neg = -1e30
    big = 1e9
    is_g = (lane >= NEXP) & (lane < NEXP + NGRP)
    gl = jnp.where(is_g, lg, neg)
    ge = jnp.where(is_g, jnp.exp(gl - jnp.max(gl, axis=-1, keepdims=True)), 0.0)
    pg = ge / jnp.sum(ge, axis=-1, keepdims=True)
    p_top_g = jnp.max(pg, axis=-1, keepdims=True)
    gidx = jnp.min(jnp.where(is_g & (pg == p_top_g), lanef, big), axis=-1, keepdims=True)
    g_sel = gidx.astype(I32) - NEXP
    emask = (lane < NEXP) & (jnp.right_shift(lane, NEPG.bit_length() - 1) == g_sel)
    el = jnp.where(emask, lg, neg)
    ee = jnp.where(emask, jnp.exp(el - jnp.max(el, axis=-1, keepdims=True)), 0.0)
    pe = ee / jnp.sum(ee, axis=-1, keepdims=True)
    p1 = jnp.max(jnp.where(emask, pe, -1.0), axis=-1, keepdims=True)
    i1 = jnp.min(jnp.where(emask & (pe == p1), lanef, big), axis=-1, keepdims=True)
    rest = emask & (lanef != i1)
    p2 = jnp.max(jnp.where(rest, pe, -1.0), axis=-1, keepdims=True)
    i2 = jnp.min(jnp.where(rest & (pe == p2), lanef, big), axis=-1, keepdims=True)
    wsum = p1 + p2
    w1 = p_top_g * p1 / wsum
    w2 = p_top_g * p2 / wsum
    sel1 = lanef == i1
    sel2 = lanef == i2
    oh = jnp.where(sel1 | sel2, 1.0, 0.0)
    r = lax.broadcasted_iota(I32, (TR, TR), 0)
    c = lax.broadcasted_iota(I32, (TR, TR), 1)
    stril = jnp.where(c < r, 1.0, 0.0).astype(BF16)
    before = jnp.dot(stril, oh.astype(BF16), preferred_element_type=F32) + cnt_ref[...]
    r1 = jnp.sum(jnp.where(sel1, before, 0.0), axis=-1, keepdims=True)
    r2 = jnp.sum(jnp.where(sel2, before, 0.0), axis=-1, keepdims=True)
    cnt_ref[...] += jnp.sum(oh, axis=0, keepdims=True)
    info = jnp.where(lane == 0, i1, jnp.where(lane == 1, i2, jnp.where(lane == 2, r1, jnp.where(
        lane == 3, r2, jnp.where(lane == 4, w1, jnp.where(lane == 5, w2, 0.0))))))
    info_ref[...] = info


def _route(lg):
    return pl.pallas_call(
        _route_kernel,
        grid=(T // TR,),
        in_specs=[pl.BlockSpec((TR, V7X_LANES), lambda i: (i, 0))],
        out_specs=(pl.BlockSpec((TR, V7X_LANES), lambda i: (i, 0)),
                   pl.BlockSpec((1, V7X_LANES), lambda i: (0, 0))),
        out_shape=(jax.ShapeDtypeStruct((T, V7X_LANES), F32), jax.ShapeDtypeStruct((1, V7X_LANES), F32)),
        compiler_params=_cparams(1, VMEM_LIMIT),
        name="route",
    )(lg)


def _scatter_kernel(pos_ref, zt_ref, t2_hbm, zeros_hbm, xs_hbm, sem, zsem):
    i = pl.program_id(0)

    def zcopy(row):
        start = pl.multiple_of(jnp.maximum(row, 0), TE)
        return pltpu.make_async_copy(zeros_hbm, xs_hbm.at[pl.ds(start, TE)], zsem)

    @pl.when(i == 0)
    def _():
        def ztail(start, e, carry):
            @pl.when(zt_ref[0, e] >= 0)
            def _():
                cp = zcopy(zt_ref[0, e])
                cp.start() if start else cp.wait()
            return carry

        lax.fori_loop(0, NEXP, functools.partial(ztail, True), 0)
        lax.fori_loop(0, NEXP, functools.partial(ztail, False), 0)

        def zrest(start, tile, carry):
            cp = zcopy(tile * TE)
            cp.start() if start else cp.wait()
            return carry

        lax.fori_loop(zt_ref[0, NEXP], NT_EXP, functools.partial(zrest, True), 0)
        lax.fori_loop(zt_ref[0, NEXP], NT_EXP, functools.partial(zrest, False), 0)

    def row_copy(j, kk):
        return pltpu.make_async_copy(t2_hbm.at[pl.ds(i * TS + j, 1)],
                                     xs_hbm.at[pl.ds(pos_ref[0, 2 * j + kk], 1)], sem)

    def issue(j, carry):
        row_copy(j, 0).start()
        row_copy(j, 1).start()
        return carry

    lax.fori_loop(0, TS, issue, 0)

    def drain(j, carry):
        row_copy(j, 0).wait()
        row_copy(j, 1).wait()
        return carry

    lax.fori_loop(0, TS, drain, 0)


def _scatter_rows(pos3, ztile, t2, zeros_tile):
    return pl.pallas_call(
        _scatter_kernel,
        grid=(T // TS,),
        in_specs=[pl.BlockSpec((None, 1, 2 * TS), lambda i: (i, 0, 0), memory_space=pltpu.SMEM),
                  pl.BlockSpec(memory_space=pltpu.SMEM),
                  pl.BlockSpec(memory_space=pl.ANY),
                  pl.BlockSpec(memory_space=pl.ANY)],
        out_specs=pl.BlockSpec(memory_space=pl.ANY),
        out_shape=jax.ShapeDtypeStruct((NP_ROWS, D), F32),
        scratch_shapes=[pltpu.SemaphoreType.DMA(()), pltpu.SemaphoreType.DMA(())],
        compiler_params=_cparams(1),
        name="scatter_rows",
    )(pos3, ztile, t2, zeros_tile)


def _expert_kernel(te_ref, na_ref, xs_ref, wg_ref, wu_ref, wd_ref, ys_ref):
    @pl.when(pl.program_id(0) < na_ref[0])
    def _():
        x = xs_ref[...].astype(BF16)
        g = jnp.dot(x, wg_ref[...].astype(BF16), preferred_element_type=F32)
        u = jnp.dot(x, wu_ref[...].astype(BF16), preferred_element_type=F32)
        hid = (g * jax.nn.sigmoid(g) * u).astype(BF16)
        ys_ref[...] = jnp.dot(hid, wd_ref[...].astype(BF16), preferred_element_type=F32)

    @pl.when(pl.program_id(0) >= na_ref[0])
    def _():
        ys_ref[...] = jnp.zeros_like(ys_ref)


def _experts(tile_expert, n_active, xs, wg, wu, wd):
    rows = lambda i, te, na: (i, 0)
    wsel = lambda i, te, na: (te[i], 0, 0)
    grid_spec = pltpu.PrefetchScalarGridSpec(
        num_scalar_prefetch=2,
        grid=(NT_EXP,),
        in_specs=[pl.BlockSpec((TE, D), rows),
                  pl.BlockSpec((None, D, DEXP), wsel),
                  pl.BlockSpec((None, D, DEXP), wsel),
                  pl.BlockSpec((None, DEXP, D), wsel)],
        out_specs=pl.BlockSpec((TE, D), rows),
    )
    return pl.pallas_call(
        _expert_kernel,
        grid_spec=grid_spec,
        out_shape=jax.ShapeDtypeStruct((NP_ROWS, D), F32),
        compiler_params=_cparams(1, VMEM_LIMIT),
        name="experts",
    )(tile_expert, n_active, xs, wg, wu, wd)


def _combine_kernel(pos_ref, info_ref, x1_ref, gt2_ref, fg_ref, ys_hbm, o_ref, buf, sem):
    def row_copy(j, kk):
        return pltpu.make_async_copy(ys_hbm.at[pl.ds(pos_ref[0, 2 * j + kk], 1)],
                                     buf.at[kk, pl.ds(j, 1)], sem)

    def issue(j, carry):
        row_copy(j, 0).start()
        row_copy(j, 1).start()
        return carry

    lax.fori_loop(0, TC, issue, 0)

    def drain(j, carry):
        row_copy(j, 0).wait()
        row_copy(j, 1).wait()
        return carry

    lax.fori_loop(0, TC, drain, 0)

    info = info_ref[...]
    lane = lax.broadcasted_iota(I32, info.shape, 1)
    w1 = _lane_pick(info, lane, 4)
    w2 = _lane_pick(info, lane, 5)
    moe = w1 * buf[0] + w2 * buf[1]
    x2 = x1_ref[...] + gt2_ref[...] * moe
    o_ref[...] = _rms(x2) * fg_ref[...]


def _combine(pos3, info, x1, gt2, fg, ys):
    tpb = L // TC
    return pl.pallas_call(
        _combine_kernel,
        grid=(T // TC,),
        in_specs=[pl.BlockSpec((None, 1, 2 * TC), lambda i: (i, 0, 0), memory_space=pltpu.SMEM),
                  pl.BlockSpec((TC, V7X_LANES), lambda i: (i, 0)),
                  pl.BlockSpec((TC, D), lambda i: (i, 0)),
                  pl.BlockSpec((None, 1, D), lambda i: (i // tpb, 0, 0)),
                  pl.BlockSpec((1, D), lambda i: (0, 0)),
                  pl.BlockSpec(memory_space=pl.ANY)],
        out_specs=pl.BlockSpec((TC, D), lambda i: (i, 0)),
        out_shape=jax.ShapeDtypeStruct((T, D), F32),
        scratch_shapes=[pltpu.VMEM((2, TC, D), F32), pltpu.SemaphoreType.DMA(())],
        compiler_params=_cparams(1, VMEM_LIMIT),
        name="combine",
    )(pos3, info, x1, gt2, fg, ys)


def _pos_emb():
    rows = L // GRID_W
    quarter = D // 4
    omega = 1.0 / (10000.0 ** (jnp.arange(quarter, dtype=F32) / quarter))

    def axis_emb(pos):
        a = pos[:, None] * omega[None, :]
        return jnp.concatenate([jnp.sin(a), jnp.cos(a)], axis=-1)

    er = axis_emb(jnp.arange(rows, dtype=F32))
    ec = axis_emb(jnp.arange(GRID_W, dtype=F32))
    return jnp.concatenate([jnp.repeat(er, GRID_W, axis=0), jnp.tile(ec, (rows, 1))], axis=-1)


def _filter_features():
    idx = np.concatenate([np.arange(L), L - np.arange(L)])
    valid_b = (np.arange(L) >= 1).astype(np.float32)
    idx = np.minimum(idx, L - 1)
    t = jnp.linspace(0.0, 1.0, L, dtype=F32)[:, None]
    bands = (HY_EMB - 1) // 2
    ang = (2.0 * math.pi * jnp.arange(L, dtype=F32) / L)[:, None] * \
        jnp.linspace(1e-4, bands - 1, bands, dtype=F32)[None, :]
    z = jnp.concatenate([t, jnp.cos(ang), -jnp.sin(ang)], axis=-1)[idx]
    is_f = jnp.concatenate([jnp.ones((L,), F32), jnp.zeros((L,), F32)])[:, None]
    is_b = jnp.concatenate([jnp.zeros((L,), F32), jnp.asarray(valid_b)])[:, None]
    pad = jnp.zeros((FFT_N, V7X_LANES - HY_EMB - 2), F32)
    return jnp.concatenate([z, is_f, is_b, pad], axis=-1)


def _pad2(a, rows, cols):
    return jnp.pad(a, ((0, rows - a.shape[0]), (0, cols - a.shape[1])))


def kernel(x, c, ctx, c_ctx, ada_w, ada_b, norm1_g, norm2_g, w_in, hgrn_lb, hgrn_norm_g, hy_conv_w, hy_conv_b, hy_filt_w1, hy_filt_b1, hy_filt_freq1, hy_filt_w2, hy_filt_b2, hy_filt_freq2, hy_filt_w3, hy_d, w_proj_a, w_proj_b, w_out, moe_router_g_w, moe_router_g_b, moe_router_e_w, moe_router_e_b, moe_w_gate, moe_w_up, moe_w_down, final_norm_g):
    cvec = jnp.zeros((8, D), F32).at[0:B].set(c).at[B].set(c_ctx)
    mod = _adaln(cvec, ada_w[0], ada_b[0][None, :])
    m6 = mod.reshape(8, 6, D)
    sh1, sc1, gt1, sh2, sc2, gt2 = [m6[0:B, k][:, None, :] for k in range(6)]
    csh1, csc1 = m6[B:B + 1, 0], m6[B:B + 1, 1]

    lbs = jnp.cumsum(jax.nn.softmax(hgrn_lb.astype(F32), axis=0), axis=0)[0]
    g1 = norm1_g[0][None, :]
    pos = _pos_emb()
    x2 = x.reshape(T, D)

    w_ctx = w_in[0][:, KD:4 * KD].astype(BF16)
    s_f, s_b = _context_states(ctx, g1, csh1, csc1, w_ctx, lbs)

    p = _in_projection(x2, pos, g1, sh1, sc1, w_in[0].astype(BF16))

    o_f = _hgrn_scan(p, lbs[0:1], s_f, False, 1)
    y_a = _hgrn_scan(p, lbs[1:2], s_b, True, 2, o_f=o_f, norm_g=hgrn_norm_g[0][None, :])

    vx, x0c = _hyena_pre(p, hy_conv_w[0], hy_conv_b[0][None, :])
    deltas = jnp.abs(jnp.linspace(math.log(HY_DECAY_TARGET) / HY_SLOW_PCT,
                                  math.log(HY_DECAY_TARGET) / HY_FAST_PCT, HYW, dtype=F32))[None, :]
    ln = V7X_LANES
    kern = _filter_taps(
        _filter_features(),
        _pad2(hy_filt_w1[0], ln, ln), _pad2(hy_filt_b1[0][None, :], 1, ln), _pad2(hy_filt_freq1[0][None, :], 1, ln),
        _pad2(hy_filt_w2[0], ln, ln), _pad2(hy_filt_b2[0][None, :], 1, ln), _pad2(hy_filt_freq2[0][None, :], 1, ln),
        _pad2(hy_filt_w3[0], ln, 2 * HYW), deltas)
    f_full, g_last, mr, mi, mrt, mit = _dft_tables()
    ak = _dft_first(kern.reshape(1, FFT_P, FFT_P, HYW), f_full)
    kh = _kernel_spectrum(ak, mr, mi)
    au = _dft_first(vx.reshape(B, L // FFT_P, FFT_P, HYW), f_full[:, :L // FFT_P])
    bp = _fft_mid(au, kh, mr, mi, mrt, mit)
    conv = _dft_last(bp, g_last).reshape(T, HYW)

    wr = jnp.concatenate([jnp.transpose(moe_router_e_w[0], (1, 0, 2)).reshape(D, NEXP),
                          moe_router_g_w[0], jnp.zeros((D, V7X_LANES - NEXP - NGRP), F32)], axis=1)
    br = jnp.concatenate([moe_router_e_b[0].reshape(NEXP), moe_router_g_b[0],
                          jnp.zeros((V7X_LANES - NEXP - NGRP,), F32)])[None, :]
    x1, t2, lg = _merge(y_a, conv, vx, x0c, p, x2, pos, gt1, hy_d[0][None, :],
                        w_proj_a[0].astype(BF16), w_proj_b[0].astype(BF16), w_out[0].astype(BF16),
                        norm2_g[0][None, :], sh2, sc2, wr, br)

    info, counts = _route(lg)
    cnt = counts[0, :NEXP].astype(I32)
    pc = ((cnt + TE - 1) // TE) * TE
    ends = jnp.cumsum(pc)
    starts = ends - pc
    n_active = (ends[-1] // TE).astype(I32)[None]
    tile_expert = jnp.minimum(jnp.searchsorted(ends, jnp.arange(NT_EXP, dtype=I32) * TE, side="right"),
                              NEXP - 1).astype(I32)
    e1 = info[:, 0].astype(I32)
    e2 = info[:, 1].astype(I32)
    pos1 = starts[e1] + info[:, 2].astype(I32)
    pos2 = starts[e2] + info[:, 3].astype(I32)
    posf = jnp.stack([pos1, pos2], axis=-1)
    ztile = jnp.concatenate([jnp.where(pc > 0, ends - TE, -1).astype(I32), n_active])[None, :]

    xs = _scatter_rows(posf.reshape(T // TS, 1, 2 * TS), ztile, t2, jnp.zeros((TE, D), F32))
    ys = _experts(tile_expert, n_active, xs,
                  moe_w_gate[0].reshape(NEXP, D, DEXP), moe_w_up[0].reshape(NEXP, D, DEXP),
                  moe_w_down[0].reshape(NEXP, DEXP, D))
    out = _combine(posf.reshape(T // TC, 1, 2 * TC), info, x1, gt2, final_norm_g[None, :], ys)
    return out.reshape(B, L, D)
```

```python
import functools
import math

import numpy as np
import jax
import jax.numpy as jnp
from jax import lax
from jax.experimental import pallas as pl
from jax.experimental.pallas import tpu as pltpu

F32 = jnp.float32
BF16 = jnp.bfloat16
I32 = jnp.int32
HIGHEST = lax.Precision.HIGHEST

D = 1024
B = 2
L = 8192
T = B * L
CTX = 256
GRID_W = 64
EPS = 1e-6
H = 4
DK = 128
DV = 128
KD = H * DK
IN_W = 6144
HYW = 512
HY_EMB = 33
NGRP = 4
NEPG = 8
NEXP = NGRP * NEPG
DEXP = 512
HY_DECAY_TARGET = 1e-2
HY_FAST_PCT = 0.3
HY_SLOW_PCT = 1.5

V7X_LANES = 128
V7X_SUBLANES = 8
V7X_VMEM_BYTES = 64 * 1024 * 1024
VMEM_LIMIT = 48 * 1024 * 1024

FFT_N = 2 * L
FFT_P = 128
FFT_BB = 8

TM_IN = 1024
TN_IN = 1024
TH = 128
CB = 32
TM_HY = 1024
TM_MG = 256
TR = 512
TE = 256
NP_ROWS = 2 * T + NEXP * TE
NT_EXP = NP_ROWS // TE
TS = 512
TC = 256
ROW_UNROLL = 8
ST_OFF = NEXP
NA_OFF = 2 * NEXP


def _cparams(n_axes, vmem=None):
    return pltpu.CompilerParams(dimension_semantics=("arbitrary",) * n_axes,
                                vmem_limit_bytes=vmem)


def _split3(x):
    hi = x.astype(BF16)
    r = x - hi.astype(F32)
    mid = r.astype(BF16)
    lo = (r - mid.astype(F32)).astype(BF16)
    return hi, mid, lo


def _dot01(m, x):
    hi, mid, lo = _split3(x)
    return (jnp.dot(m, hi, preferred_element_type=F32) + jnp.dot(m, mid, preferred_element_type=F32)
            + jnp.dot(m, lo, preferred_element_type=F32))


def _rms(x):
    return x * lax.rsqrt(jnp.mean(x * x, axis=-1, keepdims=True) + EPS)


def _lane_pick(x, lane, idx):
    return jnp.sum(jnp.where(lane == idx, x, 0.0), axis=-1, keepdims=True)


def _ada_kernel(c_ref, w_ref, b_ref, o_ref):
    c = c_ref[...]
    s = c * jax.nn.sigmoid(c)
    o_ref[...] = jnp.dot(s, w_ref[...], preferred_element_type=F32, precision=HIGHEST) + b_ref[...]


def _adaln(cvec, w, b):
    tn = 1536
    return pl.pallas_call(
        _ada_kernel,
        grid=(6 * D // tn,),
        in_specs=[pl.BlockSpec((8, D), lambda j: (0, 0)),
                  pl.BlockSpec((D, tn), lambda j: (0, j)),
                  pl.BlockSpec((1, tn), lambda j: (0, j))],
        out_specs=pl.BlockSpec((8, tn), lambda j: (0, j)),
        out_shape=jax.ShapeDtypeStruct((8, 6 * D), F32),
        compiler_params=_cparams(1, VMEM_LIMIT),
        name="adaln",
    )(cvec, w, b)


def _keys(z, lb):
    sig = jax.nn.sigmoid(z)
    logf = jnp.log(lb + (1.0 - lb) * sig)
    k = (1.0 - lb) * jax.nn.sigmoid(-z)
    return k, logf


def _ctx_kernel(ctx_ref, g_ref, sh_ref, sc_ref, w_ref, lb_ref, sf_ref, sb_ref):
    h = _rms(ctx_ref[...]) * g_ref[...]
    h = h * (1.0 + sc_ref[...]) + sh_ref[...]
    p = jnp.dot(h.astype(BF16), w_ref[...], preferred_element_type=F32)
    zf, zb, v = p[:, :KD], p[:, KD:2 * KD], p[:, 2 * KD:]
    kf, lf = _keys(zf, lb_ref[0:1, :])
    kb, lbk = _keys(zb, lb_ref[1:2, :])
    r = lax.broadcasted_iota(I32, (CTX, CTX), 0)
    c = lax.broadcasted_iota(I32, (CTX, CTX), 1)
    tril = jnp.where(c <= r, 1.0, 0.0).astype(BF16)
    cf = _dot01(tril, lf)
    cb = _dot01(tril, lbk)
    kfd = (kf * jnp.exp(cf[CTX - 1:CTX, :] - cf)).astype(BF16)
    kbd = (kb * jnp.exp(cb - lbk)).astype(BF16)
    vb = v.astype(BF16)
    tn = (((0,), (0,)), ((), ()))
    for hh in range(H):
        hs = slice(hh * DK, (hh + 1) * DK)
        sf_ref[hh] = lax.dot_general(vb[:, hs], kfd[:, hs], tn, preferred_element_type=F32)
        sb_ref[hh] = lax.dot_general(vb[:, hs], kbd[:, hs], tn, preferred_element_type=F32)


def _context_states(ctx, g1, csh1, csc1, w_ctx, lbs):
    st = jax.ShapeDtypeStruct((B, H, DV, DK), F32)
    return pl.pallas_call(
        _ctx_kernel,
        grid=(B,),
        in_specs=[pl.BlockSpec((None, CTX, D), lambda b: (b, 0, 0)),
                  pl.BlockSpec((1, D), lambda b: (0, 0)),
                  pl.BlockSpec((1, D), lambda b: (0, 0)),
                  pl.BlockSpec((1, D), lambda b: (0, 0)),
                  pl.BlockSpec((D, 3 * KD), lambda b: (0, 0)),
                  pl.BlockSpec((2, KD), lambda b: (0, 0))],
        out_specs=(pl.BlockSpec((None, H, DV, DK), lambda b: (b, 0, 0, 0)),
                   pl.BlockSpec((None, H, DV, DK), lambda b: (b, 0, 0, 0))),
        out_shape=(st, st),
        compiler_params=_cparams(1, VMEM_LIMIT),
        name="ctx_states",
    )(ctx, g1, csh1, csc1, w_ctx, lbs)


def _inproj_kernel(x_ref, pos_ref, g_ref, sh_ref, sc_ref, w_ref, o_ref, hx_ref):
    @pl.when(pl.program_id(1) == 0)
    def _():
        h = _rms(x_ref[...] + pos_ref[...]) * g_ref[...]
        hx_ref[...] = (h * (1.0 + sc_ref[...]) + sh_ref[...]).astype(BF16)

    o_ref[...] = jnp.dot(hx_ref[...], w_ref[...], preferred_element_type=F32)


def _in_projection(x2, pos, g1, sh1, sc1, w_bf):
    tiles_per_batch = L // TM_IN
    return pl.pallas_call(
        _inproj_kernel,
        grid=(T // TM_IN, IN_W // TN_IN),
        in_specs=[pl.BlockSpec((TM_IN, D), lambda i, j: (i, 0)),
                  pl.BlockSpec((TM_IN, D), lambda i, j: (i % tiles_per_batch, 0)),
                  pl.BlockSpec((1, D), lambda i, j: (0, 0)),
                  pl.BlockSpec((None, 1, D), lambda i, j: (i // tiles_per_batch, 0, 0)),
                  pl.BlockSpec((None, 1, D), lambda i, j: (i // tiles_per_batch, 0, 0)),
                  pl.BlockSpec((D, TN_IN), lambda i, j: (0, j))],
        out_specs=pl.BlockSpec((TM_IN, TN_IN), lambda i, j: (i, j)),
        out_shape=jax.ShapeDtypeStruct((T, IN_W), F32),
        scratch_shapes=[pltpu.VMEM((TM_IN, D), BF16)],
        compiler_params=_cparams(2, VMEM_LIMIT),
        name="in_proj",
    )(x2, pos, g1, sh1, sc1, w_bf)


def _hgrn_kernel(reverse, readout, *refs):
    if readout:
        q_ref, z_ref, v_ref, lb_ref, s0_ref, of_ref, g_ref, ng_ref, o_ref, st_ref = refs
    else:
        q_ref, z_ref, v_ref, lb_ref, s0_ref, o_ref, st_ref = refs

    @pl.when(pl.program_id(1) == 0)
    def _():
        st_ref[...] = s0_ref[...]

    q = q_ref[...]
    v = v_ref[...]
    k, logf = _keys(z_ref[...], lb_ref[...])
    r = lax.broadcasted_iota(I32, (TH, TH), 0)
    c = lax.broadcasted_iota(I32, (TH, TH), 1)
    cb_shift = CB.bit_length() - 1
    same = jnp.right_shift(r, cb_shift) == jnp.right_shift(c, cb_shift)
    tri = jnp.where(same & ((c >= r) if reverse else (c <= r)), 1.0, 0.0).astype(BF16)
    bl = _dot01(tri, logf)
    vb = v.astype(BF16)
    rr = lax.broadcasted_iota(I32, (CB, CB), 0)
    cc = lax.broadcasted_iota(I32, (CB, CB), 1)
    mask = (cc >= rr) if reverse else (cc <= rr)
    nt = (((1,), (1,)), ((), ()))
    tn = (((0,), (0,)), ((), ()))
    nblk = TH // CB
    order = range(nblk - 1, -1, -1) if reverse else range(nblk)
    e_row = 0 if reverse else CB - 1
    m_row = CB - 1 - CB // 2 if reverse else CB // 2
    for jb in order:
        rs = slice(jb * CB, (jb + 1) * CB)
        blj = bl[rs]
        tau = blj[e_row:e_row + 1]
        mid = blj[m_row:m_row + 1]
        qj = q[rs]
        kj = k[rs]
        qd0 = (qj * jnp.exp(blj - mid)).astype(BF16)
        kd0 = (kj * jnp.exp(mid - blj)).astype(BF16)
        qs = (qj * jnp.exp(blj)).astype(BF16)
        ke = (kj * jnp.exp(tau - blj)).astype(BF16)
        dec = jnp.exp(tau)
        for hh in range(H):
            hs = slice(hh * DK, (hh + 1) * DK)
            sc = lax.dot_general(qd0[:, hs], kd0[:, hs], nt, preferred_element_type=F32)
            sc = jnp.where(mask, sc, 0.0).astype(BF16)
            st = st_ref[hh]
            o_h = (lax.dot_general(qs[:, hs], st.astype(BF16), nt, preferred_element_type=F32)
                   + jnp.dot(sc, vb[rs, hs], preferred_element_type=F32))
            st_ref[hh] = st * dec[:, hs] + lax.dot_general(vb[rs, hs], ke[:, hs], tn,
                                                           preferred_element_type=F32)
            if readout:
                o_h = o_h + of_ref[rs, hs]
                o_h = _rms(o_h) * ng_ref[...]
                gh = g_ref[rs, hs]
                o_h = o_h * (gh * jax.nn.sigmoid(gh))
            o_ref[rs, hs] = o_h


def _hgrn_scan(p, lb_row, s0, reverse, z_col, o_f=None, norm_g=None):
    nch = L // TH
    if reverse:
        row = lambda b, c: b * nch + (nch - 1 - c)
    else:
        row = lambda b, c: b * nch + c
    col_spec = lambda j: pl.BlockSpec((TH, KD), lambda b, c: (row(b, c), j))
    in_specs = [col_spec(0), col_spec(z_col), col_spec(3),
                pl.BlockSpec((1, KD), lambda b, c: (0, 0)),
                pl.BlockSpec((None, H, DV, DK), lambda b, c: (b, 0, 0, 0))]
    args = [p, p, p, lb_row, s0]
    readout = o_f is not None
    if readout:
        in_specs += [pl.BlockSpec((TH, KD), lambda b, c: (row(b, c), 0)), col_spec(4),
                     pl.BlockSpec((1, DV), lambda b, c: (0, 0))]
        args += [o_f, p, norm_g]
    return pl.pallas_call(
        functools.partial(_hgrn_kernel, reverse, readout),
        grid=(B, nch),
        in_specs=in_specs,
        out_specs=pl.BlockSpec((TH, KD), lambda b, c: (row(b, c), 0)),
        out_shape=jax.ShapeDtypeStruct((T, KD), F32),
        scratch_shapes=[pltpu.VMEM((H, DV, DK), F32)],
        compiler_params=_cparams(2, VMEM_LIMIT),
        name="hgrn_bwd_readout" if readout else "hgrn_fwd",
    )(*args)


def _hy_pre_kernel(v_ref, x1_ref, x0_ref, vp_ref, x1p_ref, x0p_ref, vn_ref, x1n_ref, x0n_ref,
                   w_ref, b_ref, vx_ref, x0o_ref):
    i = pl.program_id(1)
    first = i == 0
    last = i == pl.num_programs(1) - 1
    row = lax.broadcasted_iota(I32, (TM_HY, 1), 0)

    def conv(c_ref, p_ref, n_ref, col):
        x = c_ref[...]
        prev_row = jnp.where(first, 0.0, p_ref[V7X_SUBLANES - 1:V7X_SUBLANES, :])
        next_row = jnp.where(last, 0.0, n_ref[0:1, :])
        xm = jnp.where(row == 0, prev_row, pltpu.roll(x, 1, axis=0))
        xp = jnp.where(row == TM_HY - 1, next_row, pltpu.roll(x, TM_HY - 1, axis=0))
        cs = slice(col * HYW, (col + 1) * HYW)
        return xm * w_ref[0:1, cs] + x * w_ref[1:2, cs] + xp * w_ref[2:3, cs] + b_ref[:, cs]

    v = conv(v_ref, vp_ref, vn_ref, 0)
    x1 = conv(x1_ref, x1p_ref, x1n_ref, 1)
    x0 = conv(x0_ref, x0p_ref, x0n_ref, 2)
    vx_ref[...] = v * x1
    x0o_ref[...] = x0


def _hyena_pre(p, conv_w, conv_b):
    nt = L // TM_HY
    hb = TM_HY // V7X_SUBLANES
    nhb = T // V7X_SUBLANES
    cur = lambda col: pl.BlockSpec((TM_HY, HYW), lambda b, i: (b * nt + i, col))
    prv = lambda col: pl.BlockSpec((V7X_SUBLANES, HYW),
                                   lambda b, i: (jnp.maximum((b * nt + i) * hb - 1, 0), col))
    nxt = lambda col: pl.BlockSpec((V7X_SUBLANES, HYW),
                                   lambda b, i: (jnp.minimum((b * nt + i + 1) * hb, nhb - 1), col))
    c0 = 5
    out = jax.ShapeDtypeStruct((T, HYW), F32)
    return pl.pallas_call(
        _hy_pre_kernel,
        grid=(B, nt),
        in_specs=[cur(c0), cur(c0 + 1), cur(c0 + 2), prv(c0), prv(c0 + 1), prv(c0 + 2),
                  nxt(c0), nxt(c0 + 1), nxt(c0 + 2),
                  pl.BlockSpec((3, 3 * HYW), lambda b, i: (0, 0)),
                  pl.BlockSpec((1, 3 * HYW), lambda b, i: (0, 0))],
        out_specs=(pl.BlockSpec((TM_HY, HYW), lambda b, i: (b * nt + i, 0)),
                   pl.BlockSpec((TM_HY, HYW), lambda b, i: (b * nt + i, 0))),
        out_shape=(out, out),
        compiler_params=_cparams(2, VMEM_LIMIT),
        name="hyena_pre",
    )(p, p, p, p, p, p, p, p, p, conv_w, conv_b)


def _filt_kernel(z_ref, w1_ref, b1_ref, f1_ref, w2_ref, b2_ref, f2_ref, w3a_ref, w3b_ref, dl_ref, o_ref):
    half = z_ref.shape[0] // 2
    zt = z_ref[0:half, :]
    zb = z_ref[half:, :]
    lane = lax.broadcasted_iota(I32, zt.shape, 1)
    dot = functools.partial(jnp.dot, preferred_element_type=F32, precision=HIGHEST)
    h = jnp.sin(f1_ref[...] * (dot(jnp.concatenate([zt, zb], axis=1), w1_ref[...]) + b1_ref[...]))
    h = jnp.sin(f2_ref[...] * (dot(h, w2_ref[...]) + b2_ref[...]))
    for zz, w3_ref, rows in ((zt, w3a_ref, slice(0, half)), (zb, w3b_ref, slice(half, 2 * half))):
        taps = dot(h, w3_ref[...])
        win = jnp.exp(-_lane_pick(zz, lane, 0) * dl_ref[...])
        o_ref[0, rows, :] = taps[:, :HYW] * win
        o_ref[1, rows, :] = taps[:, HYW:] * win * _lane_pick(zz, lane, HY_EMB)


def _filter_taps(zin, w1, b1, f1, w2, b2, f2, w3a, w3b, deltas):
    tm = 2048
    ln = V7X_LANES
    full = lambda shape: pl.BlockSpec(shape, lambda i: (0, 0))
    return pl.pallas_call(
        _filt_kernel,
        grid=(L // tm,),
        in_specs=[pl.BlockSpec((tm, ln), lambda i: (i, 0)),
                  full((2 * ln, ln)), full((1, ln)), full((1, ln)),
                  full((ln, ln)), full((1, ln)), full((1, ln)),
                  full((ln, 2 * HYW)), full((ln, 2 * HYW)), full((1, HYW))],
        out_specs=pl.BlockSpec((2, tm, HYW), lambda i: (0, i, 0)),
        out_shape=jax.ShapeDtypeStruct((2, L, HYW), F32),
        compiler_params=_cparams(1, VMEM_LIMIT),
        name="hyena_filter",
    )(zin, w1, b1, f1, w2, b2, f2, w3a, w3b, deltas)


def _strided_dft_kernel(in_split, out_split, x_hbm, f_ref, o_hbm, xbuf, obuf, sem_in, sem_out):
    n = pl.program_id(0)
    j = pl.program_id(1)

    def src(jj):
        b = j * FFT_BB + jj
        return x_hbm.at[n, :, :, b, :] if in_split else x_hbm.at[n, :, b, :]

    def dst(jj):
        b = j * FFT_BB + jj
        return o_hbm.at[n, :, :, b, :] if out_split else o_hbm.at[n, :, b, :]

    for jj in range(FFT_BB):
        pltpu.make_async_copy(src(jj), xbuf.at[jj], sem_in).start()
    for jj in range(FFT_BB):
        pltpu.make_async_copy(src(jj), xbuf.at[jj], sem_in).wait()
    for jj in range(FFT_BB):
        x = xbuf[jj]
        if in_split:
            x = x.reshape(2 * FFT_P, HYW)
        r = jnp.dot(f_ref[...], x.astype(BF16), preferred_element_type=F32)
        if out_split:
            r = r.reshape(2, FFT_P, HYW)
        obuf[jj] = r
    for jj in range(FFT_BB):
        pltpu.make_async_copy(obuf.at[jj], dst(jj), sem_out).start()
    for jj in range(FFT_BB):
        pltpu.make_async_copy(obuf.at[jj], dst(jj), sem_out).wait()


def _dft_first(xv, fmat):
    n, kk = xv.shape[0], xv.shape[1]
    return pl.pallas_call(
        functools.partial(_strided_dft_kernel, False, True),
        grid=(n, FFT_P // FFT_BB),
        in_specs=[pl.BlockSpec(memory_space=pl.ANY),
                  pl.BlockSpec((2 * FFT_P, kk), lambda b, j: (0, 0))],
        out_specs=pl.BlockSpec(memory_space=pl.ANY),
        out_shape=jax.ShapeDtypeStruct((n, 2, FFT_P, FFT_P, HYW), F32),
        scratch_shapes=[pltpu.VMEM((FFT_BB, kk, HYW), F32), pltpu.VMEM((FFT_BB, 2, FFT_P, HYW), F32),
                        pltpu.SemaphoreType.DMA(()), pltpu.SemaphoreType.DMA(())],
        compiler_params=_cparams(2, VMEM_LIMIT),
        name="dft_first",
    )(xv, fmat)


def _dft_last(bp, gmat):
    n = bp.shape[0]
    na = L // FFT_P
    return pl.pallas_call(
        functools.partial(_strided_dft_kernel, True, False),
        grid=(n, FFT_P // FFT_BB),
        in_specs=[pl.BlockSpec(memory_space=pl.ANY),
                  pl.BlockSpec((na, 2 * FFT_P), lambda b, j: (0, 0))],
        out_specs=pl.BlockSpec(memory_space=pl.ANY),
        out_shape=jax.ShapeDtypeStruct((n, na, FFT_P, HYW), F32),
        scratch_shapes=[pltpu.VMEM((FFT_BB, 2, FFT_P, HYW), F32), pltpu.VMEM((FFT_BB, na, HYW), F32),
                        pltpu.SemaphoreType.DMA(()), pltpu.SemaphoreType.DMA(())],
        compiler_params=_cparams(2, VMEM_LIMIT),
        name="dft_last",
    )(bp, gmat)


def _cblock(mr, mi):
    return jnp.concatenate([jnp.concatenate([mr, -mi], axis=1), jnp.concatenate([mi, mr], axis=1)], axis=0)


def _stage2_matrix(fr_ref, fi_ref, tw_ref):
    twr = tw_ref[0:1, :]
    twi = tw_ref[1:2, :]
    fr = fr_ref[...]
    fi = fi_ref[...]
    return _cblock(fr * twr - fi * twi, fr * twi + fi * twr).astype(BF16)


_STAGE2_SPECS = [pl.BlockSpec((FFT_P, FFT_P), lambda d: (0, 0)),
                 pl.BlockSpec((FFT_P, FFT_P), lambda d: (0, 0)),
                 pl.BlockSpec((None, 2, FFT_P), lambda d: (d, 0, 0))]


def _kspec_kernel(a_ref, fr_ref, fi_ref, tw_ref, o_ref):
    rm = _stage2_matrix(fr_ref, fi_ref, tw_ref)
    xf = jnp.dot(rm, a_ref[0].reshape(2 * FFT_P, HYW).astype(BF16), preferred_element_type=F32)
    xb = jnp.dot(rm, a_ref[1].reshape(2 * FFT_P, HYW).astype(BF16), preferred_element_type=F32)
    o_ref[0] = xf[:FFT_P] + xb[:FFT_P]
    o_ref[1] = xf[FFT_P:] - xb[FFT_P:]


def _kernel_spectrum(ak, fr, fi, tw):
    return pl.pallas_call(
        _kspec_kernel,
        grid=(FFT_P,),
        in_specs=[pl.BlockSpec((2, 2, None, FFT_P, HYW), lambda d: (0, 0, d, 0, 0))] + _STAGE2_SPECS,
        out_specs=pl.BlockSpec((2, None, FFT_P, HYW), lambda d: (0, d, 0, 0)),
        out_shape=jax.ShapeDtypeStruct((2, FFT_P, FFT_P, HYW), F32),
        compiler_params=_cparams(1, VMEM_LIMIT),
        name="kernel_spectrum",
    )(ak, fr, fi, tw)


def _mid_kernel(a_ref, kh_ref, fr_ref, fi_ref, tw_ref, o_ref):
    rm = _stage2_matrix(fr_ref, fi_ref, tw_ref)
    kr = kh_ref[0]
    ki = kh_ref[1]
    tn = (((0,), (0,)), ((), ()))
    for n in range(B):
        a = a_ref[n].reshape(2 * FFT_P, HYW).astype(BF16)
        x = jnp.dot(rm, a, preferred_element_type=F32)
        xr, xi = x[:FFT_P], x[FFT_P:]
        y = jnp.concatenate([xr * kr - xi * ki, xr * ki + xi * kr], axis=0).astype(BF16)
        o_ref[n] = lax.dot_general(rm, y, tn, preferred_element_type=F32).reshape(2, FFT_P, HYW)


def _fft_mid(au, kh, fr, fi, tw):
    return pl.pallas_call(
        _mid_kernel,
        grid=(FFT_P,),
        in_specs=[pl.BlockSpec((B, 2, None, FFT_P, HYW), lambda d: (0, 0, d, 0, 0)),
                  pl.BlockSpec((2, None, FFT_P, HYW), lambda d: (0, d, 0, 0))] + _STAGE2_SPECS,
        out_specs=pl.BlockSpec((B, 2, None, FFT_P, HYW), lambda d: (0, 0, d, 0, 0)),
        out_shape=jax.ShapeDtypeStruct((B, 2, FFT_P, FFT_P, HYW), F32),
        compiler_params=_cparams(1, VMEM_LIMIT),
        name="fft_mid",
    )(au, kh, fr, fi, tw)


def _dft_tables():
    a = np.arange(FFT_P)
    ang1 = 2.0 * np.pi * np.outer(a, a) / FFT_P
    na = L // FFT_P
    f_first = np.concatenate([np.cos(ang1), -np.sin(ang1)], axis=0)[:, :na]
    g_last = np.concatenate([np.cos(ang1[:na]), -np.sin(ang1[:na])], axis=1) / FFT_N
    ang2 = 2.0 * np.pi * np.outer(a, a) / FFT_N
    tw = np.stack([np.cos(ang2), -np.sin(ang2)], axis=1)
    f32 = lambda x: jnp.asarray(x.astype(np.float32))
    return (f32(f_first).astype(BF16), f32(g_last).astype(BF16), f32(np.cos(ang1)), f32(-np.sin(ang1)), f32(tw))


def _merge_kernel(ya_ref, cv_ref, vx_ref, x0_ref, ga_ref, gb_ref, x_ref, pos_ref, gt1_ref, hyd_ref,
                  wpa_ref, wpb_ref, wo_ref, g2_ref, sh2_ref, sc2_ref, wr_ref, br_ref,
                  x1_ref, t2_ref, lg_ref):
    vx = vx_ref[...]
    yb = x0_ref[...] * (cv_ref[...] + vx * hyd_ref[...])
    pa = jnp.dot(ya_ref[...].astype(BF16), wpa_ref[...], preferred_element_type=F32)
    pb = jnp.dot(yb.astype(BF16), wpb_ref[...], preferred_element_type=F32)
    mixed = jax.nn.sigmoid(ga_ref[...]) * pa + jax.nn.sigmoid(gb_ref[...]) * pb
    xm = jnp.dot(mixed.astype(BF16), wo_ref[...], preferred_element_type=F32)
    x1 = x_ref[...] + pos_ref[...] + gt1_ref[...] * xm
    x1_ref[...] = x1
    t2 = _rms(x1) * g2_ref[...]
    t2 = t2 * (1.0 + sc2_ref[...]) + sh2_ref[...]
    t2_ref[...] = t2
    lg_ref[...] = jnp.dot(t2, wr_ref[...], preferred_element_type=F32, precision=HIGHEST) + br_ref[...]


def _merge(ya, cv, vx, x0c, p, x2, pos, gt1, hyd, wpa, wpb, wo, g2, sh2, sc2, wr, br):
    tpb = L // TM_MG
    half = lambda: pl.BlockSpec((TM_MG, HYW), lambda i: (i, 0))
    full = lambda shape: pl.BlockSpec(shape, lambda i: tuple(0 for _ in shape))
    perb = lambda: pl.BlockSpec((None, 1, D), lambda i: (i // tpb, 0, 0))
    return pl.pallas_call(
        _merge_kernel,
        grid=(T // TM_MG,),
        in_specs=[half(), half(), half(), half(),
                  pl.BlockSpec((TM_MG, D), lambda i: (i, 4)),
                  pl.BlockSpec((TM_MG, D), lambda i: (i, 5)),
                  pl.BlockSpec((TM_MG, D), lambda i: (i, 0)),
                  pl.BlockSpec((TM_MG, D), lambda i: (i % tpb, 0)),
                  perb(), full((1, HYW)),
                  full((KD, D)), full((HYW, D)), full((D, D)),
                  full((1, D)), perb(), perb(),
                  full((D, V7X_LANES)), full((1, V7X_LANES))],
        out_specs=(pl.BlockSpec((TM_MG, D), lambda i: (i, 0)),
                   pl.BlockSpec((TM_MG, D), lambda i: (i, 0)),
                   pl.BlockSpec((TM_MG, V7X_LANES), lambda i: (i, 0))),
        out_shape=(jax.ShapeDtypeStruct((T, D), F32), jax.ShapeDtypeStruct((T, D), F32),
                   jax.ShapeDtypeStruct((T, V7X_LANES), F32)),
        compiler_params=_cparams(1, VMEM_LIMIT),
        name="merge",
    )(ya, cv, vx, x0c, p, p, x2, pos, gt1, hyd, wpa, wpb, wo, g2, sh2, sc2, wr, br)


def _route_kernel(lg_ref, info_ref, cnt_ref):
    @pl.when(pl.program_id(0) == 0)
    def _():
        cnt_ref[...] = jnp.zeros_like(cnt_ref)

    lg = lg_ref[...]
    lane = lax.broadcasted_iota(I32, lg.shape, 1)
    lanef = lane.astype(F32)
    neg = -1e30
    big = 1e9
    is_g = (lane >= NEXP) & (lane < NEXP + NGRP)
    gl = jnp.where(is_g, lg, neg)
    ge = jnp.where(is_g, jnp.exp(gl - jnp.max(gl, axis=-1, keepdims=True)), 0.0)
    pg = ge / jnp.sum(ge, axis=-1, keepdims=True)
    p_top_g = jnp.max(pg, axis=-1, keepdims=True)
    gidx = jnp.min(jnp.where(is_g & (pg == p_top_g), lanef, big), axis=-1, keepdims=True)
    g_sel = gidx.astype(I32) - NEXP
    emask = (lane < NEXP) & (jnp.right_shift(lane, NEPG.bit_length() - 1) == g_sel)
    el = jnp.where(emask, lg, neg)
    ee = jnp.where(emask, jnp.exp(el - jnp.max(el, axis=-1, keepdims=True)), 0.0)
    pe = ee / jnp.sum(ee, axis=-1, keepdims=True)
    p1 = jnp.max(jnp.where(emask, pe, -1.0), axis=-1, keepdims=True)
    i1 = jnp.min(jnp.where(emask & (pe == p1), lanef, big), axis=-1, keepdims=True)
    rest = emask & (lanef != i1)
    p2 = jnp.max(jnp.where(rest, pe, -1.0), axis=-1, keepdims=True)
    i2 = jnp.min(jnp.where(rest & (pe == p2), lanef, big), axis=-1, keepdims=True)
    wsum = p1 + p2
    w1 = p_top_g * p1 / wsum
    w2 = p_top_g * p2 / wsum
    sel1 = lanef == i1
    sel2 = lanef == i2
    oh = jnp.where(sel1 | sel2, 1.0, 0.0)
    r = lax.broadcasted_iota(I32, (TR, TR), 0)
    c = lax.broadcasted_iota(I32, (TR, TR), 1)
    stril = jnp.where(c < r, 1.0, 0.0).astype(BF16)
    before = jnp.dot(stril, oh.astype(BF16), preferred_element_type=F32) + cnt_ref[...]
    r1 = jnp.sum(jnp.where(sel1, before, 0.0), axis=-1, keepdims=True)
    r2 = jnp.sum(jnp.where(sel2, before, 0.0), axis=-1, keepdims=True)
    cnt_ref[...] += jnp.sum(oh, axis=0, keepdims=True)
    info = jnp.where(lane == 0, i1, jnp.where(lane == 1, r1, jnp.where(lane == 2, i2, jnp.where(
        lane == 3, r2, jnp.where(lane == 4, w1, jnp.where(lane == 5, w2, 0.0))))))
    info_ref[...] = info


def _route(lg):
    return pl.pallas_call(
        _route_kernel,
        grid=(T // TR,),
        in_specs=[pl.BlockSpec((TR, V7X_LANES), lambda i: (i, 0))],
        out_specs=(pl.BlockSpec((TR, V7X_LANES), lambda i: (i, 0)),
                   pl.BlockSpec((1, V7X_LANES), lambda i: (0, 0))),
        out_shape=(jax.ShapeDtypeStruct((T, V7X_LANES), F32), jax.ShapeDtypeStruct((1, V7X_LANES), F32)),
        compiler_params=_cparams(1, VMEM_LIMIT),
        name="route",
    )(lg)


def _scatter_kernel(er_ref, zt_ref, t2_ref, zeros_hbm, xs_hbm, sem, zsem):
    i = pl.program_id(0)

    def zcopy(row):
        start = pl.multiple_of(jnp.maximum(row, 0), TE)
        return pltpu.make_async_copy(zeros_hbm, xs_hbm.at[pl.ds(start, TE)], zsem)

    @pl.when(i == 0)
    def _():
        def ztail(start, e, carry):
            @pl.when(zt_ref[0, e] >= 0)
            def _():
                cp = zcopy(zt_ref[0, e])
                cp.start() if start else cp.wait()
            return carry

        lax.fori_loop(0, NEXP, functools.partial(ztail, True), 0)
        lax.fori_loop(0, NEXP, functools.partial(ztail, False), 0)

        def zrest(start, tile, carry):
            cp = zcopy(tile * TE)
            cp.start() if start else cp.wait()
            return carry

        lax.fori_loop(zt_ref[0, NA_OFF], NT_EXP, functools.partial(zrest, True), 0)
        lax.fori_loop(zt_ref[0, NA_OFF], NT_EXP, functools.partial(zrest, False), 0)

    def issue(j, carry):
        for kk in range(2):
            dst = zt_ref[0, ST_OFF + er_ref[0, 4 * j + 2 * kk]] + er_ref[0, 4 * j + 2 * kk + 1]
            pltpu.make_async_copy(t2_ref.at[pl.ds(j, 1)], xs_hbm.at[pl.ds(dst, 1)], sem).start(priority=kk)
        return carry

    lax.fori_loop(0, TS, issue, 0, unroll=ROW_UNROLL)

    def drain(j, carry):
        for kk in range(2):
            pltpu.make_async_copy(t2_ref.at[pl.ds(0, 1)], xs_hbm.at[pl.ds(0, 1)], sem).wait()
        return carry

    lax.fori_loop(0, TS, drain, 0, unroll=ROW_UNROLL)


def _scatter_rows(er3, meta, t2, zeros_tile):
    return pl.pallas_call(
        _scatter_kernel,
        grid=(T // TS,),
        in_specs=[pl.BlockSpec((None, 1, 4 * TS), lambda i: (i, 0, 0), memory_space=pltpu.SMEM),
                  pl.BlockSpec(memory_space=pltpu.SMEM),
                  pl.BlockSpec((TS, D), lambda i: (i, 0)),
                  pl.BlockSpec(memory_space=pl.ANY)],
        out_specs=pl.BlockSpec(memory_space=pl.ANY),
        out_shape=jax.ShapeDtypeStruct((NP_ROWS, D), F32),
        scratch_shapes=[pltpu.SemaphoreType.DMA(()), pltpu.SemaphoreType.DMA(())],
        compiler_params=_cparams(1, VMEM_LIMIT),
        name="scatter_rows",
    )(er3, meta, t2, zeros_tile)


def _expert_kernel(te_ref, na_ref, xs_ref, wg_ref, wu_ref, wd_ref, ys_ref):
    @pl.when(pl.program_id(0) < na_ref[0])
    def _():
        x = xs_ref[...].astype(BF16)
        g = jnp.dot(x, wg_ref[...].astype(BF16), preferred_element_type=F32)
        u = jnp.dot(x, wu_ref[...].astype(BF16), preferred_element_type=F32)
        hid = (g * jax.nn.sigmoid(g) * u).astype(BF16)
        ys_ref[...] = jnp.dot(hid, wd_ref[...].astype(BF16), preferred_element_type=F32)

    @pl.when(pl.program_id(0) >= na_ref[0])
    def _():
        ys_ref[...] = jnp.zeros_like(ys_ref)


def _experts(tile_expert, n_active, xs, wg, wu, wd):
    rows = lambda i, te, na: (i, 0)
    wsel = lambda i, te, na: (te[i], 0, 0)
    grid_spec = pltpu.PrefetchScalarGridSpec(
        num_scalar_prefetch=2,
        grid=(NT_EXP,),
        in_specs=[pl.BlockSpec((TE, D), rows),
                  pl.BlockSpec((None, D, DEXP), wsel),
                  pl.BlockSpec((None, D, DEXP), wsel),
                  pl.BlockSpec((None, DEXP, D), wsel)],
        out_specs=pl.BlockSpec((TE, D), rows),
    )
    return pl.pallas_call(
        _expert_kernel,
        grid_spec=grid_spec,
        out_shape=jax.ShapeDtypeStruct((NP_ROWS, D), F32),
        compiler_params=_cparams(1, VMEM_LIMIT),
        name="experts",
    )(tile_expert, n_active, xs, wg, wu, wd)


def _combine_kernel(er_ref, zt_ref, info_ref, x1_ref, gt2_ref, fg_ref, ys_hbm, o_ref, buf, sem):
    def issue(j, carry):
        for kk in range(2):
            src = zt_ref[0, ST_OFF + er_ref[0, 4 * j + 2 * kk]] + er_ref[0, 4 * j + 2 * kk + 1]
            pltpu.make_async_copy(ys_hbm.at[pl.ds(src, 1)], buf.at[kk, pl.ds(j, 1)], sem).start(priority=kk)
        return carry

    lax.fori_loop(0, TC, issue, 0, unroll=ROW_UNROLL)

    def drain(j, carry):
        for kk in range(2):
            pltpu.make_async_copy(ys_hbm.at[pl.ds(0, 1)], buf.at[kk, pl.ds(0, 1)], sem).wait()
        return carry

    lax.fori_loop(0, TC, drain, 0, unroll=ROW_UNROLL)

    info = info_ref[...]
    lane = lax.broadcasted_iota(I32, info.shape, 1)
    w1 = _lane_pick(info, lane, 4)
    w2 = _lane_pick(info, lane, 5)
    moe = w1 * buf[0] + w2 * buf[1]
    x2 = x1_ref[...] + gt2_ref[...] * moe
    o_ref[...] = _rms(x2) * fg_ref[...]


def _combine(er3, meta, info, x1, gt2, fg, ys):
    tpb = L // TC
    return pl.pallas_call(
        _combine_kernel,
        grid=(T // TC,),
        in_specs=[pl.BlockSpec((None, 1, 4 * TC), lambda i: (i, 0, 0), memory_space=pltpu.SMEM),
                  pl.BlockSpec(memory_space=pltpu.SMEM),
                  pl.BlockSpec((TC, V7X_LANES), lambda i: (i, 0)),
                  pl.BlockSpec((TC, D), lambda i: (i, 0)),
                  pl.BlockSpec((None, 1, D), lambda i: (i // tpb, 0, 0)),
                  pl.BlockSpec((1, D), lambda i: (0, 0)),
                  pl.BlockSpec(memory_space=pl.ANY)],
        out_specs=pl.BlockSpec((TC, D), lambda i: (i, 0)),
        out_shape=jax.ShapeDtypeStruct((T, D), F32),
        scratch_shapes=[pltpu.VMEM((2, TC, D), F32), pltpu.SemaphoreType.DMA(())],
        compiler_params=_cparams(1, VMEM_LIMIT),
        name="combine",
    )(er3, meta, info, x1, gt2, fg, ys)


def _pos_emb():
    rows = L // GRID_W
    quarter = D // 4
    omega = 1.0 / (10000.0 ** (jnp.arange(quarter, dtype=F32) / quarter))

    def axis_emb(pos):
        a = pos[:, None] * omega[None, :]
        return jnp.concatenate([jnp.sin(a), jnp.cos(a)], axis=-1)

    er = axis_emb(jnp.arange(rows, dtype=F32))
    ec = axis_emb(jnp.arange(GRID_W, dtype=F32))
    return jnp.concatenate([jnp.repeat(er, GRID_W, axis=0), jnp.tile(ec, (rows, 1))], axis=-1)


def _filter_features():
    t = jnp.linspace(0.0, 1.0, L, dtype=F32)[:, None]
    bands = (HY_EMB - 1) // 2
    ang = (2.0 * math.pi * jnp.arange(L, dtype=F32) / L)[:, None] * \
        jnp.linspace(1e-4, bands - 1, bands, dtype=F32)[None, :]
    not_tap0 = (jnp.arange(L) >= 1).astype(F32)[:, None]
    pad = jnp.zeros((L, V7X_LANES - HY_EMB - 1), F32)
    return jnp.concatenate([t, jnp.cos(ang), -jnp.sin(ang), not_tap0, pad], axis=-1)


def _pad2(a, rows, cols):
    return jnp.pad(a, ((0, rows - a.shape[0]), (0, cols - a.shape[1])))


def kernel(x, c, ctx, c_ctx, ada_w, ada_b, norm1_g, norm2_g, w_in, hgrn_lb, hgrn_norm_g, hy_conv_w, hy_conv_b, hy_filt_w1, hy_filt_b1, hy_filt_freq1, hy_filt_w2, hy_filt_b2, hy_filt_freq2, hy_filt_w3, hy_d, w_proj_a, w_proj_b, w_out, moe_router_g_w, moe_router_g_b, moe_router_e_w, moe_router_e_b, moe_w_gate, moe_w_up, moe_w_down, final_norm_g):
    cvec = jnp.zeros((8, D), F32).at[0:B].set(c).at[B].set(c_ctx)
    mod = _adaln(cvec, ada_w[0], ada_b[0][None, :])
    m6 = mod.reshape(8, 6, D)
    sh1, sc1, gt1, sh2, sc2, gt2 = [m6[0:B, k][:, None, :] for k in range(6)]
    csh1, csc1 = m6[B:B + 1, 0], m6[B:B + 1, 1]

    lbs = jnp.cumsum(jax.nn.softmax(hgrn_lb.astype(F32), axis=0), axis=0)[0]
    g1 = norm1_g[0][None, :]
    pos = _pos_emb()
    x2 = x.reshape(T, D)

    w_ctx = w_in[0][:, KD:4 * KD].astype(BF16)
    s_f, s_b = _context_states(ctx, g1, csh1, csc1, w_ctx, lbs)

    p = _in_projection(x2, pos, g1, sh1, sc1, w_in[0].astype(BF16))

    o_f = _hgrn_scan(p, lbs[0:1], s_f, False, 1)
    y_a = _hgrn_scan(p, lbs[1:2], s_b, True, 2, o_f=o_f, norm_g=hgrn_norm_g[0][None, :])

    vx, x0c = _hyena_pre(p, hy_conv_w[0], hy_conv_b[0][None, :])
    deltas = jnp.abs(jnp.linspace(math.log(HY_DECAY_TARGET) / HY_SLOW_PCT,
                                  math.log(HY_DECAY_TARGET) / HY_FAST_PCT, HYW, dtype=F32))[None, :]
    ln = V7X_LANES
    fh = hy_filt_w2.shape[-1]
    blockdiag = lambda m: jnp.concatenate([_pad2(m, m.shape[0], 2 * m.shape[1]),
                                           jnp.pad(m, ((0, 0), (m.shape[1], 0)))], axis=0)
    twice = lambda v: jnp.concatenate([v, v])[None, :]
    w3 = hy_filt_w3[0]
    taps = _filter_taps(
        _filter_features(),
        blockdiag(_pad2(hy_filt_w1[0], ln, fh)), twice(hy_filt_b1[0]), twice(hy_filt_freq1[0]),
        blockdiag(hy_filt_w2[0]), twice(hy_filt_b2[0]), twice(hy_filt_freq2[0]),
        _pad2(w3, ln, 2 * HYW), jnp.pad(w3, ((fh, 0), (0, 0))), deltas)
    f_first, g_last, fr, fi, tw = _dft_tables()
    ak = _dft_first(taps.reshape(2, L // FFT_P, FFT_P, HYW), f_first)
    kh = _kernel_spectrum(ak, fr, fi, tw)
    au = _dft_first(vx.reshape(B, L // FFT_P, FFT_P, HYW), f_first)
    bp = _fft_mid(au, kh, fr, fi, tw)
    conv = _dft_last(bp, g_last).reshape(T, HYW)

    wr = jnp.concatenate([jnp.transpose(moe_router_e_w[0], (1, 0, 2)).reshape(D, NEXP),
                          moe_router_g_w[0], jnp.zeros((D, V7X_LANES - NEXP - NGRP), F32)], axis=1)
    br = jnp.concatenate([moe_router_e_b[0].reshape(NEXP), moe_router_g_b[0],
                          jnp.zeros((V7X_LANES - NEXP - NGRP,), F32)])[None, :]
    x1, t2, lg = _merge(y_a, conv, vx, x0c, p, x2, pos, gt1, hy_d[0][None, :],
                        w_proj_a[0].astype(BF16), w_proj_b[0].astype(BF16), w_out[0].astype(BF16),
                        norm2_g[0][None, :], sh2, sc2, wr, br)

    info, counts = _route(lg)
    cnt = counts[0, :NEXP].astype(I32)
    pc = ((cnt + TE - 1) // TE) * TE
    ends = jnp.cumsum(pc)
    starts = ends - pc
    n_active = (ends[-1] // TE).astype(I32)[None]
    tile_rows = jnp.arange(NT_EXP, dtype=I32) * TE
    tile_expert = jnp.minimum(jnp.sum((ends[None, :] <= tile_rows[:, None]).astype(I32), axis=1), NEXP - 1)
    meta = jnp.concatenate([jnp.where(pc > 0, ends - TE, -1), starts, n_active]).astype(I32)[None, :]
    er = info[:, :4].astype(I32)

    xs = _scatter_rows(er.reshape(T // TS, 1, 4 * TS), meta, t2, jnp.zeros((TE, D), F32))
    ys = _experts(tile_expert, n_active, xs,
                  moe_w_gate[0].reshape(NEXP, D, DEXP), moe_w_up[0].reshape(NEXP, D, DEXP),
                  moe_w_down[0].reshape(NEXP, DEXP, D))
    out = _combine(er.reshape(T // TC, 1, 4 * TC), meta, info, x1, gt2, final_norm_g[None, :], ys)
    return out.reshape(B, L, D)
```

```python
import functools
import math

import numpy as np
import jax
import jax.numpy as jnp
from jax import lax
from jax.experimental import pallas as pl
from jax.experimental.pallas import tpu as pltpu

F32 = jnp.float32
BF16 = jnp.bfloat16
I32 = jnp.int32
HIGHEST = lax.Precision.HIGHEST

D = 1024
B = 2
L = 8192
T = B * L
CTX = 256
GRID_W = 64
EPS = 1e-6
H = 4
DK = 128
DV = 128
KD = H * DK
IN_W = 6144
HYW = 512
HY_EMB = 33
NGRP = 4
NEPG = 8
NEXP = NGRP * NEPG
DEXP = 512
HY_DECAY_TARGET = 1e-2
HY_FAST_PCT = 0.3
HY_SLOW_PCT = 1.5

V7X_LANES = 128
V7X_SUBLANES = 8
V7X_VMEM_BYTES = 64 * 1024 * 1024
VMEM_LIMIT = 48 * 1024 * 1024

FFT_N = 2 * L
FFT_P = 128
FFT_BB = 8

TM_IN = 1024
TN_IN = 1024
TH = 128
CB = 32
TM_HY = 1024
HALO = 2 * V7X_SUBLANES
TM_MG = 256
TR = 512
TE = 256
NP_ROWS = 2 * T + NEXP * TE
NT_EXP = NP_ROWS // TE
TS = 512
TC = 256
ROW_UNROLL = 8
ST_OFF = NEXP
NA_OFF = 2 * NEXP


def _cparams(n_axes, vmem=None):
    return pltpu.CompilerParams(dimension_semantics=("arbitrary",) * n_axes,
                                vmem_limit_bytes=vmem)


def _split3(x):
    hi = x.astype(BF16)
    r = x - hi.astype(F32)
    mid = r.astype(BF16)
    lo = (r - mid.astype(F32)).astype(BF16)
    return hi, mid, lo


def _dot01(m, x):
    hi, mid, lo = _split3(x)
    return (jnp.dot(m, hi, preferred_element_type=F32) + jnp.dot(m, mid, preferred_element_type=F32)
            + jnp.dot(m, lo, preferred_element_type=F32))


def _rms(x):
    return x * lax.rsqrt(jnp.mean(x * x, axis=-1, keepdims=True) + EPS)


def _lane_pick(x, lane, idx):
    return jnp.sum(jnp.where(lane == idx, x, 0.0), axis=-1, keepdims=True)


def _ada_kernel(c_ref, w_ref, b_ref, o_ref):
    c = c_ref[...]
    s = c * jax.nn.sigmoid(c)
    o_ref[...] = jnp.dot(s, w_ref[...], preferred_element_type=F32, precision=HIGHEST) + b_ref[...]


def _adaln(cvec, w, b):
    tn = 1536
    return pl.pallas_call(
        _ada_kernel,
        grid=(6 * D // tn,),
        in_specs=[pl.BlockSpec((8, D), lambda j: (0, 0)),
                  pl.BlockSpec((D, tn), lambda j: (0, j)),
                  pl.BlockSpec((1, tn), lambda j: (0, j))],
        out_specs=pl.BlockSpec((8, tn), lambda j: (0, j)),
        out_shape=jax.ShapeDtypeStruct((8, 6 * D), F32),
        compiler_params=_cparams(1, VMEM_LIMIT),
        name="adaln",
    )(cvec, w, b)


def _keys(z, lb):
    sig = jax.nn.sigmoid(z)
    logf = jnp.log(lb + (1.0 - lb) * sig)
    k = (1.0 - lb) * jax.nn.sigmoid(-z)
    return k, logf


def _ctx_kernel(ctx_ref, g_ref, sh_ref, sc_ref, w_ref, lb_ref, sf_ref, sb_ref):
    h = _rms(ctx_ref[...]) * g_ref[...]
    h = h * (1.0 + sc_ref[...]) + sh_ref[...]
    p = jnp.dot(h.astype(BF16), w_ref[...], preferred_element_type=F32)
    zf, zb, v = p[:, :KD], p[:, KD:2 * KD], p[:, 2 * KD:]
    kf, lf = _keys(zf, lb_ref[0:1, :])
    kb, lbk = _keys(zb, lb_ref[1:2, :])
    r = lax.broadcasted_iota(I32, (CTX, CTX), 0)
    c = lax.broadcasted_iota(I32, (CTX, CTX), 1)
    tril = jnp.where(c <= r, 1.0, 0.0).astype(BF16)
    cf = _dot01(tril, lf)
    cb = _dot01(tril, lbk)
    kfd = (kf * jnp.exp(cf[CTX - 1:CTX, :] - cf)).astype(BF16)
    kbd = (kb * jnp.exp(cb - lbk)).astype(BF16)
    vb = v.astype(BF16)
    tn = (((0,), (0,)), ((), ()))
    for hh in range(H):
        hs = slice(hh * DK, (hh + 1) * DK)
        sf_ref[hh] = lax.dot_general(vb[:, hs], kfd[:, hs], tn, preferred_element_type=F32)
        sb_ref[hh] = lax.dot_general(vb[:, hs], kbd[:, hs], tn, preferred_element_type=F32)


def _context_states(ctx, g1, csh1, csc1, w_ctx, lbs):
    st = jax.ShapeDtypeStruct((B, H, DV, DK), F32)
    return pl.pallas_call(
        _ctx_kernel,
        grid=(B,),
        in_specs=[pl.BlockSpec((None, CTX, D), lambda b: (b, 0, 0)),
                  pl.BlockSpec((1, D), lambda b: (0, 0)),
                  pl.BlockSpec((1, D), lambda b: (0, 0)),
                  pl.BlockSpec((1, D), lambda b: (0, 0)),
                  pl.BlockSpec((D, 3 * KD), lambda b: (0, 0)),
                  pl.BlockSpec((2, KD), lambda b: (0, 0))],
        out_specs=(pl.BlockSpec((None, H, DV, DK), lambda b: (b, 0, 0, 0)),
                   pl.BlockSpec((None, H, DV, DK), lambda b: (b, 0, 0, 0))),
        out_shape=(st, st),
        compiler_params=_cparams(1, VMEM_LIMIT),
        name="ctx_states",
    )(ctx, g1, csh1, csc1, w_ctx, lbs)


def _inproj_kernel(x_ref, pos_ref, g_ref, sh_ref, sc_ref, w_ref, oz_ref, o_ref, hx_ref):
    j = pl.program_id(1)

    @pl.when(j == 0)
    def _():
        h = _rms(x_ref[...] + pos_ref[...]) * g_ref[...]
        hx_ref[...] = (h * (1.0 + sc_ref[...]) + sh_ref[...]).astype(BF16)

    r = jnp.dot(hx_ref[...], w_ref[...], preferred_element_type=F32)

    @pl.when(j == 0)
    def _():
        oz_ref[...] = r

    @pl.when(j > 0)
    def _():
        o_ref[...] = r.astype(BF16)


def _in_projection(x2, pos, g1, sh1, sc1, w_bf):
    tiles_per_batch = L // TM_IN
    return pl.pallas_call(
        _inproj_kernel,
        grid=(T // TM_IN, IN_W // TN_IN),
        in_specs=[pl.BlockSpec((TM_IN, D), lambda i, j: (i, 0)),
                  pl.BlockSpec((TM_IN, D), lambda i, j: (i % tiles_per_batch, 0)),
                  pl.BlockSpec((1, D), lambda i, j: (0, 0)),
                  pl.BlockSpec((None, 1, D), lambda i, j: (i // tiles_per_batch, 0, 0)),
                  pl.BlockSpec((None, 1, D), lambda i, j: (i // tiles_per_batch, 0, 0)),
                  pl.BlockSpec((D, TN_IN), lambda i, j: (0, j))],
        out_specs=(pl.BlockSpec((TM_IN, TN_IN), lambda i, j: (i, 0)),
                   pl.BlockSpec((TM_IN, TN_IN), lambda i, j: (i, jnp.maximum(j - 1, 0)))),
        out_shape=(jax.ShapeDtypeStruct((T, 2 * KD), F32),
                   jax.ShapeDtypeStruct((T, IN_W - 2 * KD), BF16)),
        scratch_shapes=[pltpu.VMEM((TM_IN, D), BF16)],
        compiler_params=_cparams(2, VMEM_LIMIT),
        name="in_proj",
    )(x2, pos, g1, sh1, sc1, w_bf)


def _hgrn_kernel(reverse, readout, *refs):
    if readout:
        q_ref, z_ref, v_ref, lb_ref, s0_ref, of_ref, g_ref, ng_ref, o_ref, st_ref = refs
    else:
        q_ref, z_ref, v_ref, lb_ref, s0_ref, o_ref, st_ref = refs

    @pl.when(pl.program_id(1) == 0)
    def _():
        st_ref[...] = s0_ref[...]

    q = q_ref[...].astype(F32)
    v = v_ref[...]
    k, logf = _keys(z_ref[...], lb_ref[...])
    r = lax.broadcasted_iota(I32, (TH, TH), 0)
    c = lax.broadcasted_iota(I32, (TH, TH), 1)
    cb_shift = CB.bit_length() - 1
    same = jnp.right_shift(r, cb_shift) == jnp.right_shift(c, cb_shift)
    tri = jnp.where(same & ((c >= r) if reverse else (c <= r)), 1.0, 0.0).astype(BF16)
    bl = _dot01(tri, logf)
    vb = v.astype(BF16)
    rr = lax.broadcasted_iota(I32, (CB, CB), 0)
    cc = lax.broadcasted_iota(I32, (CB, CB), 1)
    mask = (cc >= rr) if reverse else (cc <= rr)
    nt = (((1,), (1,)), ((), ()))
    tn = (((0,), (0,)), ((), ()))
    nblk = TH // CB
    order = range(nblk - 1, -1, -1) if reverse else range(nblk)
    e_row = 0 if reverse else CB - 1
    m_row = CB - 1 - CB // 2 if reverse else CB // 2
    for jb in order:
        rs = slice(jb * CB, (jb + 1) * CB)
        blj = bl[rs]
        tau = blj[e_row:e_row + 1]
        mid = blj[m_row:m_row + 1]
        qj = q[rs]
        kj = k[rs]
        qd0 = (qj * jnp.exp(blj - mid)).astype(BF16)
        kd0 = (kj * jnp.exp(mid - blj)).astype(BF16)
        qs = (qj * jnp.exp(blj)).astype(BF16)
        ke = (kj * jnp.exp(tau - blj)).astype(BF16)
        dec = jnp.exp(tau)
        for hh in range(H):
            hs = slice(hh * DK, (hh + 1) * DK)
            sc = lax.dot_general(qd0[:, hs], kd0[:, hs], nt, preferred_element_type=F32)
            sc = jnp.where(mask, sc, 0.0).astype(BF16)
            st = st_ref[hh]
            o_h = (lax.dot_general(qs[:, hs], st.astype(BF16), nt, preferred_element_type=F32)
                   + jnp.dot(sc, vb[rs, hs], preferred_element_type=F32))
            st_ref[hh] = st * dec[:, hs] + lax.dot_general(vb[rs, hs], ke[:, hs], tn,
                                                           preferred_element_type=F32)
            if readout:
                o_h = o_h + of_ref[rs, hs]
                o_h = _rms(o_h) * ng_ref[...]
                gh = g_ref[rs, hs].astype(F32)
                o_h = o_h * (gh * jax.nn.sigmoid(gh))
            o_ref[rs, hs] = o_h


def _hgrn_scan(pz, p, lb_row, s0, reverse, o_f=None, norm_g=None):
    nch = L // TH
    if reverse:
        row = lambda b, c: b * nch + (nch - 1 - c)
    else:
        row = lambda b, c: b * nch + c
    col_spec = lambda j: pl.BlockSpec((TH, KD), lambda b, c: (row(b, c), j))
    in_specs = [col_spec(0), col_spec(1 if reverse else 0), col_spec(1),
                pl.BlockSpec((1, KD), lambda b, c: (0, 0)),
                pl.BlockSpec((None, H, DV, DK), lambda b, c: (b, 0, 0, 0))]
    args = [p, pz, p, lb_row, s0]
    readout = o_f is not None
    if readout:
        in_specs += [pl.BlockSpec((TH, KD), lambda b, c: (row(b, c), 0)), col_spec(2),
                     pl.BlockSpec((1, DV), lambda b, c: (0, 0))]
        args += [o_f, p, norm_g]
    return pl.pallas_call(
        functools.partial(_hgrn_kernel, reverse, readout),
        grid=(B, nch),
        in_specs=in_specs,
        out_specs=pl.BlockSpec((TH, KD), lambda b, c: (row(b, c), 0)),
        out_shape=jax.ShapeDtypeStruct((T, KD), F32),
        scratch_shapes=[pltpu.VMEM((H, DV, DK), F32)],
        compiler_params=_cparams(2, VMEM_LIMIT),
        name="hgrn_bwd_readout" if readout else "hgrn_fwd",
    )(*args)


def _hy_pre_kernel(v_ref, x1_ref, x0_ref, vp_ref, x1p_ref, x0p_ref, vn_ref, x1n_ref, x0n_ref,
                   w_ref, b_ref, vx_ref, x0o_ref):
    i = pl.program_id(1)
    first = i == 0
    last = i == pl.num_programs(1) - 1
    row = lax.broadcasted_iota(I32, (TM_HY, 1), 0)

    def conv(c_ref, p_ref, n_ref, col):
        x = c_ref[...].astype(F32)
        prev_row = jnp.where(first, 0.0, p_ref[...].astype(F32)[HALO - 1:HALO, :])
        next_row = jnp.where(last, 0.0, n_ref[...].astype(F32)[0:1, :])
        xm = jnp.where(row == 0, prev_row, pltpu.roll(x, 1, axis=0))
        xp = jnp.where(row == TM_HY - 1, next_row, pltpu.roll(x, TM_HY - 1, axis=0))
        cs = slice(col * HYW, (col + 1) * HYW)
        return xm * w_ref[0:1, cs] + x * w_ref[1:2, cs] + xp * w_ref[2:3, cs] + b_ref[:, cs]

    v = conv(v_ref, vp_ref, vn_ref, 0)
    x1 = conv(x1_ref, x1p_ref, x1n_ref, 1)
    x0 = conv(x0_ref, x0p_ref, x0n_ref, 2)
    vx_ref[...] = v * x1
    x0o_ref[...] = x0


def _hyena_pre(p, conv_w, conv_b):
    nt = L // TM_HY
    hb = TM_HY // HALO
    nhb = T // HALO
    cur = lambda col: pl.BlockSpec((TM_HY, HYW), lambda b, i: (b * nt + i, col))
    prv = lambda col: pl.BlockSpec((HALO, HYW), lambda b, i: (jnp.maximum((b * nt + i) * hb - 1, 0), col))
    nxt = lambda col: pl.BlockSpec((HALO, HYW), lambda b, i: (jnp.minimum((b * nt + i + 1) * hb, nhb - 1), col))
    c0 = 3
    out = jax.ShapeDtypeStruct((T, HYW), F32)
    return pl.pallas_call(
        _hy_pre_kernel,
        grid=(B, nt),
        in_specs=[cur(c0), cur(c0 + 1), cur(c0 + 2), prv(c0), prv(c0 + 1), prv(c0 + 2),
                  nxt(c0), nxt(c0 + 1), nxt(c0 + 2),
                  pl.BlockSpec((3, 3 * HYW), lambda b, i: (0, 0)),
                  pl.BlockSpec((1, 3 * HYW), lambda b, i: (0, 0))],
        out_specs=(pl.BlockSpec((TM_HY, HYW), lambda b, i: (b * nt + i, 0)),
                   pl.BlockSpec((TM_HY, HYW), lambda b, i: (b * nt + i, 0))),
        out_shape=(out, out),
        compiler_params=_cparams(2, VMEM_LIMIT),
        name="hyena_pre",
    )(p, p, p, p, p, p, p, p, p, conv_w, conv_b)


def _filt_kernel(z_ref, w1_ref, b1_ref, f1_ref, w2_ref, b2_ref, f2_ref, w3a_ref, w3b_ref, dl_ref, o_ref):
    half = z_ref.shape[0] // 2
    zt = z_ref[0:half, :]
    zb = z_ref[half:, :]
    lane = lax.broadcasted_iota(I32, zt.shape, 1)
    dot = functools.partial(jnp.dot, preferred_element_type=F32, precision=HIGHEST)
    h = jnp.sin(f1_ref[...] * (dot(jnp.concatenate([zt, zb], axis=1), w1_ref[...]) + b1_ref[...]))
    h = jnp.sin(f2_ref[...] * (dot(h, w2_ref[...]) + b2_ref[...]))
    for zz, w3_ref, rows in ((zt, w3a_ref, slice(0, half)), (zb, w3b_ref, slice(half, 2 * half))):
        taps = dot(h, w3_ref[...])
        win = jnp.exp(-_lane_pick(zz, lane, 0) * dl_ref[...])
        o_ref[0, rows, :] = taps[:, :HYW] * win
        o_ref[1, rows, :] = taps[:, HYW:] * win * _lane_pick(zz, lane, HY_EMB)


def _filter_taps(zin, w1, b1, f1, w2, b2, f2, w3a, w3b, deltas):
    tm = 2048
    ln = V7X_LANES
    full = lambda shape: pl.BlockSpec(shape, lambda i: (0, 0))
    return pl.pallas_call(
        _filt_kernel,
        grid=(L // tm,),
        in_specs=[pl.BlockSpec((tm, ln), lambda i: (i, 0)),
                  full((2 * ln, ln)), full((1, ln)), full((1, ln)),
                  full((ln, ln)), full((1, ln)), full((1, ln)),
                  full((ln, 2 * HYW)), full((ln, 2 * HYW)), full((1, HYW))],
        out_specs=pl.BlockSpec((2, tm, HYW), lambda i: (0, i, 0)),
        out_shape=jax.ShapeDtypeStruct((2, L, HYW), F32),
        compiler_params=_cparams(1, VMEM_LIMIT),
        name="hyena_filter",
    )(zin, w1, b1, f1, w2, b2, f2, w3a, w3b, deltas)


def _strided_dft_kernel(x_hbm, f_ref, o_hbm, xbuf, obuf, sem_in, sem_out):
    g = pl.program_id(0)
    ng = pl.num_programs(0)
    nb = FFT_P // FFT_BB

    def copies(grp, slot, inbound):
        n = grp // nb
        b0 = (grp % nb) * FFT_BB
        if inbound:
            return [pltpu.make_async_copy(x_hbm.at[n, :, b0 + jj, :], xbuf.at[slot, jj], sem_in.at[slot])
                    for jj in range(FFT_BB)]
        return [pltpu.make_async_copy(obuf.at[slot, jj], o_hbm.at[n, :, b0 + jj, :], sem_out.at[slot])
                for jj in range(FFT_BB)]

    def start(grp, slot, inbound):
        for cp in copies(grp, slot, inbound):
            cp.start()

    def wait(grp, slot, inbound):
        for cp in copies(grp, slot, inbound):
            cp.wait()

    @pl.when(g == 0)
    def _():
        start(0, 0, True)

    for slot in range(2):
        grp = 2 * g + slot
        if slot == 0:
            start(grp + 1, 1, True)
        else:
            @pl.when(g + 1 < ng)
            def _():
                start(grp + 1, 0, True)
        wait(grp, slot, True)

        @pl.when(g > 0)
        def _():
            wait(grp - 2, slot, False)

        for jj in range(FFT_BB):
            obuf[slot, jj] = jnp.dot(f_ref[...], xbuf[slot, jj].astype(BF16), preferred_element_type=F32)
        start(grp, slot, False)

    @pl.when(g + 1 == ng)
    def _():
        wait(2 * g, 0, False)
        wait(2 * g + 1, 1, False)


def _strided_dft(xv, fmat, name):
    n, kk = xv.shape[0], xv.shape[1]
    mm = fmat.shape[0]
    groups = n * (FFT_P // FFT_BB)
    return pl.pallas_call(
        _strided_dft_kernel,
        grid=(groups // 2,),
        in_specs=[pl.BlockSpec(memory_space=pl.ANY),
                  pl.BlockSpec((mm, kk), lambda g: (0, 0))],
        out_specs=pl.BlockSpec(memory_space=pl.ANY),
        out_shape=jax.ShapeDtypeStruct((n, mm, FFT_P, HYW), F32),
        scratch_shapes=[pltpu.VMEM((2, FFT_BB, kk, HYW), F32), pltpu.VMEM((2, FFT_BB, mm, HYW), F32),
                        pltpu.SemaphoreType.DMA((2,)), pltpu.SemaphoreType.DMA((2,))],
        compiler_params=_cparams(1, VMEM_LIMIT),
        name=name,
    )(xv, fmat)


def _cblock(mr, mi):
    return jnp.concatenate([jnp.concatenate([mr, -mi], axis=1), jnp.concatenate([mi, mr], axis=1)], axis=0)


FFT_H = FFT_P // 2


def _twiddled(fr_ref, fi_ref, tw):
    twr = tw[0:1, :]
    twi = tw[1:2, :]
    fr = fr_ref[...]
    fi = fi_ref[...]
    return fr * twr - fi * twi, fr * twi + fi * twr


def _cmul_rows(x, kr, ki):
    xr, xi = x[:FFT_P], x[FFT_P:]
    return jnp.concatenate([xr * kr - xi * ki, xr * ki + xi * kr], axis=0).astype(BF16)


_TN_DIMS = (((0,), (0,)), ((), ()))
_STAGE2_SPECS = [pl.BlockSpec((FFT_P, FFT_P), lambda d: (0, 0)),
                 pl.BlockSpec((FFT_P, FFT_P), lambda d: (0, 0)),
                 pl.BlockSpec((None, 2, FFT_P), lambda d: (d, 0, 0)),
                 pl.BlockSpec((2, FFT_P), lambda d: (0, 0))]


def _kspec_kernel(a_ref, fr_ref, fi_ref, tw_ref, twh_ref, o_ref, oh_ref):
    dd = pl.program_id(0)
    dot = functools.partial(jnp.dot, preferred_element_type=F32)

    def combine(xf, xb, out_ref):
        out_ref[0] = xf[:FFT_P] + xb[:FFT_P]
        out_ref[1] = xf[FFT_P:] - xb[FFT_P:]

    @pl.when(dd > 0)
    def _():
        rm = _cblock(*_twiddled(fr_ref, fi_ref, tw_ref[...])).astype(BF16)
        combine(dot(rm, a_ref[0].reshape(2 * FFT_P, HYW).astype(BF16)),
                dot(rm, a_ref[1].reshape(2 * FFT_P, HYW).astype(BF16)), o_ref)

    @pl.when(dd == 0)
    def _():
        for slot, tw, out_ref in ((0, tw_ref[...], o_ref), (1, twh_ref[...], oh_ref)):
            w = jnp.concatenate(_twiddled(fr_ref, fi_ref, tw), axis=0).astype(BF16)
            combine(dot(w, a_ref[0, slot].astype(BF16)), dot(w, a_ref[1, slot].astype(BF16)), out_ref)


def _kernel_spectrum(ak, fr, fi, tw, twh):
    return pl.pallas_call(
        _kspec_kernel,
        grid=(FFT_H,),
        in_specs=[pl.BlockSpec((2, None, 2, FFT_P, HYW), lambda d: (0, d, 0, 0, 0))] + _STAGE2_SPECS,
        out_specs=(pl.BlockSpec((2, None, FFT_P, HYW), lambda d: (0, d, 0, 0)),
                   pl.BlockSpec((2, FFT_P, HYW), lambda d: (0, 0, 0))),
        out_shape=(jax.ShapeDtypeStruct((2, FFT_H, FFT_P, HYW), F32),
                   jax.ShapeDtypeStruct((2, FFT_P, HYW), F32)),
        compiler_params=_cparams(1, VMEM_LIMIT),
        name="kernel_spectrum",
    )(ak, fr, fi, tw, twh)


def _mid_kernel(a_ref, kh_ref, khh_ref, fr_ref, fi_ref, tw_ref, twh_ref, o_ref):
    dd = pl.program_id(0)
    dot = functools.partial(jnp.dot, preferred_element_type=F32)
    dot_t = lambda w, y: lax.dot_general(w, y, _TN_DIMS, preferred_element_type=F32)

    @pl.when(dd > 0)
    def _():
        rm = _cblock(*_twiddled(fr_ref, fi_ref, tw_ref[...])).astype(BF16)
        for n in range(B):
            x = dot(rm, a_ref[n].reshape(2 * FFT_P, HYW).astype(BF16))
            y = _cmul_rows(x, kh_ref[0], kh_ref[1])
            o_ref[n] = dot_t(rm, y).reshape(2, FFT_P, HYW)

    @pl.when(dd == 0)
    def _():
        for slot, tw, k_ref in ((0, tw_ref[...], kh_ref), (1, twh_ref[...], khh_ref)):
            w = jnp.concatenate(_twiddled(fr_ref, fi_ref, tw), axis=0).astype(BF16)
            for n in range(B):
                y = _cmul_rows(dot(w, a_ref[n, slot].astype(BF16)), k_ref[0], k_ref[1])
                o_ref[n, slot] = dot_t(w, y)


def _fft_mid(au, kh, khh, fr, fi, tw, twh):
    pair = pl.BlockSpec((B, None, 2, FFT_P, HYW), lambda d: (0, d, 0, 0, 0))
    return pl.pallas_call(
        _mid_kernel,
        grid=(FFT_H,),
        in_specs=[pair,
                  pl.BlockSpec((2, None, FFT_P, HYW), lambda d: (0, d, 0, 0)),
                  pl.BlockSpec((2, FFT_P, HYW), lambda d: (0, 0, 0))] + _STAGE2_SPECS,
        out_specs=pair,
        out_shape=jax.ShapeDtypeStruct((B, FFT_H, 2, FFT_P, HYW), F32),
        compiler_params=_cparams(1, VMEM_LIMIT),
        name="fft_mid",
    )(au, kh, khh, fr, fi, tw, twh)


def _dft_tables():
    na = L // FFT_P
    a = np.arange(na)
    dd = np.arange(FFT_H)
    ang = 2.0 * np.pi * np.outer(dd, a) / FFT_P
    re_rows = np.cos(ang)
    im_rows = -np.sin(ang)
    im_rows[0] = np.cos(np.pi * a)
    f_first = np.stack([re_rows, im_rows], axis=1).reshape(FFT_P, na)
    gre = 2.0 * np.cos(ang)
    gim = -2.0 * np.sin(ang)
    gre[0] = 1.0
    gim[0] = np.cos(np.pi * a)
    g_last = np.stack([gre, gim], axis=1).reshape(FFT_P, na).T / FFT_N
    b = np.arange(FFT_P)
    angf = 2.0 * np.pi * np.outer(b, b) / FFT_P
    ang2 = 2.0 * np.pi * np.outer(np.arange(FFT_H + 1), b) / FFT_N
    tw = np.stack([np.cos(ang2), -np.sin(ang2)], axis=1)
    f32 = lambda x: jnp.asarray(x.astype(np.float32))
    return (f32(f_first).astype(BF16), f32(g_last).astype(BF16), f32(np.cos(angf)), f32(-np.sin(angf)),
            f32(tw[:FFT_H]), f32(tw[FFT_H]))


def _merge_kernel(ya_ref, cv_ref, vx_ref, x0_ref, ga_ref, gb_ref, x_ref, pos_ref, gt1_ref, hyd_ref,
                  wpa_ref, wpb_ref, wo_ref, g2_ref, sh2_ref, sc2_ref, wr_ref, br_ref,
                  x1_ref, t2_ref, lg_ref):
    vx = vx_ref[...]
    yb = x0_ref[...] * (cv_ref[...] + vx * hyd_ref[...])
    pa = jnp.dot(ya_ref[...].astype(BF16), wpa_ref[...], preferred_element_type=F32)
    pb = jnp.dot(yb.astype(BF16), wpb_ref[...], preferred_element_type=F32)
    mixed = (jax.nn.sigmoid(ga_ref[...].astype(F32)) * pa
             + jax.nn.sigmoid(gb_ref[...].astype(F32)) * pb)
    xm = jnp.dot(mixed.astype(BF16), wo_ref[...], preferred_element_type=F32)
    x1 = x_ref[...] + pos_ref[...] + gt1_ref[...] * xm
    x1_ref[...] = x1
    t2 = _rms(x1) * g2_ref[...]
    t2 = t2 * (1.0 + sc2_ref[...]) + sh2_ref[...]
    t2_ref[...] = t2
    lg_ref[...] = jnp.dot(t2, wr_ref[...], preferred_element_type=F32, precision=HIGHEST) + br_ref[...]


def _merge(ya, cv, vx, x0c, p, x2, pos, gt1, hyd, wpa, wpb, wo, g2, sh2, sc2, wr, br):
    tpb = L // TM_MG
    half = lambda: pl.BlockSpec((TM_MG, HYW), lambda i: (i, 0))
    full = lambda shape: pl.BlockSpec(shape, lambda i: tuple(0 for _ in shape))
    perb = lambda: pl.BlockSpec((None, 1, D), lambda i: (i // tpb, 0, 0))
    return pl.pallas_call(
        _merge_kernel,
        grid=(T // TM_MG,),
        in_specs=[half(), half(), half(), half(),
                  pl.BlockSpec((TM_MG, D), lambda i: (i, 3)),
                  pl.BlockSpec((TM_MG, D), lambda i: (i, 4)),
                  pl.BlockSpec((TM_MG, D), lambda i: (i, 0)),
                  pl.BlockSpec((TM_MG, D), lambda i: (i % tpb, 0)),
                  perb(), full((1, HYW)),
                  full((KD, D)), full((HYW, D)), full((D, D)),
                  full((1, D)), perb(), perb(),
                  full((D, V7X_LANES)), full((1, V7X_LANES))],
        out_specs=(pl.BlockSpec((TM_MG, D), lambda i: (i, 0)),
                   pl.BlockSpec((TM_MG, D), lambda i: (i, 0)),
                   pl.BlockSpec((TM_MG, V7X_LANES), lambda i: (i, 0))),
        out_shape=(jax.ShapeDtypeStruct((T, D), F32), jax.ShapeDtypeStruct((T, D), F32),
                   jax.ShapeDtypeStruct((T, V7X_LANES), F32)),
        compiler_params=_cparams(1, VMEM_LIMIT),
        name="merge",
    )(ya, cv, vx, x0c, p, p, x2, pos, gt1, hyd, wpa, wpb, wo, g2, sh2, sc2, wr, br)


def _route_kernel(lg_ref, info_ref, cnt_ref):
    @pl.when(pl.program_id(0) == 0)
    def _():
        cnt_ref[...] = jnp.zeros_like(cnt_ref)

    lg = lg_ref[...]
    lane = lax.broadcasted_iota(I32, lg.shape, 1)
    lanef = lane.astype(F32)
    neg = -1e30
    big = 1e9
    is_g = (lane >= NEXP) & (lane < NEXP + NGRP)
    gl = jnp.where(is_g, lg, neg)
    ge = jnp.where(is_g, jnp.exp(gl - jnp.max(gl, axis=-1, keepdims=True)), 0.0)
    pg = ge / jnp.sum(ge, axis=-1, keepdims=True)
    p_top_g = jnp.max(pg, axis=-1, keepdims=True)
    gidx = jnp.min(jnp.where(is_g & (pg == p_top_g), lanef, big), axis=-1, keepdims=True)
    g_sel = gidx.astype(I32) - NEXP
    emask = (lane < NEXP) & (jnp.right_shift(lane, NEPG.bit_length() - 1) == g_sel)
    el = jnp.where(emask, lg, neg)
    ee = jnp.where(emask, jnp.exp(el - jnp.max(el, axis=-1, keepdims=True)), 0.0)
    pe = ee / jnp.sum(ee, axis=-1, keepdims=True)
    p1 = jnp.max(jnp.where(emask, pe, -1.0), axis=-1, keepdims=True)
    i1 = jnp.min(jnp.where(emask & (pe == p1), lanef, big), axis=-1, keepdims=True)
    rest = emask & (lanef != i1)
    p2 = jnp.max(jnp.where(rest, pe, -1.0), axis=-1, keepdims=True)
    i2 = jnp.min(jnp.where(rest & (pe == p2), lanef, big), axis=-1, keepdims=True)
    wsum = p1 + p2
    w1 = p_top_g * p1 / wsum
    w2 = p_top_g * p2 / wsum
    sel1 = lanef == i1
    sel2 = lanef == i2
    oh = jnp.where(sel1 | sel2, 1.0, 0.0)
    r = lax.broadcasted_iota(I32, (TR, TR), 0)
    c = lax.broadcasted_iota(I32, (TR, TR), 1)
    stril = jnp.where(c < r, 1.0, 0.0).astype(BF16)
    before = jnp.dot(stril, oh.astype(BF16), preferred_element_type=F32) + cnt_ref[...]
    r1 = jnp.sum(jnp.where(sel1, before, 0.0), axis=-1, keepdims=True)
    r2 = jnp.sum(jnp.where(sel2, before, 0.0), axis=-1, keepdims=True)
    cnt_ref[...] += jnp.sum(oh, axis=0, keepdims=True)
    info = jnp.where(lane == 0, i1, jnp.where(lane == 1, r1, jnp.where(lane == 2, i2, jnp.where(
        lane == 3, r2, jnp.where(lane == 4, w1, jnp.where(lane == 5, w2, 0.0))))))
    info_ref[...] = info


def _route(lg):
    return pl.pallas_call(
        _route_kernel,
        grid=(T // TR,),
        in_specs=[pl.BlockSpec((TR, V7X_LANES), lambda i: (i, 0))],
        out_specs=(pl.BlockSpec((TR, V7X_LANES), lambda i: (i, 0)),
                   pl.BlockSpec((1, V7X_LANES), lambda i: (0, 0))),
        out_shape=(jax.ShapeDtypeStruct((T, V7X_LANES), F32), jax.ShapeDtypeStruct((1, V7X_LANES), F32)),
        compiler_params=_cparams(1, VMEM_LIMIT),
        name="route",
    )(lg)


def _scatter_kernel(er_ref, zt_ref, t2_ref, zeros_hbm, xs_hbm, sem, zsem):
    i = pl.program_id(0)

    def zcopy(row):
        start = pl.multiple_of(jnp.maximum(row, 0), TE)
        return pltpu.make_async_copy(zeros_hbm, xs_hbm.at[pl.ds(start, TE)], zsem)

    @pl.when(i == 0)
    def _():
        def ztail(start, e, carry):
            @pl.when(zt_ref[0, e] >= 0)
            def _():
                cp = zcopy(zt_ref[0, e])
                cp.start() if start else cp.wait()
            return carry

        lax.fori_loop(0, NEXP, functools.partial(ztail, True), 0)
        lax.fori_loop(0, NEXP, functools.partial(ztail, False), 0)

        def zrest(start, tile, carry):
            cp = zcopy(tile * TE)
            cp.start() if start else cp.wait()
            return carry

        lax.fori_loop(zt_ref[0, NA_OFF], NT_EXP, functools.partial(zrest, True), 0)
        lax.fori_loop(zt_ref[0, NA_OFF], NT_EXP, functools.partial(zrest, False), 0)

    def issue(j, carry):
        for kk in range(2):
            dst = zt_ref[0, ST_OFF + er_ref[0, 4 * j + 2 * kk]] + er_ref[0, 4 * j + 2 * kk + 1]
            pltpu.make_async_copy(t2_ref.at[pl.ds(j, 1)], xs_hbm.at[pl.ds(dst, 1)], sem).start(priority=kk)
        return carry

    lax.fori_loop(0, TS, issue, 0, unroll=ROW_UNROLL)

    def drain(j, carry):
        for kk in range(2):
            pltpu.make_async_copy(t2_ref.at[pl.ds(0, 1)], xs_hbm.at[pl.ds(0, 1)], sem).wait()
        return carry

    lax.fori_loop(0, TS, drain, 0, unroll=ROW_UNROLL)


def _scatter_rows(er3, meta, t2, zeros_tile):
    return pl.pallas_call(
        _scatter_kernel,
        grid=(T // TS,),
        in_specs=[pl.BlockSpec((None, 1, 4 * TS), lambda i: (i, 0, 0), memory_space=pltpu.SMEM),
                  pl.BlockSpec(memory_space=pltpu.SMEM),
                  pl.BlockSpec((TS, D), lambda i: (i, 0)),
                  pl.BlockSpec(memory_space=pl.ANY)],
        out_specs=pl.BlockSpec(memory_space=pl.ANY),
        out_shape=jax.ShapeDtypeStruct((NP_ROWS, D), F32),
        scratch_shapes=[pltpu.SemaphoreType.DMA(()), pltpu.SemaphoreType.DMA(())],
        compiler_params=_cparams(1, VMEM_LIMIT),
        name="scatter_rows",
    )(er3, meta, t2, zeros_tile)


def _expert_kernel(te_ref, na_ref, xs_ref, wg_ref, wu_ref, wd_ref, ys_ref):
    @pl.when(pl.program_id(0) < na_ref[0])
    def _():
        x = xs_ref[...].astype(BF16)
        g = jnp.dot(x, wg_ref[...].astype(BF16), preferred_element_type=F32)
        u = jnp.dot(x, wu_ref[...].astype(BF16), preferred_element_type=F32)
        hid = (g * jax.nn.sigmoid(g) * u).astype(BF16)
        ys_ref[...] = jnp.dot(hid, wd_ref[...].astype(BF16), preferred_element_type=F32)

    @pl.when(pl.program_id(0) >= na_ref[0])
    def _():
        ys_ref[...] = jnp.zeros_like(ys_ref)


def _experts(tile_expert, n_active, xs, wg, wu, wd):
    rows = lambda i, te, na: (i, 0)
    wsel = lambda i, te, na: (te[i], 0, 0)
    grid_spec = pltpu.PrefetchScalarGridSpec(
        num_scalar_prefetch=2,
        grid=(NT_EXP,),
        in_specs=[pl.BlockSpec((TE, D), rows),
                  pl.BlockSpec((None, D, DEXP), wsel),
                  pl.BlockSpec((None, D, DEXP), wsel),
                  pl.BlockSpec((None, DEXP, D), wsel)],
        out_specs=pl.BlockSpec((TE, D), rows),
    )
    return pl.pallas_call(
        _expert_kernel,
        grid_spec=grid_spec,
        out_shape=jax.ShapeDtypeStruct((NP_ROWS, D), F32),
        compiler_params=_cparams(1, VMEM_LIMIT),
        name="experts",
    )(tile_expert, n_active, xs, wg, wu, wd)


def _combine_kernel(er_ref, zt_ref, info_ref, x1_ref, gt2_ref, fg_ref, ys_hbm, o_ref, buf, sem):
    def issue(j, carry):
        for kk in range(2):
            src = zt_ref[0, ST_OFF + er_ref[0, 4 * j + 2 * kk]] + er_ref[0, 4 * j + 2 * kk + 1]
            pltpu.make_async_copy(ys_hbm.at[pl.ds(src, 1)], buf.at[kk, pl.ds(j, 1)], sem).start(priority=kk)
        return carry

    lax.fori_loop(0, TC, issue, 0, unroll=ROW_UNROLL)

    def drain(j, carry):
        for kk in range(2):
            pltpu.make_async_copy(ys_hbm.at[pl.ds(0, 1)], buf.at[kk, pl.ds(0, 1)], sem).wait()
        return carry

    lax.fori_loop(0, TC, drain, 0, unroll=ROW_UNROLL)

    info = info_ref[...]
    lane = lax.broadcasted_iota(I32, info.shape, 1)
    w1 = _lane_pick(info, lane, 4)
    w2 = _lane_pick(info, lane, 5)
    moe = w1 * buf[0] + w2 * buf[1]
    x2 = x1_ref[...] + gt2_ref[...] * moe
    o_ref[...] = _rms(x2) * fg_ref[...]


def _combine(er3, meta, info, x1, gt2, fg, ys):
    tpb = L // TC
    return pl.pallas_call(
        _combine_kernel,
        grid=(T // TC,),
        in_specs=[pl.BlockSpec((None, 1, 4 * TC), lambda i: (i, 0, 0), memory_space=pltpu.SMEM),
                  pl.BlockSpec(memory_space=pltpu.SMEM),
                  pl.BlockSpec((TC, V7X_LANES), lambda i: (i, 0)),
                  pl.BlockSpec((TC, D), lambda i: (i, 0)),
                  pl.BlockSpec((None, 1, D), lambda i: (i // tpb, 0, 0)),
                  pl.BlockSpec((1, D), lambda i: (0, 0)),
                  pl.BlockSpec(memory_space=pl.ANY)],
        out_specs=pl.BlockSpec((TC, D), lambda i: (i, 0)),
        out_shape=jax.ShapeDtypeStruct((T, D), F32),
        scratch_shapes=[pltpu.VMEM((2, TC, D), F32), pltpu.SemaphoreType.DMA(())],
        compiler_params=_cparams(1, VMEM_LIMIT),
        name="combine",
    )(er3, meta, info, x1, gt2, fg, ys)


def _pos_emb():
    rows = L // GRID_W
    quarter = D // 4
    omega = 1.0 / (10000.0 ** (jnp.arange(quarter, dtype=F32) / quarter))

    def axis_emb(pos):
        a = pos[:, None] * omega[None, :]
        return jnp.concatenate([jnp.sin(a), jnp.cos(a)], axis=-1)

    er = axis_emb(jnp.arange(rows, dtype=F32))
    ec = axis_emb(jnp.arange(GRID_W, dtype=F32))
    return jnp.concatenate([jnp.repeat(er, GRID_W, axis=0), jnp.tile(ec, (rows, 1))], axis=-1)


def _filter_features():
    t = jnp.linspace(0.0, 1.0, L, dtype=F32)[:, None]
    bands = (HY_EMB - 1) // 2
    ang = (2.0 * math.pi * jnp.arange(L, dtype=F32) / L)[:, None] * \
        jnp.linspace(1e-4, bands - 1, bands, dtype=F32)[None, :]
    not_tap0 = (jnp.arange(L) >= 1).astype(F32)[:, None]
    pad = jnp.zeros((L, V7X_LANES - HY_EMB - 1), F32)
    return jnp.concatenate([t, jnp.cos(ang), -jnp.sin(ang), not_tap0, pad], axis=-1)


def _pad2(a, rows, cols):
    return jnp.pad(a, ((0, rows - a.shape[0]), (0, cols - a.shape[1])))


def kernel(x, c, ctx, c_ctx, ada_w, ada_b, norm1_g, norm2_g, w_in, hgrn_lb, hgrn_norm_g, hy_conv_w, hy_conv_b, hy_filt_w1, hy_filt_b1, hy_filt_freq1, hy_filt_w2, hy_filt_b2, hy_filt_freq2, hy_filt_w3, hy_d, w_proj_a, w_proj_b, w_out, moe_router_g_w, moe_router_g_b, moe_router_e_w, moe_router_e_b, moe_w_gate, moe_w_up, moe_w_down, final_norm_g):
    cvec = jnp.zeros((8, D), F32).at[0:B].set(c).at[B].set(c_ctx)
    mod = _adaln(cvec, ada_w[0], ada_b[0][None, :])
    m6 = mod.reshape(8, 6, D)
    sh1, sc1, gt1, sh2, sc2, gt2 = [m6[0:B, k][:, None, :] for k in range(6)]
    csh1, csc1 = m6[B:B + 1, 0], m6[B:B + 1, 1]

    lbs = jnp.cumsum(jax.nn.softmax(hgrn_lb.astype(F32), axis=0), axis=0)[0]
    g1 = norm1_g[0][None, :]
    pos = _pos_emb()
    x2 = x.reshape(T, D)

    w_ctx = w_in[0][:, KD:4 * KD].astype(BF16)
    s_f, s_b = _context_states(ctx, g1, csh1, csc1, w_ctx, lbs)

    w0 = w_in[0]
    w_perm = jnp.concatenate([w0[:, KD:3 * KD], w0[:, :KD], w0[:, 3 * KD:]], axis=1).astype(BF16)
    pz, p = _in_projection(x2, pos, g1, sh1, sc1, w_perm)

    o_f = _hgrn_scan(pz, p, lbs[0:1], s_f, False)
    y_a = _hgrn_scan(pz, p, lbs[1:2], s_b, True, o_f=o_f, norm_g=hgrn_norm_g[0][None, :])

    vx, x0c = _hyena_pre(p, hy_conv_w[0], hy_conv_b[0][None, :])
    deltas = jnp.abs(jnp.linspace(math.log(HY_DECAY_TARGET) / HY_SLOW_PCT,
                                  math.log(HY_DECAY_TARGET) / HY_FAST_PCT, HYW, dtype=F32))[None, :]
    ln = V7X_LANES
    fh = hy_filt_w2.shape[-1]
    blockdiag = lambda m: jnp.concatenate([_pad2(m, m.shape[0], 2 * m.shape[1]),
                                           jnp.pad(m, ((0, 0), (m.shape[1], 0)))], axis=0)
    twice = lambda v: jnp.concatenate([v, v])[None, :]
    w3 = hy_filt_w3[0]
    taps = _filter_taps(
        _filter_features(),
        blockdiag(_pad2(hy_filt_w1[0], ln, fh)), twice(hy_filt_b1[0]), twice(hy_filt_freq1[0]),
        blockdiag(hy_filt_w2[0]), twice(hy_filt_b2[0]), twice(hy_filt_freq2[0]),
        _pad2(w3, ln, 2 * HYW), jnp.pad(w3, ((fh, 0), (0, 0))), deltas)
    f_first, g_last, fr, fi, tw, twh = _dft_tables()
    na = L // FFT_P
    ak = _strided_dft(taps.reshape(2, na, FFT_P, HYW), f_first, "dft_first_taps")
    kh, khh = _kernel_spectrum(ak.reshape(2, FFT_H, 2, FFT_P, HYW), fr, fi, tw, twh)
    au = _strided_dft(vx.reshape(B, na, FFT_P, HYW), f_first, "dft_first")
    bp = _fft_mid(au.reshape(B, FFT_H, 2, FFT_P, HYW), kh, khh, fr, fi, tw, twh)
    conv = _strided_dft(bp.reshape(B, FFT_P, FFT_P, HYW), g_last, "dft_last").reshape(T, HYW)

    wr = jnp.concatenate([jnp.transpose(moe_router_e_w[0], (1, 0, 2)).reshape(D, NEXP),
                          moe_router_g_w[0], jnp.zeros((D, V7X_LANES - NEXP - NGRP), F32)], axis=1)
    br = jnp.concatenate([moe_router_e_b[0].reshape(NEXP), moe_router_g_b[0],
                          jnp.zeros((V7X_LANES - NEXP - NGRP,), F32)])[None, :]
    x1, t2, lg = _merge(y_a, conv, vx, x0c, p, x2, pos, gt1, hy_d[0][None, :],
                        w_proj_a[0].astype(BF16), w_proj_b[0].astype(BF16), w_out[0].astype(BF16),
                        norm2_g[0][None, :], sh2, sc2, wr, br)

    info, counts = _route(lg)
    cnt = counts[0, :NEXP].astype(I32)
    pc = ((cnt + TE - 1) // TE) * TE
    ends = jnp.cumsum(pc)
    starts = ends - pc
    n_active = (ends[-1] // TE).astype(I32)[None]
    tile_rows = jnp.arange(NT_EXP, dtype=I32) * TE
    tile_expert = jnp.minimum(jnp.sum((ends[None, :] <= tile_rows[:, None]).astype(I32), axis=1), NEXP - 1)
    meta = jnp.concatenate([jnp.where(pc > 0, ends - TE, -1), starts, n_active]).astype(I32)[None, :]
    er = info[:, :4].astype(I32)

    xs = _scatter_rows(er.reshape(T // TS, 1, 4 * TS), meta, t2, jnp.zeros((TE, D), F32))
    ys = _experts(tile_expert, n_active, xs,
                  moe_w_gate[0].reshape(NEXP, D, DEXP), moe_w_up[0].reshape(NEXP, D, DEXP),
                  moe_w_down[0].reshape(NEXP, DEXP, D))
    out = _combine(er.reshape(T // TC, 1, 4 * TC), meta, info, x1, gt2, final_norm_g[None, :], ys)
    return out.reshape(B, L, D)
```

```python
import functools
import math

import numpy as np
import jax
import jax.numpy as jnp
from jax import lax
from jax.experimental import pallas as pl
from jax.experimental.pallas import tpu as pltpu

F32 = jnp.float32
BF16 = jnp.bfloat16
I32 = jnp.int32
HIGHEST = lax.Precision.HIGHEST

D = 1024
B = 2
L = 8192
T = B * L
CTX = 256
GRID_W = 64
EPS = 1e-6
H = 4
DK = 128
DV = 128
KD = H * DK
IN_W = 6144
HYW = 512
HY_EMB = 33
NGRP = 4
NEPG = 8
NEXP = NGRP * NEPG
DEXP = 512
HY_DECAY_TARGET = 1e-2
HY_FAST_PCT = 0.3
HY_SLOW_PCT = 1.5

V7X_LANES = 128
V7X_SUBLANES = 8
V7X_VMEM_BYTES = 64 * 1024 * 1024
VMEM_LIMIT = 48 * 1024 * 1024

FFT_N = 2 * L
FFT_P = 128
FFT_BB = 8

TM_IN = 1024
TN_IN = 1024
TH = 128
CB = 32
TM_HY = 1024
HALO = 2 * V7X_SUBLANES
TM_MG = 256
TR = 512
TE = 256
NP_ROWS = 2 * T + NEXP * TE
NT_EXP = NP_ROWS // TE
TS = 512
TC = 256
ROW_UNROLL = 8
ST_OFF = NEXP
NA_OFF = 2 * NEXP


def _cparams(n_axes, vmem=None):
    return pltpu.CompilerParams(dimension_semantics=("arbitrary",) * n_axes,
                                vmem_limit_bytes=vmem)


def _split3(x):
    hi = x.astype(BF16)
    r = x - hi.astype(F32)
    mid = r.astype(BF16)
    lo = (r - mid.astype(F32)).astype(BF16)
    return hi, mid, lo


def _dot01(m, x):
    hi, mid, lo = _split3(x)
    return (jnp.dot(m, hi, preferred_element_type=F32) + jnp.dot(m, mid, preferred_element_type=F32)
            + jnp.dot(m, lo, preferred_element_type=F32))


def _rms(x):
    return x * lax.rsqrt(jnp.mean(x * x, axis=-1, keepdims=True) + EPS)


def _lane_pick(x, lane, idx):
    return jnp.sum(jnp.where(lane == idx, x, 0.0), axis=-1, keepdims=True)


def _ada_kernel(c_ref, w_ref, b_ref, o_ref):
    c = c_ref[...]
    s = c * jax.nn.sigmoid(c)
    o_ref[...] = jnp.dot(s, w_ref[...], preferred_element_type=F32, precision=HIGHEST) + b_ref[...]


def _adaln(cvec, w, b):
    tn = 1536
    return pl.pallas_call(
        _ada_kernel,
        grid=(6 * D // tn,),
        in_specs=[pl.BlockSpec((8, D), lambda j: (0, 0)),
                  pl.BlockSpec((D, tn), lambda j: (0, j)),
                  pl.BlockSpec((1, tn), lambda j: (0, j))],
        out_specs=pl.BlockSpec((8, tn), lambda j: (0, j)),
        out_shape=jax.ShapeDtypeStruct((8, 6 * D), F32),
        compiler_params=_cparams(1, VMEM_LIMIT),
        name="adaln",
    )(cvec, w, b)


def _keys(z, lb):
    sig = jax.nn.sigmoid(z)
    logf = jnp.log(lb + (1.0 - lb) * sig)
    k = (1.0 - lb) * jax.nn.sigmoid(-z)
    return k, logf


def _ctx_kernel(ctx_ref, g_ref, sh_ref, sc_ref, w_ref, lb_ref, sf_ref, sb_ref):
    h = _rms(ctx_ref[...]) * g_ref[...]
    h = h * (1.0 + sc_ref[...]) + sh_ref[...]
    p = jnp.dot(h.astype(BF16), w_ref[...], preferred_element_type=F32)
    zf, zb, v = p[:, :KD], p[:, KD:2 * KD], p[:, 2 * KD:]
    kf, lf = _keys(zf, lb_ref[0:1, :])
    kb, lbk = _keys(zb, lb_ref[1:2, :])
    r = lax.broadcasted_iota(I32, (CTX, CTX), 0)
    c = lax.broadcasted_iota(I32, (CTX, CTX), 1)
    tril = jnp.where(c <= r, 1.0, 0.0).astype(BF16)
    cf = _dot01(tril, lf)
    cb = _dot01(tril, lbk)
    kfd = (kf * jnp.exp(cf[CTX - 1:CTX, :] - cf)).astype(BF16)
    kbd = (kb * jnp.exp(cb - lbk)).astype(BF16)
    vb = v.astype(BF16)
    tn = (((0,), (0,)), ((), ()))
    for hh in range(H):
        hs = slice(hh * DK, (hh + 1) * DK)
        sf_ref[hh] = lax.dot_general(vb[:, hs], kfd[:, hs], tn, preferred_element_type=F32)
        sb_ref[hh] = lax.dot_general(vb[:, hs], kbd[:, hs], tn, preferred_element_type=F32)


def _context_states(ctx, g1, csh1, csc1, w_ctx, lbs):
    st = jax.ShapeDtypeStruct((B, H, DV, DK), F32)
    return pl.pallas_call(
        _ctx_kernel,
        grid=(B,),
        in_specs=[pl.BlockSpec((None, CTX, D), lambda b: (b, 0, 0)),
                  pl.BlockSpec((1, D), lambda b: (0, 0)),
                  pl.BlockSpec((1, D), lambda b: (0, 0)),
                  pl.BlockSpec((1, D), lambda b: (0, 0)),
                  pl.BlockSpec((D, 3 * KD), lambda b: (0, 0)),
                  pl.BlockSpec((2, KD), lambda b: (0, 0))],
        out_specs=(pl.BlockSpec((None, H, DV, DK), lambda b: (b, 0, 0, 0)),
                   pl.BlockSpec((None, H, DV, DK), lambda b: (b, 0, 0, 0))),
        out_shape=(st, st),
        compiler_params=_cparams(1, VMEM_LIMIT),
        name="ctx_states",
    )(ctx, g1, csh1, csc1, w_ctx, lbs)


def _inproj_kernel(x_ref, pos_ref, g_ref, sh_ref, sc_ref, w_ref, oz_ref, o_ref, hx_ref):
    j = pl.program_id(1)

    @pl.when(j == 0)
    def _():
        h = _rms(x_ref[...] + pos_ref[...]) * g_ref[...]
        hx_ref[...] = (h * (1.0 + sc_ref[...]) + sh_ref[...]).astype(BF16)

    r = jnp.dot(hx_ref[...], w_ref[...], preferred_element_type=F32)

    @pl.when(j == 0)
    def _():
        oz_ref[...] = r

    @pl.when(j > 0)
    def _():
        o_ref[...] = r.astype(BF16)


def _in_projection(x2, pos, g1, sh1, sc1, w_bf):
    tiles_per_batch = L // TM_IN
    return pl.pallas_call(
        _inproj_kernel,
        grid=(T // TM_IN, IN_W // TN_IN),
        in_specs=[pl.BlockSpec((TM_IN, D), lambda i, j: (i, 0)),
                  pl.BlockSpec((TM_IN, D), lambda i, j: (i % tiles_per_batch, 0)),
                  pl.BlockSpec((1, D), lambda i, j: (0, 0)),
                  pl.BlockSpec((None, 1, D), lambda i, j: (i // tiles_per_batch, 0, 0)),
                  pl.BlockSpec((None, 1, D), lambda i, j: (i // tiles_per_batch, 0, 0)),
                  pl.BlockSpec((D, TN_IN), lambda i, j: (0, j))],
        out_specs=(pl.BlockSpec((TM_IN, TN_IN), lambda i, j: (i, 0)),
                   pl.BlockSpec((TM_IN, TN_IN), lambda i, j: (i, jnp.maximum(j - 1, 0)))),
        out_shape=(jax.ShapeDtypeStruct((T, 2 * KD), F32),
                   jax.ShapeDtypeStruct((T, IN_W - 2 * KD), BF16)),
        scratch_shapes=[pltpu.VMEM((TM_IN, D), BF16)],
        compiler_params=_cparams(2, VMEM_LIMIT),
        name="in_proj",
    )(x2, pos, g1, sh1, sc1, w_bf)


def _hgrn_kernel(reverse, readout, *refs):
    if readout:
        q_ref, z_ref, v_ref, lb_ref, s0_ref, of_ref, g_ref, ng_ref, o_ref, st_ref = refs
    else:
        q_ref, z_ref, v_ref, lb_ref, s0_ref, o_ref, st_ref = refs

    @pl.when(pl.program_id(0) == 0)
    def _():
        st_ref[...] = s0_ref[...]

    r = lax.broadcasted_iota(I32, (TH, TH), 0)
    c = lax.broadcasted_iota(I32, (TH, TH), 1)
    cb_shift = CB.bit_length() - 1
    same = jnp.right_shift(r, cb_shift) == jnp.right_shift(c, cb_shift)
    tri_mask = same & ((c >= r) if reverse else (c <= r))
    tri = jnp.where(tri_mask, 1.0, 0.0).astype(BF16)
    rblk = jnp.right_shift(r, cb_shift)
    cblk = jnp.right_shift(c, cb_shift)
    dist = (rblk - cblk) if not reverse else (cblk - rblk)
    for b in range(B):
        _hgrn_chunk(reverse, readout, b, tri, tri_mask, dist, refs)


def _hgrn_chunk(reverse, readout, b, tri, tri_mask, dist, refs):
    if readout:
        q_ref, z_ref, v_ref, lb_ref, s0_ref, of_ref, g_ref, ng_ref, o_ref, st_ref = refs
    else:
        q_ref, z_ref, v_ref, lb_ref, s0_ref, o_ref, st_ref = refs
    q = q_ref[b].astype(F32)
    v = v_ref[b]
    k, logf = _keys(z_ref[b], lb_ref[...])
    bl = _dot01(tri, logf)
    nt = (((1,), (1,)), ((), ()))
    tn = (((0,), (0,)), ((), ()))
    nblk = TH // CB
    e_row = 0 if reverse else CB - 1
    m_row = CB - 1 - CB // 2 if reverse else CB // 2
    tau = [bl[jb * CB + e_row:jb * CB + e_row + 1] for jb in range(nblk)]
    mid = [bl[jb * CB + m_row:jb * CB + m_row + 1] for jb in range(nblk)]
    rows = lambda vecs: jnp.concatenate([jnp.broadcast_to(x, (CB, KD)) for x in vecs], axis=0)
    mid_b = rows(mid)
    qd0 = (q * jnp.exp(bl - mid_b)).astype(BF16)
    kd0 = (k * jnp.exp(mid_b - bl)).astype(BF16)
    qs = q * jnp.exp(bl)
    ke = k * jnp.exp(rows(tau) - bl)
    order = list(range(nblk - 1, -1, -1)) if reverse else list(range(nblk))
    pre = [jnp.zeros((1, KD), F32)]
    for i in range(nblk):
        pre.append(pre[-1] + tau[order[i]])
    total = pre[nblk]
    entry = [None] * nblk
    leave = [None] * nblk
    gap = [[None] * nblk for _ in range(nblk)]
    for i, jb in enumerate(order):
        entry[jb] = jnp.exp(pre[i])
        leave[jb] = jnp.exp(total - pre[i + 1])
        for d in range(2, nblk):
            gap[d][jb] = jnp.exp(pre[i + d] - pre[i + 1]) if i + d < nblk else jnp.zeros((1, KD), F32)
    qc = (qs * rows(entry)).astype(BF16)
    kc = (ke * rows(leave)).astype(BF16)
    kx = jnp.concatenate([ke.astype(BF16)] + [(ke * rows(gap[d])).astype(BF16) for d in range(2, nblk)], axis=0)
    qsb = qs.astype(BF16)
    dec = jnp.exp(total)
    for hh in range(H):
        hs = slice(hh * DK, (hh + 1) * DK)
        sc = lax.dot_general(qd0[:, hs], kd0[:, hs], nt, preferred_element_type=F32)
        sc = jnp.where(tri_mask, sc, 0.0)
        scx = lax.dot_general(qsb[:, hs], kx[:, hs], nt, preferred_element_type=F32)
        for d in range(1, nblk):
            sc = jnp.where(dist == d, scx[:, (d - 1) * TH:d * TH], sc)
        st = st_ref[b, hh]
        o_h = (lax.dot_general(qc[:, hs], st.astype(BF16), nt, preferred_element_type=F32)
               + jnp.dot(sc.astype(BF16), v[:, hs], preferred_element_type=F32))
        st_ref[b, hh] = st * dec[:, hs] + lax.dot_general(v[:, hs], kc[:, hs], tn, preferred_element_type=F32)
        if readout:
            o_h = o_h + of_ref[b, :, hs]
            o_h = _rms(o_h) * ng_ref[...]
            gh = g_ref[b, :, hs].astype(F32)
            o_h = o_h * (gh * jax.nn.sigmoid(gh))
        o_ref[b, :, hs] = o_h


def _hgrn_scan(pz, p, lb_row, s0, reverse, o_f=None, norm_g=None):
    nch = L // TH
    chunk = (lambda c: nch - 1 - c) if reverse else (lambda c: c)
    col_spec = lambda j: pl.BlockSpec((B, TH, KD), lambda c: (0, chunk(c), j))
    in_specs = [col_spec(0), col_spec(1 if reverse else 0), col_spec(1),
                pl.BlockSpec((1, KD), lambda c: (0, 0)),
                pl.BlockSpec((B, H, DV, DK), lambda c: (0, 0, 0, 0))]
    p3 = p.reshape(B, L, p.shape[-1])
    args = [p3, pz.reshape(B, L, 2 * KD), p3, lb_row, s0]
    readout = o_f is not None
    if readout:
        in_specs += [col_spec(0), col_spec(2), pl.BlockSpec((1, DV), lambda c: (0, 0))]
        args += [o_f, p3, norm_g]
    return pl.pallas_call(
        functools.partial(_hgrn_kernel, reverse, readout),
        grid=(nch,),
        in_specs=in_specs,
        out_specs=col_spec(0),
        out_shape=jax.ShapeDtypeStruct((B, L, KD), F32),
        scratch_shapes=[pltpu.VMEM((B, H, DV, DK), F32)],
        compiler_params=_cparams(1, VMEM_LIMIT),
        name="hgrn_bwd_readout" if readout else "hgrn_fwd",
    )(*args)


def _hy_pre_kernel(v_ref, x1_ref, x0_ref, vp_ref, x1p_ref, x0p_ref, vn_ref, x1n_ref, x0n_ref,
                   w_ref, b_ref, vx_ref, x0o_ref):
    i = pl.program_id(1)
    first = i == 0
    last = i == pl.num_programs(1) - 1
    row = lax.broadcasted_iota(I32, (TM_HY, 1), 0)

    def conv(c_ref, p_ref, n_ref, col):
        x = c_ref[...].astype(F32)
        prev_row = jnp.where(first, 0.0, p_ref[...].astype(F32)[HALO - 1:HALO, :])
        next_row = jnp.where(last, 0.0, n_ref[...].astype(F32)[0:1, :])
        xm = jnp.where(row == 0, prev_row, pltpu.roll(x, 1, axis=0))
        xp = jnp.where(row == TM_HY - 1, next_row, pltpu.roll(x, TM_HY - 1, axis=0))
        cs = slice(col * HYW, (col + 1) * HYW)
        return xm * w_ref[0:1, cs] + x * w_ref[1:2, cs] + xp * w_ref[2:3, cs] + b_ref[:, cs]

    v = conv(v_ref, vp_ref, vn_ref, 0)
    x1 = conv(x1_ref, x1p_ref, x1n_ref, 1)
    x0 = conv(x0_ref, x0p_ref, x0n_ref, 2)
    vx_ref[...] = v * x1
    x0o_ref[...] = x0


def _hyena_pre(p, conv_w, conv_b):
    nt = L // TM_HY
    hb = TM_HY // HALO
    nhb = T // HALO
    cur = lambda col: pl.BlockSpec((TM_HY, HYW), lambda b, i: (b * nt + i, col))
    prv = lambda col: pl.BlockSpec((HALO, HYW), lambda b, i: (jnp.maximum((b * nt + i) * hb - 1, 0), col))
    nxt = lambda col: pl.BlockSpec((HALO, HYW), lambda b, i: (jnp.minimum((b * nt + i + 1) * hb, nhb - 1), col))
    c0 = 3
    out = jax.ShapeDtypeStruct((T, HYW), F32)
    return pl.pallas_call(
        _hy_pre_kernel,
        grid=(B, nt),
        in_specs=[cur(c0), cur(c0 + 1), cur(c0 + 2), prv(c0), prv(c0 + 1), prv(c0 + 2),
                  nxt(c0), nxt(c0 + 1), nxt(c0 + 2),
                  pl.BlockSpec((3, 3 * HYW), lambda b, i: (0, 0)),
                  pl.BlockSpec((1, 3 * HYW), lambda b, i: (0, 0))],
        out_specs=(pl.BlockSpec((TM_HY, HYW), lambda b, i: (b * nt + i, 0)),
                   pl.BlockSpec((TM_HY, HYW), lambda b, i: (b * nt + i, 0))),
        out_shape=(out, out),
        compiler_params=_cparams(2, VMEM_LIMIT),
        name="hyena_pre",
    )(p, p, p, p, p, p, p, p, p, conv_w, conv_b)


def _filt_kernel(z_ref, w1_ref, b1_ref, f1_ref, w2_ref, b2_ref, f2_ref, w3a_ref, w3b_ref, dl_ref, o_ref):
    half = z_ref.shape[0] // 2
    zt = z_ref[0:half, :]
    zb = z_ref[half:, :]
    lane = lax.broadcasted_iota(I32, zt.shape, 1)
    dot = functools.partial(jnp.dot, preferred_element_type=F32, precision=HIGHEST)
    h = jnp.sin(f1_ref[...] * (dot(jnp.concatenate([zt, zb], axis=1), w1_ref[...]) + b1_ref[...]))
    h = jnp.sin(f2_ref[...] * (dot(h, w2_ref[...]) + b2_ref[...]))
    for zz, w3_ref, rows in ((zt, w3a_ref, slice(0, half)), (zb, w3b_ref, slice(half, 2 * half))):
        taps = dot(h, w3_ref[...])
        win = jnp.exp(-_lane_pick(zz, lane, 0) * dl_ref[...])
        o_ref[0, rows, :] = taps[:, :HYW] * win
        o_ref[1, rows, :] = taps[:, HYW:] * win * _lane_pick(zz, lane, HY_EMB)


def _filter_taps(zin, w1, b1, f1, w2, b2, f2, w3a, w3b, deltas):
    tm = 2048
    ln = V7X_LANES
    full = lambda shape: pl.BlockSpec(shape, lambda i: (0, 0))
    return pl.pallas_call(
        _filt_kernel,
        grid=(L // tm,),
        in_specs=[pl.BlockSpec((tm, ln), lambda i: (i, 0)),
                  full((2 * ln, ln)), full((1, ln)), full((1, ln)),
                  full((ln, ln)), full((1, ln)), full((1, ln)),
                  full((ln, 2 * HYW)), full((ln, 2 * HYW)), full((1, HYW))],
        out_specs=pl.BlockSpec((2, tm, HYW), lambda i: (0, i, 0)),
        out_shape=jax.ShapeDtypeStruct((2, L, HYW), F32),
        compiler_params=_cparams(1, VMEM_LIMIT),
        name="hyena_filter",
    )(zin, w1, b1, f1, w2, b2, f2, w3a, w3b, deltas)


def _strided_dft_kernel(x_hbm, f_ref, o_hbm, xbuf, obuf, sem_in, sem_out):
    g = pl.program_id(0)
    ng = pl.num_programs(0)
    nb = FFT_P // FFT_BB

    def copies(grp, slot, inbound):
        n = grp // nb
        b0 = (grp % nb) * FFT_BB
        if inbound:
            return [pltpu.make_async_copy(x_hbm.at[n, :, b0 + jj, :], xbuf.at[slot, jj], sem_in.at[slot])
                    for jj in range(FFT_BB)]
        return [pltpu.make_async_copy(obuf.at[slot, jj], o_hbm.at[n, :, b0 + jj, :], sem_out.at[slot])
                for jj in range(FFT_BB)]

    def start(grp, slot, inbound):
        for cp in copies(grp, slot, inbound):
            cp.start()

    def wait(grp, slot, inbound):
        for cp in copies(grp, slot, inbound):
            cp.wait()

    @pl.when(g == 0)
    def _():
        start(0, 0, True)

    for slot in range(2):
        grp = 2 * g + slot
        if slot == 0:
            start(grp + 1, 1, True)
        else:
            @pl.when(g + 1 < ng)
            def _():
                start(grp + 1, 0, True)
        wait(grp, slot, True)

        @pl.when(g > 0)
        def _():
            wait(grp - 2, slot, False)

        for jj in range(FFT_BB):
            obuf[slot, jj] = jnp.dot(f_ref[...], xbuf[slot, jj].astype(BF16), preferred_element_type=F32)
        start(grp, slot, False)

    @pl.when(g + 1 == ng)
    def _():
        wait(2 * g, 0, False)
        wait(2 * g + 1, 1, False)


def _strided_dft(xv, fmat, name):
    n, kk = xv.shape[0], xv.shape[1]
    mm = fmat.shape[0]
    groups = n * (FFT_P // FFT_BB)
    return pl.pallas_call(
        _strided_dft_kernel,
        grid=(groups // 2,),
        in_specs=[pl.BlockSpec(memory_space=pl.ANY),
                  pl.BlockSpec((mm, kk), lambda g: (0, 0))],
        out_specs=pl.BlockSpec(memory_space=pl.ANY),
        out_shape=jax.ShapeDtypeStruct((n, mm, FFT_P, HYW), F32),
        scratch_shapes=[pltpu.VMEM((2, FFT_BB, kk, HYW), F32), pltpu.VMEM((2, FFT_BB, mm, HYW), F32),
                        pltpu.SemaphoreType.DMA((2,)), pltpu.SemaphoreType.DMA((2,))],
        compiler_params=_cparams(1, VMEM_LIMIT),
        name=name,
    )(xv, fmat)


def _cblock(mr, mi):
    return jnp.concatenate([jnp.concatenate([mr, -mi], axis=1), jnp.concatenate([mi, mr], axis=1)], axis=0)


FFT_H = FFT_P // 2


def _twiddled(fr_ref, fi_ref, tw):
    twr = tw[0:1, :]
    twi = tw[1:2, :]
    fr = fr_ref[...]
    fi = fi_ref[...]
    return fr * twr - fi * twi, fr * twi + fi * twr


def _cmul_rows(x, kr, ki):
    xr, xi = x[:FFT_P], x[FFT_P:]
    return jnp.concatenate([xr * kr - xi * ki, xr * ki + xi * kr], axis=0).astype(BF16)


_TN_DIMS = (((0,), (0,)), ((), ()))
_STAGE2_SPECS = [pl.BlockSpec((FFT_P, FFT_P), lambda d: (0, 0)),
                 pl.BlockSpec((FFT_P, FFT_P), lambda d: (0, 0)),
                 pl.BlockSpec((None, 2, FFT_P), lambda d: (d, 0, 0)),
                 pl.BlockSpec((2, FFT_P), lambda d: (0, 0))]


def _kspec_kernel(a_ref, fr_ref, fi_ref, tw_ref, twh_ref, o_ref, oh_ref):
    dd = pl.program_id(0)
    dot = functools.partial(jnp.dot, preferred_element_type=F32)

    def combine(xf, xb, out_ref):
        out_ref[0] = xf[:FFT_P] + xb[:FFT_P]
        out_ref[1] = xf[FFT_P:] - xb[FFT_P:]

    @pl.when(dd > 0)
    def _():
        rm = _cblock(*_twiddled(fr_ref, fi_ref, tw_ref[...])).astype(BF16)
        combine(dot(rm, a_ref[0].reshape(2 * FFT_P, HYW).astype(BF16)),
                dot(rm, a_ref[1].reshape(2 * FFT_P, HYW).astype(BF16)), o_ref)

    @pl.when(dd == 0)
    def _():
        for slot, tw, out_ref in ((0, tw_ref[...], o_ref), (1, twh_ref[...], oh_ref)):
            w = jnp.concatenate(_twiddled(fr_ref, fi_ref, tw), axis=0).astype(BF16)
            combine(dot(w, a_ref[0, slot].astype(BF16)), dot(w, a_ref[1, slot].astype(BF16)), out_ref)


def _kernel_spectrum(ak, fr, fi, tw, twh):
    return pl.pallas_call(
        _kspec_kernel,
        grid=(FFT_H,),
        in_specs=[pl.BlockSpec((2, None, 2, FFT_P, HYW), lambda d: (0, d, 0, 0, 0))] + _STAGE2_SPECS,
        out_specs=(pl.BlockSpec((2, None, FFT_P, HYW), lambda d: (0, d, 0, 0)),
                   pl.BlockSpec((2, FFT_P, HYW), lambda d: (0, 0, 0))),
        out_shape=(jax.ShapeDtypeStruct((2, FFT_H, FFT_P, HYW), F32),
                   jax.ShapeDtypeStruct((2, FFT_P, HYW), F32)),
        compiler_params=_cparams(1, VMEM_LIMIT),
        name="kernel_spectrum",
    )(ak, fr, fi, tw, twh)


def _mid_kernel(a_ref, kh_ref, khh_ref, fr_ref, fi_ref, tw_ref, twh_ref, o_ref):
    dd = pl.program_id(0)
    dot = functools.partial(jnp.dot, preferred_element_type=F32)
    dot_t = lambda w, y: lax.dot_general(w, y, _TN_DIMS, preferred_element_type=F32)

    @pl.when(dd > 0)
    def _():
        rm = _cblock(*_twiddled(fr_ref, fi_ref, tw_ref[...])).astype(BF16)
        for n in range(B):
            x = dot(rm, a_ref[n].reshape(2 * FFT_P, HYW).astype(BF16))
            y = _cmul_rows(x, kh_ref[0], kh_ref[1])
            o_ref[n] = dot_t(rm, y).reshape(2, FFT_P, HYW)

    @pl.when(dd == 0)
    def _():
        for slot, tw, k_ref in ((0, tw_ref[...], kh_ref), (1, twh_ref[...], khh_ref)):
            w = jnp.concatenate(_twiddled(fr_ref, fi_ref, tw), axis=0).astype(BF16)
            for n in range(B):
                y = _cmul_rows(dot(w, a_ref[n, slot].astype(BF16)), k_ref[0], k_ref[1])
                o_ref[n, slot] = dot_t(w, y)


def _fft_mid(au, kh, khh, fr, fi, tw, twh):
    pair = pl.BlockSpec((B, None, 2, FFT_P, HYW), lambda d: (0, d, 0, 0, 0))
    return pl.pallas_call(
        _mid_kernel,
        grid=(FFT_H,),
        in_specs=[pair,
                  pl.BlockSpec((2, None, FFT_P, HYW), lambda d: (0, d, 0, 0)),
                  pl.BlockSpec((2, FFT_P, HYW), lambda d: (0, 0, 0))] + _STAGE2_SPECS,
        out_specs=pair,
        out_shape=jax.ShapeDtypeStruct((B, FFT_H, 2, FFT_P, HYW), F32),
        compiler_params=_cparams(1, VMEM_LIMIT),
        name="fft_mid",
    )(au, kh, khh, fr, fi, tw, twh)


def _dft_tables():
    na = L // FFT_P
    a = np.arange(na)
    dd = np.arange(FFT_H)
    ang = 2.0 * np.pi * np.outer(dd, a) / FFT_P
    re_rows = np.cos(ang)
    im_rows = -np.sin(ang)
    im_rows[0] = np.cos(np.pi * a)
    f_first = np.stack([re_rows, im_rows], axis=1).reshape(FFT_P, na)
    gre = 2.0 * np.cos(ang)
    gim = -2.0 * np.sin(ang)
    gre[0] = 1.0
    gim[0] = np.cos(np.pi * a)
    g_last = np.stack([gre, gim], axis=1).reshape(FFT_P, na).T / FFT_N
    b = np.arange(FFT_P)
    angf = 2.0 * np.pi * np.outer(b, b) / FFT_P
    ang2 = 2.0 * np.pi * np.outer(np.arange(FFT_H + 1), b) / FFT_N
    tw = np.stack([np.cos(ang2), -np.sin(ang2)], axis=1)
    f32 = lambda x: jnp.asarray(x.astype(np.float32))
    return (f32(f_first).astype(BF16), f32(g_last).astype(BF16), f32(np.cos(angf)), f32(-np.sin(angf)),
            f32(tw[:FFT_H]), f32(tw[FFT_H]))


def _merge_kernel(ya_ref, cv_ref, vx_ref, x0_ref, ga_ref, gb_ref, x_ref, pos_ref, gt1_ref, hyd_ref,
                  wpa_ref, wpb_ref, wo_ref, g2_ref, sh2_ref, sc2_ref, wr_ref, br_ref,
                  x1_ref, t2_ref, lg_ref):
    vx = vx_ref[...]
    yb = x0_ref[...] * (cv_ref[...] + vx * hyd_ref[...])
    pa = jnp.dot(ya_ref[...].astype(BF16), wpa_ref[...], preferred_element_type=F32)
    pb = jnp.dot(yb.astype(BF16), wpb_ref[...], preferred_element_type=F32)
    mixed = (jax.nn.sigmoid(ga_ref[...].astype(F32)) * pa
             + jax.nn.sigmoid(gb_ref[...].astype(F32)) * pb)
    xm = jnp.dot(mixed.astype(BF16), wo_ref[...], preferred_element_type=F32)
    x1 = x_ref[...] + pos_ref[...] + gt1_ref[...] * xm
    x1_ref[...] = x1
    t2 = _rms(x1) * g2_ref[...]
    t2 = t2 * (1.0 + sc2_ref[...]) + sh2_ref[...]
    t2_ref[...] = t2
    t_hi = t2.astype(BF16)
    t_lo = (t2 - t_hi.astype(F32)).astype(BF16)
    rr = (jnp.dot(t_hi, wr_ref[...], preferred_element_type=F32)
          + jnp.dot(t_lo, wr_ref[...], preferred_element_type=F32))
    lg_ref[...] = rr[:, :V7X_LANES] + rr[:, V7X_LANES:] + br_ref[...]


def _merge(ya, cv, vx, x0c, p, x2, pos, gt1, hyd, wpa, wpb, wo, g2, sh2, sc2, wr, br):
    tpb = L // TM_MG
    half = lambda: pl.BlockSpec((TM_MG, HYW), lambda i: (i, 0))
    full = lambda shape: pl.BlockSpec(shape, lambda i: tuple(0 for _ in shape))
    perb = lambda: pl.BlockSpec((None, 1, D), lambda i: (i // tpb, 0, 0))
    return pl.pallas_call(
        _merge_kernel,
        grid=(T // TM_MG,),
        in_specs=[half(), half(), half(), half(),
                  pl.BlockSpec((TM_MG, D), lambda i: (i, 3)),
                  pl.BlockSpec((TM_MG, D), lambda i: (i, 4)),
                  pl.BlockSpec((TM_MG, D), lambda i: (i, 0)),
                  pl.BlockSpec((TM_MG, D), lambda i: (i % tpb, 0)),
                  perb(), full((1, HYW)),
                  full((KD, D)), full((HYW, D)), full((D, D)),
                  full((1, D)), perb(), perb(),
                  full((D, 2 * V7X_LANES)), full((1, V7X_LANES))],
        out_specs=(pl.BlockSpec((TM_MG, D), lambda i: (i, 0)),
                   pl.BlockSpec((TM_MG, D), lambda i: (i, 0)),
                   pl.BlockSpec((TM_MG, V7X_LANES), lambda i: (i, 0))),
        out_shape=(jax.ShapeDtypeStruct((T, D), F32), jax.ShapeDtypeStruct((T, D), F32),
                   jax.ShapeDtypeStruct((T, V7X_LANES), F32)),
        compiler_params=_cparams(1, VMEM_LIMIT),
        name="merge",
    )(ya, cv, vx, x0c, p, p, x2, pos, gt1, hyd, wpa, wpb, wo, g2, sh2, sc2, wr, br)


def _route_kernel(lg_ref, info_ref, cnt_ref):
    @pl.when(pl.program_id(0) == 0)
    def _():
        cnt_ref[...] = jnp.zeros_like(cnt_ref)

    lg = lg_ref[...]
    lane = lax.broadcasted_iota(I32, lg.shape, 1)
    lanef = lane.astype(F32)
    neg = -1e30
    big = 1e9
    is_g = (lane >= NEXP) & (lane < NEXP + NGRP)
    gl = jnp.where(is_g, lg, neg)
    ge = jnp.where(is_g, jnp.exp(gl - jnp.max(gl, axis=-1, keepdims=True)), 0.0)
    pg = ge / jnp.sum(ge, axis=-1, keepdims=True)
    p_top_g = jnp.max(pg, axis=-1, keepdims=True)
    gidx = jnp.min(jnp.where(is_g & (pg == p_top_g), lanef, big), axis=-1, keepdims=True)
    g_sel = gidx.astype(I32) - NEXP
    emask = (lane < NEXP) & (jnp.right_shift(lane, NEPG.bit_length() - 1) == g_sel)
    el = jnp.where(emask, lg, neg)
    ee = jnp.where(emask, jnp.exp(el - jnp.max(el, axis=-1, keepdims=True)), 0.0)
    pe = ee / jnp.sum(ee, axis=-1, keepdims=True)
    p1 = jnp.max(jnp.where(emask, pe, -1.0), axis=-1, keepdims=True)
    i1 = jnp.min(jnp.where(emask & (pe == p1), lanef, big), axis=-1, keepdims=True)
    rest = emask & (lanef != i1)
    p2 = jnp.max(jnp.where(rest, pe, -1.0), axis=-1, keepdims=True)
    i2 = jnp.min(jnp.where(rest & (pe == p2), lanef, big), axis=-1, keepdims=True)
    wsum = p1 + p2
    w1 = p_top_g * p1 / wsum
    w2 = p_top_g * p2 / wsum
    sel1 = lanef == i1
    sel2 = lanef == i2
    oh = jnp.where(sel1 | sel2, 1.0, 0.0)
    r = lax.broadcasted_iota(I32, (TR, TR), 0)
    c = lax.broadcasted_iota(I32, (TR, TR), 1)
    stril = jnp.where(c < r, 1.0, 0.0).astype(BF16)
    before = jnp.dot(stril, oh.astype(BF16), preferred_element_type=F32) + cnt_ref[...]
    r1 = jnp.sum(jnp.where(sel1, before, 0.0), axis=-1, keepdims=True)
    r2 = jnp.sum(jnp.where(sel2, before, 0.0), axis=-1, keepdims=True)
    cnt_ref[...] += jnp.sum(oh, axis=0, keepdims=True)
    info = jnp.where(lane == 0, i1, jnp.where(lane == 1, r1, jnp.where(lane == 2, i2, jnp.where(
        lane == 3, r2, jnp.where(lane == 4, w1, jnp.where(lane == 5, w2, 0.0))))))
    info_ref[...] = info


def _route(lg):
    return pl.pallas_call(
        _route_kernel,
        grid=(T // TR,),
        in_specs=[pl.BlockSpec((TR, V7X_LANES), lambda i: (i, 0))],
        out_specs=(pl.BlockSpec((TR, V7X_LANES), lambda i: (i, 0)),
                   pl.BlockSpec((1, V7X_LANES), lambda i: (0, 0))),
        out_shape=(jax.ShapeDtypeStruct((T, V7X_LANES), F32), jax.ShapeDtypeStruct((1, V7X_LANES), F32)),
        compiler_params=_cparams(1, VMEM_LIMIT),
        name="route",
    )(lg)


def _scatter_kernel(er_ref, zt_ref, t2_ref, zeros_hbm, xs_hbm, sem, zsem):
    i = pl.program_id(0)

    def zcopy(row):
        start = pl.multiple_of(jnp.maximum(row, 0), TE)
        return pltpu.make_async_copy(zeros_hbm, xs_hbm.at[pl.ds(start, TE)], zsem)

    @pl.when(i == 0)
    def _():
        def ztail(start, e, carry):
            @pl.when(zt_ref[0, e] >= 0)
            def _():
                cp = zcopy(zt_ref[0, e])
                cp.start() if start else cp.wait()
            return carry

        lax.fori_loop(0, NEXP, functools.partial(ztail, True), 0)
        lax.fori_loop(0, NEXP, functools.partial(ztail, False), 0)

        def zrest(start, tile, carry):
            cp = zcopy(tile * TE)
            cp.start() if start else cp.wait()
            return carry

        lax.fori_loop(zt_ref[0, NA_OFF], NT_EXP, functools.partial(zrest, True), 0)
        lax.fori_loop(zt_ref[0, NA_OFF], NT_EXP, functools.partial(zrest, False), 0)

    def issue(j, carry):
        for kk in range(2):
            dst = zt_ref[0, ST_OFF + er_ref[0, 4 * j + 2 * kk]] + er_ref[0, 4 * j + 2 * kk + 1]
            pltpu.make_async_copy(t2_ref.at[pl.ds(j, 1)], xs_hbm.at[pl.ds(dst, 1)], sem).start(priority=kk)
        return carry

    lax.fori_loop(0, TS, issue, 0, unroll=ROW_UNROLL)

    def drain(j, carry):
        for kk in range(2):
            pltpu.make_async_copy(t2_ref.at[pl.ds(0, 1)], xs_hbm.at[pl.ds(0, 1)], sem).wait()
        return carry

    lax.fori_loop(0, TS, drain, 0, unroll=ROW_UNROLL)


def _scatter_rows(er3, meta, t2, zeros_tile):
    return pl.pallas_call(
        _scatter_kernel,
        grid=(T // TS,),
        in_specs=[pl.BlockSpec((None, 1, 4 * TS), lambda i: (i, 0, 0), memory_space=pltpu.SMEM),
                  pl.BlockSpec(memory_space=pltpu.SMEM),
                  pl.BlockSpec((TS, D), lambda i: (i, 0)),
                  pl.BlockSpec(memory_space=pl.ANY)],
        out_specs=pl.BlockSpec(memory_space=pl.ANY),
        out_shape=jax.ShapeDtypeStruct((NP_ROWS, D), F32),
        scratch_shapes=[pltpu.SemaphoreType.DMA(()), pltpu.SemaphoreType.DMA(())],
        compiler_params=_cparams(1, VMEM_LIMIT),
        name="scatter_rows",
    )(er3, meta, t2, zeros_tile)


def _expert_kernel(te_ref, na_ref, xs_ref, wg_ref, wu_ref, wd_ref, ys_ref, wgb_ref, wub_ref, wdb_ref):
    i = pl.program_id(0)
    active = i < na_ref[0]

    @pl.when(active & ((i == 0) | (te_ref[i] != te_ref[jnp.maximum(i - 1, 0)])))
    def _():
        wgb_ref[...] = wg_ref[...].astype(BF16)
        wub_ref[...] = wu_ref[...].astype(BF16)
        wdb_ref[...] = wd_ref[...].astype(BF16)

    @pl.when(active)
    def _():
        x = xs_ref[...].astype(BF16)
        g = jnp.dot(x, wgb_ref[...], preferred_element_type=F32)
        u = jnp.dot(x, wub_ref[...], preferred_element_type=F32)
        hid = (g * jax.nn.sigmoid(g) * u).astype(BF16)
        ys_ref[...] = jnp.dot(hid, wdb_ref[...], preferred_element_type=F32)

    @pl.when(jnp.logical_not(active))
    def _():
        ys_ref[...] = jnp.zeros_like(ys_ref)


def _experts(tile_expert, n_active, xs, wg, wu, wd):
    rows = lambda i, te, na: (i, 0)
    wsel = lambda i, te, na: (te[i], 0, 0)
    grid_spec = pltpu.PrefetchScalarGridSpec(
        num_scalar_prefetch=2,
        grid=(NT_EXP,),
        in_specs=[pl.BlockSpec((TE, D), rows),
                  pl.BlockSpec((None, D, DEXP), wsel),
                  pl.BlockSpec((None, D, DEXP), wsel),
                  pl.BlockSpec((None, DEXP, D), wsel)],
        out_specs=pl.BlockSpec((TE, D), rows),
        scratch_shapes=[pltpu.VMEM((D, DEXP), BF16), pltpu.VMEM((D, DEXP), BF16), pltpu.VMEM((DEXP, D), BF16)],
    )
    return pl.pallas_call(
        _expert_kernel,
        grid_spec=grid_spec,
        out_shape=jax.ShapeDtypeStruct((NP_ROWS, D), F32),
        compiler_params=_cparams(1, VMEM_LIMIT),
        name="experts",
    )(tile_expert, n_active, xs, wg, wu, wd)


def _combine_kernel(er_ref, zt_ref, info_ref, x1_ref, gt2_ref, fg_ref, ys_hbm, o_ref, buf, sem):
    def issue(j, carry):
        for kk in range(2):
            src = zt_ref[0, ST_OFF + er_ref[0, 4 * j + 2 * kk]] + er_ref[0, 4 * j + 2 * kk + 1]
            pltpu.make_async_copy(ys_hbm.at[pl.ds(src, 1)], buf.at[kk, pl.ds(j, 1)], sem).start(priority=kk)
        return carry

    lax.fori_loop(0, TC, issue, 0, unroll=ROW_UNROLL)

    def drain(j, carry):
        for kk in range(2):
            pltpu.make_async_copy(ys_hbm.at[pl.ds(0, 1)], buf.at[kk, pl.ds(0, 1)], sem).wait()
        return carry

    lax.fori_loop(0, TC, drain, 0, unroll=ROW_UNROLL)

    info = info_ref[...]
    lane = lax.broadcasted_iota(I32, info.shape, 1)
    w1 = _lane_pick(info, lane, 4)
    w2 = _lane_pick(info, lane, 5)
    moe = w1 * buf[0] + w2 * buf[1]
    x2 = x1_ref[...] + gt2_ref[...] * moe
    o_ref[...] = _rms(x2) * fg_ref[...]


def _combine(er3, meta, info, x1, gt2, fg, ys):
    tpb = L // TC
    return pl.pallas_call(
        _combine_kernel,
        grid=(T // TC,),
        in_specs=[pl.BlockSpec((None, 1, 4 * TC), lambda i: (i, 0, 0), memory_space=pltpu.SMEM),
                  pl.BlockSpec(memory_space=pltpu.SMEM),
                  pl.BlockSpec((TC, V7X_LANES), lambda i: (i, 0)),
                  pl.BlockSpec((TC, D), lambda i: (i, 0)),
                  pl.BlockSpec((None, 1, D), lambda i: (i // tpb, 0, 0)),
                  pl.BlockSpec((1, D), lambda i: (0, 0)),
                  pl.BlockSpec(memory_space=pl.ANY)],
        out_specs=pl.BlockSpec((TC, D), lambda i: (i, 0)),
        out_shape=jax.ShapeDtypeStruct((T, D), F32),
        scratch_shapes=[pltpu.VMEM((2, TC, D), F32), pltpu.SemaphoreType.DMA(())],
        compiler_params=_cparams(1, VMEM_LIMIT),
        name="combine",
    )(er3, meta, info, x1, gt2, fg, ys)


def _pos_emb():
    rows = L // GRID_W
    quarter = D // 4
    omega = 1.0 / (10000.0 ** (jnp.arange(quarter, dtype=F32) / quarter))

    def axis_emb(pos):
        a = pos[:, None] * omega[None, :]
        return jnp.concatenate([jnp.sin(a), jnp.cos(a)], axis=-1)

    er = axis_emb(jnp.arange(rows, dtype=F32))
    ec = axis_emb(jnp.arange(GRID_W, dtype=F32))
    return jnp.concatenate([jnp.repeat(er, GRID_W, axis=0), jnp.tile(ec, (rows, 1))], axis=-1)


def _filter_features():
    t = jnp.linspace(0.0, 1.0, L, dtype=F32)[:, None]
    bands = (HY_EMB - 1) // 2
    ang = (2.0 * math.pi * jnp.arange(L, dtype=F32) / L)[:, None] * \
        jnp.linspace(1e-4, bands - 1, bands, dtype=F32)[None, :]
    not_tap0 = (jnp.arange(L) >= 1).astype(F32)[:, None]
    pad = jnp.zeros((L, V7X_LANES - HY_EMB - 1), F32)
    return jnp.concatenate([t, jnp.cos(ang), -jnp.sin(ang), not_tap0, pad], axis=-1)


def _pad2(a, rows, cols):
    return jnp.pad(a, ((0, rows - a.shape[0]), (0, cols - a.shape[1])))


def kernel(x, c, ctx, c_ctx, ada_w, ada_b, norm1_g, norm2_g, w_in, hgrn_lb, hgrn_norm_g, hy_conv_w, hy_conv_b, hy_filt_w1, hy_filt_b1, hy_filt_freq1, hy_filt_w2, hy_filt_b2, hy_filt_freq2, hy_filt_w3, hy_d, w_proj_a, w_proj_b, w_out, moe_router_g_w, moe_router_g_b, moe_router_e_w, moe_router_e_b, moe_w_gate, moe_w_up, moe_w_down, final_norm_g):
    cvec = jnp.zeros((8, D), F32).at[0:B].set(c).at[B].set(c_ctx)
    mod = _adaln(cvec, ada_w[0], ada_b[0][None, :])
    m6 = mod.reshape(8, 6, D)
    sh1, sc1, gt1, sh2, sc2, gt2 = [m6[0:B, k][:, None, :] for k in range(6)]
    csh1, csc1 = m6[B:B + 1, 0], m6[B:B + 1, 1]

    lbs = jnp.cumsum(jax.nn.softmax(hgrn_lb.astype(F32), axis=0), axis=0)[0]
    g1 = norm1_g[0][None, :]
    pos = _pos_emb()
    x2 = x.reshape(T, D)

    w_ctx = w_in[0][:, KD:4 * KD].astype(BF16)
    s_f, s_b = _context_states(ctx, g1, csh1, csc1, w_ctx, lbs)

    w0 = w_in[0]
    w_perm = jnp.concatenate([w0[:, KD:3 * KD], w0[:, :KD], w0[:, 3 * KD:]], axis=1).astype(BF16)
    pz, p = _in_projection(x2, pos, g1, sh1, sc1, w_perm)

    o_f = _hgrn_scan(pz, p, lbs[0:1], s_f, False)
    y_a = _hgrn_scan(pz, p, lbs[1:2], s_b, True, o_f=o_f, norm_g=hgrn_norm_g[0][None, :])

    vx, x0c = _hyena_pre(p, hy_conv_w[0], hy_conv_b[0][None, :])
    deltas = jnp.abs(jnp.linspace(math.log(HY_DECAY_TARGET) / HY_SLOW_PCT,
                                  math.log(HY_DECAY_TARGET) / HY_FAST_PCT, HYW, dtype=F32))[None, :]
    ln = V7X_LANES
    fh = hy_filt_w2.shape[-1]
    blockdiag = lambda m: jnp.concatenate([_pad2(m, m.shape[0], 2 * m.shape[1]),
                                           jnp.pad(m, ((0, 0), (m.shape[1], 0)))], axis=0)
    twice = lambda v: jnp.concatenate([v, v])[None, :]
    w3 = hy_filt_w3[0]
    taps = _filter_taps(
        _filter_features(),
        blockdiag(_pad2(hy_filt_w1[0], ln, fh)), twice(hy_filt_b1[0]), twice(hy_filt_freq1[0]),
        blockdiag(hy_filt_w2[0]), twice(hy_filt_b2[0]), twice(hy_filt_freq2[0]),
        _pad2(w3, ln, 2 * HYW), jnp.pad(w3, ((fh, 0), (0, 0))), deltas)
    f_first, g_last, fr, fi, tw, twh = _dft_tables()
    na = L // FFT_P
    ak = _strided_dft(taps.reshape(2, na, FFT_P, HYW), f_first, "dft_first_taps")
    kh, khh = _kernel_spectrum(ak.reshape(2, FFT_H, 2, FFT_P, HYW), fr, fi, tw, twh)
    au = _strided_dft(vx.reshape(B, na, FFT_P, HYW), f_first, "dft_first")
    bp = _fft_mid(au.reshape(B, FFT_H, 2, FFT_P, HYW), kh, khh, fr, fi, tw, twh)
    conv = _strided_dft(bp.reshape(B, FFT_P, FFT_P, HYW), g_last, "dft_last").reshape(T, HYW)

    wr = jnp.concatenate([jnp.transpose(moe_router_e_w[0], (1, 0, 2)).reshape(D, NEXP),
                          moe_router_g_w[0], jnp.zeros((D, V7X_LANES - NEXP - NGRP), F32)], axis=1)
    wr_hi = wr.astype(BF16)
    wr = jnp.concatenate([wr_hi, (wr - wr_hi.astype(F32)).astype(BF16)], axis=1)
    br = jnp.concatenate([moe_router_e_b[0].reshape(NEXP), moe_router_g_b[0],
                          jnp.zeros((V7X_LANES - NEXP - NGRP,), F32)])[None, :]
    x1, t2, lg = _merge(y_a.reshape(T, KD), conv, vx, x0c, p, x2, pos, gt1, hy_d[0][None, :],
                        w_proj_a[0].astype(BF16), w_proj_b[0].astype(BF16), w_out[0].astype(BF16),
                        norm2_g[0][None, :], sh2, sc2, wr, br)

    info, counts = _route(lg)
    cnt = counts[0, :NEXP].astype(I32)
    pc = ((cnt + TE - 1) // TE) * TE
    ends = jnp.cumsum(pc)
    starts = ends - pc
    n_active = (ends[-1] // TE).astype(I32)[None]
    tile_rows = jnp.arange(NT_EXP, dtype=I32) * TE
    tile_expert = jnp.minimum(jnp.sum((ends[None, :] <= tile_rows[:, None]).astype(I32), axis=1), NEXP - 1)
    meta = jnp.concatenate([jnp.where(pc > 0, ends - TE, -1), starts, n_active]).astype(I32)[None, :]
    er = info[:, :4].astype(I32)

    xs = _scatter_rows(er.reshape(T // TS, 1, 4 * TS), meta, t2, jnp.zeros((TE, D), F32))
    ys = _experts(tile_expert, n_active, xs,
                  moe_w_gate[0].reshape(NEXP, D, DEXP), moe_w_up[0].reshape(NEXP, D, DEXP),
                  moe_w_down[0].reshape(NEXP, DEXP, D))
    out = _combine(er.reshape(T // TC, 1, 4 * TC), meta, info, x1, gt2, final_norm_g[None, :], ys)
    return out.reshape(B, L, D)
```

```python
import functools
import math

import numpy as np
import jax
import jax.numpy as jnp
from jax import lax
from jax.experimental import pallas as pl
from jax.experimental.pallas import tpu as pltpu

F32 = jnp.float32
BF16 = jnp.bfloat16
I32 = jnp.int32
HIGHEST = lax.Precision.HIGHEST

D = 1024
B = 2
L = 8192
T = B * L
CTX = 256
GRID_W = 64
EPS = 1e-6
H = 4
DK = 128
DV = 128
KD = H * DK
IN_W = 6144
HYW = 512
HY_EMB = 33
NGRP = 4
NEPG = 8
NEXP = NGRP * NEPG
DEXP = 512
HY_DECAY_TARGET = 1e-2
HY_FAST_PCT = 0.3
HY_SLOW_PCT = 1.5

V7X_LANES = 128
V7X_SUBLANES = 8
V7X_VMEM_BYTES = 64 * 1024 * 1024
VMEM_LIMIT = 48 * 1024 * 1024

FFT_N = 2 * L
FFT_P = 128
FFT_BB = 8

TM_IN = 1024
TN_IN = 1024
TH = 128
CB = 32
TM_HY = 1024
HALO = 2 * V7X_SUBLANES
TM_MG = 256
TR = 512
TE = 256
NP_ROWS = 2 * T + NEXP * TE
NT_EXP = NP_ROWS // TE
TS = 512
TC = 256
ROW_UNROLL = 8
ROW_GROUP = 32
ST_OFF = NEXP
NA_OFF = 2 * NEXP


def _cparams(n_axes, vmem=None):
    return pltpu.CompilerParams(dimension_semantics=("arbitrary",) * n_axes,
                                vmem_limit_bytes=vmem)


def _split3(x):
    hi = x.astype(BF16)
    r = x - hi.astype(F32)
    mid = r.astype(BF16)
    lo = (r - mid.astype(F32)).astype(BF16)
    return hi, mid, lo


def _dot01(m, x):
    hi, mid, lo = _split3(x)
    return (jnp.dot(m, hi, preferred_element_type=F32) + jnp.dot(m, mid, preferred_element_type=F32)
            + jnp.dot(m, lo, preferred_element_type=F32))


def _rms(x):
    return x * lax.rsqrt(jnp.mean(x * x, axis=-1, keepdims=True) + EPS)


def _lane_pick(x, lane, idx):
    return jnp.sum(jnp.where(lane == idx, x, 0.0), axis=-1, keepdims=True)


def _ada_kernel(c_ref, w_ref, b_ref, o_ref):
    c = c_ref[...]
    s = c * jax.nn.sigmoid(c)
    o_ref[...] = jnp.dot(s, w_ref[...], preferred_element_type=F32, precision=HIGHEST) + b_ref[...]


def _adaln(cvec, w, b):
    tn = 1536
    return pl.pallas_call(
        _ada_kernel,
        grid=(6 * D // tn,),
        in_specs=[pl.BlockSpec((8, D), lambda j: (0, 0)),
                  pl.BlockSpec((D, tn), lambda j: (0, j)),
                  pl.BlockSpec((1, tn), lambda j: (0, j))],
        out_specs=pl.BlockSpec((8, tn), lambda j: (0, j)),
        out_shape=jax.ShapeDtypeStruct((8, 6 * D), F32),
        compiler_params=_cparams(1, VMEM_LIMIT),
        name="adaln",
    )(cvec, w, b)


def _keys(z, lb):
    sig = jax.nn.sigmoid(z)
    logf = jnp.log(lb + (1.0 - lb) * sig)
    k = (1.0 - lb) * jax.nn.sigmoid(-z)
    return k, logf


def _ctx_kernel(ctx_ref, g_ref, sh_ref, sc_ref, w_ref, lb_ref, sf_ref, sb_ref):
    h = _rms(ctx_ref[...]) * g_ref[...]
    h = h * (1.0 + sc_ref[...]) + sh_ref[...]
    p = jnp.dot(h.astype(BF16), w_ref[...], preferred_element_type=F32)
    zf, zb, v = p[:, :KD], p[:, KD:2 * KD], p[:, 2 * KD:]
    kf, lf = _keys(zf, lb_ref[0:1, :])
    kb, lbk = _keys(zb, lb_ref[1:2, :])
    r = lax.broadcasted_iota(I32, (CTX, CTX), 0)
    c = lax.broadcasted_iota(I32, (CTX, CTX), 1)
    tril = jnp.where(c <= r, 1.0, 0.0).astype(BF16)
    cf = _dot01(tril, lf)
    cb = _dot01(tril, lbk)
    kfd = (kf * jnp.exp(cf[CTX - 1:CTX, :] - cf)).astype(BF16)
    kbd = (kb * jnp.exp(cb - lbk)).astype(BF16)
    vb = v.astype(BF16)
    tn = (((0,), (0,)), ((), ()))
    for hh in range(H):
        hs = slice(hh * DK, (hh + 1) * DK)
        sf_ref[hh] = lax.dot_general(vb[:, hs], kfd[:, hs], tn, preferred_element_type=F32)
        sb_ref[hh] = lax.dot_general(vb[:, hs], kbd[:, hs], tn, preferred_element_type=F32)


def _context_states(ctx, g1, csh1, csc1, w_ctx, lbs):
    st = jax.ShapeDtypeStruct((B, H, DV, DK), F32)
    return pl.pallas_call(
        _ctx_kernel,
        grid=(B,),
        in_specs=[pl.BlockSpec((None, CTX, D), lambda b: (b, 0, 0)),
                  pl.BlockSpec((1, D), lambda b: (0, 0)),
                  pl.BlockSpec((1, D), lambda b: (0, 0)),
                  pl.BlockSpec((1, D), lambda b: (0, 0)),
                  pl.BlockSpec((D, 3 * KD), lambda b: (0, 0)),
                  pl.BlockSpec((2, KD), lambda b: (0, 0))],
        out_specs=(pl.BlockSpec((None, H, DV, DK), lambda b: (b, 0, 0, 0)),
                   pl.BlockSpec((None, H, DV, DK), lambda b: (b, 0, 0, 0))),
        out_shape=(st, st),
        compiler_params=_cparams(1, VMEM_LIMIT),
        name="ctx_states",
    )(ctx, g1, csh1, csc1, w_ctx, lbs)


def _pos_tile(er_ref, ec_ref, row0, nrow):
    lo = jnp.concatenate([jnp.broadcast_to(er_ref[pl.ds(row0 + i, 1), :], (GRID_W, D // 2))
                          for i in range(nrow)], axis=0)
    hi = jnp.concatenate([ec_ref[...]] * nrow, axis=0)
    return jnp.concatenate([lo, hi], axis=1)


def _inproj_kernel(x_ref, er_ref, ec_ref, g_ref, sh_ref, sc_ref, w_ref, oz_ref, o_ref, hx_ref):
    j = pl.program_id(1)

    @pl.when(j == 0)
    def _():
        nrow = TM_IN // GRID_W
        row0 = (pl.program_id(0) % (L // TM_IN)) * nrow
        h = _rms(x_ref[...] + _pos_tile(er_ref, ec_ref, row0, nrow)) * g_ref[...]
        hx_ref[...] = (h * (1.0 + sc_ref[...]) + sh_ref[...]).astype(BF16)

    r = jnp.dot(hx_ref[...], w_ref[...], preferred_element_type=F32)

    @pl.when(j == 0)
    def _():
        o_ref[:, :KD] = r[:, :KD].astype(BF16)
        oz_ref[:, :KD] = r[:, KD:]

    @pl.when(j == 1)
    def _():
        oz_ref[:, KD:] = r[:, :KD]
        o_ref[:, KD:] = r[:, KD:].astype(BF16)

    @pl.when(j > 1)
    def _():
        o_ref[...] = r.astype(BF16)


def _in_projection(x2, er, ec, g1, sh1, sc1, w_bf):
    tiles_per_batch = L // TM_IN
    return pl.pallas_call(
        _inproj_kernel,
        grid=(T // TM_IN, IN_W // TN_IN),
        in_specs=[pl.BlockSpec((TM_IN, D), lambda i, j: (i, 0)),
                  pl.BlockSpec((L // GRID_W, D // 2), lambda i, j: (0, 0)),
                  pl.BlockSpec((GRID_W, D // 2), lambda i, j: (0, 0)),
                  pl.BlockSpec((1, D), lambda i, j: (0, 0)),
                  pl.BlockSpec((None, 1, D), lambda i, j: (i // tiles_per_batch, 0, 0)),
                  pl.BlockSpec((None, 1, D), lambda i, j: (i // tiles_per_batch, 0, 0)),
                  pl.BlockSpec((D, TN_IN), lambda i, j: (0, j))],
        out_specs=(pl.BlockSpec((TM_IN, TN_IN), lambda i, j: (i, 0)),
                   pl.BlockSpec((TM_IN, TN_IN), lambda i, j: (i, jnp.maximum(j - 1, 0)))),
        out_shape=(jax.ShapeDtypeStruct((T, 2 * KD), F32),
                   jax.ShapeDtypeStruct((T, IN_W - 2 * KD), BF16)),
        scratch_shapes=[pltpu.VMEM((TM_IN, D), BF16)],
        compiler_params=_cparams(2, VMEM_LIMIT),
        name="in_proj",
    )(x2, er, ec, g1, sh1, sc1, w_bf)


def _hgrn_kernel(reverse, readout, *refs):
    if readout:
        q_ref, z_ref, v_ref, lb_ref, s0_ref, of_ref, g_ref, ng_ref, o_ref, st_ref = refs
    else:
        q_ref, z_ref, v_ref, lb_ref, s0_ref, o_ref, st_ref = refs

    @pl.when(pl.program_id(0) == 0)
    def _():
        st_ref[...] = s0_ref[...]

    r = lax.broadcasted_iota(I32, (TH, TH), 0)
    c = lax.broadcasted_iota(I32, (TH, TH), 1)
    cb_shift = CB.bit_length() - 1
    same = jnp.right_shift(r, cb_shift) == jnp.right_shift(c, cb_shift)
    tri_mask = same & ((c >= r) if reverse else (c <= r))
    tri = jnp.where(tri_mask, 1.0, 0.0).astype(BF16)
    rblk = jnp.right_shift(r, cb_shift)
    cblk = jnp.right_shift(c, cb_shift)
    dist = (rblk - cblk) if not reverse else (cblk - rblk)
    for b in range(B):
        _hgrn_chunk(reverse, readout, b, tri, tri_mask, dist, refs)


def _hgrn_chunk(reverse, readout, b, tri, tri_mask, dist, refs):
    if readout:
        q_ref, z_ref, v_ref, lb_ref, s0_ref, of_ref, g_ref, ng_ref, o_ref, st_ref = refs
    else:
        q_ref, z_ref, v_ref, lb_ref, s0_ref, o_ref, st_ref = refs
    q = q_ref[b].astype(F32)
    v = v_ref[b]
    k, logf = _keys(z_ref[b], lb_ref[...])
    bl = _dot01(tri, logf)
    nt = (((1,), (1,)), ((), ()))
    tn = (((0,), (0,)), ((), ()))
    nblk = TH // CB
    e_row = 0 if reverse else CB - 1
    m_row = CB - 1 - CB // 2 if reverse else CB // 2
    tau = [bl[jb * CB + e_row:jb * CB + e_row + 1] for jb in range(nblk)]
    mid = [bl[jb * CB + m_row:jb * CB + m_row + 1] for jb in range(nblk)]
    rows = lambda vecs: jnp.concatenate([jnp.broadcast_to(x, (CB, KD)) for x in vecs], axis=0)
    mid_b = rows(mid)
    qd0 = (q * jnp.exp(bl - mid_b)).astype(BF16)
    kd0 = (k * jnp.exp(mid_b - bl)).astype(BF16)
    qs = q * jnp.exp(bl)
    ke = k * jnp.exp(rows(tau) - bl)
    order = list(range(nblk - 1, -1, -1)) if reverse else list(range(nblk))
    pre = [jnp.zeros((1, KD), F32)]
    for i in range(nblk):
        pre.append(pre[-1] + tau[order[i]])
    total = pre[nblk]
    entry = [None] * nblk
    leave = [None] * nblk
    gap = [[None] * nblk for _ in range(nblk)]
    for i, jb in enumerate(order):
        entry[jb] = jnp.exp(pre[i])
        leave[jb] = jnp.exp(total - pre[i + 1])
        for d in range(2, nblk):
            gap[d][jb] = jnp.exp(pre[i + d] - pre[i + 1]) if i + d < nblk else jnp.zeros((1, KD), F32)
    qc = (qs * rows(entry)).astype(BF16)
    kc = (ke * rows(leave)).astype(BF16)
    kx = jnp.concatenate([ke.astype(BF16)] + [(ke * rows(gap[d])).astype(BF16) for d in range(2, nblk)], axis=0)
    qsb = qs.astype(BF16)
    dec = jnp.exp(total)
    for hh in range(H):
        hs = slice(hh * DK, (hh + 1) * DK)
        sc = lax.dot_general(qd0[:, hs], kd0[:, hs], nt, preferred_element_type=F32)
        sc = jnp.where(tri_mask, sc, 0.0)
        scx = lax.dot_general(qsb[:, hs], kx[:, hs], nt, preferred_element_type=F32)
        for d in range(1, nblk):
            sc = jnp.where(dist == d, scx[:, (d - 1) * TH:d * TH], sc)
        st = st_ref[b, hh]
        o_h = (lax.dot_general(qc[:, hs], st.astype(BF16), nt, preferred_element_type=F32)
               + jnp.dot(sc.astype(BF16), v[:, hs], preferred_element_type=F32))
        st_ref[b, hh] = st * dec[:, hs] + lax.dot_general(v[:, hs], kc[:, hs], tn, preferred_element_type=F32)
        if readout:
            o_h = o_h + of_ref[b, :, hs]
            o_h = _rms(o_h) * ng_ref[...]
            gh = g_ref[b, :, hs].astype(F32)
            o_h = o_h * (gh * jax.nn.sigmoid(gh))
        o_ref[b, :, hs] = o_h


def _hgrn_scan(pz, p, lb_row, s0, reverse, o_f=None, norm_g=None):
    nch = L // TH
    chunk = (lambda c: nch - 1 - c) if reverse else (lambda c: c)
    col_spec = lambda j: pl.BlockSpec((B, TH, KD), lambda c: (0, chunk(c), j))
    in_specs = [col_spec(0), col_spec(1 if reverse else 0), col_spec(1),
                pl.BlockSpec((1, KD), lambda c: (0, 0)),
                pl.BlockSpec((B, H, DV, DK), lambda c: (0, 0, 0, 0))]
    p3 = p.reshape(B, L, p.shape[-1])
    args = [p3, pz.reshape(B, L, 2 * KD), p3, lb_row, s0]
    readout = o_f is not None
    if readout:
        in_specs += [col_spec(0), col_spec(2), pl.BlockSpec((1, DV), lambda c: (0, 0))]
        args += [o_f, p3, norm_g]
    return pl.pallas_call(
        functools.partial(_hgrn_kernel, reverse, readout),
        grid=(nch,),
        in_specs=in_specs,
        out_specs=col_spec(0),
        out_shape=jax.ShapeDtypeStruct((B, L, KD), F32),
        scratch_shapes=[pltpu.VMEM((B, H, DV, DK), F32)],
        compiler_params=_cparams(1, VMEM_LIMIT),
        name="hgrn_bwd_readout" if readout else "hgrn_fwd",
    )(*args)


def _hy_pre_kernel(v_ref, x1_ref, x0_ref, vp_ref, x1p_ref, x0p_ref, vn_ref, x1n_ref, x0n_ref,
                   w_ref, b_ref, vx_ref, x0o_ref):
    i = pl.program_id(1)
    first = i == 0
    last = i == pl.num_programs(1) - 1
    row = lax.broadcasted_iota(I32, (TM_HY, 1), 0)

    def conv(c_ref, p_ref, n_ref, col):
        x = c_ref[...].astype(F32)
        prev_row = jnp.where(first, 0.0, p_ref[...].astype(F32)[HALO - 1:HALO, :])
        next_row = jnp.where(last, 0.0, n_ref[...].astype(F32)[0:1, :])
        xm = jnp.where(row == 0, prev_row, pltpu.roll(x, 1, axis=0))
        xp = jnp.where(row == TM_HY - 1, next_row, pltpu.roll(x, TM_HY - 1, axis=0))
        cs = slice(col * HYW, (col + 1) * HYW)
        return xm * w_ref[0:1, cs] + x * w_ref[1:2, cs] + xp * w_ref[2:3, cs] + b_ref[:, cs]

    v = conv(v_ref, vp_ref, vn_ref, 0)
    x1 = conv(x1_ref, x1p_ref, x1n_ref, 1)
    x0 = conv(x0_ref, x0p_ref, x0n_ref, 2)
    vx_ref[...] = v * x1
    x0o_ref[...] = x0


def _hyena_pre(p, conv_w, conv_b):
    nt = L // TM_HY
    hb = TM_HY // HALO
    nhb = T // HALO
    cur = lambda col: pl.BlockSpec((TM_HY, HYW), lambda b, i: (b * nt + i, col))
    prv = lambda col: pl.BlockSpec((HALO, HYW), lambda b, i: (jnp.maximum((b * nt + i) * hb - 1, 0), col))
    nxt = lambda col: pl.BlockSpec((HALO, HYW), lambda b, i: (jnp.minimum((b * nt + i + 1) * hb, nhb - 1), col))
    c0 = 3
    out = jax.ShapeDtypeStruct((T, HYW), F32)
    return pl.pallas_call(
        _hy_pre_kernel,
        grid=(B, nt),
        in_specs=[cur(c0), cur(c0 + 1), cur(c0 + 2), prv(c0), prv(c0 + 1), prv(c0 + 2),
                  nxt(c0), nxt(c0 + 1), nxt(c0 + 2),
                  pl.BlockSpec((3, 3 * HYW), lambda b, i: (0, 0)),
                  pl.BlockSpec((1, 3 * HYW), lambda b, i: (0, 0))],
        out_specs=(pl.BlockSpec((TM_HY, HYW), lambda b, i: (b * nt + i, 0)),
                   pl.BlockSpec((TM_HY, HYW), lambda b, i: (b * nt + i, 0))),
        out_shape=(out, out),
        compiler_params=_cparams(2, VMEM_LIMIT),
        name="hyena_pre",
    )(p, p, p, p, p, p, p, p, p, conv_w, conv_b)


def _filt_kernel(z_ref, w1_ref, b1_ref, f1_ref, w2_ref, b2_ref, f2_ref, w3a_ref, w3b_ref, dl_ref, o_ref):
    half = z_ref.shape[0] // 2
    zt = z_ref[0:half, :]
    zb = z_ref[half:, :]
    lane = lax.broadcasted_iota(I32, zt.shape, 1)
    dot = functools.partial(jnp.dot, preferred_element_type=F32, precision=HIGHEST)
    h = jnp.sin(f1_ref[...] * (dot(jnp.concatenate([zt, zb], axis=1), w1_ref[...]) + b1_ref[...]))
    h = jnp.sin(f2_ref[...] * (dot(h, w2_ref[...]) + b2_ref[...]))
    for zz, w3_ref, rows in ((zt, w3a_ref, slice(0, half)), (zb, w3b_ref, slice(half, 2 * half))):
        taps = dot(h, w3_ref[...])
        win = jnp.exp(-_lane_pick(zz, lane, 0) * dl_ref[...])
        o_ref[0, rows, :] = taps[:, :HYW] * win
        o_ref[1, rows, :] = taps[:, HYW:] * win * _lane_pick(zz, lane, HY_EMB)


def _filter_taps(zin, w1, b1, f1, w2, b2, f2, w3a, w3b, deltas):
    tm = 2048
    ln = V7X_LANES
    full = lambda shape: pl.BlockSpec(shape, lambda i: (0, 0))
    return pl.pallas_call(
        _filt_kernel,
        grid=(L // tm,),
        in_specs=[pl.BlockSpec((tm, ln), lambda i: (i, 0)),
                  full((2 * ln, ln)), full((1, ln)), full((1, ln)),
                  full((ln, ln)), full((1, ln)), full((1, ln)),
                  full((ln, 2 * HYW)), full((ln, 2 * HYW)), full((1, HYW))],
        out_specs=pl.BlockSpec((2, tm, HYW), lambda i: (0, i, 0)),
        out_shape=jax.ShapeDtypeStruct((2, L, HYW), F32),
        compiler_params=_cparams(1, VMEM_LIMIT),
        name="hyena_filter",
    )(zin, w1, b1, f1, w2, b2, f2, w3a, w3b, deltas)


def _strided_dft_kernel(x_hbm, f_ref, o_hbm, xbuf, obuf, sem_in, sem_out):
    g = pl.program_id(0)
    ng = pl.num_programs(0)
    nb = FFT_P // FFT_BB

    def copies(grp, slot, inbound):
        n = grp // nb
        b0 = (grp % nb) * FFT_BB
        if inbound:
            return [pltpu.make_async_copy(x_hbm.at[n, :, b0 + jj, :], xbuf.at[slot, jj], sem_in.at[slot])
                    for jj in range(FFT_BB)]
        return [pltpu.make_async_copy(obuf.at[slot, jj], o_hbm.at[n, :, b0 + jj, :], sem_out.at[slot])
                for jj in range(FFT_BB)]

    def start(grp, slot, inbound):
        for cp in copies(grp, slot, inbound):
            cp.start()

    def wait(grp, slot, inbound):
        for cp in copies(grp, slot, inbound):
            cp.wait()

    @pl.when(g == 0)
    def _():
        start(0, 0, True)

    for slot in range(2):
        grp = 2 * g + slot
        if slot == 0:
            start(grp + 1, 1, True)
        else:
            @pl.when(g + 1 < ng)
            def _():
                start(grp + 1, 0, True)
        wait(grp, slot, True)

        @pl.when(g > 0)
        def _():
            wait(grp - 2, slot, False)

        for jj in range(FFT_BB):
            obuf[slot, jj] = jnp.dot(f_ref[...], xbuf[slot, jj].astype(BF16), preferred_element_type=F32)
        start(grp, slot, False)

    @pl.when(g + 1 == ng)
    def _():
        wait(2 * g, 0, False)
        wait(2 * g + 1, 1, False)


def _strided_dft(xv, fmat, name):
    n, kk = xv.shape[0], xv.shape[1]
    mm = fmat.shape[0]
    groups = n * (FFT_P // FFT_BB)
    return pl.pallas_call(
        _strided_dft_kernel,
        grid=(groups // 2,),
        in_specs=[pl.BlockSpec(memory_space=pl.ANY),
                  pl.BlockSpec((mm, kk), lambda g: (0, 0))],
        out_specs=pl.BlockSpec(memory_space=pl.ANY),
        out_shape=jax.ShapeDtypeStruct((n, mm, FFT_P, HYW), F32),
        scratch_shapes=[pltpu.VMEM((2, FFT_BB, kk, HYW), F32), pltpu.VMEM((2, FFT_BB, mm, HYW), F32),
                        pltpu.SemaphoreType.DMA((2,)), pltpu.SemaphoreType.DMA((2,))],
        compiler_params=_cparams(1, VMEM_LIMIT),
        name=name,
    )(xv, fmat)


def _cblock(mr, mi):
    return jnp.concatenate([jnp.concatenate([mr, -mi], axis=1), jnp.concatenate([mi, mr], axis=1)], axis=0)


FFT_H = FFT_P // 2


def _twiddled(fr_ref, fi_ref, tw):
    twr = tw[0:1, :]
    twi = tw[1:2, :]
    fr = fr_ref[...]
    fi = fi_ref[...]
    return fr * twr - fi * twi, fr * twi + fi * twr


def _cmul_rows(x, kr, ki):
    xr, xi = x[:FFT_P], x[FFT_P:]
    return jnp.concatenate([xr * kr - xi * ki, xr * ki + xi * kr], axis=0).astype(BF16)


_TN_DIMS = (((0,), (0,)), ((), ()))
_STAGE2_SPECS = [pl.BlockSpec((FFT_P, FFT_P), lambda d: (0, 0)),
                 pl.BlockSpec((FFT_P, FFT_P), lambda d: (0, 0)),
                 pl.BlockSpec((None, 2, FFT_P), lambda d: (d, 0, 0)),
                 pl.BlockSpec((2, FFT_P), lambda d: (0, 0))]


def _kspec_kernel(a_ref, fr_ref, fi_ref, tw_ref, twh_ref, o_ref, oh_ref):
    dd = pl.program_id(0)
    dot = functools.partial(jnp.dot, preferred_element_type=F32)

    def combine(xf, xb, out_ref):
        out_ref[0] = xf[:FFT_P] + xb[:FFT_P]
        out_ref[1] = xf[FFT_P:] - xb[FFT_P:]

    @pl.when(dd > 0)
    def _():
        rm = _cblock(*_twiddled(fr_ref, fi_ref, tw_ref[...])).astype(BF16)
        combine(dot(rm, a_ref[0].reshape(2 * FFT_P, HYW).astype(BF16)),
                dot(rm, a_ref[1].reshape(2 * FFT_P, HYW).astype(BF16)), o_ref)

    @pl.when(dd == 0)
    def _():
        for slot, tw, out_ref in ((0, tw_ref[...], o_ref), (1, twh_ref[...], oh_ref)):
            w = jnp.concatenate(_twiddled(fr_ref, fi_ref, tw), axis=0).astype(BF16)
            combine(dot(w, a_ref[0, slot].astype(BF16)), dot(w, a_ref[1, slot].astype(BF16)), out_ref)


def _kernel_spectrum(ak, fr, fi, tw, twh):
    return pl.pallas_call(
        _kspec_kernel,
        grid=(FFT_H,),
        in_specs=[pl.BlockSpec((2, None, 2, FFT_P, HYW), lambda d: (0, d, 0, 0, 0))] + _STAGE2_SPECS,
        out_specs=(pl.BlockSpec((2, None, FFT_P, HYW), lambda d: (0, d, 0, 0)),
                   pl.BlockSpec((2, FFT_P, HYW), lambda d: (0, 0, 0))),
        out_shape=(jax.ShapeDtypeStruct((2, FFT_H, FFT_P, HYW), F32),
                   jax.ShapeDtypeStruct((2, FFT_P, HYW), F32)),
        compiler_params=_cparams(1, VMEM_LIMIT),
        name="kernel_spectrum",
    )(ak, fr, fi, tw, twh)


def _mid_kernel(a_ref, kh_ref, khh_ref, fr_ref, fi_ref, tw_ref, twh_ref, o_ref):
    dd = pl.program_id(0)
    dot = functools.partial(jnp.dot, preferred_element_type=F32)
    dot_t = lambda w, y: lax.dot_general(w, y, _TN_DIMS, preferred_element_type=F32)

    @pl.when(dd > 0)
    def _():
        rm = _cblock(*_twiddled(fr_ref, fi_ref, tw_ref[...])).astype(BF16)
        for n in range(B):
            x = dot(rm, a_ref[n].reshape(2 * FFT_P, HYW).astype(BF16))
            y = _cmul_rows(x, kh_ref[0], kh_ref[1])
            o_ref[n] = dot_t(rm, y).reshape(2, FFT_P, HYW)

    @pl.when(dd == 0)
    def _():
        for slot, tw, k_ref in ((0, tw_ref[...], kh_ref), (1, twh_ref[...], khh_ref)):
            w = jnp.concatenate(_twiddled(fr_ref, fi_ref, tw), axis=0).astype(BF16)
            for n in range(B):
                y = _cmul_rows(dot(w, a_ref[n, slot].astype(BF16)), k_ref[0], k_ref[1])
                o_ref[n, slot] = dot_t(w, y)


def _fft_mid(au, kh, khh, fr, fi, tw, twh):
    pair = pl.BlockSpec((B, None, 2, FFT_P, HYW), lambda d: (0, d, 0, 0, 0))
    return pl.pallas_call(
        _mid_kernel,
        grid=(FFT_H,),
        in_specs=[pair,
                  pl.BlockSpec((2, None, FFT_P, HYW), lambda d: (0, d, 0, 0)),
                  pl.BlockSpec((2, FFT_P, HYW), lambda d: (0, 0, 0))] + _STAGE2_SPECS,
        out_specs=pair,
        out_shape=jax.ShapeDtypeStruct((B, FFT_H, 2, FFT_P, HYW), F32),
        compiler_params=_cparams(1, VMEM_LIMIT),
        name="fft_mid",
    )(au, kh, khh, fr, fi, tw, twh)


def _dft_tables():
    na = L // FFT_P
    a = np.arange(na)
    dd = np.arange(FFT_H)
    ang = 2.0 * np.pi * np.outer(dd, a) / FFT_P
    re_rows = np.cos(ang)
    im_rows = -np.sin(ang)
    im_rows[0] = np.cos(np.pi * a)
    f_first = np.stack([re_rows, im_rows], axis=1).reshape(FFT_P, na)
    gre = 2.0 * np.cos(ang)
    gim = -2.0 * np.sin(ang)
    gre[0] = 1.0
    gim[0] = np.cos(np.pi * a)
    g_last = np.stack([gre, gim], axis=1).reshape(FFT_P, na).T / FFT_N
    b = np.arange(FFT_P)
    angf = 2.0 * np.pi * np.outer(b, b) / FFT_P
    ang2 = 2.0 * np.pi * np.outer(np.arange(FFT_H + 1), b) / FFT_N
    tw = np.stack([np.cos(ang2), -np.sin(ang2)], axis=1)
    f32 = lambda x: jnp.asarray(x.astype(np.float32))
    return (f32(f_first).astype(BF16), f32(g_last).astype(BF16), f32(np.cos(angf)), f32(-np.sin(angf)),
            f32(tw[:FFT_H]), f32(tw[FFT_H]))


def _merge_kernel(ya_ref, cv_ref, vx_ref, x0_ref, ga_ref, gb_ref, x_ref, er_ref, ec_ref, gt1_ref, hyd_ref,
                  wpa_ref, wpb_ref, wo_ref, g2_ref, sh2_ref, sc2_ref, wr_ref, br_ref,
                  x1_ref, t2_ref, lg_ref):
    vx = vx_ref[...]
    yb = x0_ref[...] * (cv_ref[...] + vx * hyd_ref[...])
    pa = jnp.dot(ya_ref[...].astype(BF16), wpa_ref[...], preferred_element_type=F32)
    pb = jnp.dot(yb.astype(BF16), wpb_ref[...], preferred_element_type=F32)
    mixed = (jax.nn.sigmoid(ga_ref[...].astype(F32)) * pa
             + jax.nn.sigmoid(gb_ref[...].astype(F32)) * pb)
    xm = jnp.dot(mixed.astype(BF16), wo_ref[...], preferred_element_type=F32)
    nrow = TM_MG // GRID_W
    row0 = (pl.program_id(0) % (L // TM_MG)) * nrow
    x1 = x_ref[...] + _pos_tile(er_ref, ec_ref, row0, nrow) + gt1_ref[...] * xm
    x1_ref[...] = x1
    t2 = _rms(x1) * g2_ref[...]
    t2 = t2 * (1.0 + sc2_ref[...]) + sh2_ref[...]
    t2_ref[...] = t2
    t_hi = t2.astype(BF16)
    t_lo = (t2 - t_hi.astype(F32)).astype(BF16)
    rr = (jnp.dot(t_hi, wr_ref[...], preferred_element_type=F32)
          + jnp.dot(t_lo, wr_ref[...], preferred_element_type=F32))
    lg_ref[...] = rr[:, :V7X_LANES] + rr[:, V7X_LANES:] + br_ref[...]


def _merge(ya, cv, vx, x0c, p, x2, er, ec, gt1, hyd, wpa, wpb, wo, g2, sh2, sc2, wr, br):
    tpb = L // TM_MG
    half = lambda: pl.BlockSpec((TM_MG, HYW), lambda i: (i, 0))
    full = lambda shape: pl.BlockSpec(shape, lambda i: tuple(0 for _ in shape))
    perb = lambda: pl.BlockSpec((None, 1, D), lambda i: (i // tpb, 0, 0))
    return pl.pallas_call(
        _merge_kernel,
        grid=(T // TM_MG,),
        in_specs=[half(), half(), half(), half(),
                  pl.BlockSpec((TM_MG, D), lambda i: (i, 3)),
                  pl.BlockSpec((TM_MG, D), lambda i: (i, 4)),
                  pl.BlockSpec((TM_MG, D), lambda i: (i, 0)),
                  full((L // GRID_W, D // 2)), full((GRID_W, D // 2)),
                  perb(), full((1, HYW)),
                  full((KD, D)), full((HYW, D)), full((D, D)),
                  full((1, D)), perb(), perb(),
                  full((D, 2 * V7X_LANES)), full((1, V7X_LANES))],
        out_specs=(pl.BlockSpec((TM_MG, D), lambda i: (i, 0)),
                   pl.BlockSpec((TM_MG, D), lambda i: (i, 0)),
                   pl.BlockSpec((TM_MG, V7X_LANES), lambda i: (i, 0))),
        out_shape=(jax.ShapeDtypeStruct((T, D), F32), jax.ShapeDtypeStruct((T, D), F32),
                   jax.ShapeDtypeStruct((T, V7X_LANES), F32)),
        compiler_params=_cparams(1, VMEM_LIMIT),
        name="merge",
    )(ya, cv, vx, x0c, p, p, x2, er, ec, gt1, hyd, wpa, wpb, wo, g2, sh2, sc2, wr, br)


def _route_kernel(lg_ref, info_ref, cnt_ref):
    @pl.when(pl.program_id(0) == 0)
    def _():
        cnt_ref[...] = jnp.zeros_like(cnt_ref)

    lg = lg_ref[...]
    lane = lax.broadcasted_iota(I32, lg.shape, 1)
    lanef = lane.astype(F32)
    neg = -1e30
    big = 1e9
    is_g = (lane >= NEXP) & (lane < NEXP + NGRP)
    gl = jnp.where(is_g, lg, neg)
    ge = jnp.where(is_g, jnp.exp(gl - jnp.max(gl, axis=-1, keepdims=True)), 0.0)
    pg = ge / jnp.sum(ge, axis=-1, keepdims=True)
    p_top_g = jnp.max(pg, axis=-1, keepdims=True)
    gidx = jnp.min(jnp.where(is_g & (pg == p_top_g), lanef, big), axis=-1, keepdims=True)
    g_sel = gidx.astype(I32) - NEXP
    emask = (lane < NEXP) & (jnp.right_shift(lane, NEPG.bit_length() - 1) == g_sel)
    el = jnp.where(emask, lg, neg)
    ee = jnp.where(emask, jnp.exp(el - jnp.max(el, axis=-1, keepdims=True)), 0.0)
    pe = ee / jnp.sum(ee, axis=-1, keepdims=True)
    p1 = jnp.max(jnp.where(emask, pe, -1.0), axis=-1, keepdims=True)
    i1 = jnp.min(jnp.where(emask & (pe == p1), lanef, big), axis=-1, keepdims=True)
    rest = emask & (lanef != i1)
    p2 = jnp.max(jnp.where(rest, pe, -1.0), axis=-1, keepdims=True)
    i2 = jnp.min(jnp.where(rest & (pe == p2), lanef, big), axis=-1, keepdims=True)
    wsum = p1 + p2
    w1 = p_top_g * p1 / wsum
    w2 = p_top_g * p2 / wsum
    sel1 = lanef == i1
    sel2 = lanef == i2
    oh = jnp.where(sel1 | sel2, 1.0, 0.0)
    r = lax.broadcasted_iota(I32, (TR, TR), 0)
    c = lax.broadcasted_iota(I32, (TR, TR), 1)
    stril = jnp.where(c < r, 1.0, 0.0).astype(BF16)
    before = jnp.dot(stril, oh.astype(BF16), preferred_element_type=F32) + cnt_ref[...]
    r1 = jnp.sum(jnp.where(sel1, before, 0.0), axis=-1, keepdims=True)
    r2 = jnp.sum(jnp.where(sel2, before, 0.0), axis=-1, keepdims=True)
    cnt_ref[...] += jnp.sum(oh, axis=0, keepdims=True)
    info = jnp.where(lane == 0, i1, jnp.where(lane == 1, r1, jnp.where(lane == 2, i2, jnp.where(
        lane == 3, r2, jnp.where(lane == 4, w1, jnp.where(lane == 5, w2, 0.0))))))
    info_ref[...] = info


def _route(lg):
    return pl.pallas_call(
        _route_kernel,
        grid=(T // TR,),
        in_specs=[pl.BlockSpec((TR, V7X_LANES), lambda i: (i, 0))],
        out_specs=(pl.BlockSpec((TR, V7X_LANES), lambda i: (i, 0)),
                   pl.BlockSpec((1, V7X_LANES), lambda i: (0, 0))),
        out_shape=(jax.ShapeDtypeStruct((T, V7X_LANES), F32), jax.ShapeDtypeStruct((1, V7X_LANES), F32)),
        compiler_params=_cparams(1, VMEM_LIMIT),
        name="route",
    )(lg)


def _positions_kernel(info_ref, st_ref, o_ref):
    info = info_ref[...]
    lane = lax.broadcasted_iota(I32, info.shape, 1)
    lanef = lane.astype(F32)
    st = st_ref[...]
    row = lambda e_lane, r_lane: (jnp.sum(jnp.where(lanef == _lane_pick(info, lane, e_lane), st, 0.0),
                                          axis=-1, keepdims=True) + _lane_pick(info, lane, r_lane))
    o_ref[...] = jnp.where(lane == 0, row(0, 1), jnp.where(lane == 1, row(2, 3), 0.0)).astype(I32)


def _positions(info, starts_row):
    return pl.pallas_call(
        _positions_kernel,
        grid=(T // TR,),
        in_specs=[pl.BlockSpec((TR, V7X_LANES), lambda i: (i, 0)),
                  pl.BlockSpec((1, V7X_LANES), lambda i: (0, 0))],
        out_specs=pl.BlockSpec((TR, V7X_LANES), lambda i: (i, 0)),
        out_shape=jax.ShapeDtypeStruct((T, V7X_LANES), I32),
        compiler_params=_cparams(1, VMEM_LIMIT),
        name="positions",
    )(info, starts_row)


def _scatter_kernel(pos_ref, zt_ref, t2_ref, zeros_hbm, xs_hbm, sem, zsem):
    i = pl.program_id(0)

    def zcopy(row):
        start = pl.multiple_of(jnp.maximum(row, 0), TE)
        return pltpu.make_async_copy(zeros_hbm, xs_hbm.at[pl.ds(start, TE)], zsem)

    @pl.when(i == 0)
    def _():
        def ztail(start, e, carry):
            @pl.when(zt_ref[0, e] >= 0)
            def _():
                cp = zcopy(zt_ref[0, e])
                cp.start() if start else cp.wait()
            return carry

        lax.fori_loop(0, NEXP, functools.partial(ztail, True), 0)
        lax.fori_loop(0, NEXP, functools.partial(ztail, False), 0)

        def zrest(start, tile, carry):
            cp = zcopy(tile * TE)
            cp.start() if start else cp.wait()
            return carry

        lax.fori_loop(zt_ref[0, NA_OFF], NT_EXP, functools.partial(zrest, True), 0)
        lax.fori_loop(zt_ref[0, NA_OFF], NT_EXP, functools.partial(zrest, False), 0)

    def row_copy(base, jj, kk):
        dst = pos_ref[0, 2 * (base + jj) + kk]
        return pltpu.make_async_copy(t2_ref.at[pl.ds(base, ROW_GROUP)].at[pl.ds(jj, 1)],
                                     xs_hbm.at[pl.ds(dst, 1)], sem)

    _start_rows(TS, row_copy)
    _wait_rows(TS, pltpu.make_async_copy(t2_ref.at[pl.ds(0, 1)], xs_hbm.at[pl.ds(0, 1)], sem))


def _start_rows(n_rows, row_copy):
    def group(g, carry):
        base = pl.multiple_of(g * ROW_GROUP, ROW_GROUP)
        for jj in range(ROW_GROUP):
            for kk in range(2):
                row_copy(base, jj, kk).start(priority=kk)
        return carry

    lax.fori_loop(0, n_rows // ROW_GROUP, group, 0)


def _wait_rows(n_rows, one_row_copy):
    def drain(j, carry):
        one_row_copy.wait()
        one_row_copy.wait()
        return carry

    lax.fori_loop(0, n_rows, drain, 0, unroll=ROW_UNROLL)


def _scatter_rows(pos3, meta, t2, zeros_tile):
    return pl.pallas_call(
        _scatter_kernel,
        grid=(T // TS,),
        in_specs=[pl.BlockSpec((None, 1, 2 * TS), lambda i: (i, 0, 0), memory_space=pltpu.SMEM),
                  pl.BlockSpec(memory_space=pltpu.SMEM),
                  pl.BlockSpec((TS, D), lambda i: (i, 0)),
                  pl.BlockSpec(memory_space=pl.ANY)],
        out_specs=pl.BlockSpec(memory_space=pl.ANY),
        out_shape=jax.ShapeDtypeStruct((NP_ROWS, D), F32),
        scratch_shapes=[pltpu.SemaphoreType.DMA(()), pltpu.SemaphoreType.DMA(())],
        compiler_params=_cparams(1, VMEM_LIMIT),
        name="scatter_rows",
    )(pos3, meta, t2, zeros_tile)


def _expert_kernel(te_ref, na_ref, xs_ref, wg_ref, wu_ref, wd_ref, ys_ref, wgb_ref, wub_ref, wdb_ref):
    i = pl.program_id(0)
    active = i < na_ref[0]

    @pl.when(active & ((i == 0) | (te_ref[i] != te_ref[jnp.maximum(i - 1, 0)])))
    def _():
        wgb_ref[...] = wg_ref[...].astype(BF16)
        wub_ref[...] = wu_ref[...].astype(BF16)
        wdb_ref[...] = wd_ref[...].astype(BF16)

    @pl.when(active)
    def _():
        x = xs_ref[...].astype(BF16)
        g = jnp.dot(x, wgb_ref[...], preferred_element_type=F32)
        u = jnp.dot(x, wub_ref[...], preferred_element_type=F32)
        hid = (g * jax.nn.sigmoid(g) * u).astype(BF16)
        ys_ref[...] = jnp.dot(hid, wdb_ref[...], preferred_element_type=F32)

    @pl.when(jnp.logical_not(active))
    def _():
        ys_ref[...] = jnp.zeros_like(ys_ref)


def _experts(tile_expert, n_active, xs, wg, wu, wd):
    rows = lambda i, te, na: (i, 0)
    wsel = lambda i, te, na: (te[i], 0, 0)
    grid_spec = pltpu.PrefetchScalarGridSpec(
        num_scalar_prefetch=2,
        grid=(NT_EXP,),
        in_specs=[pl.BlockSpec((TE, D), rows),
                  pl.BlockSpec((None, D, DEXP), wsel),
                  pl.BlockSpec((None, D, DEXP), wsel),
                  pl.BlockSpec((None, DEXP, D), wsel)],
        out_specs=pl.BlockSpec((TE, D), rows),
        scratch_shapes=[pltpu.VMEM((D, DEXP), BF16), pltpu.VMEM((D, DEXP), BF16), pltpu.VMEM((DEXP, D), BF16)],
    )
    return pl.pallas_call(
        _expert_kernel,
        grid_spec=grid_spec,
        out_shape=jax.ShapeDtypeStruct((NP_ROWS, D), F32),
        compiler_params=_cparams(1, VMEM_LIMIT),
        name="experts",
    )(tile_expert, n_active, xs, wg, wu, wd)


def _combine_kernel(pos_ref, posn_ref, info_ref, x1_ref, gt2_ref, fg_ref, ys_hbm, o_ref, buf, sem):
    g = pl.program_id(0)
    ng = pl.num_programs(0)

    def gather(p_ref, half, slot):
        def row_copy(base, jj, kk):
            src = p_ref[0, 2 * (half * TC + base + jj) + kk]
            return pltpu.make_async_copy(ys_hbm.at[pl.ds(src, 1)],
                                         buf.at[slot, kk].at[pl.ds(base, ROW_GROUP)].at[pl.ds(jj, 1)],
                                         sem.at[slot])
        _start_rows(TC, row_copy)

    def finish(half, slot):
        _wait_rows(TC, pltpu.make_async_copy(ys_hbm.at[pl.ds(0, 1)], buf.at[slot, 0].at[pl.ds(0, 1)],
                                             sem.at[slot]))
        rows = slice(half * TC, (half + 1) * TC)
        info = info_ref[rows, :]
        lane = lax.broadcasted_iota(I32, info.shape, 1)
        moe = _lane_pick(info, lane, 4) * buf[slot, 0] + _lane_pick(info, lane, 5) * buf[slot, 1]
        x2 = x1_ref[rows, :] + gt2_ref[...] * moe
        o_ref[rows, :] = _rms(x2) * fg_ref[...]

    @pl.when(g == 0)
    def _():
        gather(pos_ref, 0, 0)

    gather(pos_ref, 1, 1)
    finish(0, 0)

    @pl.when(g + 1 < ng)
    def _():
        gather(posn_ref, 0, 0)

    finish(1, 1)


def _combine(pos3, info, x1, gt2, fg, ys):
    step = 2 * TC
    tpb = L // step
    nsteps = T // step
    return pl.pallas_call(
        _combine_kernel,
        grid=(nsteps,),
        in_specs=[pl.BlockSpec((None, 1, 2 * step), lambda i: (i, 0, 0), memory_space=pltpu.SMEM),
                  pl.BlockSpec((None, 1, 2 * step), lambda i: (jnp.minimum(i + 1, nsteps - 1), 0, 0),
                               memory_space=pltpu.SMEM),
                  pl.BlockSpec((step, V7X_LANES), lambda i: (i, 0)),
                  pl.BlockSpec((step, D), lambda i: (i, 0)),
                  pl.BlockSpec((None, 1, D), lambda i: (i // tpb, 0, 0)),
                  pl.BlockSpec((1, D), lambda i: (0, 0)),
                  pl.BlockSpec(memory_space=pl.ANY)],
        out_specs=pl.BlockSpec((step, D), lambda i: (i, 0)),
        out_shape=jax.ShapeDtypeStruct((T, D), F32),
        scratch_shapes=[pltpu.VMEM((2, 2, TC, D), F32), pltpu.SemaphoreType.DMA((2,))],
        compiler_params=_cparams(1, VMEM_LIMIT),
        name="combine",
    )(pos3, pos3, info, x1, gt2, fg, ys)


def _pos_tables():
    rows = L // GRID_W
    quarter = D // 4
    omega = 1.0 / (10000.0 ** (jnp.arange(quarter, dtype=F32) / quarter))

    def axis_emb(pos):
        a = pos[:, None] * omega[None, :]
        return jnp.concatenate([jnp.sin(a), jnp.cos(a)], axis=-1)

    er = axis_emb(jnp.arange(rows, dtype=F32))
    ec = axis_emb(jnp.arange(GRID_W, dtype=F32))
    return er, ec


def _filter_features():
    z = np.zeros((L, V7X_LANES), np.float64)
    bands = (HY_EMB - 1) // 2
    ang = (2.0 * np.pi * np.arange(L) / L)[:, None] * np.linspace(1e-4, bands - 1, bands)[None, :]
    z[:, 0] = np.linspace(0.0, 1.0, L)
    z[:, 1:1 + bands] = np.cos(ang)
    z[:, 1 + bands:HY_EMB] = -np.sin(ang)
    z[1:, HY_EMB] = 1.0
    return jnp.asarray(z.astype(np.float32))


def _pad2(a, rows, cols):
    return jnp.pad(a, ((0, rows - a.shape[0]), (0, cols - a.shape[1])))


def kernel(x, c, ctx, c_ctx, ada_w, ada_b, norm1_g, norm2_g, w_in, hgrn_lb, hgrn_norm_g, hy_conv_w, hy_conv_b, hy_filt_w1, hy_filt_b1, hy_filt_freq1, hy_filt_w2, hy_filt_b2, hy_filt_freq2, hy_filt_w3, hy_d, w_proj_a, w_proj_b, w_out, moe_router_g_w, moe_router_g_b, moe_router_e_w, moe_router_e_b, moe_w_gate, moe_w_up, moe_w_down, final_norm_g):
    cvec = jnp.zeros((8, D), F32).at[0:B].set(c).at[B].set(c_ctx)
    mod = _adaln(cvec, ada_w[0], ada_b[0][None, :])
    m6 = mod.reshape(8, 6, D)
    sh1, sc1, gt1, sh2, sc2, gt2 = [m6[0:B, k][:, None, :] for k in range(6)]
    csh1, csc1 = m6[B:B + 1, 0], m6[B:B + 1, 1]

    lbs = jnp.cumsum(jax.nn.softmax(hgrn_lb.astype(F32), axis=0), axis=0)[0]
    g1 = norm1_g[0][None, :]
    er, ec = _pos_tables()
    x2 = x.reshape(T, D)

    w_ctx = w_in[0][:, KD:4 * KD].astype(BF16)
    s_f, s_b = _context_states(ctx, g1, csh1, csc1, w_ctx, lbs)

    pz, p = _in_projection(x2, er, ec, g1, sh1, sc1, w_in[0].astype(BF16))

    o_f = _hgrn_scan(pz, p, lbs[0:1], s_f, False)
    y_a = _hgrn_scan(pz, p, lbs[1:2], s_b, True, o_f=o_f, norm_g=hgrn_norm_g[0][None, :])

    vx, x0c = _hyena_pre(p, hy_conv_w[0], hy_conv_b[0][None, :])
    deltas = jnp.abs(jnp.linspace(math.log(HY_DECAY_TARGET) / HY_SLOW_PCT,
                                  math.log(HY_DECAY_TARGET) / HY_FAST_PCT, HYW, dtype=F32))[None, :]
    ln = V7X_LANES
    fh = hy_filt_w2.shape[-1]
    blockdiag = lambda m: jnp.concatenate([_pad2(m, m.shape[0], 2 * m.shape[1]),
                                           jnp.pad(m, ((0, 0), (m.shape[1], 0)))], axis=0)
    twice = lambda v: jnp.concatenate([v, v])[None, :]
    w3 = hy_filt_w3[0]
    taps = _filter_taps(
        _filter_features(),
        blockdiag(_pad2(hy_filt_w1[0], ln, fh)), twice(hy_filt_b1[0]), twice(hy_filt_freq1[0]),
        blockdiag(hy_filt_w2[0]), twice(hy_filt_b2[0]), twice(hy_filt_freq2[0]),
        _pad2(w3, ln, 2 * HYW), jnp.pad(w3, ((fh, 0), (0, 0))), deltas)
    f_first, g_last, fr, fi, tw, twh = _dft_tables()
    na = L // FFT_P
    ak = _strided_dft(taps.reshape(2, na, FFT_P, HYW), f_first, "dft_first_taps")
    kh, khh = _kernel_spectrum(ak.reshape(2, FFT_H, 2, FFT_P, HYW), fr, fi, tw, twh)
    au = _strided_dft(vx.reshape(B, na, FFT_P, HYW), f_first, "dft_first")
    bp = _fft_mid(au.reshape(B, FFT_H, 2, FFT_P, HYW), kh, khh, fr, fi, tw, twh)
    conv = _strided_dft(bp.reshape(B, FFT_P, FFT_P, HYW), g_last, "dft_last").reshape(T, HYW)

    wr = jnp.concatenate([jnp.transpose(moe_router_e_w[0], (1, 0, 2)).reshape(D, NEXP),
                          moe_router_g_w[0], jnp.zeros((D, V7X_LANES - NEXP - NGRP), F32)], axis=1)
    wr_hi = wr.astype(BF16)
    wr = jnp.concatenate([wr_hi, (wr - wr_hi.astype(F32)).astype(BF16)], axis=1)
    br = jnp.concatenate([moe_router_e_b[0].reshape(NEXP), moe_router_g_b[0],
                          jnp.zeros((V7X_LANES - NEXP - NGRP,), F32)])[None, :]
    x1, t2, lg = _merge(y_a.reshape(T, KD), conv, vx, x0c, p, x2, er, ec, gt1, hy_d[0][None, :],
                        w_proj_a[0].astype(BF16), w_proj_b[0].astype(BF16), w_out[0].astype(BF16),
                        norm2_g[0][None, :], sh2, sc2, wr, br)

    info, counts = _route(lg)
    cnt = counts[0, :NEXP].astype(I32)
    pc = ((cnt + TE - 1) // TE) * TE
    ends = jnp.cumsum(pc)
    starts = ends - pc
    n_active = (ends[-1] // TE).astype(I32)[None]
    tile_rows = jnp.arange(NT_EXP, dtype=I32) * TE
    tile_expert = jnp.minimum(jnp.sum((ends[None, :] <= tile_rows[:, None]).astype(I32), axis=1), NEXP - 1)
    meta = jnp.concatenate([jnp.where(pc > 0, ends - TE, -1), starts, n_active]).astype(I32)[None, :]
    starts_row = jnp.pad(starts.astype(F32), (0, V7X_LANES - NEXP))[None, :]
    pos3 = _positions(info, starts_row)[:, :2].reshape(T // TS, 1, 2 * TS)

    xs = _scatter_rows(pos3, meta, t2, jnp.zeros((TE, D), F32))
    ys = _experts(tile_expert, n_active, xs,
                  moe_w_gate[0].reshape(NEXP, D, DEXP), moe_w_up[0].reshape(NEXP, D, DEXP),
                  moe_w_down[0].reshape(NEXP, DEXP, D))
    out = _combine(pos3, info, x1, gt2, final_norm_g[None, :], ys)
    return out.reshape(B, L, D)
```

```python
import functools
import math

import numpy as np
import jax
import jax.numpy as jnp
from jax import lax
from jax.experimental import pallas as pl
from jax.experimental.pallas import tpu as pltpu

F32 = jnp.float32
BF16 = jnp.bfloat16
I32 = jnp.int32
HIGHEST = lax.Precision.HIGHEST

D = 1024
B = 2
L = 8192
T = B * L
CTX = 256
GRID_W = 64
EPS = 1e-6
H = 4
DK = 128
DV = 128
KD = H * DK
IN_W = 6144
HYW = 512
HY_EMB = 33
NGRP = 4
NEPG = 8
NEXP = NGRP * NEPG
DEXP = 512
HY_DECAY_TARGET = 1e-2
HY_FAST_PCT = 0.3
HY_SLOW_PCT = 1.5

V7X_LANES = 128
V7X_SUBLANES = 8
V7X_VMEM_BYTES = 64 * 1024 * 1024
VMEM_LIMIT = 48 * 1024 * 1024

FFT_N = 2 * L
FFT_P = 128
FFT_BB = 8

TM_IN = 1024
TN_IN = 1024
TH = 128
CB = 32
TM_HY = 1024
HALO = 2 * V7X_SUBLANES
TM_MG = 256
TR = 512
TE = 512
TPOS = 2048
NP_ROWS = 2 * T + NEXP * TE
NT_EXP = NP_ROWS // TE
TS = 512
TC = 256
ROW_UNROLL = 8
ROW_GROUP = 32
ST_OFF = NEXP
NA_OFF = 2 * NEXP


def _cparams(n_axes, vmem=None):
    return pltpu.CompilerParams(dimension_semantics=("arbitrary",) * n_axes,
                                vmem_limit_bytes=vmem)


def _split3(x):
    hi = x.astype(BF16)
    r = x - hi.astype(F32)
    mid = r.astype(BF16)
    lo = (r - mid.astype(F32)).astype(BF16)
    return hi, mid, lo


def _dot01(m, x):
    hi, mid, lo = _split3(x)
    return (jnp.dot(m, hi, preferred_element_type=F32) + jnp.dot(m, mid, preferred_element_type=F32)
            + jnp.dot(m, lo, preferred_element_type=F32))


def _rms(x):
    return x * lax.rsqrt(jnp.mean(x * x, axis=-1, keepdims=True) + EPS)


def _lane_pick(x, lane, idx):
    return jnp.sum(jnp.where(lane == idx, x, 0.0), axis=-1, keepdims=True)


def _ada_kernel(c_ref, w_ref, b_ref, o_ref):
    c = c_ref[...]
    s = c * jax.nn.sigmoid(c)
    o_ref[...] = jnp.dot(s, w_ref[...], preferred_element_type=F32, precision=HIGHEST) + b_ref[...]


def _adaln(cvec, w, b):
    tn = 1536
    return pl.pallas_call(
        _ada_kernel,
        grid=(6 * D // tn,),
        in_specs=[pl.BlockSpec((8, D), lambda j: (0, 0)),
                  pl.BlockSpec((D, tn), lambda j: (0, j)),
                  pl.BlockSpec((1, tn), lambda j: (0, j))],
        out_specs=pl.BlockSpec((8, tn), lambda j: (0, j)),
        out_shape=jax.ShapeDtypeStruct((8, 6 * D), F32),
        compiler_params=_cparams(1, VMEM_LIMIT),
        name="adaln",
    )(cvec, w, b)


def _keys(z, lb):
    sig = jax.nn.sigmoid(z)
    logf = jnp.log(lb + (1.0 - lb) * sig)
    k = (1.0 - lb) * jax.nn.sigmoid(-z)
    return k, logf


def _ctx_kernel(ctx_ref, g_ref, sh_ref, sc_ref, w_ref, lb_ref, sf_ref, sb_ref):
    h = _rms(ctx_ref[...]) * g_ref[...]
    h = h * (1.0 + sc_ref[...]) + sh_ref[...]
    p = jnp.dot(h.astype(BF16), w_ref[...], preferred_element_type=F32)
    zf, zb, v = p[:, :KD], p[:, KD:2 * KD], p[:, 2 * KD:]
    kf, lf = _keys(zf, lb_ref[0:1, :])
    kb, lbk = _keys(zb, lb_ref[1:2, :])
    r = lax.broadcasted_iota(I32, (CTX, CTX), 0)
    c = lax.broadcasted_iota(I32, (CTX, CTX), 1)
    tril = jnp.where(c <= r, 1.0, 0.0).astype(BF16)
    cf = _dot01(tril, lf)
    cb = _dot01(tril, lbk)
    kfd = (kf * jnp.exp(cf[CTX - 1:CTX, :] - cf)).astype(BF16)
    kbd = (kb * jnp.exp(cb - lbk)).astype(BF16)
    vb = v.astype(BF16)
    tn = (((0,), (0,)), ((), ()))
    for hh in range(H):
        hs = slice(hh * DK, (hh + 1) * DK)
        sf_ref[hh] = lax.dot_general(vb[:, hs], kfd[:, hs], tn, preferred_element_type=F32)
        sb_ref[hh] = lax.dot_general(vb[:, hs], kbd[:, hs], tn, preferred_element_type=F32)


def _context_states(ctx, g1, csh1, csc1, w_ctx, lbs):
    st = jax.ShapeDtypeStruct((B, H, DV, DK), F32)
    return pl.pallas_call(
        _ctx_kernel,
        grid=(B,),
        in_specs=[pl.BlockSpec((None, CTX, D), lambda b: (b, 0, 0)),
                  pl.BlockSpec((1, D), lambda b: (0, 0)),
                  pl.BlockSpec((1, D), lambda b: (0, 0)),
                  pl.BlockSpec((1, D), lambda b: (0, 0)),
                  pl.BlockSpec((D, 3 * KD), lambda b: (0, 0)),
                  pl.BlockSpec((2, KD), lambda b: (0, 0))],
        out_specs=(pl.BlockSpec((None, H, DV, DK), lambda b: (b, 0, 0, 0)),
                   pl.BlockSpec((None, H, DV, DK), lambda b: (b, 0, 0, 0))),
        out_shape=(st, st),
        compiler_params=_cparams(1, VMEM_LIMIT),
        name="ctx_states",
    )(ctx, g1, csh1, csc1, w_ctx, lbs)


def _pos_tile(er_ref, ec_ref, row0, nrow):
    lo = jnp.concatenate([jnp.broadcast_to(er_ref[pl.ds(row0 + i, 1), :], (GRID_W, D // 2))
                          for i in range(nrow)], axis=0)
    hi = jnp.concatenate([ec_ref[...]] * nrow, axis=0)
    return jnp.concatenate([lo, hi], axis=1)


def _inproj_kernel(x_ref, er_ref, ec_ref, g_ref, sh_ref, sc_ref, w_ref, oz_ref, o_ref, hx_ref):
    j = pl.program_id(1)

    @pl.when(j == 0)
    def _():
        nrow = TM_IN // GRID_W
        row0 = (pl.program_id(0) % (L // TM_IN)) * nrow
        h = _rms(x_ref[...] + _pos_tile(er_ref, ec_ref, row0, nrow)) * g_ref[...]
        hx_ref[...] = (h * (1.0 + sc_ref[...]) + sh_ref[...]).astype(BF16)

    r = jnp.dot(hx_ref[...], w_ref[...], preferred_element_type=F32)

    @pl.when(j == 0)
    def _():
        o_ref[:, :KD] = r[:, :KD].astype(BF16)
        oz_ref[:, :KD] = r[:, KD:]

    @pl.when(j == 1)
    def _():
        oz_ref[:, KD:] = r[:, :KD]
        o_ref[:, KD:] = r[:, KD:].astype(BF16)

    @pl.when(j > 1)
    def _():
        o_ref[...] = r.astype(BF16)


def _in_projection(x2, er, ec, g1, sh1, sc1, w_bf):
    tiles_per_batch = L // TM_IN
    return pl.pallas_call(
        _inproj_kernel,
        grid=(T // TM_IN, IN_W // TN_IN),
        in_specs=[pl.BlockSpec((TM_IN, D), lambda i, j: (i, 0)),
                  pl.BlockSpec((L // GRID_W, D // 2), lambda i, j: (0, 0)),
                  pl.BlockSpec((GRID_W, D // 2), lambda i, j: (0, 0)),
                  pl.BlockSpec((1, D), lambda i, j: (0, 0)),
                  pl.BlockSpec((None, 1, D), lambda i, j: (i // tiles_per_batch, 0, 0)),
                  pl.BlockSpec((None, 1, D), lambda i, j: (i // tiles_per_batch, 0, 0)),
                  pl.BlockSpec((D, TN_IN), lambda i, j: (0, j))],
        out_specs=(pl.BlockSpec((TM_IN, TN_IN), lambda i, j: (i, 0)),
                   pl.BlockSpec((TM_IN, TN_IN), lambda i, j: (i, jnp.maximum(j - 1, 0)))),
        out_shape=(jax.ShapeDtypeStruct((T, 2 * KD), F32),
                   jax.ShapeDtypeStruct((T, IN_W - 2 * KD), BF16)),
        scratch_shapes=[pltpu.VMEM((TM_IN, D), BF16)],
        compiler_params=_cparams(2, VMEM_LIMIT),
        name="in_proj",
    )(x2, er, ec, g1, sh1, sc1, w_bf)


def _hgrn_kernel(reverse, readout, *refs):
    if readout:
        q_ref, z_ref, v_ref, lb_ref, s0_ref, of_ref, g_ref, ng_ref, o_ref, st_ref = refs
    else:
        q_ref, z_ref, v_ref, lb_ref, s0_ref, o_ref, st_ref = refs

    @pl.when(pl.program_id(0) == 0)
    def _():
        st_ref[...] = s0_ref[...]

    r = lax.broadcasted_iota(I32, (TH, TH), 0)
    c = lax.broadcasted_iota(I32, (TH, TH), 1)
    cb_shift = CB.bit_length() - 1
    same = jnp.right_shift(r, cb_shift) == jnp.right_shift(c, cb_shift)
    tri_mask = same & ((c >= r) if reverse else (c <= r))
    tri = jnp.where(tri_mask, 1.0, 0.0).astype(BF16)
    rblk = jnp.right_shift(r, cb_shift)
    cblk = jnp.right_shift(c, cb_shift)
    dist = (rblk - cblk) if not reverse else (cblk - rblk)
    for b in range(B):
        _hgrn_chunk(reverse, readout, b, tri, tri_mask, dist, refs)


def _hgrn_chunk(reverse, readout, b, tri, tri_mask, dist, refs):
    if readout:
        q_ref, z_ref, v_ref, lb_ref, s0_ref, of_ref, g_ref, ng_ref, o_ref, st_ref = refs
    else:
        q_ref, z_ref, v_ref, lb_ref, s0_ref, o_ref, st_ref = refs
    q = q_ref[b].astype(F32)
    v = v_ref[b]
    k, logf = _keys(z_ref[b], lb_ref[...])
    bl = _dot01(tri, logf)
    nt = (((1,), (1,)), ((), ()))
    tn = (((0,), (0,)), ((), ()))
    nblk = TH // CB
    e_row = 0 if reverse else CB - 1
    m_row = CB - 1 - CB // 2 if reverse else CB // 2
    tau = [bl[jb * CB + e_row:jb * CB + e_row + 1] for jb in range(nblk)]
    mid = [bl[jb * CB + m_row:jb * CB + m_row + 1] for jb in range(nblk)]
    rows = lambda vecs: jnp.concatenate([jnp.broadcast_to(x, (CB, KD)) for x in vecs], axis=0)
    mid_b = rows(mid)
    qd0 = (q * jnp.exp(bl - mid_b)).astype(BF16)
    kd0 = (k * jnp.exp(mid_b - bl)).astype(BF16)
    qs = q * jnp.exp(bl)
    ke = k * jnp.exp(rows(tau) - bl)
    order = list(range(nblk - 1, -1, -1)) if reverse else list(range(nblk))
    pre = [jnp.zeros((1, KD), F32)]
    for i in range(nblk):
        pre.append(pre[-1] + tau[order[i]])
    total = pre[nblk]
    entry = [None] * nblk
    leave = [None] * nblk
    gap = [[None] * nblk for _ in range(nblk)]
    for i, jb in enumerate(order):
        entry[jb] = jnp.exp(pre[i])
        leave[jb] = jnp.exp(total - pre[i + 1])
        for d in range(2, nblk):
            gap[d][jb] = jnp.exp(pre[i + d] - pre[i + 1]) if i + d < nblk else jnp.zeros((1, KD), F32)
    qc = (qs * rows(entry)).astype(BF16)
    kc = (ke * rows(leave)).astype(BF16)
    kx = jnp.concatenate([ke.astype(BF16)] + [(ke * rows(gap[d])).astype(BF16) for d in range(2, nblk)], axis=0)
    qsb = qs.astype(BF16)
    dec = jnp.exp(total)
    for hh in range(H):
        hs = slice(hh * DK, (hh + 1) * DK)
        sc = lax.dot_general(qd0[:, hs], kd0[:, hs], nt, preferred_element_type=F32)
        sc = jnp.where(tri_mask, sc, 0.0)
        scx = lax.dot_general(qsb[:, hs], kx[:, hs], nt, preferred_element_type=F32)
        for d in range(1, nblk):
            sc = jnp.where(dist == d, scx[:, (d - 1) * TH:d * TH], sc)
        st = st_ref[b, hh]
        o_h = (lax.dot_general(qc[:, hs], st.astype(BF16), nt, preferred_element_type=F32)
               + jnp.dot(sc.astype(BF16), v[:, hs], preferred_element_type=F32))
        st_ref[b, hh] = st * dec[:, hs] + lax.dot_general(v[:, hs], kc[:, hs], tn, preferred_element_type=F32)
        if readout:
            o_h = o_h + of_ref[b, :, hs]
            o_h = _rms(o_h) * ng_ref[...]
            gh = g_ref[b, :, hs].astype(F32)
            o_h = o_h * (gh * jax.nn.sigmoid(gh))
        o_ref[b, :, hs] = o_h


def _hgrn_scan(pz, p, lb_row, s0, reverse, o_f=None, norm_g=None):
    nch = L // TH
    chunk = (lambda c: nch - 1 - c) if reverse else (lambda c: c)
    col_spec = lambda j: pl.BlockSpec((B, TH, KD), lambda c: (0, chunk(c), j))
    in_specs = [col_spec(0), col_spec(1 if reverse else 0), col_spec(1),
                pl.BlockSpec((1, KD), lambda c: (0, 0)),
                pl.BlockSpec((B, H, DV, DK), lambda c: (0, 0, 0, 0))]
    p3 = p.reshape(B, L, p.shape[-1])
    args = [p3, pz.reshape(B, L, 2 * KD), p3, lb_row, s0]
    readout = o_f is not None
    if readout:
        in_specs += [col_spec(0), col_spec(2), pl.BlockSpec((1, DV), lambda c: (0, 0))]
        args += [o_f, p3, norm_g]
    return pl.pallas_call(
        functools.partial(_hgrn_kernel, reverse, readout),
        grid=(nch,),
        in_specs=in_specs,
        out_specs=col_spec(0),
        out_shape=jax.ShapeDtypeStruct((B, L, KD), F32),
        scratch_shapes=[pltpu.VMEM((B, H, DV, DK), F32)],
        compiler_params=_cparams(1, VMEM_LIMIT),
        name="hgrn_bwd_readout" if readout else "hgrn_fwd",
    )(*args)


def _hy_pre_kernel(v_ref, x1_ref, x0_ref, vp_ref, x1p_ref, x0p_ref, vn_ref, x1n_ref, x0n_ref,
                   w_ref, b_ref, vx_ref, x0o_ref):
    i = pl.program_id(1)
    first = i == 0
    last = i == pl.num_programs(1) - 1
    row = lax.broadcasted_iota(I32, (TM_HY, 1), 0)

    def conv(c_ref, p_ref, n_ref, col):
        x = c_ref[...].astype(F32)
        prev_row = jnp.where(first, 0.0, p_ref[...].astype(F32)[HALO - 1:HALO, :])
        next_row = jnp.where(last, 0.0, n_ref[...].astype(F32)[0:1, :])
        xm = jnp.where(row == 0, prev_row, pltpu.roll(x, 1, axis=0))
        xp = jnp.where(row == TM_HY - 1, next_row, pltpu.roll(x, TM_HY - 1, axis=0))
        cs = slice(col * HYW, (col + 1) * HYW)
        return xm * w_ref[0:1, cs] + x * w_ref[1:2, cs] + xp * w_ref[2:3, cs] + b_ref[:, cs]

    v = conv(v_ref, vp_ref, vn_ref, 0)
    x1 = conv(x1_ref, x1p_ref, x1n_ref, 1)
    x0 = conv(x0_ref, x0p_ref, x0n_ref, 2)
    vx_ref[...] = v * x1
    x0o_ref[...] = x0


def _hyena_pre(p, conv_w, conv_b):
    nt = L // TM_HY
    hb = TM_HY // HALO
    nhb = T // HALO
    cur = lambda col: pl.BlockSpec((TM_HY, HYW), lambda b, i: (b * nt + i, col))
    prv = lambda col: pl.BlockSpec((HALO, HYW), lambda b, i: (jnp.maximum((b * nt + i) * hb - 1, 0), col))
    nxt = lambda col: pl.BlockSpec((HALO, HYW), lambda b, i: (jnp.minimum((b * nt + i + 1) * hb, nhb - 1), col))
    c0 = 3
    out = jax.ShapeDtypeStruct((T, HYW), F32)
    return pl.pallas_call(
        _hy_pre_kernel,
        grid=(B, nt),
        in_specs=[cur(c0), cur(c0 + 1), cur(c0 + 2), prv(c0), prv(c0 + 1), prv(c0 + 2),
                  nxt(c0), nxt(c0 + 1), nxt(c0 + 2),
                  pl.BlockSpec((3, 3 * HYW), lambda b, i: (0, 0)),
                  pl.BlockSpec((1, 3 * HYW), lambda b, i: (0, 0))],
        out_specs=(pl.BlockSpec((TM_HY, HYW), lambda b, i: (b * nt + i, 0)),
                   pl.BlockSpec((TM_HY, HYW), lambda b, i: (b * nt + i, 0))),
        out_shape=(out, out),
        compiler_params=_cparams(2, VMEM_LIMIT),
        name="hyena_pre",
    )(p, p, p, p, p, p, p, p, p, conv_w, conv_b)


def _filt_kernel(z_ref, w1_ref, b1_ref, f1_ref, w2_ref, b2_ref, f2_ref, w3a_ref, w3b_ref, dl_ref, o_ref):
    half = z_ref.shape[0] // 2
    zt = z_ref[0:half, :]
    zb = z_ref[half:, :]
    lane = lax.broadcasted_iota(I32, zt.shape, 1)
    dot = functools.partial(jnp.dot, preferred_element_type=F32, precision=HIGHEST)
    h = jnp.sin(f1_ref[...] * (dot(jnp.concatenate([zt, zb], axis=1), w1_ref[...]) + b1_ref[...]))
    h = jnp.sin(f2_ref[...] * (dot(h, w2_ref[...]) + b2_ref[...]))
    for zz, w3_ref, rows in ((zt, w3a_ref, slice(0, half)), (zb, w3b_ref, slice(half, 2 * half))):
        taps = dot(h, w3_ref[...])
        win = jnp.exp(-_lane_pick(zz, lane, 0) * dl_ref[...])
        o_ref[0, rows, :] = taps[:, :HYW] * win
        o_ref[1, rows, :] = taps[:, HYW:] * win * _lane_pick(zz, lane, HY_EMB)


def _filter_taps(zin, w1, b1, f1, w2, b2, f2, w3a, w3b, deltas):
    tm = 2048
    ln = V7X_LANES
    full = lambda shape: pl.BlockSpec(shape, lambda i: (0, 0))
    return pl.pallas_call(
        _filt_kernel,
        grid=(L // tm,),
        in_specs=[pl.BlockSpec((tm, ln), lambda i: (i, 0)),
                  full((2 * ln, ln)), full((1, ln)), full((1, ln)),
                  full((ln, ln)), full((1, ln)), full((1, ln)),
                  full((ln, 2 * HYW)), full((ln, 2 * HYW)), full((1, HYW))],
        out_specs=pl.BlockSpec((2, tm, HYW), lambda i: (0, i, 0)),
        out_shape=jax.ShapeDtypeStruct((2, L, HYW), F32),
        compiler_params=_cparams(1, VMEM_LIMIT),
        name="hyena_filter",
    )(zin, w1, b1, f1, w2, b2, f2, w3a, w3b, deltas)


def _strided_dft_kernel(x_hbm, f_ref, o_hbm, xbuf, obuf, sem_in, sem_out):
    g = pl.program_id(0)
    ng = pl.num_programs(0)
    nb = FFT_P // FFT_BB

    def copies(grp, slot, inbound):
        n = grp // nb
        b0 = (grp % nb) * FFT_BB
        if inbound:
            return [pltpu.make_async_copy(x_hbm.at[n, :, b0 + jj, :], xbuf.at[slot, jj], sem_in.at[slot])
                    for jj in range(FFT_BB)]
        return [pltpu.make_async_copy(obuf.at[slot, jj], o_hbm.at[n, :, b0 + jj, :], sem_out.at[slot])
                for jj in range(FFT_BB)]

    def start(grp, slot, inbound):
        for cp in copies(grp, slot, inbound):
            cp.start()

    def wait(grp, slot, inbound):
        for cp in copies(grp, slot, inbound):
            cp.wait()

    @pl.when(g == 0)
    def _():
        start(0, 0, True)

    for slot in range(2):
        grp = 2 * g + slot
        if slot == 0:
            start(grp + 1, 1, True)
        else:
            @pl.when(g + 1 < ng)
            def _():
                start(grp + 1, 0, True)
        wait(grp, slot, True)

        @pl.when(g > 0)
        def _():
            wait(grp - 2, slot, False)

        for jj in range(FFT_BB):
            obuf[slot, jj] = jnp.dot(f_ref[...], xbuf[slot, jj].astype(BF16), preferred_element_type=F32)
        start(grp, slot, False)

    @pl.when(g + 1 == ng)
    def _():
        wait(2 * g, 0, False)
        wait(2 * g + 1, 1, False)


def _strided_dft(xv, fmat, name):
    n, kk = xv.shape[0], xv.shape[1]
    mm = fmat.shape[0]
    groups = n * (FFT_P // FFT_BB)
    return pl.pallas_call(
        _strided_dft_kernel,
        grid=(groups // 2,),
        in_specs=[pl.BlockSpec(memory_space=pl.ANY),
                  pl.BlockSpec((mm, kk), lambda g: (0, 0))],
        out_specs=pl.BlockSpec(memory_space=pl.ANY),
        out_shape=jax.ShapeDtypeStruct((n, mm, FFT_P, HYW), F32),
        scratch_shapes=[pltpu.VMEM((2, FFT_BB, kk, HYW), F32), pltpu.VMEM((2, FFT_BB, mm, HYW), F32),
                        pltpu.SemaphoreType.DMA((2,)), pltpu.SemaphoreType.DMA((2,))],
        compiler_params=_cparams(1, VMEM_LIMIT),
        name=name,
    )(xv, fmat)


def _cblock(mr, mi):
    return jnp.concatenate([jnp.concatenate([mr, -mi], axis=1), jnp.concatenate([mi, mr], axis=1)], axis=0)


FFT_H = FFT_P // 2


def _twiddled(fr_ref, fi_ref, tw):
    twr = tw[0:1, :]
    twi = tw[1:2, :]
    fr = fr_ref[...]
    fi = fi_ref[...]
    return fr * twr - fi * twi, fr * twi + fi * twr


def _cmul_rows(x, kr, ki):
    xr, xi = x[:FFT_P], x[FFT_P:]
    return jnp.concatenate([xr * kr - xi * ki, xr * ki + xi * kr], axis=0).astype(BF16)


_TN_DIMS = (((0,), (0,)), ((), ()))
_STAGE2_SPECS = [pl.BlockSpec((FFT_P, FFT_P), lambda d: (0, 0)),
                 pl.BlockSpec((FFT_P, FFT_P), lambda d: (0, 0)),
                 pl.BlockSpec((None, 2, FFT_P), lambda d: (d, 0, 0)),
                 pl.BlockSpec((2, FFT_P), lambda d: (0, 0))]


def _kspec_kernel(a_ref, fr_ref, fi_ref, tw_ref, twh_ref, o_ref, oh_ref):
    dd = pl.program_id(0)
    dot = functools.partial(jnp.dot, preferred_element_type=F32)

    def combine(xf, xb, out_ref):
        out_ref[0] = xf[:FFT_P] + xb[:FFT_P]
        out_ref[1] = xf[FFT_P:] - xb[FFT_P:]

    @pl.when(dd > 0)
    def _():
        rm = _cblock(*_twiddled(fr_ref, fi_ref, tw_ref[...])).astype(BF16)
        combine(dot(rm, a_ref[0].reshape(2 * FFT_P, HYW).astype(BF16)),
                dot(rm, a_ref[1].reshape(2 * FFT_P, HYW).astype(BF16)), o_ref)

    @pl.when(dd == 0)
    def _():
        for slot, tw, out_ref in ((0, tw_ref[...], o_ref), (1, twh_ref[...], oh_ref)):
            w = jnp.concatenate(_twiddled(fr_ref, fi_ref, tw), axis=0).astype(BF16)
            combine(dot(w, a_ref[0, slot].astype(BF16)), dot(w, a_ref[1, slot].astype(BF16)), out_ref)


def _kernel_spectrum(ak, fr, fi, tw, twh):
    return pl.pallas_call(
        _kspec_kernel,
        grid=(FFT_H,),
        in_specs=[pl.BlockSpec((2, None, 2, FFT_P, HYW), lambda d: (0, d, 0, 0, 0))] + _STAGE2_SPECS,
        out_specs=(pl.BlockSpec((2, None, FFT_P, HYW), lambda d: (0, d, 0, 0)),
                   pl.BlockSpec((2, FFT_P, HYW), lambda d: (0, 0, 0))),
        out_shape=(jax.ShapeDtypeStruct((2, FFT_H, FFT_P, HYW), F32),
                   jax.ShapeDtypeStruct((2, FFT_P, HYW), F32)),
        compiler_params=_cparams(1, VMEM_LIMIT),
        name="kernel_spectrum",
    )(ak, fr, fi, tw, twh)


def _mid_kernel(a_ref, kh_ref, khh_ref, fr_ref, fi_ref, tw_ref, twh_ref, o_ref):
    dd = pl.program_id(0)
    dot = functools.partial(jnp.dot, preferred_element_type=F32)
    dot_t = lambda w, y: lax.dot_general(w, y, _TN_DIMS, preferred_element_type=F32)

    @pl.when(dd > 0)
    def _():
        rm = _cblock(*_twiddled(fr_ref, fi_ref, tw_ref[...])).astype(BF16)
        for n in range(B):
            x = dot(rm, a_ref[n].reshape(2 * FFT_P, HYW).astype(BF16))
            y = _cmul_rows(x, kh_ref[0], kh_ref[1])
            o_ref[n] = dot_t(rm, y).reshape(2, FFT_P, HYW)

    @pl.when(dd == 0)
    def _():
        for slot, tw, k_ref in ((0, tw_ref[...], kh_ref), (1, twh_ref[...], khh_ref)):
            w = jnp.concatenate(_twiddled(fr_ref, fi_ref, tw), axis=0).astype(BF16)
            for n in range(B):
                y = _cmul_rows(dot(w, a_ref[n, slot].astype(BF16)), k_ref[0], k_ref[1])
                o_ref[n, slot] = dot_t(w, y)


def _fft_mid(au, kh, khh, fr, fi, tw, twh):
    pair = pl.BlockSpec((B, None, 2, FFT_P, HYW), lambda d: (0, d, 0, 0, 0))
    return pl.pallas_call(
        _mid_kernel,
        grid=(FFT_H,),
        in_specs=[pair,
                  pl.BlockSpec((2, None, FFT_P, HYW), lambda d: (0, d, 0, 0)),
                  pl.BlockSpec((2, FFT_P, HYW), lambda d: (0, 0, 0))] + _STAGE2_SPECS,
        out_specs=pair,
        out_shape=jax.ShapeDtypeStruct((B, FFT_H, 2, FFT_P, HYW), F32),
        compiler_params=_cparams(1, VMEM_LIMIT),
        name="fft_mid",
    )(au, kh, khh, fr, fi, tw, twh)


def _dft_tables():
    na = L // FFT_P
    a = np.arange(na)
    dd = np.arange(FFT_H)
    ang = 2.0 * np.pi * np.outer(dd, a) / FFT_P
    re_rows = np.cos(ang)
    im_rows = -np.sin(ang)
    im_rows[0] = np.cos(np.pi * a)
    f_first = np.stack([re_rows, im_rows], axis=1).reshape(FFT_P, na)
    gre = 2.0 * np.cos(ang)
    gim = -2.0 * np.sin(ang)
    gre[0] = 1.0
    gim[0] = np.cos(np.pi * a)
    g_last = np.stack([gre, gim], axis=1).reshape(FFT_P, na).T / FFT_N
    b = np.arange(FFT_P)
    angf = 2.0 * np.pi * np.outer(b, b) / FFT_P
    ang2 = 2.0 * np.pi * np.outer(np.arange(FFT_H + 1), b) / FFT_N
    tw = np.stack([np.cos(ang2), -np.sin(ang2)], axis=1)
    f32 = lambda x: jnp.asarray(x.astype(np.float32))
    return (f32(f_first).astype(BF16), f32(g_last).astype(BF16), f32(np.cos(angf)), f32(-np.sin(angf)),
            f32(tw[:FFT_H]), f32(tw[FFT_H]))


def _merge_kernel(ya_ref, cv_ref, vx_ref, x0_ref, ga_ref, gb_ref, x_ref, er_ref, ec_ref, gt1_ref, hyd_ref,
                  wpa_ref, wpb_ref, wo_ref, g2_ref, sh2_ref, sc2_ref, wr_ref, br_ref,
                  x1_ref, t2_ref, lg_ref):
    vx = vx_ref[...]
    yb = x0_ref[...] * (cv_ref[...] + vx * hyd_ref[...])
    pa = jnp.dot(ya_ref[...].astype(BF16), wpa_ref[...], preferred_element_type=F32)
    pb = jnp.dot(yb.astype(BF16), wpb_ref[...], preferred_element_type=F32)
    mixed = (jax.nn.sigmoid(ga_ref[...].astype(F32)) * pa
             + jax.nn.sigmoid(gb_ref[...].astype(F32)) * pb)
    xm = jnp.dot(mixed.astype(BF16), wo_ref[...], preferred_element_type=F32)
    nrow = TM_MG // GRID_W
    row0 = (pl.program_id(0) % (L // TM_MG)) * nrow
    x1 = x_ref[...] + _pos_tile(er_ref, ec_ref, row0, nrow) + gt1_ref[...] * xm
    x1_ref[...] = x1
    t2 = _rms(x1) * g2_ref[...]
    t2 = t2 * (1.0 + sc2_ref[...]) + sh2_ref[...]
    t2_ref[...] = t2
    t_hi = t2.astype(BF16)
    t_lo = (t2 - t_hi.astype(F32)).astype(BF16)
    rr = (jnp.dot(t_hi, wr_ref[...], preferred_element_type=F32)
          + jnp.dot(t_lo, wr_ref[...], preferred_element_type=F32))
    lg_ref[...] = rr[:, :V7X_LANES] + rr[:, V7X_LANES:] + br_ref[...]


def _merge(ya, cv, vx, x0c, p, x2, er, ec, gt1, hyd, wpa, wpb, wo, g2, sh2, sc2, wr, br):
    tpb = L // TM_MG
    half = lambda: pl.BlockSpec((TM_MG, HYW), lambda i: (i, 0))
    full = lambda shape: pl.BlockSpec(shape, lambda i: tuple(0 for _ in shape))
    perb = lambda: pl.BlockSpec((None, 1, D), lambda i: (i // tpb, 0, 0))
    return pl.pallas_call(
        _merge_kernel,
        grid=(T // TM_MG,),
        in_specs=[half(), half(), half(), half(),
                  pl.BlockSpec((TM_MG, D), lambda i: (i, 3)),
                  pl.BlockSpec((TM_MG, D), lambda i: (i, 4)),
                  pl.BlockSpec((TM_MG, D), lambda i: (i, 0)),
                  full((L // GRID_W, D // 2)), full((GRID_W, D // 2)),
                  perb(), full((1, HYW)),
                  full((KD, D)), full((HYW, D)), full((D, D)),
                  full((1, D)), perb(), perb(),
                  full((D, 2 * V7X_LANES)), full((1, V7X_LANES))],
        out_specs=(pl.BlockSpec((TM_MG, D), lambda i: (i, 0)),
                   pl.BlockSpec((TM_MG, D), lambda i: (i, 0)),
                   pl.BlockSpec((TM_MG, V7X_LANES), lambda i: (i, 0))),
        out_shape=(jax.ShapeDtypeStruct((T, D), F32), jax.ShapeDtypeStruct((T, D), F32),
                   jax.ShapeDtypeStruct((T, V7X_LANES), F32)),
        compiler_params=_cparams(1, VMEM_LIMIT),
        name="merge",
    )(ya, cv, vx, x0c, p, p, x2, er, ec, gt1, hyd, wpa, wpb, wo, g2, sh2, sc2, wr, br)


def _route_kernel(lg_ref, info_ref, cnt_ref):
    @pl.when(pl.program_id(0) == 0)
    def _():
        cnt_ref[...] = jnp.zeros_like(cnt_ref)

    lg = lg_ref[...]
    lane = lax.broadcasted_iota(I32, lg.shape, 1)
    lanef = lane.astype(F32)
    neg = -1e30
    big = 1e9
    is_g = (lane >= NEXP) & (lane < NEXP + NGRP)
    gl = jnp.where(is_g, lg, neg)
    ge = jnp.where(is_g, jnp.exp(gl - jnp.max(gl, axis=-1, keepdims=True)), 0.0)
    pg = ge / jnp.sum(ge, axis=-1, keepdims=True)
    p_top_g = jnp.max(pg, axis=-1, keepdims=True)
    gidx = jnp.min(jnp.where(is_g & (pg == p_top_g), lanef, big), axis=-1, keepdims=True)
    g_sel = gidx.astype(I32) - NEXP
    emask = (lane < NEXP) & (jnp.right_shift(lane, NEPG.bit_length() - 1) == g_sel)
    el = jnp.where(emask, lg, neg)
    ee = jnp.where(emask, jnp.exp(el - jnp.max(el, axis=-1, keepdims=True)), 0.0)
    pe = ee / jnp.sum(ee, axis=-1, keepdims=True)
    p1 = jnp.max(jnp.where(emask, pe, -1.0), axis=-1, keepdims=True)
    i1 = jnp.min(jnp.where(emask & (pe == p1), lanef, big), axis=-1, keepdims=True)
    rest = emask & (lanef != i1)
    p2 = jnp.max(jnp.where(rest, pe, -1.0), axis=-1, keepdims=True)
    i2 = jnp.min(jnp.where(rest & (pe == p2), lanef, big), axis=-1, keepdims=True)
    wsum = p1 + p2
    w1 = p_top_g * p1 / wsum
    w2 = p_top_g * p2 / wsum
    sel1 = lanef == i1
    sel2 = lanef == i2
    oh = jnp.where(sel1 | sel2, 1.0, 0.0)
    r = lax.broadcasted_iota(I32, (TR, TR), 0)
    c = lax.broadcasted_iota(I32, (TR, TR), 1)
    stril = jnp.where(c < r, 1.0, 0.0).astype(BF16)
    before = jnp.dot(stril, oh.astype(BF16), preferred_element_type=F32) + cnt_ref[...]
    r1 = jnp.sum(jnp.where(sel1, before, 0.0), axis=-1, keepdims=True)
    r2 = jnp.sum(jnp.where(sel2, before, 0.0), axis=-1, keepdims=True)
    cnt_ref[...] += jnp.sum(oh, axis=0, keepdims=True)
    info = jnp.where(lane == 0, i1, jnp.where(lane == 1, r1, jnp.where(lane == 2, i2, jnp.where(
        lane == 3, r2, jnp.where(lane == 4, w1, jnp.where(lane == 5, w2, 0.0))))))
    info_ref[...] = info


def _route(lg):
    return pl.pallas_call(
        _route_kernel,
        grid=(T // TR,),
        in_specs=[pl.BlockSpec((TR, V7X_LANES), lambda i: (i, 0))],
        out_specs=(pl.BlockSpec((TR, V7X_LANES), lambda i: (i, 0)),
                   pl.BlockSpec((1, V7X_LANES), lambda i: (0, 0))),
        out_shape=(jax.ShapeDtypeStruct((T, V7X_LANES), F32), jax.ShapeDtypeStruct((1, V7X_LANES), F32)),
        compiler_params=_cparams(1, VMEM_LIMIT),
        name="route",
    )(lg)


def _positions_kernel(info_ref, st_ref, o_ref):
    info = info_ref[...]
    lane = lax.broadcasted_iota(I32, info.shape, 1)
    lanef = lane.astype(F32)
    st = st_ref[...]
    row = lambda e_lane, r_lane: (jnp.sum(jnp.where(lanef == _lane_pick(info, lane, e_lane), st, 0.0),
                                          axis=-1, keepdims=True) + _lane_pick(info, lane, r_lane))
    o_ref[...] = jnp.where(lane == 0, row(0, 1), jnp.where(lane == 1, row(2, 3), 0.0)).astype(I32)


def _positions(info, starts_row):
    return pl.pallas_call(
        _positions_kernel,
        grid=(T // TPOS,),
        in_specs=[pl.BlockSpec((TPOS, V7X_LANES), lambda i: (i, 0)),
                  pl.BlockSpec((1, V7X_LANES), lambda i: (0, 0))],
        out_specs=pl.BlockSpec((TPOS, V7X_LANES), lambda i: (i, 0)),
        out_shape=jax.ShapeDtypeStruct((T, V7X_LANES), I32),
        compiler_params=_cparams(1, VMEM_LIMIT),
        name="positions",
    )(info, starts_row)


def _scatter_kernel(pos_ref, zt_ref, t2_ref, zeros_hbm, xs_hbm, sem, zsem):
    i = pl.program_id(0)

    def zcopy(row):
        start = pl.multiple_of(jnp.maximum(row, 0), TE)
        return pltpu.make_async_copy(zeros_hbm, xs_hbm.at[pl.ds(start, TE)], zsem)

    @pl.when(i == 0)
    def _():
        def ztail(start, e, carry):
            @pl.when(zt_ref[0, e] >= 0)
            def _():
                cp = zcopy(zt_ref[0, e])
                cp.start() if start else cp.wait()
            return carry

        lax.fori_loop(0, NEXP, functools.partial(ztail, True), 0)
        lax.fori_loop(0, NEXP, functools.partial(ztail, False), 0)

        def zrest(start, tile, carry):
            cp = zcopy(tile * TE)
            cp.start() if start else cp.wait()
            return carry

        lax.fori_loop(zt_ref[0, NA_OFF], NT_EXP, functools.partial(zrest, True), 0)
        lax.fori_loop(zt_ref[0, NA_OFF], NT_EXP, functools.partial(zrest, False), 0)

    def row_copy(base, jj, kk):
        dst = pos_ref[0, 2 * (base + jj) + kk]
        return pltpu.make_async_copy(t2_ref.at[pl.ds(base, ROW_GROUP)].at[pl.ds(jj, 1)],
                                     xs_hbm.at[pl.ds(dst, 1)], sem)

    _start_rows(TS, row_copy)
    _wait_rows(TS, pltpu.make_async_copy(t2_ref.at[pl.ds(0, 1)], xs_hbm.at[pl.ds(0, 1)], sem))


def _start_rows(n_rows, row_copy):
    def group(g, carry):
        base = pl.multiple_of(g * ROW_GROUP, ROW_GROUP)
        for jj in range(ROW_GROUP):
            for kk in range(2):
                row_copy(base, jj, kk).start(priority=kk)
        return carry

    lax.fori_loop(0, n_rows // ROW_GROUP, group, 0)


def _wait_rows(n_rows, one_row_copy):
    def drain(j, carry):
        one_row_copy.wait()
        one_row_copy.wait()
        return carry

    lax.fori_loop(0, n_rows, drain, 0, unroll=ROW_UNROLL)


def _scatter_rows(pos3, meta, t2, zeros_tile):
    return pl.pallas_call(
        _scatter_kernel,
        grid=(T // TS,),
        in_specs=[pl.BlockSpec((None, 1, 2 * TS), lambda i: (i, 0, 0), memory_space=pltpu.SMEM),
                  pl.BlockSpec(memory_space=pltpu.SMEM),
                  pl.BlockSpec((TS, D), lambda i: (i, 0)),
                  pl.BlockSpec(memory_space=pl.ANY)],
        out_specs=pl.BlockSpec(memory_space=pl.ANY),
        out_shape=jax.ShapeDtypeStruct((NP_ROWS, D), F32),
        scratch_shapes=[pltpu.SemaphoreType.DMA(()), pltpu.SemaphoreType.DMA(())],
        compiler_params=_cparams(1, VMEM_LIMIT),
        name="scatter_rows",
    )(pos3, meta, t2, zeros_tile)


def _expert_kernel(te_ref, na_ref, xs_ref, wg_ref, wu_ref, wd_ref, ys_ref, wgb_ref, wub_ref, wdb_ref):
    i = pl.program_id(0)
    active = i < na_ref[0]

    @pl.when(active & ((i == 0) | (te_ref[i] != te_ref[jnp.maximum(i - 1, 0)])))
    def _():
        wgb_ref[...] = wg_ref[...].astype(BF16)
        wub_ref[...] = wu_ref[...].astype(BF16)
        wdb_ref[...] = wd_ref[...].astype(BF16)

    @pl.when(active)
    def _():
        x = xs_ref[...].astype(BF16)
        g = jnp.dot(x, wgb_ref[...], preferred_element_type=F32)
        u = jnp.dot(x, wub_ref[...], preferred_element_type=F32)
        hid = (g * jax.nn.sigmoid(g) * u).astype(BF16)
        ys_ref[...] = jnp.dot(hid, wdb_ref[...], preferred_element_type=F32)

    @pl.when(jnp.logical_not(active))
    def _():
        ys_ref[...] = jnp.zeros_like(ys_ref)


def _experts(tile_expert, n_active, xs, wg, wu, wd):
    rows = lambda i, te, na: (i, 0)
    wsel = lambda i, te, na: (te[i], 0, 0)
    grid_spec = pltpu.PrefetchScalarGridSpec(
        num_scalar_prefetch=2,
        grid=(NT_EXP,),
        in_specs=[pl.BlockSpec((TE, D), rows),
                  pl.BlockSpec((None, D, DEXP), wsel),
                  pl.BlockSpec((None, D, DEXP), wsel),
                  pl.BlockSpec((None, DEXP, D), wsel)],
        out_specs=pl.BlockSpec((TE, D), rows),
        scratch_shapes=[pltpu.VMEM((D, DEXP), BF16), pltpu.VMEM((D, DEXP), BF16), pltpu.VMEM((DEXP, D), BF16)],
    )
    return pl.pallas_call(
        _expert_kernel,
        grid_spec=grid_spec,
        out_shape=jax.ShapeDtypeStruct((NP_ROWS, D), F32),
        compiler_params=_cparams(1, VMEM_LIMIT),
        name="experts",
    )(tile_expert, n_active, xs, wg, wu, wd)


def _combine_kernel(pos_ref, posn_ref, info_ref, x1_ref, gt2_ref, fg_ref, ys_hbm, o_ref, buf, sem):
    g = pl.program_id(0)
    ng = pl.num_programs(0)

    def gather(p_ref, half, slot):
        def row_copy(base, jj, kk):
            src = p_ref[0, 2 * (half * TC + base + jj) + kk]
            return pltpu.make_async_copy(ys_hbm.at[pl.ds(src, 1)],
                                         buf.at[slot, kk].at[pl.ds(base, ROW_GROUP)].at[pl.ds(jj, 1)],
                                         sem.at[slot])
        _start_rows(TC, row_copy)

    def finish(half, slot):
        _wait_rows(TC, pltpu.make_async_copy(ys_hbm.at[pl.ds(0, 1)], buf.at[slot, 0].at[pl.ds(0, 1)],
                                             sem.at[slot]))
        rows = slice(half * TC, (half + 1) * TC)
        info = info_ref[rows, :]
        lane = lax.broadcasted_iota(I32, info.shape, 1)
        moe = _lane_pick(info, lane, 4) * buf[slot, 0] + _lane_pick(info, lane, 5) * buf[slot, 1]
        x2 = x1_ref[rows, :] + gt2_ref[...] * moe
        o_ref[rows, :] = _rms(x2) * fg_ref[...]

    @pl.when(g == 0)
    def _():
        gather(pos_ref, 0, 0)

    gather(pos_ref, 1, 1)
    finish(0, 0)

    @pl.when(g + 1 < ng)
    def _():
        gather(posn_ref, 0, 0)

    finish(1, 1)


def _combine(pos3, info, x1, gt2, fg, ys):
    step = 2 * TC
    tpb = L // step
    nsteps = T // step
    return pl.pallas_call(
        _combine_kernel,
        grid=(nsteps,),
        in_specs=[pl.BlockSpec((None, 1, 2 * step), lambda i: (i, 0, 0), memory_space=pltpu.SMEM),
                  pl.BlockSpec((None, 1, 2 * step), lambda i: (jnp.minimum(i + 1, nsteps - 1), 0, 0),
                               memory_space=pltpu.SMEM),
                  pl.BlockSpec((step, V7X_LANES), lambda i: (i, 0)),
                  pl.BlockSpec((step, D), lambda i: (i, 0)),
                  pl.BlockSpec((None, 1, D), lambda i: (i // tpb, 0, 0)),
                  pl.BlockSpec((1, D), lambda i: (0, 0)),
                  pl.BlockSpec(memory_space=pl.ANY)],
        out_specs=pl.BlockSpec((step, D), lambda i: (i, 0)),
        out_shape=jax.ShapeDtypeStruct((T, D), F32),
        scratch_shapes=[pltpu.VMEM((2, 2, TC, D), F32), pltpu.SemaphoreType.DMA((2,))],
        compiler_params=_cparams(1, VMEM_LIMIT),
        name="combine",
    )(pos3, pos3, info, x1, gt2, fg, ys)


def _pos_tables():
    rows = L // GRID_W
    quarter = D // 4
    omega = 1.0 / (10000.0 ** (jnp.arange(quarter, dtype=F32) / quarter))

    def axis_emb(pos):
        a = pos[:, None] * omega[None, :]
        return jnp.concatenate([jnp.sin(a), jnp.cos(a)], axis=-1)

    er = axis_emb(jnp.arange(rows, dtype=F32))
    ec = axis_emb(jnp.arange(GRID_W, dtype=F32))
    return er, ec


def _filter_features():
    z = np.zeros((L, V7X_LANES), np.float64)
    bands = (HY_EMB - 1) // 2
    ang = (2.0 * np.pi * np.arange(L) / L)[:, None] * np.linspace(1e-4, bands - 1, bands)[None, :]
    z[:, 0] = np.linspace(0.0, 1.0, L)
    z[:, 1:1 + bands] = np.cos(ang)
    z[:, 1 + bands:HY_EMB] = -np.sin(ang)
    z[1:, HY_EMB] = 1.0
    return jnp.asarray(z.astype(np.float32))


def _pad2(a, rows, cols):
    return jnp.pad(a, ((0, rows - a.shape[0]), (0, cols - a.shape[1])))


def kernel(x, c, ctx, c_ctx, ada_w, ada_b, norm1_g, norm2_g, w_in, hgrn_lb, hgrn_norm_g, hy_conv_w, hy_conv_b, hy_filt_w1, hy_filt_b1, hy_filt_freq1, hy_filt_w2, hy_filt_b2, hy_filt_freq2, hy_filt_w3, hy_d, w_proj_a, w_proj_b, w_out, moe_router_g_w, moe_router_g_b, moe_router_e_w, moe_router_e_b, moe_w_gate, moe_w_up, moe_w_down, final_norm_g):
    cvec = jnp.zeros((8, D), F32).at[0:B].set(c).at[B].set(c_ctx)
    mod = _adaln(cvec, ada_w[0], ada_b[0][None, :])
    m6 = mod.reshape(8, 6, D)
    sh1, sc1, gt1, sh2, sc2, gt2 = [m6[0:B, k][:, None, :] for k in range(6)]
    csh1, csc1 = m6[B:B + 1, 0], m6[B:B + 1, 1]

    lbs = jnp.cumsum(jax.nn.softmax(hgrn_lb.astype(F32), axis=0), axis=0)[0]
    g1 = norm1_g[0][None, :]
    er, ec = _pos_tables()
    x2 = x.reshape(T, D)

    w_ctx = w_in[0][:, KD:4 * KD].astype(BF16)
    s_f, s_b = _context_states(ctx, g1, csh1, csc1, w_ctx, lbs)

    pz, p = _in_projection(x2, er, ec, g1, sh1, sc1, w_in[0].astype(BF16))

    o_f = _hgrn_scan(pz, p, lbs[0:1], s_f, False)
    y_a = _hgrn_scan(pz, p, lbs[1:2], s_b, True, o_f=o_f, norm_g=hgrn_norm_g[0][None, :])

    vx, x0c = _hyena_pre(p, hy_conv_w[0], hy_conv_b[0][None, :])
    deltas = jnp.abs(jnp.linspace(math.log(HY_DECAY_TARGET) / HY_SLOW_PCT,
                                  math.log(HY_DECAY_TARGET) / HY_FAST_PCT, HYW, dtype=F32))[None, :]
    ln = V7X_LANES
    fh = hy_filt_w2.shape[-1]
    blockdiag = lambda m: jnp.concatenate([_pad2(m, m.shape[0], 2 * m.shape[1]),
                                           jnp.pad(m, ((0, 0), (m.shape[1], 0)))], axis=0)
    twice = lambda v: jnp.concatenate([v, v])[None, :]
    w3 = hy_filt_w3[0]
    taps = _filter_taps(
        _filter_features(),
        blockdiag(_pad2(hy_filt_w1[0], ln, fh)), twice(hy_filt_b1[0]), twice(hy_filt_freq1[0]),
        blockdiag(hy_filt_w2[0]), twice(hy_filt_b2[0]), twice(hy_filt_freq2[0]),
        _pad2(w3, ln, 2 * HYW), jnp.pad(w3, ((fh, 0), (0, 0))), deltas)
    f_first, g_last, fr, fi, tw, twh = _dft_tables()
    na = L // FFT_P
    ak = _strided_dft(taps.reshape(2, na, FFT_P, HYW), f_first, "dft_first_taps")
    kh, khh = _kernel_spectrum(ak.reshape(2, FFT_H, 2, FFT_P, HYW), fr, fi, tw, twh)
    au = _strided_dft(vx.reshape(B, na, FFT_P, HYW), f_first, "dft_first")
    bp = _fft_mid(au.reshape(B, FFT_H, 2, FFT_P, HYW), kh, khh, fr, fi, tw, twh)
    conv = _strided_dft(bp.reshape(B, FFT_P, FFT_P, HYW), g_last, "dft_last").reshape(T, HYW)

    wr = jnp.concatenate([jnp.transpose(moe_router_e_w[0], (1, 0, 2)).reshape(D, NEXP),
                          moe_router_g_w[0], jnp.zeros((D, V7X_LANES - NEXP - NGRP), F32)], axis=1)
    wr_hi = wr.astype(BF16)
    wr = jnp.concatenate([wr_hi, (wr - wr_hi.astype(F32)).astype(BF16)], axis=1)
    br = jnp.concatenate([moe_router_e_b[0].reshape(NEXP), moe_router_g_b[0],
                          jnp.zeros((V7X_LANES - NEXP - NGRP,), F32)])[None, :]
    x1, t2, lg = _merge(y_a.reshape(T, KD), conv, vx, x0c, p, x2, er, ec, gt1, hy_d[0][None, :],
                        w_proj_a[0].astype(BF16), w_proj_b[0].astype(BF16), w_out[0].astype(BF16),
                        norm2_g[0][None, :], sh2, sc2, wr, br)

    info, counts = _route(lg)
    cnt = counts[0, :NEXP].astype(I32)
    pc = ((cnt + TE - 1) // TE) * TE
    ends = jnp.cumsum(pc)
    starts = ends - pc
    n_active = (ends[-1] // TE).astype(I32)[None]
    tile_rows = jnp.arange(NT_EXP, dtype=I32) * TE
    tile_expert = jnp.minimum(jnp.sum((ends[None, :] <= tile_rows[:, None]).astype(I32), axis=1), NEXP - 1)
    meta = jnp.concatenate([jnp.where(pc > 0, ends - TE, -1), starts, n_active]).astype(I32)[None, :]
    starts_row = jnp.pad(starts.astype(F32), (0, V7X_LANES - NEXP))[None, :]
    pos3 = _positions(info, starts_row)[:, :2].reshape(T // TS, 1, 2 * TS)

    xs = _scatter_rows(pos3, meta, t2, jnp.zeros((TE, D), F32))
    ys = _experts(tile_expert, n_active, xs,
                  moe_w_gate[0].reshape(NEXP, D, DEXP), moe_w_up[0].reshape(NEXP, D, DEXP),
                  moe_w_down[0].reshape(NEXP, DEXP, D))
    out = _combine(pos3, info, x1, gt2, final_norm_g[None, :], ys)
    return out.reshape(B, L, D)
```

```python
import functools
import math

import numpy as np
import jax
import jax.numpy as jnp
from jax import lax
from jax.experimental import pallas as pl
from jax.experimental.pallas import tpu as pltpu

F32 = jnp.float32
BF16 = jnp.bfloat16
I32 = jnp.int32
HIGHEST = lax.Precision.HIGHEST

D = 1024
B = 2
L = 8192
T = B * L
CTX = 256
GRID_W = 64
EPS = 1e-6
H = 4
DK = 128
DV = 128
KD = H * DK
IN_W = 6144
HYW = 512
HY_EMB = 33
NGRP = 4
NEPG = 8
NEXP = NGRP * NEPG
DEXP = 512
HY_DECAY_TARGET = 1e-2
HY_FAST_PCT = 0.3
HY_SLOW_PCT = 1.5

V7X_LANES = 128
V7X_SUBLANES = 8
V7X_VMEM_BYTES = 64 * 1024 * 1024
VMEM_LIMIT = (3 * V7X_VMEM_BYTES) // 4

FFT_N = 2 * L
FFT_P = 128
FFT_BB = 8

TM_IN = 1024
TN_IN = 1024
TH = 128
CB = 32
TM_HY = 1024
HALO = 2 * V7X_SUBLANES
TM_MG = 512
TR = 512
TE = 512
TPOS = 2048
NP_ROWS = 2 * T + NEXP * TE
NT_EXP = NP_ROWS // TE
TS = 512
TC = 256
ROW_UNROLL = 8
ROW_GROUP = 32
ST_OFF = NEXP
NA_OFF = 2 * NEXP


def _cparams(n_axes, vmem=None):
    return pltpu.CompilerParams(dimension_semantics=("arbitrary",) * n_axes,
                                vmem_limit_bytes=vmem)


def _split3(x):
    hi = x.astype(BF16)
    r = x - hi.astype(F32)
    mid = r.astype(BF16)
    lo = (r - mid.astype(F32)).astype(BF16)
    return hi, mid, lo


def _dot01(m, x):
    hi, mid, lo = _split3(x)
    return (jnp.dot(m, hi, preferred_element_type=F32) + jnp.dot(m, mid, preferred_element_type=F32)
            + jnp.dot(m, lo, preferred_element_type=F32))


def _rms(x):
    return x * lax.rsqrt(jnp.mean(x * x, axis=-1, keepdims=True) + EPS)


def _lane_pick(x, lane, idx):
    return jnp.sum(jnp.where(lane == idx, x, 0.0), axis=-1, keepdims=True)


def _ada_kernel(c_ref, w_ref, b_ref, o_ref):
    c = c_ref[...]
    s = c * jax.nn.sigmoid(c)
    o_ref[...] = jnp.dot(s, w_ref[...], preferred_element_type=F32, precision=HIGHEST) + b_ref[...]


def _adaln(cvec, w, b):
    tn = 1536
    return pl.pallas_call(
        _ada_kernel,
        grid=(6 * D // tn,),
        in_specs=[pl.BlockSpec((8, D), lambda j: (0, 0)),
                  pl.BlockSpec((D, tn), lambda j: (0, j)),
                  pl.BlockSpec((1, tn), lambda j: (0, j))],
        out_specs=pl.BlockSpec((8, tn), lambda j: (0, j)),
        out_shape=jax.ShapeDtypeStruct((8, 6 * D), F32),
        compiler_params=_cparams(1, VMEM_LIMIT),
        name="adaln",
    )(cvec, w, b)


def _keys(z, lb):
    sig = jax.nn.sigmoid(z)
    logf = jnp.log(lb + (1.0 - lb) * sig)
    k = (1.0 - lb) * jax.nn.sigmoid(-z)
    return k, logf


def _ctx_kernel(ctx_ref, g_ref, sh_ref, sc_ref, w_ref, lb_ref, sf_ref, sb_ref):
    h = _rms(ctx_ref[...]) * g_ref[...]
    h = h * (1.0 + sc_ref[...]) + sh_ref[...]
    p = jnp.dot(h.astype(BF16), w_ref[...], preferred_element_type=F32)
    zf, zb, v = p[:, :KD], p[:, KD:2 * KD], p[:, 2 * KD:]
    kf, lf = _keys(zf, lb_ref[0:1, :])
    kb, lbk = _keys(zb, lb_ref[1:2, :])
    r = lax.broadcasted_iota(I32, (CTX, CTX), 0)
    c = lax.broadcasted_iota(I32, (CTX, CTX), 1)
    tril = jnp.where(c <= r, 1.0, 0.0).astype(BF16)
    cf = _dot01(tril, lf)
    cb = _dot01(tril, lbk)
    kfd = (kf * jnp.exp(cf[CTX - 1:CTX, :] - cf)).astype(BF16)
    kbd = (kb * jnp.exp(cb - lbk)).astype(BF16)
    vb = v.astype(BF16)
    tn = (((0,), (0,)), ((), ()))
    for hh in range(H):
        hs = slice(hh * DK, (hh + 1) * DK)
        sf_ref[hh] = lax.dot_general(vb[:, hs], kfd[:, hs], tn, preferred_element_type=F32)
        sb_ref[hh] = lax.dot_general(vb[:, hs], kbd[:, hs], tn, preferred_element_type=F32)


def _context_states(ctx, g1, csh1, csc1, w_ctx, lbs):
    st = jax.ShapeDtypeStruct((B, H, DV, DK), F32)
    return pl.pallas_call(
        _ctx_kernel,
        grid=(B,),
        in_specs=[pl.BlockSpec((None, CTX, D), lambda b: (b, 0, 0)),
                  pl.BlockSpec((1, D), lambda b: (0, 0)),
                  pl.BlockSpec((1, D), lambda b: (0, 0)),
                  pl.BlockSpec((1, D), lambda b: (0, 0)),
                  pl.BlockSpec((D, 3 * KD), lambda b: (0, 0)),
                  pl.BlockSpec((2, KD), lambda b: (0, 0))],
        out_specs=(pl.BlockSpec((None, H, DV, DK), lambda b: (b, 0, 0, 0)),
                   pl.BlockSpec((None, H, DV, DK), lambda b: (b, 0, 0, 0))),
        out_shape=(st, st),
        compiler_params=_cparams(1, VMEM_LIMIT),
        name="ctx_states",
    )(ctx, g1, csh1, csc1, w_ctx, lbs)


def _pos_tile(er_ref, ec_ref, row0, nrow):
    lo = jnp.concatenate([jnp.broadcast_to(er_ref[pl.ds(row0 + i, 1), :], (GRID_W, D // 2))
                          for i in range(nrow)], axis=0)
    hi = jnp.concatenate([ec_ref[...]] * nrow, axis=0)
    return jnp.concatenate([lo, hi], axis=1)


def _inproj_kernel(x_ref, er_ref, ec_ref, g_ref, sh_ref, sc_ref, w_ref, oz_ref, o_ref, hx_ref):
    j = pl.program_id(1)

    @pl.when(j == 0)
    def _():
        nrow = TM_IN // GRID_W
        row0 = (pl.program_id(0) % (L // TM_IN)) * nrow
        h = _rms(x_ref[...] + _pos_tile(er_ref, ec_ref, row0, nrow)) * g_ref[...]
        hx_ref[...] = (h * (1.0 + sc_ref[...]) + sh_ref[...]).astype(BF16)

    r = jnp.dot(hx_ref[...], w_ref[...], preferred_element_type=F32)

    @pl.when(j == 0)
    def _():
        o_ref[:, :KD] = r[:, :KD].astype(BF16)
        oz_ref[:, :KD] = r[:, KD:]

    @pl.when(j == 1)
    def _():
        oz_ref[:, KD:] = r[:, :KD]
        o_ref[:, KD:] = r[:, KD:].astype(BF16)

    @pl.when(j > 1)
    def _():
        o_ref[...] = r.astype(BF16)


def _in_projection(x2, er, ec, g1, sh1, sc1, w_bf):
    tiles_per_batch = L // TM_IN
    return pl.pallas_call(
        _inproj_kernel,
        grid=(T // TM_IN, IN_W // TN_IN),
        in_specs=[pl.BlockSpec((TM_IN, D), lambda i, j: (i, 0)),
                  pl.BlockSpec((L // GRID_W, D // 2), lambda i, j: (0, 0)),
                  pl.BlockSpec((GRID_W, D // 2), lambda i, j: (0, 0)),
                  pl.BlockSpec((1, D), lambda i, j: (0, 0)),
                  pl.BlockSpec((None, 1, D), lambda i, j: (i // tiles_per_batch, 0, 0)),
                  pl.BlockSpec((None, 1, D), lambda i, j: (i // tiles_per_batch, 0, 0)),
                  pl.BlockSpec((D, TN_IN), lambda i, j: (0, j))],
        out_specs=(pl.BlockSpec((TM_IN, TN_IN), lambda i, j: (i, 0)),
                   pl.BlockSpec((TM_IN, TN_IN), lambda i, j: (i, jnp.maximum(j - 1, 0)))),
        out_shape=(jax.ShapeDtypeStruct((T, 2 * KD), F32),
                   jax.ShapeDtypeStruct((T, IN_W - 2 * KD), BF16)),
        scratch_shapes=[pltpu.VMEM((TM_IN, D), BF16)],
        compiler_params=_cparams(2, VMEM_LIMIT),
        name="in_proj",
    )(x2, er, ec, g1, sh1, sc1, w_bf)


def _hgrn_kernel(reverse, readout, *refs):
    if readout:
        q_ref, z_ref, v_ref, lb_ref, s0_ref, of_ref, g_ref, ng_ref, o_ref, st_ref = refs
    else:
        q_ref, z_ref, v_ref, lb_ref, s0_ref, o_ref, st_ref = refs

    @pl.when(pl.program_id(0) == 0)
    def _():
        st_ref[...] = s0_ref[...]

    r = lax.broadcasted_iota(I32, (TH, TH), 0)
    c = lax.broadcasted_iota(I32, (TH, TH), 1)
    cb_shift = CB.bit_length() - 1
    same = jnp.right_shift(r, cb_shift) == jnp.right_shift(c, cb_shift)
    tri_mask = same & ((c >= r) if reverse else (c <= r))
    tri = jnp.where(tri_mask, 1.0, 0.0).astype(BF16)
    rblk = jnp.right_shift(r, cb_shift)
    cblk = jnp.right_shift(c, cb_shift)
    dist = (rblk - cblk) if not reverse else (cblk - rblk)
    for b in range(B):
        _hgrn_chunk(reverse, readout, b, tri, tri_mask, dist, refs)


def _hgrn_chunk(reverse, readout, b, tri, tri_mask, dist, refs):
    if readout:
        q_ref, z_ref, v_ref, lb_ref, s0_ref, of_ref, g_ref, ng_ref, o_ref, st_ref = refs
    else:
        q_ref, z_ref, v_ref, lb_ref, s0_ref, o_ref, st_ref = refs
    q = q_ref[b].astype(F32)
    v = v_ref[b]
    k, logf = _keys(z_ref[b], lb_ref[...])
    bl = _dot01(tri, logf)
    nt = (((1,), (1,)), ((), ()))
    tn = (((0,), (0,)), ((), ()))
    nblk = TH // CB
    e_row = 0 if reverse else CB - 1
    m_row = CB - 1 - CB // 2 if reverse else CB // 2
    tau = [bl[jb * CB + e_row:jb * CB + e_row + 1] for jb in range(nblk)]
    mid = [bl[jb * CB + m_row:jb * CB + m_row + 1] for jb in range(nblk)]
    rows = lambda vecs: jnp.concatenate([jnp.broadcast_to(x, (CB, KD)) for x in vecs], axis=0)
    mid_b = rows(mid)
    qd0 = (q * jnp.exp(bl - mid_b)).astype(BF16)
    kd0 = (k * jnp.exp(mid_b - bl)).astype(BF16)
    qs = q * jnp.exp(bl)
    ke = k * jnp.exp(rows(tau) - bl)
    order = list(range(nblk - 1, -1, -1)) if reverse else list(range(nblk))
    pre = [jnp.zeros((1, KD), F32)]
    for i in range(nblk):
        pre.append(pre[-1] + tau[order[i]])
    total = pre[nblk]
    entry = [None] * nblk
    leave = [None] * nblk
    gap = [[None] * nblk for _ in range(nblk)]
    for i, jb in enumerate(order):
        entry[jb] = jnp.exp(pre[i])
        leave[jb] = jnp.exp(total - pre[i + 1])
        for d in range(2, nblk):
            gap[d][jb] = jnp.exp(pre[i + d] - pre[i + 1]) if i + d < nblk else jnp.zeros((1, KD), F32)
    qc = (qs * rows(entry)).astype(BF16)
    kc = (ke * rows(leave)).astype(BF16)
    kx = jnp.concatenate([ke.astype(BF16)] + [(ke * rows(gap[d])).astype(BF16) for d in range(2, nblk)], axis=0)
    qsb = qs.astype(BF16)
    dec = jnp.exp(total)
    for hh in range(H):
        hs = slice(hh * DK, (hh + 1) * DK)
        sc = lax.dot_general(qd0[:, hs], kd0[:, hs], nt, preferred_element_type=F32)
        sc = jnp.where(tri_mask, sc, 0.0)
        scx = lax.dot_general(qsb[:, hs], kx[:, hs], nt, preferred_element_type=F32)
        for d in range(1, nblk):
            sc = jnp.where(dist == d, scx[:, (d - 1) * TH:d * TH], sc)
        st = st_ref[b, hh]
        o_h = (lax.dot_general(qc[:, hs], st.astype(BF16), nt, preferred_element_type=F32)
               + jnp.dot(sc.astype(BF16), v[:, hs], preferred_element_type=F32))
        st_ref[b, hh] = st * dec[:, hs] + lax.dot_general(v[:, hs], kc[:, hs], tn, preferred_element_type=F32)
        if readout:
            o_h = o_h + of_ref[b, :, hs]
            o_h = _rms(o_h) * ng_ref[...]
            gh = g_ref[b, :, hs].astype(F32)
            o_h = o_h * (gh * jax.nn.sigmoid(gh))
        o_ref[b, :, hs] = o_h


def _hgrn_scan(pz, p, lb_row, s0, reverse, o_f=None, norm_g=None):
    nch = L // TH
    chunk = (lambda c: nch - 1 - c) if reverse else (lambda c: c)
    col_spec = lambda j: pl.BlockSpec((B, TH, KD), lambda c: (0, chunk(c), j))
    in_specs = [col_spec(0), col_spec(1 if reverse else 0), col_spec(1),
                pl.BlockSpec((1, KD), lambda c: (0, 0)),
                pl.BlockSpec((B, H, DV, DK), lambda c: (0, 0, 0, 0))]
    p3 = p.reshape(B, L, p.shape[-1])
    args = [p3, pz.reshape(B, L, 2 * KD), p3, lb_row, s0]
    readout = o_f is not None
    if readout:
        in_specs += [col_spec(0), col_spec(2), pl.BlockSpec((1, DV), lambda c: (0, 0))]
        args += [o_f, p3, norm_g]
    return pl.pallas_call(
        functools.partial(_hgrn_kernel, reverse, readout),
        grid=(nch,),
        in_specs=in_specs,
        out_specs=col_spec(0),
        out_shape=jax.ShapeDtypeStruct((B, L, KD), F32),
        scratch_shapes=[pltpu.VMEM((B, H, DV, DK), F32)],
        compiler_params=_cparams(1, VMEM_LIMIT),
        name="hgrn_bwd_readout" if readout else "hgrn_fwd",
    )(*args)


def _hy_pre_kernel(v_ref, x1_ref, x0_ref, vp_ref, x1p_ref, x0p_ref, vn_ref, x1n_ref, x0n_ref,
                   w_ref, b_ref, vx_ref, x0o_ref):
    i = pl.program_id(1)
    first = i == 0
    last = i == pl.num_programs(1) - 1
    row = lax.broadcasted_iota(I32, (TM_HY, 1), 0)

    def conv(c_ref, p_ref, n_ref, col):
        x = c_ref[...].astype(F32)
        prev_row = jnp.where(first, 0.0, p_ref[...].astype(F32)[HALO - 1:HALO, :])
        next_row = jnp.where(last, 0.0, n_ref[...].astype(F32)[0:1, :])
        xm = jnp.where(row == 0, prev_row, pltpu.roll(x, 1, axis=0))
        xp = jnp.where(row == TM_HY - 1, next_row, pltpu.roll(x, TM_HY - 1, axis=0))
        cs = slice(col * HYW, (col + 1) * HYW)
        return xm * w_ref[0:1, cs] + x * w_ref[1:2, cs] + xp * w_ref[2:3, cs] + b_ref[:, cs]

    v = conv(v_ref, vp_ref, vn_ref, 0)
    x1 = conv(x1_ref, x1p_ref, x1n_ref, 1)
    x0 = conv(x0_ref, x0p_ref, x0n_ref, 2)
    vx_ref[...] = v * x1
    x0o_ref[...] = x0


def _hyena_pre(p, conv_w, conv_b):
    nt = L // TM_HY
    hb = TM_HY // HALO
    nhb = T // HALO
    cur = lambda col: pl.BlockSpec((TM_HY, HYW), lambda b, i: (b * nt + i, col))
    prv = lambda col: pl.BlockSpec((HALO, HYW), lambda b, i: (jnp.maximum((b * nt + i) * hb - 1, 0), col))
    nxt = lambda col: pl.BlockSpec((HALO, HYW), lambda b, i: (jnp.minimum((b * nt + i + 1) * hb, nhb - 1), col))
    c0 = 3
    out = jax.ShapeDtypeStruct((T, HYW), F32)
    return pl.pallas_call(
        _hy_pre_kernel,
        grid=(B, nt),
        in_specs=[cur(c0), cur(c0 + 1), cur(c0 + 2), prv(c0), prv(c0 + 1), prv(c0 + 2),
                  nxt(c0), nxt(c0 + 1), nxt(c0 + 2),
                  pl.BlockSpec((3, 3 * HYW), lambda b, i: (0, 0)),
                  pl.BlockSpec((1, 3 * HYW), lambda b, i: (0, 0))],
        out_specs=(pl.BlockSpec((TM_HY, HYW), lambda b, i: (b * nt + i, 0)),
                   pl.BlockSpec((TM_HY, HYW), lambda b, i: (b * nt + i, 0))),
        out_shape=(out, out),
        compiler_params=_cparams(2, VMEM_LIMIT),
        name="hyena_pre",
    )(p, p, p, p, p, p, p, p, p, conv_w, conv_b)


def _filt_kernel(z_ref, w1_ref, b1_ref, f1_ref, w2_ref, b2_ref, f2_ref, w3a_ref, w3b_ref, dl_ref, o_ref):
    half = z_ref.shape[0] // 2
    zt = z_ref[0:half, :]
    zb = z_ref[half:, :]
    lane = lax.broadcasted_iota(I32, zt.shape, 1)
    dot = functools.partial(jnp.dot, preferred_element_type=F32, precision=HIGHEST)
    h = jnp.sin(f1_ref[...] * (dot(jnp.concatenate([zt, zb], axis=1), w1_ref[...]) + b1_ref[...]))
    h = jnp.sin(f2_ref[...] * (dot(h, w2_ref[...]) + b2_ref[...]))
    for zz, w3_ref, rows in ((zt, w3a_ref, slice(0, half)), (zb, w3b_ref, slice(half, 2 * half))):
        taps = dot(h, w3_ref[...])
        win = jnp.exp(-_lane_pick(zz, lane, 0) * dl_ref[...])
        o_ref[0, rows, :] = taps[:, :HYW] * win
        o_ref[1, rows, :] = taps[:, HYW:] * win * _lane_pick(zz, lane, HY_EMB)


def _filter_taps(zin, w1, b1, f1, w2, b2, f2, w3a, w3b, deltas):
    tm = 2048
    ln = V7X_LANES
    full = lambda shape: pl.BlockSpec(shape, lambda i: (0, 0))
    return pl.pallas_call(
        _filt_kernel,
        grid=(L // tm,),
        in_specs=[pl.BlockSpec((tm, ln), lambda i: (i, 0)),
                  full((2 * ln, ln)), full((1, ln)), full((1, ln)),
                  full((ln, ln)), full((1, ln)), full((1, ln)),
                  full((ln, 2 * HYW)), full((ln, 2 * HYW)), full((1, HYW))],
        out_specs=pl.BlockSpec((2, tm, HYW), lambda i: (0, i, 0)),
        out_shape=jax.ShapeDtypeStruct((2, L, HYW), F32),
        compiler_params=_cparams(1, VMEM_LIMIT),
        name="hyena_filter",
    )(zin, w1, b1, f1, w2, b2, f2, w3a, w3b, deltas)


def _strided_dft_kernel(x_hbm, f_ref, o_hbm, xbuf, obuf, sem_in, sem_out):
    g = pl.program_id(0)
    ng = pl.num_programs(0)
    nb = FFT_P // FFT_BB

    def copies(grp, slot, inbound):
        n = grp // nb
        b0 = (grp % nb) * FFT_BB
        if inbound:
            return [pltpu.make_async_copy(x_hbm.at[n, :, b0 + jj, :], xbuf.at[slot, jj], sem_in.at[slot])
                    for jj in range(FFT_BB)]
        return [pltpu.make_async_copy(obuf.at[slot, jj], o_hbm.at[n, :, b0 + jj, :], sem_out.at[slot])
                for jj in range(FFT_BB)]

    def start(grp, slot, inbound):
        for cp in copies(grp, slot, inbound):
            cp.start()

    def wait(grp, slot, inbound):
        for cp in copies(grp, slot, inbound):
            cp.wait()

    @pl.when(g == 0)
    def _():
        start(0, 0, True)

    for slot in range(2):
        grp = 2 * g + slot
        if slot == 0:
            start(grp + 1, 1, True)
        else:
            @pl.when(g + 1 < ng)
            def _():
                start(grp + 1, 0, True)
        wait(grp, slot, True)

        @pl.when(g > 0)
        def _():
            wait(grp - 2, slot, False)

        for jj in range(FFT_BB):
            obuf[slot, jj] = jnp.dot(f_ref[...], xbuf[slot, jj].astype(BF16), preferred_element_type=F32)
        start(grp, slot, False)

    @pl.when(g + 1 == ng)
    def _():
        wait(2 * g, 0, False)
        wait(2 * g + 1, 1, False)


def _strided_dft(xv, fmat, name):
    n, kk = xv.shape[0], xv.shape[1]
    mm = fmat.shape[0]
    groups = n * (FFT_P // FFT_BB)
    return pl.pallas_call(
        _strided_dft_kernel,
        grid=(groups // 2,),
        in_specs=[pl.BlockSpec(memory_space=pl.ANY),
                  pl.BlockSpec((mm, kk), lambda g: (0, 0))],
        out_specs=pl.BlockSpec(memory_space=pl.ANY),
        out_shape=jax.ShapeDtypeStruct((n, mm, FFT_P, HYW), F32),
        scratch_shapes=[pltpu.VMEM((2, FFT_BB, kk, HYW), F32), pltpu.VMEM((2, FFT_BB, mm, HYW), F32),
                        pltpu.SemaphoreType.DMA((2,)), pltpu.SemaphoreType.DMA((2,))],
        compiler_params=_cparams(1, VMEM_LIMIT),
        name=name,
    )(xv, fmat)


def _cblock(mr, mi):
    return jnp.concatenate([jnp.concatenate([mr, -mi], axis=1), jnp.concatenate([mi, mr], axis=1)], axis=0)


FFT_H = FFT_P // 2


def _twiddled(fr_ref, fi_ref, tw):
    twr = tw[0:1, :]
    twi = tw[1:2, :]
    fr = fr_ref[...]
    fi = fi_ref[...]
    return fr * twr - fi * twi, fr * twi + fi * twr


def _cmul_rows(x, kr, ki):
    xr, xi = x[:FFT_P], x[FFT_P:]
    return jnp.concatenate([xr * kr - xi * ki, xr * ki + xi * kr], axis=0).astype(BF16)


_TN_DIMS = (((0,), (0,)), ((), ()))
_STAGE2_SPECS = [pl.BlockSpec((FFT_P, FFT_P), lambda d: (0, 0)),
                 pl.BlockSpec((FFT_P, FFT_P), lambda d: (0, 0)),
                 pl.BlockSpec((None, 2, FFT_P), lambda d: (d, 0, 0)),
                 pl.BlockSpec((2, FFT_P), lambda d: (0, 0))]


def _kspec_kernel(a_ref, fr_ref, fi_ref, tw_ref, twh_ref, o_ref, oh_ref):
    dd = pl.program_id(0)
    dot = functools.partial(jnp.dot, preferred_element_type=F32)

    def combine(xf, xb, out_ref):
        out_ref[0] = xf[:FFT_P] + xb[:FFT_P]
        out_ref[1] = xf[FFT_P:] - xb[FFT_P:]

    @pl.when(dd > 0)
    def _():
        rm = _cblock(*_twiddled(fr_ref, fi_ref, tw_ref[...])).astype(BF16)
        combine(dot(rm, a_ref[0].reshape(2 * FFT_P, HYW).astype(BF16)),
                dot(rm, a_ref[1].reshape(2 * FFT_P, HYW).astype(BF16)), o_ref)

    @pl.when(dd == 0)
    def _():
        for slot, tw, out_ref in ((0, tw_ref[...], o_ref), (1, twh_ref[...], oh_ref)):
            w = jnp.concatenate(_twiddled(fr_ref, fi_ref, tw), axis=0).astype(BF16)
            combine(dot(w, a_ref[0, slot].astype(BF16)), dot(w, a_ref[1, slot].astype(BF16)), out_ref)


def _kernel_spectrum(ak, fr, fi, tw, twh):
    return pl.pallas_call(
        _kspec_kernel,
        grid=(FFT_H,),
        in_specs=[pl.BlockSpec((2, None, 2, FFT_P, HYW), lambda d: (0, d, 0, 0, 0))] + _STAGE2_SPECS,
        out_specs=(pl.BlockSpec((2, None, FFT_P, HYW), lambda d: (0, d, 0, 0)),
                   pl.BlockSpec((2, FFT_P, HYW), lambda d: (0, 0, 0))),
        out_shape=(jax.ShapeDtypeStruct((2, FFT_H, FFT_P, HYW), F32),
                   jax.ShapeDtypeStruct((2, FFT_P, HYW), F32)),
        compiler_params=_cparams(1, VMEM_LIMIT),
        name="kernel_spectrum",
    )(ak, fr, fi, tw, twh)


def _mid_kernel(a_ref, kh_ref, khh_ref, fr_ref, fi_ref, tw_ref, twh_ref, o_ref):
    dd = pl.program_id(0)
    dot = functools.partial(jnp.dot, preferred_element_type=F32)
    dot_t = lambda w, y: lax.dot_general(w, y, _TN_DIMS, preferred_element_type=F32)

    @pl.when(dd > 0)
    def _():
        rm = _cblock(*_twiddled(fr_ref, fi_ref, tw_ref[...])).astype(BF16)
        for n in range(B):
            x = dot(rm, a_ref[n].reshape(2 * FFT_P, HYW).astype(BF16))
            y = _cmul_rows(x, kh_ref[0], kh_ref[1])
            o_ref[n] = dot_t(rm, y).reshape(2, FFT_P, HYW)

    @pl.when(dd == 0)
    def _():
        for slot, tw, k_ref in ((0, tw_ref[...], kh_ref), (1, twh_ref[...], khh_ref)):
            w = jnp.concatenate(_twiddled(fr_ref, fi_ref, tw), axis=0).astype(BF16)
            for n in range(B):
                y = _cmul_rows(dot(w, a_ref[n, slot].astype(BF16)), k_ref[0], k_ref[1])
                o_ref[n, slot] = dot_t(w, y)


def _fft_mid(au, kh, khh, fr, fi, tw, twh):
    pair = pl.BlockSpec((B, None, 2, FFT_P, HYW), lambda d: (0, d, 0, 0, 0))
    return pl.pallas_call(
        _mid_kernel,
        grid=(FFT_H,),
        in_specs=[pair,
                  pl.BlockSpec((2, None, FFT_P, HYW), lambda d: (0, d, 0, 0)),
                  pl.BlockSpec((2, FFT_P, HYW), lambda d: (0, 0, 0))] + _STAGE2_SPECS,
        out_specs=pair,
        out_shape=jax.ShapeDtypeStruct((B, FFT_H, 2, FFT_P, HYW), F32),
        compiler_params=_cparams(1, VMEM_LIMIT),
        name="fft_mid",
    )(au, kh, khh, fr, fi, tw, twh)


def _dft_tables():
    na = L // FFT_P
    a = np.arange(na)
    dd = np.arange(FFT_H)
    ang = 2.0 * np.pi * np.outer(dd, a) / FFT_P
    re_rows = np.cos(ang)
    im_rows = -np.sin(ang)
    im_rows[0] = np.cos(np.pi * a)
    f_first = np.stack([re_rows, im_rows], axis=1).reshape(FFT_P, na)
    gre = 2.0 * np.cos(ang)
    gim = -2.0 * np.sin(ang)
    gre[0] = 1.0
    gim[0] = np.cos(np.pi * a)
    g_last = np.stack([gre, gim], axis=1).reshape(FFT_P, na).T / FFT_N
    b = np.arange(FFT_P)
    angf = 2.0 * np.pi * np.outer(b, b) / FFT_P
    ang2 = 2.0 * np.pi * np.outer(np.arange(FFT_H + 1), b) / FFT_N
    tw = np.stack([np.cos(ang2), -np.sin(ang2)], axis=1)
    f32 = lambda x: jnp.asarray(x.astype(np.float32))
    return (f32(f_first).astype(BF16), f32(g_last).astype(BF16), f32(np.cos(angf)), f32(-np.sin(angf)),
            f32(tw[:FFT_H]), f32(tw[FFT_H]))


def _merge_kernel(ya_ref, cv_ref, vx_ref, x0_ref, ga_ref, gb_ref, x_ref, er_ref, ec_ref, gt1_ref, hyd_ref,
                  wpa_ref, wpb_ref, wo_ref, g2_ref, sh2_ref, sc2_ref, wr_ref, br_ref,
                  x1_ref, t2_ref, lg_ref):
    vx = vx_ref[...]
    yb = x0_ref[...] * (cv_ref[...] + vx * hyd_ref[...])
    pa = jnp.dot(ya_ref[...].astype(BF16), wpa_ref[...], preferred_element_type=F32)
    pb = jnp.dot(yb.astype(BF16), wpb_ref[...], preferred_element_type=F32)
    mixed = (jax.nn.sigmoid(ga_ref[...].astype(F32)) * pa
             + jax.nn.sigmoid(gb_ref[...].astype(F32)) * pb)
    xm = jnp.dot(mixed.astype(BF16), wo_ref[...], preferred_element_type=F32)
    nrow = TM_MG // GRID_W
    row0 = (pl.program_id(0) % (L // TM_MG)) * nrow
    x1 = x_ref[...] + _pos_tile(er_ref, ec_ref, row0, nrow) + gt1_ref[...] * xm
    x1_ref[...] = x1
    t2 = _rms(x1) * g2_ref[...]
    t2 = t2 * (1.0 + sc2_ref[...]) + sh2_ref[...]
    t2_ref[...] = t2
    t_hi = t2.astype(BF16)
    t_lo = (t2 - t_hi.astype(F32)).astype(BF16)
    rr = (jnp.dot(t_hi, wr_ref[...], preferred_element_type=F32)
          + jnp.dot(t_lo, wr_ref[...], preferred_element_type=F32))
    lg_ref[...] = rr[:, :V7X_LANES] + rr[:, V7X_LANES:] + br_ref[...]


def _merge(ya, cv, vx, x0c, p, x2, er, ec, gt1, hyd, wpa, wpb, wo, g2, sh2, sc2, wr, br):
    tpb = L // TM_MG
    half = lambda: pl.BlockSpec((TM_MG, HYW), lambda i: (i, 0))
    full = lambda shape: pl.BlockSpec(shape, lambda i: tuple(0 for _ in shape))
    perb = lambda: pl.BlockSpec((None, 1, D), lambda i: (i // tpb, 0, 0))
    return pl.pallas_call(
        _merge_kernel,
        grid=(T // TM_MG,),
        in_specs=[half(), half(), half(), half(),
                  pl.BlockSpec((TM_MG, D), lambda i: (i, 3)),
                  pl.BlockSpec((TM_MG, D), lambda i: (i, 4)),
                  pl.BlockSpec((TM_MG, D), lambda i: (i, 0)),
                  full((L // GRID_W, D // 2)), full((GRID_W, D // 2)),
                  perb(), full((1, HYW)),
                  full((KD, D)), full((HYW, D)), full((D, D)),
                  full((1, D)), perb(), perb(),
                  full((D, 2 * V7X_LANES)), full((1, V7X_LANES))],
        out_specs=(pl.BlockSpec((TM_MG, D), lambda i: (i, 0)),
                   pl.BlockSpec((TM_MG, D), lambda i: (i, 0)),
                   pl.BlockSpec((TM_MG, V7X_LANES), lambda i: (i, 0))),
        out_shape=(jax.ShapeDtypeStruct((T, D), F32), jax.ShapeDtypeStruct((T, D), F32),
                   jax.ShapeDtypeStruct((T, V7X_LANES), F32)),
        compiler_params=_cparams(1, VMEM_LIMIT),
        name="merge",
    )(ya, cv, vx, x0c, p, p, x2, er, ec, gt1, hyd, wpa, wpb, wo, g2, sh2, sc2, wr, br)


def _route_kernel(lg_ref, info_ref, cnt_ref):
    @pl.when(pl.program_id(0) == 0)
    def _():
        cnt_ref[...] = jnp.zeros_like(cnt_ref)

    lg = lg_ref[...]
    lane = lax.broadcasted_iota(I32, lg.shape, 1)
    lanef = lane.astype(F32)
    neg = -1e30
    big = 1e9
    is_g = (lane >= NEXP) & (lane < NEXP + NGRP)
    gl = jnp.where(is_g, lg, neg)
    ge = jnp.where(is_g, jnp.exp(gl - jnp.max(gl, axis=-1, keepdims=True)), 0.0)
    pg = ge / jnp.sum(ge, axis=-1, keepdims=True)
    p_top_g = jnp.max(pg, axis=-1, keepdims=True)
    gidx = jnp.min(jnp.where(is_g & (pg == p_top_g), lanef, big), axis=-1, keepdims=True)
    g_sel = gidx.astype(I32) - NEXP
    emask = (lane < NEXP) & (jnp.right_shift(lane, NEPG.bit_length() - 1) == g_sel)
    el = jnp.where(emask, lg, neg)
    ee = jnp.where(emask, jnp.exp(el - jnp.max(el, axis=-1, keepdims=True)), 0.0)
    pe = ee / jnp.sum(ee, axis=-1, keepdims=True)
    p1 = jnp.max(jnp.where(emask, pe, -1.0), axis=-1, keepdims=True)
    i1 = jnp.min(jnp.where(emask & (pe == p1), lanef, big), axis=-1, keepdims=True)
    rest = emask & (lanef != i1)
    p2 = jnp.max(jnp.where(rest, pe, -1.0), axis=-1, keepdims=True)
    i2 = jnp.min(jnp.where(rest & (pe == p2), lanef, big), axis=-1, keepdims=True)
    wsum = p1 + p2
    w1 = p_top_g * p1 / wsum
    w2 = p_top_g * p2 / wsum
    sel1 = lanef == i1
    sel2 = lanef == i2
    oh = jnp.where(sel1 | sel2, 1.0, 0.0)
    r = lax.broadcasted_iota(I32, (TR, TR), 0)
    c = lax.broadcasted_iota(I32, (TR, TR), 1)
    stril = jnp.where(c < r, 1.0, 0.0).astype(BF16)
    before = jnp.dot(stril, oh.astype(BF16), preferred_element_type=F32) + cnt_ref[...]
    r1 = jnp.sum(jnp.where(sel1, before, 0.0), axis=-1, keepdims=True)
    r2 = jnp.sum(jnp.where(sel2, before, 0.0), axis=-1, keepdims=True)
    cnt_ref[...] += jnp.sum(oh, axis=0, keepdims=True)
    info = jnp.where(lane == 0, i1, jnp.where(lane == 1, r1, jnp.where(lane == 2, i2, jnp.where(
        lane == 3, r2, jnp.where(lane == 4, w1, jnp.where(lane == 5, w2, 0.0))))))
    info_ref[...] = info


def _route(lg):
    return pl.pallas_call(
        _route_kernel,
        grid=(T // TR,),
        in_specs=[pl.BlockSpec((TR, V7X_LANES), lambda i: (i, 0))],
        out_specs=(pl.BlockSpec((TR, V7X_LANES), lambda i: (i, 0)),
                   pl.BlockSpec((1, V7X_LANES), lambda i: (0, 0))),
        out_shape=(jax.ShapeDtypeStruct((T, V7X_LANES), F32), jax.ShapeDtypeStruct((1, V7X_LANES), F32)),
        compiler_params=_cparams(1, VMEM_LIMIT),
        name="route",
    )(lg)


def _positions_kernel(info_ref, st_ref, o_ref):
    info = info_ref[...]
    lane = lax.broadcasted_iota(I32, info.shape, 1)
    lanef = lane.astype(F32)
    st = st_ref[...]
    row = lambda e_lane, r_lane: (jnp.sum(jnp.where(lanef == _lane_pick(info, lane, e_lane), st, 0.0),
                                          axis=-1, keepdims=True) + _lane_pick(info, lane, r_lane))
    o_ref[...] = jnp.where(lane == 0, row(0, 1), jnp.where(lane == 1, row(2, 3), 0.0)).astype(I32)


def _positions(info, starts_row):
    return pl.pallas_call(
        _positions_kernel,
        grid=(T // TPOS,),
        in_specs=[pl.BlockSpec((TPOS, V7X_LANES), lambda i: (i, 0)),
                  pl.BlockSpec((1, V7X_LANES), lambda i: (0, 0))],
        out_specs=pl.BlockSpec((TPOS, V7X_LANES), lambda i: (i, 0)),
        out_shape=jax.ShapeDtypeStruct((T, V7X_LANES), I32),
        compiler_params=_cparams(1, VMEM_LIMIT),
        name="positions",
    )(info, starts_row)


def _scatter_kernel(pos_ref, zt_ref, t2_ref, zeros_hbm, xs_hbm, sem, zsem):
    i = pl.program_id(0)

    def zcopy(row):
        start = pl.multiple_of(jnp.maximum(row, 0), TE)
        return pltpu.make_async_copy(zeros_hbm, xs_hbm.at[pl.ds(start, TE)], zsem)

    @pl.when(i == 0)
    def _():
        def ztail(start, e, carry):
            @pl.when(zt_ref[0, e] >= 0)
            def _():
                cp = zcopy(zt_ref[0, e])
                cp.start() if start else cp.wait()
            return carry

        lax.fori_loop(0, NEXP, functools.partial(ztail, True), 0)
        lax.fori_loop(0, NEXP, functools.partial(ztail, False), 0)

        def zrest(start, tile, carry):
            cp = zcopy(tile * TE)
            cp.start() if start else cp.wait()
            return carry

        lax.fori_loop(zt_ref[0, NA_OFF], NT_EXP, functools.partial(zrest, True), 0)
        lax.fori_loop(zt_ref[0, NA_OFF], NT_EXP, functools.partial(zrest, False), 0)

    def row_copy(base, jj, kk):
        dst = pos_ref[0, 2 * (base + jj) + kk]
        return pltpu.make_async_copy(t2_ref.at[pl.ds(base, ROW_GROUP)].at[pl.ds(jj, 1)],
                                     xs_hbm.at[pl.ds(dst, 1)], sem)

    _start_rows(TS, row_copy)
    _wait_rows(TS, pltpu.make_async_copy(t2_ref.at[pl.ds(0, 1)], xs_hbm.at[pl.ds(0, 1)], sem))


def _start_rows(n_rows, row_copy):
    def group(g, carry):
        base = pl.multiple_of(g * ROW_GROUP, ROW_GROUP)
        for jj in range(ROW_GROUP):
            for kk in range(2):
                row_copy(base, jj, kk).start(priority=kk)
        return carry

    lax.fori_loop(0, n_rows // ROW_GROUP, group, 0)


def _wait_rows(n_rows, one_row_copy):
    def drain(j, carry):
        one_row_copy.wait()
        one_row_copy.wait()
        return carry

    lax.fori_loop(0, n_rows, drain, 0, unroll=ROW_UNROLL)


def _scatter_rows(pos3, meta, t2, zeros_tile):
    return pl.pallas_call(
        _scatter_kernel,
        grid=(T // TS,),
        in_specs=[pl.BlockSpec((None, 1, 2 * TS), lambda i: (i, 0, 0), memory_space=pltpu.SMEM),
                  pl.BlockSpec(memory_space=pltpu.SMEM),
                  pl.BlockSpec((TS, D), lambda i: (i, 0)),
                  pl.BlockSpec(memory_space=pl.ANY)],
        out_specs=pl.BlockSpec(memory_space=pl.ANY),
        out_shape=jax.ShapeDtypeStruct((NP_ROWS, D), F32),
        scratch_shapes=[pltpu.SemaphoreType.DMA(()), pltpu.SemaphoreType.DMA(())],
        compiler_params=_cparams(1, VMEM_LIMIT),
        name="scatter_rows",
    )(pos3, meta, t2, zeros_tile)


def _expert_kernel(te_ref, nx_ref, sl_ref, na_ref, xs_ref, wg_hbm, wu_hbm, wd_hbm, ys_ref,
                   wgs_ref, wus_ref, wds_ref, wgb_ref, wub_ref, wdb_ref, sem):
    i = pl.program_id(0)
    active = i < na_ref[0]
    first = active & ((i == 0) | (te_ref[i] != te_ref[jnp.maximum(i - 1, 0)]))

    def fetch(e, slot):
        return [pltpu.make_async_copy(wg_hbm.at[e], wgs_ref.at[slot], sem.at[slot]),
                pltpu.make_async_copy(wu_hbm.at[e], wus_ref.at[slot], sem.at[slot]),
                pltpu.make_async_copy(wd_hbm.at[e], wds_ref.at[slot], sem.at[slot])]

    for slot in range(2):
        @pl.when(first & (sl_ref[i] == slot))
        def _():
            @pl.when(i == 0)
            def _():
                for cp in fetch(te_ref[i], slot):
                    cp.start()

            for cp in fetch(te_ref[i], slot):
                cp.wait()
            wgb_ref[...] = wgs_ref[slot].astype(BF16)
            wub_ref[...] = wus_ref[slot].astype(BF16)
            wdb_ref[...] = wds_ref[slot].astype(BF16)

            @pl.when(nx_ref[i] >= 0)
            def _():
                for cp in fetch(nx_ref[i], 1 - slot):
                    cp.start()

    @pl.when(active)
    def _():
        x = xs_ref[...].astype(BF16)
        g = jnp.dot(x, wgb_ref[...], preferred_element_type=F32)
        u = jnp.dot(x, wub_ref[...], preferred_element_type=F32)
        hid = (g * jax.nn.sigmoid(g) * u).astype(BF16)
        ys_ref[...] = jnp.dot(hid, wdb_ref[...], preferred_element_type=F32)

    @pl.when(jnp.logical_not(active))
    def _():
        ys_ref[...] = jnp.zeros_like(ys_ref)


def _experts(tile_expert, tile_next, tile_slot, n_active, xs, wg, wu, wd):
    rows = lambda i, *_: (i, 0)
    any_space = pl.BlockSpec(memory_space=pl.ANY)
    grid_spec = pltpu.PrefetchScalarGridSpec(
        num_scalar_prefetch=4,
        grid=(NT_EXP,),
        in_specs=[pl.BlockSpec((TE, D), rows), any_space, any_space, any_space],
        out_specs=pl.BlockSpec((TE, D), rows),
        scratch_shapes=[pltpu.VMEM((2, D, DEXP), F32), pltpu.VMEM((2, D, DEXP), F32), pltpu.VMEM((2, DEXP, D), F32),
                        pltpu.VMEM((D, DEXP), BF16), pltpu.VMEM((D, DEXP), BF16), pltpu.VMEM((DEXP, D), BF16),
                        pltpu.SemaphoreType.DMA((2,))],
    )
    return pl.pallas_call(
        _expert_kernel,
        grid_spec=grid_spec,
        out_shape=jax.ShapeDtypeStruct((NP_ROWS, D), F32),
        compiler_params=_cparams(1, VMEM_LIMIT),
        name="experts",
    )(tile_expert, tile_next, tile_slot, n_active, xs, wg, wu, wd)


def _combine_kernel(pos_ref, posn_ref, info_ref, x1_ref, gt2_ref, fg_ref, ys_hbm, o_ref, buf, sem):
    g = pl.program_id(0)
    ng = pl.num_programs(0)

    def gather(p_ref, half, slot):
        def row_copy(base, jj, kk):
            src = p_ref[0, 2 * (half * TC + base + jj) + kk]
            return pltpu.make_async_copy(ys_hbm.at[pl.ds(src, 1)],
                                         buf.at[slot, kk].at[pl.ds(base, ROW_GROUP)].at[pl.ds(jj, 1)],
                                         sem.at[slot])
        _start_rows(TC, row_copy)

    def finish(half, slot):
        _wait_rows(TC, pltpu.make_async_copy(ys_hbm.at[pl.ds(0, 1)], buf.at[slot, 0].at[pl.ds(0, 1)],
                                             sem.at[slot]))
        rows = slice(half * TC, (half + 1) * TC)
        info = info_ref[rows, :]
        lane = lax.broadcasted_iota(I32, info.shape, 1)
        moe = _lane_pick(info, lane, 4) * buf[slot, 0] + _lane_pick(info, lane, 5) * buf[slot, 1]
        x2 = x1_ref[rows, :] + gt2_ref[...] * moe
        o_ref[rows, :] = _rms(x2) * fg_ref[...]

    @pl.when(g == 0)
    def _():
        gather(pos_ref, 0, 0)

    gather(pos_ref, 1, 1)
    finish(0, 0)

    @pl.when(g + 1 < ng)
    def _():
        gather(posn_ref, 0, 0)

    finish(1, 1)


def _combine(pos3, info, x1, gt2, fg, ys):
    step = 2 * TC
    tpb = L // step
    nsteps = T // step
    return pl.pallas_call(
        _combine_kernel,
        grid=(nsteps,),
        in_specs=[pl.BlockSpec((None, 1, 2 * step), lambda i: (i, 0, 0), memory_space=pltpu.SMEM),
                  pl.BlockSpec((None, 1, 2 * step), lambda i: (jnp.minimum(i + 1, nsteps - 1), 0, 0),
                               memory_space=pltpu.SMEM),
                  pl.BlockSpec((step, V7X_LANES), lambda i: (i, 0)),
                  pl.BlockSpec((step, D), lambda i: (i, 0)),
                  pl.BlockSpec((None, 1, D), lambda i: (i // tpb, 0, 0)),
                  pl.BlockSpec((1, D), lambda i: (0, 0)),
                  pl.BlockSpec(memory_space=pl.ANY)],
        out_specs=pl.BlockSpec((step, D), lambda i: (i, 0)),
        out_shape=jax.ShapeDtypeStruct((T, D), F32),
        scratch_shapes=[pltpu.VMEM((2, 2, TC, D), F32), pltpu.SemaphoreType.DMA((2,))],
        compiler_params=_cparams(1, VMEM_LIMIT),
        name="combine",
    )(pos3, pos3, info, x1, gt2, fg, ys)


def _pos_tables():
    rows = L // GRID_W
    quarter = D // 4
    omega = 1.0 / (10000.0 ** (jnp.arange(quarter, dtype=F32) / quarter))

    def axis_emb(pos):
        a = pos[:, None] * omega[None, :]
        return jnp.concatenate([jnp.sin(a), jnp.cos(a)], axis=-1)

    er = axis_emb(jnp.arange(rows, dtype=F32))
    ec = axis_emb(jnp.arange(GRID_W, dtype=F32))
    return er, ec


def _filter_features():
    z = np.zeros((L, V7X_LANES), np.float64)
    bands = (HY_EMB - 1) // 2
    ang = (2.0 * np.pi * np.arange(L) / L)[:, None] * np.linspace(1e-4, bands - 1, bands)[None, :]
    z[:, 0] = np.linspace(0.0, 1.0, L)
    z[:, 1:1 + bands] = np.cos(ang)
    z[:, 1 + bands:HY_EMB] = -np.sin(ang)
    z[1:, HY_EMB] = 1.0
    return jnp.asarray(z.astype(np.float32))


def _pad2(a, rows, cols):
    return jnp.pad(a, ((0, rows - a.shape[0]), (0, cols - a.shape[1])))


def kernel(x, c, ctx, c_ctx, ada_w, ada_b, norm1_g, norm2_g, w_in, hgrn_lb, hgrn_norm_g, hy_conv_w, hy_conv_b, hy_filt_w1, hy_filt_b1, hy_filt_freq1, hy_filt_w2, hy_filt_b2, hy_filt_freq2, hy_filt_w3, hy_d, w_proj_a, w_proj_b, w_out, moe_router_g_w, moe_router_g_b, moe_router_e_w, moe_router_e_b, moe_w_gate, moe_w_up, moe_w_down, final_norm_g):
    cvec = jnp.zeros((8, D), F32).at[0:B].set(c).at[B].set(c_ctx)
    mod = _adaln(cvec, ada_w[0], ada_b[0][None, :])
    m6 = mod.reshape(8, 6, D)
    sh1, sc1, gt1, sh2, sc2, gt2 = [m6[0:B, k][:, None, :] for k in range(6)]
    csh1, csc1 = m6[B:B + 1, 0], m6[B:B + 1, 1]

    lbs = jnp.cumsum(jax.nn.softmax(hgrn_lb.astype(F32), axis=0), axis=0)[0]
    g1 = norm1_g[0][None, :]
    er, ec = _pos_tables()
    x2 = x.reshape(T, D)

    w_ctx = w_in[0][:, KD:4 * KD].astype(BF16)
    s_f, s_b = _context_states(ctx, g1, csh1, csc1, w_ctx, lbs)

    pz, p = _in_projection(x2, er, ec, g1, sh1, sc1, w_in[0].astype(BF16))

    o_f = _hgrn_scan(pz, p, lbs[0:1], s_f, False)
    y_a = _hgrn_scan(pz, p, lbs[1:2], s_b, True, o_f=o_f, norm_g=hgrn_norm_g[0][None, :])

    vx, x0c = _hyena_pre(p, hy_conv_w[0], hy_conv_b[0][None, :])
    deltas = jnp.abs(jnp.linspace(math.log(HY_DECAY_TARGET) / HY_SLOW_PCT,
                                  math.log(HY_DECAY_TARGET) / HY_FAST_PCT, HYW, dtype=F32))[None, :]
    ln = V7X_LANES
    fh = hy_filt_w2.shape[-1]
    blockdiag = lambda m: jnp.concatenate([_pad2(m, m.shape[0], 2 * m.shape[1]),
                                           jnp.pad(m, ((0, 0), (m.shape[1], 0)))], axis=0)
    twice = lambda v: jnp.concatenate([v, v])[None, :]
    w3 = hy_filt_w3[0]
    taps = _filter_taps(
        _filter_features(),
        blockdiag(_pad2(hy_filt_w1[0], ln, fh)), twice(hy_filt_b1[0]), twice(hy_filt_freq1[0]),
        blockdiag(hy_filt_w2[0]), twice(hy_filt_b2[0]), twice(hy_filt_freq2[0]),
        _pad2(w3, ln, 2 * HYW), jnp.pad(w3, ((fh, 0), (0, 0))), deltas)
    f_first, g_last, fr, fi, tw, twh = _dft_tables()
    na = L // FFT_P
    ak = _strided_dft(taps.reshape(2, na, FFT_P, HYW), f_first, "dft_first_taps")
    kh, khh = _kernel_spectrum(ak.reshape(2, FFT_H, 2, FFT_P, HYW), fr, fi, tw, twh)
    au = _strided_dft(vx.reshape(B, na, FFT_P, HYW), f_first, "dft_first")
    bp = _fft_mid(au.reshape(B, FFT_H, 2, FFT_P, HYW), kh, khh, fr, fi, tw, twh)
    conv = _strided_dft(bp.reshape(B, FFT_P, FFT_P, HYW), g_last, "dft_last").reshape(T, HYW)

    wr = jnp.concatenate([jnp.transpose(moe_router_e_w[0], (1, 0, 2)).reshape(D, NEXP),
                          moe_router_g_w[0], jnp.zeros((D, V7X_LANES - NEXP - NGRP), F32)], axis=1)
    wr_hi = wr.astype(BF16)
    wr = jnp.concatenate([wr_hi, (wr - wr_hi.astype(F32)).astype(BF16)], axis=1)
    br = jnp.concatenate([moe_router_e_b[0].reshape(NEXP), moe_router_g_b[0],
                          jnp.zeros((V7X_LANES - NEXP - NGRP,), F32)])[None, :]
    x1, t2, lg = _merge(y_a.reshape(T, KD), conv, vx, x0c, p, x2, er, ec, gt1, hy_d[0][None, :],
                        w_proj_a[0].astype(BF16), w_proj_b[0].astype(BF16), w_out[0].astype(BF16),
                        norm2_g[0][None, :], sh2, sc2, wr, br)

    info, counts = _route(lg)
    cnt = counts[0, :NEXP].astype(I32)
    pc = ((cnt + TE - 1) // TE) * TE
    ends = jnp.cumsum(pc)
    starts = ends - pc
    n_active = (ends[-1] // TE).astype(I32)[None]
    tile_rows = jnp.arange(NT_EXP, dtype=I32) * TE
    tile_expert = jnp.minimum(jnp.sum((ends[None, :] <= tile_rows[:, None]).astype(I32), axis=1), NEXP - 1)
    meta = jnp.concatenate([jnp.where(pc > 0, ends - TE, -1), starts, n_active]).astype(I32)[None, :]
    starts_row = jnp.pad(starts.astype(F32), (0, V7X_LANES - NEXP))[None, :]
    pos3 = _positions(info, starts_row)[:, :2].reshape(T // TS, 1, 2 * TS)

    xs = _scatter_rows(pos3, meta, t2, jnp.zeros((TE, D), F32))
    eid = jnp.arange(NEXP, dtype=I32)
    nonempty = pc > 0
    later = jnp.where(nonempty[None, :] & (eid[None, :] > eid[:, None]), eid[None, :], NEXP)
    next_e = jnp.min(later, axis=1)
    next_e = jnp.where(next_e < NEXP, next_e, -1).astype(I32)
    slot_e = ((jnp.cumsum(nonempty.astype(I32)) - 1) % 2).astype(I32)
    ys = _experts(tile_expert, next_e[tile_expert], slot_e[tile_expert], n_active, xs,
                  moe_w_gate[0].reshape(NEXP, D, DEXP), moe_w_up[0].reshape(NEXP, D, DEXP),
                  moe_w_down[0].reshape(NEXP, DEXP, D))
    out = _combine(pos3, info, x1, gt2, final_norm_g[None, :], ys)
    return out.reshape(B, L, D)
```

```python
import functools
import math

import numpy as np
import jax
import jax.numpy as jnp
from jax import lax
from jax.experimental import pallas as pl
from jax.experimental.pallas import tpu as pltpu

F32 = jnp.float32
BF16 = jnp.bfloat16
I32 = jnp.int32
HIGHEST = lax.Precision.HIGHEST

D = 1024
B = 2
L = 8192
T = B * L
CTX = 256
GRID_W = 64
EPS = 1e-6
H = 4
DK = 128
DV = 128
KD = H * DK
IN_W = 6144
HYW = 512
HY_EMB = 33
NGRP = 4
NEPG = 8
NEXP = NGRP * NEPG
DEXP = 512
HY_DECAY_TARGET = 1e-2
HY_FAST_PCT = 0.3
HY_SLOW_PCT = 1.5

V7X_LANES = 128
V7X_SUBLANES = 8
V7X_VMEM_BYTES = 64 * 1024 * 1024
VMEM_LIMIT = (3 * V7X_VMEM_BYTES) // 4

FFT_N = 2 * L
FFT_P = 128
FFT_BB = 8

TM_IN = 1024
TN_IN = 1024
TH = 128
CB = 32
TM_HY = 1024
HALO = 2 * V7X_SUBLANES
TM_MG = 512
TR = 512
TE = 512
TPOS = 2048
NP_ROWS = 2 * T + NEXP * TE
NT_EXP = NP_ROWS // TE
TS = 1024
TC = TS // 2
ROW_UNROLL = 8
ROW_GROUP = 32
ST_OFF = NEXP
NA_OFF = 2 * NEXP


def _cparams(n_axes, vmem=None):
    return pltpu.CompilerParams(dimension_semantics=("arbitrary",) * n_axes,
                                vmem_limit_bytes=vmem)


def _split3(x):
    hi = x.astype(BF16)
    r = x - hi.astype(F32)
    mid = r.astype(BF16)
    lo = (r - mid.astype(F32)).astype(BF16)
    return hi, mid, lo


def _dot01(m, x):
    hi, mid, lo = _split3(x)
    return (jnp.dot(m, hi, preferred_element_type=F32) + jnp.dot(m, mid, preferred_element_type=F32)
            + jnp.dot(m, lo, preferred_element_type=F32))


def _rms(x):
    return x * lax.rsqrt(jnp.mean(x * x, axis=-1, keepdims=True) + EPS)


def _lane_pick(x, lane, idx):
    return jnp.sum(jnp.where(lane == idx, x, 0.0), axis=-1, keepdims=True)


def _ada_kernel(c_ref, w_ref, b_ref, o_ref):
    c = c_ref[...]
    s = c * jax.nn.sigmoid(c)
    o_ref[...] = jnp.dot(s, w_ref[...], preferred_element_type=F32, precision=HIGHEST) + b_ref[...]


def _adaln(cvec, w, b):
    tn = 1536
    return pl.pallas_call(
        _ada_kernel,
        grid=(6 * D // tn,),
        in_specs=[pl.BlockSpec((8, D), lambda j: (0, 0)),
                  pl.BlockSpec((D, tn), lambda j: (0, j)),
                  pl.BlockSpec((1, tn), lambda j: (0, j))],
        out_specs=pl.BlockSpec((8, tn), lambda j: (0, j)),
        out_shape=jax.ShapeDtypeStruct((8, 6 * D), F32),
        compiler_params=_cparams(1, VMEM_LIMIT),
        name="adaln",
    )(cvec, w, b)


def _keys(z, lb):
    sig = jax.nn.sigmoid(z)
    logf = jnp.log(lb + (1.0 - lb) * sig)
    k = (1.0 - lb) * jax.nn.sigmoid(-z)
    return k, logf


def _ctx_kernel(ctx_ref, g_ref, sh_ref, sc_ref, w_ref, lb_ref, sf_ref, sb_ref):
    h = _rms(ctx_ref[...]) * g_ref[...]
    h = h * (1.0 + sc_ref[...]) + sh_ref[...]
    p = jnp.dot(h.astype(BF16), w_ref[...], preferred_element_type=F32)
    zf, zb, v = p[:, :KD], p[:, KD:2 * KD], p[:, 2 * KD:]
    kf, lf = _keys(zf, lb_ref[0:1, :])
    kb, lbk = _keys(zb, lb_ref[1:2, :])
    r = lax.broadcasted_iota(I32, (CTX, CTX), 0)
    c = lax.broadcasted_iota(I32, (CTX, CTX), 1)
    tril = jnp.where(c <= r, 1.0, 0.0).astype(BF16)
    cf = _dot01(tril, lf)
    cb = _dot01(tril, lbk)
    kfd = (kf * jnp.exp(cf[CTX - 1:CTX, :] - cf)).astype(BF16)
    kbd = (kb * jnp.exp(cb - lbk)).astype(BF16)
    vb = v.astype(BF16)
    tn = (((0,), (0,)), ((), ()))
    for hh in range(H):
        hs = slice(hh * DK, (hh + 1) * DK)
        sf_ref[hh] = lax.dot_general(vb[:, hs], kfd[:, hs], tn, preferred_element_type=F32)
        sb_ref[hh] = lax.dot_general(vb[:, hs], kbd[:, hs], tn, preferred_element_type=F32)


def _context_states(ctx, g1, csh1, csc1, w_ctx, lbs):
    st = jax.ShapeDtypeStruct((B, H, DV, DK), F32)
    return pl.pallas_call(
        _ctx_kernel,
        grid=(B,),
        in_specs=[pl.BlockSpec((None, CTX, D), lambda b: (b, 0, 0)),
                  pl.BlockSpec((1, D), lambda b: (0, 0)),
                  pl.BlockSpec((1, D), lambda b: (0, 0)),
                  pl.BlockSpec((1, D), lambda b: (0, 0)),
                  pl.BlockSpec((D, 3 * KD), lambda b: (0, 0)),
                  pl.BlockSpec((2, KD), lambda b: (0, 0))],
        out_specs=(pl.BlockSpec((None, H, DV, DK), lambda b: (b, 0, 0, 0)),
                   pl.BlockSpec((None, H, DV, DK), lambda b: (b, 0, 0, 0))),
        out_shape=(st, st),
        compiler_params=_cparams(1, VMEM_LIMIT),
        name="ctx_states",
    )(ctx, g1, csh1, csc1, w_ctx, lbs)


def _pos_tile(er_ref, ec_ref, row0, nrow):
    lo = jnp.concatenate([jnp.broadcast_to(er_ref[pl.ds(row0 + i, 1), :], (GRID_W, D // 2))
                          for i in range(nrow)], axis=0)
    hi = jnp.concatenate([ec_ref[...]] * nrow, axis=0)
    return jnp.concatenate([lo, hi], axis=1)


def _inproj_kernel(x_ref, er_ref, ec_ref, g_ref, sh_ref, sc_ref, w_ref, oz_ref, o_ref, hx_ref):
    j = pl.program_id(1)

    @pl.when(j == 0)
    def _():
        nrow = TM_IN // GRID_W
        row0 = (pl.program_id(0) % (L // TM_IN)) * nrow
        h = _rms(x_ref[...] + _pos_tile(er_ref, ec_ref, row0, nrow)) * g_ref[...]
        hx_ref[...] = (h * (1.0 + sc_ref[...]) + sh_ref[...]).astype(BF16)

    r = jnp.dot(hx_ref[...], w_ref[...], preferred_element_type=F32)

    @pl.when(j == 0)
    def _():
        o_ref[:, :KD] = r[:, :KD].astype(BF16)
        oz_ref[:, :KD] = r[:, KD:]

    @pl.when(j == 1)
    def _():
        oz_ref[:, KD:] = r[:, :KD]
        o_ref[:, KD:] = r[:, KD:].astype(BF16)

    @pl.when(j > 1)
    def _():
        o_ref[...] = r.astype(BF16)


def _in_projection(x2, er, ec, g1, sh1, sc1, w_bf):
    tiles_per_batch = L // TM_IN
    return pl.pallas_call(
        _inproj_kernel,
        grid=(T // TM_IN, IN_W // TN_IN),
        in_specs=[pl.BlockSpec((TM_IN, D), lambda i, j: (i, 0)),
                  pl.BlockSpec((L // GRID_W, D // 2), lambda i, j: (0, 0)),
                  pl.BlockSpec((GRID_W, D // 2), lambda i, j: (0, 0)),
                  pl.BlockSpec((1, D), lambda i, j: (0, 0)),
                  pl.BlockSpec((None, 1, D), lambda i, j: (i // tiles_per_batch, 0, 0)),
                  pl.BlockSpec((None, 1, D), lambda i, j: (i // tiles_per_batch, 0, 0)),
                  pl.BlockSpec((D, TN_IN), lambda i, j: (0, j))],
        out_specs=(pl.BlockSpec((TM_IN, TN_IN), lambda i, j: (i, 0)),
                   pl.BlockSpec((TM_IN, TN_IN), lambda i, j: (i, jnp.maximum(j - 1, 0)))),
        out_shape=(jax.ShapeDtypeStruct((T, 2 * KD), F32),
                   jax.ShapeDtypeStruct((T, IN_W - 2 * KD), BF16)),
        scratch_shapes=[pltpu.VMEM((TM_IN, D), BF16)],
        compiler_params=_cparams(2, VMEM_LIMIT),
        name="in_proj",
    )(x2, er, ec, g1, sh1, sc1, w_bf)


def _hgrn_kernel(reverse, readout, *refs):
    if readout:
        q_ref, z_ref, v_ref, lb_ref, s0_ref, of_ref, g_ref, ng_ref, o_ref, st_ref = refs
    else:
        q_ref, z_ref, v_ref, lb_ref, s0_ref, o_ref, st_ref = refs

    @pl.when(pl.program_id(0) == 0)
    def _():
        st_ref[...] = s0_ref[...]

    r = lax.broadcasted_iota(I32, (TH, TH), 0)
    c = lax.broadcasted_iota(I32, (TH, TH), 1)
    cb_shift = CB.bit_length() - 1
    same = jnp.right_shift(r, cb_shift) == jnp.right_shift(c, cb_shift)
    tri_mask = same & ((c >= r) if reverse else (c <= r))
    tri = jnp.where(tri_mask, 1.0, 0.0).astype(BF16)
    rblk = jnp.right_shift(r, cb_shift)
    cblk = jnp.right_shift(c, cb_shift)
    dist = (rblk - cblk) if not reverse else (cblk - rblk)
    for b in range(B):
        _hgrn_chunk(reverse, readout, b, tri, tri_mask, dist, refs)


def _hgrn_chunk(reverse, readout, b, tri, tri_mask, dist, refs):
    if readout:
        q_ref, z_ref, v_ref, lb_ref, s0_ref, of_ref, g_ref, ng_ref, o_ref, st_ref = refs
    else:
        q_ref, z_ref, v_ref, lb_ref, s0_ref, o_ref, st_ref = refs
    q = q_ref[b].astype(F32)
    v = v_ref[b]
    k, logf = _keys(z_ref[b], lb_ref[...])
    bl = _dot01(tri, logf)
    nt = (((1,), (1,)), ((), ()))
    tn = (((0,), (0,)), ((), ()))
    nblk = TH // CB
    e_row = 0 if reverse else CB - 1
    m_row = CB - 1 - CB // 2 if reverse else CB // 2
    tau = [bl[jb * CB + e_row:jb * CB + e_row + 1] for jb in range(nblk)]
    mid = [bl[jb * CB + m_row:jb * CB + m_row + 1] for jb in range(nblk)]
    rows = lambda vecs: jnp.concatenate([jnp.broadcast_to(x, (CB, KD)) for x in vecs], axis=0)
    mid_b = rows(mid)
    qd0 = (q * jnp.exp(bl - mid_b)).astype(BF16)
    kd0 = (k * jnp.exp(mid_b - bl)).astype(BF16)
    qs = q * jnp.exp(bl)
    ke = k * jnp.exp(rows(tau) - bl)
    order = list(range(nblk - 1, -1, -1)) if reverse else list(range(nblk))
    pre = [jnp.zeros((1, KD), F32)]
    for i in range(nblk):
        pre.append(pre[-1] + tau[order[i]])
    total = pre[nblk]
    entry = [None] * nblk
    leave = [None] * nblk
    gap = [[None] * nblk for _ in range(nblk)]
    for i, jb in enumerate(order):
        entry[jb] = jnp.exp(pre[i])
        leave[jb] = jnp.exp(total - pre[i + 1])
        for d in range(2, nblk):
            gap[d][jb] = jnp.exp(pre[i + d] - pre[i + 1]) if i + d < nblk else jnp.zeros((1, KD), F32)
    qc = (qs * rows(entry)).astype(BF16)
    kc = (ke * rows(leave)).astype(BF16)
    kx = jnp.concatenate([ke.astype(BF16)] + [(ke * rows(gap[d])).astype(BF16) for d in range(2, nblk)], axis=0)
    qsb = qs.astype(BF16)
    dec = jnp.exp(total)

    def blockdiag(x):
        first_head = lax.broadcasted_iota(I32, x.shape, 1) < DK
        zero = jnp.zeros_like(x)
        return jnp.concatenate([jnp.where(first_head, x, zero), jnp.where(first_head, zero, x)], axis=0)

    for hp in range(H // 2):
        ps = slice(2 * hp * DK, 2 * (hp + 1) * DK)
        sc0 = lax.dot_general(qd0[:, ps], blockdiag(kd0[:, ps]), nt, preferred_element_type=F32)
        scx = lax.dot_general(qsb[:, ps], blockdiag(kx[:, ps]), nt, preferred_element_type=F32)
        halves = []
        for hh in range(2):
            sc = jnp.where(tri_mask, sc0[:, hh * TH:(hh + 1) * TH], 0.0)
            base = hh * (nblk - 1) * TH
            for d in range(1, nblk):
                sc = jnp.where(dist == d, scx[:, base + (d - 1) * TH:base + d * TH], sc)
            halves.append(sc.astype(BF16))
        sc_pair = jnp.concatenate(halves, axis=1)
        st_a = st_ref[b, 2 * hp]
        st_b = st_ref[b, 2 * hp + 1]
        zst = jnp.zeros((DV, DK), BF16)
        st_pair = jnp.concatenate([jnp.concatenate([st_a.astype(BF16), zst], axis=1),
                                   jnp.concatenate([zst, st_b.astype(BF16)], axis=1)], axis=0)
        o_pair = (lax.dot_general(qc[:, ps], st_pair, nt, preferred_element_type=F32)
                  + jnp.dot(sc_pair, blockdiag(v[:, ps]), preferred_element_type=F32))
        upd = lax.dot_general(v[:, ps], kc[:, ps], tn, preferred_element_type=F32)
        st_ref[b, 2 * hp] = st_a * dec[:, ps][:, :DK] + upd[:DV, :DK]
        st_ref[b, 2 * hp + 1] = st_b * dec[:, ps][:, DK:] + upd[DV:, DK:]
        for hh in range(2):
            hs = slice((2 * hp + hh) * DK, (2 * hp + hh + 1) * DK)
            o_h = o_pair[:, hh * DV:(hh + 1) * DV]
            if readout:
                o_h = o_h + of_ref[b, :, hs]
                o_h = _rms(o_h) * ng_ref[...]
                gh = g_ref[b, :, hs].astype(F32)
                o_h = o_h * (gh * jax.nn.sigmoid(gh))
            o_ref[b, :, hs] = o_h


def _hgrn_scan(pz, p, lb_row, s0, reverse, o_f=None, norm_g=None):
    nch = L // TH
    chunk = (lambda c: nch - 1 - c) if reverse else (lambda c: c)
    col_spec = lambda j: pl.BlockSpec((B, TH, KD), lambda c: (0, chunk(c), j))
    in_specs = [col_spec(0), col_spec(1 if reverse else 0), col_spec(1),
                pl.BlockSpec((1, KD), lambda c: (0, 0)),
                pl.BlockSpec((B, H, DV, DK), lambda c: (0, 0, 0, 0))]
    p3 = p.reshape(B, L, p.shape[-1])
    args = [p3, pz.reshape(B, L, 2 * KD), p3, lb_row, s0]
    readout = o_f is not None
    if readout:
        in_specs += [col_spec(0), col_spec(2), pl.BlockSpec((1, DV), lambda c: (0, 0))]
        args += [o_f, p3, norm_g]
    return pl.pallas_call(
        functools.partial(_hgrn_kernel, reverse, readout),
        grid=(nch,),
        in_specs=in_specs,
        out_specs=col_spec(0),
        out_shape=jax.ShapeDtypeStruct((B, L, KD), F32),
        scratch_shapes=[pltpu.VMEM((B, H, DV, DK), F32)],
        compiler_params=_cparams(1, VMEM_LIMIT),
        name="hgrn_bwd_readout" if readout else "hgrn_fwd",
    )(*args)


def _hy_pre_kernel(v_ref, x1_ref, x0_ref, vp_ref, x1p_ref, x0p_ref, vn_ref, x1n_ref, x0n_ref,
                   w_ref, b_ref, vx_ref, x0o_ref):
    i = pl.program_id(1)
    first = i == 0
    last = i == pl.num_programs(1) - 1
    row = lax.broadcasted_iota(I32, (TM_HY, 1), 0)

    def conv(c_ref, p_ref, n_ref, col):
        x = c_ref[...].astype(F32)
        prev_row = jnp.where(first, 0.0, p_ref[...].astype(F32)[HALO - 1:HALO, :])
        next_row = jnp.where(last, 0.0, n_ref[...].astype(F32)[0:1, :])
        xm = jnp.where(row == 0, prev_row, pltpu.roll(x, 1, axis=0))
        xp = jnp.where(row == TM_HY - 1, next_row, pltpu.roll(x, TM_HY - 1, axis=0))
        cs = slice(col * HYW, (col + 1) * HYW)
        return xm * w_ref[0:1, cs] + x * w_ref[1:2, cs] + xp * w_ref[2:3, cs] + b_ref[:, cs]

    v = conv(v_ref, vp_ref, vn_ref, 0)
    x1 = conv(x1_ref, x1p_ref, x1n_ref, 1)
    x0 = conv(x0_ref, x0p_ref, x0n_ref, 2)
    vx_ref[...] = v * x1
    x0o_ref[...] = x0


def _hyena_pre(p, conv_w, conv_b):
    nt = L // TM_HY
    hb = TM_HY // HALO
    nhb = T // HALO
    cur = lambda col: pl.BlockSpec((TM_HY, HYW), lambda b, i: (b * nt + i, col))
    prv = lambda col: pl.BlockSpec((HALO, HYW), lambda b, i: (jnp.maximum((b * nt + i) * hb - 1, 0), col))
    nxt = lambda col: pl.BlockSpec((HALO, HYW), lambda b, i: (jnp.minimum((b * nt + i + 1) * hb, nhb - 1), col))
    c0 = 3
    out = jax.ShapeDtypeStruct((T, HYW), F32)
    return pl.pallas_call(
        _hy_pre_kernel,
        grid=(B, nt),
        in_specs=[cur(c0), cur(c0 + 1), cur(c0 + 2), prv(c0), prv(c0 + 1), prv(c0 + 2),
                  nxt(c0), nxt(c0 + 1), nxt(c0 + 2),
                  pl.BlockSpec((3, 3 * HYW), lambda b, i: (0, 0)),
                  pl.BlockSpec((1, 3 * HYW), lambda b, i: (0, 0))],
        out_specs=(pl.BlockSpec((TM_HY, HYW), lambda b, i: (b * nt + i, 0)),
                   pl.BlockSpec((TM_HY, HYW), lambda b, i: (b * nt + i, 0))),
        out_shape=(out, out),
        compiler_params=_cparams(2, VMEM_LIMIT),
        name="hyena_pre",
    )(p, p, p, p, p, p, p, p, p, conv_w, conv_b)


def _filt_kernel(z_ref, w1_ref, b1_ref, f1_ref, w2_ref, b2_ref, f2_ref, w3a_ref, w3b_ref, dl_ref, o_ref):
    half = z_ref.shape[0] // 2
    zt = z_ref[0:half, :]
    zb = z_ref[half:, :]
    lane = lax.broadcasted_iota(I32, zt.shape, 1)
    dot = functools.partial(jnp.dot, preferred_element_type=F32, precision=HIGHEST)
    h = jnp.sin(f1_ref[...] * (dot(jnp.concatenate([zt, zb], axis=1), w1_ref[...]) + b1_ref[...]))
    h = jnp.sin(f2_ref[...] * (dot(h, w2_ref[...]) + b2_ref[...]))
    for zz, w3_ref, rows in ((zt, w3a_ref, slice(0, half)), (zb, w3b_ref, slice(half, 2 * half))):
        taps = dot(h, w3_ref[...])
        win = jnp.exp(-_lane_pick(zz, lane, 0) * dl_ref[...])
        o_ref[0, rows, :] = taps[:, :HYW] * win
        o_ref[1, rows, :] = taps[:, HYW:] * win * _lane_pick(zz, lane, HY_EMB)


def _filter_taps(zin, w1, b1, f1, w2, b2, f2, w3a, w3b, deltas):
    tm = 2048
    ln = V7X_LANES
    full = lambda shape: pl.BlockSpec(shape, lambda i: (0, 0))
    return pl.pallas_call(
        _filt_kernel,
        grid=(L // tm,),
        in_specs=[pl.BlockSpec((tm, ln), lambda i: (i, 0)),
                  full((2 * ln, ln)), full((1, ln)), full((1, ln)),
                  full((ln, ln)), full((1, ln)), full((1, ln)),
                  full((ln, 2 * HYW)), full((ln, 2 * HYW)), full((1, HYW))],
        out_specs=pl.BlockSpec((2, tm, HYW), lambda i: (0, i, 0)),
        out_shape=jax.ShapeDtypeStruct((2, L, HYW), F32),
        compiler_params=_cparams(1, VMEM_LIMIT),
        name="hyena_filter",
    )(zin, w1, b1, f1, w2, b2, f2, w3a, w3b, deltas)


def _strided_dft_kernel(x_hbm, f_ref, o_hbm, xbuf, obuf, sem_in, sem_out):
    g = pl.program_id(0)
    ng = pl.num_programs(0)
    nb = FFT_P // FFT_BB

    def copies(grp, slot, inbound):
        n = grp // nb
        b0 = (grp % nb) * FFT_BB
        if inbound:
            return [pltpu.make_async_copy(x_hbm.at[n, :, b0 + jj, :], xbuf.at[slot, jj], sem_in.at[slot])
                    for jj in range(FFT_BB)]
        return [pltpu.make_async_copy(obuf.at[slot, jj], o_hbm.at[n, :, b0 + jj, :], sem_out.at[slot])
                for jj in range(FFT_BB)]

    def start(grp, slot, inbound):
        for cp in copies(grp, slot, inbound):
            cp.start()

    def wait(grp, slot, inbound):
        for cp in copies(grp, slot, inbound):
            cp.wait()

    @pl.when(g == 0)
    def _():
        start(0, 0, True)

    for slot in range(2):
        grp = 2 * g + slot
        if slot == 0:
            start(grp + 1, 1, True)
        else:
            @pl.when(g + 1 < ng)
            def _():
                start(grp + 1, 0, True)
        wait(grp, slot, True)

        @pl.when(g > 0)
        def _():
            wait(grp - 2, slot, False)

        for jj in range(FFT_BB):
            obuf[slot, jj] = jnp.dot(f_ref[...], xbuf[slot, jj].astype(BF16), preferred_element_type=F32)
        start(grp, slot, False)

    @pl.when(g + 1 == ng)
    def _():
        wait(2 * g, 0, False)
        wait(2 * g + 1, 1, False)


def _strided_dft(xv, fmat, name):
    n, kk = xv.shape[0], xv.shape[1]
    mm = fmat.shape[0]
    groups = n * (FFT_P // FFT_BB)
    return pl.pallas_call(
        _strided_dft_kernel,
        grid=(groups // 2,),
        in_specs=[pl.BlockSpec(memory_space=pl.ANY),
                  pl.BlockSpec((mm, kk), lambda g: (0, 0))],
        out_specs=pl.BlockSpec(memory_space=pl.ANY),
        out_shape=jax.ShapeDtypeStruct((n, mm, FFT_P, HYW), F32),
        scratch_shapes=[pltpu.VMEM((2, FFT_BB, kk, HYW), F32), pltpu.VMEM((2, FFT_BB, mm, HYW), F32),
                        pltpu.SemaphoreType.DMA((2,)), pltpu.SemaphoreType.DMA((2,))],
        compiler_params=_cparams(1, VMEM_LIMIT),
        name=name,
    )(xv, fmat)


def _cblock(mr, mi):
    return jnp.concatenate([jnp.concatenate([mr, -mi], axis=1), jnp.concatenate([mi, mr], axis=1)], axis=0)


FFT_H = FFT_P // 2


def _twiddled(fr_ref, fi_ref, tw):
    twr = tw[0:1, :]
    twi = tw[1:2, :]
    fr = fr_ref[...]
    fi = fi_ref[...]
    return fr * twr - fi * twi, fr * twi + fi * twr


def _cmul_rows(x, kr, ki):
    xr, xi = x[:FFT_P], x[FFT_P:]
    return jnp.concatenate([xr * kr - xi * ki, xr * ki + xi * kr], axis=0).astype(BF16)


_TN_DIMS = (((0,), (0,)), ((), ()))
_STAGE2_SPECS = [pl.BlockSpec((FFT_P, FFT_P), lambda d: (0, 0)),
                 pl.BlockSpec((FFT_P, FFT_P), lambda d: (0, 0)),
                 pl.BlockSpec((None, 2, FFT_P), lambda d: (d, 0, 0)),
                 pl.BlockSpec((2, FFT_P), lambda d: (0, 0))]


def _kspec_kernel(a_ref, fr_ref, fi_ref, tw_ref, twh_ref, o_ref, oh_ref):
    dd = pl.program_id(0)
    dot = functools.partial(jnp.dot, preferred_element_type=F32)

    def combine(xf, xb, out_ref):
        out_ref[0] = xf[:FFT_P] + xb[:FFT_P]
        out_ref[1] = xf[FFT_P:] - xb[FFT_P:]

    @pl.when(dd > 0)
    def _():
        rm = _cblock(*_twiddled(fr_ref, fi_ref, tw_ref[...])).astype(BF16)
        combine(dot(rm, a_ref[0].reshape(2 * FFT_P, HYW).astype(BF16)),
                dot(rm, a_ref[1].reshape(2 * FFT_P, HYW).astype(BF16)), o_ref)

    @pl.when(dd == 0)
    def _():
        for slot, tw, out_ref in ((0, tw_ref[...], o_ref), (1, twh_ref[...], oh_ref)):
            w = jnp.concatenate(_twiddled(fr_ref, fi_ref, tw), axis=0).astype(BF16)
            combine(dot(w, a_ref[0, slot].astype(BF16)), dot(w, a_ref[1, slot].astype(BF16)), out_ref)


def _kernel_spectrum(ak, fr, fi, tw, twh):
    return pl.pallas_call(
        _kspec_kernel,
        grid=(FFT_H,),
        in_specs=[pl.BlockSpec((2, None, 2, FFT_P, HYW), lambda d: (0, d, 0, 0, 0))] + _STAGE2_SPECS,
        out_specs=(pl.BlockSpec((2, None, FFT_P, HYW), lambda d: (0, d, 0, 0)),
                   pl.BlockSpec((2, FFT_P, HYW), lambda d: (0, 0, 0))),
        out_shape=(jax.ShapeDtypeStruct((2, FFT_H, FFT_P, HYW), F32),
                   jax.ShapeDtypeStruct((2, FFT_P, HYW), F32)),
        compiler_params=_cparams(1, VMEM_LIMIT),
        name="kernel_spectrum",
    )(ak, fr, fi, tw, twh)


def _mid_kernel(a_ref, kh_ref, khh_ref, fr_ref, fi_ref, tw_ref, twh_ref, o_ref):
    dd = pl.program_id(0)
    dot = functools.partial(jnp.dot, preferred_element_type=F32)
    dot_t = lambda w, y: lax.dot_general(w, y, _TN_DIMS, preferred_element_type=F32)

    @pl.when(dd > 0)
    def _():
        rm = _cblock(*_twiddled(fr_ref, fi_ref, tw_ref[...])).astype(BF16)
        for n in range(B):
            x = dot(rm, a_ref[n].reshape(2 * FFT_P, HYW).astype(BF16))
            y = _cmul_rows(x, kh_ref[0], kh_ref[1])
            o_ref[n] = dot_t(rm, y).reshape(2, FFT_P, HYW)

    @pl.when(dd == 0)
    def _():
        for slot, tw, k_ref in ((0, tw_ref[...], kh_ref), (1, twh_ref[...], khh_ref)):
            w = jnp.concatenate(_twiddled(fr_ref, fi_ref, tw), axis=0).astype(BF16)
            for n in range(B):
                y = _cmul_rows(dot(w, a_ref[n, slot].astype(BF16)), k_ref[0], k_ref[1])
                o_ref[n, slot] = dot_t(w, y)


def _fft_mid(au, kh, khh, fr, fi, tw, twh):
    pair = pl.BlockSpec((B, None, 2, FFT_P, HYW), lambda d: (0, d, 0, 0, 0))
    return pl.pallas_call(
        _mid_kernel,
        grid=(FFT_H,),
        in_specs=[pair,
                  pl.BlockSpec((2, None, FFT_P, HYW), lambda d: (0, d, 0, 0)),
                  pl.BlockSpec((2, FFT_P, HYW), lambda d: (0, 0, 0))] + _STAGE2_SPECS,
        out_specs=pair,
        out_shape=jax.ShapeDtypeStruct((B, FFT_H, 2, FFT_P, HYW), F32),
        compiler_params=_cparams(1, VMEM_LIMIT),
        name="fft_mid",
    )(au, kh, khh, fr, fi, tw, twh)


def _dft_tables():
    na = L // FFT_P
    a = np.arange(na)
    dd = np.arange(FFT_H)
    ang = 2.0 * np.pi * np.outer(dd, a) / FFT_P
    re_rows = np.cos(ang)
    im_rows = -np.sin(ang)
    im_rows[0] = np.cos(np.pi * a)
    f_first = np.stack([re_rows, im_rows], axis=1).reshape(FFT_P, na)
    gre = 2.0 * np.cos(ang)
    gim = -2.0 * np.sin(ang)
    gre[0] = 1.0
    gim[0] = np.cos(np.pi * a)
    g_last = np.stack([gre, gim], axis=1).reshape(FFT_P, na).T / FFT_N
    b = np.arange(FFT_P)
    angf = 2.0 * np.pi * np.outer(b, b) / FFT_P
    ang2 = 2.0 * np.pi * np.outer(np.arange(FFT_H + 1), b) / FFT_N
    tw = np.stack([np.cos(ang2), -np.sin(ang2)], axis=1)
    f32 = lambda x: jnp.asarray(x.astype(np.float32))
    return (f32(f_first).astype(BF16), f32(g_last).astype(BF16), f32(np.cos(angf)), f32(-np.sin(angf)),
            f32(tw[:FFT_H]), f32(tw[FFT_H]))


def _merge_kernel(ya_ref, cv_ref, vx_ref, x0_ref, ga_ref, gb_ref, x_ref, er_ref, ec_ref, gt1_ref, hyd_ref,
                  wpa_ref, wpb_ref, wo_ref, g2_ref, sh2_ref, sc2_ref, wr_ref, br_ref,
                  x1_ref, t2_ref, lg_ref):
    vx = vx_ref[...]
    yb = x0_ref[...] * (cv_ref[...] + vx * hyd_ref[...])
    pa = jnp.dot(ya_ref[...].astype(BF16), wpa_ref[...], preferred_element_type=F32)
    pb = jnp.dot(yb.astype(BF16), wpb_ref[...], preferred_element_type=F32)
    mixed = (jax.nn.sigmoid(ga_ref[...].astype(F32)) * pa
             + jax.nn.sigmoid(gb_ref[...].astype(F32)) * pb)
    xm = jnp.dot(mixed.astype(BF16), wo_ref[...], preferred_element_type=F32)
    nrow = TM_MG // GRID_W
    row0 = (pl.program_id(0) % (L // TM_MG)) * nrow
    x1 = x_ref[...] + _pos_tile(er_ref, ec_ref, row0, nrow) + gt1_ref[...] * xm
    x1_ref[...] = x1
    t2 = _rms(x1) * g2_ref[...]
    t2 = t2 * (1.0 + sc2_ref[...]) + sh2_ref[...]
    t2_ref[...] = t2
    t_hi = t2.astype(BF16)
    t_lo = (t2 - t_hi.astype(F32)).astype(BF16)
    rr = (jnp.dot(t_hi, wr_ref[...], preferred_element_type=F32)
          + jnp.dot(t_lo, wr_ref[...], preferred_element_type=F32))
    lg_ref[...] = rr[:, :V7X_LANES] + rr[:, V7X_LANES:] + br_ref[...]


def _merge(ya, cv, vx, x0c, p, x2, er, ec, gt1, hyd, wpa, wpb, wo, g2, sh2, sc2, wr, br):
    tpb = L // TM_MG
    half = lambda: pl.BlockSpec((TM_MG, HYW), lambda i: (i, 0))
    full = lambda shape: pl.BlockSpec(shape, lambda i: tuple(0 for _ in shape))
    perb = lambda: pl.BlockSpec((None, 1, D), lambda i: (i // tpb, 0, 0))
    return pl.pallas_call(
        _merge_kernel,
        grid=(T // TM_MG,),
        in_specs=[half(), half(), half(), half(),
                  pl.BlockSpec((TM_MG, D), lambda i: (i, 3)),
                  pl.BlockSpec((TM_MG, D), lambda i: (i, 4)),
                  pl.BlockSpec((TM_MG, D), lambda i: (i, 0)),
                  full((L // GRID_W, D // 2)), full((GRID_W, D // 2)),
                  perb(), full((1, HYW)),
                  full((KD, D)), full((HYW, D)), full((D, D)),
                  full((1, D)), perb(), perb(),
                  full((D, 2 * V7X_LANES)), full((1, V7X_LANES))],
        out_specs=(pl.BlockSpec((TM_MG, D), lambda i: (i, 0)),
                   pl.BlockSpec((TM_MG, D), lambda i: (i, 0)),
                   pl.BlockSpec((TM_MG, V7X_LANES), lambda i: (i, 0))),
        out_shape=(jax.ShapeDtypeStruct((T, D), F32), jax.ShapeDtypeStruct((T, D), F32),
                   jax.ShapeDtypeStruct((T, V7X_LANES), F32)),
        compiler_params=_cparams(1, VMEM_LIMIT),
        name="merge",
    )(ya, cv, vx, x0c, p, p, x2, er, ec, gt1, hyd, wpa, wpb, wo, g2, sh2, sc2, wr, br)


def _route_kernel(lg_ref, info_ref, cnt_ref):
    @pl.when(pl.program_id(0) == 0)
    def _():
        cnt_ref[...] = jnp.zeros_like(cnt_ref)

    lg = lg_ref[...]
    lane = lax.broadcasted_iota(I32, lg.shape, 1)
    lanef = lane.astype(F32)
    neg = -1e30
    big = 1e9
    is_g = (lane >= NEXP) & (lane < NEXP + NGRP)
    gl = jnp.where(is_g, lg, neg)
    ge = jnp.where(is_g, jnp.exp(gl - jnp.max(gl, axis=-1, keepdims=True)), 0.0)
    pg = ge / jnp.sum(ge, axis=-1, keepdims=True)
    p_top_g = jnp.max(pg, axis=-1, keepdims=True)
    gidx = jnp.min(jnp.where(is_g & (pg == p_top_g), lanef, big), axis=-1, keepdims=True)
    g_sel = gidx.astype(I32) - NEXP
    emask = (lane < NEXP) & (jnp.right_shift(lane, NEPG.bit_length() - 1) == g_sel)
    el = jnp.where(emask, lg, neg)
    ee = jnp.where(emask, jnp.exp(el - jnp.max(el, axis=-1, keepdims=True)), 0.0)
    pe = ee / jnp.sum(ee, axis=-1, keepdims=True)
    p1 = jnp.max(jnp.where(emask, pe, -1.0), axis=-1, keepdims=True)
    i1 = jnp.min(jnp.where(emask & (pe == p1), lanef, big), axis=-1, keepdims=True)
    rest = emask & (lanef != i1)
    p2 = jnp.max(jnp.where(rest, pe, -1.0), axis=-1, keepdims=True)
    i2 = jnp.min(jnp.where(rest & (pe == p2), lanef, big), axis=-1, keepdims=True)
    wsum = p1 + p2
    w1 = p_top_g * p1 / wsum
    w2 = p_top_g * p2 / wsum
    sel1 = lanef == i1
    sel2 = lanef == i2
    oh = jnp.where(sel1 | sel2, 1.0, 0.0)
    r = lax.broadcasted_iota(I32, (TR, TR), 0)
    c = lax.broadcasted_iota(I32, (TR, TR), 1)
    stril = jnp.where(c < r, 1.0, 0.0).astype(BF16)
    before = jnp.dot(stril, oh.astype(BF16), preferred_element_type=F32) + cnt_ref[...]
    r1 = jnp.sum(jnp.where(sel1, before, 0.0), axis=-1, keepdims=True)
    r2 = jnp.sum(jnp.where(sel2, before, 0.0), axis=-1, keepdims=True)
    cnt_ref[...] += jnp.sum(oh, axis=0, keepdims=True)
    info = jnp.where(lane == 0, i1, jnp.where(lane == 1, r1, jnp.where(lane == 2, i2, jnp.where(
        lane == 3, r2, jnp.where(lane == 4, w1, jnp.where(lane == 5, w2, 0.0))))))
    info_ref[...] = info


def _route(lg):
    return pl.pallas_call(
        _route_kernel,
        grid=(T // TR,),
        in_specs=[pl.BlockSpec((TR, V7X_LANES), lambda i: (i, 0))],
        out_specs=(pl.BlockSpec((TR, V7X_LANES), lambda i: (i, 0)),
                   pl.BlockSpec((1, V7X_LANES), lambda i: (0, 0))),
        out_shape=(jax.ShapeDtypeStruct((T, V7X_LANES), F32), jax.ShapeDtypeStruct((1, V7X_LANES), F32)),
        compiler_params=_cparams(1, VMEM_LIMIT),
        name="route",
    )(lg)


def _positions_kernel(info_ref, st_ref, o_ref):
    info = info_ref[...]
    lane = lax.broadcasted_iota(I32, info.shape, 1)
    lanef = lane.astype(F32)
    st = st_ref[...]
    row = lambda e_lane, r_lane: (jnp.sum(jnp.where(lanef == _lane_pick(info, lane, e_lane), st, 0.0),
                                          axis=-1, keepdims=True) + _lane_pick(info, lane, r_lane))
    o_ref[...] = jnp.where(lane == 0, row(0, 1), jnp.where(lane == 1, row(2, 3), 0.0)).astype(I32)


def _positions(info, starts_row):
    return pl.pallas_call(
        _positions_kernel,
        grid=(T // TPOS,),
        in_specs=[pl.BlockSpec((TPOS, V7X_LANES), lambda i: (i, 0)),
                  pl.BlockSpec((1, V7X_LANES), lambda i: (0, 0))],
        out_specs=pl.BlockSpec((TPOS, V7X_LANES), lambda i: (i, 0)),
        out_shape=jax.ShapeDtypeStruct((T, V7X_LANES), I32),
        compiler_params=_cparams(1, VMEM_LIMIT),
        name="positions",
    )(info, starts_row)


def _scatter_kernel(pos_ref, zt_ref, t2_ref, zeros_hbm, xs_hbm, sem, zsem):
    i = pl.program_id(0)

    def zcopy(row):
        start = pl.multiple_of(jnp.maximum(row, 0), TE)
        return pltpu.make_async_copy(zeros_hbm, xs_hbm.at[pl.ds(start, TE)], zsem)

    @pl.when(i == 0)
    def _():
        def ztail(start, e, carry):
            @pl.when(zt_ref[0, e] >= 0)
            def _():
                cp = zcopy(zt_ref[0, e])
                cp.start() if start else cp.wait()
            return carry

        lax.fori_loop(0, NEXP, functools.partial(ztail, True), 0)
        lax.fori_loop(0, NEXP, functools.partial(ztail, False), 0)

        def zrest(start, tile, carry):
            cp = zcopy(tile * TE)
            cp.start() if start else cp.wait()
            return carry

        lax.fori_loop(zt_ref[0, NA_OFF], NT_EXP, functools.partial(zrest, True), 0)
        lax.fori_loop(zt_ref[0, NA_OFF], NT_EXP, functools.partial(zrest, False), 0)

    def row_copy(base, jj, kk):
        dst = pos_ref[0, 2 * (base + jj) + kk]
        return pltpu.make_async_copy(t2_ref.at[pl.ds(base, ROW_GROUP)].at[pl.ds(jj, 1)],
                                     xs_hbm.at[pl.ds(dst, 1)], sem)

    _start_rows(TS, row_copy)
    _wait_rows(TS, pltpu.make_async_copy(t2_ref.at[pl.ds(0, 1)], xs_hbm.at[pl.ds(0, 1)], sem))


def _start_rows(n_rows, row_copy):
    def group(g, carry):
        base = pl.multiple_of(g * ROW_GROUP, ROW_GROUP)
        for jj in range(ROW_GROUP):
            for kk in range(2):
                row_copy(base, jj, kk).start(priority=kk)
        return carry

    lax.fori_loop(0, n_rows // ROW_GROUP, group, 0)


def _wait_rows(n_rows, one_row_copy):
    def drain(j, carry):
        one_row_copy.wait()
        one_row_copy.wait()
        return carry

    lax.fori_loop(0, n_rows, drain, 0, unroll=ROW_UNROLL)


def _scatter_rows(pos3, meta, t2, zeros_tile):
    return pl.pallas_call(
        _scatter_kernel,
        grid=(T // TS,),
        in_specs=[pl.BlockSpec((None, 1, 2 * TS), lambda i: (i, 0, 0), memory_space=pltpu.SMEM),
                  pl.BlockSpec(memory_space=pltpu.SMEM),
                  pl.BlockSpec((TS, D), lambda i: (i, 0)),
                  pl.BlockSpec(memory_space=pl.ANY)],
        out_specs=pl.BlockSpec(memory_space=pl.ANY),
        out_shape=jax.ShapeDtypeStruct((NP_ROWS, D), F32),
        scratch_shapes=[pltpu.SemaphoreType.DMA(()), pltpu.SemaphoreType.DMA(())],
        compiler_params=_cparams(1, VMEM_LIMIT),
        name="scatter_rows",
    )(pos3, meta, t2, zeros_tile)


def _expert_kernel(te_ref, nx_ref, sl_ref, na_ref, xs_ref, wg_hbm, wu_hbm, wd_hbm, ys_ref,
                   wgs_ref, wus_ref, wds_ref, wgb_ref, wub_ref, wdb_ref, sem):
    i = pl.program_id(0)
    active = i < na_ref[0]
    first = active & ((i == 0) | (te_ref[i] != te_ref[jnp.maximum(i - 1, 0)]))

    def fetch(e, slot):
        return [pltpu.make_async_copy(wg_hbm.at[e], wgs_ref.at[slot], sem.at[slot]),
                pltpu.make_async_copy(wu_hbm.at[e], wus_ref.at[slot], sem.at[slot]),
                pltpu.make_async_copy(wd_hbm.at[e], wds_ref.at[slot], sem.at[slot])]

    for slot in range(2):
        @pl.when(first & (sl_ref[i] == slot))
        def _():
            @pl.when(i == 0)
            def _():
                for cp in fetch(te_ref[i], slot):
                    cp.start()

            for cp in fetch(te_ref[i], slot):
                cp.wait()
            wgb_ref[...] = wgs_ref[slot].astype(BF16)
            wub_ref[...] = wus_ref[slot].astype(BF16)
            wdb_ref[...] = wds_ref[slot].astype(BF16)

            @pl.when(nx_ref[i] >= 0)
            def _():
                for cp in fetch(nx_ref[i], 1 - slot):
                    cp.start()

    @pl.when(active)
    def _():
        x = xs_ref[...].astype(BF16)
        g = jnp.dot(x, wgb_ref[...], preferred_element_type=F32)
        u = jnp.dot(x, wub_ref[...], preferred_element_type=F32)
        hid = (g * jax.nn.sigmoid(g) * u).astype(BF16)
        ys_ref[...] = jnp.dot(hid, wdb_ref[...], preferred_element_type=F32)

    @pl.when(jnp.logical_not(active))
    def _():
        ys_ref[...] = jnp.zeros_like(ys_ref)


def _experts(tile_expert, tile_next, tile_slot, n_active, xs, wg, wu, wd):
    rows = lambda i, *_: (i, 0)
    rows_in = lambda i, te, nx, sl, na: (jnp.minimum(i, na[0] - 1), 0)
    any_space = pl.BlockSpec(memory_space=pl.ANY)
    grid_spec = pltpu.PrefetchScalarGridSpec(
        num_scalar_prefetch=4,
        grid=(NT_EXP,),
        in_specs=[pl.BlockSpec((TE, D), rows_in), any_space, any_space, any_space],
        out_specs=pl.BlockSpec((TE, D), rows),
        scratch_shapes=[pltpu.VMEM((2, D, DEXP), F32), pltpu.VMEM((2, D, DEXP), F32), pltpu.VMEM((2, DEXP, D), F32),
                        pltpu.VMEM((D, DEXP), BF16), pltpu.VMEM((D, DEXP), BF16), pltpu.VMEM((DEXP, D), BF16),
                        pltpu.SemaphoreType.DMA((2,))],
    )
    return pl.pallas_call(
        _expert_kernel,
        grid_spec=grid_spec,
        out_shape=jax.ShapeDtypeStruct((NP_ROWS, D), F32),
        compiler_params=_cparams(1, VMEM_LIMIT),
        name="experts",
    )(tile_expert, tile_next, tile_slot, n_active, xs, wg, wu, wd)


def _combine_kernel(pos_ref, posn_ref, info_ref, x1_ref, gt2_ref, fg_ref, ys_hbm, o_ref, buf, sem):
    g = pl.program_id(0)
    ng = pl.num_programs(0)

    def gather(p_ref, half, slot):
        def row_copy(base, jj, kk):
            src = p_ref[0, 2 * (half * TC + base + jj) + kk]
            return pltpu.make_async_copy(ys_hbm.at[pl.ds(src, 1)],
                                         buf.at[slot, kk].at[pl.ds(base, ROW_GROUP)].at[pl.ds(jj, 1)],
                                         sem.at[slot])
        _start_rows(TC, row_copy)

    def finish(half, slot):
        _wait_rows(TC, pltpu.make_async_copy(ys_hbm.at[pl.ds(0, 1)], buf.at[slot, 0].at[pl.ds(0, 1)],
                                             sem.at[slot]))
        rows = slice(half * TC, (half + 1) * TC)
        info = info_ref[rows, :]
        lane = lax.broadcasted_iota(I32, info.shape, 1)
        moe = _lane_pick(info, lane, 4) * buf[slot, 0] + _lane_pick(info, lane, 5) * buf[slot, 1]
        x2 = x1_ref[rows, :] + gt2_ref[...] * moe
        o_ref[rows, :] = _rms(x2) * fg_ref[...]

    @pl.when(g == 0)
    def _():
        gather(pos_ref, 0, 0)

    gather(pos_ref, 1, 1)
    finish(0, 0)

    @pl.when(g + 1 < ng)
    def _():
        gather(posn_ref, 0, 0)

    finish(1, 1)


def _combine(pos3, info, x1, gt2, fg, ys):
    step = 2 * TC
    tpb = L // step
    nsteps = T // step
    return pl.pallas_call(
        _combine_kernel,
        grid=(nsteps,),
        in_specs=[pl.BlockSpec((None, 1, 2 * step), lambda i: (i, 0, 0), memory_space=pltpu.SMEM),
                  pl.BlockSpec((None, 1, 2 * step), lambda i: (jnp.minimum(i + 1, nsteps - 1), 0, 0),
                               memory_space=pltpu.SMEM),
                  pl.BlockSpec((step, V7X_LANES), lambda i: (i, 0)),
                  pl.BlockSpec((step, D), lambda i: (i, 0)),
                  pl.BlockSpec((None, 1, D), lambda i: (i // tpb, 0, 0)),
                  pl.BlockSpec((1, D), lambda i: (0, 0)),
                  pl.BlockSpec(memory_space=pl.ANY)],
        out_specs=pl.BlockSpec((step, D), lambda i: (i, 0)),
        out_shape=jax.ShapeDtypeStruct((T, D), F32),
        scratch_shapes=[pltpu.VMEM((2, 2, TC, D), F32), pltpu.SemaphoreType.DMA((2,))],
        compiler_params=_cparams(1, VMEM_LIMIT),
        name="combine",
    )(pos3, pos3, info, x1, gt2, fg, ys)


def _pos_tables():
    rows = L // GRID_W
    quarter = D // 4
    omega = 1.0 / (10000.0 ** (jnp.arange(quarter, dtype=F32) / quarter))

    def axis_emb(pos):
        a = pos[:, None] * omega[None, :]
        return jnp.concatenate([jnp.sin(a), jnp.cos(a)], axis=-1)

    er = axis_emb(jnp.arange(rows, dtype=F32))
    ec = axis_emb(jnp.arange(GRID_W, dtype=F32))
    return er, ec


def _filter_features():
    z = np.zeros((L, V7X_LANES), np.float64)
    bands = (HY_EMB - 1) // 2
    ang = (2.0 * np.pi * np.arange(L) / L)[:, None] * np.linspace(1e-4, bands - 1, bands)[None, :]
    z[:, 0] = np.linspace(0.0, 1.0, L)
    z[:, 1:1 + bands] = np.cos(ang)
    z[:, 1 + bands:HY_EMB] = -np.sin(ang)
    z[1:, HY_EMB] = 1.0
    return jnp.asarray(z.astype(np.float32))


def _pad2(a, rows, cols):
    return jnp.pad(a, ((0, rows - a.shape[0]), (0, cols - a.shape[1])))


def kernel(x, c, ctx, c_ctx, ada_w, ada_b, norm1_g, norm2_g, w_in, hgrn_lb, hgrn_norm_g, hy_conv_w, hy_conv_b, hy_filt_w1, hy_filt_b1, hy_filt_freq1, hy_filt_w2, hy_filt_b2, hy_filt_freq2, hy_filt_w3, hy_d, w_proj_a, w_proj_b, w_out, moe_router_g_w, moe_router_g_b, moe_router_e_w, moe_router_e_b, moe_w_gate, moe_w_up, moe_w_down, final_norm_g):
    cvec = jnp.zeros((8, D), F32).at[0:B].set(c).at[B].set(c_ctx)
    mod = _adaln(cvec, ada_w[0], ada_b[0][None, :])
    m6 = mod.reshape(8, 6, D)
    sh1, sc1, gt1, sh2, sc2, gt2 = [m6[0:B, k][:, None, :] for k in range(6)]
    csh1, csc1 = m6[B:B + 1, 0], m6[B:B + 1, 1]

    lbs = jnp.cumsum(jax.nn.softmax(hgrn_lb.astype(F32), axis=0), axis=0)[0]
    g1 = norm1_g[0][None, :]
    er, ec = _pos_tables()
    x2 = x.reshape(T, D)

    w_ctx = w_in[0][:, KD:4 * KD].astype(BF16)
    s_f, s_b = _context_states(ctx, g1, csh1, csc1, w_ctx, lbs)

    pz, p = _in_projection(x2, er, ec, g1, sh1, sc1, w_in[0].astype(BF16))

    o_f = _hgrn_scan(pz, p, lbs[0:1], s_f, False)
    y_a = _hgrn_scan(pz, p, lbs[1:2], s_b, True, o_f=o_f, norm_g=hgrn_norm_g[0][None, :])

    vx, x0c = _hyena_pre(p, hy_conv_w[0], hy_conv_b[0][None, :])
    deltas = jnp.abs(jnp.linspace(math.log(HY_DECAY_TARGET) / HY_SLOW_PCT,
                                  math.log(HY_DECAY_TARGET) / HY_FAST_PCT, HYW, dtype=F32))[None, :]
    ln = V7X_LANES
    fh = hy_filt_w2.shape[-1]
    blockdiag = lambda m: jnp.concatenate([_pad2(m, m.shape[0], 2 * m.shape[1]),
                                           jnp.pad(m, ((0, 0), (m.shape[1], 0)))], axis=0)
    twice = lambda v: jnp.concatenate([v, v])[None, :]
    w3 = hy_filt_w3[0]
    taps = _filter_taps(
        _filter_features(),
        blockdiag(_pad2(hy_filt_w1[0], ln, fh)), twice(hy_filt_b1[0]), twice(hy_filt_freq1[0]),
        blockdiag(hy_filt_w2[0]), twice(hy_filt_b2[0]), twice(hy_filt_freq2[0]),
        _pad2(w3, ln, 2 * HYW), jnp.pad(w3, ((fh, 0), (0, 0))), deltas)
    f_first, g_last, fr, fi, tw, twh = _dft_tables()
    na = L // FFT_P
    ak = _strided_dft(taps.reshape(2, na, FFT_P, HYW), f_first, "dft_first_taps")
    kh, khh = _kernel_spectrum(ak.reshape(2, FFT_H, 2, FFT_P, HYW), fr, fi, tw, twh)
    au = _strided_dft(vx.reshape(B, na, FFT_P, HYW), f_first, "dft_first")
    bp = _fft_mid(au.reshape(B, FFT_H, 2, FFT_P, HYW), kh, khh, fr, fi, tw, twh)
    conv = _strided_dft(bp.reshape(B, FFT_P, FFT_P, HYW), g_last, "dft_last").reshape(T, HYW)

    wr = jnp.concatenate([jnp.transpose(moe_router_e_w[0], (1, 0, 2)).reshape(D, NEXP),
                          moe_router_g_w[0], jnp.zeros((D, V7X_LANES - NEXP - NGRP), F32)], axis=1)
    wr_hi = wr.astype(BF16)
    wr = jnp.concatenate([wr_hi, (wr - wr_hi.astype(F32)).astype(BF16)], axis=1)
    br = jnp.concatenate([moe_router_e_b[0].reshape(NEXP), moe_router_g_b[0],
                          jnp.zeros((V7X_LANES - NEXP - NGRP,), F32)])[None, :]
    x1, t2, lg = _merge(y_a.reshape(T, KD), conv, vx, x0c, p, x2, er, ec, gt1, hy_d[0][None, :],
                        w_proj_a[0].astype(BF16), w_proj_b[0].astype(BF16), w_out[0].astype(BF16),
                        norm2_g[0][None, :], sh2, sc2, wr, br)

    info, counts = _route(lg)
    cnt = counts[0, :NEXP].astype(I32)
    pc = ((cnt + TE - 1) // TE) * TE
    ends = jnp.cumsum(pc)
    starts = ends - pc
    n_active = (ends[-1] // TE).astype(I32)[None]
    tile_rows = jnp.arange(NT_EXP, dtype=I32) * TE
    tile_expert = jnp.minimum(jnp.sum((ends[None, :] <= tile_rows[:, None]).astype(I32), axis=1), NEXP - 1)
    meta = jnp.concatenate([jnp.where(pc > 0, ends - TE, -1), starts, n_active]).astype(I32)[None, :]
    starts_row = jnp.pad(starts.astype(F32), (0, V7X_LANES - NEXP))[None, :]
    pos3 = _positions(info, starts_row)[:, :2].reshape(T // TS, 1, 2 * TS)

    xs = _scatter_rows(pos3, meta, t2, jnp.zeros((TE, D), F32))
    eid = jnp.arange(NEXP, dtype=I32)
    nonempty = pc > 0
    later = jnp.where(nonempty[None, :] & (eid[None, :] > eid[:, None]), eid[None, :], NEXP)
    next_e = jnp.min(later, axis=1)
    next_e = jnp.where(next_e < NEXP, next_e, -1).astype(I32)
    slot_e = ((jnp.cumsum(nonempty.astype(I32)) - 1) % 2).astype(I32)
    ys = _experts(tile_expert, next_e[tile_expert], slot_e[tile_expert], n_active, xs,
                  moe_w_gate[0].reshape(NEXP, D, DEXP), moe_w_up[0].reshape(NEXP, D, DEXP),
                  moe_w_down[0].reshape(NEXP, DEXP, D))
    out = _combine(pos3, info, x1, gt2, final_norm_g[None, :], ys)
    return out.reshape(B, L, D)
```

```python
import functools
import math

import numpy as np
import jax
import jax.numpy as jnp
from jax import lax
from jax.experimental import pallas as pl
from jax.experimental.pallas import tpu as pltpu

F32 = jnp.float32
BF16 = jnp.bfloat16
I32 = jnp.int32
HIGHEST = lax.Precision.HIGHEST

D = 1024
B = 2
L = 8192
T = B * L
CTX = 256
GRID_W = 64
EPS = 1e-6
H = 4
DK = 128
DV = 128
KD = H * DK
IN_W = 6144
HYW = 512
HY_EMB = 33
NGRP = 4
NEPG = 8
NEXP = NGRP * NEPG
DEXP = 512
HY_DECAY_TARGET = 1e-2
HY_FAST_PCT = 0.3
HY_SLOW_PCT = 1.5

V7X_LANES = 128
V7X_SUBLANES = 8
V7X_VMEM_BYTES = 64 * 1024 * 1024
VMEM_LIMIT = (3 * V7X_VMEM_BYTES) // 4
VMEM_LIMIT_IN = (7 * V7X_VMEM_BYTES) // 8

FFT_N = 2 * L
FFT_P = 128
FFT_BB = 8

TM_IN = 2048
TN_IN = 1024
TH = 128
CB = 32
TM_HY = 1024
HALO = 2 * V7X_SUBLANES
TM_MG = 512
TR = 512
TE = 512
TPOS = 2048
NP_ROWS = 2 * T + NEXP * TE
NT_EXP = NP_ROWS // TE
TS = 512
TC = TS // 2
ROW_UNROLL = 8
ROW_GROUP = 32
ST_OFF = NEXP
NA_OFF = 2 * NEXP


def _cparams(n_axes, vmem=None):
    return pltpu.CompilerParams(dimension_semantics=("arbitrary",) * n_axes,
                                vmem_limit_bytes=vmem)


def _split3(x):
    hi = x.astype(BF16)
    r = x - hi.astype(F32)
    mid = r.astype(BF16)
    lo = (r - mid.astype(F32)).astype(BF16)
    return hi, mid, lo


def _dot01(m, x):
    hi, mid, lo = _split3(x)
    return (jnp.dot(m, hi, preferred_element_type=F32) + jnp.dot(m, mid, preferred_element_type=F32)
            + jnp.dot(m, lo, preferred_element_type=F32))


def _rms(x):
    return x * lax.rsqrt(jnp.mean(x * x, axis=-1, keepdims=True) + EPS)


def _lane_pick(x, lane, idx):
    return jnp.sum(jnp.where(lane == idx, x, 0.0), axis=-1, keepdims=True)


def _ada_kernel(c_ref, w_ref, b_ref, o_ref):
    c = c_ref[...]
    s = c * jax.nn.sigmoid(c)
    o_ref[...] = jnp.dot(s, w_ref[...], preferred_element_type=F32, precision=HIGHEST) + b_ref[...]


def _adaln(cvec, w, b):
    tn = 1536
    return pl.pallas_call(
        _ada_kernel,
        grid=(6 * D // tn,),
        in_specs=[pl.BlockSpec((8, D), lambda j: (0, 0)),
                  pl.BlockSpec((D, tn), lambda j: (0, j)),
                  pl.BlockSpec((1, tn), lambda j: (0, j))],
        out_specs=pl.BlockSpec((8, tn), lambda j: (0, j)),
        out_shape=jax.ShapeDtypeStruct((8, 6 * D), F32),
        compiler_params=_cparams(1, VMEM_LIMIT),
        name="adaln",
    )(cvec, w, b)


def _keys(z, lb):
    sig = jax.nn.sigmoid(z)
    logf = jnp.log(lb + (1.0 - lb) * sig)
    k = (1.0 - lb) * jax.nn.sigmoid(-z)
    return k, logf


def _ctx_kernel(ctx_ref, g_ref, sh_ref, sc_ref, w_ref, lb_ref, sf_ref, sb_ref):
    h = _rms(ctx_ref[...]) * g_ref[...]
    h = h * (1.0 + sc_ref[...]) + sh_ref[...]
    p = jnp.dot(h.astype(BF16), w_ref[...], preferred_element_type=F32)
    zf, zb, v = p[:, :KD], p[:, KD:2 * KD], p[:, 2 * KD:]
    kf, lf = _keys(zf, lb_ref[0:1, :])
    kb, lbk = _keys(zb, lb_ref[1:2, :])
    r = lax.broadcasted_iota(I32, (CTX, CTX), 0)
    c = lax.broadcasted_iota(I32, (CTX, CTX), 1)
    tril = jnp.where(c <= r, 1.0, 0.0).astype(BF16)
    cf = _dot01(tril, lf)
    cb = _dot01(tril, lbk)
    kfd = (kf * jnp.exp(cf[CTX - 1:CTX, :] - cf)).astype(BF16)
    kbd = (kb * jnp.exp(cb - lbk)).astype(BF16)
    vb = v.astype(BF16)
    tn = (((0,), (0,)), ((), ()))
    for hh in range(H):
        hs = slice(hh * DK, (hh + 1) * DK)
        sf_ref[hh] = lax.dot_general(vb[:, hs], kfd[:, hs], tn, preferred_element_type=F32)
        sb_ref[hh] = lax.dot_general(vb[:, hs], kbd[:, hs], tn, preferred_element_type=F32)


def _context_states(ctx, g1, csh1, csc1, w_ctx, lbs):
    st = jax.ShapeDtypeStruct((B, H, DV, DK), F32)
    return pl.pallas_call(
        _ctx_kernel,
        grid=(B,),
        in_specs=[pl.BlockSpec((None, CTX, D), lambda b: (b, 0, 0)),
                  pl.BlockSpec((1, D), lambda b: (0, 0)),
                  pl.BlockSpec((1, D), lambda b: (0, 0)),
                  pl.BlockSpec((1, D), lambda b: (0, 0)),
                  pl.BlockSpec((D, 3 * KD), lambda b: (0, 0)),
                  pl.BlockSpec((2, KD), lambda b: (0, 0))],
        out_specs=(pl.BlockSpec((None, H, DV, DK), lambda b: (b, 0, 0, 0)),
                   pl.BlockSpec((None, H, DV, DK), lambda b: (b, 0, 0, 0))),
        out_shape=(st, st),
        compiler_params=_cparams(1, VMEM_LIMIT),
        name="ctx_states",
    )(ctx, g1, csh1, csc1, w_ctx, lbs)


def _pos_tile(er_ref, ec_ref, row0, nrow):
    lo = jnp.concatenate([jnp.broadcast_to(er_ref[pl.ds(row0 + i, 1), :], (GRID_W, D // 2))
                          for i in range(nrow)], axis=0)
    hi = jnp.concatenate([ec_ref[...]] * nrow, axis=0)
    return jnp.concatenate([lo, hi], axis=1)


def _inproj_kernel(x_ref, er_ref, ec_ref, g_ref, sh_ref, sc_ref, w_ref, oz_ref, o_ref, hx_ref):
    j = pl.program_id(1)

    @pl.when(j == 0)
    def _():
        nrow = TM_IN // GRID_W
        row0 = (pl.program_id(0) % (L // TM_IN)) * nrow
        h = _rms(x_ref[...] + _pos_tile(er_ref, ec_ref, row0, nrow)) * g_ref[...]
        hx_ref[...] = (h * (1.0 + sc_ref[...]) + sh_ref[...]).astype(BF16)

    r = jnp.dot(hx_ref[...], w_ref[...], preferred_element_type=F32)

    @pl.when(j == 0)
    def _():
        o_ref[:, :KD] = r[:, :KD].astype(BF16)
        oz_ref[:, :KD] = r[:, KD:]

    @pl.when(j == 1)
    def _():
        oz_ref[:, KD:] = r[:, :KD]
        o_ref[:, KD:] = r[:, KD:].astype(BF16)

    @pl.when(j > 1)
    def _():
        o_ref[...] = r.astype(BF16)


def _in_projection(x2, er, ec, g1, sh1, sc1, w_bf):
    tiles_per_batch = L // TM_IN
    return pl.pallas_call(
        _inproj_kernel,
        grid=(T // TM_IN, IN_W // TN_IN),
        in_specs=[pl.BlockSpec((TM_IN, D), lambda i, j: (i, 0), pipeline_mode=pl.Buffered(1)),
                  pl.BlockSpec((L // GRID_W, D // 2), lambda i, j: (0, 0)),
                  pl.BlockSpec((GRID_W, D // 2), lambda i, j: (0, 0)),
                  pl.BlockSpec((1, D), lambda i, j: (0, 0)),
                  pl.BlockSpec((None, 1, D), lambda i, j: (i // tiles_per_batch, 0, 0)),
                  pl.BlockSpec((None, 1, D), lambda i, j: (i // tiles_per_batch, 0, 0)),
                  pl.BlockSpec((D, TN_IN), lambda i, j: (0, j))],
        out_specs=(pl.BlockSpec((TM_IN, TN_IN), lambda i, j: (i, 0)),
                   pl.BlockSpec((TM_IN, TN_IN), lambda i, j: (i, jnp.maximum(j - 1, 0)))),
        out_shape=(jax.ShapeDtypeStruct((T, 2 * KD), F32),
                   jax.ShapeDtypeStruct((T, IN_W - 2 * KD), BF16)),
        scratch_shapes=[pltpu.VMEM((TM_IN, D), BF16)],
        compiler_params=_cparams(2, VMEM_LIMIT_IN),
        name="in_proj",
    )(x2, er, ec, g1, sh1, sc1, w_bf)


def _hgrn_kernel(reverse, readout, *refs):
    if readout:
        q_ref, z_ref, v_ref, lb_ref, s0_ref, of_ref, g_ref, ng_ref, o_ref, st_ref = refs
    else:
        q_ref, z_ref, v_ref, lb_ref, s0_ref, o_ref, st_ref = refs

    @pl.when(pl.program_id(0) == 0)
    def _():
        st_ref[...] = s0_ref[...]

    r = lax.broadcasted_iota(I32, (TH, TH), 0)
    c = lax.broadcasted_iota(I32, (TH, TH), 1)
    cb_shift = CB.bit_length() - 1
    same = jnp.right_shift(r, cb_shift) == jnp.right_shift(c, cb_shift)
    tri_mask = same & ((c >= r) if reverse else (c <= r))
    tri = jnp.where(tri_mask, 1.0, 0.0).astype(BF16)
    rblk = jnp.right_shift(r, cb_shift)
    cblk = jnp.right_shift(c, cb_shift)
    dist = (rblk - cblk) if not reverse else (cblk - rblk)
    for b in range(B):
        _hgrn_chunk(reverse, readout, b, tri, tri_mask, dist, refs)


def _hgrn_chunk(reverse, readout, b, tri, tri_mask, dist, refs):
    if readout:
        q_ref, z_ref, v_ref, lb_ref, s0_ref, of_ref, g_ref, ng_ref, o_ref, st_ref = refs
    else:
        q_ref, z_ref, v_ref, lb_ref, s0_ref, o_ref, st_ref = refs
    q = q_ref[b].astype(F32)
    v = v_ref[b]
    k, logf = _keys(z_ref[b], lb_ref[...])
    bl = _dot01(tri, logf)
    nt = (((1,), (1,)), ((), ()))
    tn = (((0,), (0,)), ((), ()))
    nblk = TH // CB
    e_row = 0 if reverse else CB - 1
    m_row = CB - 1 - CB // 2 if reverse else CB // 2
    tau = [bl[jb * CB + e_row:jb * CB + e_row + 1] for jb in range(nblk)]
    mid = [bl[jb * CB + m_row:jb * CB + m_row + 1] for jb in range(nblk)]
    rows = lambda vecs: jnp.concatenate([jnp.broadcast_to(x, (CB, KD)) for x in vecs], axis=0)
    mid_b = rows(mid)
    qd0 = (q * jnp.exp(bl - mid_b)).astype(BF16)
    kd0 = (k * jnp.exp(mid_b - bl)).astype(BF16)
    qs = q * jnp.exp(bl)
    ke = k * jnp.exp(rows(tau) - bl)
    order = list(range(nblk - 1, -1, -1)) if reverse else list(range(nblk))
    pre = [jnp.zeros((1, KD), F32)]
    for i in range(nblk):
        pre.append(pre[-1] + tau[order[i]])
    total = pre[nblk]
    entry = [None] * nblk
    leave = [None] * nblk
    gap = [[None] * nblk for _ in range(nblk)]
    for i, jb in enumerate(order):
        entry[jb] = jnp.exp(pre[i])
        leave[jb] = jnp.exp(total - pre[i + 1])
        for d in range(2, nblk):
            gap[d][jb] = jnp.exp(pre[i + d] - pre[i + 1]) if i + d < nblk else jnp.zeros((1, KD), F32)
    qc = (qs * rows(entry)).astype(BF16)
    kc = (ke * rows(leave)).astype(BF16)
    kx = jnp.concatenate([ke.astype(BF16)] + [(ke * rows(gap[d])).astype(BF16) for d in range(2, nblk)], axis=0)
    qsb = qs.astype(BF16)
    dec = jnp.exp(total)

    def blockdiag(x):
        first_head = lax.broadcasted_iota(I32, x.shape, 1) < DK
        zero = jnp.zeros_like(x)
        return jnp.concatenate([jnp.where(first_head, x, zero), jnp.where(first_head, zero, x)], axis=0)

    for hp in range(H // 2):
        ps = slice(2 * hp * DK, 2 * (hp + 1) * DK)
        sc0 = lax.dot_general(qd0[:, ps], blockdiag(kd0[:, ps]), nt, preferred_element_type=F32)
        scx = lax.dot_general(qsb[:, ps], blockdiag(kx[:, ps]), nt, preferred_element_type=F32)
        halves = []
        for hh in range(2):
            sc = jnp.where(tri_mask, sc0[:, hh * TH:(hh + 1) * TH], 0.0)
            base = hh * (nblk - 1) * TH
            for d in range(1, nblk):
                sc = jnp.where(dist == d, scx[:, base + (d - 1) * TH:base + d * TH], sc)
            halves.append(sc.astype(BF16))
        sc_pair = jnp.concatenate(halves, axis=1)
        st_a = st_ref[b, 2 * hp]
        st_b = st_ref[b, 2 * hp + 1]
        zst = jnp.zeros((DV, DK), BF16)
        st_pair = jnp.concatenate([jnp.concatenate([st_a.astype(BF16), zst], axis=1),
                                   jnp.concatenate([zst, st_b.astype(BF16)], axis=1)], axis=0)
        o_pair = (lax.dot_general(qc[:, ps], st_pair, nt, preferred_element_type=F32)
                  + jnp.dot(sc_pair, blockdiag(v[:, ps]), preferred_element_type=F32))
        upd = lax.dot_general(v[:, ps], kc[:, ps], tn, preferred_element_type=F32)
        st_ref[b, 2 * hp] = st_a * dec[:, ps][:, :DK] + upd[:DV, :DK]
        st_ref[b, 2 * hp + 1] = st_b * dec[:, ps][:, DK:] + upd[DV:, DK:]
        for hh in range(2):
            hs = slice((2 * hp + hh) * DK, (2 * hp + hh + 1) * DK)
            o_h = o_pair[:, hh * DV:(hh + 1) * DV]
            if readout:
                o_h = o_h + of_ref[b, :, hs]
                o_h = _rms(o_h) * ng_ref[...]
                gh = g_ref[b, :, hs].astype(F32)
                o_h = o_h * (gh * jax.nn.sigmoid(gh))
            o_ref[b, :, hs] = o_h


def _hgrn_scan(pz, p, lb_row, s0, reverse, o_f=None, norm_g=None):
    nch = L // TH
    chunk = (lambda c: nch - 1 - c) if reverse else (lambda c: c)
    col_spec = lambda j: pl.BlockSpec((B, TH, KD), lambda c: (0, chunk(c), j))
    in_specs = [col_spec(0), col_spec(1 if reverse else 0), col_spec(1),
                pl.BlockSpec((1, KD), lambda c: (0, 0)),
                pl.BlockSpec((B, H, DV, DK), lambda c: (0, 0, 0, 0))]
    p3 = p.reshape(B, L, p.shape[-1])
    args = [p3, pz.reshape(B, L, 2 * KD), p3, lb_row, s0]
    readout = o_f is not None
    if readout:
        in_specs += [col_spec(0), col_spec(2), pl.BlockSpec((1, DV), lambda c: (0, 0))]
        args += [o_f, p3, norm_g]
    return pl.pallas_call(
        functools.partial(_hgrn_kernel, reverse, readout),
        grid=(nch,),
        in_specs=in_specs,
        out_specs=col_spec(0),
        out_shape=jax.ShapeDtypeStruct((B, L, KD), F32),
        scratch_shapes=[pltpu.VMEM((B, H, DV, DK), F32)],
        compiler_params=_cparams(1, VMEM_LIMIT),
        name="hgrn_bwd_readout" if readout else "hgrn_fwd",
    )(*args)


def _hy_pre_kernel(v_ref, x1_ref, x0_ref, vp_ref, x1p_ref, x0p_ref, vn_ref, x1n_ref, x0n_ref,
                   w_ref, b_ref, vx_ref, x0o_ref):
    i = pl.program_id(1)
    first = i == 0
    last = i == pl.num_programs(1) - 1
    row = lax.broadcasted_iota(I32, (TM_HY, 1), 0)

    def conv(c_ref, p_ref, n_ref, col):
        x = c_ref[...].astype(F32)
        prev_row = jnp.where(first, 0.0, p_ref[...].astype(F32)[HALO - 1:HALO, :])
        next_row = jnp.where(last, 0.0, n_ref[...].astype(F32)[0:1, :])
        xm = jnp.where(row == 0, prev_row, pltpu.roll(x, 1, axis=0))
        xp = jnp.where(row == TM_HY - 1, next_row, pltpu.roll(x, TM_HY - 1, axis=0))
        cs = slice(col * HYW, (col + 1) * HYW)
        return xm * w_ref[0:1, cs] + x * w_ref[1:2, cs] + xp * w_ref[2:3, cs] + b_ref[:, cs]

    v = conv(v_ref, vp_ref, vn_ref, 0)
    x1 = conv(x1_ref, x1p_ref, x1n_ref, 1)
    x0 = conv(x0_ref, x0p_ref, x0n_ref, 2)
    vx_ref[...] = v * x1
    x0o_ref[...] = x0


def _hyena_pre(p, conv_w, conv_b):
    nt = L // TM_HY
    hb = TM_HY // HALO
    nhb = T // HALO
    cur = lambda col: pl.BlockSpec((TM_HY, HYW), lambda b, i: (b * nt + i, col))
    prv = lambda col: pl.BlockSpec((HALO, HYW), lambda b, i: (jnp.maximum((b * nt + i) * hb - 1, 0), col))
    nxt = lambda col: pl.BlockSpec((HALO, HYW), lambda b, i: (jnp.minimum((b * nt + i + 1) * hb, nhb - 1), col))
    c0 = 3
    out = jax.ShapeDtypeStruct((T, HYW), F32)
    return pl.pallas_call(
        _hy_pre_kernel,
        grid=(B, nt),
        in_specs=[cur(c0), cur(c0 + 1), cur(c0 + 2), prv(c0), prv(c0 + 1), prv(c0 + 2),
                  nxt(c0), nxt(c0 + 1), nxt(c0 + 2),
                  pl.BlockSpec((3, 3 * HYW), lambda b, i: (0, 0)),
                  pl.BlockSpec((1, 3 * HYW), lambda b, i: (0, 0))],
        out_specs=(pl.BlockSpec((TM_HY, HYW), lambda b, i: (b * nt + i, 0)),
                   pl.BlockSpec((TM_HY, HYW), lambda b, i: (b * nt + i, 0))),
        out_shape=(out, out),
        compiler_params=_cparams(2, VMEM_LIMIT),
        name="hyena_pre",
    )(p, p, p, p, p, p, p, p, p, conv_w, conv_b)


def _filt_kernel(z_ref, w1_ref, b1_ref, f1_ref, w2_ref, b2_ref, f2_ref, w3a_ref, w3b_ref, dl_ref, o_ref):
    half = z_ref.shape[0] // 2
    zt = z_ref[0:half, :]
    zb = z_ref[half:, :]
    lane = lax.broadcasted_iota(I32, zt.shape, 1)
    dot = functools.partial(jnp.dot, preferred_element_type=F32, precision=HIGHEST)
    h = jnp.sin(f1_ref[...] * (dot(jnp.concatenate([zt, zb], axis=1), w1_ref[...]) + b1_ref[...]))
    h = jnp.sin(f2_ref[...] * (dot(h, w2_ref[...]) + b2_ref[...]))
    for zz, w3_ref, rows in ((zt, w3a_ref, slice(0, half)), (zb, w3b_ref, slice(half, 2 * half))):
        taps = dot(h, w3_ref[...])
        win = jnp.exp(-_lane_pick(zz, lane, 0) * dl_ref[...])
        o_ref[0, rows, :] = taps[:, :HYW] * win
        o_ref[1, rows, :] = taps[:, HYW:] * win * _lane_pick(zz, lane, HY_EMB)


def _filter_taps(zin, w1, b1, f1, w2, b2, f2, w3a, w3b, deltas):
    tm = 2048
    ln = V7X_LANES
    full = lambda shape: pl.BlockSpec(shape, lambda i: (0, 0))
    return pl.pallas_call(
        _filt_kernel,
        grid=(L // tm,),
        in_specs=[pl.BlockSpec((tm, ln), lambda i: (i, 0)),
                  full((2 * ln, ln)), full((1, ln)), full((1, ln)),
                  full((ln, ln)), full((1, ln)), full((1, ln)),
                  full((ln, 2 * HYW)), full((ln, 2 * HYW)), full((1, HYW))],
        out_specs=pl.BlockSpec((2, tm, HYW), lambda i: (0, i, 0)),
        out_shape=jax.ShapeDtypeStruct((2, L, HYW), F32),
        compiler_params=_cparams(1, VMEM_LIMIT),
        name="hyena_filter",
    )(zin, w1, b1, f1, w2, b2, f2, w3a, w3b, deltas)


def _strided_dft_kernel(x_hbm, f_ref, o_hbm, xbuf, obuf, sem_in, sem_out):
    g = pl.program_id(0)
    ng = pl.num_programs(0)
    nb = FFT_P // FFT_BB

    def copies(grp, slot, inbound):
        n = grp // nb
        b0 = (grp % nb) * FFT_BB
        if inbound:
            return [pltpu.make_async_copy(x_hbm.at[n, :, b0 + jj, :], xbuf.at[slot, jj], sem_in.at[slot])
                    for jj in range(FFT_BB)]
        return [pltpu.make_async_copy(obuf.at[slot, jj], o_hbm.at[n, :, b0 + jj, :], sem_out.at[slot])
                for jj in range(FFT_BB)]

    def start(grp, slot, inbound):
        for cp in copies(grp, slot, inbound):
            cp.start()

    def wait(grp, slot, inbound):
        for cp in copies(grp, slot, inbound):
            cp.wait()

    @pl.when(g == 0)
    def _():
        start(0, 0, True)

    for slot in range(2):
        grp = 2 * g + slot
        if slot == 0:
            start(grp + 1, 1, True)
        else:
            @pl.when(g + 1 < ng)
            def _():
                start(grp + 1, 0, True)
        wait(grp, slot, True)

        @pl.when(g > 0)
        def _():
            wait(grp - 2, slot, False)

        for jj in range(FFT_BB):
            obuf[slot, jj] = jnp.dot(f_ref[...], xbuf[slot, jj].astype(BF16), preferred_element_type=F32)
        start(grp, slot, False)

    @pl.when(g + 1 == ng)
    def _():
        wait(2 * g, 0, False)
        wait(2 * g + 1, 1, False)


def _strided_dft(xv, fmat, name):
    n, kk = xv.shape[0], xv.shape[1]
    mm = fmat.shape[0]
    groups = n * (FFT_P // FFT_BB)
    return pl.pallas_call(
        _strided_dft_kernel,
        grid=(groups // 2,),
        in_specs=[pl.BlockSpec(memory_space=pl.ANY),
                  pl.BlockSpec((mm, kk), lambda g: (0, 0))],
        out_specs=pl.BlockSpec(memory_space=pl.ANY),
        out_shape=jax.ShapeDtypeStruct((n, mm, FFT_P, HYW), F32),
        scratch_shapes=[pltpu.VMEM((2, FFT_BB, kk, HYW), F32), pltpu.VMEM((2, FFT_BB, mm, HYW), F32),
                        pltpu.SemaphoreType.DMA((2,)), pltpu.SemaphoreType.DMA((2,))],
        compiler_params=_cparams(1, VMEM_LIMIT),
        name=name,
    )(xv, fmat)


def _cblock(mr, mi):
    return jnp.concatenate([jnp.concatenate([mr, -mi], axis=1), jnp.concatenate([mi, mr], axis=1)], axis=0)


FFT_H = FFT_P // 2


def _twiddled(fr_ref, fi_ref, tw):
    twr = tw[0:1, :]
    twi = tw[1:2, :]
    fr = fr_ref[...]
    fi = fi_ref[...]
    return fr * twr - fi * twi, fr * twi + fi * twr


def _cmul_rows(x, kr, ki):
    xr, xi = x[:FFT_P], x[FFT_P:]
    return jnp.concatenate([xr * kr - xi * ki, xr * ki + xi * kr], axis=0).astype(BF16)


_TN_DIMS = (((0,), (0,)), ((), ()))
_STAGE2_SPECS = [pl.BlockSpec((FFT_P, FFT_P), lambda d: (0, 0)),
                 pl.BlockSpec((FFT_P, FFT_P), lambda d: (0, 0)),
                 pl.BlockSpec((None, 2, FFT_P), lambda d: (d, 0, 0)),
                 pl.BlockSpec((2, FFT_P), lambda d: (0, 0))]


def _kspec_kernel(a_ref, fr_ref, fi_ref, tw_ref, twh_ref, o_ref, oh_ref):
    dd = pl.program_id(0)
    dot = functools.partial(jnp.dot, preferred_element_type=F32)

    def combine(xf, xb, out_ref):
        out_ref[0] = xf[:FFT_P] + xb[:FFT_P]
        out_ref[1] = xf[FFT_P:] - xb[FFT_P:]

    @pl.when(dd > 0)
    def _():
        rm = _cblock(*_twiddled(fr_ref, fi_ref, tw_ref[...])).astype(BF16)
        combine(dot(rm, a_ref[0].reshape(2 * FFT_P, HYW).astype(BF16)),
                dot(rm, a_ref[1].reshape(2 * FFT_P, HYW).astype(BF16)), o_ref)

    @pl.when(dd == 0)
    def _():
        for slot, tw, out_ref in ((0, tw_ref[...], o_ref), (1, twh_ref[...], oh_ref)):
            w = jnp.concatenate(_twiddled(fr_ref, fi_ref, tw), axis=0).astype(BF16)
            combine(dot(w, a_ref[0, slot].astype(BF16)), dot(w, a_ref[1, slot].astype(BF16)), out_ref)


def _kernel_spectrum(ak, fr, fi, tw, twh):
    return pl.pallas_call(
        _kspec_kernel,
        grid=(FFT_H,),
        in_specs=[pl.BlockSpec((2, None, 2, FFT_P, HYW), lambda d: (0, d, 0, 0, 0))] + _STAGE2_SPECS,
        out_specs=(pl.BlockSpec((2, None, FFT_P, HYW), lambda d: (0, d, 0, 0)),
                   pl.BlockSpec((2, FFT_P, HYW), lambda d: (0, 0, 0))),
        out_shape=(jax.ShapeDtypeStruct((2, FFT_H, FFT_P, HYW), F32),
                   jax.ShapeDtypeStruct((2, FFT_P, HYW), F32)),
        compiler_params=_cparams(1, VMEM_LIMIT),
        name="kernel_spectrum",
    )(ak, fr, fi, tw, twh)


def _mid_kernel(a_ref, kh_ref, khh_ref, fr_ref, fi_ref, tw_ref, twh_ref, o_ref):
    dd = pl.program_id(0)
    dot = functools.partial(jnp.dot, preferred_element_type=F32)
    dot_t = lambda w, y: lax.dot_general(w, y, _TN_DIMS, preferred_element_type=F32)

    @pl.when(dd > 0)
    def _():
        rm = _cblock(*_twiddled(fr_ref, fi_ref, tw_ref[...])).astype(BF16)
        for n in range(B):
            x = dot(rm, a_ref[n].reshape(2 * FFT_P, HYW).astype(BF16))
            y = _cmul_rows(x, kh_ref[0], kh_ref[1])
            o_ref[n] = dot_t(rm, y).reshape(2, FFT_P, HYW)

    @pl.when(dd == 0)
    def _():
        for slot, tw, k_ref in ((0, tw_ref[...], kh_ref), (1, twh_ref[...], khh_ref)):
            w = jnp.concatenate(_twiddled(fr_ref, fi_ref, tw), axis=0).astype(BF16)
            for n in range(B):
                y = _cmul_rows(dot(w, a_ref[n, slot].astype(BF16)), k_ref[0], k_ref[1])
                o_ref[n, slot] = dot_t(w, y)


def _fft_mid(au, kh, khh, fr, fi, tw, twh):
    pair = pl.BlockSpec((B, None, 2, FFT_P, HYW), lambda d: (0, d, 0, 0, 0))
    return pl.pallas_call(
        _mid_kernel,
        grid=(FFT_H,),
        in_specs=[pair,
                  pl.BlockSpec((2, None, FFT_P, HYW), lambda d: (0, d, 0, 0)),
                  pl.BlockSpec((2, FFT_P, HYW), lambda d: (0, 0, 0))] + _STAGE2_SPECS,
        out_specs=pair,
        out_shape=jax.ShapeDtypeStruct((B, FFT_H, 2, FFT_P, HYW), F32),
        compiler_params=_cparams(1, VMEM_LIMIT),
        name="fft_mid",
    )(au, kh, khh, fr, fi, tw, twh)


def _dft_tables():
    na = L // FFT_P
    a = np.arange(na)
    dd = np.arange(FFT_H)
    ang = 2.0 * np.pi * np.outer(dd, a) / FFT_P
    re_rows = np.cos(ang)
    im_rows = -np.sin(ang)
    im_rows[0] = np.cos(np.pi * a)
    f_first = np.stack([re_rows, im_rows], axis=1).reshape(FFT_P, na)
    gre = 2.0 * np.cos(ang)
    gim = -2.0 * np.sin(ang)
    gre[0] = 1.0
    gim[0] = np.cos(np.pi * a)
    g_last = np.stack([gre, gim], axis=1).reshape(FFT_P, na).T / FFT_N
    b = np.arange(FFT_P)
    angf = 2.0 * np.pi * np.outer(b, b) / FFT_P
    ang2 = 2.0 * np.pi * np.outer(np.arange(FFT_H + 1), b) / FFT_N
    tw = np.stack([np.cos(ang2), -np.sin(ang2)], axis=1)
    f32 = lambda x: jnp.asarray(x.astype(np.float32))
    return (f32(f_first).astype(BF16), f32(g_last).astype(BF16), f32(np.cos(angf)), f32(-np.sin(angf)),
            f32(tw[:FFT_H]), f32(tw[FFT_H]))


def _merge_kernel(ya_ref, cv_ref, vx_ref, x0_ref, ga_ref, gb_ref, x_ref, er_ref, ec_ref, gt1_ref, hyd_ref,
                  wpa_ref, wpb_ref, wo_ref, g2_ref, sh2_ref, sc2_ref, wr_ref, br_ref,
                  x1_ref, t2_ref, lg_ref):
    vx = vx_ref[...]
    yb = x0_ref[...] * (cv_ref[...] + vx * hyd_ref[...])
    pa = jnp.dot(ya_ref[...].astype(BF16), wpa_ref[...], preferred_element_type=F32)
    pb = jnp.dot(yb.astype(BF16), wpb_ref[...], preferred_element_type=F32)
    mixed = (jax.nn.sigmoid(ga_ref[...].astype(F32)) * pa
             + jax.nn.sigmoid(gb_ref[...].astype(F32)) * pb)
    xm = jnp.dot(mixed.astype(BF16), wo_ref[...], preferred_element_type=F32)
    nrow = TM_MG // GRID_W
    row0 = (pl.program_id(0) % (L // TM_MG)) * nrow
    x1 = x_ref[...] + _pos_tile(er_ref, ec_ref, row0, nrow) + gt1_ref[...] * xm
    x1_ref[...] = x1
    t2 = _rms(x1) * g2_ref[...]
    t2 = t2 * (1.0 + sc2_ref[...]) + sh2_ref[...]
    t2_ref[...] = t2
    t_hi = t2.astype(BF16)
    t_lo = (t2 - t_hi.astype(F32)).astype(BF16)
    rr = (jnp.dot(t_hi, wr_ref[...], preferred_element_type=F32)
          + jnp.dot(t_lo, wr_ref[...], preferred_element_type=F32))
    lg_ref[...] = rr[:, :V7X_LANES] + rr[:, V7X_LANES:] + br_ref[...]


def _merge(ya, cv, vx, x0c, p, x2, er, ec, gt1, hyd, wpa, wpb, wo, g2, sh2, sc2, wr, br):
    tpb = L // TM_MG
    half = lambda: pl.BlockSpec((TM_MG, HYW), lambda i: (i, 0))
    full = lambda shape: pl.BlockSpec(shape, lambda i: tuple(0 for _ in shape))
    perb = lambda: pl.BlockSpec((None, 1, D), lambda i: (i // tpb, 0, 0))
    return pl.pallas_call(
        _merge_kernel,
        grid=(T // TM_MG,),
        in_specs=[half(), half(), half(), half(),
                  pl.BlockSpec((TM_MG, D), lambda i: (i, 3)),
                  pl.BlockSpec((TM_MG, D), lambda i: (i, 4)),
                  pl.BlockSpec((TM_MG, D), lambda i: (i, 0)),
                  full((L // GRID_W, D // 2)), full((GRID_W, D // 2)),
                  perb(), full((1, HYW)),
                  full((KD, D)), full((HYW, D)), full((D, D)),
                  full((1, D)), perb(), perb(),
                  full((D, 2 * V7X_LANES)), full((1, V7X_LANES))],
        out_specs=(pl.BlockSpec((TM_MG, D), lambda i: (i, 0)),
                   pl.BlockSpec((TM_MG, D), lambda i: (i, 0)),
                   pl.BlockSpec((TM_MG, V7X_LANES), lambda i: (i, 0))),
        out_shape=(jax.ShapeDtypeStruct((T, D), F32), jax.ShapeDtypeStruct((T, D), F32),
                   jax.ShapeDtypeStruct((T, V7X_LANES), F32)),
        compiler_params=_cparams(1, VMEM_LIMIT),
        name="merge",
    )(ya, cv, vx, x0c, p, p, x2, er, ec, gt1, hyd, wpa, wpb, wo, g2, sh2, sc2, wr, br)


def _route_kernel(lg_ref, info_ref, cnt_ref):
    @pl.when(pl.program_id(0) == 0)
    def _():
        cnt_ref[...] = jnp.zeros_like(cnt_ref)

    lg = lg_ref[...]
    lane = lax.broadcasted_iota(I32, lg.shape, 1)
    lanef = lane.astype(F32)
    neg = -1e30
    big = 1e9
    is_g = (lane >= NEXP) & (lane < NEXP + NGRP)
    gl = jnp.where(is_g, lg, neg)
    ge = jnp.where(is_g, jnp.exp(gl - jnp.max(gl, axis=-1, keepdims=True)), 0.0)
    pg = ge / jnp.sum(ge, axis=-1, keepdims=True)
    p_top_g = jnp.max(pg, axis=-1, keepdims=True)
    gidx = jnp.min(jnp.where(is_g & (pg == p_top_g), lanef, big), axis=-1, keepdims=True)
    g_sel = gidx.astype(I32) - NEXP
    emask = (lane < NEXP) & (jnp.right_shift(lane, NEPG.bit_length() - 1) == g_sel)
    el = jnp.where(emask, lg, neg)
    ee = jnp.where(emask, jnp.exp(el - jnp.max(el, axis=-1, keepdims=True)), 0.0)
    pe = ee / jnp.sum(ee, axis=-1, keepdims=True)
    p1 = jnp.max(jnp.where(emask, pe, -1.0), axis=-1, keepdims=True)
    i1 = jnp.min(jnp.where(emask & (pe == p1), lanef, big), axis=-1, keepdims=True)
    rest = emask & (lanef != i1)
    p2 = jnp.max(jnp.where(rest, pe, -1.0), axis=-1, keepdims=True)
    i2 = jnp.min(jnp.where(rest & (pe == p2), lanef, big), axis=-1, keepdims=True)
    wsum = p1 + p2
    w1 = p_top_g * p1 / wsum
    w2 = p_top_g * p2 / wsum
    sel1 = lanef == i1
    sel2 = lanef == i2
    oh = jnp.where(sel1 | sel2, 1.0, 0.0)
    r = lax.broadcasted_iota(I32, (TR, TR), 0)
    c = lax.broadcasted_iota(I32, (TR, TR), 1)
    stril = jnp.where(c < r, 1.0, 0.0).astype(BF16)
    before = jnp.dot(stril, oh.astype(BF16), preferred_element_type=F32) + cnt_ref[...]
    r1 = jnp.sum(jnp.where(sel1, before, 0.0), axis=-1, keepdims=True)
    r2 = jnp.sum(jnp.where(sel2, before, 0.0), axis=-1, keepdims=True)
    cnt_ref[...] += jnp.sum(oh, axis=0, keepdims=True)
    info = jnp.where(lane == 0, i1, jnp.where(lane == 1, r1, jnp.where(lane == 2, i2, jnp.where(
        lane == 3, r2, jnp.where(lane == 4, w1, jnp.where(lane == 5, w2, 0.0))))))
    info_ref[...] = info


def _route(lg):
    return pl.pallas_call(
        _route_kernel,
        grid=(T // TR,),
        in_specs=[pl.BlockSpec((TR, V7X_LANES), lambda i: (i, 0))],
        out_specs=(pl.BlockSpec((TR, V7X_LANES), lambda i: (i, 0)),
                   pl.BlockSpec((1, V7X_LANES), lambda i: (0, 0))),
        out_shape=(jax.ShapeDtypeStruct((T, V7X_LANES), F32), jax.ShapeDtypeStruct((1, V7X_LANES), F32)),
        compiler_params=_cparams(1, VMEM_LIMIT),
        name="route",
    )(lg)


def _positions_kernel(info_ref, st_ref, o_ref):
    info = info_ref[...]
    lane = lax.broadcasted_iota(I32, info.shape, 1)
    lanef = lane.astype(F32)
    st = st_ref[...]
    row = lambda e_lane, r_lane: (jnp.sum(jnp.where(lanef == _lane_pick(info, lane, e_lane), st, 0.0),
                                          axis=-1, keepdims=True) + _lane_pick(info, lane, r_lane))
    o_ref[...] = jnp.where(lane == 0, row(0, 1), jnp.where(lane == 1, row(2, 3), 0.0)).astype(I32)


def _positions(info, starts_row):
    return pl.pallas_call(
        _positions_kernel,
        grid=(T // TPOS,),
        in_specs=[pl.BlockSpec((TPOS, V7X_LANES), lambda i: (i, 0)),
                  pl.BlockSpec((1, V7X_LANES), lambda i: (0, 0))],
        out_specs=pl.BlockSpec((TPOS, V7X_LANES), lambda i: (i, 0)),
        out_shape=jax.ShapeDtypeStruct((T, V7X_LANES), I32),
        compiler_params=_cparams(1, VMEM_LIMIT),
        name="positions",
    )(info, starts_row)


def _scatter_kernel(pos_ref, zt_ref, t2_ref, zeros_hbm, xs_hbm, sem, zsem):
    i = pl.program_id(0)

    def zcopy(row):
        start = pl.multiple_of(jnp.maximum(row, 0), TE)
        return pltpu.make_async_copy(zeros_hbm, xs_hbm.at[pl.ds(start, TE)], zsem)

    @pl.when(i == 0)
    def _():
        def ztail(start, e, carry):
            @pl.when(zt_ref[0, e] >= 0)
            def _():
                cp = zcopy(zt_ref[0, e])
                cp.start() if start else cp.wait()
            return carry

        lax.fori_loop(0, NEXP, functools.partial(ztail, True), 0)
        lax.fori_loop(0, NEXP, functools.partial(ztail, False), 0)

        def zrest(start, tile, carry):
            cp = zcopy(tile * TE)
            cp.start() if start else cp.wait()
            return carry

        lax.fori_loop(zt_ref[0, NA_OFF], NT_EXP, functools.partial(zrest, True), 0)
        lax.fori_loop(zt_ref[0, NA_OFF], NT_EXP, functools.partial(zrest, False), 0)

    def row_copy(base, jj, kk):
        dst = pos_ref[0, 2 * (base + jj) + kk]
        return pltpu.make_async_copy(t2_ref.at[pl.ds(base, ROW_GROUP)].at[pl.ds(jj, 1)],
                                     xs_hbm.at[pl.ds(dst, 1)], sem)

    _start_rows(TS, row_copy)
    _wait_rows(TS, pltpu.make_async_copy(t2_ref.at[pl.ds(0, 1)], xs_hbm.at[pl.ds(0, 1)], sem))


def _start_rows(n_rows, row_copy):
    def group(g, carry):
        base = pl.multiple_of(g * ROW_GROUP, ROW_GROUP)
        for jj in range(ROW_GROUP):
            for kk in range(2):
                row_copy(base, jj, kk).start(priority=kk)
        return carry

    lax.fori_loop(0, n_rows // ROW_GROUP, group, 0)


def _wait_rows(n_rows, one_row_copy):
    def drain(j, carry):
        one_row_copy.wait()
        one_row_copy.wait()
        return carry

    lax.fori_loop(0, n_rows, drain, 0, unroll=ROW_UNROLL)


def _scatter_rows(pos3, meta, t2, zeros_tile):
    return pl.pallas_call(
        _scatter_kernel,
        grid=(T // TS,),
        in_specs=[pl.BlockSpec((None, 1, 2 * TS), lambda i: (i, 0, 0), memory_space=pltpu.SMEM),
                  pl.BlockSpec(memory_space=pltpu.SMEM),
                  pl.BlockSpec((TS, D), lambda i: (i, 0)),
                  pl.BlockSpec(memory_space=pl.ANY)],
        out_specs=pl.BlockSpec(memory_space=pl.ANY),
        out_shape=jax.ShapeDtypeStruct((NP_ROWS, D), F32),
        scratch_shapes=[pltpu.SemaphoreType.DMA(()), pltpu.SemaphoreType.DMA(())],
        compiler_params=_cparams(1, VMEM_LIMIT),
        name="scatter_rows",
    )(pos3, meta, t2, zeros_tile)


def _expert_kernel(te_ref, nx_ref, sl_ref, na_ref, xs_ref, wg_hbm, wu_hbm, wd_hbm, ys_ref,
                   wgs_ref, wus_ref, wds_ref, wgb_ref, wub_ref, wdb_ref, sem):
    i = pl.program_id(0)
    active = i < na_ref[0]
    first = active & ((i == 0) | (te_ref[i] != te_ref[jnp.maximum(i - 1, 0)]))

    def fetch(e, slot):
        return [pltpu.make_async_copy(wg_hbm.at[e], wgs_ref.at[slot], sem.at[slot]),
                pltpu.make_async_copy(wu_hbm.at[e], wus_ref.at[slot], sem.at[slot]),
                pltpu.make_async_copy(wd_hbm.at[e], wds_ref.at[slot], sem.at[slot])]

    for slot in range(2):
        @pl.when(first & (sl_ref[i] == slot))
        def _():
            @pl.when(i == 0)
            def _():
                for cp in fetch(te_ref[i], slot):
                    cp.start()

            for cp in fetch(te_ref[i], slot):
                cp.wait()
            wgb_ref[...] = wgs_ref[slot].astype(BF16)
            wub_ref[...] = wus_ref[slot].astype(BF16)
            wdb_ref[...] = wds_ref[slot].astype(BF16)

            @pl.when(nx_ref[i] >= 0)
            def _():
                for cp in fetch(nx_ref[i], 1 - slot):
                    cp.start()

    @pl.when(active)
    def _():
        x = xs_ref[...].astype(BF16)
        g = jnp.dot(x, wgb_ref[...], preferred_element_type=F32)
        u = jnp.dot(x, wub_ref[...], preferred_element_type=F32)
        hid = (g * jax.nn.sigmoid(g) * u).astype(BF16)
        ys_ref[...] = jnp.dot(hid, wdb_ref[...], preferred_element_type=F32)

    @pl.when(jnp.logical_not(active))
    def _():
        ys_ref[...] = jnp.zeros_like(ys_ref)


def _experts(tile_expert, tile_next, tile_slot, n_active, xs, wg, wu, wd):
    rows = lambda i, *_: (i, 0)
    rows_in = lambda i, te, nx, sl, na: (jnp.minimum(i, na[0] - 1), 0)
    any_space = pl.BlockSpec(memory_space=pl.ANY)
    grid_spec = pltpu.PrefetchScalarGridSpec(
        num_scalar_prefetch=4,
        grid=(NT_EXP,),
        in_specs=[pl.BlockSpec((TE, D), rows_in), any_space, any_space, any_space],
        out_specs=pl.BlockSpec((TE, D), rows),
        scratch_shapes=[pltpu.VMEM((2, D, DEXP), F32), pltpu.VMEM((2, D, DEXP), F32), pltpu.VMEM((2, DEXP, D), F32),
                        pltpu.VMEM((D, DEXP), BF16), pltpu.VMEM((D, DEXP), BF16), pltpu.VMEM((DEXP, D), BF16),
                        pltpu.SemaphoreType.DMA((2,))],
    )
    return pl.pallas_call(
        _expert_kernel,
        grid_spec=grid_spec,
        out_shape=jax.ShapeDtypeStruct((NP_ROWS, D), F32),
        compiler_params=_cparams(1, VMEM_LIMIT),
        name="experts",
    )(tile_expert, tile_next, tile_slot, n_active, xs, wg, wu, wd)


def _combine_kernel(pos_ref, posn_ref, info_ref, x1_ref, gt2_ref, fg_ref, ys_hbm, o_ref, buf, sem):
    g = pl.program_id(0)
    ng = pl.num_programs(0)

    def gather(p_ref, half, slot):
        def row_copy(base, jj, kk):
            src = p_ref[0, 2 * (half * TC + base + jj) + kk]
            return pltpu.make_async_copy(ys_hbm.at[pl.ds(src, 1)],
                                         buf.at[slot, kk].at[pl.ds(base, ROW_GROUP)].at[pl.ds(jj, 1)],
                                         sem.at[slot])
        _start_rows(TC, row_copy)

    def finish(half, slot):
        _wait_rows(TC, pltpu.make_async_copy(ys_hbm.at[pl.ds(0, 1)], buf.at[slot, 0].at[pl.ds(0, 1)],
                                             sem.at[slot]))
        rows = slice(half * TC, (half + 1) * TC)
        info = info_ref[rows, :]
        lane = lax.broadcasted_iota(I32, info.shape, 1)
        moe = _lane_pick(info, lane, 4) * buf[slot, 0] + _lane_pick(info, lane, 5) * buf[slot, 1]
        x2 = x1_ref[rows, :] + gt2_ref[...] * moe
        o_ref[rows, :] = _rms(x2) * fg_ref[...]

    @pl.when(g == 0)
    def _():
        gather(pos_ref, 0, 0)

    gather(pos_ref, 1, 1)
    finish(0, 0)

    @pl.when(g + 1 < ng)
    def _():
        gather(posn_ref, 0, 0)

    finish(1, 1)


def _combine(pos3, info, x1, gt2, fg, ys):
    step = 2 * TC
    tpb = L // step
    nsteps = T // step
    return pl.pallas_call(
        _combine_kernel,
        grid=(nsteps,),
        in_specs=[pl.BlockSpec((None, 1, 2 * step), lambda i: (i, 0, 0), memory_space=pltpu.SMEM),
                  pl.BlockSpec((None, 1, 2 * step), lambda i: (jnp.minimum(i + 1, nsteps - 1), 0, 0),
                               memory_space=pltpu.SMEM),
                  pl.BlockSpec((step, V7X_LANES), lambda i: (i, 0)),
                  pl.BlockSpec((step, D), lambda i: (i, 0)),
                  pl.BlockSpec((None, 1, D), lambda i: (i // tpb, 0, 0)),
                  pl.BlockSpec((1, D), lambda i: (0, 0)),
                  pl.BlockSpec(memory_space=pl.ANY)],
        out_specs=pl.BlockSpec((step, D), lambda i: (i, 0)),
        out_shape=jax.ShapeDtypeStruct((T, D), F32),
        scratch_shapes=[pltpu.VMEM((2, 2, TC, D), F32), pltpu.SemaphoreType.DMA((2,))],
        compiler_params=_cparams(1, VMEM_LIMIT),
        name="combine",
    )(pos3, pos3, info, x1, gt2, fg, ys)


def _pos_tables():
    rows = L // GRID_W
    quarter = D // 4
    omega = 1.0 / (10000.0 ** (jnp.arange(quarter, dtype=F32) / quarter))

    def axis_emb(pos):
        a = pos[:, None] * omega[None, :]
        return jnp.concatenate([jnp.sin(a), jnp.cos(a)], axis=-1)

    er = axis_emb(jnp.arange(rows, dtype=F32))
    ec = axis_emb(jnp.arange(GRID_W, dtype=F32))
    return er, ec


def _filter_features():
    z = np.zeros((L, V7X_LANES), np.float64)
    bands = (HY_EMB - 1) // 2
    ang = (2.0 * np.pi * np.arange(L) / L)[:, None] * np.linspace(1e-4, bands - 1, bands)[None, :]
    z[:, 0] = np.linspace(0.0, 1.0, L)
    z[:, 1:1 + bands] = np.cos(ang)
    z[:, 1 + bands:HY_EMB] = -np.sin(ang)
    z[1:, HY_EMB] = 1.0
    return jnp.asarray(z.astype(np.float32))


def _pad2(a, rows, cols):
    return jnp.pad(a, ((0, rows - a.shape[0]), (0, cols - a.shape[1])))


def kernel(x, c, ctx, c_ctx, ada_w, ada_b, norm1_g, norm2_g, w_in, hgrn_lb, hgrn_norm_g, hy_conv_w, hy_conv_b, hy_filt_w1, hy_filt_b1, hy_filt_freq1, hy_filt_w2, hy_filt_b2, hy_filt_freq2, hy_filt_w3, hy_d, w_proj_a, w_proj_b, w_out, moe_router_g_w, moe_router_g_b, moe_router_e_w, moe_router_e_b, moe_w_gate, moe_w_up, moe_w_down, final_norm_g):
    cvec = jnp.zeros((8, D), F32).at[0:B].set(c).at[B].set(c_ctx)
    mod = _adaln(cvec, ada_w[0], ada_b[0][None, :])
    m6 = mod.reshape(8, 6, D)
    sh1, sc1, gt1, sh2, sc2, gt2 = [m6[0:B, k][:, None, :] for k in range(6)]
    csh1, csc1 = m6[B:B + 1, 0], m6[B:B + 1, 1]

    lbs = jnp.cumsum(jax.nn.softmax(hgrn_lb.astype(F32), axis=0), axis=0)[0]
    g1 = norm1_g[0][None, :]
    er, ec = _pos_tables()
    x2 = x.reshape(T, D)

    w_ctx = w_in[0][:, KD:4 * KD].astype(BF16)
    s_f, s_b = _context_states(ctx, g1, csh1, csc1, w_ctx, lbs)

    pz, p = _in_projection(x2, er, ec, g1, sh1, sc1, w_in[0].astype(BF16))

    o_f = _hgrn_scan(pz, p, lbs[0:1], s_f, False)
    y_a = _hgrn_scan(pz, p, lbs[1:2], s_b, True, o_f=o_f, norm_g=hgrn_norm_g[0][None, :])

    vx, x0c = _hyena_pre(p, hy_conv_w[0], hy_conv_b[0][None, :])
    deltas = jnp.abs(jnp.linspace(math.log(HY_DECAY_TARGET) / HY_SLOW_PCT,
                                  math.log(HY_DECAY_TARGET) / HY_FAST_PCT, HYW, dtype=F32))[None, :]
    ln = V7X_LANES
    fh = hy_filt_w2.shape[-1]
    blockdiag = lambda m: jnp.concatenate([_pad2(m, m.shape[0], 2 * m.shape[1]),
                                           jnp.pad(m, ((0, 0), (m.shape[1], 0)))], axis=0)
    twice = lambda v: jnp.concatenate([v, v])[None, :]
    w3 = hy_filt_w3[0]
    taps = _filter_taps(
        _filter_features(),
        blockdiag(_pad2(hy_filt_w1[0], ln, fh)), twice(hy_filt_b1[0]), twice(hy_filt_freq1[0]),
        blockdiag(hy_filt_w2[0]), twice(hy_filt_b2[0]), twice(hy_filt_freq2[0]),
        _pad2(w3, ln, 2 * HYW), jnp.pad(w3, ((fh, 0), (0, 0))), deltas)
    f_first, g_last, fr, fi, tw, twh = _dft_tables()
    na = L // FFT_P
    ak = _strided_dft(taps.reshape(2, na, FFT_P, HYW), f_first, "dft_first_taps")
    kh, khh = _kernel_spectrum(ak.reshape(2, FFT_H, 2, FFT_P, HYW), fr, fi, tw, twh)
    au = _strided_dft(vx.reshape(B, na, FFT_P, HYW), f_first, "dft_first")
    bp = _fft_mid(au.reshape(B, FFT_H, 2, FFT_P, HYW), kh, khh, fr, fi, tw, twh)
    conv = _strided_dft(bp.reshape(B, FFT_P, FFT_P, HYW), g_last, "dft_last").reshape(T, HYW)

    wr = jnp.concatenate([jnp.transpose(moe_router_e_w[0], (1, 0, 2)).reshape(D, NEXP),
                          moe_router_g_w[0], jnp.zeros((D, V7X_LANES - NEXP - NGRP), F32)], axis=1)
    wr_hi = wr.astype(BF16)
    wr = jnp.concatenate([wr_hi, (wr - wr_hi.astype(F32)).astype(BF16)], axis=1)
    br = jnp.concatenate([moe_router_e_b[0].reshape(NEXP), moe_router_g_b[0],
                          jnp.zeros((V7X_LANES - NEXP - NGRP,), F32)])[None, :]
    x1, t2, lg = _merge(y_a.reshape(T, KD), conv, vx, x0c, p, x2, er, ec, gt1, hy_d[0][None, :],
                        w_proj_a[0].astype(BF16), w_proj_b[0].astype(BF16), w_out[0].astype(BF16),
                        norm2_g[0][None, :], sh2, sc2, wr, br)

    info, counts = _route(lg)
    cnt = counts[0, :NEXP].astype(I32)
    pc = ((cnt + TE - 1) // TE) * TE
    ends = jnp.cumsum(pc)
    starts = ends - pc
    n_active = (ends[-1] // TE).astype(I32)[None]
    tile_rows = jnp.arange(NT_EXP, dtype=I32) * TE
    tile_expert = jnp.minimum(jnp.sum((ends[None, :] <= tile_rows[:, None]).astype(I32), axis=1), NEXP - 1)
    meta = jnp.concatenate([jnp.where(pc > 0, ends - TE, -1), starts, n_active]).astype(I32)[None, :]
    starts_row = jnp.pad(starts.astype(F32), (0, V7X_LANES - NEXP))[None, :]
    pos3 = _positions(info, starts_row)[:, :2].reshape(T // TS, 1, 2 * TS)

    xs = _scatter_rows(pos3, meta, t2, jnp.zeros((TE, D), F32))
    eid = jnp.arange(NEXP, dtype=I32)
    nonempty = pc > 0
    later = jnp.where(nonempty[None, :] & (eid[None, :] > eid[:, None]), eid[None, :], NEXP)
    next_e = jnp.min(later, axis=1)
    next_e = jnp.where(next_e < NEXP, next_e, -1).astype(I32)
    slot_e = ((jnp.cumsum(nonempty.astype(I32)) - 1) % 2).astype(I32)
    ys = _experts(tile_expert, next_e[tile_expert], slot_e[tile_expert], n_active, xs,
                  moe_w_gate[0].reshape(NEXP, D, DEXP), moe_w_up[0].reshape(NEXP, D, DEXP),
                  moe_w_down[0].reshape(NEXP, DEXP, D))
    out = _combine(pos3, info, x1, gt2, final_norm_g[None, :], ys)
    return out.reshape(B, L, D)
```

```python
import functools
import math

import numpy as np
import jax
import jax.numpy as jnp
from jax import lax
from jax.experimental import pallas as pl
from jax.experimental.pallas import tpu as pltpu

F32 = jnp.float32
BF16 = jnp.bfloat16
I32 = jnp.int32
HIGHEST = lax.Precision.HIGHEST

D = 1024
B = 2
L = 8192
T = B * L
CTX = 256
GRID_W = 64
EPS = 1e-6
H = 4
DK = 128
DV = 128
KD = H * DK
IN_W = 6144
HYW = 512
HY_EMB = 33
NGRP = 4
NEPG = 8
NEXP = NGRP * NEPG
DEXP = 512
HY_DECAY_TARGET = 1e-2
HY_FAST_PCT = 0.3
HY_SLOW_PCT = 1.5

V7X_LANES = 128
V7X_SUBLANES = 8
V7X_VMEM_BYTES = 64 * 1024 * 1024
VMEM_LIMIT = (3 * V7X_VMEM_BYTES) // 4

FFT_N = 2 * L
FFT_P = 128
FFT_BB = 8

TM_IN = 1024
TN_IN = 1024
TH = 128
CB = 32
TM_HY = 1024
HALO = 2 * V7X_SUBLANES
TM_MG = 512
TR = 512
TE = 512
TPOS = 2048
NP_ROWS = 2 * T + NEXP * TE
NT_EXP = NP_ROWS // TE
TS = 512
TC = TS // 2
ROW_UNROLL = 8
ROW_GROUP = 32
ST_OFF = NEXP
NA_OFF = 2 * NEXP


def _cparams(n_axes, vmem=None):
    return pltpu.CompilerParams(dimension_semantics=("arbitrary",) * n_axes,
                                vmem_limit_bytes=vmem)


def _split3(x):
    hi = x.astype(BF16)
    r = x - hi.astype(F32)
    mid = r.astype(BF16)
    lo = (r - mid.astype(F32)).astype(BF16)
    return hi, mid, lo


def _dot01(m, x):
    hi, mid, lo = _split3(x)
    return (jnp.dot(m, hi, preferred_element_type=F32) + jnp.dot(m, mid, preferred_element_type=F32)
            + jnp.dot(m, lo, preferred_element_type=F32))


def _rms(x):
    return x * lax.rsqrt(jnp.mean(x * x, axis=-1, keepdims=True) + EPS)


def _lane_pick(x, lane, idx):
    return jnp.sum(jnp.where(lane == idx, x, 0.0), axis=-1, keepdims=True)


def _ada_kernel(c_ref, w_ref, b_ref, o_ref):
    c = c_ref[...]
    s = c * jax.nn.sigmoid(c)
    o_ref[...] = jnp.dot(s, w_ref[...], preferred_element_type=F32, precision=HIGHEST) + b_ref[...]


def _adaln(cvec, w, b):
    tn = 1536
    return pl.pallas_call(
        _ada_kernel,
        grid=(6 * D // tn,),
        in_specs=[pl.BlockSpec((8, D), lambda j: (0, 0)),
                  pl.BlockSpec((D, tn), lambda j: (0, j)),
                  pl.BlockSpec((1, tn), lambda j: (0, j))],
        out_specs=pl.BlockSpec((8, tn), lambda j: (0, j)),
        out_shape=jax.ShapeDtypeStruct((8, 6 * D), F32),
        compiler_params=_cparams(1, VMEM_LIMIT),
        name="adaln",
    )(cvec, w, b)


def _keys(z, lb):
    sig = jax.nn.sigmoid(z)
    logf = jnp.log(lb + (1.0 - lb) * sig)
    k = (1.0 - lb) * jax.nn.sigmoid(-z)
    return k, logf


def _ctx_kernel(ctx_ref, g_ref, sh_ref, sc_ref, w_ref, lb_ref, sf_ref, sb_ref):
    h = _rms(ctx_ref[...]) * g_ref[...]
    h = h * (1.0 + sc_ref[...]) + sh_ref[...]
    p = jnp.dot(h.astype(BF16), w_ref[...], preferred_element_type=F32)
    zf, zb, v = p[:, :KD], p[:, KD:2 * KD], p[:, 2 * KD:]
    kf, lf = _keys(zf, lb_ref[0:1, :])
    kb, lbk = _keys(zb, lb_ref[1:2, :])
    r = lax.broadcasted_iota(I32, (CTX, CTX), 0)
    c = lax.broadcasted_iota(I32, (CTX, CTX), 1)
    tril = jnp.where(c <= r, 1.0, 0.0).astype(BF16)
    cf = _dot01(tril, lf)
    cb = _dot01(tril, lbk)
    kfd = (kf * jnp.exp(cf[CTX - 1:CTX, :] - cf)).astype(BF16)
    kbd = (kb * jnp.exp(cb - lbk)).astype(BF16)
    vb = v.astype(BF16)
    tn = (((0,), (0,)), ((), ()))
    for hh in range(H):
        hs = slice(hh * DK, (hh + 1) * DK)
        sf_ref[hh] = lax.dot_general(vb[:, hs], kfd[:, hs], tn, preferred_element_type=F32)
        sb_ref[hh] = lax.dot_general(vb[:, hs], kbd[:, hs], tn, preferred_element_type=F32)


def _context_states(ctx, g1, csh1, csc1, w_ctx, lbs):
    st = jax.ShapeDtypeStruct((B, H, DV, DK), F32)
    return pl.pallas_call(
        _ctx_kernel,
        grid=(B,),
        in_specs=[pl.BlockSpec((None, CTX, D), lambda b: (b, 0, 0)),
                  pl.BlockSpec((1, D), lambda b: (0, 0)),
                  pl.BlockSpec((1, D), lambda b: (0, 0)),
                  pl.BlockSpec((1, D), lambda b: (0, 0)),
                  pl.BlockSpec((D, 3 * KD), lambda b: (0, 0)),
                  pl.BlockSpec((2, KD), lambda b: (0, 0))],
        out_specs=(pl.BlockSpec((None, H, DV, DK), lambda b: (b, 0, 0, 0)),
                   pl.BlockSpec((None, H, DV, DK), lambda b: (b, 0, 0, 0))),
        out_shape=(st, st),
        compiler_params=_cparams(1, VMEM_LIMIT),
        name="ctx_states",
    )(ctx, g1, csh1, csc1, w_ctx, lbs)


def _pos_tile(er_ref, ec_ref, row0, nrow):
    lo = jnp.concatenate([jnp.broadcast_to(er_ref[pl.ds(row0 + i, 1), :], (GRID_W, D // 2))
                          for i in range(nrow)], axis=0)
    hi = jnp.concatenate([ec_ref[...]] * nrow, axis=0)
    return jnp.concatenate([lo, hi], axis=1)


def _inproj_kernel(x_ref, er_ref, ec_ref, g_ref, sh_ref, sc_ref, w_ref, oz_ref, o_ref, hx_ref):
    i = pl.program_id(0)
    j = pl.program_id(1)
    ni = pl.num_programs(0)
    nj = pl.num_programs(1)
    slot = i % 2

    def prepare(tile, dst):
        nrow = TM_IN // GRID_W
        row0 = (tile % (L // TM_IN)) * nrow
        h = _rms(x_ref[...] + _pos_tile(er_ref, ec_ref, row0, nrow)) * g_ref[...]
        hx_ref[dst] = (h * (1.0 + sc_ref[...]) + sh_ref[...]).astype(BF16)

    def project():
        return jnp.dot(hx_ref[slot], w_ref[...], preferred_element_type=F32)

    @pl.when((i == 0) & (j == 0))
    def _():
        prepare(0, 0)

    @pl.when(j == 0)
    def _():
        r = project()
        o_ref[:, :KD] = r[:, :KD].astype(BF16)
        oz_ref[:, :KD] = r[:, KD:]

    @pl.when(j == 1)
    def _():
        r = project()
        oz_ref[:, KD:] = r[:, :KD]
        o_ref[:, KD:] = r[:, KD:].astype(BF16)

    @pl.when((j > 1) & (j < nj - 1))
    def _():
        o_ref[...] = project().astype(BF16)

    @pl.when((j == nj - 1) & (i + 1 < ni))
    def _():
        o_ref[...] = project().astype(BF16)
        prepare(i + 1, 1 - slot)

    @pl.when((j == nj - 1) & (i + 1 == ni))
    def _():
        o_ref[...] = project().astype(BF16)


def _in_projection(x2, er, ec, g1, sh1, sc1, w_bf):
    tiles_per_batch = L // TM_IN
    n_i = T // TM_IN
    n_j = IN_W // TN_IN
    ahead = lambda i, j: jnp.minimum(i + jnp.where(j == n_j - 1, 1, 0), n_i - 1)
    return pl.pallas_call(
        _inproj_kernel,
        grid=(n_i, n_j),
        in_specs=[pl.BlockSpec((TM_IN, D), lambda i, j: (ahead(i, j), 0)),
                  pl.BlockSpec((L // GRID_W, D // 2), lambda i, j: (0, 0)),
                  pl.BlockSpec((GRID_W, D // 2), lambda i, j: (0, 0)),
                  pl.BlockSpec((1, D), lambda i, j: (0, 0)),
                  pl.BlockSpec((None, 1, D), lambda i, j: (ahead(i, j) // tiles_per_batch, 0, 0)),
                  pl.BlockSpec((None, 1, D), lambda i, j: (ahead(i, j) // tiles_per_batch, 0, 0)),
                  pl.BlockSpec((D, TN_IN), lambda i, j: (0, j))],
        out_specs=(pl.BlockSpec((TM_IN, TN_IN), lambda i, j: (i, 0)),
                   pl.BlockSpec((TM_IN, TN_IN), lambda i, j: (i, jnp.maximum(j - 1, 0)))),
        out_shape=(jax.ShapeDtypeStruct((T, 2 * KD), F32),
                   jax.ShapeDtypeStruct((T, IN_W - 2 * KD), BF16)),
        scratch_shapes=[pltpu.VMEM((2, TM_IN, D), BF16)],
        compiler_params=_cparams(2, VMEM_LIMIT),
        name="in_proj",
    )(x2, er, ec, g1, sh1, sc1, w_bf)


def _hgrn_kernel(reverse, readout, *refs):
    if readout:
        q_ref, z_ref, v_ref, lb_ref, s0_ref, of_ref, g_ref, ng_ref, o_ref, st_ref = refs
    else:
        q_ref, z_ref, v_ref, lb_ref, s0_ref, o_ref, st_ref = refs

    @pl.when(pl.program_id(0) == 0)
    def _():
        st_ref[...] = s0_ref[...]

    r = lax.broadcasted_iota(I32, (TH, TH), 0)
    c = lax.broadcasted_iota(I32, (TH, TH), 1)
    cb_shift = CB.bit_length() - 1
    same = jnp.right_shift(r, cb_shift) == jnp.right_shift(c, cb_shift)
    tri_mask = same & ((c >= r) if reverse else (c <= r))
    tri = jnp.where(tri_mask, 1.0, 0.0).astype(BF16)
    rblk = jnp.right_shift(r, cb_shift)
    cblk = jnp.right_shift(c, cb_shift)
    dist = (rblk - cblk) if not reverse else (cblk - rblk)
    for b in range(B):
        _hgrn_chunk(reverse, readout, b, tri, tri_mask, dist, refs)


def _hgrn_chunk(reverse, readout, b, tri, tri_mask, dist, refs):
    if readout:
        q_ref, z_ref, v_ref, lb_ref, s0_ref, of_ref, g_ref, ng_ref, o_ref, st_ref = refs
    else:
        q_ref, z_ref, v_ref, lb_ref, s0_ref, o_ref, st_ref = refs
    q = q_ref[b].astype(F32)
    v = v_ref[b]
    k, logf = _keys(z_ref[b], lb_ref[...])
    bl = _dot01(tri, logf)
    nt = (((1,), (1,)), ((), ()))
    tn = (((0,), (0,)), ((), ()))
    nblk = TH // CB
    e_row = 0 if reverse else CB - 1
    m_row = CB - 1 - CB // 2 if reverse else CB // 2
    tau = [bl[jb * CB + e_row:jb * CB + e_row + 1] for jb in range(nblk)]
    mid = [bl[jb * CB + m_row:jb * CB + m_row + 1] for jb in range(nblk)]
    rows = lambda vecs: jnp.concatenate([jnp.broadcast_to(x, (CB, KD)) for x in vecs], axis=0)
    mid_b = rows(mid)
    qd0 = (q * jnp.exp(bl - mid_b)).astype(BF16)
    kd0 = (k * jnp.exp(mid_b - bl)).astype(BF16)
    qs = q * jnp.exp(bl)
    ke = k * jnp.exp(rows(tau) - bl)
    order = list(range(nblk - 1, -1, -1)) if reverse else list(range(nblk))
    pre = [jnp.zeros((1, KD), F32)]
    for i in range(nblk):
        pre.append(pre[-1] + tau[order[i]])
    total = pre[nblk]
    entry = [None] * nblk
    leave = [None] * nblk
    gap = [[None] * nblk for _ in range(nblk)]
    for i, jb in enumerate(order):
        entry[jb] = jnp.exp(pre[i])
        leave[jb] = jnp.exp(total - pre[i + 1])
        for d in range(2, nblk):
            gap[d][jb] = jnp.exp(pre[i + d] - pre[i + 1]) if i + d < nblk else jnp.zeros((1, KD), F32)
    qc = (qs * rows(entry)).astype(BF16)
    kc = (ke * rows(leave)).astype(BF16)
    kx = jnp.concatenate([ke.astype(BF16)] + [(ke * rows(gap[d])).astype(BF16) for d in range(2, nblk)], axis=0)
    qsb = qs.astype(BF16)
    dec = jnp.exp(total)

    def blockdiag(x):
        first_head = lax.broadcasted_iota(I32, x.shape, 1) < DK
        zero = jnp.zeros_like(x)
        return jnp.concatenate([jnp.where(first_head, x, zero), jnp.where(first_head, zero, x)], axis=0)

    for hp in range(H // 2):
        ps = slice(2 * hp * DK, 2 * (hp + 1) * DK)
        sc0 = lax.dot_general(qd0[:, ps], blockdiag(kd0[:, ps]), nt, preferred_element_type=F32)
        scx = lax.dot_general(qsb[:, ps], blockdiag(kx[:, ps]), nt, preferred_element_type=F32)
        halves = []
        for hh in range(2):
            sc = jnp.where(tri_mask, sc0[:, hh * TH:(hh + 1) * TH], 0.0)
            base = hh * (nblk - 1) * TH
            for d in range(1, nblk):
                sc = jnp.where(dist == d, scx[:, base + (d - 1) * TH:base + d * TH], sc)
            halves.append(sc.astype(BF16))
        sc_pair = jnp.concatenate(halves, axis=1)
        st_a = st_ref[b, 2 * hp]
        st_b = st_ref[b, 2 * hp + 1]
        zst = jnp.zeros((DV, DK), BF16)
        st_pair = jnp.concatenate([jnp.concatenate([st_a.astype(BF16), zst], axis=1),
                                   jnp.concatenate([zst, st_b.astype(BF16)], axis=1)], axis=0)
        o_pair = (lax.dot_general(qc[:, ps], st_pair, nt, preferred_element_type=F32)
                  + jnp.dot(sc_pair, blockdiag(v[:, ps]), preferred_element_type=F32))
        upd = lax.dot_general(v[:, ps], kc[:, ps], tn, preferred_element_type=F32)
        st_ref[b, 2 * hp] = st_a * dec[:, ps][:, :DK] + upd[:DV, :DK]
        st_ref[b, 2 * hp + 1] = st_b * dec[:, ps][:, DK:] + upd[DV:, DK:]
        for hh in range(2):
            hs = slice((2 * hp + hh) * DK, (2 * hp + hh + 1) * DK)
            o_h = o_pair[:, hh * DV:(hh + 1) * DV]
            if readout:
                o_h = o_h + of_ref[b, :, hs]
                o_h = _rms(o_h) * ng_ref[...]
                gh = g_ref[b, :, hs].astype(F32)
                o_h = o_h * (gh * jax.nn.sigmoid(gh))
            o_ref[b, :, hs] = o_h


def _hgrn_scan(pz, p, lb_row, s0, reverse, o_f=None, norm_g=None):
    nch = L // TH
    chunk = (lambda c: nch - 1 - c) if reverse else (lambda c: c)
    col_spec = lambda j: pl.BlockSpec((B, TH, KD), lambda c: (0, chunk(c), j))
    in_specs = [col_spec(0), col_spec(1 if reverse else 0), col_spec(1),
                pl.BlockSpec((1, KD), lambda c: (0, 0)),
                pl.BlockSpec((B, H, DV, DK), lambda c: (0, 0, 0, 0))]
    p3 = p.reshape(B, L, p.shape[-1])
    args = [p3, pz.reshape(B, L, 2 * KD), p3, lb_row, s0]
    readout = o_f is not None
    if readout:
        in_specs += [col_spec(0), col_spec(2), pl.BlockSpec((1, DV), lambda c: (0, 0))]
        args += [o_f, p3, norm_g]
    return pl.pallas_call(
        functools.partial(_hgrn_kernel, reverse, readout),
        grid=(nch,),
        in_specs=in_specs,
        out_specs=col_spec(0),
        out_shape=jax.ShapeDtypeStruct((B, L, KD), F32),
        scratch_shapes=[pltpu.VMEM((B, H, DV, DK), F32)],
        compiler_params=_cparams(1, VMEM_LIMIT),
        name="hgrn_bwd_readout" if readout else "hgrn_fwd",
    )(*args)


def _hy_pre_kernel(v_ref, x1_ref, x0_ref, vp_ref, x1p_ref, x0p_ref, vn_ref, x1n_ref, x0n_ref,
                   w_ref, b_ref, vx_ref, x0o_ref):
    i = pl.program_id(1)
    first = i == 0
    last = i == pl.num_programs(1) - 1
    row = lax.broadcasted_iota(I32, (TM_HY, 1), 0)

    def conv(c_ref, p_ref, n_ref, col):
        x = c_ref[...].astype(F32)
        prev_row = jnp.where(first, 0.0, p_ref[...].astype(F32)[HALO - 1:HALO, :])
        next_row = jnp.where(last, 0.0, n_ref[...].astype(F32)[0:1, :])
        xm = jnp.where(row == 0, prev_row, pltpu.roll(x, 1, axis=0))
        xp = jnp.where(row == TM_HY - 1, next_row, pltpu.roll(x, TM_HY - 1, axis=0))
        cs = slice(col * HYW, (col + 1) * HYW)
        return xm * w_ref[0:1, cs] + x * w_ref[1:2, cs] + xp * w_ref[2:3, cs] + b_ref[:, cs]

    v = conv(v_ref, vp_ref, vn_ref, 0)
    x1 = conv(x1_ref, x1p_ref, x1n_ref, 1)
    x0 = conv(x0_ref, x0p_ref, x0n_ref, 2)
    vx_ref[...] = v * x1
    x0o_ref[...] = x0


def _hyena_pre(p, conv_w, conv_b):
    nt = L // TM_HY
    hb = TM_HY // HALO
    nhb = T // HALO
    cur = lambda col: pl.BlockSpec((TM_HY, HYW), lambda b, i: (b * nt + i, col))
    prv = lambda col: pl.BlockSpec((HALO, HYW), lambda b, i: (jnp.maximum((b * nt + i) * hb - 1, 0), col))
    nxt = lambda col: pl.BlockSpec((HALO, HYW), lambda b, i: (jnp.minimum((b * nt + i + 1) * hb, nhb - 1), col))
    c0 = 3
    out = jax.ShapeDtypeStruct((T, HYW), F32)
    return pl.pallas_call(
        _hy_pre_kernel,
        grid=(B, nt),
        in_specs=[cur(c0), cur(c0 + 1), cur(c0 + 2), prv(c0), prv(c0 + 1), prv(c0 + 2),
                  nxt(c0), nxt(c0 + 1), nxt(c0 + 2),
                  pl.BlockSpec((3, 3 * HYW), lambda b, i: (0, 0)),
                  pl.BlockSpec((1, 3 * HYW), lambda b, i: (0, 0))],
        out_specs=(pl.BlockSpec((TM_HY, HYW), lambda b, i: (b * nt + i, 0)),
                   pl.BlockSpec((TM_HY, HYW), lambda b, i: (b * nt + i, 0))),
        out_shape=(out, out),
        compiler_params=_cparams(2, VMEM_LIMIT),
        name="hyena_pre",
    )(p, p, p, p, p, p, p, p, p, conv_w, conv_b)


def _filt_kernel(z_ref, w1_ref, b1_ref, f1_ref, w2_ref, b2_ref, f2_ref, w3a_ref, w3b_ref, dl_ref, o_ref):
    half = z_ref.shape[0] // 2
    zt = z_ref[0:half, :]
    zb = z_ref[half:, :]
    lane = lax.broadcasted_iota(I32, zt.shape, 1)
    dot = functools.partial(jnp.dot, preferred_element_type=F32, precision=HIGHEST)
    h = jnp.sin(f1_ref[...] * (dot(jnp.concatenate([zt, zb], axis=1), w1_ref[...]) + b1_ref[...]))
    h = jnp.sin(f2_ref[...] * (dot(h, w2_ref[...]) + b2_ref[...]))
    for zz, w3_ref, rows in ((zt, w3a_ref, slice(0, half)), (zb, w3b_ref, slice(half, 2 * half))):
        taps = dot(h, w3_ref[...])
        win = jnp.exp(-_lane_pick(zz, lane, 0) * dl_ref[...])
        o_ref[0, rows, :] = taps[:, :HYW] * win
        o_ref[1, rows, :] = taps[:, HYW:] * win * _lane_pick(zz, lane, HY_EMB)


def _filter_taps(zin, w1, b1, f1, w2, b2, f2, w3a, w3b, deltas):
    tm = 2048
    ln = V7X_LANES
    full = lambda shape: pl.BlockSpec(shape, lambda i: (0, 0))
    return pl.pallas_call(
        _filt_kernel,
        grid=(L // tm,),
        in_specs=[pl.BlockSpec((tm, ln), lambda i: (i, 0)),
                  full((2 * ln, ln)), full((1, ln)), full((1, ln)),
                  full((ln, ln)), full((1, ln)), full((1, ln)),
                  full((ln, 2 * HYW)), full((ln, 2 * HYW)), full((1, HYW))],
        out_specs=pl.BlockSpec((2, tm, HYW), lambda i: (0, i, 0)),
        out_shape=jax.ShapeDtypeStruct((2, L, HYW), F32),
        compiler_params=_cparams(1, VMEM_LIMIT),
        name="hyena_filter",
    )(zin, w1, b1, f1, w2, b2, f2, w3a, w3b, deltas)


def _strided_dft_kernel(x_hbm, f_ref, o_hbm, xbuf, obuf, sem_in, sem_out):
    g = pl.program_id(0)
    ng = pl.num_programs(0)
    nb = FFT_P // FFT_BB

    def copies(grp, slot, inbound):
        n = grp // nb
        b0 = (grp % nb) * FFT_BB
        if inbound:
            return [pltpu.make_async_copy(x_hbm.at[n, :, b0 + jj, :], xbuf.at[slot, jj], sem_in.at[slot])
                    for jj in range(FFT_BB)]
        return [pltpu.make_async_copy(obuf.at[slot, jj], o_hbm.at[n, :, b0 + jj, :], sem_out.at[slot])
                for jj in range(FFT_BB)]

    def start(grp, slot, inbound):
        for cp in copies(grp, slot, inbound):
            cp.start()

    def wait(grp, slot, inbound):
        for cp in copies(grp, slot, inbound):
            cp.wait()

    @pl.when(g == 0)
    def _():
        start(0, 0, True)

    for slot in range(2):
        grp = 2 * g + slot
        if slot == 0:
            start(grp + 1, 1, True)
        else:
            @pl.when(g + 1 < ng)
            def _():
                start(grp + 1, 0, True)
        wait(grp, slot, True)

        @pl.when(g > 0)
        def _():
            wait(grp - 2, slot, False)

        for jj in range(FFT_BB):
            obuf[slot, jj] = jnp.dot(f_ref[...], xbuf[slot, jj].astype(BF16), preferred_element_type=F32)
        start(grp, slot, False)

    @pl.when(g + 1 == ng)
    def _():
        wait(2 * g, 0, False)
        wait(2 * g + 1, 1, False)


def _strided_dft(xv, fmat, name):
    n, kk = xv.shape[0], xv.shape[1]
    mm = fmat.shape[0]
    groups = n * (FFT_P // FFT_BB)
    return pl.pallas_call(
        _strided_dft_kernel,
        grid=(groups // 2,),
        in_specs=[pl.BlockSpec(memory_space=pl.ANY),
                  pl.BlockSpec((mm, kk), lambda g: (0, 0))],
        out_specs=pl.BlockSpec(memory_space=pl.ANY),
        out_shape=jax.ShapeDtypeStruct((n, mm, FFT_P, HYW), F32),
        scratch_shapes=[pltpu.VMEM((2, FFT_BB, kk, HYW), F32), pltpu.VMEM((2, FFT_BB, mm, HYW), F32),
                        pltpu.SemaphoreType.DMA((2,)), pltpu.SemaphoreType.DMA((2,))],
        compiler_params=_cparams(1, VMEM_LIMIT),
        name=name,
    )(xv, fmat)


def _cblock(mr, mi):
    return jnp.concatenate([jnp.concatenate([mr, -mi], axis=1), jnp.concatenate([mi, mr], axis=1)], axis=0)


FFT_H = FFT_P // 2


def _twiddled(fr_ref, fi_ref, tw):
    twr = tw[0:1, :]
    twi = tw[1:2, :]
    fr = fr_ref[...]
    fi = fi_ref[...]
    return fr * twr - fi * twi, fr * twi + fi * twr


def _cmul_rows(x, kr, ki):
    xr, xi = x[:FFT_P], x[FFT_P:]
    return jnp.concatenate([xr * kr - xi * ki, xr * ki + xi * kr], axis=0).astype(BF16)


_TN_DIMS = (((0,), (0,)), ((), ()))
_STAGE2_SPECS = [pl.BlockSpec((FFT_P, FFT_P), lambda d: (0, 0)),
                 pl.BlockSpec((FFT_P, FFT_P), lambda d: (0, 0)),
                 pl.BlockSpec((None, 2, FFT_P), lambda d: (d, 0, 0)),
                 pl.BlockSpec((2, FFT_P), lambda d: (0, 0))]


def _kspec_kernel(a_ref, fr_ref, fi_ref, tw_ref, twh_ref, o_ref, oh_ref):
    dd = pl.program_id(0)
    dot = functools.partial(jnp.dot, preferred_element_type=F32)

    def combine(xf, xb, out_ref):
        out_ref[0] = xf[:FFT_P] + xb[:FFT_P]
        out_ref[1] = xf[FFT_P:] - xb[FFT_P:]

    @pl.when(dd > 0)
    def _():
        rm = _cblock(*_twiddled(fr_ref, fi_ref, tw_ref[...])).astype(BF16)
        combine(dot(rm, a_ref[0].reshape(2 * FFT_P, HYW).astype(BF16)),
                dot(rm, a_ref[1].reshape(2 * FFT_P, HYW).astype(BF16)), o_ref)

    @pl.when(dd == 0)
    def _():
        for slot, tw, out_ref in ((0, tw_ref[...], o_ref), (1, twh_ref[...], oh_ref)):
            w = jnp.concatenate(_twiddled(fr_ref, fi_ref, tw), axis=0).astype(BF16)
            combine(dot(w, a_ref[0, slot].astype(BF16)), dot(w, a_ref[1, slot].astype(BF16)), out_ref)


def _kernel_spectrum(ak, fr, fi, tw, twh):
    return pl.pallas_call(
        _kspec_kernel,
        grid=(FFT_H,),
        in_specs=[pl.BlockSpec((2, None, 2, FFT_P, HYW), lambda d: (0, d, 0, 0, 0))] + _STAGE2_SPECS,
        out_specs=(pl.BlockSpec((2, None, FFT_P, HYW), lambda d: (0, d, 0, 0)),
                   pl.BlockSpec((2, FFT_P, HYW), lambda d: (0, 0, 0))),
        out_shape=(jax.ShapeDtypeStruct((2, FFT_H, FFT_P, HYW), F32),
                   jax.ShapeDtypeStruct((2, FFT_P, HYW), F32)),
        compiler_params=_cparams(1, VMEM_LIMIT),
        name="kernel_spectrum",
    )(ak, fr, fi, tw, twh)


def _mid_kernel(a_ref, kh_ref, khh_ref, fr_ref, fi_ref, tw_ref, twh_ref, o_ref):
    dd = pl.program_id(0)
    dot = functools.partial(jnp.dot, preferred_element_type=F32)
    dot_t = lambda w, y: lax.dot_general(w, y, _TN_DIMS, preferred_element_type=F32)

    @pl.when(dd > 0)
    def _():
        rm = _cblock(*_twiddled(fr_ref, fi_ref, tw_ref[...])).astype(BF16)
        for n in range(B):
            x = dot(rm, a_ref[n].reshape(2 * FFT_P, HYW).astype(BF16))
            y = _cmul_rows(x, kh_ref[0], kh_ref[1])
            o_ref[n] = dot_t(rm, y).reshape(2, FFT_P, HYW)

    @pl.when(dd == 0)
    def _():
        for slot, tw, k_ref in ((0, tw_ref[...], kh_ref), (1, twh_ref[...], khh_ref)):
            w = jnp.concatenate(_twiddled(fr_ref, fi_ref, tw), axis=0).astype(BF16)
            for n in range(B):
                y = _cmul_rows(dot(w, a_ref[n, slot].astype(BF16)), k_ref[0], k_ref[1])
                o_ref[n, slot] = dot_t(w, y)


def _fft_mid(au, kh, khh, fr, fi, tw, twh):
    pair = pl.BlockSpec((B, None, 2, FFT_P, HYW), lambda d: (0, d, 0, 0, 0))
    return pl.pallas_call(
        _mid_kernel,
        grid=(FFT_H,),
        in_specs=[pair,
                  pl.BlockSpec((2, None, FFT_P, HYW), lambda d: (0, d, 0, 0)),
                  pl.BlockSpec((2, FFT_P, HYW), lambda d: (0, 0, 0))] + _STAGE2_SPECS,
        out_specs=pair,
        out_shape=jax.ShapeDtypeStruct((B, FFT_H, 2, FFT_P, HYW), F32),
        compiler_params=_cparams(1, VMEM_LIMIT),
        name="fft_mid",
    )(au, kh, khh, fr, fi, tw, twh)


def _dft_tables():
    na = L // FFT_P
    a = np.arange(na)
    dd = np.arange(FFT_H)
    ang = 2.0 * np.pi * np.outer(dd, a) / FFT_P
    re_rows = np.cos(ang)
    im_rows = -np.sin(ang)
    im_rows[0] = np.cos(np.pi * a)
    f_first = np.stack([re_rows, im_rows], axis=1).reshape(FFT_P, na)
    gre = 2.0 * np.cos(ang)
    gim = -2.0 * np.sin(ang)
    gre[0] = 1.0
    gim[0] = np.cos(np.pi * a)
    g_last = np.stack([gre, gim], axis=1).reshape(FFT_P, na).T / FFT_N
    b = np.arange(FFT_P)
    angf = 2.0 * np.pi * np.outer(b, b) / FFT_P
    ang2 = 2.0 * np.pi * np.outer(np.arange(FFT_H + 1), b) / FFT_N
    tw = np.stack([np.cos(ang2), -np.sin(ang2)], axis=1)
    f32 = lambda x: jnp.asarray(x.astype(np.float32))
    return (f32(f_first).astype(BF16), f32(g_last).astype(BF16), f32(np.cos(angf)), f32(-np.sin(angf)),
            f32(tw[:FFT_H]), f32(tw[FFT_H]))


def _merge_kernel(ya_ref, cv_ref, vx_ref, x0_ref, ga_ref, gb_ref, x_ref, er_ref, ec_ref, gt1_ref, hyd_ref,
                  wpa_ref, wpb_ref, wo_ref, g2_ref, sh2_ref, sc2_ref, wr_ref, br_ref,
                  x1_ref, t2_ref, lg_ref):
    vx = vx_ref[...]
    yb = x0_ref[...] * (cv_ref[...] + vx * hyd_ref[...])
    pa = jnp.dot(ya_ref[...].astype(BF16), wpa_ref[...], preferred_element_type=F32)
    pb = jnp.dot(yb.astype(BF16), wpb_ref[...], preferred_element_type=F32)
    mixed = (jax.nn.sigmoid(ga_ref[...].astype(F32)) * pa
             + jax.nn.sigmoid(gb_ref[...].astype(F32)) * pb)
    xm = jnp.dot(mixed.astype(BF16), wo_ref[...], preferred_element_type=F32)
    nrow = TM_MG // GRID_W
    row0 = (pl.program_id(0) % (L // TM_MG)) * nrow
    x1 = x_ref[...] + _pos_tile(er_ref, ec_ref, row0, nrow) + gt1_ref[...] * xm
    x1_ref[...] = x1
    t2 = _rms(x1) * g2_ref[...]
    t2 = t2 * (1.0 + sc2_ref[...]) + sh2_ref[...]
    t2_ref[...] = t2
    t_hi = t2.astype(BF16)
    t_lo = (t2 - t_hi.astype(F32)).astype(BF16)
    rr = (jnp.dot(t_hi, wr_ref[...], preferred_element_type=F32)
          + jnp.dot(t_lo, wr_ref[...], preferred_element_type=F32))
    lg_ref[...] = rr[:, :V7X_LANES] + rr[:, V7X_LANES:] + br_ref[...]


def _merge(ya, cv, vx, x0c, p, x2, er, ec, gt1, hyd, wpa, wpb, wo, g2, sh2, sc2, wr, br):
    tpb = L // TM_MG
    half = lambda: pl.BlockSpec((TM_MG, HYW), lambda i: (i, 0))
    full = lambda shape: pl.BlockSpec(shape, lambda i: tuple(0 for _ in shape))
    perb = lambda: pl.BlockSpec((None, 1, D), lambda i: (i // tpb, 0, 0))
    return pl.pallas_call(
        _merge_kernel,
        grid=(T // TM_MG,),
        in_specs=[half(), half(), half(), half(),
                  pl.BlockSpec((TM_MG, D), lambda i: (i, 3)),
                  pl.BlockSpec((TM_MG, D), lambda i: (i, 4)),
                  pl.BlockSpec((TM_MG, D), lambda i: (i, 0)),
                  full((L // GRID_W, D // 2)), full((GRID_W, D // 2)),
                  perb(), full((1, HYW)),
                  full((KD, D)), full((HYW, D)), full((D, D)),
                  full((1, D)), perb(), perb(),
                  full((D, 2 * V7X_LANES)), full((1, V7X_LANES))],
        out_specs=(pl.BlockSpec((TM_MG, D), lambda i: (i, 0)),
                   pl.BlockSpec((TM_MG, D), lambda i: (i, 0)),
                   pl.BlockSpec((TM_MG, V7X_LANES), lambda i: (i, 0))),
        out_shape=(jax.ShapeDtypeStruct((T, D), F32), jax.ShapeDtypeStruct((T, D), F32),
                   jax.ShapeDtypeStruct((T, V7X_LANES), F32)),
        compiler_params=_cparams(1, VMEM_LIMIT),
        name="merge",
    )(ya, cv, vx, x0c, p, p, x2, er, ec, gt1, hyd, wpa, wpb, wo, g2, sh2, sc2, wr, br)


def _route_kernel(lg_ref, info_ref, cnt_ref):
    @pl.when(pl.program_id(0) == 0)
    def _():
        cnt_ref[...] = jnp.zeros_like(cnt_ref)

    lg = lg_ref[...]
    lane = lax.broadcasted_iota(I32, lg.shape, 1)
    lanef = lane.astype(F32)
    neg = -1e30
    big = 1e9
    is_g = (lane >= NEXP) & (lane < NEXP + NGRP)
    gl = jnp.where(is_g, lg, neg)
    ge = jnp.where(is_g, jnp.exp(gl - jnp.max(gl, axis=-1, keepdims=True)), 0.0)
    pg = ge / jnp.sum(ge, axis=-1, keepdims=True)
    p_top_g = jnp.max(pg, axis=-1, keepdims=True)
    gidx = jnp.min(jnp.where(is_g & (pg == p_top_g), lanef, big), axis=-1, keepdims=True)
    g_sel = gidx.astype(I32) - NEXP
    emask = (lane < NEXP) & (jnp.right_shift(lane, NEPG.bit_length() - 1) == g_sel)
    el = jnp.where(emask, lg, neg)
    ee = jnp.where(emask, jnp.exp(el - jnp.max(el, axis=-1, keepdims=True)), 0.0)
    pe = ee / jnp.sum(ee, axis=-1, keepdims=True)
    p1 = jnp.max(jnp.where(emask, pe, -1.0), axis=-1, keepdims=True)
    i1 = jnp.min(jnp.where(emask & (pe == p1), lanef, big), axis=-1, keepdims=True)
    rest = emask & (lanef != i1)
    p2 = jnp.max(jnp.where(rest, pe, -1.0), axis=-1, keepdims=True)
    i2 = jnp.min(jnp.where(rest & (pe == p2), lanef, big), axis=-1, keepdims=True)
    wsum = p1 + p2
    w1 = p_top_g * p1 / wsum
    w2 = p_top_g * p2 / wsum
    sel1 = lanef == i1
    sel2 = lanef == i2
    oh = jnp.where(sel1 | sel2, 1.0, 0.0)
    r = lax.broadcasted_iota(I32, (TR, TR), 0)
    c = lax.broadcasted_iota(I32, (TR, TR), 1)
    stril = jnp.where(c < r, 1.0, 0.0).astype(BF16)
    before = jnp.dot(stril, oh.astype(BF16), preferred_element_type=F32) + cnt_ref[...]
    r1 = jnp.sum(jnp.where(sel1, before, 0.0), axis=-1, keepdims=True)
    r2 = jnp.sum(jnp.where(sel2, before, 0.0), axis=-1, keepdims=True)
    cnt_ref[...] += jnp.sum(oh, axis=0, keepdims=True)
    info = jnp.where(lane == 0, i1, jnp.where(lane == 1, r1, jnp.where(lane == 2, i2, jnp.where(
        lane == 3, r2, jnp.where(lane == 4, w1, jnp.where(lane == 5, w2, 0.0))))))
    info_ref[...] = info


def _route(lg):
    return pl.pallas_call(
        _route_kernel,
        grid=(T // TR,),
        in_specs=[pl.BlockSpec((TR, V7X_LANES), lambda i: (i, 0))],
        out_specs=(pl.BlockSpec((TR, V7X_LANES), lambda i: (i, 0)),
                   pl.BlockSpec((1, V7X_LANES), lambda i: (0, 0))),
        out_shape=(jax.ShapeDtypeStruct((T, V7X_LANES), F32), jax.ShapeDtypeStruct((1, V7X_LANES), F32)),
        compiler_params=_cparams(1, VMEM_LIMIT),
        name="route",
    )(lg)


def _positions_kernel(info_ref, st_ref, o_ref):
    info = info_ref[...]
    lane = lax.broadcasted_iota(I32, info.shape, 1)
    lanef = lane.astype(F32)
    st = st_ref[...]
    row = lambda e_lane, r_lane: (jnp.sum(jnp.where(lanef == _lane_pick(info, lane, e_lane), st, 0.0),
                                          axis=-1, keepdims=True) + _lane_pick(info, lane, r_lane))
    o_ref[...] = jnp.where(lane == 0, row(0, 1), jnp.where(lane == 1, row(2, 3), 0.0)).astype(I32)


def _positions(info, starts_row):
    return pl.pallas_call(
        _positions_kernel,
        grid=(T // TPOS,),
        in_specs=[pl.BlockSpec((TPOS, V7X_LANES), lambda i: (i, 0)),
                  pl.BlockSpec((1, V7X_LANES), lambda i: (0, 0))],
        out_specs=pl.BlockSpec((TPOS, V7X_LANES), lambda i: (i, 0)),
        out_shape=jax.ShapeDtypeStruct((T, V7X_LANES), I32),
        compiler_params=_cparams(1, VMEM_LIMIT),
        name="positions",
    )(info, starts_row)


def _scatter_kernel(pos_ref, zt_ref, t2_ref, zeros_hbm, xs_hbm, sem, zsem):
    i = pl.program_id(0)

    def zcopy(row):
        start = pl.multiple_of(jnp.maximum(row, 0), TE)
        return pltpu.make_async_copy(zeros_hbm, xs_hbm.at[pl.ds(start, TE)], zsem)

    @pl.when(i == 0)
    def _():
        def ztail(start, e, carry):
            @pl.when(zt_ref[0, e] >= 0)
            def _():
                cp = zcopy(zt_ref[0, e])
                cp.start() if start else cp.wait()
            return carry

        lax.fori_loop(0, NEXP, functools.partial(ztail, True), 0)
        lax.fori_loop(0, NEXP, functools.partial(ztail, False), 0)

        def zrest(start, tile, carry):
            cp = zcopy(tile * TE)
            cp.start() if start else cp.wait()
            return carry

        lax.fori_loop(zt_ref[0, NA_OFF], NT_EXP, functools.partial(zrest, True), 0)
        lax.fori_loop(zt_ref[0, NA_OFF], NT_EXP, functools.partial(zrest, False), 0)

    def row_copy(base, jj, kk):
        dst = pos_ref[0, 2 * (base + jj) + kk]
        return pltpu.make_async_copy(t2_ref.at[pl.ds(base, ROW_GROUP)].at[pl.ds(jj, 1)],
                                     xs_hbm.at[pl.ds(dst, 1)], sem)

    _start_rows(TS, row_copy)
    _wait_rows(TS, pltpu.make_async_copy(t2_ref.at[pl.ds(0, 1)], xs_hbm.at[pl.ds(0, 1)], sem))


def _start_rows(n_rows, row_copy):
    def group(g, carry):
        base = pl.multiple_of(g * ROW_GROUP, ROW_GROUP)
        for jj in range(ROW_GROUP):
            for kk in range(2):
                row_copy(base, jj, kk).start(priority=kk)
        return carry

    lax.fori_loop(0, n_rows // ROW_GROUP, group, 0)


def _wait_rows(n_rows, one_row_copy):
    def drain(j, carry):
        one_row_copy.wait()
        one_row_copy.wait()
        return carry

    lax.fori_loop(0, n_rows, drain, 0, unroll=ROW_UNROLL)


def _scatter_rows(pos3, meta, t2, zeros_tile):
    return pl.pallas_call(
        _scatter_kernel,
        grid=(T // TS,),
        in_specs=[pl.BlockSpec((None, 1, 2 * TS), lambda i: (i, 0, 0), memory_space=pltpu.SMEM),
                  pl.BlockSpec(memory_space=pltpu.SMEM),
                  pl.BlockSpec((TS, D), lambda i: (i, 0)),
                  pl.BlockSpec(memory_space=pl.ANY)],
        out_specs=pl.BlockSpec(memory_space=pl.ANY),
        out_shape=jax.ShapeDtypeStruct((NP_ROWS, D), F32),
        scratch_shapes=[pltpu.SemaphoreType.DMA(()), pltpu.SemaphoreType.DMA(())],
        compiler_params=_cparams(1, VMEM_LIMIT),
        name="scatter_rows",
    )(pos3, meta, t2, zeros_tile)


def _expert_kernel(te_ref, nx_ref, sl_ref, na_ref, xs_ref, wg_hbm, wu_hbm, wd_hbm, ys_ref,
                   wgs_ref, wus_ref, wds_ref, wgb_ref, wub_ref, wdb_ref, sem):
    i = pl.program_id(0)
    active = i < na_ref[0]
    first = active & ((i == 0) | (te_ref[i] != te_ref[jnp.maximum(i - 1, 0)]))

    def fetch(e, slot):
        return [pltpu.make_async_copy(wg_hbm.at[e], wgs_ref.at[slot], sem.at[slot]),
                pltpu.make_async_copy(wu_hbm.at[e], wus_ref.at[slot], sem.at[slot]),
                pltpu.make_async_copy(wd_hbm.at[e], wds_ref.at[slot], sem.at[slot])]

    for slot in range(2):
        @pl.when(first & (sl_ref[i] == slot))
        def _():
            @pl.when(i == 0)
            def _():
                for cp in fetch(te_ref[i], slot):
                    cp.start()

            for cp in fetch(te_ref[i], slot):
                cp.wait()
            wgb_ref[...] = wgs_ref[slot].astype(BF16)
            wub_ref[...] = wus_ref[slot].astype(BF16)
            wdb_ref[...] = wds_ref[slot].astype(BF16)

            @pl.when(nx_ref[i] >= 0)
            def _():
                for cp in fetch(nx_ref[i], 1 - slot):
                    cp.start()

    @pl.when(active)
    def _():
        x = xs_ref[...].astype(BF16)
        g = jnp.dot(x, wgb_ref[...], preferred_element_type=F32)
        u = jnp.dot(x, wub_ref[...], preferred_element_type=F32)
        hid = (g * jax.nn.sigmoid(g) * u).astype(BF16)
        ys_ref[...] = jnp.dot(hid, wdb_ref[...], preferred_element_type=F32)

    @pl.when(jnp.logical_not(active))
    def _():
        ys_ref[...] = jnp.zeros_like(ys_ref)


def _experts(tile_expert, tile_next, tile_slot, n_active, xs, wg, wu, wd):
    rows = lambda i, *_: (i, 0)
    rows_in = lambda i, te, nx, sl, na: (jnp.minimum(i, na[0] - 1), 0)
    any_space = pl.BlockSpec(memory_space=pl.ANY)
    grid_spec = pltpu.PrefetchScalarGridSpec(
        num_scalar_prefetch=4,
        grid=(NT_EXP,),
        in_specs=[pl.BlockSpec((TE, D), rows_in), any_space, any_space, any_space],
        out_specs=pl.BlockSpec((TE, D), rows),
        scratch_shapes=[pltpu.VMEM((2, D, DEXP), F32), pltpu.VMEM((2, D, DEXP), F32), pltpu.VMEM((2, DEXP, D), F32),
                        pltpu.VMEM((D, DEXP), BF16), pltpu.VMEM((D, DEXP), BF16), pltpu.VMEM((DEXP, D), BF16),
                        pltpu.SemaphoreType.DMA((2,))],
    )
    return pl.pallas_call(
        _expert_kernel,
        grid_spec=grid_spec,
        out_shape=jax.ShapeDtypeStruct((NP_ROWS, D), F32),
        compiler_params=_cparams(1, VMEM_LIMIT),
        name="experts",
    )(tile_expert, tile_next, tile_slot, n_active, xs, wg, wu, wd)


def _combine_kernel(pos_ref, posn_ref, info_ref, x1_ref, gt2_ref, fg_ref, ys_hbm, o_ref, buf, sem):
    g = pl.program_id(0)
    ng = pl.num_programs(0)

    def gather(p_ref, half, slot):
        def row_copy(base, jj, kk):
            src = p_ref[0, 2 * (half * TC + base + jj) + kk]
            return pltpu.make_async_copy(ys_hbm.at[pl.ds(src, 1)],
                                         buf.at[slot, kk].at[pl.ds(base, ROW_GROUP)].at[pl.ds(jj, 1)],
                                         sem.at[slot])
        _start_rows(TC, row_copy)

    def finish(half, slot):
        _wait_rows(TC, pltpu.make_async_copy(ys_hbm.at[pl.ds(0, 1)], buf.at[slot, 0].at[pl.ds(0, 1)],
                                             sem.at[slot]))
        rows = slice(half * TC, (half + 1) * TC)
        info = info_ref[rows, :]
        lane = lax.broadcasted_iota(I32, info.shape, 1)
        moe = _lane_pick(info, lane, 4) * buf[slot, 0] + _lane_pick(info, lane, 5) * buf[slot, 1]
        x2 = x1_ref[rows, :] + gt2_ref[...] * moe
        o_ref[rows, :] = _rms(x2) * fg_ref[...]

    @pl.when(g == 0)
    def _():
        gather(pos_ref, 0, 0)

    gather(pos_ref, 1, 1)
    finish(0, 0)

    @pl.when(g + 1 < ng)
    def _():
        gather(posn_ref, 0, 0)

    finish(1, 1)


def _combine(pos3, info, x1, gt2, fg, ys):
    step = 2 * TC
    tpb = L // step
    nsteps = T // step
    return pl.pallas_call(
        _combine_kernel,
        grid=(nsteps,),
        in_specs=[pl.BlockSpec((None, 1, 2 * step), lambda i: (i, 0, 0), memory_space=pltpu.SMEM),
                  pl.BlockSpec((None, 1, 2 * step), lambda i: (jnp.minimum(i + 1, nsteps - 1), 0, 0),
                               memory_space=pltpu.SMEM),
                  pl.BlockSpec((step, V7X_LANES), lambda i: (i, 0)),
                  pl.BlockSpec((step, D), lambda i: (i, 0)),
                  pl.BlockSpec((None, 1, D), lambda i: (i // tpb, 0, 0)),
                  pl.BlockSpec((1, D), lambda i: (0, 0)),
                  pl.BlockSpec(memory_space=pl.ANY)],
        out_specs=pl.BlockSpec((step, D), lambda i: (i, 0)),
        out_shape=jax.ShapeDtypeStruct((T, D), F32),
        scratch_shapes=[pltpu.VMEM((2, 2, TC, D), F32), pltpu.SemaphoreType.DMA((2,))],
        compiler_params=_cparams(1, VMEM_LIMIT),
        name="combine",
    )(pos3, pos3, info, x1, gt2, fg, ys)


def _pos_tables():
    rows = L // GRID_W
    quarter = D // 4
    omega = 1.0 / (10000.0 ** (jnp.arange(quarter, dtype=F32) / quarter))

    def axis_emb(pos):
        a = pos[:, None] * omega[None, :]
        return jnp.concatenate([jnp.sin(a), jnp.cos(a)], axis=-1)

    er = axis_emb(jnp.arange(rows, dtype=F32))
    ec = axis_emb(jnp.arange(GRID_W, dtype=F32))
    return er, ec


def _filter_features():
    z = np.zeros((L, V7X_LANES), np.float64)
    bands = (HY_EMB - 1) // 2
    ang = (2.0 * np.pi * np.arange(L) / L)[:, None] * np.linspace(1e-4, bands - 1, bands)[None, :]
    z[:, 0] = np.linspace(0.0, 1.0, L)
    z[:, 1:1 + bands] = np.cos(ang)
    z[:, 1 + bands:HY_EMB] = -np.sin(ang)
    z[1:, HY_EMB] = 1.0
    return jnp.asarray(z.astype(np.float32))


def _pad2(a, rows, cols):
    return jnp.pad(a, ((0, rows - a.shape[0]), (0, cols - a.shape[1])))


def kernel(x, c, ctx, c_ctx, ada_w, ada_b, norm1_g, norm2_g, w_in, hgrn_lb, hgrn_norm_g, hy_conv_w, hy_conv_b, hy_filt_w1, hy_filt_b1, hy_filt_freq1, hy_filt_w2, hy_filt_b2, hy_filt_freq2, hy_filt_w3, hy_d, w_proj_a, w_proj_b, w_out, moe_router_g_w, moe_router_g_b, moe_router_e_w, moe_router_e_b, moe_w_gate, moe_w_up, moe_w_down, final_norm_g):
    cvec = jnp.zeros((8, D), F32).at[0:B].set(c).at[B].set(c_ctx)
    mod = _adaln(cvec, ada_w[0], ada_b[0][None, :])
    m6 = mod.reshape(8, 6, D)
    sh1, sc1, gt1, sh2, sc2, gt2 = [m6[0:B, k][:, None, :] for k in range(6)]
    csh1, csc1 = m6[B:B + 1, 0], m6[B:B + 1, 1]

    lbs = jnp.cumsum(jax.nn.softmax(hgrn_lb.astype(F32), axis=0), axis=0)[0]
    g1 = norm1_g[0][None, :]
    er, ec = _pos_tables()
    x2 = x.reshape(T, D)

    w_ctx = w_in[0][:, KD:4 * KD].astype(BF16)
    s_f, s_b = _context_states(ctx, g1, csh1, csc1, w_ctx, lbs)

    pz, p = _in_projection(x2, er, ec, g1, sh1, sc1, w_in[0].astype(BF16))

    o_f = _hgrn_scan(pz, p, lbs[0:1], s_f, False)
    y_a = _hgrn_scan(pz, p, lbs[1:2], s_b, True, o_f=o_f, norm_g=hgrn_norm_g[0][None, :])

    vx, x0c = _hyena_pre(p, hy_conv_w[0], hy_conv_b[0][None, :])
    deltas = jnp.abs(jnp.linspace(math.log(HY_DECAY_TARGET) / HY_SLOW_PCT,
                                  math.log(HY_DECAY_TARGET) / HY_FAST_PCT, HYW, dtype=F32))[None, :]
    ln = V7X_LANES
    fh = hy_filt_w2.shape[-1]
    blockdiag = lambda m: jnp.concatenate([_pad2(m, m.shape[0], 2 * m.shape[1]),
                                           jnp.pad(m, ((0, 0), (m.shape[1], 0)))], axis=0)
    twice = lambda v: jnp.concatenate([v, v])[None, :]
    w3 = hy_filt_w3[0]
    taps = _filter_taps(
        _filter_features(),
        blockdiag(_pad2(hy_filt_w1[0], ln, fh)), twice(hy_filt_b1[0]), twice(hy_filt_freq1[0]),
        blockdiag(hy_filt_w2[0]), twice(hy_filt_b2[0]), twice(hy_filt_freq2[0]),
        _pad2(w3, ln, 2 * HYW), jnp.pad(w3, ((fh, 0), (0, 0))), deltas)
    f_first, g_last, fr, fi, tw, twh = _dft_tables()
    na = L // FFT_P
    ak = _strided_dft(taps.reshape(2, na, FFT_P, HYW), f_first, "dft_first_taps")
    kh, khh = _kernel_spectrum(ak.reshape(2, FFT_H, 2, FFT_P, HYW), fr, fi, tw, twh)
    au = _strided_dft(vx.reshape(B, na, FFT_P, HYW), f_first, "dft_first")
    bp = _fft_mid(au.reshape(B, FFT_H, 2, FFT_P, HYW), kh, khh, fr, fi, tw, twh)
    conv = _strided_dft(bp.reshape(B, FFT_P, FFT_P, HYW), g_last, "dft_last").reshape(T, HYW)

    wr = jnp.concatenate([jnp.transpose(moe_router_e_w[0], (1, 0, 2)).reshape(D, NEXP),
                          moe_router_g_w[0], jnp.zeros((D, V7X_LANES - NEXP - NGRP), F32)], axis=1)
    wr_hi = wr.astype(BF16)
    wr = jnp.concatenate([wr_hi, (wr - wr_hi.astype(F32)).astype(BF16)], axis=1)
    br = jnp.concatenate([moe_router_e_b[0].reshape(NEXP), moe_router_g_b[0],
                          jnp.zeros((V7X_LANES - NEXP - NGRP,), F32)])[None, :]
    x1, t2, lg = _merge(y_a.reshape(T, KD), conv, vx, x0c, p, x2, er, ec, gt1, hy_d[0][None, :],
                        w_proj_a[0].astype(BF16), w_proj_b[0].astype(BF16), w_out[0].astype(BF16),
                        norm2_g[0][None, :], sh2, sc2, wr, br)

    info, counts = _route(lg)
    cnt = counts[0, :NEXP].astype(I32)
    pc = ((cnt + TE - 1) // TE) * TE
    ends = jnp.cumsum(pc)
    starts = ends - pc
    n_active = (ends[-1] // TE).astype(I32)[None]
    tile_rows = jnp.arange(NT_EXP, dtype=I32) * TE
    tile_expert = jnp.minimum(jnp.sum((ends[None, :] <= tile_rows[:, None]).astype(I32), axis=1), NEXP - 1)
    meta = jnp.concatenate([jnp.where(pc > 0, ends - TE, -1), starts, n_active]).astype(I32)[None, :]
    starts_row = jnp.pad(starts.astype(F32), (0, V7X_LANES - NEXP))[None, :]
    pos3 = _positions(info, starts_row)[:, :2].reshape(T // TS, 1, 2 * TS)

    xs = _scatter_rows(pos3, meta, t2, jnp.zeros((TE, D), F32))
    eid = jnp.arange(NEXP, dtype=I32)
    nonempty = pc > 0
    later = jnp.where(nonempty[None, :] & (eid[None, :] > eid[:, None]), eid[None, :], NEXP)
    next_e = jnp.min(later, axis=1)
    next_e = jnp.where(next_e < NEXP, next_e, -1).astype(I32)
    slot_e = ((jnp.cumsum(nonempty.astype(I32)) - 1) % 2).astype(I32)
    ys = _experts(tile_expert, next_e[tile_expert], slot_e[tile_expert], n_active, xs,
                  moe_w_gate[0].reshape(NEXP, D, DEXP), moe_w_up[0].reshape(NEXP, D, DEXP),
                  moe_w_down[0].reshape(NEXP, DEXP, D))
    out = _combine(pos3, info, x1, gt2, final_norm_g[None, :], ys)
    return out.reshape(B, L, D)
```

```python
import functools
import math

import numpy as np
import jax
import jax.numpy as jnp
from jax import lax
from jax.experimental import pallas as pl
from jax.experimental.pallas import tpu as pltpu

F32 = jnp.float32
BF16 = jnp.bfloat16
I32 = jnp.int32
HIGHEST = lax.Precision.HIGHEST

D = 1024
B = 2
L = 8192
T = B * L
CTX = 256
GRID_W = 64
EPS = 1e-6
H = 4
DK = 128
DV = 128
KD = H * DK
IN_W = 6144
HYW = 512
HY_EMB = 33
NGRP = 4
NEPG = 8
NEXP = NGRP * NEPG
DEXP = 512
HY_DECAY_TARGET = 1e-2
HY_FAST_PCT = 0.3
HY_SLOW_PCT = 1.5

V7X_LANES = 128
V7X_SUBLANES = 8
V7X_VMEM_BYTES = 64 * 1024 * 1024
VMEM_LIMIT = (3 * V7X_VMEM_BYTES) // 4

FFT_N = 2 * L
FFT_P = 128
FFT_BB = 8

TM_IN = 1024
TN_IN = 1024
TH = 128
CB = 32
TM_HY = 1024
HALO = 2 * V7X_SUBLANES
TM_MG = 512
TR = 512
TE = 512
TPOS = 2048
NP_ROWS = 2 * T + NEXP * TE
NT_EXP = NP_ROWS // TE
TS = 512
TC = TS // 2
ROW_UNROLL = 8
ROW_GROUP = 32
ST_OFF = NEXP
NA_OFF = 2 * NEXP


def _cparams(n_axes, vmem=None):
    return pltpu.CompilerParams(dimension_semantics=("arbitrary",) * n_axes,
                                vmem_limit_bytes=vmem)


def _split3(x):
    hi = x.astype(BF16)
    r = x - hi.astype(F32)
    mid = r.astype(BF16)
    lo = (r - mid.astype(F32)).astype(BF16)
    return hi, mid, lo


def _dot01(m, x):
    hi, mid, lo = _split3(x)
    return (jnp.dot(m, hi, preferred_element_type=F32) + jnp.dot(m, mid, preferred_element_type=F32)
            + jnp.dot(m, lo, preferred_element_type=F32))


def _rms(x):
    return x * lax.rsqrt(jnp.mean(x * x, axis=-1, keepdims=True) + EPS)


def _lane_pick(x, lane, idx):
    return jnp.sum(jnp.where(lane == idx, x, 0.0), axis=-1, keepdims=True)


def _ada_kernel(c_ref, w_ref, b_ref, o_ref):
    c = c_ref[...]
    s = c * jax.nn.sigmoid(c)
    o_ref[...] = jnp.dot(s, w_ref[...], preferred_element_type=F32, precision=HIGHEST) + b_ref[...]


def _adaln(cvec, w, b):
    tn = 1536
    return pl.pallas_call(
        _ada_kernel,
        grid=(6 * D // tn,),
        in_specs=[pl.BlockSpec((8, D), lambda j: (0, 0)),
                  pl.BlockSpec((D, tn), lambda j: (0, j)),
                  pl.BlockSpec((1, tn), lambda j: (0, j))],
        out_specs=pl.BlockSpec((8, tn), lambda j: (0, j)),
        out_shape=jax.ShapeDtypeStruct((8, 6 * D), F32),
        compiler_params=_cparams(1, VMEM_LIMIT),
        name="adaln",
    )(cvec, w, b)


def _keys(z, lb):
    sig = jax.nn.sigmoid(z)
    logf = jnp.log(lb + (1.0 - lb) * sig)
    k = (1.0 - lb) * jax.nn.sigmoid(-z)
    return k, logf


def _ctx_kernel(ctx_ref, g_ref, sh_ref, sc_ref, w_ref, lb_ref, sf_ref, sb_ref):
    h = _rms(ctx_ref[...]) * g_ref[...]
    h = h * (1.0 + sc_ref[...]) + sh_ref[...]
    p = jnp.dot(h.astype(BF16), w_ref[...], preferred_element_type=F32)
    zf, zb, v = p[:, :KD], p[:, KD:2 * KD], p[:, 2 * KD:]
    kf, lf = _keys(zf, lb_ref[0:1, :])
    kb, lbk = _keys(zb, lb_ref[1:2, :])
    r = lax.broadcasted_iota(I32, (CTX, CTX), 0)
    c = lax.broadcasted_iota(I32, (CTX, CTX), 1)
    tril = jnp.where(c <= r, 1.0, 0.0).astype(BF16)
    cf = _dot01(tril, lf)
    cb = _dot01(tril, lbk)
    kfd = (kf * jnp.exp(cf[CTX - 1:CTX, :] - cf)).astype(BF16)
    kbd = (kb * jnp.exp(cb - lbk)).astype(BF16)
    vb = v.astype(BF16)
    tn = (((0,), (0,)), ((), ()))
    for hh in range(H):
        hs = slice(hh * DK, (hh + 1) * DK)
        sf_ref[hh] = lax.dot_general(vb[:, hs], kfd[:, hs], tn, preferred_element_type=F32)
        sb_ref[hh] = lax.dot_general(vb[:, hs], kbd[:, hs], tn, preferred_element_type=F32)


def _context_states(ctx, g1, csh1, csc1, w_ctx, lbs):
    st = jax.ShapeDtypeStruct((B, H, DV, DK), F32)
    return pl.pallas_call(
        _ctx_kernel,
        grid=(B,),
        in_specs=[pl.BlockSpec((None, CTX, D), lambda b: (b, 0, 0)),
                  pl.BlockSpec((1, D), lambda b: (0, 0)),
                  pl.BlockSpec((1, D), lambda b: (0, 0)),
                  pl.BlockSpec((1, D), lambda b: (0, 0)),
                  pl.BlockSpec((D, 3 * KD), lambda b: (0, 0)),
                  pl.BlockSpec((2, KD), lambda b: (0, 0))],
        out_specs=(pl.BlockSpec((None, H, DV, DK), lambda b: (b, 0, 0, 0)),
                   pl.BlockSpec((None, H, DV, DK), lambda b: (b, 0, 0, 0))),
        out_shape=(st, st),
        compiler_params=_cparams(1, VMEM_LIMIT),
        name="ctx_states",
    )(ctx, g1, csh1, csc1, w_ctx, lbs)


def _pos_tile(er_ref, ec_ref, row0, nrow):
    lo = jnp.concatenate([jnp.broadcast_to(er_ref[pl.ds(row0 + i, 1), :], (GRID_W, D // 2))
                          for i in range(nrow)], axis=0)
    hi = jnp.concatenate([ec_ref[...]] * nrow, axis=0)
    return jnp.concatenate([lo, hi], axis=1)


def _inproj_kernel(x_ref, er_ref, ec_ref, g_ref, sh_ref, sc_ref, w_ref, oz_ref, o_ref, hx_ref):
    i = pl.program_id(0)
    j = pl.program_id(1)
    ni = pl.num_programs(0)
    nj = pl.num_programs(1)
    slot = i % 2

    def prepare(tile, dst):
        nrow = TM_IN // GRID_W
        row0 = (tile % (L // TM_IN)) * nrow
        h = _rms(x_ref[...] + _pos_tile(er_ref, ec_ref, row0, nrow)) * g_ref[...]
        hx_ref[dst] = (h * (1.0 + sc_ref[...]) + sh_ref[...]).astype(BF16)

    def project():
        return jnp.dot(hx_ref[slot], w_ref[...], preferred_element_type=F32)

    @pl.when((i == 0) & (j == 0))
    def _():
        prepare(0, 0)

    @pl.when(j == 0)
    def _():
        r = project()
        o_ref[:, :KD] = r[:, :KD].astype(BF16)
        oz_ref[:, :KD] = r[:, KD:]

    @pl.when(j == 1)
    def _():
        r = project()
        oz_ref[:, KD:] = r[:, :KD]
        o_ref[:, KD:] = r[:, KD:].astype(BF16)

    @pl.when((j > 1) & (j < nj - 1))
    def _():
        o_ref[...] = project().astype(BF16)

    @pl.when((j == nj - 1) & (i + 1 < ni))
    def _():
        o_ref[...] = project().astype(BF16)
        prepare(i + 1, 1 - slot)

    @pl.when((j == nj - 1) & (i + 1 == ni))
    def _():
        o_ref[...] = project().astype(BF16)


def _in_projection(x2, er, ec, g1, sh1, sc1, w_bf):
    tiles_per_batch = L // TM_IN
    n_i = T // TM_IN
    n_j = IN_W // TN_IN
    ahead = lambda i, j: jnp.minimum(i + jnp.where(j == n_j - 1, 1, 0), n_i - 1)
    return pl.pallas_call(
        _inproj_kernel,
        grid=(n_i, n_j),
        in_specs=[pl.BlockSpec((TM_IN, D), lambda i, j: (ahead(i, j), 0)),
                  pl.BlockSpec((L // GRID_W, D // 2), lambda i, j: (0, 0)),
                  pl.BlockSpec((GRID_W, D // 2), lambda i, j: (0, 0)),
                  pl.BlockSpec((1, D), lambda i, j: (0, 0)),
                  pl.BlockSpec((None, 1, D), lambda i, j: (ahead(i, j) // tiles_per_batch, 0, 0)),
                  pl.BlockSpec((None, 1, D), lambda i, j: (ahead(i, j) // tiles_per_batch, 0, 0)),
                  pl.BlockSpec((D, TN_IN), lambda i, j: (0, j))],
        out_specs=(pl.BlockSpec((TM_IN, TN_IN), lambda i, j: (i, 0)),
                   pl.BlockSpec((TM_IN, TN_IN), lambda i, j: (i, jnp.maximum(j - 1, 0)))),
        out_shape=(jax.ShapeDtypeStruct((T, 2 * KD), F32),
                   jax.ShapeDtypeStruct((T, IN_W - 2 * KD), BF16)),
        scratch_shapes=[pltpu.VMEM((2, TM_IN, D), BF16)],
        compiler_params=_cparams(2, VMEM_LIMIT),
        name="in_proj",
    )(x2, er, ec, g1, sh1, sc1, w_bf)


def _hgrn_kernel(reverse, readout, *refs):
    if readout:
        q_ref, z_ref, v_ref, lb_ref, s0_ref, of_ref, g_ref, ng_ref, o_ref, st_ref = refs
    else:
        q_ref, z_ref, v_ref, lb_ref, s0_ref, o_ref, st_ref = refs

    @pl.when(pl.program_id(0) == 0)
    def _():
        st_ref[...] = s0_ref[...]

    r = lax.broadcasted_iota(I32, (TH, TH), 0)
    c = lax.broadcasted_iota(I32, (TH, TH), 1)
    cb_shift = CB.bit_length() - 1
    same = jnp.right_shift(r, cb_shift) == jnp.right_shift(c, cb_shift)
    tri_mask = same & ((c >= r) if reverse else (c <= r))
    tri = jnp.where(tri_mask, 1.0, 0.0).astype(BF16)
    rblk = jnp.right_shift(r, cb_shift)
    cblk = jnp.right_shift(c, cb_shift)
    dist = (rblk - cblk) if not reverse else (cblk - rblk)
    for b in range(B):
        _hgrn_chunk(reverse, readout, b, tri, tri_mask, dist, refs)


def _hgrn_chunk(reverse, readout, b, tri, tri_mask, dist, refs):
    if readout:
        q_ref, z_ref, v_ref, lb_ref, s0_ref, of_ref, g_ref, ng_ref, o_ref, st_ref = refs
    else:
        q_ref, z_ref, v_ref, lb_ref, s0_ref, o_ref, st_ref = refs
    q = q_ref[b].astype(F32)
    v = v_ref[b]
    k, logf = _keys(z_ref[b], lb_ref[...])
    bl = _dot01(tri, logf)
    nt = (((1,), (1,)), ((), ()))
    tn = (((0,), (0,)), ((), ()))
    nblk = TH // CB
    e_row = 0 if reverse else CB - 1
    m_row = CB - 1 - CB // 2 if reverse else CB // 2
    tau = [bl[jb * CB + e_row:jb * CB + e_row + 1] for jb in range(nblk)]
    mid = [bl[jb * CB + m_row:jb * CB + m_row + 1] for jb in range(nblk)]
    rows = lambda vecs: jnp.concatenate([jnp.broadcast_to(x, (CB, KD)) for x in vecs], axis=0)
    mid_b = rows(mid)
    qd0 = (q * jnp.exp(bl - mid_b)).astype(BF16)
    kd0 = (k * jnp.exp(mid_b - bl)).astype(BF16)
    qs = q * jnp.exp(bl)
    ke = k * jnp.exp(rows(tau) - bl)
    order = list(range(nblk - 1, -1, -1)) if reverse else list(range(nblk))
    pre = [jnp.zeros((1, KD), F32)]
    for i in range(nblk):
        pre.append(pre[-1] + tau[order[i]])
    total = pre[nblk]
    entry = [None] * nblk
    leave = [None] * nblk
    gap = [[None] * nblk for _ in range(nblk)]
    for i, jb in enumerate(order):
        entry[jb] = jnp.exp(pre[i])
        leave[jb] = jnp.exp(total - pre[i + 1])
        for d in range(2, nblk):
            gap[d][jb] = jnp.exp(pre[i + d] - pre[i + 1]) if i + d < nblk else jnp.zeros((1, KD), F32)
    qc = (qs * rows(entry)).astype(BF16)
    kc = (ke * rows(leave)).astype(BF16)
    kx = jnp.concatenate([ke.astype(BF16)] + [(ke * rows(gap[d])).astype(BF16) for d in range(2, nblk)], axis=0)
    qsb = qs.astype(BF16)
    dec = jnp.exp(total)

    def blockdiag(x):
        first_head = lax.broadcasted_iota(I32, x.shape, 1) < DK
        zero = jnp.zeros_like(x)
        return jnp.concatenate([jnp.where(first_head, x, zero), jnp.where(first_head, zero, x)], axis=0)

    for hp in range(H // 2):
        ps = slice(2 * hp * DK, 2 * (hp + 1) * DK)
        sc0 = lax.dot_general(qd0[:, ps], blockdiag(kd0[:, ps]), nt, preferred_element_type=F32)
        scx = lax.dot_general(qsb[:, ps], blockdiag(kx[:, ps]), nt, preferred_element_type=F32)
        halves = []
        for hh in range(2):
            sc = jnp.where(tri_mask, sc0[:, hh * TH:(hh + 1) * TH], 0.0)
            base = hh * (nblk - 1) * TH
            for d in range(1, nblk):
                sc = jnp.where(dist == d, scx[:, base + (d - 1) * TH:base + d * TH], sc)
            halves.append(sc.astype(BF16))
        sc_pair = jnp.concatenate(halves, axis=1)
        st_a = st_ref[b, 2 * hp]
        st_b = st_ref[b, 2 * hp + 1]
        zst = jnp.zeros((DV, DK), BF16)
        st_pair = jnp.concatenate([jnp.concatenate([st_a.astype(BF16), zst], axis=1),
                                   jnp.concatenate([zst, st_b.astype(BF16)], axis=1)], axis=0)
        o_pair = (lax.dot_general(qc[:, ps], st_pair, nt, preferred_element_type=F32)
                  + jnp.dot(sc_pair, blockdiag(v[:, ps]), preferred_element_type=F32))
        upd = lax.dot_general(v[:, ps], kc[:, ps], tn, preferred_element_type=F32)
        st_ref[b, 2 * hp] = st_a * dec[:, ps][:, :DK] + upd[:DV, :DK]
        st_ref[b, 2 * hp + 1] = st_b * dec[:, ps][:, DK:] + upd[DV:, DK:]
        for hh in range(2):
            hs = slice((2 * hp + hh) * DK, (2 * hp + hh + 1) * DK)
            o_h = o_pair[:, hh * DV:(hh + 1) * DV]
            if readout:
                o_h = o_h + of_ref[b, :, hs]
                o_h = _rms(o_h) * ng_ref[...]
                gh = g_ref[b, :, hs].astype(F32)
                o_h = o_h * (gh * jax.nn.sigmoid(gh))
            o_ref[b, :, hs] = o_h


def _hgrn_scan(pz, p, lb_row, s0, reverse, o_f=None, norm_g=None):
    nch = L // TH
    chunk = (lambda c: nch - 1 - c) if reverse else (lambda c: c)
    col_spec = lambda j: pl.BlockSpec((B, TH, KD), lambda c: (0, chunk(c), j))
    in_specs = [col_spec(0), col_spec(1 if reverse else 0), col_spec(1),
                pl.BlockSpec((1, KD), lambda c: (0, 0)),
                pl.BlockSpec((B, H, DV, DK), lambda c: (0, 0, 0, 0))]
    p3 = p.reshape(B, L, p.shape[-1])
    args = [p3, pz.reshape(B, L, 2 * KD), p3, lb_row, s0]
    readout = o_f is not None
    if readout:
        in_specs += [col_spec(0), col_spec(2), pl.BlockSpec((1, DV), lambda c: (0, 0))]
        args += [o_f, p3, norm_g]
    return pl.pallas_call(
        functools.partial(_hgrn_kernel, reverse, readout),
        grid=(nch,),
        in_specs=in_specs,
        out_specs=col_spec(0),
        out_shape=jax.ShapeDtypeStruct((B, L, KD), F32),
        scratch_shapes=[pltpu.VMEM((B, H, DV, DK), F32)],
        compiler_params=_cparams(1, VMEM_LIMIT),
        name="hgrn_bwd_readout" if readout else "hgrn_fwd",
    )(*args)


def _hy_pre_kernel(v_ref, x1_ref, x0_ref, vp_ref, x1p_ref, x0p_ref, vn_ref, x1n_ref, x0n_ref,
                   w_ref, b_ref, vx_ref, x0o_ref):
    i = pl.program_id(1)
    first = i == 0
    last = i == pl.num_programs(1) - 1
    row = lax.broadcasted_iota(I32, (TM_HY, 1), 0)

    def conv(c_ref, p_ref, n_ref, col):
        x = c_ref[...].astype(F32)
        prev_row = jnp.where(first, 0.0, p_ref[...].astype(F32)[HALO - 1:HALO, :])
        next_row = jnp.where(last, 0.0, n_ref[...].astype(F32)[0:1, :])
        xm = jnp.where(row == 0, prev_row, pltpu.roll(x, 1, axis=0))
        xp = jnp.where(row == TM_HY - 1, next_row, pltpu.roll(x, TM_HY - 1, axis=0))
        cs = slice(col * HYW, (col + 1) * HYW)
        return xm * w_ref[0:1, cs] + x * w_ref[1:2, cs] + xp * w_ref[2:3, cs] + b_ref[:, cs]

    v = conv(v_ref, vp_ref, vn_ref, 0)
    x1 = conv(x1_ref, x1p_ref, x1n_ref, 1)
    x0 = conv(x0_ref, x0p_ref, x0n_ref, 2)
    vx_ref[...] = v * x1
    x0o_ref[...] = x0


def _hyena_pre(p, conv_w, conv_b):
    nt = L // TM_HY
    hb = TM_HY // HALO
    nhb = T // HALO
    cur = lambda col: pl.BlockSpec((TM_HY, HYW), lambda b, i: (b * nt + i, col))
    prv = lambda col: pl.BlockSpec((HALO, HYW), lambda b, i: (jnp.maximum((b * nt + i) * hb - 1, 0), col))
    nxt = lambda col: pl.BlockSpec((HALO, HYW), lambda b, i: (jnp.minimum((b * nt + i + 1) * hb, nhb - 1), col))
    c0 = 3
    out = jax.ShapeDtypeStruct((T, HYW), F32)
    return pl.pallas_call(
        _hy_pre_kernel,
        grid=(B, nt),
        in_specs=[cur(c0), cur(c0 + 1), cur(c0 + 2), prv(c0), prv(c0 + 1), prv(c0 + 2),
                  nxt(c0), nxt(c0 + 1), nxt(c0 + 2),
                  pl.BlockSpec((3, 3 * HYW), lambda b, i: (0, 0)),
                  pl.BlockSpec((1, 3 * HYW), lambda b, i: (0, 0))],
        out_specs=(pl.BlockSpec((TM_HY, HYW), lambda b, i: (b * nt + i, 0)),
                   pl.BlockSpec((TM_HY, HYW), lambda b, i: (b * nt + i, 0))),
        out_shape=(out, out),
        compiler_params=_cparams(2, VMEM_LIMIT),
        name="hyena_pre",
    )(p, p, p, p, p, p, p, p, p, conv_w, conv_b)


def _taps_dft_kernel(z_ref, w1_ref, b1_ref, f1_ref, w2_ref, b2_ref, f2_ref, w3a_ref, w3b_ref, dl_ref, fm_ref,
                     o_hbm, obuf, sem):
    g = pl.program_id(0)
    na = L // FFT_P
    half = z_ref.shape[0] // 2
    zt = z_ref[0:half, :]
    zb = z_ref[half:, :]
    lane = lax.broadcasted_iota(I32, zt.shape, 1)
    dot = functools.partial(jnp.dot, preferred_element_type=F32, precision=HIGHEST)
    h = jnp.sin(f1_ref[...] * (dot(jnp.concatenate([zt, zb], axis=1), w1_ref[...]) + b1_ref[...]))
    h = jnp.sin(f2_ref[...] * (dot(h, w2_ref[...]) + b2_ref[...]))

    def copies(step):
        return [pltpu.make_async_copy(obuf.at[n, jj], o_hbm.at[n, :, step * FFT_BB + jj, :], sem)
                for n in range(2) for jj in range(FFT_BB)]

    @pl.when(g > 0)
    def _():
        for cp in copies(g - 1):
            cp.wait()

    for hi, (zz, w3_ref) in enumerate(((zt, w3a_ref), (zb, w3b_ref))):
        taps = dot(h, w3_ref[...])
        win = jnp.exp(-_lane_pick(zz, lane, 0) * dl_ref[...])
        fwd = (taps[:, :HYW] * win).astype(BF16)
        bwd = (taps[:, HYW:] * win * _lane_pick(zz, lane, HY_EMB)).astype(BF16)
        for bb in range(FFT_BB // 2):
            jj = hi * (FFT_BB // 2) + bb
            rows = slice(bb * na, (bb + 1) * na)
            obuf[0, jj] = jnp.dot(fm_ref[...], fwd[rows], preferred_element_type=F32)
            obuf[1, jj] = jnp.dot(fm_ref[...], bwd[rows], preferred_element_type=F32)

    for cp in copies(g):
        cp.start()

    @pl.when(g == pl.num_programs(0) - 1)
    def _():
        for cp in copies(g):
            cp.wait()


def _filter_taps_dft(zin, w1, b1, f1, w2, b2, f2, w3a, w3b, deltas, fmat):
    na = L // FFT_P
    tm = FFT_BB * na
    ln = V7X_LANES
    full = lambda shape: pl.BlockSpec(shape, lambda i: (0, 0))
    return pl.pallas_call(
        _taps_dft_kernel,
        grid=(FFT_P // FFT_BB,),
        in_specs=[pl.BlockSpec((tm, ln), lambda i: (i, 0)),
                  full((2 * ln, ln)), full((1, ln)), full((1, ln)),
                  full((ln, ln)), full((1, ln)), full((1, ln)),
                  full((ln, 2 * HYW)), full((ln, 2 * HYW)), full((1, HYW)), full((FFT_P, na))],
        out_specs=pl.BlockSpec(memory_space=pl.ANY),
        out_shape=jax.ShapeDtypeStruct((2, FFT_P, FFT_P, HYW), F32),
        scratch_shapes=[pltpu.VMEM((2, FFT_BB, FFT_P, HYW), F32), pltpu.SemaphoreType.DMA(())],
        compiler_params=_cparams(1, VMEM_LIMIT),
        name="filter_taps_dft",
    )(zin, w1, b1, f1, w2, b2, f2, w3a, w3b, deltas, fmat)


def _strided_dft_kernel(x_hbm, f_ref, o_hbm, xbuf, obuf, sem_in, sem_out):
    g = pl.program_id(0)
    ng = pl.num_programs(0)
    nb = FFT_P // FFT_BB

    def copies(grp, slot, inbound):
        n = grp // nb
        b0 = (grp % nb) * FFT_BB
        if inbound:
            return [pltpu.make_async_copy(x_hbm.at[n, :, b0 + jj, :], xbuf.at[slot, jj], sem_in.at[slot])
                    for jj in range(FFT_BB)]
        return [pltpu.make_async_copy(obuf.at[slot, jj], o_hbm.at[n, :, b0 + jj, :], sem_out.at[slot])
                for jj in range(FFT_BB)]

    def start(grp, slot, inbound):
        for cp in copies(grp, slot, inbound):
            cp.start()

    def wait(grp, slot, inbound):
        for cp in copies(grp, slot, inbound):
            cp.wait()

    @pl.when(g == 0)
    def _():
        start(0, 0, True)

    for slot in range(2):
        grp = 2 * g + slot
        if slot == 0:
            start(grp + 1, 1, True)
        else:
            @pl.when(g + 1 < ng)
            def _():
                start(grp + 1, 0, True)
        wait(grp, slot, True)

        @pl.when(g > 0)
        def _():
            wait(grp - 2, slot, False)

        for jj in range(FFT_BB):
            obuf[slot, jj] = jnp.dot(f_ref[...], xbuf[slot, jj].astype(BF16), preferred_element_type=F32)
        start(grp, slot, False)

    @pl.when(g + 1 == ng)
    def _():
        wait(2 * g, 0, False)
        wait(2 * g + 1, 1, False)


def _strided_dft(xv, fmat, name):
    n, kk = xv.shape[0], xv.shape[1]
    mm = fmat.shape[0]
    groups = n * (FFT_P // FFT_BB)
    return pl.pallas_call(
        _strided_dft_kernel,
        grid=(groups // 2,),
        in_specs=[pl.BlockSpec(memory_space=pl.ANY),
                  pl.BlockSpec((mm, kk), lambda g: (0, 0))],
        out_specs=pl.BlockSpec(memory_space=pl.ANY),
        out_shape=jax.ShapeDtypeStruct((n, mm, FFT_P, HYW), F32),
        scratch_shapes=[pltpu.VMEM((2, FFT_BB, kk, HYW), F32), pltpu.VMEM((2, FFT_BB, mm, HYW), F32),
                        pltpu.SemaphoreType.DMA((2,)), pltpu.SemaphoreType.DMA((2,))],
        compiler_params=_cparams(1, VMEM_LIMIT),
        name=name,
    )(xv, fmat)


def _cblock(mr, mi):
    return jnp.concatenate([jnp.concatenate([mr, -mi], axis=1), jnp.concatenate([mi, mr], axis=1)], axis=0)


FFT_H = FFT_P // 2


def _twiddled(fr_ref, fi_ref, tw):
    twr = tw[0:1, :]
    twi = tw[1:2, :]
    fr = fr_ref[...]
    fi = fi_ref[...]
    return fr * twr - fi * twi, fr * twi + fi * twr


def _cmul_rows(x, kr, ki):
    xr, xi = x[:FFT_P], x[FFT_P:]
    return jnp.concatenate([xr * kr - xi * ki, xr * ki + xi * kr], axis=0).astype(BF16)


_TN_DIMS = (((0,), (0,)), ((), ()))
_STAGE2_SPECS = [pl.BlockSpec((FFT_P, FFT_P), lambda d: (0, 0)),
                 pl.BlockSpec((FFT_P, FFT_P), lambda d: (0, 0)),
                 pl.BlockSpec((None, 2, FFT_P), lambda d: (d, 0, 0)),
                 pl.BlockSpec((2, FFT_P), lambda d: (0, 0))]


def _kspec_kernel(a_ref, fr_ref, fi_ref, tw_ref, twh_ref, o_ref, oh_ref):
    dd = pl.program_id(0)
    dot = functools.partial(jnp.dot, preferred_element_type=F32)

    def combine(xf, xb, out_ref):
        out_ref[0] = xf[:FFT_P] + xb[:FFT_P]
        out_ref[1] = xf[FFT_P:] - xb[FFT_P:]

    @pl.when(dd > 0)
    def _():
        rm = _cblock(*_twiddled(fr_ref, fi_ref, tw_ref[...])).astype(BF16)
        combine(dot(rm, a_ref[0].reshape(2 * FFT_P, HYW).astype(BF16)),
                dot(rm, a_ref[1].reshape(2 * FFT_P, HYW).astype(BF16)), o_ref)

    @pl.when(dd == 0)
    def _():
        for slot, tw, out_ref in ((0, tw_ref[...], o_ref), (1, twh_ref[...], oh_ref)):
            w = jnp.concatenate(_twiddled(fr_ref, fi_ref, tw), axis=0).astype(BF16)
            combine(dot(w, a_ref[0, slot].astype(BF16)), dot(w, a_ref[1, slot].astype(BF16)), out_ref)


def _kernel_spectrum(ak, fr, fi, tw, twh):
    return pl.pallas_call(
        _kspec_kernel,
        grid=(FFT_H,),
        in_specs=[pl.BlockSpec((2, None, 2, FFT_P, HYW), lambda d: (0, d, 0, 0, 0))] + _STAGE2_SPECS,
        out_specs=(pl.BlockSpec((2, None, FFT_P, HYW), lambda d: (0, d, 0, 0)),
                   pl.BlockSpec((2, FFT_P, HYW), lambda d: (0, 0, 0))),
        out_shape=(jax.ShapeDtypeStruct((2, FFT_H, FFT_P, HYW), F32),
                   jax.ShapeDtypeStruct((2, FFT_P, HYW), F32)),
        compiler_params=_cparams(1, VMEM_LIMIT),
        name="kernel_spectrum",
    )(ak, fr, fi, tw, twh)


def _mid_kernel(a_ref, kh_ref, khh_ref, fr_ref, fi_ref, tw_ref, twh_ref, o_ref):
    dd = pl.program_id(0)
    dot = functools.partial(jnp.dot, preferred_element_type=F32)
    dot_t = lambda w, y: lax.dot_general(w, y, _TN_DIMS, preferred_element_type=F32)

    @pl.when(dd > 0)
    def _():
        rm = _cblock(*_twiddled(fr_ref, fi_ref, tw_ref[...])).astype(BF16)
        for n in range(B):
            x = dot(rm, a_ref[n].reshape(2 * FFT_P, HYW).astype(BF16))
            y = _cmul_rows(x, kh_ref[0], kh_ref[1])
            o_ref[n] = dot_t(rm, y).reshape(2, FFT_P, HYW)

    @pl.when(dd == 0)
    def _():
        for slot, tw, k_ref in ((0, tw_ref[...], kh_ref), (1, twh_ref[...], khh_ref)):
            w = jnp.concatenate(_twiddled(fr_ref, fi_ref, tw), axis=0).astype(BF16)
            for n in range(B):
                y = _cmul_rows(dot(w, a_ref[n, slot].astype(BF16)), k_ref[0], k_ref[1])
                o_ref[n, slot] = dot_t(w, y)


def _fft_mid(au, kh, khh, fr, fi, tw, twh):
    pair = pl.BlockSpec((B, None, 2, FFT_P, HYW), lambda d: (0, d, 0, 0, 0))
    return pl.pallas_call(
        _mid_kernel,
        grid=(FFT_H,),
        in_specs=[pair,
                  pl.BlockSpec((2, None, FFT_P, HYW), lambda d: (0, d, 0, 0)),
                  pl.BlockSpec((2, FFT_P, HYW), lambda d: (0, 0, 0))] + _STAGE2_SPECS,
        out_specs=pair,
        out_shape=jax.ShapeDtypeStruct((B, FFT_H, 2, FFT_P, HYW), F32),
        compiler_params=_cparams(1, VMEM_LIMIT),
        name="fft_mid",
    )(au, kh, khh, fr, fi, tw, twh)


def _dft_tables():
    na = L // FFT_P
    a = np.arange(na)
    dd = np.arange(FFT_H)
    ang = 2.0 * np.pi * np.outer(dd, a) / FFT_P
    re_rows = np.cos(ang)
    im_rows = -np.sin(ang)
    im_rows[0] = np.cos(np.pi * a)
    f_first = np.stack([re_rows, im_rows], axis=1).reshape(FFT_P, na)
    gre = 2.0 * np.cos(ang)
    gim = -2.0 * np.sin(ang)
    gre[0] = 1.0
    gim[0] = np.cos(np.pi * a)
    g_last = np.stack([gre, gim], axis=1).reshape(FFT_P, na).T / FFT_N
    b = np.arange(FFT_P)
    angf = 2.0 * np.pi * np.outer(b, b) / FFT_P
    ang2 = 2.0 * np.pi * np.outer(np.arange(FFT_H + 1), b) / FFT_N
    tw = np.stack([np.cos(ang2), -np.sin(ang2)], axis=1)
    f32 = lambda x: jnp.asarray(x.astype(np.float32))
    return (f32(f_first).astype(BF16), f32(g_last).astype(BF16), f32(np.cos(angf)), f32(-np.sin(angf)),
            f32(tw[:FFT_H]), f32(tw[FFT_H]))


def _merge_kernel(ya_ref, cv_ref, vx_ref, x0_ref, ga_ref, gb_ref, x_ref, er_ref, ec_ref, gt1_ref, hyd_ref,
                  wpa_ref, wpb_ref, wo_ref, g2_ref, sh2_ref, sc2_ref, wr_ref, br_ref,
                  x1_ref, t2_ref, lg_ref):
    vx = vx_ref[...]
    yb = x0_ref[...] * (cv_ref[...] + vx * hyd_ref[...])
    pa = jnp.dot(ya_ref[...].astype(BF16), wpa_ref[...], preferred_element_type=F32)
    pb = jnp.dot(yb.astype(BF16), wpb_ref[...], preferred_element_type=F32)
    mixed = (jax.nn.sigmoid(ga_ref[...].astype(F32)) * pa
             + jax.nn.sigmoid(gb_ref[...].astype(F32)) * pb)
    xm = jnp.dot(mixed.astype(BF16), wo_ref[...], preferred_element_type=F32)
    nrow = TM_MG // GRID_W
    row0 = (pl.program_id(0) % (L // TM_MG)) * nrow
    x1 = x_ref[...] + _pos_tile(er_ref, ec_ref, row0, nrow) + gt1_ref[...] * xm
    x1_ref[...] = x1
    t2 = _rms(x1) * g2_ref[...]
    t2 = t2 * (1.0 + sc2_ref[...]) + sh2_ref[...]
    t2_ref[...] = t2
    t_hi = t2.astype(BF16)
    t_lo = (t2 - t_hi.astype(F32)).astype(BF16)
    rr = (jnp.dot(t_hi, wr_ref[...], preferred_element_type=F32)
          + jnp.dot(t_lo, wr_ref[...], preferred_element_type=F32))
    lg_ref[...] = rr[:, :V7X_LANES] + rr[:, V7X_LANES:] + br_ref[...]


def _merge(ya, cv, vx, x0c, p, x2, er, ec, gt1, hyd, wpa, wpb, wo, g2, sh2, sc2, wr, br):
    tpb = L // TM_MG
    half = lambda: pl.BlockSpec((TM_MG, HYW), lambda i: (i, 0))
    full = lambda shape: pl.BlockSpec(shape, lambda i: tuple(0 for _ in shape))
    perb = lambda: pl.BlockSpec((None, 1, D), lambda i: (i // tpb, 0, 0))
    return pl.pallas_call(
        _merge_kernel,
        grid=(T // TM_MG,),
        in_specs=[half(), half(), half(), half(),
                  pl.BlockSpec((TM_MG, D), lambda i: (i, 3)),
                  pl.BlockSpec((TM_MG, D), lambda i: (i, 4)),
                  pl.BlockSpec((TM_MG, D), lambda i: (i, 0)),
                  full((L // GRID_W, D // 2)), full((GRID_W, D // 2)),
                  perb(), full((1, HYW)),
                  full((KD, D)), full((HYW, D)), full((D, D)),
                  full((1, D)), perb(), perb(),
                  full((D, 2 * V7X_LANES)), full((1, V7X_LANES))],
        out_specs=(pl.BlockSpec((TM_MG, D), lambda i: (i, 0)),
                   pl.BlockSpec((TM_MG, D), lambda i: (i, 0)),
                   pl.BlockSpec((TM_MG, V7X_LANES), lambda i: (i, 0))),
        out_shape=(jax.ShapeDtypeStruct((T, D), F32), jax.ShapeDtypeStruct((T, D), F32),
                   jax.ShapeDtypeStruct((T, V7X_LANES), F32)),
        compiler_params=_cparams(1, VMEM_LIMIT),
        name="merge",
    )(ya, cv, vx, x0c, p, p, x2, er, ec, gt1, hyd, wpa, wpb, wo, g2, sh2, sc2, wr, br)


def _route_kernel(lg_ref, info_ref, cnt_ref):
    @pl.when(pl.program_id(0) == 0)
    def _():
        cnt_ref[...] = jnp.zeros_like(cnt_ref)

    lg = lg_ref[...]
    lane = lax.broadcasted_iota(I32, lg.shape, 1)
    lanef = lane.astype(F32)
    neg = -1e30
    big = 1e9
    is_g = (lane >= NEXP) & (lane < NEXP + NGRP)
    gl = jnp.where(is_g, lg, neg)
    ge = jnp.where(is_g, jnp.exp(gl - jnp.max(gl, axis=-1, keepdims=True)), 0.0)
    pg = ge / jnp.sum(ge, axis=-1, keepdims=True)
    p_top_g = jnp.max(pg, axis=-1, keepdims=True)
    gidx = jnp.min(jnp.where(is_g & (pg == p_top_g), lanef, big), axis=-1, keepdims=True)
    g_sel = gidx.astype(I32) - NEXP
    emask = (lane < NEXP) & (jnp.right_shift(lane, NEPG.bit_length() - 1) == g_sel)
    el = jnp.where(emask, lg, neg)
    ee = jnp.where(emask, jnp.exp(el - jnp.max(el, axis=-1, keepdims=True)), 0.0)
    pe = ee / jnp.sum(ee, axis=-1, keepdims=True)
    p1 = jnp.max(jnp.where(emask, pe, -1.0), axis=-1, keepdims=True)
    i1 = jnp.min(jnp.where(emask & (pe == p1), lanef, big), axis=-1, keepdims=True)
    rest = emask & (lanef != i1)
    p2 = jnp.max(jnp.where(rest, pe, -1.0), axis=-1, keepdims=True)
    i2 = jnp.min(jnp.where(rest & (pe == p2), lanef, big), axis=-1, keepdims=True)
    wsum = p1 + p2
    w1 = p_top_g * p1 / wsum
    w2 = p_top_g * p2 / wsum
    sel1 = lanef == i1
    sel2 = lanef == i2
    oh = jnp.where(sel1 | sel2, 1.0, 0.0)
    r = lax.broadcasted_iota(I32, (TR, TR), 0)
    c = lax.broadcasted_iota(I32, (TR, TR), 1)
    stril = jnp.where(c < r, 1.0, 0.0).astype(BF16)
    before = jnp.dot(stril, oh.astype(BF16), preferred_element_type=F32) + cnt_ref[...]
    r1 = jnp.sum(jnp.where(sel1, before, 0.0), axis=-1, keepdims=True)
    r2 = jnp.sum(jnp.where(sel2, before, 0.0), axis=-1, keepdims=True)
    cnt_ref[...] += jnp.sum(oh, axis=0, keepdims=True)
    info = jnp.where(lane == 0, i1, jnp.where(lane == 1, r1, jnp.where(lane == 2, i2, jnp.where(
        lane == 3, r2, jnp.where(lane == 4, w1, jnp.where(lane == 5, w2, 0.0))))))
    info_ref[...] = info


def _route(lg):
    return pl.pallas_call(
        _route_kernel,
        grid=(T // TR,),
        in_specs=[pl.BlockSpec((TR, V7X_LANES), lambda i: (i, 0))],
        out_specs=(pl.BlockSpec((TR, V7X_LANES), lambda i: (i, 0)),
                   pl.BlockSpec((1, V7X_LANES), lambda i: (0, 0))),
        out_shape=(jax.ShapeDtypeStruct((T, V7X_LANES), F32), jax.ShapeDtypeStruct((1, V7X_LANES), F32)),
        compiler_params=_cparams(1, VMEM_LIMIT),
        name="route",
    )(lg)


def _positions_kernel(info_ref, st_ref, o_ref):
    info = info_ref[...]
    lane = lax.broadcasted_iota(I32, info.shape, 1)
    lanef = lane.astype(F32)
    st = st_ref[...]
    row = lambda e_lane, r_lane: (jnp.sum(jnp.where(lanef == _lane_pick(info, lane, e_lane), st, 0.0),
                                          axis=-1, keepdims=True) + _lane_pick(info, lane, r_lane))
    o_ref[...] = jnp.where(lane == 0, row(0, 1), jnp.where(lane == 1, row(2, 3), 0.0)).astype(I32)


def _positions(info, starts_row):
    return pl.pallas_call(
        _positions_kernel,
        grid=(T // TPOS,),
        in_specs=[pl.BlockSpec((TPOS, V7X_LANES), lambda i: (i, 0)),
                  pl.BlockSpec((1, V7X_LANES), lambda i: (0, 0))],
        out_specs=pl.BlockSpec((TPOS, V7X_LANES), lambda i: (i, 0)),
        out_shape=jax.ShapeDtypeStruct((T, V7X_LANES), I32),
        compiler_params=_cparams(1, VMEM_LIMIT),
        name="positions",
    )(info, starts_row)


def _scatter_kernel(pos_ref, zt_ref, t2_ref, zeros_hbm, xs_hbm, sem, zsem):
    i = pl.program_id(0)

    def zcopy(row):
        start = pl.multiple_of(jnp.maximum(row, 0), TE)
        return pltpu.make_async_copy(zeros_hbm, xs_hbm.at[pl.ds(start, TE)], zsem)

    @pl.when(i == 0)
    def _():
        def ztail(start, e, carry):
            @pl.when(zt_ref[0, e] >= 0)
            def _():
                cp = zcopy(zt_ref[0, e])
                cp.start() if start else cp.wait()
            return carry

        lax.fori_loop(0, NEXP, functools.partial(ztail, True), 0)
        lax.fori_loop(0, NEXP, functools.partial(ztail, False), 0)

        def zrest(start, tile, carry):
            cp = zcopy(tile * TE)
            cp.start() if start else cp.wait()
            return carry

        lax.fori_loop(zt_ref[0, NA_OFF], NT_EXP, functools.partial(zrest, True), 0)
        lax.fori_loop(zt_ref[0, NA_OFF], NT_EXP, functools.partial(zrest, False), 0)

    def row_copy(base, jj, kk):
        dst = pos_ref[0, 2 * (base + jj) + kk]
        return pltpu.make_async_copy(t2_ref.at[pl.ds(base, ROW_GROUP)].at[pl.ds(jj, 1)],
                                     xs_hbm.at[pl.ds(dst, 1)], sem)

    _start_rows(TS, row_copy)
    _wait_rows(TS, pltpu.make_async_copy(t2_ref.at[pl.ds(0, 1)], xs_hbm.at[pl.ds(0, 1)], sem))


def _start_rows(n_rows, row_copy):
    def group(g, carry):
        base = pl.multiple_of(g * ROW_GROUP, ROW_GROUP)
        for jj in range(ROW_GROUP):
            for kk in range(2):
                row_copy(base, jj, kk).start(priority=kk)
        return carry

    lax.fori_loop(0, n_rows // ROW_GROUP, group, 0)


def _wait_rows(n_rows, one_row_copy):
    def drain(j, carry):
        one_row_copy.wait()
        one_row_copy.wait()
        return carry

    lax.fori_loop(0, n_rows, drain, 0, unroll=ROW_UNROLL)


def _scatter_rows(pos3, meta, t2, zeros_tile):
    return pl.pallas_call(
        _scatter_kernel,
        grid=(T // TS,),
        in_specs=[pl.BlockSpec((None, 1, 2 * TS), lambda i: (i, 0, 0), memory_space=pltpu.SMEM),
                  pl.BlockSpec(memory_space=pltpu.SMEM),
                  pl.BlockSpec((TS, D), lambda i: (i, 0)),
                  pl.BlockSpec(memory_space=pl.ANY)],
        out_specs=pl.BlockSpec(memory_space=pl.ANY),
        out_shape=jax.ShapeDtypeStruct((NP_ROWS, D), F32),
        scratch_shapes=[pltpu.SemaphoreType.DMA(()), pltpu.SemaphoreType.DMA(())],
        compiler_params=_cparams(1, VMEM_LIMIT),
        name="scatter_rows",
    )(pos3, meta, t2, zeros_tile)


def _expert_kernel(te_ref, nx_ref, sl_ref, na_ref, xs_ref, wg_hbm, wu_hbm, wd_hbm, ys_ref,
                   wgs_ref, wus_ref, wds_ref, wgb_ref, wub_ref, wdb_ref, sem):
    i = pl.program_id(0)
    active = i < na_ref[0]
    first = active & ((i == 0) | (te_ref[i] != te_ref[jnp.maximum(i - 1, 0)]))

    def fetch(e, slot):
        return [pltpu.make_async_copy(wg_hbm.at[e], wgs_ref.at[slot], sem.at[slot]),
                pltpu.make_async_copy(wu_hbm.at[e], wus_ref.at[slot], sem.at[slot]),
                pltpu.make_async_copy(wd_hbm.at[e], wds_ref.at[slot], sem.at[slot])]

    for slot in range(2):
        @pl.when(first & (sl_ref[i] == slot))
        def _():
            @pl.when(i == 0)
            def _():
                for cp in fetch(te_ref[i], slot):
                    cp.start()

            for cp in fetch(te_ref[i], slot):
                cp.wait()
            wgb_ref[...] = wgs_ref[slot].astype(BF16)
            wub_ref[...] = wus_ref[slot].astype(BF16)
            wdb_ref[...] = wds_ref[slot].astype(BF16)

            @pl.when(nx_ref[i] >= 0)
            def _():
                for cp in fetch(nx_ref[i], 1 - slot):
                    cp.start()

    @pl.when(active)
    def _():
        x = xs_ref[...].astype(BF16)
        g = jnp.dot(x, wgb_ref[...], preferred_element_type=F32)
        u = jnp.dot(x, wub_ref[...], preferred_element_type=F32)
        hid = (g * jax.nn.sigmoid(g) * u).astype(BF16)
        ys_ref[...] = jnp.dot(hid, wdb_ref[...], preferred_element_type=F32)

    @pl.when(jnp.logical_not(active))
    def _():
        ys_ref[...] = jnp.zeros_like(ys_ref)


def _experts(tile_expert, tile_next, tile_slot, n_active, xs, wg, wu, wd):
    rows = lambda i, *_: (i, 0)
    rows_in = lambda i, te, nx, sl, na: (jnp.minimum(i, na[0] - 1), 0)
    any_space = pl.BlockSpec(memory_space=pl.ANY)
    grid_spec = pltpu.PrefetchScalarGridSpec(
        num_scalar_prefetch=4,
        grid=(NT_EXP,),
        in_specs=[pl.BlockSpec((TE, D), rows_in), any_space, any_space, any_space],
        out_specs=pl.BlockSpec((TE, D), rows),
        scratch_shapes=[pltpu.VMEM((2, D, DEXP), F32), pltpu.VMEM((2, D, DEXP), F32), pltpu.VMEM((2, DEXP, D), F32),
                        pltpu.VMEM((D, DEXP), BF16), pltpu.VMEM((D, DEXP), BF16), pltpu.VMEM((DEXP, D), BF16),
                        pltpu.SemaphoreType.DMA((2,))],
    )
    return pl.pallas_call(
        _expert_kernel,
        grid_spec=grid_spec,
        out_shape=jax.ShapeDtypeStruct((NP_ROWS, D), F32),
        compiler_params=_cparams(1, VMEM_LIMIT),
        name="experts",
    )(tile_expert, tile_next, tile_slot, n_active, xs, wg, wu, wd)


def _combine_kernel(pos_ref, posn_ref, info_ref, x1_ref, gt2_ref, fg_ref, ys_hbm, o_ref, buf, sem):
    g = pl.program_id(0)
    ng = pl.num_programs(0)

    def gather(p_ref, half, slot):
        def row_copy(base, jj, kk):
            src = p_ref[0, 2 * (half * TC + base + jj) + kk]
            return pltpu.make_async_copy(ys_hbm.at[pl.ds(src, 1)],
                                         buf.at[slot, kk].at[pl.ds(base, ROW_GROUP)].at[pl.ds(jj, 1)],
                                         sem.at[slot])
        _start_rows(TC, row_copy)

    def finish(half, slot):
        _wait_rows(TC, pltpu.make_async_copy(ys_hbm.at[pl.ds(0, 1)], buf.at[slot, 0].at[pl.ds(0, 1)],
                                             sem.at[slot]))
        rows = slice(half * TC, (half + 1) * TC)
        info = info_ref[rows, :]
        lane = lax.broadcasted_iota(I32, info.shape, 1)
        moe = _lane_pick(info, lane, 4) * buf[slot, 0] + _lane_pick(info, lane, 5) * buf[slot, 1]
        x2 = x1_ref[rows, :] + gt2_ref[...] * moe
        o_ref[rows, :] = _rms(x2) * fg_ref[...]

    @pl.when(g == 0)
    def _():
        gather(pos_ref, 0, 0)

    gather(pos_ref, 1, 1)
    finish(0, 0)

    @pl.when(g + 1 < ng)
    def _():
        gather(posn_ref, 0, 0)

    finish(1, 1)


def _combine(pos3, info, x1, gt2, fg, ys):
    step = 2 * TC
    tpb = L // step
    nsteps = T // step
    return pl.pallas_call(
        _combine_kernel,
        grid=(nsteps,),
        in_specs=[pl.BlockSpec((None, 1, 2 * step), lambda i: (i, 0, 0), memory_space=pltpu.SMEM),
                  pl.BlockSpec((None, 1, 2 * step), lambda i: (jnp.minimum(i + 1, nsteps - 1), 0, 0),
                               memory_space=pltpu.SMEM),
                  pl.BlockSpec((step, V7X_LANES), lambda i: (i, 0)),
                  pl.BlockSpec((step, D), lambda i: (i, 0)),
                  pl.BlockSpec((None, 1, D), lambda i: (i // tpb, 0, 0)),
                  pl.BlockSpec((1, D), lambda i: (0, 0)),
                  pl.BlockSpec(memory_space=pl.ANY)],
        out_specs=pl.BlockSpec((step, D), lambda i: (i, 0)),
        out_shape=jax.ShapeDtypeStruct((T, D), F32),
        scratch_shapes=[pltpu.VMEM((2, 2, TC, D), F32), pltpu.SemaphoreType.DMA((2,))],
        compiler_params=_cparams(1, VMEM_LIMIT),
        name="combine",
    )(pos3, pos3, info, x1, gt2, fg, ys)


def _pos_tables():
    rows = L // GRID_W
    quarter = D // 4
    omega = 1.0 / (10000.0 ** (jnp.arange(quarter, dtype=F32) / quarter))

    def axis_emb(pos):
        a = pos[:, None] * omega[None, :]
        return jnp.concatenate([jnp.sin(a), jnp.cos(a)], axis=-1)

    er = axis_emb(jnp.arange(rows, dtype=F32))
    ec = axis_emb(jnp.arange(GRID_W, dtype=F32))
    return er, ec


def _filter_features():
    z = np.zeros((L, V7X_LANES), np.float64)
    bands = (HY_EMB - 1) // 2
    ang = (2.0 * np.pi * np.arange(L) / L)[:, None] * np.linspace(1e-4, bands - 1, bands)[None, :]
    z[:, 0] = np.linspace(0.0, 1.0, L)
    z[:, 1:1 + bands] = np.cos(ang)
    z[:, 1 + bands:HY_EMB] = -np.sin(ang)
    z[1:, HY_EMB] = 1.0
    z = z.reshape(L // FFT_P, FFT_P, V7X_LANES).transpose(1, 0, 2).reshape(L, V7X_LANES)
    return jnp.asarray(z.astype(np.float32))


def _pad2(a, rows, cols):
    return jnp.pad(a, ((0, rows - a.shape[0]), (0, cols - a.shape[1])))


def kernel(x, c, ctx, c_ctx, ada_w, ada_b, norm1_g, norm2_g, w_in, hgrn_lb, hgrn_norm_g, hy_conv_w, hy_conv_b, hy_filt_w1, hy_filt_b1, hy_filt_freq1, hy_filt_w2, hy_filt_b2, hy_filt_freq2, hy_filt_w3, hy_d, w_proj_a, w_proj_b, w_out, moe_router_g_w, moe_router_g_b, moe_router_e_w, moe_router_e_b, moe_w_gate, moe_w_up, moe_w_down, final_norm_g):
    cvec = jnp.zeros((8, D), F32).at[0:B].set(c).at[B].set(c_ctx)
    mod = _adaln(cvec, ada_w[0], ada_b[0][None, :])
    m6 = mod.reshape(8, 6, D)
    sh1, sc1, gt1, sh2, sc2, gt2 = [m6[0:B, k][:, None, :] for k in range(6)]
    csh1, csc1 = m6[B:B + 1, 0], m6[B:B + 1, 1]

    lbs = jnp.cumsum(jax.nn.softmax(hgrn_lb.astype(F32), axis=0), axis=0)[0]
    g1 = norm1_g[0][None, :]
    er, ec = _pos_tables()
    x2 = x.reshape(T, D)

    w_ctx = w_in[0][:, KD:4 * KD].astype(BF16)
    s_f, s_b = _context_states(ctx, g1, csh1, csc1, w_ctx, lbs)

    pz, p = _in_projection(x2, er, ec, g1, sh1, sc1, w_in[0].astype(BF16))

    o_f = _hgrn_scan(pz, p, lbs[0:1], s_f, False)
    y_a = _hgrn_scan(pz, p, lbs[1:2], s_b, True, o_f=o_f, norm_g=hgrn_norm_g[0][None, :])

    vx, x0c = _hyena_pre(p, hy_conv_w[0], hy_conv_b[0][None, :])
    deltas = jnp.abs(jnp.linspace(math.log(HY_DECAY_TARGET) / HY_SLOW_PCT,
                                  math.log(HY_DECAY_TARGET) / HY_FAST_PCT, HYW, dtype=F32))[None, :]
    ln = V7X_LANES
    fh = hy_filt_w2.shape[-1]
    blockdiag = lambda m: jnp.concatenate([_pad2(m, m.shape[0], 2 * m.shape[1]),
                                           jnp.pad(m, ((0, 0), (m.shape[1], 0)))], axis=0)
    twice = lambda v: jnp.concatenate([v, v])[None, :]
    w3 = hy_filt_w3[0]
    f_first, g_last, fr, fi, tw, twh = _dft_tables()
    na = L // FFT_P
    ak = _filter_taps_dft(
        _filter_features(),
        blockdiag(_pad2(hy_filt_w1[0], ln, fh)), twice(hy_filt_b1[0]), twice(hy_filt_freq1[0]),
        blockdiag(hy_filt_w2[0]), twice(hy_filt_b2[0]), twice(hy_filt_freq2[0]),
        _pad2(w3, ln, 2 * HYW), jnp.pad(w3, ((fh, 0), (0, 0))), deltas, f_first)
    kh, khh = _kernel_spectrum(ak.reshape(2, FFT_H, 2, FFT_P, HYW), fr, fi, tw, twh)
    au = _strided_dft(vx.reshape(B, na, FFT_P, HYW), f_first, "dft_first")
    bp = _fft_mid(au.reshape(B, FFT_H, 2, FFT_P, HYW), kh, khh, fr, fi, tw, twh)
    conv = _strided_dft(bp.reshape(B, FFT_P, FFT_P, HYW), g_last, "dft_last").reshape(T, HYW)

    wr = jnp.concatenate([jnp.transpose(moe_router_e_w[0], (1, 0, 2)).reshape(D, NEXP),
                          moe_router_g_w[0], jnp.zeros((D, V7X_LANES - NEXP - NGRP), F32)], axis=1)
    wr_hi = wr.astype(BF16)
    wr = jnp.concatenate([wr_hi, (wr - wr_hi.astype(F32)).astype(BF16)], axis=1)
    br = jnp.concatenate([moe_router_e_b[0].reshape(NEXP), moe_router_g_b[0],
                          jnp.zeros((V7X_LANES - NEXP - NGRP,), F32)])[None, :]
    x1, t2, lg = _merge(y_a.reshape(T, KD), conv, vx, x0c, p, x2, er, ec, gt1, hy_d[0][None, :],
                        w_proj_a[0].astype(BF16), w_proj_b[0].astype(BF16), w_out[0].astype(BF16),
                        norm2_g[0][None, :], sh2, sc2, wr, br)

    info, counts = _route(lg)
    cnt = counts[0, :NEXP].astype(I32)
    pc = ((cnt + TE - 1) // TE) * TE
    ends = jnp.cumsum(pc)
    starts = ends - pc
    n_active = (ends[-1] // TE).astype(I32)[None]
    tile_rows = jnp.arange(NT_EXP, dtype=I32) * TE
    tile_expert = jnp.minimum(jnp.sum((ends[None, :] <= tile_rows[:, None]).astype(I32), axis=1), NEXP - 1)
    meta = jnp.concatenate([jnp.where(pc > 0, ends - TE, -1), starts, n_active]).astype(I32)[None, :]
    starts_row = jnp.pad(starts.astype(F32), (0, V7X_LANES - NEXP))[None, :]
    pos3 = _positions(info, starts_row)[:, :2].reshape(T // TS, 1, 2 * TS)

    xs = _scatter_rows(pos3, meta, t2, jnp.zeros((TE, D), F32))
    eid = jnp.arange(NEXP, dtype=I32)
    nonempty = pc > 0
    later = jnp.where(nonempty[None, :] & (eid[None, :] > eid[:, None]), eid[None, :], NEXP)
    next_e = jnp.min(later, axis=1)
    next_e = jnp.where(next_e < NEXP, next_e, -1).astype(I32)
    slot_e = ((jnp.cumsum(nonempty.astype(I32)) - 1) % 2).astype(I32)
    ys = _experts(tile_expert, next_e[tile_expert], slot_e[tile_expert], n_active, xs,
                  moe_w_gate[0].reshape(NEXP, D, DEXP), moe_w_up[0].reshape(NEXP, D, DEXP),
                  moe_w_down[0].reshape(NEXP, DEXP, D))
    out = _combine(pos3, info, x1, gt2, final_norm_g[None, :], ys)
    return out.reshape(B, L, D)
```

```python
import functools
import math

import numpy as np
import jax
import jax.numpy as jnp
from jax import lax
from jax.experimental import pallas as pl
from jax.experimental.pallas import tpu as pltpu

F32 = jnp.float32
BF16 = jnp.bfloat16
I32 = jnp.int32

D = 1024
B = 2
L = 8192
T = B * L
CTX = 256
GRID_W = 64
EPS = 1e-6
H = 4
DK = 128
DV = 128
KD = H * DK
IN_W = 6144
HYW = 512
HY_EMB = 33
NGRP = 4
NEPG = 8
NEXP = NGRP * NEPG
DEXP = 512
HY_DECAY_TARGET = 1e-2
HY_FAST_PCT = 0.3
HY_SLOW_PCT = 1.5

V7X_LANES = 128
V7X_SUBLANES = 8
V7X_VMEM_BYTES = 64 * 1024 * 1024
VMEM_LIMIT = (3 * V7X_VMEM_BYTES) // 4

FFT_N = 2 * L
FFT_P = 128
FFT_BB = 8

TM_IN = 1024
TN_IN = 1024
TH = 128
CB = 32
TM_HY = 1024
HALO = 2 * V7X_SUBLANES
TM_MG = 512
TR = 512
TE = 512
TPOS = 2048
NP_ROWS = 2 * T + NEXP * TE
NT_EXP = NP_ROWS // TE
TS = 512
TC = TS // 2
ROW_UNROLL = 8
ROW_GROUP = 32
ST_OFF = NEXP
NA_OFF = 2 * NEXP


def _cparams(n_axes, vmem=None):
    return pltpu.CompilerParams(dimension_semantics=("arbitrary",) * n_axes,
                                vmem_limit_bytes=vmem)


def _split3(x):
    hi = x.astype(BF16)
    r = x - hi.astype(F32)
    mid = r.astype(BF16)
    lo = (r - mid.astype(F32)).astype(BF16)
    return hi, mid, lo


def _dot01(m, x):
    hi, mid, lo = _split3(x)
    return (jnp.dot(m, hi, preferred_element_type=F32) + jnp.dot(m, mid, preferred_element_type=F32)
            + jnp.dot(m, lo, preferred_element_type=F32))


def _dot_split(a, w_ref):
    a_hi = a.astype(BF16)
    a_lo = (a - a_hi.astype(F32)).astype(BF16)
    return (jnp.dot(a_hi, w_ref[0], preferred_element_type=F32) + jnp.dot(a_hi, w_ref[1], preferred_element_type=F32)
            + jnp.dot(a_lo, w_ref[0], preferred_element_type=F32))


def _hi_lo(w):
    hi = w.astype(BF16)
    return jnp.stack([hi, (w - hi.astype(F32)).astype(BF16)])


def _rms(x):
    return x * lax.rsqrt(jnp.mean(x * x, axis=-1, keepdims=True) + EPS)


def _lane_pick(x, lane, idx):
    return jnp.sum(jnp.where(lane == idx, x, 0.0), axis=-1, keepdims=True)


def _ada_kernel(c_ref, w_ref, b_ref, o_ref):
    c = c_ref[...]
    s = c * jax.nn.sigmoid(c)
    w = w_ref[...]
    w_hi = w.astype(BF16)
    w_lo = (w - w_hi.astype(F32)).astype(BF16)
    s_hi = s.astype(BF16)
    s_lo = (s - s_hi.astype(F32)).astype(BF16)
    o_ref[...] = (jnp.dot(s_hi, w_hi, preferred_element_type=F32) + jnp.dot(s_hi, w_lo, preferred_element_type=F32)
                  + jnp.dot(s_lo, w_hi, preferred_element_type=F32) + b_ref[...])


def _adaln(cvec, w, b):
    tn = 1536
    return pl.pallas_call(
        _ada_kernel,
        grid=(6 * D // tn,),
        in_specs=[pl.BlockSpec((8, D), lambda j: (0, 0)),
                  pl.BlockSpec((D, tn), lambda j: (0, j)),
                  pl.BlockSpec((1, tn), lambda j: (0, j))],
        out_specs=pl.BlockSpec((8, tn), lambda j: (0, j)),
        out_shape=jax.ShapeDtypeStruct((8, 6 * D), F32),
        compiler_params=_cparams(1, VMEM_LIMIT),
        name="adaln",
    )(cvec, w, b)


def _keys(z, lb):
    sig = jax.nn.sigmoid(z)
    logf = jnp.log(lb + (1.0 - lb) * sig)
    k = (1.0 - lb) * jax.nn.sigmoid(-z)
    return k, logf


def _ctx_kernel(ctx_ref, g_ref, sh_ref, sc_ref, w_ref, lb_ref, sf_ref, sb_ref):
    h = _rms(ctx_ref[...]) * g_ref[...]
    h = h * (1.0 + sc_ref[...]) + sh_ref[...]
    p = jnp.dot(h.astype(BF16), w_ref[...], preferred_element_type=F32)
    zf, zb, v = p[:, :KD], p[:, KD:2 * KD], p[:, 2 * KD:]
    kf, lf = _keys(zf, lb_ref[0:1, :])
    kb, lbk = _keys(zb, lb_ref[1:2, :])
    r = lax.broadcasted_iota(I32, (CTX, CTX), 0)
    c = lax.broadcasted_iota(I32, (CTX, CTX), 1)
    tril = jnp.where(c <= r, 1.0, 0.0).astype(BF16)
    cf = _dot01(tril, lf)
    cb = _dot01(tril, lbk)
    kfd = (kf * jnp.exp(cf[CTX - 1:CTX, :] - cf)).astype(BF16)
    kbd = (kb * jnp.exp(cb - lbk)).astype(BF16)
    vb = v.astype(BF16)
    tn = (((0,), (0,)), ((), ()))
    for hh in range(H):
        hs = slice(hh * DK, (hh + 1) * DK)
        sf_ref[hh] = lax.dot_general(vb[:, hs], kfd[:, hs], tn, preferred_element_type=F32)
        sb_ref[hh] = lax.dot_general(vb[:, hs], kbd[:, hs], tn, preferred_element_type=F32)


def _context_states(ctx, g1, csh1, csc1, w_ctx, lbs):
    st = jax.ShapeDtypeStruct((B, H, DV, DK), F32)
    return pl.pallas_call(
        _ctx_kernel,
        grid=(B,),
        in_specs=[pl.BlockSpec((None, CTX, D), lambda b: (b, 0, 0)),
                  pl.BlockSpec((1, D), lambda b: (0, 0)),
                  pl.BlockSpec((1, D), lambda b: (0, 0)),
                  pl.BlockSpec((1, D), lambda b: (0, 0)),
                  pl.BlockSpec((D, 3 * KD), lambda b: (0, 0)),
                  pl.BlockSpec((2, KD), lambda b: (0, 0))],
        out_specs=(pl.BlockSpec((None, H, DV, DK), lambda b: (b, 0, 0, 0)),
                   pl.BlockSpec((None, H, DV, DK), lambda b: (b, 0, 0, 0))),
        out_shape=(st, st),
        compiler_params=_cparams(1, VMEM_LIMIT),
        name="ctx_states",
    )(ctx, g1, csh1, csc1, w_ctx, lbs)


def _pos_tile(er_ref, ec_ref, row0, nrow):
    lo = jnp.concatenate([jnp.broadcast_to(er_ref[pl.ds(row0 + i, 1), :], (GRID_W, D // 2))
                          for i in range(nrow)], axis=0)
    hi = jnp.concatenate([ec_ref[...]] * nrow, axis=0)
    return jnp.concatenate([lo, hi], axis=1)


def _inproj_kernel(x_ref, er_ref, ec_ref, g_ref, sh_ref, sc_ref, w_ref, oz_ref, o_ref, hx_ref):
    i = pl.program_id(0)
    j = pl.program_id(1)
    ni = pl.num_programs(0)
    nj = pl.num_programs(1)
    slot = i % 2

    def prepare(tile, dst):
        nrow = TM_IN // GRID_W
        row0 = (tile % (L // TM_IN)) * nrow
        h = _rms(x_ref[...] + _pos_tile(er_ref, ec_ref, row0, nrow)) * g_ref[...]
        hx_ref[dst] = (h * (1.0 + sc_ref[...]) + sh_ref[...]).astype(BF16)

    def project():
        return jnp.dot(hx_ref[slot], w_ref[...], preferred_element_type=F32)

    @pl.when((i == 0) & (j == 0))
    def _():
        prepare(0, 0)

    @pl.when(j == 0)
    def _():
        r = project()
        o_ref[:, :KD] = r[:, :KD].astype(BF16)
        oz_ref[:, :KD] = r[:, KD:]

    @pl.when(j == 1)
    def _():
        r = project()
        oz_ref[:, KD:] = r[:, :KD]
        o_ref[:, KD:] = r[:, KD:].astype(BF16)

    @pl.when((j > 1) & (j < nj - 1))
    def _():
        o_ref[...] = project().astype(BF16)

    @pl.when((j == nj - 1) & (i + 1 < ni))
    def _():
        o_ref[...] = project().astype(BF16)
        prepare(i + 1, 1 - slot)

    @pl.when((j == nj - 1) & (i + 1 == ni))
    def _():
        o_ref[...] = project().astype(BF16)


def _in_projection(x2, er, ec, g1, sh1, sc1, w_bf):
    tiles_per_batch = L // TM_IN
    n_i = T // TM_IN
    n_j = IN_W // TN_IN
    ahead = lambda i, j: jnp.minimum(i + jnp.where(j == n_j - 1, 1, 0), n_i - 1)
    return pl.pallas_call(
        _inproj_kernel,
        grid=(n_i, n_j),
        in_specs=[pl.BlockSpec((TM_IN, D), lambda i, j: (ahead(i, j), 0)),
                  pl.BlockSpec((L // GRID_W, D // 2), lambda i, j: (0, 0)),
                  pl.BlockSpec((GRID_W, D // 2), lambda i, j: (0, 0)),
                  pl.BlockSpec((1, D), lambda i, j: (0, 0)),
                  pl.BlockSpec((None, 1, D), lambda i, j: (ahead(i, j) // tiles_per_batch, 0, 0)),
                  pl.BlockSpec((None, 1, D), lambda i, j: (ahead(i, j) // tiles_per_batch, 0, 0)),
                  pl.BlockSpec((D, TN_IN), lambda i, j: (0, j))],
        out_specs=(pl.BlockSpec((TM_IN, TN_IN), lambda i, j: (i, 0)),
                   pl.BlockSpec((TM_IN, TN_IN), lambda i, j: (i, jnp.maximum(j - 1, 0)))),
        out_shape=(jax.ShapeDtypeStruct((T, 2 * KD), F32),
                   jax.ShapeDtypeStruct((T, IN_W - 2 * KD), BF16)),
        scratch_shapes=[pltpu.VMEM((2, TM_IN, D), BF16)],
        compiler_params=_cparams(2, VMEM_LIMIT),
        name="in_proj",
    )(x2, er, ec, g1, sh1, sc1, w_bf)


def _hgrn_kernel(reverse, readout, *refs):
    if readout:
        q_ref, z_ref, v_ref, lb_ref, s0_ref, of_ref, g_ref, ng_ref, o_ref, st_ref = refs
    else:
        q_ref, z_ref, v_ref, lb_ref, s0_ref, o_ref, st_ref = refs

    @pl.when(pl.program_id(0) == 0)
    def _():
        st_ref[...] = s0_ref[...]

    r = lax.broadcasted_iota(I32, (TH, TH), 0)
    c = lax.broadcasted_iota(I32, (TH, TH), 1)
    cb_shift = CB.bit_length() - 1
    same = jnp.right_shift(r, cb_shift) == jnp.right_shift(c, cb_shift)
    tri_mask = same & ((c >= r) if reverse else (c <= r))
    tri = jnp.where(tri_mask, 1.0, 0.0).astype(BF16)
    rblk = jnp.right_shift(r, cb_shift)
    cblk = jnp.right_shift(c, cb_shift)
    dist = (rblk - cblk) if not reverse else (cblk - rblk)
    for b in range(B):
        _hgrn_chunk(reverse, readout, b, tri, tri_mask, dist, refs)


def _hgrn_chunk(reverse, readout, b, tri, tri_mask, dist, refs):
    if readout:
        q_ref, z_ref, v_ref, lb_ref, s0_ref, of_ref, g_ref, ng_ref, o_ref, st_ref = refs
    else:
        q_ref, z_ref, v_ref, lb_ref, s0_ref, o_ref, st_ref = refs
    q = q_ref[b].astype(F32)
    v = v_ref[b]
    k, logf = _keys(z_ref[b], lb_ref[...])
    bl = _dot01(tri, logf)
    nt = (((1,), (1,)), ((), ()))
    tn = (((0,), (0,)), ((), ()))
    nblk = TH // CB
    e_row = 0 if reverse else CB - 1
    m_row = CB - 1 - CB // 2 if reverse else CB // 2
    tau = [bl[jb * CB + e_row:jb * CB + e_row + 1] for jb in range(nblk)]
    mid = [bl[jb * CB + m_row:jb * CB + m_row + 1] for jb in range(nblk)]
    rows = lambda vecs: jnp.concatenate([jnp.broadcast_to(x, (CB, KD)) for x in vecs], axis=0)
    mid_b = rows(mid)
    qd0 = (q * jnp.exp(bl - mid_b)).astype(BF16)
    kd0 = (k * jnp.exp(mid_b - bl)).astype(BF16)
    qs = q * jnp.exp(bl)
    ke = k * jnp.exp(rows(tau) - bl)
    order = list(range(nblk - 1, -1, -1)) if reverse else list(range(nblk))
    pre = [jnp.zeros((1, KD), F32)]
    for i in range(nblk):
        pre.append(pre[-1] + tau[order[i]])
    total = pre[nblk]
    entry = [None] * nblk
    leave = [None] * nblk
    gap = [[None] * nblk for _ in range(nblk)]
    for i, jb in enumerate(order):
        entry[jb] = jnp.exp(pre[i])
        leave[jb] = jnp.exp(total - pre[i + 1])
        for d in range(2, nblk):
            gap[d][jb] = jnp.exp(pre[i + d] - pre[i + 1]) if i + d < nblk else jnp.zeros((1, KD), F32)
    qc = (qs * rows(entry)).astype(BF16)
    kc = (ke * rows(leave)).astype(BF16)
    kx = jnp.concatenate([ke.astype(BF16)] + [(ke * rows(gap[d])).astype(BF16) for d in range(2, nblk)], axis=0)
    qsb = qs.astype(BF16)
    dec = jnp.exp(total)

    def blockdiag(x):
        first_head = lax.broadcasted_iota(I32, x.shape, 1) < DK
        zero = jnp.zeros_like(x)
        return jnp.concatenate([jnp.where(first_head, x, zero), jnp.where(first_head, zero, x)], axis=0)

    for hp in range(H // 2):
        ps = slice(2 * hp * DK, 2 * (hp + 1) * DK)
        sc0 = lax.dot_general(qd0[:, ps], blockdiag(kd0[:, ps]), nt, preferred_element_type=F32)
        scx = lax.dot_general(qsb[:, ps], blockdiag(kx[:, ps]), nt, preferred_element_type=F32)
        halves = []
        for hh in range(2):
            sc = jnp.where(tri_mask, sc0[:, hh * TH:(hh + 1) * TH], 0.0)
            base = hh * (nblk - 1) * TH
            for d in range(1, nblk):
                sc = jnp.where(dist == d, scx[:, base + (d - 1) * TH:base + d * TH], sc)
            halves.append(sc.astype(BF16))
        sc_pair = jnp.concatenate(halves, axis=1)
        st_a = st_ref[b, 2 * hp]
        st_b = st_ref[b, 2 * hp + 1]
        zst = jnp.zeros((DV, DK), BF16)
        st_pair = jnp.concatenate([jnp.concatenate([st_a.astype(BF16), zst], axis=1),
                                   jnp.concatenate([zst, st_b.astype(BF16)], axis=1)], axis=0)
        o_pair = (lax.dot_general(qc[:, ps], st_pair, nt, preferred_element_type=F32)
                  + jnp.dot(sc_pair, blockdiag(v[:, ps]), preferred_element_type=F32))
        upd = lax.dot_general(v[:, ps], kc[:, ps], tn, preferred_element_type=F32)
        st_ref[b, 2 * hp] = st_a * dec[:, ps][:, :DK] + upd[:DV, :DK]
        st_ref[b, 2 * hp + 1] = st_b * dec[:, ps][:, DK:] + upd[DV:, DK:]
        for hh in range(2):
            hs = slice((2 * hp + hh) * DK, (2 * hp + hh + 1) * DK)
            o_h = o_pair[:, hh * DV:(hh + 1) * DV]
            if readout:
                o_h = o_h + of_ref[b, :, hs]
                o_h = _rms(o_h) * ng_ref[...]
                gh = g_ref[b, :, hs].astype(F32)
                o_h = o_h * (gh * jax.nn.sigmoid(gh))
            o_ref[b, :, hs] = o_h


def _hgrn_scan(pz, p, lb_row, s0, reverse, o_f=None, norm_g=None):
    nch = L // TH
    chunk = (lambda c: nch - 1 - c) if reverse else (lambda c: c)
    col_spec = lambda j: pl.BlockSpec((B, TH, KD), lambda c: (0, chunk(c), j))
    in_specs = [col_spec(0), col_spec(1 if reverse else 0), col_spec(1),
                pl.BlockSpec((1, KD), lambda c: (0, 0)),
                pl.BlockSpec((B, H, DV, DK), lambda c: (0, 0, 0, 0))]
    p3 = p.reshape(B, L, p.shape[-1])
    args = [p3, pz.reshape(B, L, 2 * KD), p3, lb_row, s0]
    readout = o_f is not None
    if readout:
        in_specs += [col_spec(0), col_spec(2), pl.BlockSpec((1, DV), lambda c: (0, 0))]
        args += [o_f, p3, norm_g]
    return pl.pallas_call(
        functools.partial(_hgrn_kernel, reverse, readout),
        grid=(nch,),
        in_specs=in_specs,
        out_specs=col_spec(0),
        out_shape=jax.ShapeDtypeStruct((B, L, KD), F32),
        scratch_shapes=[pltpu.VMEM((B, H, DV, DK), F32)],
        compiler_params=_cparams(1, VMEM_LIMIT),
        name="hgrn_bwd_readout" if readout else "hgrn_fwd",
    )(*args)


def _hy_pre_kernel(v_ref, x1_ref, x0_ref, vp_ref, x1p_ref, x0p_ref, vn_ref, x1n_ref, x0n_ref,
                   w_ref, b_ref, vx_ref, x0o_ref):
    i = pl.program_id(1)
    first = i == 0
    last = i == pl.num_programs(1) - 1
    row = lax.broadcasted_iota(I32, (TM_HY, 1), 0)

    def conv(c_ref, p_ref, n_ref, col):
        x = c_ref[...].astype(F32)
        prev_row = jnp.where(first, 0.0, p_ref[...].astype(F32)[HALO - 1:HALO, :])
        next_row = jnp.where(last, 0.0, n_ref[...].astype(F32)[0:1, :])
        xm = jnp.where(row == 0, prev_row, pltpu.roll(x, 1, axis=0))
        xp = jnp.where(row == TM_HY - 1, next_row, pltpu.roll(x, TM_HY - 1, axis=0))
        cs = slice(col * HYW, (col + 1) * HYW)
        return xm * w_ref[0:1, cs] + x * w_ref[1:2, cs] + xp * w_ref[2:3, cs] + b_ref[:, cs]

    v = conv(v_ref, vp_ref, vn_ref, 0)
    x1 = conv(x1_ref, x1p_ref, x1n_ref, 1)
    x0 = conv(x0_ref, x0p_ref, x0n_ref, 2)
    vx_ref[...] = v * x1
    x0o_ref[...] = x0


def _hyena_pre(p, conv_w, conv_b):
    nt = L // TM_HY
    hb = TM_HY // HALO
    nhb = T // HALO
    cur = lambda col: pl.BlockSpec((TM_HY, HYW), lambda b, i: (b * nt + i, col))
    prv = lambda col: pl.BlockSpec((HALO, HYW), lambda b, i: (jnp.maximum((b * nt + i) * hb - 1, 0), col))
    nxt = lambda col: pl.BlockSpec((HALO, HYW), lambda b, i: (jnp.minimum((b * nt + i + 1) * hb, nhb - 1), col))
    c0 = 3
    out = jax.ShapeDtypeStruct((T, HYW), F32)
    return pl.pallas_call(
        _hy_pre_kernel,
        grid=(B, nt),
        in_specs=[cur(c0), cur(c0 + 1), cur(c0 + 2), prv(c0), prv(c0 + 1), prv(c0 + 2),
                  nxt(c0), nxt(c0 + 1), nxt(c0 + 2),
                  pl.BlockSpec((3, 3 * HYW), lambda b, i: (0, 0)),
                  pl.BlockSpec((1, 3 * HYW), lambda b, i: (0, 0))],
        out_specs=(pl.BlockSpec((TM_HY, HYW), lambda b, i: (b * nt + i, 0)),
                   pl.BlockSpec((TM_HY, HYW), lambda b, i: (b * nt + i, 0))),
        out_shape=(out, out),
        compiler_params=_cparams(2, VMEM_LIMIT),
        name="hyena_pre",
    )(p, p, p, p, p, p, p, p, p, conv_w, conv_b)


def _taps_dft_kernel(z_ref, w1_ref, b1_ref, f1_ref, w2_ref, b2_ref, f2_ref, w3a_ref, w3b_ref, dl_ref, fm_ref,
                     o_hbm, obuf, sem):
    g = pl.program_id(0)
    na = L // FFT_P
    half = z_ref.shape[0] // 2
    zt = z_ref[0:half, :]
    zb = z_ref[half:, :]
    lane = lax.broadcasted_iota(I32, zt.shape, 1)
    dot = _dot_split
    h = jnp.sin(f1_ref[...] * (dot(jnp.concatenate([zt, zb], axis=1), w1_ref) + b1_ref[...]))
    h = jnp.sin(f2_ref[...] * (dot(h, w2_ref) + b2_ref[...]))

    def copies(step):
        return [pltpu.make_async_copy(obuf.at[n, jj], o_hbm.at[n, :, step * FFT_BB + jj, :], sem)
                for n in range(2) for jj in range(FFT_BB)]

    @pl.when(g > 0)
    def _():
        for cp in copies(g - 1):
            cp.wait()

    for hi, (zz, w3_ref) in enumerate(((zt, w3a_ref), (zb, w3b_ref))):
        taps = dot(h, w3_ref)
        win = jnp.exp(-_lane_pick(zz, lane, 0) * dl_ref[...])
        fwd = (taps[:, :HYW] * win).astype(BF16)
        bwd = (taps[:, HYW:] * win * _lane_pick(zz, lane, HY_EMB)).astype(BF16)
        for bb in range(FFT_BB // 2):
            jj = hi * (FFT_BB // 2) + bb
            rows = slice(bb * na, (bb + 1) * na)
            obuf[0, jj] = jnp.dot(fm_ref[...], fwd[rows], preferred_element_type=F32)
            obuf[1, jj] = jnp.dot(fm_ref[...], bwd[rows], preferred_element_type=F32)

    for cp in copies(g):
        cp.start()

    @pl.when(g == pl.num_programs(0) - 1)
    def _():
        for cp in copies(g):
            cp.wait()


def _filter_taps_dft(zin, w1, b1, f1, w2, b2, f2, w3a, w3b, deltas, fmat):
    na = L // FFT_P
    tm = FFT_BB * na
    ln = V7X_LANES
    full = lambda shape: pl.BlockSpec(shape, lambda i: tuple(0 for _ in shape))
    return pl.pallas_call(
        _taps_dft_kernel,
        grid=(FFT_P // FFT_BB,),
        in_specs=[pl.BlockSpec((tm, ln), lambda i: (i, 0)),
                  full((2, 2 * ln, ln)), full((1, ln)), full((1, ln)),
                  full((2, ln, ln)), full((1, ln)), full((1, ln)),
                  full((2, ln, 2 * HYW)), full((2, ln, 2 * HYW)), full((1, HYW)), full((FFT_P, na))],
        out_specs=pl.BlockSpec(memory_space=pl.ANY),
        out_shape=jax.ShapeDtypeStruct((2, FFT_P, FFT_P, HYW), F32),
        scratch_shapes=[pltpu.VMEM((2, FFT_BB, FFT_P, HYW), F32), pltpu.SemaphoreType.DMA(())],
        compiler_params=_cparams(1, VMEM_LIMIT),
        name="filter_taps_dft",
    )(zin, w1, b1, f1, w2, b2, f2, w3a, w3b, deltas, fmat)


def _strided_dft_kernel(x_hbm, f_ref, o_hbm, xbuf, obuf, sem_in, sem_out):
    g = pl.program_id(0)
    ng = pl.num_programs(0)
    nb = FFT_P // FFT_BB

    def copies(grp, slot, inbound):
        n = grp // nb
        b0 = (grp % nb) * FFT_BB
        if inbound:
            return [pltpu.make_async_copy(x_hbm.at[n, :, b0 + jj, :], xbuf.at[slot, jj], sem_in.at[slot])
                    for jj in range(FFT_BB)]
        return [pltpu.make_async_copy(obuf.at[slot, jj], o_hbm.at[n, :, b0 + jj, :], sem_out.at[slot])
                for jj in range(FFT_BB)]

    def start(grp, slot, inbound):
        for cp in copies(grp, slot, inbound):
            cp.start()

    def wait(grp, slot, inbound):
        for cp in copies(grp, slot, inbound):
            cp.wait()

    @pl.when(g == 0)
    def _():
        start(0, 0, True)

    for slot in range(2):
        grp = 2 * g + slot
        if slot == 0:
            start(grp + 1, 1, True)
        else:
            @pl.when(g + 1 < ng)
            def _():
                start(grp + 1, 0, True)
        wait(grp, slot, True)

        @pl.when(g > 0)
        def _():
            wait(grp - 2, slot, False)

        for jj in range(FFT_BB):
            obuf[slot, jj] = jnp.dot(f_ref[...], xbuf[slot, jj].astype(BF16), preferred_element_type=F32)
        start(grp, slot, False)

    @pl.when(g + 1 == ng)
    def _():
        wait(2 * g, 0, False)
        wait(2 * g + 1, 1, False)


def _strided_dft(xv, fmat, name):
    n, kk = xv.shape[0], xv.shape[1]
    mm = fmat.shape[0]
    groups = n * (FFT_P // FFT_BB)
    return pl.pallas_call(
        _strided_dft_kernel,
        grid=(groups // 2,),
        in_specs=[pl.BlockSpec(memory_space=pl.ANY),
                  pl.BlockSpec((mm, kk), lambda g: (0, 0))],
        out_specs=pl.BlockSpec(memory_space=pl.ANY),
        out_shape=jax.ShapeDtypeStruct((n, mm, FFT_P, HYW), F32),
        scratch_shapes=[pltpu.VMEM((2, FFT_BB, kk, HYW), F32), pltpu.VMEM((2, FFT_BB, mm, HYW), F32),
                        pltpu.SemaphoreType.DMA((2,)), pltpu.SemaphoreType.DMA((2,))],
        compiler_params=_cparams(1, VMEM_LIMIT),
        name=name,
    )(xv, fmat)


def _cblock(mr, mi):
    return jnp.concatenate([jnp.concatenate([mr, -mi], axis=1), jnp.concatenate([mi, mr], axis=1)], axis=0)


FFT_H = FFT_P // 2


def _twiddled(fr_ref, fi_ref, tw):
    twr = tw[0:1, :]
    twi = tw[1:2, :]
    fr = fr_ref[...]
    fi = fi_ref[...]
    return fr * twr - fi * twi, fr * twi + fi * twr


def _cmul_rows(x, kr, ki):
    xr, xi = x[:FFT_P], x[FFT_P:]
    return jnp.concatenate([xr * kr - xi * ki, xr * ki + xi * kr], axis=0).astype(BF16)


_TN_DIMS = (((0,), (0,)), ((), ()))
_STAGE2_SPECS = [pl.BlockSpec((FFT_P, FFT_P), lambda d: (0, 0)),
                 pl.BlockSpec((FFT_P, FFT_P), lambda d: (0, 0)),
                 pl.BlockSpec((None, 2, FFT_P), lambda d: (d, 0, 0)),
                 pl.BlockSpec((2, FFT_P), lambda d: (0, 0))]


def _kspec_kernel(a_ref, fr_ref, fi_ref, tw_ref, twh_ref, o_ref, oh_ref):
    dd = pl.program_id(0)
    dot = functools.partial(jnp.dot, preferred_element_type=F32)

    def combine(xf, xb, out_ref):
        out_ref[0] = xf[:FFT_P] + xb[:FFT_P]
        out_ref[1] = xf[FFT_P:] - xb[FFT_P:]

    @pl.when(dd > 0)
    def _():
        rm = _cblock(*_twiddled(fr_ref, fi_ref, tw_ref[...])).astype(BF16)
        combine(dot(rm, a_ref[0].reshape(2 * FFT_P, HYW).astype(BF16)),
                dot(rm, a_ref[1].reshape(2 * FFT_P, HYW).astype(BF16)), o_ref)

    @pl.when(dd == 0)
    def _():
        for slot, tw, out_ref in ((0, tw_ref[...], o_ref), (1, twh_ref[...], oh_ref)):
            w = jnp.concatenate(_twiddled(fr_ref, fi_ref, tw), axis=0).astype(BF16)
            combine(dot(w, a_ref[0, slot].astype(BF16)), dot(w, a_ref[1, slot].astype(BF16)), out_ref)


def _kernel_spectrum(ak, fr, fi, tw, twh):
    return pl.pallas_call(
        _kspec_kernel,
        grid=(FFT_H,),
        in_specs=[pl.BlockSpec((2, None, 2, FFT_P, HYW), lambda d: (0, d, 0, 0, 0))] + _STAGE2_SPECS,
        out_specs=(pl.BlockSpec((2, None, FFT_P, HYW), lambda d: (0, d, 0, 0)),
                   pl.BlockSpec((2, FFT_P, HYW), lambda d: (0, 0, 0))),
        out_shape=(jax.ShapeDtypeStruct((2, FFT_H, FFT_P, HYW), F32),
                   jax.ShapeDtypeStruct((2, FFT_P, HYW), F32)),
        compiler_params=_cparams(1, VMEM_LIMIT),
        name="kernel_spectrum",
    )(ak, fr, fi, tw, twh)


def _mid_kernel(a_ref, kh_ref, khh_ref, fr_ref, fi_ref, tw_ref, twh_ref, o_ref):
    dd = pl.program_id(0)
    dot = functools.partial(jnp.dot, preferred_element_type=F32)
    dot_t = lambda w, y: lax.dot_general(w, y, _TN_DIMS, preferred_element_type=F32)

    @pl.when(dd > 0)
    def _():
        rm = _cblock(*_twiddled(fr_ref, fi_ref, tw_ref[...])).astype(BF16)
        for n in range(B):
            x = dot(rm, a_ref[n].reshape(2 * FFT_P, HYW).astype(BF16))
            y = _cmul_rows(x, kh_ref[0], kh_ref[1])
            o_ref[n] = dot_t(rm, y).reshape(2, FFT_P, HYW)

    @pl.when(dd == 0)
    def _():
        for slot, tw, k_ref in ((0, tw_ref[...], kh_ref), (1, twh_ref[...], khh_ref)):
            w = jnp.concatenate(_twiddled(fr_ref, fi_ref, tw), axis=0).astype(BF16)
            for n in range(B):
                y = _cmul_rows(dot(w, a_ref[n, slot].astype(BF16)), k_ref[0], k_ref[1])
                o_ref[n, slot] = dot_t(w, y)


def _fft_mid(au, kh, khh, fr, fi, tw, twh):
    pair = pl.BlockSpec((B, None, 2, FFT_P, HYW), lambda d: (0, d, 0, 0, 0))
    return pl.pallas_call(
        _mid_kernel,
        grid=(FFT_H,),
        in_specs=[pair,
                  pl.BlockSpec((2, None, FFT_P, HYW), lambda d: (0, d, 0, 0)),
                  pl.BlockSpec((2, FFT_P, HYW), lambda d: (0, 0, 0))] + _STAGE2_SPECS,
        out_specs=pair,
        out_shape=jax.ShapeDtypeStruct((B, FFT_H, 2, FFT_P, HYW), F32),
        compiler_params=_cparams(1, VMEM_LIMIT),
        name="fft_mid",
    )(au, kh, khh, fr, fi, tw, twh)


def _dft_tables():
    na = L // FFT_P
    a = np.arange(na)
    dd = np.arange(FFT_H)
    ang = 2.0 * np.pi * np.outer(dd, a) / FFT_P
    re_rows = np.cos(ang)
    im_rows = -np.sin(ang)
    im_rows[0] = np.cos(np.pi * a)
    f_first = np.stack([re_rows, im_rows], axis=1).reshape(FFT_P, na)
    gre = 2.0 * np.cos(ang)
    gim = -2.0 * np.sin(ang)
    gre[0] = 1.0
    gim[0] = np.cos(np.pi * a)
    g_last = np.stack([gre, gim], axis=1).reshape(FFT_P, na).T / FFT_N
    b = np.arange(FFT_P)
    angf = 2.0 * np.pi * np.outer(b, b) / FFT_P
    ang2 = 2.0 * np.pi * np.outer(np.arange(FFT_H + 1), b) / FFT_N
    tw = np.stack([np.cos(ang2), -np.sin(ang2)], axis=1)
    f32 = lambda x: jnp.asarray(x.astype(np.float32))
    return (f32(f_first).astype(BF16), f32(g_last).astype(BF16), f32(np.cos(angf)), f32(-np.sin(angf)),
            f32(tw[:FFT_H]), f32(tw[FFT_H]))


def _merge_kernel(ya_ref, cv_ref, vx_ref, x0_ref, ga_ref, gb_ref, x_ref, er_ref, ec_ref, gt1_ref, hyd_ref,
                  wpa_ref, wpb_ref, wo_ref, g2_ref, sh2_ref, sc2_ref, wr_ref, br_ref,
                  x1_ref, t2_ref, lg_ref):
    vx = vx_ref[...]
    yb = x0_ref[...] * (cv_ref[...] + vx * hyd_ref[...])
    pa = jnp.dot(ya_ref[...].astype(BF16), wpa_ref[...], preferred_element_type=F32)
    pb = jnp.dot(yb.astype(BF16), wpb_ref[...], preferred_element_type=F32)
    mixed = (jax.nn.sigmoid(ga_ref[...].astype(F32)) * pa
             + jax.nn.sigmoid(gb_ref[...].astype(F32)) * pb)
    xm = jnp.dot(mixed.astype(BF16), wo_ref[...], preferred_element_type=F32)
    nrow = TM_MG // GRID_W
    row0 = (pl.program_id(0) % (L // TM_MG)) * nrow
    x1 = x_ref[...] + _pos_tile(er_ref, ec_ref, row0, nrow) + gt1_ref[...] * xm
    x1_ref[...] = x1
    t2 = _rms(x1) * g2_ref[...]
    t2 = t2 * (1.0 + sc2_ref[...]) + sh2_ref[...]
    t2_ref[...] = t2
    t_hi = t2.astype(BF16)
    t_lo = (t2 - t_hi.astype(F32)).astype(BF16)
    rr = (jnp.dot(t_hi, wr_ref[...], preferred_element_type=F32)
          + jnp.dot(t_lo, wr_ref[...], preferred_element_type=F32))
    lg_ref[...] = rr[:, :V7X_LANES] + rr[:, V7X_LANES:] + br_ref[...]


def _merge(ya, cv, vx, x0c, p, x2, er, ec, gt1, hyd, wpa, wpb, wo, g2, sh2, sc2, wr, br):
    tpb = L // TM_MG
    half = lambda: pl.BlockSpec((TM_MG, HYW), lambda i: (i, 0))
    full = lambda shape: pl.BlockSpec(shape, lambda i: tuple(0 for _ in shape))
    perb = lambda: pl.BlockSpec((None, 1, D), lambda i: (i // tpb, 0, 0))
    return pl.pallas_call(
        _merge_kernel,
        grid=(T // TM_MG,),
        in_specs=[half(), half(), half(), half(),
                  pl.BlockSpec((TM_MG, D), lambda i: (i, 3)),
                  pl.BlockSpec((TM_MG, D), lambda i: (i, 4)),
                  pl.BlockSpec((TM_MG, D), lambda i: (i, 0)),
                  full((L // GRID_W, D // 2)), full((GRID_W, D // 2)),
                  perb(), full((1, HYW)),
                  full((KD, D)), full((HYW, D)), full((D, D)),
                  full((1, D)), perb(), perb(),
                  full((D, 2 * V7X_LANES)), full((1, V7X_LANES))],
        out_specs=(pl.BlockSpec((TM_MG, D), lambda i: (i, 0)),
                   pl.BlockSpec((TM_MG, D), lambda i: (i, 0)),
                   pl.BlockSpec((TM_MG, V7X_LANES), lambda i: (i, 0))),
        out_shape=(jax.ShapeDtypeStruct((T, D), F32), jax.ShapeDtypeStruct((T, D), F32),
                   jax.ShapeDtypeStruct((T, V7X_LANES), F32)),
        compiler_params=_cparams(1, VMEM_LIMIT),
        name="merge",
    )(ya, cv, vx, x0c, p, p, x2, er, ec, gt1, hyd, wpa, wpb, wo, g2, sh2, sc2, wr, br)


def _route_kernel(lg_ref, info_ref, cnt_ref):
    @pl.when(pl.program_id(0) == 0)
    def _():
        cnt_ref[...] = jnp.zeros_like(cnt_ref)

    lg = lg_ref[...]
    lane = lax.broadcasted_iota(I32, lg.shape, 1)
    lanef = lane.astype(F32)
    neg = -1e30
    big = 1e9
    is_g = (lane >= NEXP) & (lane < NEXP + NGRP)
    gl = jnp.where(is_g, lg, neg)
    ge = jnp.where(is_g, jnp.exp(gl - jnp.max(gl, axis=-1, keepdims=True)), 0.0)
    pg = ge / jnp.sum(ge, axis=-1, keepdims=True)
    p_top_g = jnp.max(pg, axis=-1, keepdims=True)
    gidx = jnp.min(jnp.where(is_g & (pg == p_top_g), lanef, big), axis=-1, keepdims=True)
    g_sel = gidx.astype(I32) - NEXP
    emask = (lane < NEXP) & (jnp.right_shift(lane, NEPG.bit_length() - 1) == g_sel)
    el = jnp.where(emask, lg, neg)
    ee = jnp.where(emask, jnp.exp(el - jnp.max(el, axis=-1, keepdims=True)), 0.0)
    pe = ee / jnp.sum(ee, axis=-1, keepdims=True)
    p1 = jnp.max(jnp.where(emask, pe, -1.0), axis=-1, keepdims=True)
    i1 = jnp.min(jnp.where(emask & (pe == p1), lanef, big), axis=-1, keepdims=True)
    rest = emask & (lanef != i1)
    p2 = jnp.max(jnp.where(rest, pe, -1.0), axis=-1, keepdims=True)
    i2 = jnp.min(jnp.where(rest & (pe == p2), lanef, big), axis=-1, keepdims=True)
    wsum = p1 + p2
    w1 = p_top_g * p1 / wsum
    w2 = p_top_g * p2 / wsum
    sel1 = lanef == i1
    sel2 = lanef == i2
    oh = jnp.where(sel1 | sel2, 1.0, 0.0)
    r = lax.broadcasted_iota(I32, (TR, TR), 0)
    c = lax.broadcasted_iota(I32, (TR, TR), 1)
    stril = jnp.where(c < r, 1.0, 0.0).astype(BF16)
    before = jnp.dot(stril, oh.astype(BF16), preferred_element_type=F32) + cnt_ref[...]
    r1 = jnp.sum(jnp.where(sel1, before, 0.0), axis=-1, keepdims=True)
    r2 = jnp.sum(jnp.where(sel2, before, 0.0), axis=-1, keepdims=True)
    cnt_ref[...] += jnp.sum(oh, axis=0, keepdims=True)
    info = jnp.where(lane == 0, i1, jnp.where(lane == 1, r1, jnp.where(lane == 2, i2, jnp.where(
        lane == 3, r2, jnp.where(lane == 4, w1, jnp.where(lane == 5, w2, 0.0))))))
    info_ref[...] = info


def _route(lg):
    return pl.pallas_call(
        _route_kernel,
        grid=(T // TR,),
        in_specs=[pl.BlockSpec((TR, V7X_LANES), lambda i: (i, 0))],
        out_specs=(pl.BlockSpec((TR, V7X_LANES), lambda i: (i, 0)),
                   pl.BlockSpec((1, V7X_LANES), lambda i: (0, 0))),
        out_shape=(jax.ShapeDtypeStruct((T, V7X_LANES), F32), jax.ShapeDtypeStruct((1, V7X_LANES), F32)),
        compiler_params=_cparams(1, VMEM_LIMIT),
        name="route",
    )(lg)


def _positions_kernel(info_ref, st_ref, o_ref):
    info = info_ref[...]
    lane = lax.broadcasted_iota(I32, info.shape, 1)
    lanef = lane.astype(F32)
    st = st_ref[...]
    row = lambda e_lane, r_lane: (jnp.sum(jnp.where(lanef == _lane_pick(info, lane, e_lane), st, 0.0),
                                          axis=-1, keepdims=True) + _lane_pick(info, lane, r_lane))
    o_ref[...] = jnp.where(lane == 0, row(0, 1), jnp.where(lane == 1, row(2, 3), 0.0)).astype(I32)


def _positions(info, starts_row):
    return pl.pallas_call(
        _positions_kernel,
        grid=(T // TPOS,),
        in_specs=[pl.BlockSpec((TPOS, V7X_LANES), lambda i: (i, 0)),
                  pl.BlockSpec((1, V7X_LANES), lambda i: (0, 0))],
        out_specs=pl.BlockSpec((TPOS, V7X_LANES), lambda i: (i, 0)),
        out_shape=jax.ShapeDtypeStruct((T, V7X_LANES), I32),
        compiler_params=_cparams(1, VMEM_LIMIT),
        name="positions",
    )(info, starts_row)


def _scatter_kernel(pos_ref, zt_ref, t2_ref, zeros_hbm, xs_hbm, sem, zsem):
    i = pl.program_id(0)

    def zcopy(row):
        start = pl.multiple_of(jnp.maximum(row, 0), TE)
        return pltpu.make_async_copy(zeros_hbm, xs_hbm.at[pl.ds(start, TE)], zsem)

    @pl.when(i == 0)
    def _():
        def ztail(start, e, carry):
            @pl.when(zt_ref[0, e] >= 0)
            def _():
                cp = zcopy(zt_ref[0, e])
                cp.start() if start else cp.wait()
            return carry

        lax.fori_loop(0, NEXP, functools.partial(ztail, True), 0)
        lax.fori_loop(0, NEXP, functools.partial(ztail, False), 0)

        def zrest(start, tile, carry):
            cp = zcopy(tile * TE)
            cp.start() if start else cp.wait()
            return carry

        lax.fori_loop(zt_ref[0, NA_OFF], NT_EXP, functools.partial(zrest, True), 0)
        lax.fori_loop(zt_ref[0, NA_OFF], NT_EXP, functools.partial(zrest, False), 0)

    def row_copy(base, jj, kk):
        dst = pos_ref[0, 2 * (base + jj) + kk]
        return pltpu.make_async_copy(t2_ref.at[pl.ds(base, ROW_GROUP)].at[pl.ds(jj, 1)],
                                     xs_hbm.at[pl.ds(dst, 1)], sem)

    _start_rows(TS, row_copy)
    _wait_rows(TS, pltpu.make_async_copy(t2_ref.at[pl.ds(0, 1)], xs_hbm.at[pl.ds(0, 1)], sem))


def _start_rows(n_rows, row_copy):
    def group(g, carry):
        base = pl.multiple_of(g * ROW_GROUP, ROW_GROUP)
        for jj in range(ROW_GROUP):
            for kk in range(2):
                row_copy(base, jj, kk).start(priority=kk)
        return carry

    lax.fori_loop(0, n_rows // ROW_GROUP, group, 0)


def _wait_rows(n_rows, one_row_copy):
    def drain(j, carry):
        one_row_copy.wait()
        one_row_copy.wait()
        return carry

    lax.fori_loop(0, n_rows, drain, 0, unroll=ROW_UNROLL)


def _scatter_rows(pos3, meta, t2, zeros_tile):
    return pl.pallas_call(
        _scatter_kernel,
        grid=(T // TS,),
        in_specs=[pl.BlockSpec((None, 1, 2 * TS), lambda i: (i, 0, 0), memory_space=pltpu.SMEM),
                  pl.BlockSpec(memory_space=pltpu.SMEM),
                  pl.BlockSpec((TS, D), lambda i: (i, 0)),
                  pl.BlockSpec(memory_space=pl.ANY)],
        out_specs=pl.BlockSpec(memory_space=pl.ANY),
        out_shape=jax.ShapeDtypeStruct((NP_ROWS, D), F32),
        scratch_shapes=[pltpu.SemaphoreType.DMA(()), pltpu.SemaphoreType.DMA(())],
        compiler_params=_cparams(1, VMEM_LIMIT),
        name="scatter_rows",
    )(pos3, meta, t2, zeros_tile)


def _expert_kernel(te_ref, nx_ref, sl_ref, na_ref, xs_ref, wg_hbm, wu_hbm, wd_hbm, ys_ref,
                   wgs_ref, wus_ref, wds_ref, wgb_ref, wub_ref, wdb_ref, sem):
    i = pl.program_id(0)
    active = i < na_ref[0]
    first = active & ((i == 0) | (te_ref[i] != te_ref[jnp.maximum(i - 1, 0)]))

    def fetch(e, slot):
        return [pltpu.make_async_copy(wg_hbm.at[e], wgs_ref.at[slot], sem.at[slot]),
                pltpu.make_async_copy(wu_hbm.at[e], wus_ref.at[slot], sem.at[slot]),
                pltpu.make_async_copy(wd_hbm.at[e], wds_ref.at[slot], sem.at[slot])]

    for slot in range(2):
        @pl.when(first & (sl_ref[i] == slot))
        def _():
            @pl.when(i == 0)
            def _():
                for cp in fetch(te_ref[i], slot):
                    cp.start()

            for cp in fetch(te_ref[i], slot):
                cp.wait()
            wgb_ref[...] = wgs_ref[slot].astype(BF16)
            wub_ref[...] = wus_ref[slot].astype(BF16)
            wdb_ref[...] = wds_ref[slot].astype(BF16)

            @pl.when(nx_ref[i] >= 0)
            def _():
                for cp in fetch(nx_ref[i], 1 - slot):
                    cp.start()

    @pl.when(active)
    def _():
        x = xs_ref[...].astype(BF16)
        g = jnp.dot(x, wgb_ref[...], preferred_element_type=F32)
        u = jnp.dot(x, wub_ref[...], preferred_element_type=F32)
        hid = (g * jax.nn.sigmoid(g) * u).astype(BF16)
        ys_ref[...] = jnp.dot(hid, wdb_ref[...], preferred_element_type=F32)

    @pl.when(jnp.logical_not(active))
    def _():
        ys_ref[...] = jnp.zeros_like(ys_ref)


def _experts(tile_expert, tile_next, tile_slot, n_active, xs, wg, wu, wd):
    rows = lambda i, *_: (i, 0)
    rows_in = lambda i, te, nx, sl, na: (jnp.minimum(i, na[0] - 1), 0)
    any_space = pl.BlockSpec(memory_space=pl.ANY)
    grid_spec = pltpu.PrefetchScalarGridSpec(
        num_scalar_prefetch=4,
        grid=(NT_EXP,),
        in_specs=[pl.BlockSpec((TE, D), rows_in), any_space, any_space, any_space],
        out_specs=pl.BlockSpec((TE, D), rows),
        scratch_shapes=[pltpu.VMEM((2, D, DEXP), F32), pltpu.VMEM((2, D, DEXP), F32), pltpu.VMEM((2, DEXP, D), F32),
                        pltpu.VMEM((D, DEXP), BF16), pltpu.VMEM((D, DEXP), BF16), pltpu.VMEM((DEXP, D), BF16),
                        pltpu.SemaphoreType.DMA((2,))],
    )
    return pl.pallas_call(
        _expert_kernel,
        grid_spec=grid_spec,
        out_shape=jax.ShapeDtypeStruct((NP_ROWS, D), F32),
        compiler_params=_cparams(1, VMEM_LIMIT),
        name="experts",
    )(tile_expert, tile_next, tile_slot, n_active, xs, wg, wu, wd)


def _combine_kernel(pos_ref, posn_ref, info_ref, x1_ref, gt2_ref, fg_ref, ys_hbm, o_ref, buf, sem):
    g = pl.program_id(0)
    ng = pl.num_programs(0)

    def gather(p_ref, half, slot):
        def row_copy(base, jj, kk):
            src = p_ref[0, 2 * (half * TC + base + jj) + kk]
            return pltpu.make_async_copy(ys_hbm.at[pl.ds(src, 1)],
                                         buf.at[slot, kk].at[pl.ds(base, ROW_GROUP)].at[pl.ds(jj, 1)],
                                         sem.at[slot])
        _start_rows(TC, row_copy)

    def finish(half, slot):
        _wait_rows(TC, pltpu.make_async_copy(ys_hbm.at[pl.ds(0, 1)], buf.at[slot, 0].at[pl.ds(0, 1)],
                                             sem.at[slot]))
        rows = slice(half * TC, (half + 1) * TC)
        info = info_ref[rows, :]
        lane = lax.broadcasted_iota(I32, info.shape, 1)
        moe = _lane_pick(info, lane, 4) * buf[slot, 0] + _lane_pick(info, lane, 5) * buf[slot, 1]
        x2 = x1_ref[rows, :] + gt2_ref[...] * moe
        o_ref[rows, :] = _rms(x2) * fg_ref[...]

    @pl.when(g == 0)
    def _():
        gather(pos_ref, 0, 0)

    gather(pos_ref, 1, 1)
    finish(0, 0)

    @pl.when(g + 1 < ng)
    def _():
        gather(posn_ref, 0, 0)

    finish(1, 1)


def _combine(pos3, info, x1, gt2, fg, ys):
    step = 2 * TC
    tpb = L // step
    nsteps = T // step
    return pl.pallas_call(
        _combine_kernel,
        grid=(nsteps,),
        in_specs=[pl.BlockSpec((None, 1, 2 * step), lambda i: (i, 0, 0), memory_space=pltpu.SMEM),
                  pl.BlockSpec((None, 1, 2 * step), lambda i: (jnp.minimum(i + 1, nsteps - 1), 0, 0),
                               memory_space=pltpu.SMEM),
                  pl.BlockSpec((step, V7X_LANES), lambda i: (i, 0)),
                  pl.BlockSpec((step, D), lambda i: (i, 0)),
                  pl.BlockSpec((None, 1, D), lambda i: (i // tpb, 0, 0)),
                  pl.BlockSpec((1, D), lambda i: (0, 0)),
                  pl.BlockSpec(memory_space=pl.ANY)],
        out_specs=pl.BlockSpec((step, D), lambda i: (i, 0)),
        out_shape=jax.ShapeDtypeStruct((T, D), F32),
        scratch_shapes=[pltpu.VMEM((2, 2, TC, D), F32), pltpu.SemaphoreType.DMA((2,))],
        compiler_params=_cparams(1, VMEM_LIMIT),
        name="combine",
    )(pos3, pos3, info, x1, gt2, fg, ys)


def _pos_tables():
    rows = L // GRID_W
    quarter = D // 4
    omega = 1.0 / (10000.0 ** (jnp.arange(quarter, dtype=F32) / quarter))

    def axis_emb(pos):
        a = pos[:, None] * omega[None, :]
        return jnp.concatenate([jnp.sin(a), jnp.cos(a)], axis=-1)

    er = axis_emb(jnp.arange(rows, dtype=F32))
    ec = axis_emb(jnp.arange(GRID_W, dtype=F32))
    return er, ec


def _filter_features():
    z = np.zeros((L, V7X_LANES), np.float64)
    bands = (HY_EMB - 1) // 2
    ang = (2.0 * np.pi * np.arange(L) / L)[:, None] * np.linspace(1e-4, bands - 1, bands)[None, :]
    z[:, 0] = np.linspace(0.0, 1.0, L)
    z[:, 1:1 + bands] = np.cos(ang)
    z[:, 1 + bands:HY_EMB] = -np.sin(ang)
    z[1:, HY_EMB] = 1.0
    z = z.reshape(L // FFT_P, FFT_P, V7X_LANES).transpose(1, 0, 2).reshape(L, V7X_LANES)
    return jnp.asarray(z.astype(np.float32))


def _pad2(a, rows, cols):
    return jnp.pad(a, ((0, rows - a.shape[0]), (0, cols - a.shape[1])))


def kernel(x, c, ctx, c_ctx, ada_w, ada_b, norm1_g, norm2_g, w_in, hgrn_lb, hgrn_norm_g, hy_conv_w, hy_conv_b, hy_filt_w1, hy_filt_b1, hy_filt_freq1, hy_filt_w2, hy_filt_b2, hy_filt_freq2, hy_filt_w3, hy_d, w_proj_a, w_proj_b, w_out, moe_router_g_w, moe_router_g_b, moe_router_e_w, moe_router_e_b, moe_w_gate, moe_w_up, moe_w_down, final_norm_g):
    cvec = jnp.zeros((8, D), F32).at[0:B].set(c).at[B].set(c_ctx)
    mod = _adaln(cvec, ada_w[0], ada_b[0][None, :])
    m6 = mod.reshape(8, 6, D)
    sh1, sc1, gt1, sh2, sc2, gt2 = [m6[0:B, k][:, None, :] for k in range(6)]
    csh1, csc1 = m6[B:B + 1, 0], m6[B:B + 1, 1]

    lbs = jnp.cumsum(jax.nn.softmax(hgrn_lb.astype(F32), axis=0), axis=0)[0]
    g1 = norm1_g[0][None, :]
    er, ec = _pos_tables()
    x2 = x.reshape(T, D)

    w_ctx = w_in[0][:, KD:4 * KD].astype(BF16)
    s_f, s_b = _context_states(ctx, g1, csh1, csc1, w_ctx, lbs)

    pz, p = _in_projection(x2, er, ec, g1, sh1, sc1, w_in[0].astype(BF16))

    o_f = _hgrn_scan(pz, p, lbs[0:1], s_f, False)
    y_a = _hgrn_scan(pz, p, lbs[1:2], s_b, True, o_f=o_f, norm_g=hgrn_norm_g[0][None, :])

    vx, x0c = _hyena_pre(p, hy_conv_w[0], hy_conv_b[0][None, :])
    deltas = jnp.abs(jnp.linspace(math.log(HY_DECAY_TARGET) / HY_SLOW_PCT,
                                  math.log(HY_DECAY_TARGET) / HY_FAST_PCT, HYW, dtype=F32))[None, :]
    ln = V7X_LANES
    fh = hy_filt_w2.shape[-1]
    blockdiag = lambda m: jnp.concatenate([_pad2(m, m.shape[0], 2 * m.shape[1]),
                                           jnp.pad(m, ((0, 0), (m.shape[1], 0)))], axis=0)
    twice = lambda v: jnp.concatenate([v, v])[None, :]
    w3 = hy_filt_w3[0]
    f_first, g_last, fr, fi, tw, twh = _dft_tables()
    na = L // FFT_P
    ak = _filter_taps_dft(
        _filter_features(),
        _hi_lo(blockdiag(_pad2(hy_filt_w1[0], ln, fh))), twice(hy_filt_b1[0]), twice(hy_filt_freq1[0]),
        _hi_lo(blockdiag(hy_filt_w2[0])), twice(hy_filt_b2[0]), twice(hy_filt_freq2[0]),
        _hi_lo(_pad2(w3, ln, 2 * HYW)), _hi_lo(jnp.pad(w3, ((fh, 0), (0, 0)))), deltas, f_first)
    kh, khh = _kernel_spectrum(ak.reshape(2, FFT_H, 2, FFT_P, HYW), fr, fi, tw, twh)
    au = _strided_dft(vx.reshape(B, na, FFT_P, HYW), f_first, "dft_first")
    bp = _fft_mid(au.reshape(B, FFT_H, 2, FFT_P, HYW), kh, khh, fr, fi, tw, twh)
    conv = _strided_dft(bp.reshape(B, FFT_P, FFT_P, HYW), g_last, "dft_last").reshape(T, HYW)

    wr = jnp.concatenate([jnp.transpose(moe_router_e_w[0], (1, 0, 2)).reshape(D, NEXP),
                          moe_router_g_w[0], jnp.zeros((D, V7X_LANES - NEXP - NGRP), F32)], axis=1)
    wr_hi = wr.astype(BF16)
    wr = jnp.concatenate([wr_hi, (wr - wr_hi.astype(F32)).astype(BF16)], axis=1)
    br = jnp.concatenate([moe_router_e_b[0].reshape(NEXP), moe_router_g_b[0],
                          jnp.zeros((V7X_LANES - NEXP - NGRP,), F32)])[None, :]
    x1, t2, lg = _merge(y_a.reshape(T, KD), conv, vx, x0c, p, x2, er, ec, gt1, hy_d[0][None, :],
                        w_proj_a[0].astype(BF16), w_proj_b[0].astype(BF16), w_out[0].astype(BF16),
                        norm2_g[0][None, :], sh2, sc2, wr, br)

    info, counts = _route(lg)
    cnt = counts[0, :NEXP].astype(I32)
    pc = ((cnt + TE - 1) // TE) * TE
    ends = jnp.cumsum(pc)
    starts = ends - pc
    n_active = (ends[-1] // TE).astype(I32)[None]
    tile_rows = jnp.arange(NT_EXP, dtype=I32) * TE
    tile_expert = jnp.minimum(jnp.sum((ends[None, :] <= tile_rows[:, None]).astype(I32), axis=1), NEXP - 1)
    meta = jnp.concatenate([jnp.where(pc > 0, ends - TE, -1), starts, n_active]).astype(I32)[None, :]
    starts_row = jnp.pad(starts.astype(F32), (0, V7X_LANES - NEXP))[None, :]
    pos3 = _positions(info, starts_row)[:, :2].reshape(T // TS, 1, 2 * TS)

    xs = _scatter_rows(pos3, meta, t2, jnp.zeros((TE, D), F32))
    eid = jnp.arange(NEXP, dtype=I32)
    nonempty = pc > 0
    later = jnp.where(nonempty[None, :] & (eid[None, :] > eid[:, None]), eid[None, :], NEXP)
    next_e = jnp.min(later, axis=1)
    next_e = jnp.where(next_e < NEXP, next_e, -1).astype(I32)
    slot_e = ((jnp.cumsum(nonempty.astype(I32)) - 1) % 2).astype(I32)
    ys = _experts(tile_expert, next_e[tile_expert], slot_e[tile_expert], n_active, xs,
                  moe_w_gate[0].reshape(NEXP, D, DEXP), moe_w_up[0].reshape(NEXP, D, DEXP),
                  moe_w_down[0].reshape(NEXP, DEXP, D))
    out = _combine(pos3, info, x1, gt2, final_norm_g[None, :], ys)
    return out.reshape(B, L, D)
```

```python
import functools
import math

import numpy as np
import jax
import jax.numpy as jnp
from jax import lax
from jax.experimental import pallas as pl
from jax.experimental.pallas import tpu as pltpu

F32 = jnp.float32
BF16 = jnp.bfloat16
I32 = jnp.int32

D = 1024
B = 2
L = 8192
T = B * L
CTX = 256
GRID_W = 64
EPS = 1e-6
H = 4
DK = 128
DV = 128
KD = H * DK
IN_W = 6144
HYW = 512
HY_EMB = 33
NGRP = 4
NEPG = 8
NEXP = NGRP * NEPG
DEXP = 512
HY_DECAY_TARGET = 1e-2
HY_FAST_PCT = 0.3
HY_SLOW_PCT = 1.5

V7X_LANES = 128
V7X_SUBLANES = 8
V7X_VMEM_BYTES = 64 * 1024 * 1024
VMEM_LIMIT = (3 * V7X_VMEM_BYTES) // 4

FFT_N = 2 * L
FFT_P = 128
FFT_BB = 8

TM_IN = 1024
TN_IN = 1024
TH = 128
CB = 32
TM_HY = 1024
HALO = 2 * V7X_SUBLANES
TM_MG = 512
TR = 512
TE = 512
TPOS = 2048
NP_ROWS = 2 * T + NEXP * TE
NT_EXP = NP_ROWS // TE
TS = 512
TC = TS // 2
SCATTER_SLOTS = 3
ROW_UNROLL = 8
ROW_GROUP = 32
ST_OFF = NEXP
NA_OFF = 2 * NEXP


def _cparams(n_axes, vmem=None):
    return pltpu.CompilerParams(dimension_semantics=("arbitrary",) * n_axes,
                                vmem_limit_bytes=vmem)


def _split3(x):
    hi = x.astype(BF16)
    r = x - hi.astype(F32)
    mid = r.astype(BF16)
    lo = (r - mid.astype(F32)).astype(BF16)
    return hi, mid, lo


def _dot01(m, x):
    hi, mid, lo = _split3(x)
    return (jnp.dot(m, hi, preferred_element_type=F32) + jnp.dot(m, mid, preferred_element_type=F32)
            + jnp.dot(m, lo, preferred_element_type=F32))


def _dot_split(a, w_ref):
    a_hi = a.astype(BF16)
    a_lo = (a - a_hi.astype(F32)).astype(BF16)
    return (jnp.dot(a_hi, w_ref[0], preferred_element_type=F32) + jnp.dot(a_hi, w_ref[1], preferred_element_type=F32)
            + jnp.dot(a_lo, w_ref[0], preferred_element_type=F32))


def _hi_lo(w):
    hi = w.astype(BF16)
    return jnp.stack([hi, (w - hi.astype(F32)).astype(BF16)])


def _rms(x):
    return x * lax.rsqrt(jnp.mean(x * x, axis=-1, keepdims=True) + EPS)


def _lane_pick(x, lane, idx):
    return jnp.sum(jnp.where(lane == idx, x, 0.0), axis=-1, keepdims=True)


def _ada_kernel(c_ref, w_ref, b_ref, o_ref):
    c = c_ref[...]
    s = c * jax.nn.sigmoid(c)
    w = w_ref[...]
    w_hi = w.astype(BF16)
    w_lo = (w - w_hi.astype(F32)).astype(BF16)
    s_hi = s.astype(BF16)
    s_lo = (s - s_hi.astype(F32)).astype(BF16)
    o_ref[...] = (jnp.dot(s_hi, w_hi, preferred_element_type=F32) + jnp.dot(s_hi, w_lo, preferred_element_type=F32)
                  + jnp.dot(s_lo, w_hi, preferred_element_type=F32) + b_ref[...])


def _adaln(cvec, w, b):
    tn = 1536
    return pl.pallas_call(
        _ada_kernel,
        grid=(6 * D // tn,),
        in_specs=[pl.BlockSpec((8, D), lambda j: (0, 0)),
                  pl.BlockSpec((D, tn), lambda j: (0, j)),
                  pl.BlockSpec((1, tn), lambda j: (0, j))],
        out_specs=pl.BlockSpec((8, tn), lambda j: (0, j)),
        out_shape=jax.ShapeDtypeStruct((8, 6 * D), F32),
        compiler_params=_cparams(1, VMEM_LIMIT),
        name="adaln",
    )(cvec, w, b)


def _keys(z, lb):
    sig = jax.nn.sigmoid(z)
    logf = jnp.log(lb + (1.0 - lb) * sig)
    k = (1.0 - lb) * jax.nn.sigmoid(-z)
    return k, logf


def _ctx_kernel(ctx_ref, g_ref, sh_ref, sc_ref, w_ref, lb_ref, sf_ref, sb_ref):
    h = _rms(ctx_ref[...]) * g_ref[...]
    h = h * (1.0 + sc_ref[...]) + sh_ref[...]
    p = jnp.dot(h.astype(BF16), w_ref[...], preferred_element_type=F32)
    zf, zb, v = p[:, :KD], p[:, KD:2 * KD], p[:, 2 * KD:]
    kf, lf = _keys(zf, lb_ref[0:1, :])
    kb, lbk = _keys(zb, lb_ref[1:2, :])
    r = lax.broadcasted_iota(I32, (CTX, CTX), 0)
    c = lax.broadcasted_iota(I32, (CTX, CTX), 1)
    tril = jnp.where(c <= r, 1.0, 0.0).astype(BF16)
    cf = _dot01(tril, lf)
    cb = _dot01(tril, lbk)
    kfd = (kf * jnp.exp(cf[CTX - 1:CTX, :] - cf)).astype(BF16)
    kbd = (kb * jnp.exp(cb - lbk)).astype(BF16)
    vb = v.astype(BF16)
    tn = (((0,), (0,)), ((), ()))
    for hh in range(H):
        hs = slice(hh * DK, (hh + 1) * DK)
        sf_ref[hh] = lax.dot_general(vb[:, hs], kfd[:, hs], tn, preferred_element_type=F32)
        sb_ref[hh] = lax.dot_general(vb[:, hs], kbd[:, hs], tn, preferred_element_type=F32)


def _context_states(ctx, g1, csh1, csc1, w_ctx, lbs):
    st = jax.ShapeDtypeStruct((B, H, DV, DK), F32)
    return pl.pallas_call(
        _ctx_kernel,
        grid=(B,),
        in_specs=[pl.BlockSpec((None, CTX, D), lambda b: (b, 0, 0)),
                  pl.BlockSpec((1, D), lambda b: (0, 0)),
                  pl.BlockSpec((1, D), lambda b: (0, 0)),
                  pl.BlockSpec((1, D), lambda b: (0, 0)),
                  pl.BlockSpec((D, 3 * KD), lambda b: (0, 0)),
                  pl.BlockSpec((2, KD), lambda b: (0, 0))],
        out_specs=(pl.BlockSpec((None, H, DV, DK), lambda b: (b, 0, 0, 0)),
                   pl.BlockSpec((None, H, DV, DK), lambda b: (b, 0, 0, 0))),
        out_shape=(st, st),
        compiler_params=_cparams(1, VMEM_LIMIT),
        name="ctx_states",
    )(ctx, g1, csh1, csc1, w_ctx, lbs)


def _pos_tile(er_ref, ec_ref, row0, nrow):
    lo = jnp.concatenate([jnp.broadcast_to(er_ref[pl.ds(row0 + i, 1), :], (GRID_W, D // 2))
                          for i in range(nrow)], axis=0)
    hi = jnp.concatenate([ec_ref[...]] * nrow, axis=0)
    return jnp.concatenate([lo, hi], axis=1)


def _inproj_kernel(x_ref, er_ref, ec_ref, g_ref, sh_ref, sc_ref, w_ref, oz_ref, o_ref, hx_ref):
    i = pl.program_id(0)
    j = pl.program_id(1)
    ni = pl.num_programs(0)
    nj = pl.num_programs(1)
    slot = i % 2

    def prepare(tile, dst):
        nrow = TM_IN // GRID_W
        row0 = (tile % (L // TM_IN)) * nrow
        h = _rms(x_ref[...] + _pos_tile(er_ref, ec_ref, row0, nrow)) * g_ref[...]
        hx_ref[dst] = (h * (1.0 + sc_ref[...]) + sh_ref[...]).astype(BF16)

    def project():
        return jnp.dot(hx_ref[slot], w_ref[...], preferred_element_type=F32)

    @pl.when((i == 0) & (j == 0))
    def _():
        prepare(0, 0)

    @pl.when(j == 0)
    def _():
        r = project()
        o_ref[:, :KD] = r[:, :KD].astype(BF16)
        oz_ref[:, :KD] = r[:, KD:]

    @pl.when(j == 1)
    def _():
        r = project()
        oz_ref[:, KD:] = r[:, :KD]
        o_ref[:, KD:] = r[:, KD:].astype(BF16)

    @pl.when((j > 1) & (j < nj - 1))
    def _():
        o_ref[...] = project().astype(BF16)

    @pl.when((j == nj - 1) & (i + 1 < ni))
    def _():
        o_ref[...] = project().astype(BF16)
        prepare(i + 1, 1 - slot)

    @pl.when((j == nj - 1) & (i + 1 == ni))
    def _():
        o_ref[...] = project().astype(BF16)


def _in_projection(x2, er, ec, g1, sh1, sc1, w_bf):
    tiles_per_batch = L // TM_IN
    n_i = T // TM_IN
    n_j = IN_W // TN_IN
    ahead = lambda i, j: jnp.minimum(i + jnp.where(j == n_j - 1, 1, 0), n_i - 1)
    return pl.pallas_call(
        _inproj_kernel,
        grid=(n_i, n_j),
        in_specs=[pl.BlockSpec((TM_IN, D), lambda i, j: (ahead(i, j), 0)),
                  pl.BlockSpec((L // GRID_W, D // 2), lambda i, j: (0, 0)),
                  pl.BlockSpec((GRID_W, D // 2), lambda i, j: (0, 0)),
                  pl.BlockSpec((1, D), lambda i, j: (0, 0)),
                  pl.BlockSpec((None, 1, D), lambda i, j: (ahead(i, j) // tiles_per_batch, 0, 0)),
                  pl.BlockSpec((None, 1, D), lambda i, j: (ahead(i, j) // tiles_per_batch, 0, 0)),
                  pl.BlockSpec((D, TN_IN), lambda i, j: (0, j))],
        out_specs=(pl.BlockSpec((TM_IN, TN_IN), lambda i, j: (i, 0)),
                   pl.BlockSpec((TM_IN, TN_IN), lambda i, j: (i, jnp.maximum(j - 1, 0)))),
        out_shape=(jax.ShapeDtypeStruct((T, 2 * KD), F32),
                   jax.ShapeDtypeStruct((T, IN_W - 2 * KD), BF16)),
        scratch_shapes=[pltpu.VMEM((2, TM_IN, D), BF16)],
        compiler_params=_cparams(2, VMEM_LIMIT),
        name="in_proj",
    )(x2, er, ec, g1, sh1, sc1, w_bf)


def _hgrn_kernel(reverse, readout, *refs):
    if readout:
        q_ref, z_ref, v_ref, lb_ref, s0_ref, of_ref, g_ref, ng_ref, o_ref, st_ref = refs
    else:
        q_ref, z_ref, v_ref, lb_ref, s0_ref, o_ref, st_ref = refs

    @pl.when(pl.program_id(0) == 0)
    def _():
        st_ref[...] = s0_ref[...]

    r = lax.broadcasted_iota(I32, (TH, TH), 0)
    c = lax.broadcasted_iota(I32, (TH, TH), 1)
    cb_shift = CB.bit_length() - 1
    same = jnp.right_shift(r, cb_shift) == jnp.right_shift(c, cb_shift)
    tri_mask = same & ((c >= r) if reverse else (c <= r))
    tri = jnp.where(tri_mask, 1.0, 0.0).astype(BF16)
    rblk = jnp.right_shift(r, cb_shift)
    cblk = jnp.right_shift(c, cb_shift)
    dist = (rblk - cblk) if not reverse else (cblk - rblk)
    for b in range(B):
        _hgrn_chunk(reverse, readout, b, tri, tri_mask, dist, refs)


def _hgrn_chunk(reverse, readout, b, tri, tri_mask, dist, refs):
    if readout:
        q_ref, z_ref, v_ref, lb_ref, s0_ref, of_ref, g_ref, ng_ref, o_ref, st_ref = refs
    else:
        q_ref, z_ref, v_ref, lb_ref, s0_ref, o_ref, st_ref = refs
    q = q_ref[b].astype(F32)
    v = v_ref[b]
    k, logf = _keys(z_ref[b], lb_ref[...])
    bl = _dot01(tri, logf)
    nt = (((1,), (1,)), ((), ()))
    tn = (((0,), (0,)), ((), ()))
    nblk = TH // CB
    e_row = 0 if reverse else CB - 1
    m_row = CB - 1 - CB // 2 if reverse else CB // 2
    tau = [bl[jb * CB + e_row:jb * CB + e_row + 1] for jb in range(nblk)]
    mid = [bl[jb * CB + m_row:jb * CB + m_row + 1] for jb in range(nblk)]
    rows = lambda vecs: jnp.concatenate([jnp.broadcast_to(x, (CB, KD)) for x in vecs], axis=0)
    mid_b = rows(mid)
    qd0 = (q * jnp.exp(bl - mid_b)).astype(BF16)
    kd0 = (k * jnp.exp(mid_b - bl)).astype(BF16)
    qs = q * jnp.exp(bl)
    ke = k * jnp.exp(rows(tau) - bl)
    order = list(range(nblk - 1, -1, -1)) if reverse else list(range(nblk))
    pre = [jnp.zeros((1, KD), F32)]
    for i in range(nblk):
        pre.append(pre[-1] + tau[order[i]])
    total = pre[nblk]
    entry = [None] * nblk
    leave = [None] * nblk
    gap = [[None] * nblk for _ in range(nblk)]
    for i, jb in enumerate(order):
        entry[jb] = jnp.exp(pre[i])
        leave[jb] = jnp.exp(total - pre[i + 1])
        for d in range(2, nblk):
            gap[d][jb] = jnp.exp(pre[i + d] - pre[i + 1]) if i + d < nblk else jnp.zeros((1, KD), F32)
    qc = (qs * rows(entry)).astype(BF16)
    kc = (ke * rows(leave)).astype(BF16)
    kx = jnp.concatenate([ke.astype(BF16)] + [(ke * rows(gap[d])).astype(BF16) for d in range(2, nblk)], axis=0)
    qsb = qs.astype(BF16)
    dec = jnp.exp(total)

    def blockdiag(x):
        first_head = lax.broadcasted_iota(I32, x.shape, 1) < DK
        zero = jnp.zeros_like(x)
        return jnp.concatenate([jnp.where(first_head, x, zero), jnp.where(first_head, zero, x)], axis=0)

    for hp in range(H // 2):
        ps = slice(2 * hp * DK, 2 * (hp + 1) * DK)
        sc0 = lax.dot_general(qd0[:, ps], blockdiag(kd0[:, ps]), nt, preferred_element_type=F32)
        scx = lax.dot_general(qsb[:, ps], blockdiag(kx[:, ps]), nt, preferred_element_type=F32)
        halves = []
        for hh in range(2):
            sc = jnp.where(tri_mask, sc0[:, hh * TH:(hh + 1) * TH], 0.0)
            base = hh * (nblk - 1) * TH
            for d in range(1, nblk):
                sc = jnp.where(dist == d, scx[:, base + (d - 1) * TH:base + d * TH], sc)
            halves.append(sc.astype(BF16))
        sc_pair = jnp.concatenate(halves, axis=1)
        st_a = st_ref[b, 2 * hp]
        st_b = st_ref[b, 2 * hp + 1]
        zst = jnp.zeros((DV, DK), BF16)
        st_pair = jnp.concatenate([jnp.concatenate([st_a.astype(BF16), zst], axis=1),
                                   jnp.concatenate([zst, st_b.astype(BF16)], axis=1)], axis=0)
        o_pair = (lax.dot_general(qc[:, ps], st_pair, nt, preferred_element_type=F32)
                  + jnp.dot(sc_pair, blockdiag(v[:, ps]), preferred_element_type=F32))
        upd = lax.dot_general(v[:, ps], kc[:, ps], tn, preferred_element_type=F32)
        st_ref[b, 2 * hp] = st_a * dec[:, ps][:, :DK] + upd[:DV, :DK]
        st_ref[b, 2 * hp + 1] = st_b * dec[:, ps][:, DK:] + upd[DV:, DK:]
        for hh in range(2):
            hs = slice((2 * hp + hh) * DK, (2 * hp + hh + 1) * DK)
            o_h = o_pair[:, hh * DV:(hh + 1) * DV]
            if readout:
                o_h = o_h + of_ref[b, :, hs]
                o_h = _rms(o_h) * ng_ref[...]
                gh = g_ref[b, :, hs].astype(F32)
                o_h = o_h * (gh * jax.nn.sigmoid(gh))
            o_ref[b, :, hs] = o_h


def _hgrn_scan(pz, p, lb_row, s0, reverse, o_f=None, norm_g=None):
    nch = L // TH
    chunk = (lambda c: nch - 1 - c) if reverse else (lambda c: c)
    col_spec = lambda j: pl.BlockSpec((B, TH, KD), lambda c: (0, chunk(c), j))
    in_specs = [col_spec(0), col_spec(1 if reverse else 0), col_spec(1),
                pl.BlockSpec((1, KD), lambda c: (0, 0)),
                pl.BlockSpec((B, H, DV, DK), lambda c: (0, 0, 0, 0))]
    p3 = p.reshape(B, L, p.shape[-1])
    args = [p3, pz.reshape(B, L, 2 * KD), p3, lb_row, s0]
    readout = o_f is not None
    if readout:
        in_specs += [col_spec(0), col_spec(2), pl.BlockSpec((1, DV), lambda c: (0, 0))]
        args += [o_f, p3, norm_g]
    return pl.pallas_call(
        functools.partial(_hgrn_kernel, reverse, readout),
        grid=(nch,),
        in_specs=in_specs,
        out_specs=col_spec(0),
        out_shape=jax.ShapeDtypeStruct((B, L, KD), F32),
        scratch_shapes=[pltpu.VMEM((B, H, DV, DK), F32)],
        compiler_params=_cparams(1, VMEM_LIMIT),
        name="hgrn_bwd_readout" if readout else "hgrn_fwd",
    )(*args)


def _hy_pre_kernel(v_ref, x1_ref, x0_ref, vp_ref, x1p_ref, x0p_ref, vn_ref, x1n_ref, x0n_ref,
                   w_ref, b_ref, vx_ref, x0o_ref):
    i = pl.program_id(1)
    first = i == 0
    last = i == pl.num_programs(1) - 1
    row = lax.broadcasted_iota(I32, (TM_HY, 1), 0)

    def conv(c_ref, p_ref, n_ref, col):
        x = c_ref[...].astype(F32)
        prev_row = jnp.where(first, 0.0, p_ref[...].astype(F32)[HALO - 1:HALO, :])
        next_row = jnp.where(last, 0.0, n_ref[...].astype(F32)[0:1, :])
        xm = jnp.where(row == 0, prev_row, pltpu.roll(x, 1, axis=0))
        xp = jnp.where(row == TM_HY - 1, next_row, pltpu.roll(x, TM_HY - 1, axis=0))
        cs = slice(col * HYW, (col + 1) * HYW)
        return xm * w_ref[0:1, cs] + x * w_ref[1:2, cs] + xp * w_ref[2:3, cs] + b_ref[:, cs]

    v = conv(v_ref, vp_ref, vn_ref, 0)
    x1 = conv(x1_ref, x1p_ref, x1n_ref, 1)
    x0 = conv(x0_ref, x0p_ref, x0n_ref, 2)
    vx_ref[...] = v * x1
    x0o_ref[...] = x0


def _hyena_pre(p, conv_w, conv_b):
    nt = L // TM_HY
    hb = TM_HY // HALO
    nhb = T // HALO
    cur = lambda col: pl.BlockSpec((TM_HY, HYW), lambda b, i: (b * nt + i, col))
    prv = lambda col: pl.BlockSpec((HALO, HYW), lambda b, i: (jnp.maximum((b * nt + i) * hb - 1, 0), col))
    nxt = lambda col: pl.BlockSpec((HALO, HYW), lambda b, i: (jnp.minimum((b * nt + i + 1) * hb, nhb - 1), col))
    c0 = 3
    out = jax.ShapeDtypeStruct((T, HYW), F32)
    return pl.pallas_call(
        _hy_pre_kernel,
        grid=(B, nt),
        in_specs=[cur(c0), cur(c0 + 1), cur(c0 + 2), prv(c0), prv(c0 + 1), prv(c0 + 2),
                  nxt(c0), nxt(c0 + 1), nxt(c0 + 2),
                  pl.BlockSpec((3, 3 * HYW), lambda b, i: (0, 0)),
                  pl.BlockSpec((1, 3 * HYW), lambda b, i: (0, 0))],
        out_specs=(pl.BlockSpec((TM_HY, HYW), lambda b, i: (b * nt + i, 0)),
                   pl.BlockSpec((TM_HY, HYW), lambda b, i: (b * nt + i, 0))),
        out_shape=(out, out),
        compiler_params=_cparams(2, VMEM_LIMIT),
        name="hyena_pre",
    )(p, p, p, p, p, p, p, p, p, conv_w, conv_b)


def _taps_dft_kernel(z_ref, w1_ref, b1_ref, f1_ref, w2_ref, b2_ref, f2_ref, w3a_ref, w3b_ref, dl_ref, fm_ref,
                     o_hbm, obuf, sem):
    g = pl.program_id(0)
    na = L // FFT_P
    half = z_ref.shape[0] // 2
    zt = z_ref[0:half, :]
    zb = z_ref[half:, :]
    lane = lax.broadcasted_iota(I32, zt.shape, 1)
    dot = _dot_split
    h = jnp.sin(f1_ref[...] * (dot(jnp.concatenate([zt, zb], axis=1), w1_ref) + b1_ref[...]))
    h = jnp.sin(f2_ref[...] * (dot(h, w2_ref) + b2_ref[...]))

    def copies(step):
        return [pltpu.make_async_copy(obuf.at[n, jj], o_hbm.at[n, :, step * FFT_BB + jj, :], sem)
                for n in range(2) for jj in range(FFT_BB)]

    @pl.when(g > 0)
    def _():
        for cp in copies(g - 1):
            cp.wait()

    for hi, (zz, w3_ref) in enumerate(((zt, w3a_ref), (zb, w3b_ref))):
        taps = dot(h, w3_ref)
        win = jnp.exp(-_lane_pick(zz, lane, 0) * dl_ref[...])
        fwd = (taps[:, :HYW] * win).astype(BF16)
        bwd = (taps[:, HYW:] * win * _lane_pick(zz, lane, HY_EMB)).astype(BF16)
        for bb in range(FFT_BB // 2):
            jj = hi * (FFT_BB // 2) + bb
            rows = slice(bb * na, (bb + 1) * na)
            obuf[0, jj] = jnp.dot(fm_ref[...], fwd[rows], preferred_element_type=F32)
            obuf[1, jj] = jnp.dot(fm_ref[...], bwd[rows], preferred_element_type=F32)

    for cp in copies(g):
        cp.start()

    @pl.when(g == pl.num_programs(0) - 1)
    def _():
        for cp in copies(g):
            cp.wait()


def _filter_taps_dft(zin, w1, b1, f1, w2, b2, f2, w3a, w3b, deltas, fmat):
    na = L // FFT_P
    tm = FFT_BB * na
    ln = V7X_LANES
    full = lambda shape: pl.BlockSpec(shape, lambda i: tuple(0 for _ in shape))
    return pl.pallas_call(
        _taps_dft_kernel,
        grid=(FFT_P // FFT_BB,),
        in_specs=[pl.BlockSpec((tm, ln), lambda i: (i, 0)),
                  full((2, 2 * ln, ln)), full((1, ln)), full((1, ln)),
                  full((2, ln, ln)), full((1, ln)), full((1, ln)),
                  full((2, ln, 2 * HYW)), full((2, ln, 2 * HYW)), full((1, HYW)), full((FFT_P, na))],
        out_specs=pl.BlockSpec(memory_space=pl.ANY),
        out_shape=jax.ShapeDtypeStruct((2, FFT_P, FFT_P, HYW), F32),
        scratch_shapes=[pltpu.VMEM((2, FFT_BB, FFT_P, HYW), F32), pltpu.SemaphoreType.DMA(())],
        compiler_params=_cparams(1, VMEM_LIMIT),
        name="filter_taps_dft",
    )(zin, w1, b1, f1, w2, b2, f2, w3a, w3b, deltas, fmat)


def _strided_dft_kernel(x_hbm, f_ref, o_hbm, xbuf, obuf, sem_in, sem_out):
    g = pl.program_id(0)
    ng = pl.num_programs(0)
    nb = FFT_P // FFT_BB

    def copies(grp, slot, inbound):
        n = grp // nb
        b0 = (grp % nb) * FFT_BB
        if inbound:
            return [pltpu.make_async_copy(x_hbm.at[n, :, b0 + jj, :], xbuf.at[slot, jj], sem_in.at[slot])
                    for jj in range(FFT_BB)]
        return [pltpu.make_async_copy(obuf.at[slot, jj], o_hbm.at[n, :, b0 + jj, :], sem_out.at[slot])
                for jj in range(FFT_BB)]

    def start(grp, slot, inbound):
        for cp in copies(grp, slot, inbound):
            cp.start()

    def wait(grp, slot, inbound):
        for cp in copies(grp, slot, inbound):
            cp.wait()

    @pl.when(g == 0)
    def _():
        start(0, 0, True)

    for slot in range(2):
        grp = 2 * g + slot
        if slot == 0:
            start(grp + 1, 1, True)
        else:
            @pl.when(g + 1 < ng)
            def _():
                start(grp + 1, 0, True)
        wait(grp, slot, True)

        @pl.when(g > 0)
        def _():
            wait(grp - 2, slot, False)

        for jj in range(FFT_BB):
            obuf[slot, jj] = jnp.dot(f_ref[...], xbuf[slot, jj].astype(BF16), preferred_element_type=F32)
        start(grp, slot, False)

    @pl.when(g + 1 == ng)
    def _():
        wait(2 * g, 0, False)
        wait(2 * g + 1, 1, False)


def _strided_dft(xv, fmat, name):
    n, kk = xv.shape[0], xv.shape[1]
    mm = fmat.shape[0]
    groups = n * (FFT_P // FFT_BB)
    return pl.pallas_call(
        _strided_dft_kernel,
        grid=(groups // 2,),
        in_specs=[pl.BlockSpec(memory_space=pl.ANY),
                  pl.BlockSpec((mm, kk), lambda g: (0, 0))],
        out_specs=pl.BlockSpec(memory_space=pl.ANY),
        out_shape=jax.ShapeDtypeStruct((n, mm, FFT_P, HYW), F32),
        scratch_shapes=[pltpu.VMEM((2, FFT_BB, kk, HYW), F32), pltpu.VMEM((2, FFT_BB, mm, HYW), F32),
                        pltpu.SemaphoreType.DMA((2,)), pltpu.SemaphoreType.DMA((2,))],
        compiler_params=_cparams(1, VMEM_LIMIT),
        name=name,
    )(xv, fmat)


def _cblock(mr, mi):
    return jnp.concatenate([jnp.concatenate([mr, -mi], axis=1), jnp.concatenate([mi, mr], axis=1)], axis=0)


FFT_H = FFT_P // 2


def _twiddled(fr_ref, fi_ref, tw):
    twr = tw[0:1, :]
    twi = tw[1:2, :]
    fr = fr_ref[...]
    fi = fi_ref[...]
    return fr * twr - fi * twi, fr * twi + fi * twr


def _cmul_rows(x, kr, ki):
    xr, xi = x[:FFT_P], x[FFT_P:]
    return jnp.concatenate([xr * kr - xi * ki, xr * ki + xi * kr], axis=0).astype(BF16)


_TN_DIMS = (((0,), (0,)), ((), ()))
_STAGE2_SPECS = [pl.BlockSpec((FFT_P, FFT_P), lambda d: (0, 0)),
                 pl.BlockSpec((FFT_P, FFT_P), lambda d: (0, 0)),
                 pl.BlockSpec((None, 2, FFT_P), lambda d: (d, 0, 0)),
                 pl.BlockSpec((2, FFT_P), lambda d: (0, 0))]


def _kspec_kernel(a_ref, fr_ref, fi_ref, tw_ref, twh_ref, o_ref, oh_ref):
    dd = pl.program_id(0)
    dot = functools.partial(jnp.dot, preferred_element_type=F32)

    def combine(xf, xb, out_ref):
        out_ref[0] = xf[:FFT_P] + xb[:FFT_P]
        out_ref[1] = xf[FFT_P:] - xb[FFT_P:]

    @pl.when(dd > 0)
    def _():
        rm = _cblock(*_twiddled(fr_ref, fi_ref, tw_ref[...])).astype(BF16)
        combine(dot(rm, a_ref[0].reshape(2 * FFT_P, HYW).astype(BF16)),
                dot(rm, a_ref[1].reshape(2 * FFT_P, HYW).astype(BF16)), o_ref)

    @pl.when(dd == 0)
    def _():
        for slot, tw, out_ref in ((0, tw_ref[...], o_ref), (1, twh_ref[...], oh_ref)):
            w = jnp.concatenate(_twiddled(fr_ref, fi_ref, tw), axis=0).astype(BF16)
            combine(dot(w, a_ref[0, slot].astype(BF16)), dot(w, a_ref[1, slot].astype(BF16)), out_ref)


def _kernel_spectrum(ak, fr, fi, tw, twh):
    return pl.pallas_call(
        _kspec_kernel,
        grid=(FFT_H,),
        in_specs=[pl.BlockSpec((2, None, 2, FFT_P, HYW), lambda d: (0, d, 0, 0, 0))] + _STAGE2_SPECS,
        out_specs=(pl.BlockSpec((2, None, FFT_P, HYW), lambda d: (0, d, 0, 0)),
                   pl.BlockSpec((2, FFT_P, HYW), lambda d: (0, 0, 0))),
        out_shape=(jax.ShapeDtypeStruct((2, FFT_H, FFT_P, HYW), F32),
                   jax.ShapeDtypeStruct((2, FFT_P, HYW), F32)),
        compiler_params=_cparams(1, VMEM_LIMIT),
        name="kernel_spectrum",
    )(ak, fr, fi, tw, twh)


def _mid_kernel(a_ref, kh_ref, khh_ref, fr_ref, fi_ref, tw_ref, twh_ref, o_ref):
    dd = pl.program_id(0)
    dot = functools.partial(jnp.dot, preferred_element_type=F32)
    dot_t = lambda w, y: lax.dot_general(w, y, _TN_DIMS, preferred_element_type=F32)

    @pl.when(dd > 0)
    def _():
        rm = _cblock(*_twiddled(fr_ref, fi_ref, tw_ref[...])).astype(BF16)
        for n in range(B):
            x = dot(rm, a_ref[n].reshape(2 * FFT_P, HYW).astype(BF16))
            y = _cmul_rows(x, kh_ref[0], kh_ref[1])
            o_ref[n] = dot_t(rm, y).reshape(2, FFT_P, HYW)

    @pl.when(dd == 0)
    def _():
        for slot, tw, k_ref in ((0, tw_ref[...], kh_ref), (1, twh_ref[...], khh_ref)):
            w = jnp.concatenate(_twiddled(fr_ref, fi_ref, tw), axis=0).astype(BF16)
            for n in range(B):
                y = _cmul_rows(dot(w, a_ref[n, slot].astype(BF16)), k_ref[0], k_ref[1])
                o_ref[n, slot] = dot_t(w, y)


def _fft_mid(au, kh, khh, fr, fi, tw, twh):
    pair = pl.BlockSpec((B, None, 2, FFT_P, HYW), lambda d: (0, d, 0, 0, 0))
    return pl.pallas_call(
        _mid_kernel,
        grid=(FFT_H,),
        in_specs=[pair,
                  pl.BlockSpec((2, None, FFT_P, HYW), lambda d: (0, d, 0, 0)),
                  pl.BlockSpec((2, FFT_P, HYW), lambda d: (0, 0, 0))] + _STAGE2_SPECS,
        out_specs=pair,
        out_shape=jax.ShapeDtypeStruct((B, FFT_H, 2, FFT_P, HYW), F32),
        compiler_params=_cparams(1, VMEM_LIMIT),
        name="fft_mid",
    )(au, kh, khh, fr, fi, tw, twh)


def _dft_tables():
    na = L // FFT_P
    a = np.arange(na)
    dd = np.arange(FFT_H)
    ang = 2.0 * np.pi * np.outer(dd, a) / FFT_P
    re_rows = np.cos(ang)
    im_rows = -np.sin(ang)
    im_rows[0] = np.cos(np.pi * a)
    f_first = np.stack([re_rows, im_rows], axis=1).reshape(FFT_P, na)
    gre = 2.0 * np.cos(ang)
    gim = -2.0 * np.sin(ang)
    gre[0] = 1.0
    gim[0] = np.cos(np.pi * a)
    g_last = np.stack([gre, gim], axis=1).reshape(FFT_P, na).T / FFT_N
    b = np.arange(FFT_P)
    angf = 2.0 * np.pi * np.outer(b, b) / FFT_P
    ang2 = 2.0 * np.pi * np.outer(np.arange(FFT_H + 1), b) / FFT_N
    tw = np.stack([np.cos(ang2), -np.sin(ang2)], axis=1)
    f32 = lambda x: jnp.asarray(x.astype(np.float32))
    return (f32(f_first).astype(BF16), f32(g_last).astype(BF16), f32(np.cos(angf)), f32(-np.sin(angf)),
            f32(tw[:FFT_H]), f32(tw[FFT_H]))


def _merge_kernel(ya_ref, cv_ref, vx_ref, x0_ref, ga_ref, gb_ref, x_ref, er_ref, ec_ref, gt1_ref, hyd_ref,
                  wpa_ref, wpb_ref, wo_ref, g2_ref, sh2_ref, sc2_ref, wr_ref, br_ref,
                  x1_ref, t2_ref, lg_ref):
    vx = vx_ref[...]
    yb = x0_ref[...] * (cv_ref[...] + vx * hyd_ref[...])
    pa = jnp.dot(ya_ref[...].astype(BF16), wpa_ref[...], preferred_element_type=F32)
    pb = jnp.dot(yb.astype(BF16), wpb_ref[...], preferred_element_type=F32)
    mixed = (jax.nn.sigmoid(ga_ref[...].astype(F32)) * pa
             + jax.nn.sigmoid(gb_ref[...].astype(F32)) * pb)
    xm = jnp.dot(mixed.astype(BF16), wo_ref[...], preferred_element_type=F32)
    nrow = TM_MG // GRID_W
    row0 = (pl.program_id(0) % (L // TM_MG)) * nrow
    x1 = x_ref[...] + _pos_tile(er_ref, ec_ref, row0, nrow) + gt1_ref[...] * xm
    x1_ref[...] = x1
    t2 = _rms(x1) * g2_ref[...]
    t2 = t2 * (1.0 + sc2_ref[...]) + sh2_ref[...]
    t2_ref[...] = t2
    t_hi = t2.astype(BF16)
    t_lo = (t2 - t_hi.astype(F32)).astype(BF16)
    rr = (jnp.dot(t_hi, wr_ref[...], preferred_element_type=F32)
          + jnp.dot(t_lo, wr_ref[...], preferred_element_type=F32))
    lg_ref[...] = rr[:, :V7X_LANES] + rr[:, V7X_LANES:] + br_ref[...]


def _merge(ya, cv, vx, x0c, p, x2, er, ec, gt1, hyd, wpa, wpb, wo, g2, sh2, sc2, wr, br):
    tpb = L // TM_MG
    half = lambda: pl.BlockSpec((TM_MG, HYW), lambda i: (i, 0))
    full = lambda shape: pl.BlockSpec(shape, lambda i: tuple(0 for _ in shape))
    perb = lambda: pl.BlockSpec((None, 1, D), lambda i: (i // tpb, 0, 0))
    return pl.pallas_call(
        _merge_kernel,
        grid=(T // TM_MG,),
        in_specs=[half(), half(), half(), half(),
                  pl.BlockSpec((TM_MG, D), lambda i: (i, 3)),
                  pl.BlockSpec((TM_MG, D), lambda i: (i, 4)),
                  pl.BlockSpec((TM_MG, D), lambda i: (i, 0)),
                  full((L // GRID_W, D // 2)), full((GRID_W, D // 2)),
                  perb(), full((1, HYW)),
                  full((KD, D)), full((HYW, D)), full((D, D)),
                  full((1, D)), perb(), perb(),
                  full((D, 2 * V7X_LANES)), full((1, V7X_LANES))],
        out_specs=(pl.BlockSpec((TM_MG, D), lambda i: (i, 0)),
                   pl.BlockSpec((TM_MG, D), lambda i: (i, 0)),
                   pl.BlockSpec((TM_MG, V7X_LANES), lambda i: (i, 0))),
        out_shape=(jax.ShapeDtypeStruct((T, D), F32), jax.ShapeDtypeStruct((T, D), F32),
                   jax.ShapeDtypeStruct((T, V7X_LANES), F32)),
        compiler_params=_cparams(1, VMEM_LIMIT),
        name="merge",
    )(ya, cv, vx, x0c, p, p, x2, er, ec, gt1, hyd, wpa, wpb, wo, g2, sh2, sc2, wr, br)


def _route_kernel(lg_ref, info_ref, cnt_ref):
    @pl.when(pl.program_id(0) == 0)
    def _():
        cnt_ref[...] = jnp.zeros_like(cnt_ref)

    lg = lg_ref[...]
    lane = lax.broadcasted_iota(I32, lg.shape, 1)
    lanef = lane.astype(F32)
    neg = -1e30
    big = 1e9
    is_g = (lane >= NEXP) & (lane < NEXP + NGRP)
    gl = jnp.where(is_g, lg, neg)
    ge = jnp.where(is_g, jnp.exp(gl - jnp.max(gl, axis=-1, keepdims=True)), 0.0)
    pg = ge / jnp.sum(ge, axis=-1, keepdims=True)
    p_top_g = jnp.max(pg, axis=-1, keepdims=True)
    gidx = jnp.min(jnp.where(is_g & (pg == p_top_g), lanef, big), axis=-1, keepdims=True)
    g_sel = gidx.astype(I32) - NEXP
    emask = (lane < NEXP) & (jnp.right_shift(lane, NEPG.bit_length() - 1) == g_sel)
    el = jnp.where(emask, lg, neg)
    ee = jnp.where(emask, jnp.exp(el - jnp.max(el, axis=-1, keepdims=True)), 0.0)
    pe = ee / jnp.sum(ee, axis=-1, keepdims=True)
    p1 = jnp.max(jnp.where(emask, pe, -1.0), axis=-1, keepdims=True)
    i1 = jnp.min(jnp.where(emask & (pe == p1), lanef, big), axis=-1, keepdims=True)
    rest = emask & (lanef != i1)
    p2 = jnp.max(jnp.where(rest, pe, -1.0), axis=-1, keepdims=True)
    i2 = jnp.min(jnp.where(rest & (pe == p2), lanef, big), axis=-1, keepdims=True)
    wsum = p1 + p2
    w1 = p_top_g * p1 / wsum
    w2 = p_top_g * p2 / wsum
    sel1 = lanef == i1
    sel2 = lanef == i2
    oh = jnp.where(sel1 | sel2, 1.0, 0.0)
    r = lax.broadcasted_iota(I32, (TR, TR), 0)
    c = lax.broadcasted_iota(I32, (TR, TR), 1)
    stril = jnp.where(c < r, 1.0, 0.0).astype(BF16)
    before = jnp.dot(stril, oh.astype(BF16), preferred_element_type=F32) + cnt_ref[...]
    r1 = jnp.sum(jnp.where(sel1, before, 0.0), axis=-1, keepdims=True)
    r2 = jnp.sum(jnp.where(sel2, before, 0.0), axis=-1, keepdims=True)
    cnt_ref[...] += jnp.sum(oh, axis=0, keepdims=True)
    info = jnp.where(lane == 0, i1, jnp.where(lane == 1, r1, jnp.where(lane == 2, i2, jnp.where(
        lane == 3, r2, jnp.where(lane == 4, w1, jnp.where(lane == 5, w2, 0.0))))))
    info_ref[...] = info


def _route(lg):
    return pl.pallas_call(
        _route_kernel,
        grid=(T // TR,),
        in_specs=[pl.BlockSpec((TR, V7X_LANES), lambda i: (i, 0))],
        out_specs=(pl.BlockSpec((TR, V7X_LANES), lambda i: (i, 0)),
                   pl.BlockSpec((1, V7X_LANES), lambda i: (0, 0))),
        out_shape=(jax.ShapeDtypeStruct((T, V7X_LANES), F32), jax.ShapeDtypeStruct((1, V7X_LANES), F32)),
        compiler_params=_cparams(1, VMEM_LIMIT),
        name="route",
    )(lg)


def _positions_kernel(info_ref, st_ref, o_ref):
    info = info_ref[...]
    lane = lax.broadcasted_iota(I32, info.shape, 1)
    lanef = lane.astype(F32)
    st = st_ref[...]
    row = lambda e_lane, r_lane: (jnp.sum(jnp.where(lanef == _lane_pick(info, lane, e_lane), st, 0.0),
                                          axis=-1, keepdims=True) + _lane_pick(info, lane, r_lane))
    o_ref[...] = jnp.where(lane == 0, row(0, 1), jnp.where(lane == 1, row(2, 3), 0.0)).astype(I32)


def _positions(info, starts_row):
    return pl.pallas_call(
        _positions_kernel,
        grid=(T // TPOS,),
        in_specs=[pl.BlockSpec((TPOS, V7X_LANES), lambda i: (i, 0)),
                  pl.BlockSpec((1, V7X_LANES), lambda i: (0, 0))],
        out_specs=pl.BlockSpec((TPOS, V7X_LANES), lambda i: (i, 0)),
        out_shape=jax.ShapeDtypeStruct((T, V7X_LANES), I32),
        compiler_params=_cparams(1, VMEM_LIMIT),
        name="positions",
    )(info, starts_row)


def _scatter_kernel(pos_ref, zt_ref, t2_hbm, zeros_hbm, xs_hbm, tbuf, fsem, sem, zsem):
    i = pl.program_id(0)

    def zcopy(row):
        start = pl.multiple_of(jnp.maximum(row, 0), TE)
        return pltpu.make_async_copy(zeros_hbm, xs_hbm.at[pl.ds(start, TE)], zsem)

    @pl.when(i == 0)
    def _():
        def ztail(start, e, carry):
            @pl.when(zt_ref[0, e] >= 0)
            def _():
                cp = zcopy(zt_ref[0, e])
                cp.start() if start else cp.wait()
            return carry

        lax.fori_loop(0, NEXP, functools.partial(ztail, True), 0)
        lax.fori_loop(0, NEXP, functools.partial(ztail, False), 0)

        def zrest(start, tile, carry):
            cp = zcopy(tile * TE)
            cp.start() if start else cp.wait()
            return carry

        lax.fori_loop(zt_ref[0, NA_OFF], NT_EXP, functools.partial(zrest, True), 0)
        lax.fori_loop(zt_ref[0, NA_OFF], NT_EXP, functools.partial(zrest, False), 0)

    n = pl.num_programs(0)
    slot = i % SCATTER_SLOTS

    def fetch(tile, s):
        return pltpu.make_async_copy(t2_hbm.at[pl.ds(pl.multiple_of(tile * TS, TS), TS)], tbuf.at[s], fsem.at[s])

    def wait_rows(s):
        _wait_rows(TS, pltpu.make_async_copy(tbuf.at[s].at[pl.ds(0, 1)], xs_hbm.at[pl.ds(0, 1)], sem.at[s]))

    @pl.when(i == 0)
    def _():
        fetch(0, 0).start()

    @pl.when(i + 1 < n)
    def _():
        fetch(i + 1, (i + 1) % SCATTER_SLOTS).start()

    fetch(i, slot).wait()

    def row_copy(base, jj, kk):
        dst = pos_ref[0, 2 * (base + jj) + kk]
        return pltpu.make_async_copy(tbuf.at[slot].at[pl.ds(base, ROW_GROUP)].at[pl.ds(jj, 1)],
                                     xs_hbm.at[pl.ds(dst, 1)], sem.at[slot])

    _start_rows(TS, row_copy)

    @pl.when(i > 0)
    def _():
        wait_rows((i + SCATTER_SLOTS - 1) % SCATTER_SLOTS)

    @pl.when(i + 1 == n)
    def _():
        wait_rows(slot)


def _start_rows(n_rows, row_copy):
    def group(g, carry):
        base = pl.multiple_of(g * ROW_GROUP, ROW_GROUP)
        for jj in range(ROW_GROUP):
            for kk in range(2):
                row_copy(base, jj, kk).start(priority=kk)
        return carry

    lax.fori_loop(0, n_rows // ROW_GROUP, group, 0)


def _wait_rows(n_rows, one_row_copy):
    def drain(j, carry):
        one_row_copy.wait()
        one_row_copy.wait()
        return carry

    lax.fori_loop(0, n_rows, drain, 0, unroll=ROW_UNROLL)


def _scatter_rows(pos3, meta, t2, zeros_tile):
    return pl.pallas_call(
        _scatter_kernel,
        grid=(T // TS,),
        in_specs=[pl.BlockSpec((None, 1, 2 * TS), lambda i: (i, 0, 0), memory_space=pltpu.SMEM),
                  pl.BlockSpec(memory_space=pltpu.SMEM),
                  pl.BlockSpec(memory_space=pl.ANY),
                  pl.BlockSpec(memory_space=pl.ANY)],
        out_specs=pl.BlockSpec(memory_space=pl.ANY),
        out_shape=jax.ShapeDtypeStruct((NP_ROWS, D), F32),
        scratch_shapes=[pltpu.VMEM((SCATTER_SLOTS, TS, D), F32), pltpu.SemaphoreType.DMA((SCATTER_SLOTS,)),
                        pltpu.SemaphoreType.DMA((SCATTER_SLOTS,)), pltpu.SemaphoreType.DMA(())],
        compiler_params=_cparams(1, VMEM_LIMIT),
        name="scatter_rows",
    )(pos3, meta, t2, zeros_tile)


def _expert_kernel(te_ref, nx_ref, sl_ref, na_ref, xs_ref, wg_hbm, wu_hbm, wd_hbm, ys_ref,
                   wgs_ref, wus_ref, wds_ref, wgb_ref, wub_ref, wdb_ref, sem):
    i = pl.program_id(0)
    active = i < na_ref[0]
    first = active & ((i == 0) | (te_ref[i] != te_ref[jnp.maximum(i - 1, 0)]))

    def fetch(e, slot):
        return [pltpu.make_async_copy(wg_hbm.at[e], wgs_ref.at[slot], sem.at[slot]),
                pltpu.make_async_copy(wu_hbm.at[e], wus_ref.at[slot], sem.at[slot]),
                pltpu.make_async_copy(wd_hbm.at[e], wds_ref.at[slot], sem.at[slot])]

    for slot in range(2):
        @pl.when(first & (sl_ref[i] == slot))
        def _():
            @pl.when(i == 0)
            def _():
                for cp in fetch(te_ref[i], slot):
                    cp.start()

            for cp in fetch(te_ref[i], slot):
                cp.wait()
            wgb_ref[...] = wgs_ref[slot].astype(BF16)
            wub_ref[...] = wus_ref[slot].astype(BF16)
            wdb_ref[...] = wds_ref[slot].astype(BF16)

            @pl.when(nx_ref[i] >= 0)
            def _():
                for cp in fetch(nx_ref[i], 1 - slot):
                    cp.start()

    @pl.when(active)
    def _():
        x = xs_ref[...].astype(BF16)
        g = jnp.dot(x, wgb_ref[...], preferred_element_type=F32)
        u = jnp.dot(x, wub_ref[...], preferred_element_type=F32)
        hid = (g * jax.nn.sigmoid(g) * u).astype(BF16)
        ys_ref[...] = jnp.dot(hid, wdb_ref[...], preferred_element_type=F32)

    @pl.when(jnp.logical_not(active))
    def _():
        ys_ref[...] = jnp.zeros_like(ys_ref)


def _experts(tile_expert, tile_next, tile_slot, n_active, xs, wg, wu, wd):
    rows = lambda i, *_: (i, 0)
    rows_in = lambda i, te, nx, sl, na: (jnp.minimum(i, na[0] - 1), 0)
    any_space = pl.BlockSpec(memory_space=pl.ANY)
    grid_spec = pltpu.PrefetchScalarGridSpec(
        num_scalar_prefetch=4,
        grid=(NT_EXP,),
        in_specs=[pl.BlockSpec((TE, D), rows_in), any_space, any_space, any_space],
        out_specs=pl.BlockSpec((TE, D), rows),
        scratch_shapes=[pltpu.VMEM((2, D, DEXP), F32), pltpu.VMEM((2, D, DEXP), F32), pltpu.VMEM((2, DEXP, D), F32),
                        pltpu.VMEM((D, DEXP), BF16), pltpu.VMEM((D, DEXP), BF16), pltpu.VMEM((DEXP, D), BF16),
                        pltpu.SemaphoreType.DMA((2,))],
    )
    return pl.pallas_call(
        _expert_kernel,
        grid_spec=grid_spec,
        out_shape=jax.ShapeDtypeStruct((NP_ROWS, D), F32),
        compiler_params=_cparams(1, VMEM_LIMIT),
        name="experts",
    )(tile_expert, tile_next, tile_slot, n_active, xs, wg, wu, wd)


def _combine_kernel(pos_ref, posn_ref, info_ref, x1_ref, gt2_ref, fg_ref, ys_hbm, o_ref, buf, sem):
    g = pl.program_id(0)
    ng = pl.num_programs(0)

    def gather(p_ref, half, slot):
        def row_copy(base, jj, kk):
            src = p_ref[0, 2 * (half * TC + base + jj) + kk]
            return pltpu.make_async_copy(ys_hbm.at[pl.ds(src, 1)],
                                         buf.at[slot, kk].at[pl.ds(base, ROW_GROUP)].at[pl.ds(jj, 1)],
                                         sem.at[slot])
        _start_rows(TC, row_copy)

    def finish(half, slot):
        _wait_rows(TC, pltpu.make_async_copy(ys_hbm.at[pl.ds(0, 1)], buf.at[slot, 0].at[pl.ds(0, 1)],
                                             sem.at[slot]))
        rows = slice(half * TC, (half + 1) * TC)
        info = info_ref[rows, :]
        lane = lax.broadcasted_iota(I32, info.shape, 1)
        moe = _lane_pick(info, lane, 4) * buf[slot, 0] + _lane_pick(info, lane, 5) * buf[slot, 1]
        x2 = x1_ref[rows, :] + gt2_ref[...] * moe
        o_ref[rows, :] = _rms(x2) * fg_ref[...]

    @pl.when(g == 0)
    def _():
        gather(pos_ref, 0, 0)

    gather(pos_ref, 1, 1)
    finish(0, 0)

    @pl.when(g + 1 < ng)
    def _():
        gather(posn_ref, 0, 0)

    finish(1, 1)


def _combine(pos3, info, x1, gt2, fg, ys):
    step = 2 * TC
    tpb = L // step
    nsteps = T // step
    return pl.pallas_call(
        _combine_kernel,
        grid=(nsteps,),
        in_specs=[pl.BlockSpec((None, 1, 2 * step), lambda i: (i, 0, 0), memory_space=pltpu.SMEM),
                  pl.BlockSpec((None, 1, 2 * step), lambda i: (jnp.minimum(i + 1, nsteps - 1), 0, 0),
                               memory_space=pltpu.SMEM),
                  pl.BlockSpec((step, V7X_LANES), lambda i: (i, 0)),
                  pl.BlockSpec((step, D), lambda i: (i, 0)),
                  pl.BlockSpec((None, 1, D), lambda i: (i // tpb, 0, 0)),
                  pl.BlockSpec((1, D), lambda i: (0, 0)),
                  pl.BlockSpec(memory_space=pl.ANY)],
        out_specs=pl.BlockSpec((step, D), lambda i: (i, 0)),
        out_shape=jax.ShapeDtypeStruct((T, D), F32),
        scratch_shapes=[pltpu.VMEM((2, 2, TC, D), F32), pltpu.SemaphoreType.DMA((2,))],
        compiler_params=_cparams(1, VMEM_LIMIT),
        name="combine",
    )(pos3, pos3, info, x1, gt2, fg, ys)


def _pos_tables():
    rows = L // GRID_W
    quarter = D // 4
    omega = 1.0 / (10000.0 ** (jnp.arange(quarter, dtype=F32) / quarter))

    def axis_emb(pos):
        a = pos[:, None] * omega[None, :]
        return jnp.concatenate([jnp.sin(a), jnp.cos(a)], axis=-1)

    er = axis_emb(jnp.arange(rows, dtype=F32))
    ec = axis_emb(jnp.arange(GRID_W, dtype=F32))
    return er, ec


def _filter_features():
    z = np.zeros((L, V7X_LANES), np.float64)
    bands = (HY_EMB - 1) // 2
    ang = (2.0 * np.pi * np.arange(L) / L)[:, None] * np.linspace(1e-4, bands - 1, bands)[None, :]
    z[:, 0] = np.linspace(0.0, 1.0, L)
    z[:, 1:1 + bands] = np.cos(ang)
    z[:, 1 + bands:HY_EMB] = -np.sin(ang)
    z[1:, HY_EMB] = 1.0
    z = z.reshape(L // FFT_P, FFT_P, V7X_LANES).transpose(1, 0, 2).reshape(L, V7X_LANES)
    return jnp.asarray(z.astype(np.float32))


def _pad2(a, rows, cols):
    return jnp.pad(a, ((0, rows - a.shape[0]), (0, cols - a.shape[1])))


def kernel(x, c, ctx, c_ctx, ada_w, ada_b, norm1_g, norm2_g, w_in, hgrn_lb, hgrn_norm_g, hy_conv_w, hy_conv_b, hy_filt_w1, hy_filt_b1, hy_filt_freq1, hy_filt_w2, hy_filt_b2, hy_filt_freq2, hy_filt_w3, hy_d, w_proj_a, w_proj_b, w_out, moe_router_g_w, moe_router_g_b, moe_router_e_w, moe_router_e_b, moe_w_gate, moe_w_up, moe_w_down, final_norm_g):
    cvec = jnp.zeros((8, D), F32).at[0:B].set(c).at[B].set(c_ctx)
    mod = _adaln(cvec, ada_w[0], ada_b[0][None, :])
    m6 = mod.reshape(8, 6, D)
    sh1, sc1, gt1, sh2, sc2, gt2 = [m6[0:B, k][:, None, :] for k in range(6)]
    csh1, csc1 = m6[B:B + 1, 0], m6[B:B + 1, 1]

    lbs = jnp.cumsum(jax.nn.softmax(hgrn_lb.astype(F32), axis=0), axis=0)[0]
    g1 = norm1_g[0][None, :]
    er, ec = _pos_tables()
    x2 = x.reshape(T, D)

    w_ctx = w_in[0][:, KD:4 * KD].astype(BF16)
    s_f, s_b = _context_states(ctx, g1, csh1, csc1, w_ctx, lbs)

    pz, p = _in_projection(x2, er, ec, g1, sh1, sc1, w_in[0].astype(BF16))

    o_f = _hgrn_scan(pz, p, lbs[0:1], s_f, False)
    y_a = _hgrn_scan(pz, p, lbs[1:2], s_b, True, o_f=o_f, norm_g=hgrn_norm_g[0][None, :])

    vx, x0c = _hyena_pre(p, hy_conv_w[0], hy_conv_b[0][None, :])
    deltas = jnp.abs(jnp.linspace(math.log(HY_DECAY_TARGET) / HY_SLOW_PCT,
                                  math.log(HY_DECAY_TARGET) / HY_FAST_PCT, HYW, dtype=F32))[None, :]
    ln = V7X_LANES
    fh = hy_filt_w2.shape[-1]
    blockdiag = lambda m: jnp.concatenate([_pad2(m, m.shape[0], 2 * m.shape[1]),
                                           jnp.pad(m, ((0, 0), (m.shape[1], 0)))], axis=0)
    twice = lambda v: jnp.concatenate([v, v])[None, :]
    w3 = hy_filt_w3[0]
    f_first, g_last, fr, fi, tw, twh = _dft_tables()
    na = L // FFT_P
    ak = _filter_taps_dft(
        _filter_features(),
        _hi_lo(blockdiag(_pad2(hy_filt_w1[0], ln, fh))), twice(hy_filt_b1[0]), twice(hy_filt_freq1[0]),
        _hi_lo(blockdiag(hy_filt_w2[0])), twice(hy_filt_b2[0]), twice(hy_filt_freq2[0]),
        _hi_lo(_pad2(w3, ln, 2 * HYW)), _hi_lo(jnp.pad(w3, ((fh, 0), (0, 0)))), deltas, f_first)
    kh, khh = _kernel_spectrum(ak.reshape(2, FFT_H, 2, FFT_P, HYW), fr, fi, tw, twh)
    au = _strided_dft(vx.reshape(B, na, FFT_P, HYW), f_first, "dft_first")
    bp = _fft_mid(au.reshape(B, FFT_H, 2, FFT_P, HYW), kh, khh, fr, fi, tw, twh)
    conv = _strided_dft(bp.reshape(B, FFT_P, FFT_P, HYW), g_last, "dft_last").reshape(T, HYW)

    wr = jnp.concatenate([jnp.transpose(moe_router_e_w[0], (1, 0, 2)).reshape(D, NEXP),
                          moe_router_g_w[0], jnp.zeros((D, V7X_LANES - NEXP - NGRP), F32)], axis=1)
    wr_hi = wr.astype(BF16)
    wr = jnp.concatenate([wr_hi, (wr - wr_hi.astype(F32)).astype(BF16)], axis=1)
    br = jnp.concatenate([moe_router_e_b[0].reshape(NEXP), moe_router_g_b[0],
                          jnp.zeros((V7X_LANES - NEXP - NGRP,), F32)])[None, :]
    x1, t2, lg = _merge(y_a.reshape(T, KD), conv, vx, x0c, p, x2, er, ec, gt1, hy_d[0][None, :],
                        w_proj_a[0].astype(BF16), w_proj_b[0].astype(BF16), w_out[0].astype(BF16),
                        norm2_g[0][None, :], sh2, sc2, wr, br)

    info, counts = _route(lg)
    cnt = counts[0, :NEXP].astype(I32)
    pc = ((cnt + TE - 1) // TE) * TE
    ends = jnp.cumsum(pc)
    starts = ends - pc
    n_active = (ends[-1] // TE).astype(I32)[None]
    tile_rows = jnp.arange(NT_EXP, dtype=I32) * TE
    tile_expert = jnp.minimum(jnp.sum((ends[None, :] <= tile_rows[:, None]).astype(I32), axis=1), NEXP - 1)
    meta = jnp.concatenate([jnp.where(pc > 0, ends - TE, -1), starts, n_active]).astype(I32)[None, :]
    starts_row = jnp.pad(starts.astype(F32), (0, V7X_LANES - NEXP))[None, :]
    pos3 = _positions(info, starts_row)[:, :2].reshape(T // TS, 1, 2 * TS)

    xs = _scatter_rows(pos3, meta, t2, jnp.zeros((TE, D), F32))
    eid = jnp.arange(NEXP, dtype=I32)
    nonempty = pc > 0
    later = jnp.where(nonempty[None, :] & (eid[None, :] > eid[:, None]), eid[None, :], NEXP)
    next_e = jnp.min(later, axis=1)
    next_e = jnp.where(next_e < NEXP, next_e, -1).astype(I32)
    slot_e = ((jnp.cumsum(nonempty.astype(I32)) - 1) % 2).astype(I32)
    ys = _experts(tile_expert, next_e[tile_expert], slot_e[tile_expert], n_active, xs,
                  moe_w_gate[0].reshape(NEXP, D, DEXP), moe_w_up[0].reshape(NEXP, D, DEXP),
                  moe_w_down[0].reshape(NEXP, DEXP, D))
    out = _combine(pos3, info, x1, gt2, final_norm_g[None, :], ys)
    return out.reshape(B, L, D)
```

```python
import functools
import math

import numpy as np
import jax
import jax.numpy as jnp
from jax import lax
from jax.experimental import pallas as pl
from jax.experimental.pallas import tpu as pltpu

F32 = jnp.float32
BF16 = jnp.bfloat16
I32 = jnp.int32

D = 1024
B = 2
L = 8192
T = B * L
CTX = 256
GRID_W = 64
EPS = 1e-6
H = 4
DK = 128
DV = 128
KD = H * DK
IN_W = 6144
HYW = 512
HY_EMB = 33
NGRP = 4
NEPG = 8
NEXP = NGRP * NEPG
DEXP = 512
HY_DECAY_TARGET = 1e-2
HY_FAST_PCT = 0.3
HY_SLOW_PCT = 1.5

V7X_LANES = 128
V7X_SUBLANES = 8
V7X_VMEM_BYTES = 64 * 1024 * 1024
VMEM_LIMIT = (3 * V7X_VMEM_BYTES) // 4

FFT_N = 2 * L
FFT_P = 128
FFT_BB = 8

TM_IN = 1024
TN_IN = 1024
TH = 128
CB = 32
TM_HY = 1024
HALO = 2 * V7X_SUBLANES
TM_MG = 512
TR = 512
TE = 512
TPOS = 2048
NP_ROWS = 2 * T + NEXP * TE
NT_EXP = NP_ROWS // TE
TS = 512
TC = TS // 2
SCATTER_SLOTS = 3
ROW_UNROLL = 8
ROW_GROUP = 32
ST_OFF = NEXP
NA_OFF = 2 * NEXP


def _cparams(n_axes, vmem=None):
    return pltpu.CompilerParams(dimension_semantics=("arbitrary",) * n_axes,
                                vmem_limit_bytes=vmem)


def _split3(x):
    hi = x.astype(BF16)
    r = x - hi.astype(F32)
    mid = r.astype(BF16)
    lo = (r - mid.astype(F32)).astype(BF16)
    return hi, mid, lo


def _dot01(m, x):
    hi, mid, lo = _split3(x)
    return (jnp.dot(m, hi, preferred_element_type=F32) + jnp.dot(m, mid, preferred_element_type=F32)
            + jnp.dot(m, lo, preferred_element_type=F32))


def _dot_split(a, w_ref):
    a_hi = a.astype(BF16)
    a_lo = (a - a_hi.astype(F32)).astype(BF16)
    return (jnp.dot(a_hi, w_ref[0], preferred_element_type=F32) + jnp.dot(a_hi, w_ref[1], preferred_element_type=F32)
            + jnp.dot(a_lo, w_ref[0], preferred_element_type=F32))


def _hi_lo(w):
    hi = w.astype(BF16)
    return jnp.stack([hi, (w - hi.astype(F32)).astype(BF16)])


def _rms(x):
    return x * lax.rsqrt(jnp.mean(x * x, axis=-1, keepdims=True) + EPS)


def _lane_pick(x, lane, idx):
    return jnp.sum(jnp.where(lane == idx, x, 0.0), axis=-1, keepdims=True)


def _ada_kernel(c_ref, w_ref, b_ref, o_ref):
    c = c_ref[...]
    s = c * jax.nn.sigmoid(c)
    w = w_ref[...]
    w_hi = w.astype(BF16)
    w_lo = (w - w_hi.astype(F32)).astype(BF16)
    s_hi = s.astype(BF16)
    s_lo = (s - s_hi.astype(F32)).astype(BF16)
    o_ref[...] = (jnp.dot(s_hi, w_hi, preferred_element_type=F32) + jnp.dot(s_hi, w_lo, preferred_element_type=F32)
                  + jnp.dot(s_lo, w_hi, preferred_element_type=F32) + b_ref[...])


def _adaln(cvec, w, b):
    tn = 1536
    return pl.pallas_call(
        _ada_kernel,
        grid=(6 * D // tn,),
        in_specs=[pl.BlockSpec((8, D), lambda j: (0, 0)),
                  pl.BlockSpec((D, tn), lambda j: (0, j)),
                  pl.BlockSpec((1, tn), lambda j: (0, j))],
        out_specs=pl.BlockSpec((8, tn), lambda j: (0, j)),
        out_shape=jax.ShapeDtypeStruct((8, 6 * D), F32),
        compiler_params=_cparams(1, VMEM_LIMIT),
        name="adaln",
    )(cvec, w, b)


def _keys(z, lb):
    sig = jax.nn.sigmoid(z)
    logf = jnp.log(lb + (1.0 - lb) * sig)
    k = (1.0 - lb) * jax.nn.sigmoid(-z)
    return k, logf


def _ctx_kernel(ctx_ref, g_ref, sh_ref, sc_ref, w_ref, lb_ref, sf_ref, sb_ref):
    h = _rms(ctx_ref[...]) * g_ref[...]
    h = h * (1.0 + sc_ref[...]) + sh_ref[...]
    p = jnp.dot(h.astype(BF16), w_ref[...], preferred_element_type=F32)
    zf, zb, v = p[:, :KD], p[:, KD:2 * KD], p[:, 2 * KD:]
    kf, lf = _keys(zf, lb_ref[0:1, :])
    kb, lbk = _keys(zb, lb_ref[1:2, :])
    r = lax.broadcasted_iota(I32, (CTX, CTX), 0)
    c = lax.broadcasted_iota(I32, (CTX, CTX), 1)
    tril = jnp.where(c <= r, 1.0, 0.0).astype(BF16)
    cf = _dot01(tril, lf)
    cb = _dot01(tril, lbk)
    kfd = (kf * jnp.exp(cf[CTX - 1:CTX, :] - cf)).astype(BF16)
    kbd = (kb * jnp.exp(cb - lbk)).astype(BF16)
    vb = v.astype(BF16)
    tn = (((0,), (0,)), ((), ()))
    for hh in range(H):
        hs = slice(hh * DK, (hh + 1) * DK)
        sf_ref[hh] = lax.dot_general(vb[:, hs], kfd[:, hs], tn, preferred_element_type=F32)
        sb_ref[hh] = lax.dot_general(vb[:, hs], kbd[:, hs], tn, preferred_element_type=F32)


def _context_states(ctx, g1, csh1, csc1, w_ctx, lbs):
    st = jax.ShapeDtypeStruct((B, H, DV, DK), F32)
    return pl.pallas_call(
        _ctx_kernel,
        grid=(B,),
        in_specs=[pl.BlockSpec((None, CTX, D), lambda b: (b, 0, 0)),
                  pl.BlockSpec((1, D), lambda b: (0, 0)),
                  pl.BlockSpec((1, D), lambda b: (0, 0)),
                  pl.BlockSpec((1, D), lambda b: (0, 0)),
                  pl.BlockSpec((D, 3 * KD), lambda b: (0, 0)),
                  pl.BlockSpec((2, KD), lambda b: (0, 0))],
        out_specs=(pl.BlockSpec((None, H, DV, DK), lambda b: (b, 0, 0, 0)),
                   pl.BlockSpec((None, H, DV, DK), lambda b: (b, 0, 0, 0))),
        out_shape=(st, st),
        compiler_params=_cparams(1, VMEM_LIMIT),
        name="ctx_states",
    )(ctx, g1, csh1, csc1, w_ctx, lbs)


def _pos_tile(er_ref, ec_ref, row0, nrow):
    lo = jnp.concatenate([jnp.broadcast_to(er_ref[pl.ds(row0 + i, 1), :], (GRID_W, D // 2))
                          for i in range(nrow)], axis=0)
    hi = jnp.concatenate([ec_ref[...]] * nrow, axis=0)
    return jnp.concatenate([lo, hi], axis=1)


def _inproj_kernel(x_ref, er_ref, ec_ref, g_ref, sh_ref, sc_ref, w_ref, oz_ref, o_ref, hx_ref):
    i = pl.program_id(0)
    j = pl.program_id(1)
    ni = pl.num_programs(0)
    nj = pl.num_programs(1)
    slot = i % 2

    def prepare(tile, dst):
        nrow = TM_IN // GRID_W
        row0 = (tile % (L // TM_IN)) * nrow
        h = _rms(x_ref[...] + _pos_tile(er_ref, ec_ref, row0, nrow)) * g_ref[...]
        hx_ref[dst] = (h * (1.0 + sc_ref[...]) + sh_ref[...]).astype(BF16)

    def project():
        return jnp.dot(hx_ref[slot], w_ref[...], preferred_element_type=F32)

    @pl.when((i == 0) & (j == 0))
    def _():
        prepare(0, 0)

    @pl.when(j == 0)
    def _():
        r = project()
        o_ref[:, :KD] = r[:, :KD].astype(BF16)
        oz_ref[:, :KD] = r[:, KD:]

    @pl.when(j == 1)
    def _():
        r = project()
        oz_ref[:, KD:] = r[:, :KD]
        o_ref[:, KD:] = r[:, KD:].astype(BF16)

    @pl.when((j > 1) & (j < nj - 1))
    def _():
        o_ref[...] = project().astype(BF16)

    @pl.when((j == nj - 1) & (i + 1 < ni))
    def _():
        o_ref[...] = project().astype(BF16)
        prepare(i + 1, 1 - slot)

    @pl.when((j == nj - 1) & (i + 1 == ni))
    def _():
        o_ref[...] = project().astype(BF16)


def _in_projection(x2, er, ec, g1, sh1, sc1, w_bf):
    tiles_per_batch = L // TM_IN
    n_i = T // TM_IN
    n_j = IN_W // TN_IN
    ahead = lambda i, j: jnp.minimum(i + jnp.where(j == n_j - 1, 1, 0), n_i - 1)
    return pl.pallas_call(
        _inproj_kernel,
        grid=(n_i, n_j),
        in_specs=[pl.BlockSpec((TM_IN, D), lambda i, j: (ahead(i, j), 0)),
                  pl.BlockSpec((L // GRID_W, D // 2), lambda i, j: (0, 0)),
                  pl.BlockSpec((GRID_W, D // 2), lambda i, j: (0, 0)),
                  pl.BlockSpec((1, D), lambda i, j: (0, 0)),
                  pl.BlockSpec((None, 1, D), lambda i, j: (ahead(i, j) // tiles_per_batch, 0, 0)),
                  pl.BlockSpec((None, 1, D), lambda i, j: (ahead(i, j) // tiles_per_batch, 0, 0)),
                  pl.BlockSpec((D, TN_IN), lambda i, j: (0, j))],
        out_specs=(pl.BlockSpec((TM_IN, TN_IN), lambda i, j: (i, 0)),
                   pl.BlockSpec((TM_IN, TN_IN), lambda i, j: (i, jnp.maximum(j - 1, 0)))),
        out_shape=(jax.ShapeDtypeStruct((T, 2 * KD), F32),
                   jax.ShapeDtypeStruct((T, IN_W - 2 * KD), BF16)),
        scratch_shapes=[pltpu.VMEM((2, TM_IN, D), BF16)],
        compiler_params=_cparams(2, VMEM_LIMIT),
        name="in_proj",
    )(x2, er, ec, g1, sh1, sc1, w_bf)


def _hgrn_kernel(reverse, readout, *refs):
    if readout:
        q_ref, z_ref, v_ref, lb_ref, s0_ref, of_ref, g_ref, ng_ref, o_ref, st_ref = refs
    else:
        q_ref, z_ref, v_ref, lb_ref, s0_ref, o_ref, st_ref = refs

    @pl.when(pl.program_id(0) == 0)
    def _():
        st_ref[...] = s0_ref[...]

    r = lax.broadcasted_iota(I32, (TH, TH), 0)
    c = lax.broadcasted_iota(I32, (TH, TH), 1)
    cb_shift = CB.bit_length() - 1
    same = jnp.right_shift(r, cb_shift) == jnp.right_shift(c, cb_shift)
    tri_mask = same & ((c >= r) if reverse else (c <= r))
    tri = jnp.where(tri_mask, 1.0, 0.0).astype(BF16)
    rblk = jnp.right_shift(r, cb_shift)
    cblk = jnp.right_shift(c, cb_shift)
    dist = (rblk - cblk) if not reverse else (cblk - rblk)
    for b in range(B):
        _hgrn_chunk(reverse, readout, b, tri, tri_mask, dist, refs)


def _hgrn_chunk(reverse, readout, b, tri, tri_mask, dist, refs):
    if readout:
        q_ref, z_ref, v_ref, lb_ref, s0_ref, of_ref, g_ref, ng_ref, o_ref, st_ref = refs
    else:
        q_ref, z_ref, v_ref, lb_ref, s0_ref, o_ref, st_ref = refs
    q = q_ref[b].astype(F32)
    v = v_ref[b]
    k, logf = _keys(z_ref[b], lb_ref[...])
    bl = _dot01(tri, logf)
    nt = (((1,), (1,)), ((), ()))
    tn = (((0,), (0,)), ((), ()))
    nblk = TH // CB
    e_row = 0 if reverse else CB - 1
    m_row = CB - 1 - CB // 2 if reverse else CB // 2
    tau = [bl[jb * CB + e_row:jb * CB + e_row + 1] for jb in range(nblk)]
    mid = [bl[jb * CB + m_row:jb * CB + m_row + 1] for jb in range(nblk)]
    rows = lambda vecs: jnp.concatenate([jnp.broadcast_to(x, (CB, KD)) for x in vecs], axis=0)
    mid_b = rows(mid)
    qd0 = (q * jnp.exp(bl - mid_b)).astype(BF16)
    kd0 = (k * jnp.exp(mid_b - bl)).astype(BF16)
    qs = q * jnp.exp(bl)
    ke = k * jnp.exp(rows(tau) - bl)
    order = list(range(nblk - 1, -1, -1)) if reverse else list(range(nblk))
    pre = [jnp.zeros((1, KD), F32)]
    for i in range(nblk):
        pre.append(pre[-1] + tau[order[i]])
    total = pre[nblk]
    entry = [None] * nblk
    leave = [None] * nblk
    gap = [[None] * nblk for _ in range(nblk)]
    for i, jb in enumerate(order):
        entry[jb] = jnp.exp(pre[i])
        leave[jb] = jnp.exp(total - pre[i + 1])
        for d in range(2, nblk):
            gap[d][jb] = jnp.exp(pre[i + d] - pre[i + 1]) if i + d < nblk else jnp.zeros((1, KD), F32)
    qc = (qs * rows(entry)).astype(BF16)
    kc = (ke * rows(leave)).astype(BF16)
    kx = jnp.concatenate([ke.astype(BF16)] + [(ke * rows(gap[d])).astype(BF16) for d in range(2, nblk)], axis=0)
    qsb = qs.astype(BF16)
    dec = jnp.exp(total)

    def blockdiag(x):
        first_head = lax.broadcasted_iota(I32, x.shape, 1) < DK
        zero = jnp.zeros_like(x)
        return jnp.concatenate([jnp.where(first_head, x, zero), jnp.where(first_head, zero, x)], axis=0)

    for hp in range(H // 2):
        ps = slice(2 * hp * DK, 2 * (hp + 1) * DK)
        sc0 = lax.dot_general(qd0[:, ps], blockdiag(kd0[:, ps]), nt, preferred_element_type=F32)
        scx = lax.dot_general(qsb[:, ps], blockdiag(kx[:, ps]), nt, preferred_element_type=F32)
        halves = []
        for hh in range(2):
            sc = jnp.where(tri_mask, sc0[:, hh * TH:(hh + 1) * TH], 0.0)
            base = hh * (nblk - 1) * TH
            for d in range(1, nblk):
                sc = jnp.where(dist == d, scx[:, base + (d - 1) * TH:base + d * TH], sc)
            halves.append(sc.astype(BF16))
        sc_pair = jnp.concatenate(halves, axis=1)
        st_a = st_ref[b, 2 * hp]
        st_b = st_ref[b, 2 * hp + 1]
        zst = jnp.zeros((DV, DK), BF16)
        st_pair = jnp.concatenate([jnp.concatenate([st_a.astype(BF16), zst], axis=1),
                                   jnp.concatenate([zst, st_b.astype(BF16)], axis=1)], axis=0)
        o_pair = (lax.dot_general(qc[:, ps], st_pair, nt, preferred_element_type=F32)
                  + jnp.dot(sc_pair, blockdiag(v[:, ps]), preferred_element_type=F32))
        upd = lax.dot_general(v[:, ps], kc[:, ps], tn, preferred_element_type=F32)
        st_ref[b, 2 * hp] = st_a * dec[:, ps][:, :DK] + upd[:DV, :DK]
        st_ref[b, 2 * hp + 1] = st_b * dec[:, ps][:, DK:] + upd[DV:, DK:]
        for hh in range(2):
            hs = slice((2 * hp + hh) * DK, (2 * hp + hh + 1) * DK)
            o_h = o_pair[:, hh * DV:(hh + 1) * DV]
            if readout:
                o_h = o_h + of_ref[b, :, hs]
                o_h = _rms(o_h) * ng_ref[...]
                gh = g_ref[b, :, hs].astype(F32)
                o_h = o_h * (gh * jax.nn.sigmoid(gh))
            o_ref[b, :, hs] = o_h.astype(o_ref.dtype)


def _hgrn_scan(pz, p, lb_row, s0, reverse, o_f=None, norm_g=None):
    nch = L // TH
    chunk = (lambda c: nch - 1 - c) if reverse else (lambda c: c)
    col_spec = lambda j: pl.BlockSpec((B, TH, KD), lambda c: (0, chunk(c), j))
    in_specs = [col_spec(0), col_spec(1 if reverse else 0), col_spec(1),
                pl.BlockSpec((1, KD), lambda c: (0, 0)),
                pl.BlockSpec((B, H, DV, DK), lambda c: (0, 0, 0, 0))]
    p3 = p.reshape(B, L, p.shape[-1])
    args = [p3, pz.reshape(B, L, 2 * KD), p3, lb_row, s0]
    readout = o_f is not None
    if readout:
        in_specs += [col_spec(0), col_spec(2), pl.BlockSpec((1, DV), lambda c: (0, 0))]
        args += [o_f, p3, norm_g]
    return pl.pallas_call(
        functools.partial(_hgrn_kernel, reverse, readout),
        grid=(nch,),
        in_specs=in_specs,
        out_specs=col_spec(0),
        out_shape=jax.ShapeDtypeStruct((B, L, KD), BF16 if readout else F32),
        scratch_shapes=[pltpu.VMEM((B, H, DV, DK), F32)],
        compiler_params=_cparams(1, VMEM_LIMIT),
        name="hgrn_bwd_readout" if readout else "hgrn_fwd",
    )(*args)


def _hy_pre_kernel(v_ref, x1_ref, x0_ref, vp_ref, x1p_ref, x0p_ref, vn_ref, x1n_ref, x0n_ref,
                   w_ref, b_ref, vx_ref, x0o_ref):
    i = pl.program_id(1)
    first = i == 0
    last = i == pl.num_programs(1) - 1
    row = lax.broadcasted_iota(I32, (TM_HY, 1), 0)

    def conv(c_ref, p_ref, n_ref, col):
        x = c_ref[...].astype(F32)
        prev_row = jnp.where(first, 0.0, p_ref[...].astype(F32)[HALO - 1:HALO, :])
        next_row = jnp.where(last, 0.0, n_ref[...].astype(F32)[0:1, :])
        xm = jnp.where(row == 0, prev_row, pltpu.roll(x, 1, axis=0))
        xp = jnp.where(row == TM_HY - 1, next_row, pltpu.roll(x, TM_HY - 1, axis=0))
        cs = slice(col * HYW, (col + 1) * HYW)
        return xm * w_ref[0:1, cs] + x * w_ref[1:2, cs] + xp * w_ref[2:3, cs] + b_ref[:, cs]

    v = conv(v_ref, vp_ref, vn_ref, 0)
    x1 = conv(x1_ref, x1p_ref, x1n_ref, 1)
    x0 = conv(x0_ref, x0p_ref, x0n_ref, 2)
    vx_ref[...] = v * x1
    x0o_ref[...] = x0.astype(x0o_ref.dtype)


def _hyena_pre(p, conv_w, conv_b):
    nt = L // TM_HY
    hb = TM_HY // HALO
    nhb = T // HALO
    cur = lambda col: pl.BlockSpec((TM_HY, HYW), lambda b, i: (b * nt + i, col))
    prv = lambda col: pl.BlockSpec((HALO, HYW), lambda b, i: (jnp.maximum((b * nt + i) * hb - 1, 0), col))
    nxt = lambda col: pl.BlockSpec((HALO, HYW), lambda b, i: (jnp.minimum((b * nt + i + 1) * hb, nhb - 1), col))
    c0 = 3
    out = jax.ShapeDtypeStruct((T, HYW), F32)
    return pl.pallas_call(
        _hy_pre_kernel,
        grid=(B, nt),
        in_specs=[cur(c0), cur(c0 + 1), cur(c0 + 2), prv(c0), prv(c0 + 1), prv(c0 + 2),
                  nxt(c0), nxt(c0 + 1), nxt(c0 + 2),
                  pl.BlockSpec((3, 3 * HYW), lambda b, i: (0, 0)),
                  pl.BlockSpec((1, 3 * HYW), lambda b, i: (0, 0))],
        out_specs=(pl.BlockSpec((TM_HY, HYW), lambda b, i: (b * nt + i, 0)),
                   pl.BlockSpec((TM_HY, HYW), lambda b, i: (b * nt + i, 0))),
        out_shape=(out, jax.ShapeDtypeStruct((T, HYW), BF16)),
        compiler_params=_cparams(2, VMEM_LIMIT),
        name="hyena_pre",
    )(p, p, p, p, p, p, p, p, p, conv_w, conv_b)


def _taps_dft_kernel(z_ref, w1_ref, b1_ref, f1_ref, w2_ref, b2_ref, f2_ref, w3a_ref, w3b_ref, dl_ref, fm_ref,
                     o_hbm, obuf, sem):
    g = pl.program_id(0)
    na = L // FFT_P
    half = z_ref.shape[0] // 2
    zt = z_ref[0:half, :]
    zb = z_ref[half:, :]
    lane = lax.broadcasted_iota(I32, zt.shape, 1)
    dot = _dot_split
    h = jnp.sin(f1_ref[...] * (dot(jnp.concatenate([zt, zb], axis=1), w1_ref) + b1_ref[...]))
    h = jnp.sin(f2_ref[...] * (dot(h, w2_ref) + b2_ref[...]))

    def copies(step):
        return [pltpu.make_async_copy(obuf.at[n, jj], o_hbm.at[n, :, step * FFT_BB + jj, :], sem)
                for n in range(2) for jj in range(FFT_BB)]

    @pl.when(g > 0)
    def _():
        for cp in copies(g - 1):
            cp.wait()

    for hi, (zz, w3_ref) in enumerate(((zt, w3a_ref), (zb, w3b_ref))):
        taps = dot(h, w3_ref)
        win = jnp.exp(-_lane_pick(zz, lane, 0) * dl_ref[...])
        fwd = (taps[:, :HYW] * win).astype(BF16)
        bwd = (taps[:, HYW:] * win * _lane_pick(zz, lane, HY_EMB)).astype(BF16)
        for bb in range(FFT_BB // 2):
            jj = hi * (FFT_BB // 2) + bb
            rows = slice(bb * na, (bb + 1) * na)
            obuf[0, jj] = jnp.dot(fm_ref[...], fwd[rows], preferred_element_type=F32)
            obuf[1, jj] = jnp.dot(fm_ref[...], bwd[rows], preferred_element_type=F32)

    for cp in copies(g):
        cp.start()

    @pl.when(g == pl.num_programs(0) - 1)
    def _():
        for cp in copies(g):
            cp.wait()


def _filter_taps_dft(zin, w1, b1, f1, w2, b2, f2, w3a, w3b, deltas, fmat):
    na = L // FFT_P
    tm = FFT_BB * na
    ln = V7X_LANES
    full = lambda shape: pl.BlockSpec(shape, lambda i: tuple(0 for _ in shape))
    return pl.pallas_call(
        _taps_dft_kernel,
        grid=(FFT_P // FFT_BB,),
        in_specs=[pl.BlockSpec((tm, ln), lambda i: (i, 0)),
                  full((2, 2 * ln, ln)), full((1, ln)), full((1, ln)),
                  full((2, ln, ln)), full((1, ln)), full((1, ln)),
                  full((2, ln, 2 * HYW)), full((2, ln, 2 * HYW)), full((1, HYW)), full((FFT_P, na))],
        out_specs=pl.BlockSpec(memory_space=pl.ANY),
        out_shape=jax.ShapeDtypeStruct((2, FFT_P, FFT_P, HYW), F32),
        scratch_shapes=[pltpu.VMEM((2, FFT_BB, FFT_P, HYW), F32), pltpu.SemaphoreType.DMA(())],
        compiler_params=_cparams(1, VMEM_LIMIT),
        name="filter_taps_dft",
    )(zin, w1, b1, f1, w2, b2, f2, w3a, w3b, deltas, fmat)


def _strided_dft_kernel(x_hbm, f_ref, o_hbm, xbuf, obuf, sem_in, sem_out):
    g = pl.program_id(0)
    ng = pl.num_programs(0)
    nb = FFT_P // FFT_BB

    def copies(grp, slot, inbound):
        n = grp // nb
        b0 = (grp % nb) * FFT_BB
        if inbound:
            return [pltpu.make_async_copy(x_hbm.at[n, :, b0 + jj, :], xbuf.at[slot, jj], sem_in.at[slot])
                    for jj in range(FFT_BB)]
        return [pltpu.make_async_copy(obuf.at[slot, jj], o_hbm.at[n, :, b0 + jj, :], sem_out.at[slot])
                for jj in range(FFT_BB)]

    def start(grp, slot, inbound):
        for cp in copies(grp, slot, inbound):
            cp.start()

    def wait(grp, slot, inbound):
        for cp in copies(grp, slot, inbound):
            cp.wait()

    @pl.when(g == 0)
    def _():
        start(0, 0, True)

    for slot in range(2):
        grp = 2 * g + slot
        if slot == 0:
            start(grp + 1, 1, True)
        else:
            @pl.when(g + 1 < ng)
            def _():
                start(grp + 1, 0, True)
        wait(grp, slot, True)

        @pl.when(g > 0)
        def _():
            wait(grp - 2, slot, False)

        for jj in range(FFT_BB):
            obuf[slot, jj] = jnp.dot(f_ref[...], xbuf[slot, jj].astype(BF16), preferred_element_type=F32)
        start(grp, slot, False)

    @pl.when(g + 1 == ng)
    def _():
        wait(2 * g, 0, False)
        wait(2 * g + 1, 1, False)


def _strided_dft(xv, fmat, name):
    n, kk = xv.shape[0], xv.shape[1]
    mm = fmat.shape[0]
    groups = n * (FFT_P // FFT_BB)
    return pl.pallas_call(
        _strided_dft_kernel,
        grid=(groups // 2,),
        in_specs=[pl.BlockSpec(memory_space=pl.ANY),
                  pl.BlockSpec((mm, kk), lambda g: (0, 0))],
        out_specs=pl.BlockSpec(memory_space=pl.ANY),
        out_shape=jax.ShapeDtypeStruct((n, mm, FFT_P, HYW), F32),
        scratch_shapes=[pltpu.VMEM((2, FFT_BB, kk, HYW), F32), pltpu.VMEM((2, FFT_BB, mm, HYW), F32),
                        pltpu.SemaphoreType.DMA((2,)), pltpu.SemaphoreType.DMA((2,))],
        compiler_params=_cparams(1, VMEM_LIMIT),
        name=name,
    )(xv, fmat)


def _cblock(mr, mi):
    return jnp.concatenate([jnp.concatenate([mr, -mi], axis=1), jnp.concatenate([mi, mr], axis=1)], axis=0)


FFT_H = FFT_P // 2
KSPEC_DD = 4


def _twiddled(fr_ref, fi_ref, tw):
    twr = tw[0:1, :]
    twi = tw[1:2, :]
    fr = fr_ref[...]
    fi = fi_ref[...]
    return fr * twr - fi * twi, fr * twi + fi * twr


def _cmul_rows(x, kr, ki):
    xr, xi = x[:FFT_P], x[FFT_P:]
    return jnp.concatenate([xr * kr - xi * ki, xr * ki + xi * kr], axis=0).astype(BF16)


_TN_DIMS = (((0,), (0,)), ((), ()))
_STAGE2_SPECS = [pl.BlockSpec((FFT_P, FFT_P), lambda d: (0, 0)),
                 pl.BlockSpec((FFT_P, FFT_P), lambda d: (0, 0)),
                 pl.BlockSpec((None, 2, FFT_P), lambda d: (d, 0, 0)),
                 pl.BlockSpec((2, FFT_P), lambda d: (0, 0))]


def _kspec_kernel(a_ref, fr_ref, fi_ref, tw_ref, twh_ref, o_ref, oh_ref):
    g = pl.program_id(0)
    dot = functools.partial(jnp.dot, preferred_element_type=F32)

    def combine(xf, xb, store):
        store(0, xf[:FFT_P] + xb[:FFT_P])
        store(1, xf[FFT_P:] - xb[FFT_P:])

    def general(u):
        def store(ri, val):
            o_ref[ri, u] = val
        rm = _cblock(*_twiddled(fr_ref, fi_ref, tw_ref[u])).astype(BF16)
        combine(dot(rm, a_ref[0, u].reshape(2 * FFT_P, HYW).astype(BF16)),
                dot(rm, a_ref[1, u].reshape(2 * FFT_P, HYW).astype(BF16)), store)

    @pl.when(g > 0)
    def _():
        general(0)

    @pl.when(g == 0)
    def _():
        def store_first(ri, val):
            o_ref[ri, 0] = val

        def store_half(ri, val):
            oh_ref[ri] = val
        for slot, tw, store in ((0, tw_ref[0], store_first), (1, twh_ref[...], store_half)):
            w = jnp.concatenate(_twiddled(fr_ref, fi_ref, tw), axis=0).astype(BF16)
            combine(dot(w, a_ref[0, 0, slot].astype(BF16)), dot(w, a_ref[1, 0, slot].astype(BF16)), store)

    for u in range(1, KSPEC_DD):
        general(u)


def _kernel_spectrum(ak, fr, fi, tw, twh):
    same = lambda shape: pl.BlockSpec(shape, lambda g: tuple(0 for _ in shape))
    return pl.pallas_call(
        _kspec_kernel,
        grid=(FFT_H // KSPEC_DD,),
        in_specs=[pl.BlockSpec((2, KSPEC_DD, 2, FFT_P, HYW), lambda g: (0, g, 0, 0, 0)),
                  same((FFT_P, FFT_P)), same((FFT_P, FFT_P)),
                  pl.BlockSpec((KSPEC_DD, 2, FFT_P), lambda g: (g, 0, 0)), same((2, FFT_P))],
        out_specs=(pl.BlockSpec((2, KSPEC_DD, FFT_P, HYW), lambda g: (0, g, 0, 0)),
                   pl.BlockSpec((2, FFT_P, HYW), lambda g: (0, 0, 0))),
        out_shape=(jax.ShapeDtypeStruct((2, FFT_H, FFT_P, HYW), F32),
                   jax.ShapeDtypeStruct((2, FFT_P, HYW), F32)),
        compiler_params=_cparams(1, VMEM_LIMIT),
        name="kernel_spectrum",
    )(ak, fr, fi, tw, twh)


def _mid_kernel(a_ref, kh_ref, khh_ref, fr_ref, fi_ref, tw_ref, twh_ref, o_ref):
    dd = pl.program_id(0)
    dot = functools.partial(jnp.dot, preferred_element_type=F32)
    dot_t = lambda w, y: lax.dot_general(w, y, _TN_DIMS, preferred_element_type=F32)

    @pl.when(dd > 0)
    def _():
        rm = _cblock(*_twiddled(fr_ref, fi_ref, tw_ref[...])).astype(BF16)
        for n in range(B):
            x = dot(rm, a_ref[n].reshape(2 * FFT_P, HYW).astype(BF16))
            y = _cmul_rows(x, kh_ref[0], kh_ref[1])
            o_ref[n] = dot_t(rm, y).reshape(2, FFT_P, HYW)

    @pl.when(dd == 0)
    def _():
        for slot, tw, k_ref in ((0, tw_ref[...], kh_ref), (1, twh_ref[...], khh_ref)):
            w = jnp.concatenate(_twiddled(fr_ref, fi_ref, tw), axis=0).astype(BF16)
            for n in range(B):
                y = _cmul_rows(dot(w, a_ref[n, slot].astype(BF16)), k_ref[0], k_ref[1])
                o_ref[n, slot] = dot_t(w, y)


def _fft_mid(au, kh, khh, fr, fi, tw, twh):
    pair = pl.BlockSpec((B, None, 2, FFT_P, HYW), lambda d: (0, d, 0, 0, 0))
    return pl.pallas_call(
        _mid_kernel,
        grid=(FFT_H,),
        in_specs=[pair,
                  pl.BlockSpec((2, None, FFT_P, HYW), lambda d: (0, d, 0, 0)),
                  pl.BlockSpec((2, FFT_P, HYW), lambda d: (0, 0, 0))] + _STAGE2_SPECS,
        out_specs=pair,
        out_shape=jax.ShapeDtypeStruct((B, FFT_H, 2, FFT_P, HYW), F32),
        compiler_params=_cparams(1, VMEM_LIMIT),
        name="fft_mid",
    )(au, kh, khh, fr, fi, tw, twh)


def _dft_tables():
    na = L // FFT_P
    a = np.arange(na)
    dd = np.arange(FFT_H)
    ang = 2.0 * np.pi * np.outer(dd, a) / FFT_P
    re_rows = np.cos(ang)
    im_rows = -np.sin(ang)
    im_rows[0] = np.cos(np.pi * a)
    f_first = np.stack([re_rows, im_rows], axis=1).reshape(FFT_P, na)
    gre = 2.0 * np.cos(ang)
    gim = -2.0 * np.sin(ang)
    gre[0] = 1.0
    gim[0] = np.cos(np.pi * a)
    g_last = np.stack([gre, gim], axis=1).reshape(FFT_P, na).T / FFT_N
    b = np.arange(FFT_P)
    angf = 2.0 * np.pi * np.outer(b, b) / FFT_P
    ang2 = 2.0 * np.pi * np.outer(np.arange(FFT_H + 1), b) / FFT_N
    tw = np.stack([np.cos(ang2), -np.sin(ang2)], axis=1)
    f32 = lambda x: jnp.asarray(x.astype(np.float32))
    return (f32(f_first).astype(BF16), f32(g_last).astype(BF16), f32(np.cos(angf)), f32(-np.sin(angf)),
            f32(tw[:FFT_H]), f32(tw[FFT_H]))


def _merge_kernel(ya_ref, cv_ref, vx_ref, x0_ref, ga_ref, gb_ref, x_ref, er_ref, ec_ref, gt1_ref, hyd_ref,
                  wpa_ref, wpb_ref, wo_ref, g2_ref, sh2_ref, sc2_ref, wr_ref, br_ref,
                  x1_ref, t2_ref, lg_ref):
    vx = vx_ref[...]
    yb = x0_ref[...].astype(F32) * (cv_ref[...] + vx * hyd_ref[...])
    pa = jnp.dot(ya_ref[...].astype(BF16), wpa_ref[...], preferred_element_type=F32)
    pb = jnp.dot(yb.astype(BF16), wpb_ref[...], preferred_element_type=F32)
    mixed = (jax.nn.sigmoid(ga_ref[...].astype(F32)) * pa
             + jax.nn.sigmoid(gb_ref[...].astype(F32)) * pb)
    xm = jnp.dot(mixed.astype(BF16), wo_ref[...], preferred_element_type=F32)
    nrow = TM_MG // GRID_W
    row0 = (pl.program_id(0) % (L // TM_MG)) * nrow
    x1 = x_ref[...] + _pos_tile(er_ref, ec_ref, row0, nrow) + gt1_ref[...] * xm
    x1_ref[...] = x1
    t2 = _rms(x1) * g2_ref[...]
    t2 = t2 * (1.0 + sc2_ref[...]) + sh2_ref[...]
    t2_ref[...] = t2
    t_hi = t2.astype(BF16)
    t_lo = (t2 - t_hi.astype(F32)).astype(BF16)
    rr = (jnp.dot(t_hi, wr_ref[...], preferred_element_type=F32)
          + jnp.dot(t_lo, wr_ref[...], preferred_element_type=F32))
    lg_ref[...] = rr[:, :V7X_LANES] + rr[:, V7X_LANES:] + br_ref[...]


def _merge(ya, cv, vx, x0c, p, x2, er, ec, gt1, hyd, wpa, wpb, wo, g2, sh2, sc2, wr, br):
    tpb = L // TM_MG
    half = lambda: pl.BlockSpec((TM_MG, HYW), lambda i: (i, 0))
    full = lambda shape: pl.BlockSpec(shape, lambda i: tuple(0 for _ in shape))
    perb = lambda: pl.BlockSpec((None, 1, D), lambda i: (i // tpb, 0, 0))
    return pl.pallas_call(
        _merge_kernel,
        grid=(T // TM_MG,),
        in_specs=[half(), half(), half(), half(),
                  pl.BlockSpec((TM_MG, D), lambda i: (i, 3)),
                  pl.BlockSpec((TM_MG, D), lambda i: (i, 4)),
                  pl.BlockSpec((TM_MG, D), lambda i: (i, 0)),
                  full((L // GRID_W, D // 2)), full((GRID_W, D // 2)),
                  perb(), full((1, HYW)),
                  full((KD, D)), full((HYW, D)), full((D, D)),
                  full((1, D)), perb(), perb(),
                  full((D, 2 * V7X_LANES)), full((1, V7X_LANES))],
        out_specs=(pl.BlockSpec((TM_MG, D), lambda i: (i, 0)),
                   pl.BlockSpec((TM_MG, D), lambda i: (i, 0)),
                   pl.BlockSpec((TM_MG, V7X_LANES), lambda i: (i, 0))),
        out_shape=(jax.ShapeDtypeStruct((T, D), F32), jax.ShapeDtypeStruct((T, D), F32),
                   jax.ShapeDtypeStruct((T, V7X_LANES), F32)),
        compiler_params=_cparams(1, VMEM_LIMIT),
        name="merge",
    )(ya, cv, vx, x0c, p, p, x2, er, ec, gt1, hyd, wpa, wpb, wo, g2, sh2, sc2, wr, br)


def _route_kernel(lg_ref, info_ref, cnt_ref):
    @pl.when(pl.program_id(0) == 0)
    def _():
        cnt_ref[...] = jnp.zeros_like(cnt_ref)

    lg = lg_ref[...]
    lane = lax.broadcasted_iota(I32, lg.shape, 1)
    lanef = lane.astype(F32)
    neg = -1e30
    big = 1e9
    is_g = (lane >= NEXP) & (lane < NEXP + NGRP)
    gl = jnp.where(is_g, lg, neg)
    ge = jnp.where(is_g, jnp.exp(gl - jnp.max(gl, axis=-1, keepdims=True)), 0.0)
    pg = ge / jnp.sum(ge, axis=-1, keepdims=True)
    p_top_g = jnp.max(pg, axis=-1, keepdims=True)
    gidx = jnp.min(jnp.where(is_g & (pg == p_top_g), lanef, big), axis=-1, keepdims=True)
    g_sel = gidx.astype(I32) - NEXP
    emask = (lane < NEXP) & (jnp.right_shift(lane, NEPG.bit_length() - 1) == g_sel)
    el = jnp.where(emask, lg, neg)
    ee = jnp.where(emask, jnp.exp(el - jnp.max(el, axis=-1, keepdims=True)), 0.0)
    pe = ee / jnp.sum(ee, axis=-1, keepdims=True)
    p1 = jnp.max(jnp.where(emask, pe, -1.0), axis=-1, keepdims=True)
    i1 = jnp.min(jnp.where(emask & (pe == p1), lanef, big), axis=-1, keepdims=True)
    rest = emask & (lanef != i1)
    p2 = jnp.max(jnp.where(rest, pe, -1.0), axis=-1, keepdims=True)
    i2 = jnp.min(jnp.where(rest & (pe == p2), lanef, big), axis=-1, keepdims=True)
    wsum = p1 + p2
    w1 = p_top_g * p1 / wsum
    w2 = p_top_g * p2 / wsum
    sel1 = lanef == i1
    sel2 = lanef == i2
    oh = jnp.where(sel1 | sel2, 1.0, 0.0)
    r = lax.broadcasted_iota(I32, (TR, TR), 0)
    c = lax.broadcasted_iota(I32, (TR, TR), 1)
    stril = jnp.where(c < r, 1.0, 0.0).astype(BF16)
    before = jnp.dot(stril, oh.astype(BF16), preferred_element_type=F32) + cnt_ref[...]
    r1 = jnp.sum(jnp.where(sel1, before, 0.0), axis=-1, keepdims=True)
    r2 = jnp.sum(jnp.where(sel2, before, 0.0), axis=-1, keepdims=True)
    cnt_ref[...] += jnp.sum(oh, axis=0, keepdims=True)
    info = jnp.where(lane == 0, i1, jnp.where(lane == 1, r1, jnp.where(lane == 2, i2, jnp.where(
        lane == 3, r2, jnp.where(lane == 4, w1, jnp.where(lane == 5, w2, 0.0))))))
    info_ref[...] = info


def _route(lg):
    return pl.pallas_call(
        _route_kernel,
        grid=(T // TR,),
        in_specs=[pl.BlockSpec((TR, V7X_LANES), lambda i: (i, 0))],
        out_specs=(pl.BlockSpec((TR, V7X_LANES), lambda i: (i, 0)),
                   pl.BlockSpec((1, V7X_LANES), lambda i: (0, 0))),
        out_shape=(jax.ShapeDtypeStruct((T, V7X_LANES), F32), jax.ShapeDtypeStruct((1, V7X_LANES), F32)),
        compiler_params=_cparams(1, VMEM_LIMIT),
        name="route",
    )(lg)


def _positions_kernel(info_ref, st_ref, o_ref):
    info = info_ref[...]
    lane = lax.broadcasted_iota(I32, info.shape, 1)
    lanef = lane.astype(F32)
    st = st_ref[...]
    row = lambda e_lane, r_lane: (jnp.sum(jnp.where(lanef == _lane_pick(info, lane, e_lane), st, 0.0),
                                          axis=-1, keepdims=True) + _lane_pick(info, lane, r_lane))
    o_ref[...] = jnp.where(lane == 0, row(0, 1), jnp.where(lane == 1, row(2, 3), 0.0)).astype(I32)


def _positions(info, starts_row):
    return pl.pallas_call(
        _positions_kernel,
        grid=(T // TPOS,),
        in_specs=[pl.BlockSpec((TPOS, V7X_LANES), lambda i: (i, 0)),
                  pl.BlockSpec((1, V7X_LANES), lambda i: (0, 0))],
        out_specs=pl.BlockSpec((TPOS, V7X_LANES), lambda i: (i, 0)),
        out_shape=jax.ShapeDtypeStruct((T, V7X_LANES), I32),
        compiler_params=_cparams(1, VMEM_LIMIT),
        name="positions",
    )(info, starts_row)


def _scatter_kernel(pos_ref, zt_ref, t2_hbm, zeros_hbm, xs_hbm, tbuf, fsem, sem, zsem):
    i = pl.program_id(0)

    def zcopy(row):
        start = pl.multiple_of(jnp.maximum(row, 0), TE)
        return pltpu.make_async_copy(zeros_hbm, xs_hbm.at[pl.ds(start, TE)], zsem)

    @pl.when(i == 0)
    def _():
        def ztail(start, e, carry):
            @pl.when(zt_ref[0, e] >= 0)
            def _():
                cp = zcopy(zt_ref[0, e])
                cp.start() if start else cp.wait()
            return carry

        lax.fori_loop(0, NEXP, functools.partial(ztail, True), 0)
        lax.fori_loop(0, NEXP, functools.partial(ztail, False), 0)

        def zrest(start, tile, carry):
            cp = zcopy(tile * TE)
            cp.start() if start else cp.wait()
            return carry

        lax.fori_loop(zt_ref[0, NA_OFF], NT_EXP, functools.partial(zrest, True), 0)
        lax.fori_loop(zt_ref[0, NA_OFF], NT_EXP, functools.partial(zrest, False), 0)

    n = pl.num_programs(0)
    slot = i % SCATTER_SLOTS

    def fetch(tile, s):
        return pltpu.make_async_copy(t2_hbm.at[pl.ds(pl.multiple_of(tile * TS, TS), TS)], tbuf.at[s], fsem.at[s])

    def wait_rows(s):
        _wait_rows(TS, pltpu.make_async_copy(tbuf.at[s].at[pl.ds(0, 1)], xs_hbm.at[pl.ds(0, 1)], sem.at[s]))

    @pl.when(i == 0)
    def _():
        fetch(0, 0).start()

    @pl.when(i + 1 < n)
    def _():
        fetch(i + 1, (i + 1) % SCATTER_SLOTS).start()

    fetch(i, slot).wait()

    def row_copy(base, jj, kk):
        dst = pos_ref[0, 2 * (base + jj) + kk]
        return pltpu.make_async_copy(tbuf.at[slot].at[pl.ds(base, ROW_GROUP)].at[pl.ds(jj, 1)],
                                     xs_hbm.at[pl.ds(dst, 1)], sem.at[slot])

    _start_rows(TS, row_copy)

    @pl.when(i > 0)
    def _():
        wait_rows((i + SCATTER_SLOTS - 1) % SCATTER_SLOTS)

    @pl.when(i + 1 == n)
    def _():
        wait_rows(slot)


def _start_rows(n_rows, row_copy):
    def group(g, carry):
        base = pl.multiple_of(g * ROW_GROUP, ROW_GROUP)
        for jj in range(ROW_GROUP):
            for kk in range(2):
                row_copy(base, jj, kk).start(priority=kk)
        return carry

    lax.fori_loop(0, n_rows // ROW_GROUP, group, 0)


def _wait_rows(n_rows, one_row_copy):
    def drain(j, carry):
        one_row_copy.wait()
        one_row_copy.wait()
        return carry

    lax.fori_loop(0, n_rows, drain, 0, unroll=ROW_UNROLL)


def _scatter_rows(pos3, meta, t2, zeros_tile):
    return pl.pallas_call(
        _scatter_kernel,
        grid=(T // TS,),
        in_specs=[pl.BlockSpec((None, 1, 2 * TS), lambda i: (i, 0, 0), memory_space=pltpu.SMEM),
                  pl.BlockSpec(memory_space=pltpu.SMEM),
                  pl.BlockSpec(memory_space=pl.ANY),
                  pl.BlockSpec(memory_space=pl.ANY)],
        out_specs=pl.BlockSpec(memory_space=pl.ANY),
        out_shape=jax.ShapeDtypeStruct((NP_ROWS, D), F32),
        scratch_shapes=[pltpu.VMEM((SCATTER_SLOTS, TS, D), F32), pltpu.SemaphoreType.DMA((SCATTER_SLOTS,)),
                        pltpu.SemaphoreType.DMA((SCATTER_SLOTS,)), pltpu.SemaphoreType.DMA(())],
        compiler_params=_cparams(1, VMEM_LIMIT),
        name="scatter_rows",
    )(pos3, meta, t2, zeros_tile)


def _expert_kernel(te_ref, nx_ref, sl_ref, na_ref, xs_ref, wg_hbm, wu_hbm, wd_hbm, ys_ref,
                   wgs_ref, wus_ref, wds_ref, wgb_ref, wub_ref, wdb_ref, sem):
    i = pl.program_id(0)
    active = i < na_ref[0]
    first = active & ((i == 0) | (te_ref[i] != te_ref[jnp.maximum(i - 1, 0)]))

    def fetch(e, slot):
        return [pltpu.make_async_copy(wg_hbm.at[e], wgs_ref.at[slot], sem.at[slot]),
                pltpu.make_async_copy(wu_hbm.at[e], wus_ref.at[slot], sem.at[slot]),
                pltpu.make_async_copy(wd_hbm.at[e], wds_ref.at[slot], sem.at[slot])]

    for slot in range(2):
        @pl.when(first & (sl_ref[i] == slot))
        def _():
            @pl.when(i == 0)
            def _():
                for cp in fetch(te_ref[i], slot):
                    cp.start()

            for cp in fetch(te_ref[i], slot):
                cp.wait()
            wgb_ref[...] = wgs_ref[slot].astype(BF16)
            wub_ref[...] = wus_ref[slot].astype(BF16)
            wdb_ref[...] = wds_ref[slot].astype(BF16)

            @pl.when(nx_ref[i] >= 0)
            def _():
                for cp in fetch(nx_ref[i], 1 - slot):
                    cp.start()

    @pl.when(active)
    def _():
        x = xs_ref[...].astype(BF16)
        g = jnp.dot(x, wgb_ref[...], preferred_element_type=F32)
        u = jnp.dot(x, wub_ref[...], preferred_element_type=F32)
        hid = (g * jax.nn.sigmoid(g) * u).astype(BF16)
        ys_ref[...] = jnp.dot(hid, wdb_ref[...], preferred_element_type=F32)

    @pl.when(jnp.logical_not(active))
    def _():
        ys_ref[...] = jnp.zeros_like(ys_ref)


def _experts(tile_expert, tile_next, tile_slot, n_active, xs, wg, wu, wd):
    rows = lambda i, *_: (i, 0)
    rows_in = lambda i, te, nx, sl, na: (jnp.minimum(i, na[0] - 1), 0)
    any_space = pl.BlockSpec(memory_space=pl.ANY)
    grid_spec = pltpu.PrefetchScalarGridSpec(
        num_scalar_prefetch=4,
        grid=(NT_EXP,),
        in_specs=[pl.BlockSpec((TE, D), rows_in), any_space, any_space, any_space],
        out_specs=pl.BlockSpec((TE, D), rows),
        scratch_shapes=[pltpu.VMEM((2, D, DEXP), F32), pltpu.VMEM((2, D, DEXP), F32), pltpu.VMEM((2, DEXP, D), F32),
                        pltpu.VMEM((D, DEXP), BF16), pltpu.VMEM((D, DEXP), BF16), pltpu.VMEM((DEXP, D), BF16),
                        pltpu.SemaphoreType.DMA((2,))],
    )
    return pl.pallas_call(
        _expert_kernel,
        grid_spec=grid_spec,
        out_shape=jax.ShapeDtypeStruct((NP_ROWS, D), F32),
        compiler_params=_cparams(1, VMEM_LIMIT),
        name="experts",
    )(tile_expert, tile_next, tile_slot, n_active, xs, wg, wu, wd)


def _combine_kernel(pos_ref, posn_ref, info_ref, x1_ref, gt2_ref, fg_ref, ys_hbm, o_ref, buf, sem):
    g = pl.program_id(0)
    ng = pl.num_programs(0)

    def gather(p_ref, half, slot):
        def row_copy(base, jj, kk):
            src = p_ref[0, 2 * (half * TC + base + jj) + kk]
            return pltpu.make_async_copy(ys_hbm.at[pl.ds(src, 1)],
                                         buf.at[slot, kk].at[pl.ds(base, ROW_GROUP)].at[pl.ds(jj, 1)],
                                         sem.at[slot])
        _start_rows(TC, row_copy)

    def finish(half, slot):
        _wait_rows(TC, pltpu.make_async_copy(ys_hbm.at[pl.ds(0, 1)], buf.at[slot, 0].at[pl.ds(0, 1)],
                                             sem.at[slot]))
        rows = slice(half * TC, (half + 1) * TC)
        info = info_ref[rows, :]
        lane = lax.broadcasted_iota(I32, info.shape, 1)
        moe = _lane_pick(info, lane, 4) * buf[slot, 0] + _lane_pick(info, lane, 5) * buf[slot, 1]
        x2 = x1_ref[rows, :] + gt2_ref[...] * moe
        o_ref[rows, :] = _rms(x2) * fg_ref[...]

    @pl.when(g == 0)
    def _():
        gather(pos_ref, 0, 0)

    gather(pos_ref, 1, 1)
    finish(0, 0)

    @pl.when(g + 1 < ng)
    def _():
        gather(posn_ref, 0, 0)

    finish(1, 1)


def _combine(pos3, info, x1, gt2, fg, ys):
    step = 2 * TC
    tpb = L // step
    nsteps = T // step
    return pl.pallas_call(
        _combine_kernel,
        grid=(nsteps,),
        in_specs=[pl.BlockSpec((None, 1, 2 * step), lambda i: (i, 0, 0), memory_space=pltpu.SMEM),
                  pl.BlockSpec((None, 1, 2 * step), lambda i: (jnp.minimum(i + 1, nsteps - 1), 0, 0),
                               memory_space=pltpu.SMEM),
                  pl.BlockSpec((step, V7X_LANES), lambda i: (i, 0)),
                  pl.BlockSpec((step, D), lambda i: (i, 0)),
                  pl.BlockSpec((None, 1, D), lambda i: (i // tpb, 0, 0)),
                  pl.BlockSpec((1, D), lambda i: (0, 0)),
                  pl.BlockSpec(memory_space=pl.ANY)],
        out_specs=pl.BlockSpec((step, D), lambda i: (i, 0)),
        out_shape=jax.ShapeDtypeStruct((T, D), F32),
        scratch_shapes=[pltpu.VMEM((2, 2, TC, D), F32), pltpu.SemaphoreType.DMA((2,))],
        compiler_params=_cparams(1, VMEM_LIMIT),
        name="combine",
    )(pos3, pos3, info, x1, gt2, fg, ys)


def _pos_tables():
    rows = L // GRID_W
    quarter = D // 4
    omega = 1.0 / (10000.0 ** (jnp.arange(quarter, dtype=F32) / quarter))

    def axis_emb(pos):
        a = pos[:, None] * omega[None, :]
        return jnp.concatenate([jnp.sin(a), jnp.cos(a)], axis=-1)

    er = axis_emb(jnp.arange(rows, dtype=F32))
    ec = axis_emb(jnp.arange(GRID_W, dtype=F32))
    return er, ec


def _filter_features():
    z = np.zeros((L, V7X_LANES), np.float64)
    bands = (HY_EMB - 1) // 2
    ang = (2.0 * np.pi * np.arange(L) / L)[:, None] * np.linspace(1e-4, bands - 1, bands)[None, :]
    z[:, 0] = np.linspace(0.0, 1.0, L)
    z[:, 1:1 + bands] = np.cos(ang)
    z[:, 1 + bands:HY_EMB] = -np.sin(ang)
    z[1:, HY_EMB] = 1.0
    z = z.reshape(L // FFT_P, FFT_P, V7X_LANES).transpose(1, 0, 2).reshape(L, V7X_LANES)
    return jnp.asarray(z.astype(np.float32))


def _pad2(a, rows, cols):
    return jnp.pad(a, ((0, rows - a.shape[0]), (0, cols - a.shape[1])))


def kernel(x, c, ctx, c_ctx, ada_w, ada_b, norm1_g, norm2_g, w_in, hgrn_lb, hgrn_norm_g, hy_conv_w, hy_conv_b, hy_filt_w1, hy_filt_b1, hy_filt_freq1, hy_filt_w2, hy_filt_b2, hy_filt_freq2, hy_filt_w3, hy_d, w_proj_a, w_proj_b, w_out, moe_router_g_w, moe_router_g_b, moe_router_e_w, moe_router_e_b, moe_w_gate, moe_w_up, moe_w_down, final_norm_g):
    cvec = jnp.zeros((8, D), F32).at[0:B].set(c).at[B].set(c_ctx)
    mod = _adaln(cvec, ada_w[0], ada_b[0][None, :])
    m6 = mod.reshape(8, 6, D)
    sh1, sc1, gt1, sh2, sc2, gt2 = [m6[0:B, k][:, None, :] for k in range(6)]
    csh1, csc1 = m6[B:B + 1, 0], m6[B:B + 1, 1]

    lbs = jnp.cumsum(jax.nn.softmax(hgrn_lb.astype(F32), axis=0), axis=0)[0]
    g1 = norm1_g[0][None, :]
    er, ec = _pos_tables()
    x2 = x.reshape(T, D)

    w_ctx = w_in[0][:, KD:4 * KD].astype(BF16)
    s_f, s_b = _context_states(ctx, g1, csh1, csc1, w_ctx, lbs)

    pz, p = _in_projection(x2, er, ec, g1, sh1, sc1, w_in[0].astype(BF16))

    o_f = _hgrn_scan(pz, p, lbs[0:1], s_f, False)
    y_a = _hgrn_scan(pz, p, lbs[1:2], s_b, True, o_f=o_f, norm_g=hgrn_norm_g[0][None, :])

    vx, x0c = _hyena_pre(p, hy_conv_w[0], hy_conv_b[0][None, :])
    deltas = jnp.abs(jnp.linspace(math.log(HY_DECAY_TARGET) / HY_SLOW_PCT,
                                  math.log(HY_DECAY_TARGET) / HY_FAST_PCT, HYW, dtype=F32))[None, :]
    ln = V7X_LANES
    fh = hy_filt_w2.shape[-1]
    blockdiag = lambda m: jnp.concatenate([_pad2(m, m.shape[0], 2 * m.shape[1]),
                                           jnp.pad(m, ((0, 0), (m.shape[1], 0)))], axis=0)
    twice = lambda v: jnp.concatenate([v, v])[None, :]
    w3 = hy_filt_w3[0]
    f_first, g_last, fr, fi, tw, twh = _dft_tables()
    na = L // FFT_P
    ak = _filter_taps_dft(
        _filter_features(),
        _hi_lo(blockdiag(_pad2(hy_filt_w1[0], ln, fh))), twice(hy_filt_b1[0]), twice(hy_filt_freq1[0]),
        _hi_lo(blockdiag(hy_filt_w2[0])), twice(hy_filt_b2[0]), twice(hy_filt_freq2[0]),
        _hi_lo(_pad2(w3, ln, 2 * HYW)), _hi_lo(jnp.pad(w3, ((fh, 0), (0, 0)))), deltas, f_first)
    kh, khh = _kernel_spectrum(ak.reshape(2, FFT_H, 2, FFT_P, HYW), fr, fi, tw, twh)
    au = _strided_dft(vx.reshape(B, na, FFT_P, HYW), f_first, "dft_first")
    bp = _fft_mid(au.reshape(B, FFT_H, 2, FFT_P, HYW), kh, khh, fr, fi, tw, twh)
    conv = _strided_dft(bp.reshape(B, FFT_P, FFT_P, HYW), g_last, "dft_last").reshape(T, HYW)

    wr = jnp.concatenate([jnp.transpose(moe_router_e_w[0], (1, 0, 2)).reshape(D, NEXP),
                          moe_router_g_w[0], jnp.zeros((D, V7X_LANES - NEXP - NGRP), F32)], axis=1)
    wr_hi = wr.astype(BF16)
    wr = jnp.concatenate([wr_hi, (wr - wr_hi.astype(F32)).astype(BF16)], axis=1)
    br = jnp.concatenate([moe_router_e_b[0].reshape(NEXP), moe_router_g_b[0],
                          jnp.zeros((V7X_LANES - NEXP - NGRP,), F32)])[None, :]
    x1, t2, lg = _merge(y_a.reshape(T, KD), conv, vx, x0c, p, x2, er, ec, gt1, hy_d[0][None, :],
                        w_proj_a[0].astype(BF16), w_proj_b[0].astype(BF16), w_out[0].astype(BF16),
                        norm2_g[0][None, :], sh2, sc2, wr, br)

    info, counts = _route(lg)
    cnt = counts[0, :NEXP].astype(I32)
    pc = ((cnt + TE - 1) // TE) * TE
    ends = jnp.cumsum(pc)
    starts = ends - pc
    n_active = (ends[-1] // TE).astype(I32)[None]
    tile_rows = jnp.arange(NT_EXP, dtype=I32) * TE
    tile_expert = jnp.minimum(jnp.sum((ends[None, :] <= tile_rows[:, None]).astype(I32), axis=1), NEXP - 1)
    meta = jnp.concatenate([jnp.where(pc > 0, ends - TE, -1), starts, n_active]).astype(I32)[None, :]
    starts_row = jnp.pad(starts.astype(F32), (0, V7X_LANES - NEXP))[None, :]
    pos3 = _positions(info, starts_row)[:, :2].reshape(T // TS, 1, 2 * TS)

    xs = _scatter_rows(pos3, meta, t2, jnp.zeros((TE, D), F32))
    eid = jnp.arange(NEXP, dtype=I32)
    nonempty = pc > 0
    later = jnp.where(nonempty[None, :] & (eid[None, :] > eid[:, None]), eid[None, :], NEXP)
    next_e = jnp.min(later, axis=1)
    next_e = jnp.where(next_e < NEXP, next_e, -1).astype(I32)
    slot_e = ((jnp.cumsum(nonempty.astype(I32)) - 1) % 2).astype(I32)
    ys = _experts(tile_expert, next_e[tile_expert], slot_e[tile_expert], n_active, xs,
                  moe_w_gate[0].reshape(NEXP, D, DEXP), moe_w_up[0].reshape(NEXP, D, DEXP),
                  moe_w_down[0].reshape(NEXP, DEXP, D))
    out = _combine(pos3, info, x1, gt2, final_norm_g[None, :], ys)
    return out.reshape(B, L, D)
```

```python
import functools
import math

import numpy as np
import jax
import jax.numpy as jnp
from jax import lax
from jax.experimental import pallas as pl
from jax.experimental.pallas import tpu as pltpu

F32 = jnp.float32
BF16 = jnp.bfloat16
I32 = jnp.int32

D = 1024
B = 2
L = 8192
T = B * L
CTX = 256
GRID_W = 64
EPS = 1e-6
H = 4
DK = 128
DV = 128
KD = H * DK
IN_W = 6144
HYW = 512
HY_EMB = 33
NGRP = 4
NEPG = 8
NEXP = NGRP * NEPG
DEXP = 512
HY_DECAY_TARGET = 1e-2
HY_FAST_PCT = 0.3
HY_SLOW_PCT = 1.5

V7X_LANES = 128
V7X_SUBLANES = 8
V7X_VMEM_BYTES = 64 * 1024 * 1024
VMEM_LIMIT = (3 * V7X_VMEM_BYTES) // 4

FFT_N = 2 * L
FFT_P = 128
FFT_BB = 8

TM_IN = 1024
TN_IN = 1024
TH = 128
HG_CHUNKS = 2
CB = 32
TM_HY = 1024
HALO = 2 * V7X_SUBLANES
TM_MG = 512
TR = 1024
TE = 512
TPOS = 2048
NP_ROWS = 2 * T + NEXP * TE
NT_EXP = NP_ROWS // TE
TS = 512
TC = TS // 2
SCATTER_SLOTS = 3
ROW_UNROLL = 8
ROW_GROUP = 32
ST_OFF = NEXP
NA_OFF = 2 * NEXP


def _cparams(n_axes, vmem=None):
    return pltpu.CompilerParams(dimension_semantics=("arbitrary",) * n_axes,
                                vmem_limit_bytes=vmem)


def _split3(x):
    hi = x.astype(BF16)
    r = x - hi.astype(F32)
    mid = r.astype(BF16)
    lo = (r - mid.astype(F32)).astype(BF16)
    return hi, mid, lo


def _dot01(m, x):
    hi, mid, lo = _split3(x)
    return (jnp.dot(m, hi, preferred_element_type=F32) + jnp.dot(m, mid, preferred_element_type=F32)
            + jnp.dot(m, lo, preferred_element_type=F32))


def _dot_split(a, w_ref):
    a_hi = a.astype(BF16)
    a_lo = (a - a_hi.astype(F32)).astype(BF16)
    return (jnp.dot(a_hi, w_ref[0], preferred_element_type=F32) + jnp.dot(a_hi, w_ref[1], preferred_element_type=F32)
            + jnp.dot(a_lo, w_ref[0], preferred_element_type=F32))


def _hi_lo(w):
    hi = w.astype(BF16)
    return jnp.stack([hi, (w - hi.astype(F32)).astype(BF16)])


def _rms(x):
    return x * lax.rsqrt(jnp.mean(x * x, axis=-1, keepdims=True) + EPS)


def _lane_pick(x, lane, idx):
    return jnp.sum(jnp.where(lane == idx, x, 0.0), axis=-1, keepdims=True)


def _ada_kernel(c_ref, w_ref, b_ref, o_ref):
    c = c_ref[...]
    s = c * jax.nn.sigmoid(c)
    w = w_ref[...]
    w_hi = w.astype(BF16)
    w_lo = (w - w_hi.astype(F32)).astype(BF16)
    s_hi = s.astype(BF16)
    s_lo = (s - s_hi.astype(F32)).astype(BF16)
    o_ref[...] = (jnp.dot(s_hi, w_hi, preferred_element_type=F32) + jnp.dot(s_hi, w_lo, preferred_element_type=F32)
                  + jnp.dot(s_lo, w_hi, preferred_element_type=F32) + b_ref[...])


def _adaln(cvec, w, b):
    tn = 1536
    return pl.pallas_call(
        _ada_kernel,
        grid=(6 * D // tn,),
        in_specs=[pl.BlockSpec((8, D), lambda j: (0, 0)),
                  pl.BlockSpec((D, tn), lambda j: (0, j)),
                  pl.BlockSpec((1, tn), lambda j: (0, j))],
        out_specs=pl.BlockSpec((8, tn), lambda j: (0, j)),
        out_shape=jax.ShapeDtypeStruct((8, 6 * D), F32),
        compiler_params=_cparams(1, VMEM_LIMIT),
        name="adaln",
    )(cvec, w, b)


def _keys(z, lb):
    sig = jax.nn.sigmoid(z)
    logf = jnp.log(lb + (1.0 - lb) * sig)
    k = (1.0 - lb) * jax.nn.sigmoid(-z)
    return k, logf


def _ctx_kernel(ctx_ref, g_ref, sh_ref, sc_ref, w_ref, lb_ref, sf_ref, sb_ref):
    h = _rms(ctx_ref[...]) * g_ref[...]
    h = h * (1.0 + sc_ref[...]) + sh_ref[...]
    p = jnp.dot(h.astype(BF16), w_ref[...], preferred_element_type=F32)
    zf, zb, v = p[:, :KD], p[:, KD:2 * KD], p[:, 2 * KD:]
    kf, lf = _keys(zf, lb_ref[0:1, :])
    kb, lbk = _keys(zb, lb_ref[1:2, :])
    r = lax.broadcasted_iota(I32, (CTX, CTX), 0)
    c = lax.broadcasted_iota(I32, (CTX, CTX), 1)
    tril = jnp.where(c <= r, 1.0, 0.0).astype(BF16)
    cf = _dot01(tril, lf)
    cb = _dot01(tril, lbk)
    kfd = (kf * jnp.exp(cf[CTX - 1:CTX, :] - cf)).astype(BF16)
    kbd = (kb * jnp.exp(cb - lbk)).astype(BF16)
    vb = v.astype(BF16)
    tn = (((0,), (0,)), ((), ()))
    for hh in range(H):
        hs = slice(hh * DK, (hh + 1) * DK)
        sf_ref[hh] = lax.dot_general(vb[:, hs], kfd[:, hs], tn, preferred_element_type=F32)
        sb_ref[hh] = lax.dot_general(vb[:, hs], kbd[:, hs], tn, preferred_element_type=F32)


def _context_states(ctx, g1, csh1, csc1, w_ctx, lbs):
    st = jax.ShapeDtypeStruct((B, H, DV, DK), F32)
    return pl.pallas_call(
        _ctx_kernel,
        grid=(B,),
        in_specs=[pl.BlockSpec((None, CTX, D), lambda b: (b, 0, 0)),
                  pl.BlockSpec((1, D), lambda b: (0, 0)),
                  pl.BlockSpec((1, D), lambda b: (0, 0)),
                  pl.BlockSpec((1, D), lambda b: (0, 0)),
                  pl.BlockSpec((D, 3 * KD), lambda b: (0, 0)),
                  pl.BlockSpec((2, KD), lambda b: (0, 0))],
        out_specs=(pl.BlockSpec((None, H, DV, DK), lambda b: (b, 0, 0, 0)),
                   pl.BlockSpec((None, H, DV, DK), lambda b: (b, 0, 0, 0))),
        out_shape=(st, st),
        compiler_params=_cparams(1, VMEM_LIMIT),
        name="ctx_states",
    )(ctx, g1, csh1, csc1, w_ctx, lbs)


def _pos_tile(er_ref, ec_ref, row0, nrow):
    lo = jnp.concatenate([jnp.broadcast_to(er_ref[pl.ds(row0 + i, 1), :], (GRID_W, D // 2))
                          for i in range(nrow)], axis=0)
    hi = jnp.concatenate([ec_ref[...]] * nrow, axis=0)
    return jnp.concatenate([lo, hi], axis=1)


def _inproj_kernel(x_ref, er_ref, ec_ref, g_ref, sh_ref, sc_ref, w_ref, oz_ref, o_ref, hx_ref):
    i = pl.program_id(0)
    j = pl.program_id(1)
    ni = pl.num_programs(0)
    nj = pl.num_programs(1)
    slot = i % 2

    def prepare(tile, dst):
        nrow = TM_IN // GRID_W
        row0 = (tile % (L // TM_IN)) * nrow
        h = _rms(x_ref[...] + _pos_tile(er_ref, ec_ref, row0, nrow)) * g_ref[...]
        hx_ref[dst] = (h * (1.0 + sc_ref[...]) + sh_ref[...]).astype(BF16)

    def project():
        return jnp.dot(hx_ref[slot], w_ref[...], preferred_element_type=F32)

    @pl.when((i == 0) & (j == 0))
    def _():
        prepare(0, 0)

    @pl.when(j == 0)
    def _():
        r = project()
        o_ref[:, :KD] = r[:, :KD].astype(BF16)
        oz_ref[:, :KD] = r[:, KD:]

    @pl.when(j == 1)
    def _():
        r = project()
        oz_ref[:, KD:] = r[:, :KD]
        o_ref[:, KD:] = r[:, KD:].astype(BF16)

    @pl.when((j > 1) & (j < nj - 1))
    def _():
        o_ref[...] = project().astype(BF16)

    @pl.when((j == nj - 1) & (i + 1 < ni))
    def _():
        o_ref[...] = project().astype(BF16)
        prepare(i + 1, 1 - slot)

    @pl.when((j == nj - 1) & (i + 1 == ni))
    def _():
        o_ref[...] = project().astype(BF16)


def _in_projection(x2, er, ec, g1, sh1, sc1, w_bf):
    tiles_per_batch = L // TM_IN
    n_i = T // TM_IN
    n_j = IN_W // TN_IN
    ahead = lambda i, j: jnp.minimum(i + jnp.where(j == n_j - 1, 1, 0), n_i - 1)
    return pl.pallas_call(
        _inproj_kernel,
        grid=(n_i, n_j),
        in_specs=[pl.BlockSpec((TM_IN, D), lambda i, j: (ahead(i, j), 0)),
                  pl.BlockSpec((L // GRID_W, D // 2), lambda i, j: (0, 0)),
                  pl.BlockSpec((GRID_W, D // 2), lambda i, j: (0, 0)),
                  pl.BlockSpec((1, D), lambda i, j: (0, 0)),
                  pl.BlockSpec((None, 1, D), lambda i, j: (ahead(i, j) // tiles_per_batch, 0, 0)),
                  pl.BlockSpec((None, 1, D), lambda i, j: (ahead(i, j) // tiles_per_batch, 0, 0)),
                  pl.BlockSpec((D, TN_IN), lambda i, j: (0, j))],
        out_specs=(pl.BlockSpec((TM_IN, TN_IN), lambda i, j: (i, 0)),
                   pl.BlockSpec((TM_IN, TN_IN), lambda i, j: (i, jnp.maximum(j - 1, 0)))),
        out_shape=(jax.ShapeDtypeStruct((T, 2 * KD), F32),
                   jax.ShapeDtypeStruct((T, IN_W - 2 * KD), BF16)),
        scratch_shapes=[pltpu.VMEM((2, TM_IN, D), BF16)],
        compiler_params=_cparams(2, VMEM_LIMIT),
        name="in_proj",
    )(x2, er, ec, g1, sh1, sc1, w_bf)


def _hgrn_kernel(reverse, readout, *refs):
    if readout:
        q_ref, z_ref, v_ref, lb_ref, s0_ref, of_ref, g_ref, ng_ref, o_ref, st_ref = refs
    else:
        q_ref, z_ref, v_ref, lb_ref, s0_ref, o_ref, st_ref = refs

    @pl.when(pl.program_id(0) == 0)
    def _():
        st_ref[...] = s0_ref[...]

    r = lax.broadcasted_iota(I32, (TH, TH), 0)
    c = lax.broadcasted_iota(I32, (TH, TH), 1)
    cb_shift = CB.bit_length() - 1
    same = jnp.right_shift(r, cb_shift) == jnp.right_shift(c, cb_shift)
    tri_mask = same & ((c >= r) if reverse else (c <= r))
    tri = jnp.where(tri_mask, 1.0, 0.0).astype(BF16)
    rblk = jnp.right_shift(r, cb_shift)
    cblk = jnp.right_shift(c, cb_shift)
    dist = (rblk - cblk) if not reverse else (cblk - rblk)
    for cc in (range(HG_CHUNKS - 1, -1, -1) if reverse else range(HG_CHUNKS)):
        for b in range(B):
            _hgrn_chunk(reverse, readout, b, slice(cc * TH, (cc + 1) * TH), tri, tri_mask, dist, refs)


def _hgrn_chunk(reverse, readout, b, rs, tri, tri_mask, dist, refs):
    if readout:
        q_ref, z_ref, v_ref, lb_ref, s0_ref, of_ref, g_ref, ng_ref, o_ref, st_ref = refs
    else:
        q_ref, z_ref, v_ref, lb_ref, s0_ref, o_ref, st_ref = refs
    q = q_ref[b, rs, :].astype(F32)
    v = v_ref[b, rs, :]
    k, logf = _keys(z_ref[b, rs, :], lb_ref[...])
    bl = _dot01(tri, logf)
    nt = (((1,), (1,)), ((), ()))
    tn = (((0,), (0,)), ((), ()))
    nblk = TH // CB
    e_row = 0 if reverse else CB - 1
    m_row = CB - 1 - CB // 2 if reverse else CB // 2
    tau = [bl[jb * CB + e_row:jb * CB + e_row + 1] for jb in range(nblk)]
    mid = [bl[jb * CB + m_row:jb * CB + m_row + 1] for jb in range(nblk)]
    rows = lambda vecs: jnp.concatenate([jnp.broadcast_to(x, (CB, KD)) for x in vecs], axis=0)
    mid_b = rows(mid)
    qd0 = (q * jnp.exp(bl - mid_b)).astype(BF16)
    kd0 = (k * jnp.exp(mid_b - bl)).astype(BF16)
    qs = q * jnp.exp(bl)
    ke = k * jnp.exp(rows(tau) - bl)
    order = list(range(nblk - 1, -1, -1)) if reverse else list(range(nblk))
    pre = [jnp.zeros((1, KD), F32)]
    for i in range(nblk):
        pre.append(pre[-1] + tau[order[i]])
    total = pre[nblk]
    entry = [None] * nblk
    leave = [None] * nblk
    gap = [[None] * nblk for _ in range(nblk)]
    for i, jb in enumerate(order):
        entry[jb] = jnp.exp(pre[i])
        leave[jb] = jnp.exp(total - pre[i + 1])
        for d in range(2, nblk):
            gap[d][jb] = jnp.exp(pre[i + d] - pre[i + 1]) if i + d < nblk else jnp.zeros((1, KD), F32)
    qc = (qs * rows(entry)).astype(BF16)
    kc = (ke * rows(leave)).astype(BF16)
    kx = jnp.concatenate([ke.astype(BF16)] + [(ke * rows(gap[d])).astype(BF16) for d in range(2, nblk)], axis=0)
    qsb = qs.astype(BF16)
    dec = jnp.exp(total)

    def blockdiag(x):
        first_head = lax.broadcasted_iota(I32, x.shape, 1) < DK
        zero = jnp.zeros_like(x)
        return jnp.concatenate([jnp.where(first_head, x, zero), jnp.where(first_head, zero, x)], axis=0)

    for hp in range(H // 2):
        ps = slice(2 * hp * DK, 2 * (hp + 1) * DK)
        sc0 = lax.dot_general(qd0[:, ps], blockdiag(kd0[:, ps]), nt, preferred_element_type=F32)
        scx = lax.dot_general(qsb[:, ps], blockdiag(kx[:, ps]), nt, preferred_element_type=F32)
        halves = []
        for hh in range(2):
            sc = jnp.where(tri_mask, sc0[:, hh * TH:(hh + 1) * TH], 0.0)
            base = hh * (nblk - 1) * TH
            for d in range(1, nblk):
                sc = jnp.where(dist == d, scx[:, base + (d - 1) * TH:base + d * TH], sc)
            halves.append(sc.astype(BF16))
        sc_pair = jnp.concatenate(halves, axis=1)
        st_a = st_ref[b, 2 * hp]
        st_b = st_ref[b, 2 * hp + 1]
        zst = jnp.zeros((DV, DK), BF16)
        st_pair = jnp.concatenate([jnp.concatenate([st_a.astype(BF16), zst], axis=1),
                                   jnp.concatenate([zst, st_b.astype(BF16)], axis=1)], axis=0)
        o_pair = (lax.dot_general(qc[:, ps], st_pair, nt, preferred_element_type=F32)
                  + jnp.dot(sc_pair, blockdiag(v[:, ps]), preferred_element_type=F32))
        upd = lax.dot_general(v[:, ps], kc[:, ps], tn, preferred_element_type=F32)
        st_ref[b, 2 * hp] = st_a * dec[:, ps][:, :DK] + upd[:DV, :DK]
        st_ref[b, 2 * hp + 1] = st_b * dec[:, ps][:, DK:] + upd[DV:, DK:]
        for hh in range(2):
            hs = slice((2 * hp + hh) * DK, (2 * hp + hh + 1) * DK)
            o_h = o_pair[:, hh * DV:(hh + 1) * DV]
            if readout:
                o_h = o_h + of_ref[b, rs, hs]
                o_h = _rms(o_h) * ng_ref[...]
                gh = g_ref[b, rs, hs].astype(F32)
                o_h = o_h * (gh * jax.nn.sigmoid(gh))
            o_ref[b, rs, hs] = o_h.astype(o_ref.dtype)


def _hgrn_scan(pz, p, lb_row, s0, reverse, o_f=None, norm_g=None):
    nch = L // (HG_CHUNKS * TH)
    chunk = (lambda c: nch - 1 - c) if reverse else (lambda c: c)
    col_spec = lambda j: pl.BlockSpec((B, HG_CHUNKS * TH, KD), lambda c: (0, chunk(c), j))
    in_specs = [col_spec(0), col_spec(1 if reverse else 0), col_spec(1),
                pl.BlockSpec((1, KD), lambda c: (0, 0)),
                pl.BlockSpec((B, H, DV, DK), lambda c: (0, 0, 0, 0))]
    p3 = p.reshape(B, L, p.shape[-1])
    args = [p3, pz.reshape(B, L, 2 * KD), p3, lb_row, s0]
    readout = o_f is not None
    if readout:
        in_specs += [col_spec(0), col_spec(2), pl.BlockSpec((1, DV), lambda c: (0, 0))]
        args += [o_f, p3, norm_g]
    return pl.pallas_call(
        functools.partial(_hgrn_kernel, reverse, readout),
        grid=(nch,),
        in_specs=in_specs,
        out_specs=col_spec(0),
        out_shape=jax.ShapeDtypeStruct((B, L, KD), BF16 if readout else F32),
        scratch_shapes=[pltpu.VMEM((B, H, DV, DK), F32)],
        compiler_params=_cparams(1, VMEM_LIMIT),
        name="hgrn_bwd_readout" if readout else "hgrn_fwd",
    )(*args)


def _hy_pre_kernel(v_ref, x1_ref, x0_ref, vp_ref, x1p_ref, x0p_ref, vn_ref, x1n_ref, x0n_ref,
                   w_ref, b_ref, vx_ref, x0o_ref):
    i = pl.program_id(1)
    first = i == 0
    last = i == pl.num_programs(1) - 1
    row = lax.broadcasted_iota(I32, (TM_HY, 1), 0)

    def conv(c_ref, p_ref, n_ref, col):
        x = c_ref[...].astype(F32)
        prev_row = jnp.where(first, 0.0, p_ref[...].astype(F32)[HALO - 1:HALO, :])
        next_row = jnp.where(last, 0.0, n_ref[...].astype(F32)[0:1, :])
        xm = jnp.where(row == 0, prev_row, pltpu.roll(x, 1, axis=0))
        xp = jnp.where(row == TM_HY - 1, next_row, pltpu.roll(x, TM_HY - 1, axis=0))
        cs = slice(col * HYW, (col + 1) * HYW)
        return xm * w_ref[0:1, cs] + x * w_ref[1:2, cs] + xp * w_ref[2:3, cs] + b_ref[:, cs]

    v = conv(v_ref, vp_ref, vn_ref, 0)
    x1 = conv(x1_ref, x1p_ref, x1n_ref, 1)
    x0 = conv(x0_ref, x0p_ref, x0n_ref, 2)
    vx_ref[...] = v * x1
    x0o_ref[...] = x0.astype(x0o_ref.dtype)


def _hyena_pre(p, conv_w, conv_b):
    nt = L // TM_HY
    hb = TM_HY // HALO
    nhb = T // HALO
    cur = lambda col: pl.BlockSpec((TM_HY, HYW), lambda b, i: (b * nt + i, col))
    prv = lambda col: pl.BlockSpec((HALO, HYW), lambda b, i: (jnp.maximum((b * nt + i) * hb - 1, 0), col))
    nxt = lambda col: pl.BlockSpec((HALO, HYW), lambda b, i: (jnp.minimum((b * nt + i + 1) * hb, nhb - 1), col))
    c0 = 3
    out = jax.ShapeDtypeStruct((T, HYW), F32)
    return pl.pallas_call(
        _hy_pre_kernel,
        grid=(B, nt),
        in_specs=[cur(c0), cur(c0 + 1), cur(c0 + 2), prv(c0), prv(c0 + 1), prv(c0 + 2),
                  nxt(c0), nxt(c0 + 1), nxt(c0 + 2),
                  pl.BlockSpec((3, 3 * HYW), lambda b, i: (0, 0)),
                  pl.BlockSpec((1, 3 * HYW), lambda b, i: (0, 0))],
        out_specs=(pl.BlockSpec((TM_HY, HYW), lambda b, i: (b * nt + i, 0)),
                   pl.BlockSpec((TM_HY, HYW), lambda b, i: (b * nt + i, 0))),
        out_shape=(out, jax.ShapeDtypeStruct((T, HYW), BF16)),
        compiler_params=_cparams(2, VMEM_LIMIT),
        name="hyena_pre",
    )(p, p, p, p, p, p, p, p, p, conv_w, conv_b)


def _taps_dft_kernel(z_ref, w1_ref, b1_ref, f1_ref, w2_ref, b2_ref, f2_ref, w3a_ref, w3b_ref, dl_ref, fm_ref,
                     o_hbm, obuf, sem):
    g = pl.program_id(0)
    na = L // FFT_P
    half = z_ref.shape[0] // 2
    zt = z_ref[0:half, :]
    zb = z_ref[half:, :]
    lane = lax.broadcasted_iota(I32, zt.shape, 1)
    dot = _dot_split
    h = jnp.sin(f1_ref[...] * (dot(jnp.concatenate([zt, zb], axis=1), w1_ref) + b1_ref[...]))
    h = jnp.sin(f2_ref[...] * (dot(h, w2_ref) + b2_ref[...]))

    def copies(step):
        return [pltpu.make_async_copy(obuf.at[n, jj], o_hbm.at[n, :, step * FFT_BB + jj, :], sem)
                for n in range(2) for jj in range(FFT_BB)]

    @pl.when(g > 0)
    def _():
        for cp in copies(g - 1):
            cp.wait()

    for hi, (zz, w3_ref) in enumerate(((zt, w3a_ref), (zb, w3b_ref))):
        taps = dot(h, w3_ref)
        win = jnp.exp(-_lane_pick(zz, lane, 0) * dl_ref[...])
        fwd = (taps[:, :HYW] * win).astype(BF16)
        bwd = (taps[:, HYW:] * win * _lane_pick(zz, lane, HY_EMB)).astype(BF16)
        for bb in range(FFT_BB // 2):
            jj = hi * (FFT_BB // 2) + bb
            rows = slice(bb * na, (bb + 1) * na)
            obuf[0, jj] = jnp.dot(fm_ref[...], fwd[rows], preferred_element_type=F32)
            obuf[1, jj] = jnp.dot(fm_ref[...], bwd[rows], preferred_element_type=F32)

    for cp in copies(g):
        cp.start()

    @pl.when(g == pl.num_programs(0) - 1)
    def _():
        for cp in copies(g):
            cp.wait()


def _filter_taps_dft(zin, w1, b1, f1, w2, b2, f2, w3a, w3b, deltas, fmat):
    na = L // FFT_P
    tm = FFT_BB * na
    ln = V7X_LANES
    full = lambda shape: pl.BlockSpec(shape, lambda i: tuple(0 for _ in shape))
    return pl.pallas_call(
        _taps_dft_kernel,
        grid=(FFT_P // FFT_BB,),
        in_specs=[pl.BlockSpec((tm, ln), lambda i: (i, 0)),
                  full((2, 2 * ln, ln)), full((1, ln)), full((1, ln)),
                  full((2, ln, ln)), full((1, ln)), full((1, ln)),
                  full((2, ln, 2 * HYW)), full((2, ln, 2 * HYW)), full((1, HYW)), full((FFT_P, na))],
        out_specs=pl.BlockSpec(memory_space=pl.ANY),
        out_shape=jax.ShapeDtypeStruct((2, FFT_P, FFT_P, HYW), F32),
        scratch_shapes=[pltpu.VMEM((2, FFT_BB, FFT_P, HYW), F32), pltpu.SemaphoreType.DMA(())],
        compiler_params=_cparams(1, VMEM_LIMIT),
        name="filter_taps_dft",
    )(zin, w1, b1, f1, w2, b2, f2, w3a, w3b, deltas, fmat)


def _strided_dft_kernel(x_hbm, f_ref, o_hbm, xbuf, obuf, sem_in, sem_out):
    g = pl.program_id(0)
    ng = pl.num_programs(0)
    nb = FFT_P // FFT_BB

    def copies(grp, slot, inbound):
        n = grp // nb
        b0 = (grp % nb) * FFT_BB
        if inbound:
            return [pltpu.make_async_copy(x_hbm.at[n, :, b0 + jj, :], xbuf.at[slot, jj], sem_in.at[slot])
                    for jj in range(FFT_BB)]
        return [pltpu.make_async_copy(obuf.at[slot, jj], o_hbm.at[n, :, b0 + jj, :], sem_out.at[slot])
                for jj in range(FFT_BB)]

    def start(grp, slot, inbound):
        for cp in copies(grp, slot, inbound):
            cp.start()

    def wait(grp, slot, inbound):
        for cp in copies(grp, slot, inbound):
            cp.wait()

    @pl.when(g == 0)
    def _():
        start(0, 0, True)

    for slot in range(2):
        grp = 2 * g + slot
        if slot == 0:
            start(grp + 1, 1, True)
        else:
            @pl.when(g + 1 < ng)
            def _():
                start(grp + 1, 0, True)
        wait(grp, slot, True)

        @pl.when(g > 0)
        def _():
            wait(grp - 2, slot, False)

        for jj in range(FFT_BB):
            obuf[slot, jj] = jnp.dot(f_ref[...], xbuf[slot, jj].astype(BF16), preferred_element_type=F32)
        start(grp, slot, False)

    @pl.when(g + 1 == ng)
    def _():
        wait(2 * g, 0, False)
        wait(2 * g + 1, 1, False)


def _strided_dft(xv, fmat, name):
    n, kk = xv.shape[0], xv.shape[1]
    mm = fmat.shape[0]
    groups = n * (FFT_P // FFT_BB)
    return pl.pallas_call(
        _strided_dft_kernel,
        grid=(groups // 2,),
        in_specs=[pl.BlockSpec(memory_space=pl.ANY),
                  pl.BlockSpec((mm, kk), lambda g: (0, 0))],
        out_specs=pl.BlockSpec(memory_space=pl.ANY),
        out_shape=jax.ShapeDtypeStruct((n, mm, FFT_P, HYW), F32),
        scratch_shapes=[pltpu.VMEM((2, FFT_BB, kk, HYW), F32), pltpu.VMEM((2, FFT_BB, mm, HYW), F32),
                        pltpu.SemaphoreType.DMA((2,)), pltpu.SemaphoreType.DMA((2,))],
        compiler_params=_cparams(1, VMEM_LIMIT),
        name=name,
    )(xv, fmat)


def _cblock(mr, mi):
    return jnp.concatenate([jnp.concatenate([mr, -mi], axis=1), jnp.concatenate([mi, mr], axis=1)], axis=0)


FFT_H = FFT_P // 2
KSPEC_DD = 4
MID_DD = 2


def _twiddled(fr_ref, fi_ref, tw):
    twr = tw[0:1, :]
    twi = tw[1:2, :]
    fr = fr_ref[...]
    fi = fi_ref[...]
    return fr * twr - fi * twi, fr * twi + fi * twr


def _cmul_rows(x, kr, ki):
    xr, xi = x[:FFT_P], x[FFT_P:]
    return jnp.concatenate([xr * kr - xi * ki, xr * ki + xi * kr], axis=0).astype(BF16)


_TN_DIMS = (((0,), (0,)), ((), ()))

def _kspec_kernel(a_ref, fr_ref, fi_ref, tw_ref, twh_ref, o_ref, oh_ref):
    g = pl.program_id(0)
    dot = functools.partial(jnp.dot, preferred_element_type=F32)

    def combine(xf, xb, store):
        store(0, xf[:FFT_P] + xb[:FFT_P])
        store(1, xf[FFT_P:] - xb[FFT_P:])

    def general(u):
        def store(ri, val):
            o_ref[ri, u] = val
        rm = _cblock(*_twiddled(fr_ref, fi_ref, tw_ref[u])).astype(BF16)
        combine(dot(rm, a_ref[0, u].reshape(2 * FFT_P, HYW).astype(BF16)),
                dot(rm, a_ref[1, u].reshape(2 * FFT_P, HYW).astype(BF16)), store)

    @pl.when(g > 0)
    def _():
        general(0)

    @pl.when(g == 0)
    def _():
        def store_first(ri, val):
            o_ref[ri, 0] = val

        def store_half(ri, val):
            oh_ref[ri] = val
        for slot, tw, store in ((0, tw_ref[0], store_first), (1, twh_ref[...], store_half)):
            w = jnp.concatenate(_twiddled(fr_ref, fi_ref, tw), axis=0).astype(BF16)
            combine(dot(w, a_ref[0, 0, slot].astype(BF16)), dot(w, a_ref[1, 0, slot].astype(BF16)), store)

    for u in range(1, KSPEC_DD):
        general(u)


def _kernel_spectrum(ak, fr, fi, tw, twh):
    same = lambda shape: pl.BlockSpec(shape, lambda g: tuple(0 for _ in shape))
    return pl.pallas_call(
        _kspec_kernel,
        grid=(FFT_H // KSPEC_DD,),
        in_specs=[pl.BlockSpec((2, KSPEC_DD, 2, FFT_P, HYW), lambda g: (0, g, 0, 0, 0)),
                  same((FFT_P, FFT_P)), same((FFT_P, FFT_P)),
                  pl.BlockSpec((KSPEC_DD, 2, FFT_P), lambda g: (g, 0, 0)), same((2, FFT_P))],
        out_specs=(pl.BlockSpec((2, KSPEC_DD, FFT_P, HYW), lambda g: (0, g, 0, 0)),
                   pl.BlockSpec((2, FFT_P, HYW), lambda g: (0, 0, 0))),
        out_shape=(jax.ShapeDtypeStruct((2, FFT_H, FFT_P, HYW), F32),
                   jax.ShapeDtypeStruct((2, FFT_P, HYW), F32)),
        compiler_params=_cparams(1, VMEM_LIMIT),
        name="kernel_spectrum",
    )(ak, fr, fi, tw, twh)


def _mid_kernel(a_ref, kh_ref, khh_ref, fr_ref, fi_ref, tw_ref, twh_ref, o_ref):
    g = pl.program_id(0)
    dot = functools.partial(jnp.dot, preferred_element_type=F32)
    dot_t = lambda w, y: lax.dot_general(w, y, _TN_DIMS, preferred_element_type=F32)

    def general(u):
        rm = _cblock(*_twiddled(fr_ref, fi_ref, tw_ref[u])).astype(BF16)
        for n in range(B):
            x = dot(rm, a_ref[n, u].reshape(2 * FFT_P, HYW).astype(BF16))
            y = _cmul_rows(x, kh_ref[0, u], kh_ref[1, u])
            o_ref[n, u] = dot_t(rm, y).reshape(2, FFT_P, HYW)

    @pl.when(g > 0)
    def _():
        general(0)

    @pl.when(g == 0)
    def _():
        for slot, tw, kr, ki in ((0, tw_ref[0], kh_ref[0, 0], kh_ref[1, 0]), (1, twh_ref[...], khh_ref[0], khh_ref[1])):
            w = jnp.concatenate(_twiddled(fr_ref, fi_ref, tw), axis=0).astype(BF16)
            for n in range(B):
                y = _cmul_rows(dot(w, a_ref[n, 0, slot].astype(BF16)), kr, ki)
                o_ref[n, 0, slot] = dot_t(w, y)

    for u in range(1, MID_DD):
        general(u)


def _fft_mid(au, kh, khh, fr, fi, tw, twh):
    pair = pl.BlockSpec((B, MID_DD, 2, FFT_P, HYW), lambda g: (0, g, 0, 0, 0))
    same = lambda shape: pl.BlockSpec(shape, lambda g: tuple(0 for _ in shape))
    return pl.pallas_call(
        _mid_kernel,
        grid=(FFT_H // MID_DD,),
        in_specs=[pair,
                  pl.BlockSpec((2, MID_DD, FFT_P, HYW), lambda g: (0, g, 0, 0)),
                  same((2, FFT_P, HYW)), same((FFT_P, FFT_P)), same((FFT_P, FFT_P)),
                  pl.BlockSpec((MID_DD, 2, FFT_P), lambda g: (g, 0, 0)), same((2, FFT_P))],
        out_specs=pair,
        out_shape=jax.ShapeDtypeStruct((B, FFT_H, 2, FFT_P, HYW), F32),
        compiler_params=_cparams(1, VMEM_LIMIT),
        name="fft_mid",
    )(au, kh, khh, fr, fi, tw, twh)


def _dft_tables():
    na = L // FFT_P
    a = np.arange(na)
    dd = np.arange(FFT_H)
    ang = 2.0 * np.pi * np.outer(dd, a) / FFT_P
    re_rows = np.cos(ang)
    im_rows = -np.sin(ang)
    im_rows[0] = np.cos(np.pi * a)
    f_first = np.stack([re_rows, im_rows], axis=1).reshape(FFT_P, na)
    gre = 2.0 * np.cos(ang)
    gim = -2.0 * np.sin(ang)
    gre[0] = 1.0
    gim[0] = np.cos(np.pi * a)
    g_last = np.stack([gre, gim], axis=1).reshape(FFT_P, na).T / FFT_N
    b = np.arange(FFT_P)
    angf = 2.0 * np.pi * np.outer(b, b) / FFT_P
    ang2 = 2.0 * np.pi * np.outer(np.arange(FFT_H + 1), b) / FFT_N
    tw = np.stack([np.cos(ang2), -np.sin(ang2)], axis=1)
    f32 = lambda x: jnp.asarray(x.astype(np.float32))
    return (f32(f_first).astype(BF16), f32(g_last).astype(BF16), f32(np.cos(angf)), f32(-np.sin(angf)),
            f32(tw[:FFT_H]), f32(tw[FFT_H]))


def _merge_kernel(ya_ref, cv_ref, vx_ref, x0_ref, ga_ref, gb_ref, x_ref, er_ref, ec_ref, gt1_ref, hyd_ref,
                  wpa_ref, wpb_ref, wo_ref, g2_ref, sh2_ref, sc2_ref, wr_ref, br_ref,
                  x1_ref, t2_ref, lg_ref):
    vx = vx_ref[...]
    yb = x0_ref[...].astype(F32) * (cv_ref[...] + vx * hyd_ref[...])
    pa = jnp.dot(ya_ref[...].astype(BF16), wpa_ref[...], preferred_element_type=F32)
    pb = jnp.dot(yb.astype(BF16), wpb_ref[...], preferred_element_type=F32)
    mixed = (jax.nn.sigmoid(ga_ref[...].astype(F32)) * pa
             + jax.nn.sigmoid(gb_ref[...].astype(F32)) * pb)
    xm = jnp.dot(mixed.astype(BF16), wo_ref[...], preferred_element_type=F32)
    nrow = TM_MG // GRID_W
    row0 = (pl.program_id(0) % (L // TM_MG)) * nrow
    x1 = x_ref[...] + _pos_tile(er_ref, ec_ref, row0, nrow) + gt1_ref[...] * xm
    x1_ref[...] = x1
    t2 = _rms(x1) * g2_ref[...]
    t2 = t2 * (1.0 + sc2_ref[...]) + sh2_ref[...]
    t2_ref[...] = t2
    t_hi = t2.astype(BF16)
    t_lo = (t2 - t_hi.astype(F32)).astype(BF16)
    rr = (jnp.dot(t_hi, wr_ref[...], preferred_element_type=F32)
          + jnp.dot(t_lo, wr_ref[...], preferred_element_type=F32))
    lg_ref[...] = rr[:, :V7X_LANES] + rr[:, V7X_LANES:] + br_ref[...]


def _merge(ya, cv, vx, x0c, p, x2, er, ec, gt1, hyd, wpa, wpb, wo, g2, sh2, sc2, wr, br):
    tpb = L // TM_MG
    half = lambda: pl.BlockSpec((TM_MG, HYW), lambda i: (i, 0))
    full = lambda shape: pl.BlockSpec(shape, lambda i: tuple(0 for _ in shape))
    perb = lambda: pl.BlockSpec((None, 1, D), lambda i: (i // tpb, 0, 0))
    return pl.pallas_call(
        _merge_kernel,
        grid=(T // TM_MG,),
        in_specs=[half(), half(), half(), half(),
                  pl.BlockSpec((TM_MG, D), lambda i: (i, 3)),
                  pl.BlockSpec((TM_MG, D), lambda i: (i, 4)),
                  pl.BlockSpec((TM_MG, D), lambda i: (i, 0)),
                  full((L // GRID_W, D // 2)), full((GRID_W, D // 2)),
                  perb(), full((1, HYW)),
                  full((KD, D)), full((HYW, D)), full((D, D)),
                  full((1, D)), perb(), perb(),
                  full((D, 2 * V7X_LANES)), full((1, V7X_LANES))],
        out_specs=(pl.BlockSpec((TM_MG, D), lambda i: (i, 0)),
                   pl.BlockSpec((TM_MG, D), lambda i: (i, 0)),
                   pl.BlockSpec((TM_MG, V7X_LANES), lambda i: (i, 0))),
        out_shape=(jax.ShapeDtypeStruct((T, D), F32), jax.ShapeDtypeStruct((T, D), F32),
                   jax.ShapeDtypeStruct((T, V7X_LANES), F32)),
        compiler_params=_cparams(1, VMEM_LIMIT),
        name="merge",
    )(ya, cv, vx, x0c, p, p, x2, er, ec, gt1, hyd, wpa, wpb, wo, g2, sh2, sc2, wr, br)


def _route_kernel(lg_ref, info_ref, cnt_ref):
    @pl.when(pl.program_id(0) == 0)
    def _():
        cnt_ref[...] = jnp.zeros_like(cnt_ref)

    lg = lg_ref[...]
    lane = lax.broadcasted_iota(I32, lg.shape, 1)
    lanef = lane.astype(F32)
    neg = -1e30
    big = 1e9
    is_g = (lane >= NEXP) & (lane < NEXP + NGRP)
    gl = jnp.where(is_g, lg, neg)
    ge = jnp.where(is_g, jnp.exp(gl - jnp.max(gl, axis=-1, keepdims=True)), 0.0)
    pg = ge / jnp.sum(ge, axis=-1, keepdims=True)
    p_top_g = jnp.max(pg, axis=-1, keepdims=True)
    gidx = jnp.min(jnp.where(is_g & (pg == p_top_g), lanef, big), axis=-1, keepdims=True)
    g_sel = gidx.astype(I32) - NEXP
    emask = (lane < NEXP) & (jnp.right_shift(lane, NEPG.bit_length() - 1) == g_sel)
    el = jnp.where(emask, lg, neg)
    ee = jnp.where(emask, jnp.exp(el - jnp.max(el, axis=-1, keepdims=True)), 0.0)
    pe = ee / jnp.sum(ee, axis=-1, keepdims=True)
    p1 = jnp.max(jnp.where(emask, pe, -1.0), axis=-1, keepdims=True)
    i1 = jnp.min(jnp.where(emask & (pe == p1), lanef, big), axis=-1, keepdims=True)
    rest = emask & (lanef != i1)
    p2 = jnp.max(jnp.where(rest, pe, -1.0), axis=-1, keepdims=True)
    i2 = jnp.min(jnp.where(rest & (pe == p2), lanef, big), axis=-1, keepdims=True)
    wsum = p1 + p2
    w1 = p_top_g * p1 / wsum
    w2 = p_top_g * p2 / wsum
    sel1 = lanef == i1
    sel2 = lanef == i2
    oh = jnp.where(sel1 | sel2, 1.0, 0.0)
    r = lax.broadcasted_iota(I32, (TR, TR), 0)
    c = lax.broadcasted_iota(I32, (TR, TR), 1)
    stril = jnp.where(c < r, 1.0, 0.0).astype(BF16)
    before = jnp.dot(stril, oh.astype(BF16), preferred_element_type=F32) + cnt_ref[...]
    r1 = jnp.sum(jnp.where(sel1, before, 0.0), axis=-1, keepdims=True)
    r2 = jnp.sum(jnp.where(sel2, before, 0.0), axis=-1, keepdims=True)
    cnt_ref[...] += jnp.sum(oh, axis=0, keepdims=True)
    info = jnp.where(lane == 0, i1, jnp.where(lane == 1, r1, jnp.where(lane == 2, i2, jnp.where(
        lane == 3, r2, jnp.where(lane == 4, w1, jnp.where(lane == 5, w2, 0.0))))))
    info_ref[...] = info


def _route(lg):
    return pl.pallas_call(
        _route_kernel,
        grid=(T // TR,),
        in_specs=[pl.BlockSpec((TR, V7X_LANES), lambda i: (i, 0))],
        out_specs=(pl.BlockSpec((TR, V7X_LANES), lambda i: (i, 0)),
                   pl.BlockSpec((1, V7X_LANES), lambda i: (0, 0))),
        out_shape=(jax.ShapeDtypeStruct((T, V7X_LANES), F32), jax.ShapeDtypeStruct((1, V7X_LANES), F32)),
        compiler_params=_cparams(1, VMEM_LIMIT),
        name="route",
    )(lg)


def _positions_kernel(info_ref, st_ref, o_ref):
    info = info_ref[...]
    lane = lax.broadcasted_iota(I32, info.shape, 1)
    lanef = lane.astype(F32)
    st = st_ref[...]
    row = lambda e_lane, r_lane: (jnp.sum(jnp.where(lanef == _lane_pick(info, lane, e_lane), st, 0.0),
                                          axis=-1, keepdims=True) + _lane_pick(info, lane, r_lane))
    o_ref[...] = jnp.where(lane == 0, row(0, 1), jnp.where(lane == 1, row(2, 3), 0.0)).astype(I32)


def _positions(info, starts_row):
    return pl.pallas_call(
        _positions_kernel,
        grid=(T // TPOS,),
        in_specs=[pl.BlockSpec((TPOS, V7X_LANES), lambda i: (i, 0)),
                  pl.BlockSpec((1, V7X_LANES), lambda i: (0, 0))],
        out_specs=pl.BlockSpec((TPOS, V7X_LANES), lambda i: (i, 0)),
        out_shape=jax.ShapeDtypeStruct((T, V7X_LANES), I32),
        compiler_params=_cparams(1, VMEM_LIMIT),
        name="positions",
    )(info, starts_row)


def _scatter_kernel(pos_ref, zt_ref, t2_hbm, zeros_hbm, xs_hbm, tbuf, fsem, sem, zsem):
    i = pl.program_id(0)

    def zcopy(row):
        start = pl.multiple_of(jnp.maximum(row, 0), TE)
        return pltpu.make_async_copy(zeros_hbm, xs_hbm.at[pl.ds(start, TE)], zsem)

    @pl.when(i == 0)
    def _():
        def ztail(start, e, carry):
            @pl.when(zt_ref[0, e] >= 0)
            def _():
                cp = zcopy(zt_ref[0, e])
                cp.start() if start else cp.wait()
            return carry

        lax.fori_loop(0, NEXP, functools.partial(ztail, True), 0)
        lax.fori_loop(0, NEXP, functools.partial(ztail, False), 0)

        def zrest(start, tile, carry):
            cp = zcopy(tile * TE)
            cp.start() if start else cp.wait()
            return carry

        lax.fori_loop(zt_ref[0, NA_OFF], NT_EXP, functools.partial(zrest, True), 0)
        lax.fori_loop(zt_ref[0, NA_OFF], NT_EXP, functools.partial(zrest, False), 0)

    n = pl.num_programs(0)
    slot = i % SCATTER_SLOTS

    def fetch(tile, s):
        return pltpu.make_async_copy(t2_hbm.at[pl.ds(pl.multiple_of(tile * TS, TS), TS)], tbuf.at[s], fsem.at[s])

    def wait_rows(s):
        _wait_rows(TS, pltpu.make_async_copy(tbuf.at[s].at[pl.ds(0, 1)], xs_hbm.at[pl.ds(0, 1)], sem.at[s]))

    @pl.when(i == 0)
    def _():
        fetch(0, 0).start()

    @pl.when(i + 1 < n)
    def _():
        fetch(i + 1, (i + 1) % SCATTER_SLOTS).start()

    fetch(i, slot).wait()

    def row_copy(base, jj, kk):
        dst = pos_ref[0, 2 * (base + jj) + kk]
        return pltpu.make_async_copy(tbuf.at[slot].at[pl.ds(base, ROW_GROUP)].at[pl.ds(jj, 1)],
                                     xs_hbm.at[pl.ds(dst, 1)], sem.at[slot])

    _start_rows(TS, row_copy)

    @pl.when(i > 0)
    def _():
        wait_rows((i + SCATTER_SLOTS - 1) % SCATTER_SLOTS)

    @pl.when(i + 1 == n)
    def _():
        wait_rows(slot)


def _start_rows(n_rows, row_copy):
    def group(g, carry):
        base = pl.multiple_of(g * ROW_GROUP, ROW_GROUP)
        for jj in range(ROW_GROUP):
            for kk in range(2):
                row_copy(base, jj, kk).start(priority=kk)
        return carry

    lax.fori_loop(0, n_rows // ROW_GROUP, group, 0)


def _wait_rows(n_rows, one_row_copy):
    def drain(j, carry):
        one_row_copy.wait()
        one_row_copy.wait()
        return carry

    lax.fori_loop(0, n_rows, drain, 0, unroll=ROW_UNROLL)


def _scatter_rows(pos3, meta, t2, zeros_tile):
    return pl.pallas_call(
        _scatter_kernel,
        grid=(T // TS,),
        in_specs=[pl.BlockSpec((None, 1, 2 * TS), lambda i: (i, 0, 0), memory_space=pltpu.SMEM),
                  pl.BlockSpec(memory_space=pltpu.SMEM),
                  pl.BlockSpec(memory_space=pl.ANY),
                  pl.BlockSpec(memory_space=pl.ANY)],
        out_specs=pl.BlockSpec(memory_space=pl.ANY),
        out_shape=jax.ShapeDtypeStruct((NP_ROWS, D), F32),
        scratch_shapes=[pltpu.VMEM((SCATTER_SLOTS, TS, D), F32), pltpu.SemaphoreType.DMA((SCATTER_SLOTS,)),
                        pltpu.SemaphoreType.DMA((SCATTER_SLOTS,)), pltpu.SemaphoreType.DMA(())],
        compiler_params=_cparams(1, VMEM_LIMIT),
        name="scatter_rows",
    )(pos3, meta, t2, zeros_tile)


def _expert_kernel(te_ref, nx_ref, sl_ref, na_ref, xs_ref, wg_hbm, wu_hbm, wd_hbm, ys_ref,
                   wgs_ref, wus_ref, wds_ref, wgb_ref, wub_ref, wdb_ref, sem):
    i = pl.program_id(0)
    active = i < na_ref[0]
    first = active & ((i == 0) | (te_ref[i] != te_ref[jnp.maximum(i - 1, 0)]))

    def fetch(e, slot):
        return [pltpu.make_async_copy(wg_hbm.at[e], wgs_ref.at[slot], sem.at[slot]),
                pltpu.make_async_copy(wu_hbm.at[e], wus_ref.at[slot], sem.at[slot]),
                pltpu.make_async_copy(wd_hbm.at[e], wds_ref.at[slot], sem.at[slot])]

    for slot in range(2):
        @pl.when(first & (sl_ref[i] == slot))
        def _():
            @pl.when(i == 0)
            def _():
                for cp in fetch(te_ref[i], slot):
                    cp.start()

            for cp in fetch(te_ref[i], slot):
                cp.wait()
            wgb_ref[...] = wgs_ref[slot].astype(BF16)
            wub_ref[...] = wus_ref[slot].astype(BF16)
            wdb_ref[...] = wds_ref[slot].astype(BF16)

            @pl.when(nx_ref[i] >= 0)
            def _():
                for cp in fetch(nx_ref[i], 1 - slot):
                    cp.start()

    @pl.when(active)
    def _():
        x = xs_ref[...].astype(BF16)
        g = jnp.dot(x, wgb_ref[...], preferred_element_type=F32)
        u = jnp.dot(x, wub_ref[...], preferred_element_type=F32)
        hid = (g * jax.nn.sigmoid(g) * u).astype(BF16)
        ys_ref[...] = jnp.dot(hid, wdb_ref[...], preferred_element_type=F32)

    @pl.when(jnp.logical_not(active))
    def _():
        ys_ref[...] = jnp.zeros_like(ys_ref)


def _experts(tile_expert, tile_next, tile_slot, n_active, xs, wg, wu, wd):
    rows = lambda i, *_: (i, 0)
    rows_in = lambda i, te, nx, sl, na: (jnp.minimum(i, na[0] - 1), 0)
    any_space = pl.BlockSpec(memory_space=pl.ANY)
    grid_spec = pltpu.PrefetchScalarGridSpec(
        num_scalar_prefetch=4,
        grid=(NT_EXP,),
        in_specs=[pl.BlockSpec((TE, D), rows_in), any_space, any_space, any_space],
        out_specs=pl.BlockSpec((TE, D), rows),
        scratch_shapes=[pltpu.VMEM((2, D, DEXP), F32), pltpu.VMEM((2, D, DEXP), F32), pltpu.VMEM((2, DEXP, D), F32),
                        pltpu.VMEM((D, DEXP), BF16), pltpu.VMEM((D, DEXP), BF16), pltpu.VMEM((DEXP, D), BF16),
                        pltpu.SemaphoreType.DMA((2,))],
    )
    return pl.pallas_call(
        _expert_kernel,
        grid_spec=grid_spec,
        out_shape=jax.ShapeDtypeStruct((NP_ROWS, D), F32),
        compiler_params=_cparams(1, VMEM_LIMIT),
        name="experts",
    )(tile_expert, tile_next, tile_slot, n_active, xs, wg, wu, wd)


def _combine_kernel(pos_ref, posn_ref, info_ref, x1_ref, gt2_ref, fg_ref, ys_hbm, o_ref, buf, sem):
    g = pl.program_id(0)
    ng = pl.num_programs(0)

    def gather(p_ref, half, slot):
        def row_copy(base, jj, kk):
            src = p_ref[0, 2 * (half * TC + base + jj) + kk]
            return pltpu.make_async_copy(ys_hbm.at[pl.ds(src, 1)],
                                         buf.at[slot, kk].at[pl.ds(base, ROW_GROUP)].at[pl.ds(jj, 1)],
                                         sem.at[slot])
        _start_rows(TC, row_copy)

    def finish(half, slot):
        _wait_rows(TC, pltpu.make_async_copy(ys_hbm.at[pl.ds(0, 1)], buf.at[slot, 0].at[pl.ds(0, 1)],
                                             sem.at[slot]))
        rows = slice(half * TC, (half + 1) * TC)
        info = info_ref[rows, :]
        lane = lax.broadcasted_iota(I32, info.shape, 1)
        moe = _lane_pick(info, lane, 4) * buf[slot, 0] + _lane_pick(info, lane, 5) * buf[slot, 1]
        x2 = x1_ref[rows, :] + gt2_ref[...] * moe
        o_ref[rows, :] = _rms(x2) * fg_ref[...]

    @pl.when(g == 0)
    def _():
        gather(pos_ref, 0, 0)

    gather(pos_ref, 1, 1)
    finish(0, 0)

    @pl.when(g + 1 < ng)
    def _():
        gather(posn_ref, 0, 0)

    finish(1, 1)


def _combine(pos3, info, x1, gt2, fg, ys):
    step = 2 * TC
    tpb = L // step
    nsteps = T // step
    return pl.pallas_call(
        _combine_kernel,
        grid=(nsteps,),
        in_specs=[pl.BlockSpec((None, 1, 2 * step), lambda i: (i, 0, 0), memory_space=pltpu.SMEM),
                  pl.BlockSpec((None, 1, 2 * step), lambda i: (jnp.minimum(i + 1, nsteps - 1), 0, 0),
                               memory_space=pltpu.SMEM),
                  pl.BlockSpec((step, V7X_LANES), lambda i: (i, 0)),
                  pl.BlockSpec((step, D), lambda i: (i, 0)),
                  pl.BlockSpec((None, 1, D), lambda i: (i // tpb, 0, 0)),
                  pl.BlockSpec((1, D), lambda i: (0, 0)),
                  pl.BlockSpec(memory_space=pl.ANY)],
        out_specs=pl.BlockSpec((step, D), lambda i: (i, 0)),
        out_shape=jax.ShapeDtypeStruct((T, D), F32),
        scratch_shapes=[pltpu.VMEM((2, 2, TC, D), F32), pltpu.SemaphoreType.DMA((2,))],
        compiler_params=_cparams(1, VMEM_LIMIT),
        name="combine",
    )(pos3, pos3, info, x1, gt2, fg, ys)


def _pos_tables():
    rows = L // GRID_W
    quarter = D // 4
    omega = 1.0 / (10000.0 ** (jnp.arange(quarter, dtype=F32) / quarter))

    def axis_emb(pos):
        a = pos[:, None] * omega[None, :]
        return jnp.concatenate([jnp.sin(a), jnp.cos(a)], axis=-1)

    er = axis_emb(jnp.arange(rows, dtype=F32))
    ec = axis_emb(jnp.arange(GRID_W, dtype=F32))
    return er, ec


def _filter_features():
    z = np.zeros((L, V7X_LANES), np.float64)
    bands = (HY_EMB - 1) // 2
    ang = (2.0 * np.pi * np.arange(L) / L)[:, None] * np.linspace(1e-4, bands - 1, bands)[None, :]
    z[:, 0] = np.linspace(0.0, 1.0, L)
    z[:, 1:1 + bands] = np.cos(ang)
    z[:, 1 + bands:HY_EMB] = -np.sin(ang)
    z[1:, HY_EMB] = 1.0
    z = z.reshape(L // FFT_P, FFT_P, V7X_LANES).transpose(1, 0, 2).reshape(L, V7X_LANES)
    return jnp.asarray(z.astype(np.float32))


def _pad2(a, rows, cols):
    return jnp.pad(a, ((0, rows - a.shape[0]), (0, cols - a.shape[1])))


def kernel(x, c, ctx, c_ctx, ada_w, ada_b, norm1_g, norm2_g, w_in, hgrn_lb, hgrn_norm_g, hy_conv_w, hy_conv_b, hy_filt_w1, hy_filt_b1, hy_filt_freq1, hy_filt_w2, hy_filt_b2, hy_filt_freq2, hy_filt_w3, hy_d, w_proj_a, w_proj_b, w_out, moe_router_g_w, moe_router_g_b, moe_router_e_w, moe_router_e_b, moe_w_gate, moe_w_up, moe_w_down, final_norm_g):
    cvec = jnp.zeros((8, D), F32).at[0:B].set(c).at[B].set(c_ctx)
    mod = _adaln(cvec, ada_w[0], ada_b[0][None, :])
    m6 = mod.reshape(8, 6, D)
    sh1, sc1, gt1, sh2, sc2, gt2 = [m6[0:B, k][:, None, :] for k in range(6)]
    csh1, csc1 = m6[B:B + 1, 0], m6[B:B + 1, 1]

    lbs = jnp.cumsum(jax.nn.softmax(hgrn_lb.astype(F32), axis=0), axis=0)[0]
    g1 = norm1_g[0][None, :]
    er, ec = _pos_tables()
    x2 = x.reshape(T, D)

    w_ctx = w_in[0][:, KD:4 * KD].astype(BF16)
    s_f, s_b = _context_states(ctx, g1, csh1, csc1, w_ctx, lbs)

    pz, p = _in_projection(x2, er, ec, g1, sh1, sc1, w_in[0].astype(BF16))

    o_f = _hgrn_scan(pz, p, lbs[0:1], s_f, False)
    y_a = _hgrn_scan(pz, p, lbs[1:2], s_b, True, o_f=o_f, norm_g=hgrn_norm_g[0][None, :])

    vx, x0c = _hyena_pre(p, hy_conv_w[0], hy_conv_b[0][None, :])
    deltas = jnp.abs(jnp.linspace(math.log(HY_DECAY_TARGET) / HY_SLOW_PCT,
                                  math.log(HY_DECAY_TARGET) / HY_FAST_PCT, HYW, dtype=F32))[None, :]
    ln = V7X_LANES
    fh = hy_filt_w2.shape[-1]
    blockdiag = lambda m: jnp.concatenate([_pad2(m, m.shape[0], 2 * m.shape[1]),
                                           jnp.pad(m, ((0, 0), (m.shape[1], 0)))], axis=0)
    twice = lambda v: jnp.concatenate([v, v])[None, :]
    w3 = hy_filt_w3[0]
    f_first, g_last, fr, fi, tw, twh = _dft_tables()
    na = L // FFT_P
    ak = _filter_taps_dft(
        _filter_features(),
        _hi_lo(blockdiag(_pad2(hy_filt_w1[0], ln, fh))), twice(hy_filt_b1[0]), twice(hy_filt_freq1[0]),
        _hi_lo(blockdiag(hy_filt_w2[0])), twice(hy_filt_b2[0]), twice(hy_filt_freq2[0]),
        _hi_lo(_pad2(w3, ln, 2 * HYW)), _hi_lo(jnp.pad(w3, ((fh, 0), (0, 0)))), deltas, f_first)
    kh, khh = _kernel_spectrum(ak.reshape(2, FFT_H, 2, FFT_P, HYW), fr, fi, tw, twh)
    au = _strided_dft(vx.reshape(B, na, FFT_P, HYW), f_first, "dft_first")
    bp = _fft_mid(au.reshape(B, FFT_H, 2, FFT_P, HYW), kh, khh, fr, fi, tw, twh)
    conv = _strided_dft(bp.reshape(B, FFT_P, FFT_P, HYW), g_last, "dft_last").reshape(T, HYW)

    wr = jnp.concatenate([jnp.transpose(moe_router_e_w[0], (1, 0, 2)).reshape(D, NEXP),
                          moe_router_g_w[0], jnp.zeros((D, V7X_LANES - NEXP - NGRP), F32)], axis=1)
    wr_hi = wr.astype(BF16)
    wr = jnp.concatenate([wr_hi, (wr - wr_hi.astype(F32)).astype(BF16)], axis=1)
    br = jnp.concatenate([moe_router_e_b[0].reshape(NEXP), moe_router_g_b[0],
                          jnp.zeros((V7X_LANES - NEXP - NGRP,), F32)])[None, :]
    x1, t2, lg = _merge(y_a.reshape(T, KD), conv, vx, x0c, p, x2, er, ec, gt1, hy_d[0][None, :],
                        w_proj_a[0].astype(BF16), w_proj_b[0].astype(BF16), w_out[0].astype(BF16),
                        norm2_g[0][None, :], sh2, sc2, wr, br)

    info, counts = _route(lg)
    cnt = counts[0, :NEXP].astype(I32)
    pc = ((cnt + TE - 1) // TE) * TE
    ends = jnp.cumsum(pc)
    starts = ends - pc
    n_active = (ends[-1] // TE).astype(I32)[None]
    tile_rows = jnp.arange(NT_EXP, dtype=I32) * TE
    tile_expert = jnp.minimum(jnp.sum((ends[None, :] <= tile_rows[:, None]).astype(I32), axis=1), NEXP - 1)
    meta = jnp.concatenate([jnp.where(pc > 0, ends - TE, -1), starts, n_active]).astype(I32)[None, :]
    starts_row = jnp.pad(starts.astype(F32), (0, V7X_LANES - NEXP))[None, :]
    pos3 = _positions(info, starts_row)[:, :2].reshape(T // TS, 1, 2 * TS)

    xs = _scatter_rows(pos3, meta, t2, jnp.zeros((TE, D), F32))
    eid = jnp.arange(NEXP, dtype=I32)
    nonempty = pc > 0
    later = jnp.where(nonempty[None, :] & (eid[None, :] > eid[:, None]), eid[None, :], NEXP)
    next_e = jnp.min(later, axis=1)
    next_e = jnp.where(next_e < NEXP, next_e, -1).astype(I32)
    slot_e = ((jnp.cumsum(nonempty.astype(I32)) - 1) % 2).astype(I32)
    ys = _experts(tile_expert, next_e[tile_expert], slot_e[tile_expert], n_active, xs,
                  moe_w_gate[0].reshape(NEXP, D, DEXP), moe_w_up[0].reshape(NEXP, D, DEXP),
                  moe_w_down[0].reshape(NEXP, DEXP, D))
    out = _combine(pos3, info, x1, gt2, final_norm_g[None, :], ys)
    return out.reshape(B, L, D)
```

```python
import functools
import math

import numpy as np
import jax
import jax.numpy as jnp
from jax import lax
from jax.experimental import pallas as pl
from jax.experimental.pallas import tpu as pltpu

F32 = jnp.float32
BF16 = jnp.bfloat16
I32 = jnp.int32

D = 1024
B = 2
L = 8192
T = B * L
CTX = 256
GRID_W = 64
EPS = 1e-6
H = 4
DK = 128
DV = 128
KD = H * DK
IN_W = 6144
HYW = 512
HY_EMB = 33
NGRP = 4
NEPG = 8
NEXP = NGRP * NEPG
DEXP = 512
HY_DECAY_TARGET = 1e-2
HY_FAST_PCT = 0.3
HY_SLOW_PCT = 1.5

V7X_LANES = 128
V7X_SUBLANES = 8
V7X_VMEM_BYTES = 64 * 1024 * 1024
VMEM_LIMIT = (3 * V7X_VMEM_BYTES) // 4

FFT_N = 2 * L
FFT_P = 128
FFT_BB = 16

TM_IN = 1024
TN_IN = 1024
TH = 128
HG_CHUNKS = 4
CB = 32
TM_HY = 2048
HALO = 2 * V7X_SUBLANES
TM_MG = 512
TR = 1024
TE = 512
TPOS = 2048
NP_ROWS = 2 * T + NEXP * TE
NT_EXP = NP_ROWS // TE
TS = 512
TC = TS // 2
SCATTER_SLOTS = 3
ROW_UNROLL = 8
ROW_GROUP = 32
ST_OFF = NEXP
NA_OFF = 2 * NEXP


def _cparams(n_axes, vmem=None):
    return pltpu.CompilerParams(dimension_semantics=("arbitrary",) * n_axes,
                                vmem_limit_bytes=vmem)


def _split3(x):
    hi = x.astype(BF16)
    r = x - hi.astype(F32)
    mid = r.astype(BF16)
    lo = (r - mid.astype(F32)).astype(BF16)
    return hi, mid, lo


def _dot01(m, x):
    hi, mid, lo = _split3(x)
    return (jnp.dot(m, hi, preferred_element_type=F32) + jnp.dot(m, mid, preferred_element_type=F32)
            + jnp.dot(m, lo, preferred_element_type=F32))


def _dot_split(a, w_ref):
    a_hi = a.astype(BF16)
    a_lo = (a - a_hi.astype(F32)).astype(BF16)
    return (jnp.dot(a_hi, w_ref[0], preferred_element_type=F32) + jnp.dot(a_hi, w_ref[1], preferred_element_type=F32)
            + jnp.dot(a_lo, w_ref[0], preferred_element_type=F32))


def _hi_lo(w):
    hi = w.astype(BF16)
    return jnp.stack([hi, (w - hi.astype(F32)).astype(BF16)])


def _rms(x):
    return x * lax.rsqrt(jnp.mean(x * x, axis=-1, keepdims=True) + EPS)


def _lane_pick(x, lane, idx):
    return jnp.sum(jnp.where(lane == idx, x, 0.0), axis=-1, keepdims=True)


def _ada_kernel(c_ref, w_ref, b_ref, o_ref):
    c = c_ref[...]
    s = c * jax.nn.sigmoid(c)
    w = w_ref[...]
    w_hi = w.astype(BF16)
    w_lo = (w - w_hi.astype(F32)).astype(BF16)
    s_hi = s.astype(BF16)
    s_lo = (s - s_hi.astype(F32)).astype(BF16)
    o_ref[...] = (jnp.dot(s_hi, w_hi, preferred_element_type=F32) + jnp.dot(s_hi, w_lo, preferred_element_type=F32)
                  + jnp.dot(s_lo, w_hi, preferred_element_type=F32) + b_ref[...])


def _adaln(cvec, w, b):
    tn = 1536
    return pl.pallas_call(
        _ada_kernel,
        grid=(6 * D // tn,),
        in_specs=[pl.BlockSpec((8, D), lambda j: (0, 0)),
                  pl.BlockSpec((D, tn), lambda j: (0, j)),
                  pl.BlockSpec((1, tn), lambda j: (0, j))],
        out_specs=pl.BlockSpec((8, tn), lambda j: (0, j)),
        out_shape=jax.ShapeDtypeStruct((8, 6 * D), F32),
        compiler_params=_cparams(1, VMEM_LIMIT),
        name="adaln",
    )(cvec, w, b)


def _keys(z, lb):
    sig = jax.nn.sigmoid(z)
    logf = jnp.log(lb + (1.0 - lb) * sig)
    k = (1.0 - lb) * jax.nn.sigmoid(-z)
    return k, logf


def _ctx_kernel(ctx_ref, g_ref, sh_ref, sc_ref, w_ref, lb_ref, sf_ref, sb_ref):
    h = _rms(ctx_ref[...]) * g_ref[...]
    h = h * (1.0 + sc_ref[...]) + sh_ref[...]
    p = jnp.dot(h.astype(BF16), w_ref[...], preferred_element_type=F32)
    zf, zb, v = p[:, :KD], p[:, KD:2 * KD], p[:, 2 * KD:]
    kf, lf = _keys(zf, lb_ref[0:1, :])
    kb, lbk = _keys(zb, lb_ref[1:2, :])
    r = lax.broadcasted_iota(I32, (CTX, CTX), 0)
    c = lax.broadcasted_iota(I32, (CTX, CTX), 1)
    tril = jnp.where(c <= r, 1.0, 0.0).astype(BF16)
    cf = _dot01(tril, lf)
    cb = _dot01(tril, lbk)
    kfd = (kf * jnp.exp(cf[CTX - 1:CTX, :] - cf)).astype(BF16)
    kbd = (kb * jnp.exp(cb - lbk)).astype(BF16)
    vb = v.astype(BF16)
    tn = (((0,), (0,)), ((), ()))
    for hh in range(H):
        hs = slice(hh * DK, (hh + 1) * DK)
        sf_ref[hh] = lax.dot_general(vb[:, hs], kfd[:, hs], tn, preferred_element_type=F32)
        sb_ref[hh] = lax.dot_general(vb[:, hs], kbd[:, hs], tn, preferred_element_type=F32)


def _context_states(ctx, g1, csh1, csc1, w_ctx, lbs):
    st = jax.ShapeDtypeStruct((B, H, DV, DK), F32)
    return pl.pallas_call(
        _ctx_kernel,
        grid=(B,),
        in_specs=[pl.BlockSpec((None, CTX, D), lambda b: (b, 0, 0)),
                  pl.BlockSpec((1, D), lambda b: (0, 0)),
                  pl.BlockSpec((1, D), lambda b: (0, 0)),
                  pl.BlockSpec((1, D), lambda b: (0, 0)),
                  pl.BlockSpec((D, 3 * KD), lambda b: (0, 0)),
                  pl.BlockSpec((2, KD), lambda b: (0, 0))],
        out_specs=(pl.BlockSpec((None, H, DV, DK), lambda b: (b, 0, 0, 0)),
                   pl.BlockSpec((None, H, DV, DK), lambda b: (b, 0, 0, 0))),
        out_shape=(st, st),
        compiler_params=_cparams(1, VMEM_LIMIT),
        name="ctx_states",
    )(ctx, g1, csh1, csc1, w_ctx, lbs)


def _pos_tile(er_ref, ec_ref, row0, nrow):
    lo = jnp.concatenate([jnp.broadcast_to(er_ref[pl.ds(row0 + i, 1), :], (GRID_W, D // 2))
                          for i in range(nrow)], axis=0)
    hi = jnp.concatenate([ec_ref[...]] * nrow, axis=0)
    return jnp.concatenate([lo, hi], axis=1)


def _inproj_kernel(x_ref, er_ref, ec_ref, g_ref, sh_ref, sc_ref, w_ref, oz_ref, o_ref, hx_ref):
    i = pl.program_id(0)
    j = pl.program_id(1)
    ni = pl.num_programs(0)
    nj = pl.num_programs(1)
    slot = i % 2

    def prepare(tile, dst):
        nrow = TM_IN // GRID_W
        row0 = (tile % (L // TM_IN)) * nrow
        h = _rms(x_ref[...] + _pos_tile(er_ref, ec_ref, row0, nrow)) * g_ref[...]
        hx_ref[dst] = (h * (1.0 + sc_ref[...]) + sh_ref[...]).astype(BF16)

    def project():
        return jnp.dot(hx_ref[slot], w_ref[...], preferred_element_type=F32)

    @pl.when((i == 0) & (j == 0))
    def _():
        prepare(0, 0)

    @pl.when(j == 0)
    def _():
        r = project()
        o_ref[:, :KD] = r[:, :KD].astype(BF16)
        oz_ref[:, :KD] = r[:, KD:]

    @pl.when(j == 1)
    def _():
        r = project()
        oz_ref[:, KD:] = r[:, :KD]
        o_ref[:, KD:] = r[:, KD:].astype(BF16)

    @pl.when((j > 1) & (j < nj - 1))
    def _():
        o_ref[...] = project().astype(BF16)

    @pl.when((j == nj - 1) & (i + 1 < ni))
    def _():
        o_ref[...] = project().astype(BF16)
        prepare(i + 1, 1 - slot)

    @pl.when((j == nj - 1) & (i + 1 == ni))
    def _():
        o_ref[...] = project().astype(BF16)


def _in_projection(x2, er, ec, g1, sh1, sc1, w_bf):
    tiles_per_batch = L // TM_IN
    n_i = T // TM_IN
    n_j = IN_W // TN_IN
    ahead = lambda i, j: jnp.minimum(i + jnp.where(j == n_j - 1, 1, 0), n_i - 1)
    return pl.pallas_call(
        _inproj_kernel,
        grid=(n_i, n_j),
        in_specs=[pl.BlockSpec((TM_IN, D), lambda i, j: (ahead(i, j), 0)),
                  pl.BlockSpec((L // GRID_W, D // 2), lambda i, j: (0, 0)),
                  pl.BlockSpec((GRID_W, D // 2), lambda i, j: (0, 0)),
                  pl.BlockSpec((1, D), lambda i, j: (0, 0)),
                  pl.BlockSpec((None, 1, D), lambda i, j: (ahead(i, j) // tiles_per_batch, 0, 0)),
                  pl.BlockSpec((None, 1, D), lambda i, j: (ahead(i, j) // tiles_per_batch, 0, 0)),
                  pl.BlockSpec((D, TN_IN), lambda i, j: (0, j))],
        out_specs=(pl.BlockSpec((TM_IN, TN_IN), lambda i, j: (i, 0)),
                   pl.BlockSpec((TM_IN, TN_IN), lambda i, j: (i, jnp.maximum(j - 1, 0)))),
        out_shape=(jax.ShapeDtypeStruct((T, 2 * KD), F32),
                   jax.ShapeDtypeStruct((T, IN_W - 2 * KD), BF16)),
        scratch_shapes=[pltpu.VMEM((2, TM_IN, D), BF16)],
        compiler_params=_cparams(2, VMEM_LIMIT),
        name="in_proj",
    )(x2, er, ec, g1, sh1, sc1, w_bf)


def _hgrn_kernel(reverse, readout, *refs):
    if readout:
        q_ref, z_ref, v_ref, lb_ref, s0_ref, of_ref, g_ref, ng_ref, o_ref, st_ref = refs
    else:
        q_ref, z_ref, v_ref, lb_ref, s0_ref, o_ref, st_ref = refs

    @pl.when(pl.program_id(0) == 0)
    def _():
        st_ref[...] = s0_ref[...]

    r = lax.broadcasted_iota(I32, (TH, TH), 0)
    c = lax.broadcasted_iota(I32, (TH, TH), 1)
    cb_shift = CB.bit_length() - 1
    same = jnp.right_shift(r, cb_shift) == jnp.right_shift(c, cb_shift)
    tri_mask = same & ((c >= r) if reverse else (c <= r))
    tri = jnp.where(tri_mask, 1.0, 0.0).astype(BF16)
    rblk = jnp.right_shift(r, cb_shift)
    cblk = jnp.right_shift(c, cb_shift)
    dist = (rblk - cblk) if not reverse else (cblk - rblk)
    for cc in (range(HG_CHUNKS - 1, -1, -1) if reverse else range(HG_CHUNKS)):
        for b in range(B):
            _hgrn_chunk(reverse, readout, b, slice(cc * TH, (cc + 1) * TH), tri, tri_mask, dist, refs)


def _hgrn_chunk(reverse, readout, b, rs, tri, tri_mask, dist, refs):
    if readout:
        q_ref, z_ref, v_ref, lb_ref, s0_ref, of_ref, g_ref, ng_ref, o_ref, st_ref = refs
    else:
        q_ref, z_ref, v_ref, lb_ref, s0_ref, o_ref, st_ref = refs
    q = q_ref[b, rs, :].astype(F32)
    v = v_ref[b, rs, :]
    k, logf = _keys(z_ref[b, rs, :], lb_ref[...])
    bl = _dot01(tri, logf)
    nt = (((1,), (1,)), ((), ()))
    tn = (((0,), (0,)), ((), ()))
    nblk = TH // CB
    e_row = 0 if reverse else CB - 1
    m_row = CB - 1 - CB // 2 if reverse else CB // 2
    tau = [bl[jb * CB + e_row:jb * CB + e_row + 1] for jb in range(nblk)]
    mid = [bl[jb * CB + m_row:jb * CB + m_row + 1] for jb in range(nblk)]
    rows = lambda vecs: jnp.concatenate([jnp.broadcast_to(x, (CB, KD)) for x in vecs], axis=0)
    mid_b = rows(mid)
    qd0 = (q * jnp.exp(bl - mid_b)).astype(BF16)
    kd0 = (k * jnp.exp(mid_b - bl)).astype(BF16)
    qs = q * jnp.exp(bl)
    ke = k * jnp.exp(rows(tau) - bl)
    order = list(range(nblk - 1, -1, -1)) if reverse else list(range(nblk))
    pre = [jnp.zeros((1, KD), F32)]
    for i in range(nblk):
        pre.append(pre[-1] + tau[order[i]])
    total = pre[nblk]
    entry = [None] * nblk
    leave = [None] * nblk
    gap = [[None] * nblk for _ in range(nblk)]
    for i, jb in enumerate(order):
        entry[jb] = jnp.exp(pre[i])
        leave[jb] = jnp.exp(total - pre[i + 1])
        for d in range(2, nblk):
            gap[d][jb] = jnp.exp(pre[i + d] - pre[i + 1]) if i + d < nblk else jnp.zeros((1, KD), F32)
    qc = (qs * rows(entry)).astype(BF16)
    kc = (ke * rows(leave)).astype(BF16)
    kx = jnp.concatenate([ke.astype(BF16)] + [(ke * rows(gap[d])).astype(BF16) for d in range(2, nblk)], axis=0)
    qsb = qs.astype(BF16)
    dec = jnp.exp(total)

    def blockdiag(x):
        first_head = lax.broadcasted_iota(I32, x.shape, 1) < DK
        zero = jnp.zeros_like(x)
        return jnp.concatenate([jnp.where(first_head, x, zero), jnp.where(first_head, zero, x)], axis=0)

    for hp in range(H // 2):
        ps = slice(2 * hp * DK, 2 * (hp + 1) * DK)
        sc0 = lax.dot_general(qd0[:, ps], blockdiag(kd0[:, ps]), nt, preferred_element_type=F32)
        scx = lax.dot_general(qsb[:, ps], blockdiag(kx[:, ps]), nt, preferred_element_type=F32)
        halves = []
        for hh in range(2):
            sc = jnp.where(tri_mask, sc0[:, hh * TH:(hh + 1) * TH], 0.0)
            base = hh * (nblk - 1) * TH
            for d in range(1, nblk):
                sc = jnp.where(dist == d, scx[:, base + (d - 1) * TH:base + d * TH], sc)
            halves.append(sc.astype(BF16))
        sc_pair = jnp.concatenate(halves, axis=1)
        st_a = st_ref[b, 2 * hp]
        st_b = st_ref[b, 2 * hp + 1]
        zst = jnp.zeros((DV, DK), BF16)
        st_pair = jnp.concatenate([jnp.concatenate([st_a.astype(BF16), zst], axis=1),
                                   jnp.concatenate([zst, st_b.astype(BF16)], axis=1)], axis=0)
        o_pair = (lax.dot_general(qc[:, ps], st_pair, nt, preferred_element_type=F32)
                  + jnp.dot(sc_pair, blockdiag(v[:, ps]), preferred_element_type=F32))
        upd = lax.dot_general(v[:, ps], kc[:, ps], tn, preferred_element_type=F32)
        st_ref[b, 2 * hp] = st_a * dec[:, ps][:, :DK] + upd[:DV, :DK]
        st_ref[b, 2 * hp + 1] = st_b * dec[:, ps][:, DK:] + upd[DV:, DK:]
        for hh in range(2):
            hs = slice((2 * hp + hh) * DK, (2 * hp + hh + 1) * DK)
            o_h = o_pair[:, hh * DV:(hh + 1) * DV]
            if readout:
                o_h = o_h + of_ref[b, rs, hs]
                o_h = _rms(o_h) * ng_ref[...]
                gh = g_ref[b, rs, hs].astype(F32)
                o_h = o_h * (gh * jax.nn.sigmoid(gh))
            o_ref[b, rs, hs] = o_h.astype(o_ref.dtype)


def _hgrn_scan(pz, p, lb_row, s0, reverse, o_f=None, norm_g=None):
    nch = L // (HG_CHUNKS * TH)
    chunk = (lambda c: nch - 1 - c) if reverse else (lambda c: c)
    col_spec = lambda j: pl.BlockSpec((B, HG_CHUNKS * TH, KD), lambda c: (0, chunk(c), j))
    in_specs = [col_spec(0), col_spec(1 if reverse else 0), col_spec(1),
                pl.BlockSpec((1, KD), lambda c: (0, 0)),
                pl.BlockSpec((B, H, DV, DK), lambda c: (0, 0, 0, 0))]
    p3 = p.reshape(B, L, p.shape[-1])
    args = [p3, pz.reshape(B, L, 2 * KD), p3, lb_row, s0]
    readout = o_f is not None
    if readout:
        in_specs += [col_spec(0), col_spec(2), pl.BlockSpec((1, DV), lambda c: (0, 0))]
        args += [o_f, p3, norm_g]
    return pl.pallas_call(
        functools.partial(_hgrn_kernel, reverse, readout),
        grid=(nch,),
        in_specs=in_specs,
        out_specs=col_spec(0),
        out_shape=jax.ShapeDtypeStruct((B, L, KD), BF16 if readout else F32),
        scratch_shapes=[pltpu.VMEM((B, H, DV, DK), F32)],
        compiler_params=_cparams(1, VMEM_LIMIT),
        name="hgrn_bwd_readout" if readout else "hgrn_fwd",
    )(*args)


def _hy_pre_kernel(v_ref, x1_ref, x0_ref, vp_ref, x1p_ref, x0p_ref, vn_ref, x1n_ref, x0n_ref,
                   w_ref, b_ref, vx_ref, x0o_ref):
    i = pl.program_id(1)
    first = i == 0
    last = i == pl.num_programs(1) - 1
    row = lax.broadcasted_iota(I32, (TM_HY, 1), 0)

    def conv(c_ref, p_ref, n_ref, col):
        x = c_ref[...].astype(F32)
        prev_row = jnp.where(first, 0.0, p_ref[...].astype(F32)[HALO - 1:HALO, :])
        next_row = jnp.where(last, 0.0, n_ref[...].astype(F32)[0:1, :])
        xm = jnp.where(row == 0, prev_row, pltpu.roll(x, 1, axis=0))
        xp = jnp.where(row == TM_HY - 1, next_row, pltpu.roll(x, TM_HY - 1, axis=0))
        cs = slice(col * HYW, (col + 1) * HYW)
        return xm * w_ref[0:1, cs] + x * w_ref[1:2, cs] + xp * w_ref[2:3, cs] + b_ref[:, cs]

    v = conv(v_ref, vp_ref, vn_ref, 0)
    x1 = conv(x1_ref, x1p_ref, x1n_ref, 1)
    x0 = conv(x0_ref, x0p_ref, x0n_ref, 2)
    vx_ref[...] = v * x1
    x0o_ref[...] = x0.astype(x0o_ref.dtype)


def _hyena_pre(p, conv_w, conv_b):
    nt = L // TM_HY
    hb = TM_HY // HALO
    nhb = T // HALO
    cur = lambda col: pl.BlockSpec((TM_HY, HYW), lambda b, i: (b * nt + i, col))
    prv = lambda col: pl.BlockSpec((HALO, HYW), lambda b, i: (jnp.maximum((b * nt + i) * hb - 1, 0), col))
    nxt = lambda col: pl.BlockSpec((HALO, HYW), lambda b, i: (jnp.minimum((b * nt + i + 1) * hb, nhb - 1), col))
    c0 = 3
    out = jax.ShapeDtypeStruct((T, HYW), F32)
    return pl.pallas_call(
        _hy_pre_kernel,
        grid=(B, nt),
        in_specs=[cur(c0), cur(c0 + 1), cur(c0 + 2), prv(c0), prv(c0 + 1), prv(c0 + 2),
                  nxt(c0), nxt(c0 + 1), nxt(c0 + 2),
                  pl.BlockSpec((3, 3 * HYW), lambda b, i: (0, 0)),
                  pl.BlockSpec((1, 3 * HYW), lambda b, i: (0, 0))],
        out_specs=(pl.BlockSpec((TM_HY, HYW), lambda b, i: (b * nt + i, 0)),
                   pl.BlockSpec((TM_HY, HYW), lambda b, i: (b * nt + i, 0))),
        out_shape=(out, jax.ShapeDtypeStruct((T, HYW), BF16)),
        compiler_params=_cparams(2, VMEM_LIMIT),
        name="hyena_pre",
    )(p, p, p, p, p, p, p, p, p, conv_w, conv_b)


def _taps_dft_kernel(z_ref, w1_ref, b1_ref, f1_ref, w2_ref, b2_ref, f2_ref, w3a_ref, w3b_ref, dl_ref, fm_ref,
                     o_hbm, obuf, sem):
    g = pl.program_id(0)
    na = L // FFT_P
    half = z_ref.shape[0] // 2
    zt = z_ref[0:half, :]
    zb = z_ref[half:, :]
    lane = lax.broadcasted_iota(I32, zt.shape, 1)
    dot = _dot_split
    h = jnp.sin(f1_ref[...] * (dot(jnp.concatenate([zt, zb], axis=1), w1_ref) + b1_ref[...]))
    h = jnp.sin(f2_ref[...] * (dot(h, w2_ref) + b2_ref[...]))

    def copies(step):
        return [pltpu.make_async_copy(obuf.at[n, jj], o_hbm.at[n, :, step * FFT_BB + jj, :], sem)
                for n in range(2) for jj in range(FFT_BB)]

    @pl.when(g > 0)
    def _():
        for cp in copies(g - 1):
            cp.wait()

    for hi, (zz, w3_ref) in enumerate(((zt, w3a_ref), (zb, w3b_ref))):
        taps = dot(h, w3_ref)
        win = jnp.exp(-_lane_pick(zz, lane, 0) * dl_ref[...])
        fwd = (taps[:, :HYW] * win).astype(BF16)
        bwd = (taps[:, HYW:] * win * _lane_pick(zz, lane, HY_EMB)).astype(BF16)
        for bb in range(FFT_BB // 2):
            jj = hi * (FFT_BB // 2) + bb
            rows = slice(bb * na, (bb + 1) * na)
            obuf[0, jj] = jnp.dot(fm_ref[...], fwd[rows], preferred_element_type=F32)
            obuf[1, jj] = jnp.dot(fm_ref[...], bwd[rows], preferred_element_type=F32)

    for cp in copies(g):
        cp.start()

    @pl.when(g == pl.num_programs(0) - 1)
    def _():
        for cp in copies(g):
            cp.wait()


def _filter_taps_dft(zin, w1, b1, f1, w2, b2, f2, w3a, w3b, deltas, fmat):
    na = L // FFT_P
    tm = FFT_BB * na
    ln = V7X_LANES
    full = lambda shape: pl.BlockSpec(shape, lambda i: tuple(0 for _ in shape))
    return pl.pallas_call(
        _taps_dft_kernel,
        grid=(FFT_P // FFT_BB,),
        in_specs=[pl.BlockSpec((tm, ln), lambda i: (i, 0)),
                  full((2, 2 * ln, ln)), full((1, ln)), full((1, ln)),
                  full((2, ln, ln)), full((1, ln)), full((1, ln)),
                  full((2, ln, 2 * HYW)), full((2, ln, 2 * HYW)), full((1, HYW)), full((FFT_P, na))],
        out_specs=pl.BlockSpec(memory_space=pl.ANY),
        out_shape=jax.ShapeDtypeStruct((2, FFT_P, FFT_P, HYW), F32),
        scratch_shapes=[pltpu.VMEM((2, FFT_BB, FFT_P, HYW), F32), pltpu.SemaphoreType.DMA(())],
        compiler_params=_cparams(1, VMEM_LIMIT),
        name="filter_taps_dft",
    )(zin, w1, b1, f1, w2, b2, f2, w3a, w3b, deltas, fmat)


def _strided_dft_kernel(x_hbm, f_ref, o_hbm, xbuf, obuf, sem_in, sem_out):
    g = pl.program_id(0)
    ng = pl.num_programs(0)
    nb = FFT_P // FFT_BB

    def copies(grp, slot, inbound):
        n = grp // nb
        b0 = (grp % nb) * FFT_BB
        if inbound:
            return [pltpu.make_async_copy(x_hbm.at[n, :, b0 + jj, :], xbuf.at[slot, jj], sem_in.at[slot])
                    for jj in range(FFT_BB)]
        return [pltpu.make_async_copy(obuf.at[slot, jj], o_hbm.at[n, :, b0 + jj, :], sem_out.at[slot])
                for jj in range(FFT_BB)]

    def start(grp, slot, inbound):
        for cp in copies(grp, slot, inbound):
            cp.start()

    def wait(grp, slot, inbound):
        for cp in copies(grp, slot, inbound):
            cp.wait()

    @pl.when(g == 0)
    def _():
        start(0, 0, True)

    for slot in range(2):
        grp = 2 * g + slot
        if slot == 0:
            start(grp + 1, 1, True)
        else:
            @pl.when(g + 1 < ng)
            def _():
                start(grp + 1, 0, True)
        wait(grp, slot, True)

        @pl.when(g > 0)
        def _():
            wait(grp - 2, slot, False)

        for jj in range(FFT_BB):
            obuf[slot, jj] = jnp.dot(f_ref[...], xbuf[slot, jj].astype(BF16), preferred_element_type=F32)
        start(grp, slot, False)

    @pl.when(g + 1 == ng)
    def _():
        wait(2 * g, 0, False)
        wait(2 * g + 1, 1, False)


def _strided_dft(xv, fmat, name):
    n, kk = xv.shape[0], xv.shape[1]
    mm = fmat.shape[0]
    groups = n * (FFT_P // FFT_BB)
    return pl.pallas_call(
        _strided_dft_kernel,
        grid=(groups // 2,),
        in_specs=[pl.BlockSpec(memory_space=pl.ANY),
                  pl.BlockSpec((mm, kk), lambda g: (0, 0))],
        out_specs=pl.BlockSpec(memory_space=pl.ANY),
        out_shape=jax.ShapeDtypeStruct((n, mm, FFT_P, HYW), F32),
        scratch_shapes=[pltpu.VMEM((2, FFT_BB, kk, HYW), F32), pltpu.VMEM((2, FFT_BB, mm, HYW), F32),
                        pltpu.SemaphoreType.DMA((2,)), pltpu.SemaphoreType.DMA((2,))],
        compiler_params=_cparams(1, VMEM_LIMIT),
        name=name,
    )(xv, fmat)


def _cblock(mr, mi):
    return jnp.concatenate([jnp.concatenate([mr, -mi], axis=1), jnp.concatenate([mi, mr], axis=1)], axis=0)


FFT_H = FFT_P // 2
KSPEC_DD = 4
MID_DD = 2


def _twiddled(fr_ref, fi_ref, tw):
    twr = tw[0:1, :]
    twi = tw[1:2, :]
    fr = fr_ref[...]
    fi = fi_ref[...]
    return fr * twr - fi * twi, fr * twi + fi * twr


def _cmul_rows(x, kr, ki):
    xr, xi = x[:FFT_P], x[FFT_P:]
    return jnp.concatenate([xr * kr - xi * ki, xr * ki + xi * kr], axis=0).astype(BF16)


_TN_DIMS = (((0,), (0,)), ((), ()))

def _kspec_kernel(a_ref, fr_ref, fi_ref, tw_ref, twh_ref, o_ref, oh_ref):
    g = pl.program_id(0)
    dot = functools.partial(jnp.dot, preferred_element_type=F32)

    def combine(xf, xb, store):
        store(0, xf[:FFT_P] + xb[:FFT_P])
        store(1, xf[FFT_P:] - xb[FFT_P:])

    def general(u):
        def store(ri, val):
            o_ref[ri, u] = val
        rm = _cblock(*_twiddled(fr_ref, fi_ref, tw_ref[u])).astype(BF16)
        combine(dot(rm, a_ref[0, u].reshape(2 * FFT_P, HYW).astype(BF16)),
                dot(rm, a_ref[1, u].reshape(2 * FFT_P, HYW).astype(BF16)), store)

    @pl.when(g > 0)
    def _():
        general(0)

    @pl.when(g == 0)
    def _():
        def store_first(ri, val):
            o_ref[ri, 0] = val

        def store_half(ri, val):
            oh_ref[ri] = val
        for slot, tw, store in ((0, tw_ref[0], store_first), (1, twh_ref[...], store_half)):
            w = jnp.concatenate(_twiddled(fr_ref, fi_ref, tw), axis=0).astype(BF16)
            combine(dot(w, a_ref[0, 0, slot].astype(BF16)), dot(w, a_ref[1, 0, slot].astype(BF16)), store)

    for u in range(1, KSPEC_DD):
        general(u)


def _kernel_spectrum(ak, fr, fi, tw, twh):
    same = lambda shape: pl.BlockSpec(shape, lambda g: tuple(0 for _ in shape))
    return pl.pallas_call(
        _kspec_kernel,
        grid=(FFT_H // KSPEC_DD,),
        in_specs=[pl.BlockSpec((2, KSPEC_DD, 2, FFT_P, HYW), lambda g: (0, g, 0, 0, 0)),
                  same((FFT_P, FFT_P)), same((FFT_P, FFT_P)),
                  pl.BlockSpec((KSPEC_DD, 2, FFT_P), lambda g: (g, 0, 0)), same((2, FFT_P))],
        out_specs=(pl.BlockSpec((2, KSPEC_DD, FFT_P, HYW), lambda g: (0, g, 0, 0)),
                   pl.BlockSpec((2, FFT_P, HYW), lambda g: (0, 0, 0))),
        out_shape=(jax.ShapeDtypeStruct((2, FFT_H, FFT_P, HYW), F32),
                   jax.ShapeDtypeStruct((2, FFT_P, HYW), F32)),
        compiler_params=_cparams(1, VMEM_LIMIT),
        name="kernel_spectrum",
    )(ak, fr, fi, tw, twh)


def _mid_kernel(a_ref, kh_ref, khh_ref, fr_ref, fi_ref, tw_ref, twh_ref, o_ref):
    g = pl.program_id(0)
    dot = functools.partial(jnp.dot, preferred_element_type=F32)
    dot_t = lambda w, y: lax.dot_general(w, y, _TN_DIMS, preferred_element_type=F32)

    def general(u):
        rm = _cblock(*_twiddled(fr_ref, fi_ref, tw_ref[u])).astype(BF16)
        for n in range(B):
            x = dot(rm, a_ref[n, u].reshape(2 * FFT_P, HYW).astype(BF16))
            y = _cmul_rows(x, kh_ref[0, u], kh_ref[1, u])
            o_ref[n, u] = dot_t(rm, y).reshape(2, FFT_P, HYW)

    @pl.when(g > 0)
    def _():
        general(0)

    @pl.when(g == 0)
    def _():
        for slot, tw, kr, ki in ((0, tw_ref[0], kh_ref[0, 0], kh_ref[1, 0]), (1, twh_ref[...], khh_ref[0], khh_ref[1])):
            w = jnp.concatenate(_twiddled(fr_ref, fi_ref, tw), axis=0).astype(BF16)
            for n in range(B):
                y = _cmul_rows(dot(w, a_ref[n, 0, slot].astype(BF16)), kr, ki)
                o_ref[n, 0, slot] = dot_t(w, y)

    for u in range(1, MID_DD):
        general(u)


def _fft_mid(au, kh, khh, fr, fi, tw, twh):
    pair = pl.BlockSpec((B, MID_DD, 2, FFT_P, HYW), lambda g: (0, g, 0, 0, 0))
    same = lambda shape: pl.BlockSpec(shape, lambda g: tuple(0 for _ in shape))
    return pl.pallas_call(
        _mid_kernel,
        grid=(FFT_H // MID_DD,),
        in_specs=[pair,
                  pl.BlockSpec((2, MID_DD, FFT_P, HYW), lambda g: (0, g, 0, 0)),
                  same((2, FFT_P, HYW)), same((FFT_P, FFT_P)), same((FFT_P, FFT_P)),
                  pl.BlockSpec((MID_DD, 2, FFT_P), lambda g: (g, 0, 0)), same((2, FFT_P))],
        out_specs=pair,
        out_shape=jax.ShapeDtypeStruct((B, FFT_H, 2, FFT_P, HYW), F32),
        compiler_params=_cparams(1, VMEM_LIMIT),
        name="fft_mid",
    )(au, kh, khh, fr, fi, tw, twh)


def _dft_tables():
    na = L // FFT_P
    a = np.arange(na)
    dd = np.arange(FFT_H)
    ang = 2.0 * np.pi * np.outer(dd, a) / FFT_P
    re_rows = np.cos(ang)
    im_rows = -np.sin(ang)
    im_rows[0] = np.cos(np.pi * a)
    f_first = np.stack([re_rows, im_rows], axis=1).reshape(FFT_P, na)
    gre = 2.0 * np.cos(ang)
    gim = -2.0 * np.sin(ang)
    gre[0] = 1.0
    gim[0] = np.cos(np.pi * a)
    g_last = np.stack([gre, gim], axis=1).reshape(FFT_P, na).T / FFT_N
    b = np.arange(FFT_P)
    angf = 2.0 * np.pi * np.outer(b, b) / FFT_P
    ang2 = 2.0 * np.pi * np.outer(np.arange(FFT_H + 1), b) / FFT_N
    tw = np.stack([np.cos(ang2), -np.sin(ang2)], axis=1)
    f32 = lambda x: jnp.asarray(x.astype(np.float32))
    return (f32(f_first).astype(BF16), f32(g_last).astype(BF16), f32(np.cos(angf)), f32(-np.sin(angf)),
            f32(tw[:FFT_H]), f32(tw[FFT_H]))


def _merge_kernel(ya_ref, cv_ref, vx_ref, x0_ref, ga_ref, gb_ref, x_ref, er_ref, ec_ref, gt1_ref, hyd_ref,
                  wpa_ref, wpb_ref, wo_ref, g2_ref, sh2_ref, sc2_ref, wr_ref, br_ref,
                  x1_ref, t2_ref, lg_ref):
    vx = vx_ref[...]
    yb = x0_ref[...].astype(F32) * (cv_ref[...] + vx * hyd_ref[...])
    pa = jnp.dot(ya_ref[...].astype(BF16), wpa_ref[...], preferred_element_type=F32)
    pb = jnp.dot(yb.astype(BF16), wpb_ref[...], preferred_element_type=F32)
    mixed = (jax.nn.sigmoid(ga_ref[...].astype(F32)) * pa
             + jax.nn.sigmoid(gb_ref[...].astype(F32)) * pb)
    xm = jnp.dot(mixed.astype(BF16), wo_ref[...], preferred_element_type=F32)
    nrow = TM_MG // GRID_W
    row0 = (pl.program_id(0) % (L // TM_MG)) * nrow
    x1 = x_ref[...] + _pos_tile(er_ref, ec_ref, row0, nrow) + gt1_ref[...] * xm
    x1_ref[...] = x1
    t2 = _rms(x1) * g2_ref[...]
    t2 = t2 * (1.0 + sc2_ref[...]) + sh2_ref[...]
    t2_ref[...] = t2
    t_hi = t2.astype(BF16)
    t_lo = (t2 - t_hi.astype(F32)).astype(BF16)
    rr = (jnp.dot(t_hi, wr_ref[...], preferred_element_type=F32)
          + jnp.dot(t_lo, wr_ref[...], preferred_element_type=F32))
    lg_ref[...] = rr[:, :V7X_LANES] + rr[:, V7X_LANES:] + br_ref[...]


def _merge(ya, cv, vx, x0c, p, x2, er, ec, gt1, hyd, wpa, wpb, wo, g2, sh2, sc2, wr, br):
    tpb = L // TM_MG
    half = lambda: pl.BlockSpec((TM_MG, HYW), lambda i: (i, 0))
    full = lambda shape: pl.BlockSpec(shape, lambda i: tuple(0 for _ in shape))
    perb = lambda: pl.BlockSpec((None, 1, D), lambda i: (i // tpb, 0, 0))
    return pl.pallas_call(
        _merge_kernel,
        grid=(T // TM_MG,),
        in_specs=[half(), half(), half(), half(),
                  pl.BlockSpec((TM_MG, D), lambda i: (i, 3)),
                  pl.BlockSpec((TM_MG, D), lambda i: (i, 4)),
                  pl.BlockSpec((TM_MG, D), lambda i: (i, 0)),
                  full((L // GRID_W, D // 2)), full((GRID_W, D // 2)),
                  perb(), full((1, HYW)),
                  full((KD, D)), full((HYW, D)), full((D, D)),
                  full((1, D)), perb(), perb(),
                  full((D, 2 * V7X_LANES)), full((1, V7X_LANES))],
        out_specs=(pl.BlockSpec((TM_MG, D), lambda i: (i, 0)),
                   pl.BlockSpec((TM_MG, D), lambda i: (i, 0)),
                   pl.BlockSpec((TM_MG, V7X_LANES), lambda i: (i, 0))),
        out_shape=(jax.ShapeDtypeStruct((T, D), F32), jax.ShapeDtypeStruct((T, D), F32),
                   jax.ShapeDtypeStruct((T, V7X_LANES), F32)),
        compiler_params=_cparams(1, VMEM_LIMIT),
        name="merge",
    )(ya, cv, vx, x0c, p, p, x2, er, ec, gt1, hyd, wpa, wpb, wo, g2, sh2, sc2, wr, br)


def _route_kernel(lg_ref, info_ref, cnt_ref):
    @pl.when(pl.program_id(0) == 0)
    def _():
        cnt_ref[...] = jnp.zeros_like(cnt_ref)

    lg = lg_ref[...]
    lane = lax.broadcasted_iota(I32, lg.shape, 1)
    lanef = lane.astype(F32)
    neg = -1e30
    big = 1e9
    is_g = (lane >= NEXP) & (lane < NEXP + NGRP)
    gl = jnp.where(is_g, lg, neg)
    ge = jnp.where(is_g, jnp.exp(gl - jnp.max(gl, axis=-1, keepdims=True)), 0.0)
    pg = ge / jnp.sum(ge, axis=-1, keepdims=True)
    p_top_g = jnp.max(pg, axis=-1, keepdims=True)
    gidx = jnp.min(jnp.where(is_g & (pg == p_top_g), lanef, big), axis=-1, keepdims=True)
    g_sel = gidx.astype(I32) - NEXP
    emask = (lane < NEXP) & (jnp.right_shift(lane, NEPG.bit_length() - 1) == g_sel)
    el = jnp.where(emask, lg, neg)
    ee = jnp.where(emask, jnp.exp(el - jnp.max(el, axis=-1, keepdims=True)), 0.0)
    pe = ee / jnp.sum(ee, axis=-1, keepdims=True)
    p1 = jnp.max(jnp.where(emask, pe, -1.0), axis=-1, keepdims=True)
    i1 = jnp.min(jnp.where(emask & (pe == p1), lanef, big), axis=-1, keepdims=True)
    rest = emask & (lanef != i1)
    p2 = jnp.max(jnp.where(rest, pe, -1.0), axis=-1, keepdims=True)
    i2 = jnp.min(jnp.where(rest & (pe == p2), lanef, big), axis=-1, keepdims=True)
    wsum = p1 + p2
    w1 = p_top_g * p1 / wsum
    w2 = p_top_g * p2 / wsum
    sel1 = lanef == i1
    sel2 = lanef == i2
    oh = jnp.where(sel1 | sel2, 1.0, 0.0)
    r = lax.broadcasted_iota(I32, (TR, TR), 0)
    c = lax.broadcasted_iota(I32, (TR, TR), 1)
    stril = jnp.where(c < r, 1.0, 0.0).astype(BF16)
    before = jnp.dot(stril, oh.astype(BF16), preferred_element_type=F32) + cnt_ref[...]
    r1 = jnp.sum(jnp.where(sel1, before, 0.0), axis=-1, keepdims=True)
    r2 = jnp.sum(jnp.where(sel2, before, 0.0), axis=-1, keepdims=True)
    cnt_ref[...] += jnp.sum(oh, axis=0, keepdims=True)
    info = jnp.where(lane == 0, i1, jnp.where(lane == 1, r1, jnp.where(lane == 2, i2, jnp.where(
        lane == 3, r2, jnp.where(lane == 4, w1, jnp.where(lane == 5, w2, 0.0))))))
    info_ref[...] = info


def _route(lg):
    return pl.pallas_call(
        _route_kernel,
        grid=(T // TR,),
        in_specs=[pl.BlockSpec((TR, V7X_LANES), lambda i: (i, 0))],
        out_specs=(pl.BlockSpec((TR, V7X_LANES), lambda i: (i, 0)),
                   pl.BlockSpec((1, V7X_LANES), lambda i: (0, 0))),
        out_shape=(jax.ShapeDtypeStruct((T, V7X_LANES), F32), jax.ShapeDtypeStruct((1, V7X_LANES), F32)),
        compiler_params=_cparams(1, VMEM_LIMIT),
        name="route",
    )(lg)


def _positions_kernel(info_ref, st_ref, o_ref):
    info = info_ref[...]
    lane = lax.broadcasted_iota(I32, info.shape, 1)
    lanef = lane.astype(F32)
    st = st_ref[...]
    row = lambda e_lane, r_lane: (jnp.sum(jnp.where(lanef == _lane_pick(info, lane, e_lane), st, 0.0),
                                          axis=-1, keepdims=True) + _lane_pick(info, lane, r_lane))
    o_ref[...] = jnp.where(lane == 0, row(0, 1), jnp.where(lane == 1, row(2, 3), 0.0)).astype(I32)


def _positions(info, starts_row):
    return pl.pallas_call(
        _positions_kernel,
        grid=(T // TPOS,),
        in_specs=[pl.BlockSpec((TPOS, V7X_LANES), lambda i: (i, 0)),
                  pl.BlockSpec((1, V7X_LANES), lambda i: (0, 0))],
        out_specs=pl.BlockSpec((TPOS, V7X_LANES), lambda i: (i, 0)),
        out_shape=jax.ShapeDtypeStruct((T, V7X_LANES), I32),
        compiler_params=_cparams(1, VMEM_LIMIT),
        name="positions",
    )(info, starts_row)


def _scatter_kernel(pos_ref, zt_ref, t2_hbm, zeros_hbm, xs_hbm, tbuf, fsem, sem, zsem):
    i = pl.program_id(0)

    def zcopy(row):
        start = pl.multiple_of(jnp.maximum(row, 0), TE)
        return pltpu.make_async_copy(zeros_hbm, xs_hbm.at[pl.ds(start, TE)], zsem)

    @pl.when(i == 0)
    def _():
        def ztail(start, e, carry):
            @pl.when(zt_ref[0, e] >= 0)
            def _():
                cp = zcopy(zt_ref[0, e])
                cp.start() if start else cp.wait()
            return carry

        lax.fori_loop(0, NEXP, functools.partial(ztail, True), 0)
        lax.fori_loop(0, NEXP, functools.partial(ztail, False), 0)

        def zrest(start, tile, carry):
            cp = zcopy(tile * TE)
            cp.start() if start else cp.wait()
            return carry

        lax.fori_loop(zt_ref[0, NA_OFF], NT_EXP, functools.partial(zrest, True), 0)
        lax.fori_loop(zt_ref[0, NA_OFF], NT_EXP, functools.partial(zrest, False), 0)

    n = pl.num_programs(0)
    slot = i % SCATTER_SLOTS

    def fetch(tile, s):
        return pltpu.make_async_copy(t2_hbm.at[pl.ds(pl.multiple_of(tile * TS, TS), TS)], tbuf.at[s], fsem.at[s])

    def wait_rows(s):
        _wait_rows(TS, pltpu.make_async_copy(tbuf.at[s].at[pl.ds(0, 1)], xs_hbm.at[pl.ds(0, 1)], sem.at[s]))

    @pl.when(i == 0)
    def _():
        fetch(0, 0).start()

    @pl.when(i + 1 < n)
    def _():
        fetch(i + 1, (i + 1) % SCATTER_SLOTS).start()

    fetch(i, slot).wait()

    def row_copy(base, jj, kk):
        dst = pos_ref[0, 2 * (base + jj) + kk]
        return pltpu.make_async_copy(tbuf.at[slot].at[pl.ds(base, ROW_GROUP)].at[pl.ds(jj, 1)],
                                     xs_hbm.at[pl.ds(dst, 1)], sem.at[slot])

    _start_rows(TS, row_copy)

    @pl.when(i > 0)
    def _():
        wait_rows((i + SCATTER_SLOTS - 1) % SCATTER_SLOTS)

    @pl.when(i + 1 == n)
    def _():
        wait_rows(slot)


def _start_rows(n_rows, row_copy):
    def group(g, carry):
        base = pl.multiple_of(g * ROW_GROUP, ROW_GROUP)
        for jj in range(ROW_GROUP):
            for kk in range(2):
                row_copy(base, jj, kk).start(priority=kk)
        return carry

    lax.fori_loop(0, n_rows // ROW_GROUP, group, 0)


def _wait_rows(n_rows, one_row_copy):
    def drain(j, carry):
        one_row_copy.wait()
        one_row_copy.wait()
        return carry

    lax.fori_loop(0, n_rows, drain, 0, unroll=ROW_UNROLL)


def _scatter_rows(pos3, meta, t2, zeros_tile):
    return pl.pallas_call(
        _scatter_kernel,
        grid=(T // TS,),
        in_specs=[pl.BlockSpec((None, 1, 2 * TS), lambda i: (i, 0, 0), memory_space=pltpu.SMEM),
                  pl.BlockSpec(memory_space=pltpu.SMEM),
                  pl.BlockSpec(memory_space=pl.ANY),
                  pl.BlockSpec(memory_space=pl.ANY)],
        out_specs=pl.BlockSpec(memory_space=pl.ANY),
        out_shape=jax.ShapeDtypeStruct((NP_ROWS, D), F32),
        scratch_shapes=[pltpu.VMEM((SCATTER_SLOTS, TS, D), F32), pltpu.SemaphoreType.DMA((SCATTER_SLOTS,)),
                        pltpu.SemaphoreType.DMA((SCATTER_SLOTS,)), pltpu.SemaphoreType.DMA(())],
        compiler_params=_cparams(1, VMEM_LIMIT),
        name="scatter_rows",
    )(pos3, meta, t2, zeros_tile)


def _expert_kernel(te_ref, nx_ref, sl_ref, na_ref, xs_ref, wg_hbm, wu_hbm, wd_hbm, ys_ref,
                   wgs_ref, wus_ref, wds_ref, wgb_ref, wub_ref, wdb_ref, sem):
    i = pl.program_id(0)
    active = i < na_ref[0]
    first = active & ((i == 0) | (te_ref[i] != te_ref[jnp.maximum(i - 1, 0)]))

    def fetch(e, slot):
        return [pltpu.make_async_copy(wg_hbm.at[e], wgs_ref.at[slot], sem.at[slot]),
                pltpu.make_async_copy(wu_hbm.at[e], wus_ref.at[slot], sem.at[slot]),
                pltpu.make_async_copy(wd_hbm.at[e], wds_ref.at[slot], sem.at[slot])]

    for slot in range(2):
        @pl.when(first & (sl_ref[i] == slot))
        def _():
            @pl.when(i == 0)
            def _():
                for cp in fetch(te_ref[i], slot):
                    cp.start()

            for cp in fetch(te_ref[i], slot):
                cp.wait()
            wgb_ref[...] = wgs_ref[slot].astype(BF16)
            wub_ref[...] = wus_ref[slot].astype(BF16)
            wdb_ref[...] = wds_ref[slot].astype(BF16)

            @pl.when(nx_ref[i] >= 0)
            def _():
                for cp in fetch(nx_ref[i], 1 - slot):
                    cp.start()

    @pl.when(active)
    def _():
        x = xs_ref[...].astype(BF16)
        g = jnp.dot(x, wgb_ref[...], preferred_element_type=F32)
        u = jnp.dot(x, wub_ref[...], preferred_element_type=F32)
        hid = (g * jax.nn.sigmoid(g) * u).astype(BF16)
        ys_ref[...] = jnp.dot(hid, wdb_ref[...], preferred_element_type=F32)

    @pl.when(jnp.logical_not(active))
    def _():
        ys_ref[...] = jnp.zeros_like(ys_ref)


def _experts(tile_expert, tile_next, tile_slot, n_active, xs, wg, wu, wd):
    rows = lambda i, *_: (i, 0)
    rows_in = lambda i, te, nx, sl, na: (jnp.minimum(i, na[0] - 1), 0)
    any_space = pl.BlockSpec(memory_space=pl.ANY)
    grid_spec = pltpu.PrefetchScalarGridSpec(
        num_scalar_prefetch=4,
        grid=(NT_EXP,),
        in_specs=[pl.BlockSpec((TE, D), rows_in), any_space, any_space, any_space],
        out_specs=pl.BlockSpec((TE, D), rows),
        scratch_shapes=[pltpu.VMEM((2, D, DEXP), F32), pltpu.VMEM((2, D, DEXP), F32), pltpu.VMEM((2, DEXP, D), F32),
                        pltpu.VMEM((D, DEXP), BF16), pltpu.VMEM((D, DEXP), BF16), pltpu.VMEM((DEXP, D), BF16),
                        pltpu.SemaphoreType.DMA((2,))],
    )
    return pl.pallas_call(
        _expert_kernel,
        grid_spec=grid_spec,
        out_shape=jax.ShapeDtypeStruct((NP_ROWS, D), F32),
        compiler_params=_cparams(1, VMEM_LIMIT),
        name="experts",
    )(tile_expert, tile_next, tile_slot, n_active, xs, wg, wu, wd)


def _combine_kernel(pos_ref, posn_ref, info_ref, x1_ref, gt2_ref, fg_ref, ys_hbm, o_ref, buf, sem):
    g = pl.program_id(0)
    ng = pl.num_programs(0)

    def gather(p_ref, half, slot):
        def row_copy(base, jj, kk):
            src = p_ref[0, 2 * (half * TC + base + jj) + kk]
            return pltpu.make_async_copy(ys_hbm.at[pl.ds(src, 1)],
                                         buf.at[slot, kk].at[pl.ds(base, ROW_GROUP)].at[pl.ds(jj, 1)],
                                         sem.at[slot])
        _start_rows(TC, row_copy)

    def finish(half, slot):
        _wait_rows(TC, pltpu.make_async_copy(ys_hbm.at[pl.ds(0, 1)], buf.at[slot, 0].at[pl.ds(0, 1)],
                                             sem.at[slot]))
        rows = slice(half * TC, (half + 1) * TC)
        info = info_ref[rows, :]
        lane = lax.broadcasted_iota(I32, info.shape, 1)
        moe = _lane_pick(info, lane, 4) * buf[slot, 0] + _lane_pick(info, lane, 5) * buf[slot, 1]
        x2 = x1_ref[rows, :] + gt2_ref[...] * moe
        o_ref[rows, :] = _rms(x2) * fg_ref[...]

    @pl.when(g == 0)
    def _():
        gather(pos_ref, 0, 0)

    gather(pos_ref, 1, 1)
    finish(0, 0)

    @pl.when(g + 1 < ng)
    def _():
        gather(posn_ref, 0, 0)

    finish(1, 1)


def _combine(pos3, info, x1, gt2, fg, ys):
    step = 2 * TC
    tpb = L // step
    nsteps = T // step
    return pl.pallas_call(
        _combine_kernel,
        grid=(nsteps,),
        in_specs=[pl.BlockSpec((None, 1, 2 * step), lambda i: (i, 0, 0), memory_space=pltpu.SMEM),
                  pl.BlockSpec((None, 1, 2 * step), lambda i: (jnp.minimum(i + 1, nsteps - 1), 0, 0),
                               memory_space=pltpu.SMEM),
                  pl.BlockSpec((step, V7X_LANES), lambda i: (i, 0)),
                  pl.BlockSpec((step, D), lambda i: (i, 0)),
                  pl.BlockSpec((None, 1, D), lambda i: (i // tpb, 0, 0)),
                  pl.BlockSpec((1, D), lambda i: (0, 0)),
                  pl.BlockSpec(memory_space=pl.ANY)],
        out_specs=pl.BlockSpec((step, D), lambda i: (i, 0)),
        out_shape=jax.ShapeDtypeStruct((T, D), F32),
        scratch_shapes=[pltpu.VMEM((2, 2, TC, D), F32), pltpu.SemaphoreType.DMA((2,))],
        compiler_params=_cparams(1, VMEM_LIMIT),
        name="combine",
    )(pos3, pos3, info, x1, gt2, fg, ys)


def _pos_tables():
    rows = L // GRID_W
    quarter = D // 4
    omega = 1.0 / (10000.0 ** (jnp.arange(quarter, dtype=F32) / quarter))

    def axis_emb(pos):
        a = pos[:, None] * omega[None, :]
        return jnp.concatenate([jnp.sin(a), jnp.cos(a)], axis=-1)

    er = axis_emb(jnp.arange(rows, dtype=F32))
    ec = axis_emb(jnp.arange(GRID_W, dtype=F32))
    return er, ec


def _filter_features():
    z = np.zeros((L, V7X_LANES), np.float64)
    bands = (HY_EMB - 1) // 2
    ang = (2.0 * np.pi * np.arange(L) / L)[:, None] * np.linspace(1e-4, bands - 1, bands)[None, :]
    z[:, 0] = np.linspace(0.0, 1.0, L)
    z[:, 1:1 + bands] = np.cos(ang)
    z[:, 1 + bands:HY_EMB] = -np.sin(ang)
    z[1:, HY_EMB] = 1.0
    z = z.reshape(L // FFT_P, FFT_P, V7X_LANES).transpose(1, 0, 2).reshape(L, V7X_LANES)
    return jnp.asarray(z.astype(np.float32))


def _pad2(a, rows, cols):
    return jnp.pad(a, ((0, rows - a.shape[0]), (0, cols - a.shape[1])))


def kernel(x, c, ctx, c_ctx, ada_w, ada_b, norm1_g, norm2_g, w_in, hgrn_lb, hgrn_norm_g, hy_conv_w, hy_conv_b, hy_filt_w1, hy_filt_b1, hy_filt_freq1, hy_filt_w2, hy_filt_b2, hy_filt_freq2, hy_filt_w3, hy_d, w_proj_a, w_proj_b, w_out, moe_router_g_w, moe_router_g_b, moe_router_e_w, moe_router_e_b, moe_w_gate, moe_w_up, moe_w_down, final_norm_g):
    cvec = jnp.zeros((8, D), F32).at[0:B].set(c).at[B].set(c_ctx)
    mod = _adaln(cvec, ada_w[0], ada_b[0][None, :])
    m6 = mod.reshape(8, 6, D)
    sh1, sc1, gt1, sh2, sc2, gt2 = [m6[0:B, k][:, None, :] for k in range(6)]
    csh1, csc1 = m6[B:B + 1, 0], m6[B:B + 1, 1]

    lbs = jnp.cumsum(jax.nn.softmax(hgrn_lb.astype(F32), axis=0), axis=0)[0]
    g1 = norm1_g[0][None, :]
    er, ec = _pos_tables()
    x2 = x.reshape(T, D)

    w_ctx = w_in[0][:, KD:4 * KD].astype(BF16)
    s_f, s_b = _context_states(ctx, g1, csh1, csc1, w_ctx, lbs)

    pz, p = _in_projection(x2, er, ec, g1, sh1, sc1, w_in[0].astype(BF16))

    o_f = _hgrn_scan(pz, p, lbs[0:1], s_f, False)
    y_a = _hgrn_scan(pz, p, lbs[1:2], s_b, True, o_f=o_f, norm_g=hgrn_norm_g[0][None, :])

    vx, x0c = _hyena_pre(p, hy_conv_w[0], hy_conv_b[0][None, :])
    deltas = jnp.abs(jnp.linspace(math.log(HY_DECAY_TARGET) / HY_SLOW_PCT,
                                  math.log(HY_DECAY_TARGET) / HY_FAST_PCT, HYW, dtype=F32))[None, :]
    ln = V7X_LANES
    fh = hy_filt_w2.shape[-1]
    blockdiag = lambda m: jnp.concatenate([_pad2(m, m.shape[0], 2 * m.shape[1]),
                                           jnp.pad(m, ((0, 0), (m.shape[1], 0)))], axis=0)
    twice = lambda v: jnp.concatenate([v, v])[None, :]
    w3 = hy_filt_w3[0]
    f_first, g_last, fr, fi, tw, twh = _dft_tables()
    na = L // FFT_P
    ak = _filter_taps_dft(
        _filter_features(),
        _hi_lo(blockdiag(_pad2(hy_filt_w1[0], ln, fh))), twice(hy_filt_b1[0]), twice(hy_filt_freq1[0]),
        _hi_lo(blockdiag(hy_filt_w2[0])), twice(hy_filt_b2[0]), twice(hy_filt_freq2[0]),
        _hi_lo(_pad2(w3, ln, 2 * HYW)), _hi_lo(jnp.pad(w3, ((fh, 0), (0, 0)))), deltas, f_first)
    kh, khh = _kernel_spectrum(ak.reshape(2, FFT_H, 2, FFT_P, HYW), fr, fi, tw, twh)
    au = _strided_dft(vx.reshape(B, na, FFT_P, HYW), f_first, "dft_first")
    bp = _fft_mid(au.reshape(B, FFT_H, 2, FFT_P, HYW), kh, khh, fr, fi, tw, twh)
    conv = _strided_dft(bp.reshape(B, FFT_P, FFT_P, HYW), g_last, "dft_last").reshape(T, HYW)

    wr = jnp.concatenate([jnp.transpose(moe_router_e_w[0], (1, 0, 2)).reshape(D, NEXP),
                          moe_router_g_w[0], jnp.zeros((D, V7X_LANES - NEXP - NGRP), F32)], axis=1)
    wr_hi = wr.astype(BF16)
    wr = jnp.concatenate([wr_hi, (wr - wr_hi.astype(F32)).astype(BF16)], axis=1)
    br = jnp.concatenate([moe_router_e_b[0].reshape(NEXP), moe_router_g_b[0],
                          jnp.zeros((V7X_LANES - NEXP - NGRP,), F32)])[None, :]
    x1, t2, lg = _merge(y_a.reshape(T, KD), conv, vx, x0c, p, x2, er, ec, gt1, hy_d[0][None, :],
                        w_proj_a[0].astype(BF16), w_proj_b[0].astype(BF16), w_out[0].astype(BF16),
                        norm2_g[0][None, :], sh2, sc2, wr, br)

    info, counts = _route(lg)
    cnt = counts[0, :NEXP].astype(I32)
    pc = ((cnt + TE - 1) // TE) * TE
    ends = jnp.cumsum(pc)
    starts = ends - pc
    n_active = (ends[-1] // TE).astype(I32)[None]
    tile_rows = jnp.arange(NT_EXP, dtype=I32) * TE
    tile_expert = jnp.minimum(jnp.sum((ends[None, :] <= tile_rows[:, None]).astype(I32), axis=1), NEXP - 1)
    meta = jnp.concatenate([jnp.where(pc > 0, ends - TE, -1), starts, n_active]).astype(I32)[None, :]
    starts_row = jnp.pad(starts.astype(F32), (0, V7X_LANES - NEXP))[None, :]
    pos3 = _positions(info, starts_row)[:, :2].reshape(T // TS, 1, 2 * TS)

    xs = _scatter_rows(pos3, meta, t2, jnp.zeros((TE, D), F32))
    eid = jnp.arange(NEXP, dtype=I32)
    nonempty = pc > 0
    later = jnp.where(nonempty[None, :] & (eid[None, :] > eid[:, None]), eid[None, :], NEXP)
    next_e = jnp.min(later, axis=1)
    next_e = jnp.where(next_e < NEXP, next_e, -1).astype(I32)
    slot_e = ((jnp.cumsum(nonempty.astype(I32)) - 1) % 2).astype(I32)
    ys = _experts(tile_expert, next_e[tile_expert], slot_e[tile_expert], n_active, xs,
                  moe_w_gate[0].reshape(NEXP, D, DEXP), moe_w_up[0].reshape(NEXP, D, DEXP),
                  moe_w_down[0].reshape(NEXP, DEXP, D))
    out = _combine(pos3, info, x1, gt2, final_norm_g[None, :], ys)
    return out.reshape(B, L, D)
```

```python
import functools
import math

import numpy as np
import jax
import jax.numpy as jnp
from jax import lax
from jax.experimental import pallas as pl
from jax.experimental.pallas import tpu as pltpu

F32 = jnp.float32
BF16 = jnp.bfloat16
I32 = jnp.int32

D = 1024
B = 2
L = 8192
T = B * L
CTX = 256
GRID_W = 64
EPS = 1e-6
H = 4
DK = 128
DV = 128
KD = H * DK
IN_W = 6144
HYW = 512
HY_EMB = 33
NGRP = 4
NEPG = 8
NEXP = NGRP * NEPG
DEXP = 512
HY_DECAY_TARGET = 1e-2
HY_FAST_PCT = 0.3
HY_SLOW_PCT = 1.5

V7X_LANES = 128
V7X_SUBLANES = 8
V7X_VMEM_BYTES = 64 * 1024 * 1024
VMEM_LIMIT = (3 * V7X_VMEM_BYTES) // 4

FFT_N = 2 * L
FFT_P = 128
FFT_BB = 16

TM_IN = 1024
TN_IN = 1024
TH = 128
HG_CHUNKS = 4
CB = 32
TM_HY = 2048
HALO = 2 * V7X_SUBLANES
TM_MG = 512
TR = 1024
TE = 512
TPOS = 4096
NP_ROWS = 2 * T + NEXP * TE
NT_EXP = NP_ROWS // TE
TS = 512
TC = TS // 2
SCATTER_SLOTS = 3
ROW_UNROLL = 8
ROW_GROUP = 32
ST_OFF = NEXP
NA_OFF = 2 * NEXP


def _cparams(n_axes, vmem=None):
    return pltpu.CompilerParams(dimension_semantics=("arbitrary",) * n_axes,
                                vmem_limit_bytes=vmem)


def _split3(x):
    hi = x.astype(BF16)
    r = x - hi.astype(F32)
    mid = r.astype(BF16)
    lo = (r - mid.astype(F32)).astype(BF16)
    return hi, mid, lo


def _dot01(m, x):
    hi, mid, lo = _split3(x)
    return (jnp.dot(m, hi, preferred_element_type=F32) + jnp.dot(m, mid, preferred_element_type=F32)
            + jnp.dot(m, lo, preferred_element_type=F32))


def _dot_split(a, w_ref):
    a_hi = a.astype(BF16)
    a_lo = (a - a_hi.astype(F32)).astype(BF16)
    return (jnp.dot(a_hi, w_ref[0], preferred_element_type=F32) + jnp.dot(a_hi, w_ref[1], preferred_element_type=F32)
            + jnp.dot(a_lo, w_ref[0], preferred_element_type=F32))


def _hi_lo(w):
    hi = w.astype(BF16)
    return jnp.stack([hi, (w - hi.astype(F32)).astype(BF16)])


def _rms(x):
    return x * lax.rsqrt(jnp.mean(x * x, axis=-1, keepdims=True) + EPS)


def _lane_pick(x, lane, idx):
    return jnp.sum(jnp.where(lane == idx, x, 0.0), axis=-1, keepdims=True)


def _ada_kernel(c_ref, w_ref, b_ref, o_ref):
    c = c_ref[...]
    s = c * jax.nn.sigmoid(c)
    w = w_ref[...]
    w_hi = w.astype(BF16)
    w_lo = (w - w_hi.astype(F32)).astype(BF16)
    s_hi = s.astype(BF16)
    s_lo = (s - s_hi.astype(F32)).astype(BF16)
    o_ref[...] = (jnp.dot(s_hi, w_hi, preferred_element_type=F32) + jnp.dot(s_hi, w_lo, preferred_element_type=F32)
                  + jnp.dot(s_lo, w_hi, preferred_element_type=F32) + b_ref[...])


def _adaln(cvec, w, b):
    tn = 1536
    return pl.pallas_call(
        _ada_kernel,
        grid=(6 * D // tn,),
        in_specs=[pl.BlockSpec((8, D), lambda j: (0, 0)),
                  pl.BlockSpec((D, tn), lambda j: (0, j)),
                  pl.BlockSpec((1, tn), lambda j: (0, j))],
        out_specs=pl.BlockSpec((8, tn), lambda j: (0, j)),
        out_shape=jax.ShapeDtypeStruct((8, 6 * D), F32),
        compiler_params=_cparams(1, VMEM_LIMIT),
        name="adaln",
    )(cvec, w, b)


def _keys(z, lb):
    sig = jax.nn.sigmoid(z)
    logf = jnp.log(lb + (1.0 - lb) * sig)
    k = (1.0 - lb) * jax.nn.sigmoid(-z)
    return k, logf


def _ctx_kernel(ctx_ref, g_ref, sh_ref, sc_ref, w_ref, lb_ref, sf_ref, sb_ref):
    h = _rms(ctx_ref[...]) * g_ref[...]
    h = h * (1.0 + sc_ref[...]) + sh_ref[...]
    p = jnp.dot(h.astype(BF16), w_ref[...], preferred_element_type=F32)
    zf, zb, v = p[:, :KD], p[:, KD:2 * KD], p[:, 2 * KD:]
    kf, lf = _keys(zf, lb_ref[0:1, :])
    kb, lbk = _keys(zb, lb_ref[1:2, :])
    r = lax.broadcasted_iota(I32, (CTX, CTX), 0)
    c = lax.broadcasted_iota(I32, (CTX, CTX), 1)
    tril = jnp.where(c <= r, 1.0, 0.0).astype(BF16)
    cf = _dot01(tril, lf)
    cb = _dot01(tril, lbk)
    kfd = (kf * jnp.exp(cf[CTX - 1:CTX, :] - cf)).astype(BF16)
    kbd = (kb * jnp.exp(cb - lbk)).astype(BF16)
    vb = v.astype(BF16)
    tn = (((0,), (0,)), ((), ()))
    for hh in range(H):
        hs = slice(hh * DK, (hh + 1) * DK)
        sf_ref[hh] = lax.dot_general(vb[:, hs], kfd[:, hs], tn, preferred_element_type=F32)
        sb_ref[hh] = lax.dot_general(vb[:, hs], kbd[:, hs], tn, preferred_element_type=F32)


def _context_states(ctx, g1, csh1, csc1, w_ctx, lbs):
    st = jax.ShapeDtypeStruct((B, H, DV, DK), F32)
    return pl.pallas_call(
        _ctx_kernel,
        grid=(B,),
        in_specs=[pl.BlockSpec((None, CTX, D), lambda b: (b, 0, 0)),
                  pl.BlockSpec((1, D), lambda b: (0, 0)),
                  pl.BlockSpec((1, D), lambda b: (0, 0)),
                  pl.BlockSpec((1, D), lambda b: (0, 0)),
                  pl.BlockSpec((D, 3 * KD), lambda b: (0, 0)),
                  pl.BlockSpec((2, KD), lambda b: (0, 0))],
        out_specs=(pl.BlockSpec((None, H, DV, DK), lambda b: (b, 0, 0, 0)),
                   pl.BlockSpec((None, H, DV, DK), lambda b: (b, 0, 0, 0))),
        out_shape=(st, st),
        compiler_params=_cparams(1, VMEM_LIMIT),
        name="ctx_states",
    )(ctx, g1, csh1, csc1, w_ctx, lbs)


def _pos_tile(er_ref, ec_ref, row0, nrow):
    lo = jnp.concatenate([jnp.broadcast_to(er_ref[pl.ds(row0 + i, 1), :], (GRID_W, D // 2))
                          for i in range(nrow)], axis=0)
    hi = jnp.concatenate([ec_ref[...]] * nrow, axis=0)
    return jnp.concatenate([lo, hi], axis=1)


def _inproj_kernel(x_ref, er_ref, ec_ref, g_ref, sh_ref, sc_ref, w_ref, oz_ref, o_ref, hx_ref):
    i = pl.program_id(0)
    j = pl.program_id(1)
    ni = pl.num_programs(0)
    nj = pl.num_programs(1)
    slot = i % 2

    def prepare(tile, dst):
        nrow = TM_IN // GRID_W
        row0 = (tile % (L // TM_IN)) * nrow
        h = _rms(x_ref[...] + _pos_tile(er_ref, ec_ref, row0, nrow)) * g_ref[...]
        hx_ref[dst] = (h * (1.0 + sc_ref[...]) + sh_ref[...]).astype(BF16)

    def project():
        return jnp.dot(hx_ref[slot], w_ref[...], preferred_element_type=F32)

    @pl.when((i == 0) & (j == 0))
    def _():
        prepare(0, 0)

    @pl.when(j == 0)
    def _():
        r = project()
        o_ref[:, :KD] = r[:, :KD].astype(BF16)
        oz_ref[:, :KD] = r[:, KD:]

    @pl.when(j == 1)
    def _():
        r = project()
        oz_ref[:, KD:] = r[:, :KD]
        o_ref[:, KD:] = r[:, KD:].astype(BF16)

    @pl.when((j > 1) & (j < nj - 1))
    def _():
        o_ref[...] = project().astype(BF16)

    @pl.when((j == nj - 1) & (i + 1 < ni))
    def _():
        o_ref[...] = project().astype(BF16)
        prepare(i + 1, 1 - slot)

    @pl.when((j == nj - 1) & (i + 1 == ni))
    def _():
        o_ref[...] = project().astype(BF16)


def _in_projection(x2, er, ec, g1, sh1, sc1, w_bf):
    tiles_per_batch = L // TM_IN
    n_i = T // TM_IN
    n_j = IN_W // TN_IN
    ahead = lambda i, j: jnp.minimum(i + jnp.where(j == n_j - 1, 1, 0), n_i - 1)
    return pl.pallas_call(
        _inproj_kernel,
        grid=(n_i, n_j),
        in_specs=[pl.BlockSpec((TM_IN, D), lambda i, j: (ahead(i, j), 0)),
                  pl.BlockSpec((L // GRID_W, D // 2), lambda i, j: (0, 0)),
                  pl.BlockSpec((GRID_W, D // 2), lambda i, j: (0, 0)),
                  pl.BlockSpec((1, D), lambda i, j: (0, 0)),
                  pl.BlockSpec((None, 1, D), lambda i, j: (ahead(i, j) // tiles_per_batch, 0, 0)),
                  pl.BlockSpec((None, 1, D), lambda i, j: (ahead(i, j) // tiles_per_batch, 0, 0)),
                  pl.BlockSpec((D, TN_IN), lambda i, j: (0, j))],
        out_specs=(pl.BlockSpec((TM_IN, TN_IN), lambda i, j: (i, 0)),
                   pl.BlockSpec((TM_IN, TN_IN), lambda i, j: (i, jnp.maximum(j - 1, 0)))),
        out_shape=(jax.ShapeDtypeStruct((T, 2 * KD), F32),
                   jax.ShapeDtypeStruct((T, IN_W - 2 * KD), BF16)),
        scratch_shapes=[pltpu.VMEM((2, TM_IN, D), BF16)],
        compiler_params=_cparams(2, VMEM_LIMIT),
        name="in_proj",
    )(x2, er, ec, g1, sh1, sc1, w_bf)


def _hgrn_kernel(reverse, readout, *refs):
    if readout:
        q_ref, z_ref, v_ref, lb_ref, s0_ref, of_ref, g_ref, ng_ref, o_ref, st_ref = refs
    else:
        q_ref, z_ref, v_ref, lb_ref, s0_ref, o_ref, st_ref = refs

    @pl.when(pl.program_id(0) == 0)
    def _():
        st_ref[...] = s0_ref[...]

    r = lax.broadcasted_iota(I32, (TH, TH), 0)
    c = lax.broadcasted_iota(I32, (TH, TH), 1)
    cb_shift = CB.bit_length() - 1
    same = jnp.right_shift(r, cb_shift) == jnp.right_shift(c, cb_shift)
    tri_mask = same & ((c >= r) if reverse else (c <= r))
    tri = jnp.where(tri_mask, 1.0, 0.0).astype(BF16)
    rblk = jnp.right_shift(r, cb_shift)
    cblk = jnp.right_shift(c, cb_shift)
    dist = (rblk - cblk) if not reverse else (cblk - rblk)
    for cc in (range(HG_CHUNKS - 1, -1, -1) if reverse else range(HG_CHUNKS)):
        for b in range(B):
            _hgrn_chunk(reverse, readout, b, slice(cc * TH, (cc + 1) * TH), tri, tri_mask, dist, refs)


def _hgrn_chunk(reverse, readout, b, rs, tri, tri_mask, dist, refs):
    if readout:
        q_ref, z_ref, v_ref, lb_ref, s0_ref, of_ref, g_ref, ng_ref, o_ref, st_ref = refs
    else:
        q_ref, z_ref, v_ref, lb_ref, s0_ref, o_ref, st_ref = refs
    q = q_ref[b, rs, :].astype(F32)
    v = v_ref[b, rs, :]
    k, logf = _keys(z_ref[b, rs, :], lb_ref[...])
    bl = _dot01(tri, logf)
    nt = (((1,), (1,)), ((), ()))
    tn = (((0,), (0,)), ((), ()))
    nblk = TH // CB
    e_row = 0 if reverse else CB - 1
    m_row = CB - 1 - CB // 2 if reverse else CB // 2
    tau = [bl[jb * CB + e_row:jb * CB + e_row + 1] for jb in range(nblk)]
    mid = [bl[jb * CB + m_row:jb * CB + m_row + 1] for jb in range(nblk)]
    rows = lambda vecs: jnp.concatenate([jnp.broadcast_to(x, (CB, KD)) for x in vecs], axis=0)
    mid_b = rows(mid)
    qd0 = (q * jnp.exp(bl - mid_b)).astype(BF16)
    kd0 = (k * jnp.exp(mid_b - bl)).astype(BF16)
    qs = q * jnp.exp(bl)
    ke = k * jnp.exp(rows(tau) - bl)
    order = list(range(nblk - 1, -1, -1)) if reverse else list(range(nblk))
    pre = [jnp.zeros((1, KD), F32)]
    for i in range(nblk):
        pre.append(pre[-1] + tau[order[i]])
    total = pre[nblk]
    entry = [None] * nblk
    leave = [None] * nblk
    gap = [[None] * nblk for _ in range(nblk)]
    for i, jb in enumerate(order):
        entry[jb] = jnp.exp(pre[i])
        leave[jb] = jnp.exp(total - pre[i + 1])
        for d in range(2, nblk):
            gap[d][jb] = jnp.exp(pre[i + d] - pre[i + 1]) if i + d < nblk else jnp.zeros((1, KD), F32)
    qc = (qs * rows(entry)).astype(BF16)
    kc = (ke * rows(leave)).astype(BF16)
    kx = jnp.concatenate([ke.astype(BF16)] + [(ke * rows(gap[d])).astype(BF16) for d in range(2, nblk)], axis=0)
    qsb = qs.astype(BF16)
    dec = jnp.exp(total)

    def blockdiag(x):
        first_head = lax.broadcasted_iota(I32, x.shape, 1) < DK
        zero = jnp.zeros_like(x)
        return jnp.concatenate([jnp.where(first_head, x, zero), jnp.where(first_head, zero, x)], axis=0)

    for hp in range(H // 2):
        ps = slice(2 * hp * DK, 2 * (hp + 1) * DK)
        sc0 = lax.dot_general(qd0[:, ps], blockdiag(kd0[:, ps]), nt, preferred_element_type=F32)
        scx = lax.dot_general(qsb[:, ps], blockdiag(kx[:, ps]), nt, preferred_element_type=F32)
        halves = []
        for hh in range(2):
            sc = jnp.where(tri_mask, sc0[:, hh * TH:(hh + 1) * TH], 0.0)
            base = hh * (nblk - 1) * TH
            for d in range(1, nblk):
                sc = jnp.where(dist == d, scx[:, base + (d - 1) * TH:base + d * TH], sc)
            halves.append(sc.astype(BF16))
        sc_pair = jnp.concatenate(halves, axis=1)
        st_a = st_ref[b, 2 * hp]
        st_b = st_ref[b, 2 * hp + 1]
        zst = jnp.zeros((DV, DK), BF16)
        st_pair = jnp.concatenate([jnp.concatenate([st_a.astype(BF16), zst], axis=1),
                                   jnp.concatenate([zst, st_b.astype(BF16)], axis=1)], axis=0)
        o_pair = (lax.dot_general(qc[:, ps], st_pair, nt, preferred_element_type=F32)
                  + jnp.dot(sc_pair, blockdiag(v[:, ps]), preferred_element_type=F32))
        upd = lax.dot_general(v[:, ps], kc[:, ps], tn, preferred_element_type=F32)
        st_ref[b, 2 * hp] = st_a * dec[:, ps][:, :DK] + upd[:DV, :DK]
        st_ref[b, 2 * hp + 1] = st_b * dec[:, ps][:, DK:] + upd[DV:, DK:]
        for hh in range(2):
            hs = slice((2 * hp + hh) * DK, (2 * hp + hh + 1) * DK)
            o_h = o_pair[:, hh * DV:(hh + 1) * DV]
            if readout:
                o_h = o_h + of_ref[b, rs, hs]
                o_h = _rms(o_h) * ng_ref[...]
                gh = g_ref[b, rs, hs].astype(F32)
                o_h = o_h * (gh * jax.nn.sigmoid(gh))
            o_ref[b, rs, hs] = o_h.astype(o_ref.dtype)


def _hgrn_scan(pz, p, lb_row, s0, reverse, o_f=None, norm_g=None):
    nch = L // (HG_CHUNKS * TH)
    chunk = (lambda c: nch - 1 - c) if reverse else (lambda c: c)
    col_spec = lambda j: pl.BlockSpec((B, HG_CHUNKS * TH, KD), lambda c: (0, chunk(c), j))
    in_specs = [col_spec(0), col_spec(1 if reverse else 0), col_spec(1),
                pl.BlockSpec((1, KD), lambda c: (0, 0)),
                pl.BlockSpec((B, H, DV, DK), lambda c: (0, 0, 0, 0))]
    p3 = p.reshape(B, L, p.shape[-1])
    args = [p3, pz.reshape(B, L, 2 * KD), p3, lb_row, s0]
    readout = o_f is not None
    if readout:
        in_specs += [col_spec(0), col_spec(2), pl.BlockSpec((1, DV), lambda c: (0, 0))]
        args += [o_f, p3, norm_g]
    return pl.pallas_call(
        functools.partial(_hgrn_kernel, reverse, readout),
        grid=(nch,),
        in_specs=in_specs,
        out_specs=col_spec(0),
        out_shape=jax.ShapeDtypeStruct((B, L, KD), BF16 if readout else F32),
        scratch_shapes=[pltpu.VMEM((B, H, DV, DK), F32)],
        compiler_params=_cparams(1, VMEM_LIMIT),
        name="hgrn_bwd_readout" if readout else "hgrn_fwd",
    )(*args)


def _hy_pre_kernel(v_ref, x1_ref, x0_ref, vp_ref, x1p_ref, x0p_ref, vn_ref, x1n_ref, x0n_ref,
                   w_ref, b_ref, vx_ref, x0o_ref):
    i = pl.program_id(1)
    first = i == 0
    last = i == pl.num_programs(1) - 1
    row = lax.broadcasted_iota(I32, (TM_HY, 1), 0)

    def conv(c_ref, p_ref, n_ref, col):
        x = c_ref[...].astype(F32)
        prev_row = jnp.where(first, 0.0, p_ref[...].astype(F32)[HALO - 1:HALO, :])
        next_row = jnp.where(last, 0.0, n_ref[...].astype(F32)[0:1, :])
        xm = jnp.where(row == 0, prev_row, pltpu.roll(x, 1, axis=0))
        xp = jnp.where(row == TM_HY - 1, next_row, pltpu.roll(x, TM_HY - 1, axis=0))
        cs = slice(col * HYW, (col + 1) * HYW)
        return xm * w_ref[0:1, cs] + x * w_ref[1:2, cs] + xp * w_ref[2:3, cs] + b_ref[:, cs]

    v = conv(v_ref, vp_ref, vn_ref, 0)
    x1 = conv(x1_ref, x1p_ref, x1n_ref, 1)
    x0 = conv(x0_ref, x0p_ref, x0n_ref, 2)
    vx_ref[...] = v * x1
    x0o_ref[...] = x0.astype(x0o_ref.dtype)


def _hyena_pre(p, conv_w, conv_b):
    nt = L // TM_HY
    hb = TM_HY // HALO
    nhb = T // HALO
    cur = lambda col: pl.BlockSpec((TM_HY, HYW), lambda b, i: (b * nt + i, col))
    prv = lambda col: pl.BlockSpec((HALO, HYW), lambda b, i: (jnp.maximum((b * nt + i) * hb - 1, 0), col))
    nxt = lambda col: pl.BlockSpec((HALO, HYW), lambda b, i: (jnp.minimum((b * nt + i + 1) * hb, nhb - 1), col))
    c0 = 3
    out = jax.ShapeDtypeStruct((T, HYW), F32)
    return pl.pallas_call(
        _hy_pre_kernel,
        grid=(B, nt),
        in_specs=[cur(c0), cur(c0 + 1), cur(c0 + 2), prv(c0), prv(c0 + 1), prv(c0 + 2),
                  nxt(c0), nxt(c0 + 1), nxt(c0 + 2),
                  pl.BlockSpec((3, 3 * HYW), lambda b, i: (0, 0)),
                  pl.BlockSpec((1, 3 * HYW), lambda b, i: (0, 0))],
        out_specs=(pl.BlockSpec((TM_HY, HYW), lambda b, i: (b * nt + i, 0)),
                   pl.BlockSpec((TM_HY, HYW), lambda b, i: (b * nt + i, 0))),
        out_shape=(out, jax.ShapeDtypeStruct((T, HYW), BF16)),
        compiler_params=_cparams(2, VMEM_LIMIT),
        name="hyena_pre",
    )(p, p, p, p, p, p, p, p, p, conv_w, conv_b)


def _taps_dft_kernel(z_ref, w1_ref, b1_ref, f1_ref, w2_ref, b2_ref, f2_ref, w3a_ref, w3b_ref, dl_ref, fm_ref,
                     o_hbm, obuf, sem):
    g = pl.program_id(0)
    na = L // FFT_P
    half = z_ref.shape[0] // 2
    zt = z_ref[0:half, :]
    zb = z_ref[half:, :]
    lane = lax.broadcasted_iota(I32, zt.shape, 1)
    dot = _dot_split
    h = jnp.sin(f1_ref[...] * (dot(jnp.concatenate([zt, zb], axis=1), w1_ref) + b1_ref[...]))
    h = jnp.sin(f2_ref[...] * (dot(h, w2_ref) + b2_ref[...]))

    def copies(step):
        return [pltpu.make_async_copy(obuf.at[n, jj], o_hbm.at[n, :, step * FFT_BB + jj, :], sem)
                for n in range(2) for jj in range(FFT_BB)]

    @pl.when(g > 0)
    def _():
        for cp in copies(g - 1):
            cp.wait()

    for hi, (zz, w3_ref) in enumerate(((zt, w3a_ref), (zb, w3b_ref))):
        taps = dot(h, w3_ref)
        win = jnp.exp(-_lane_pick(zz, lane, 0) * dl_ref[...])
        fwd = (taps[:, :HYW] * win).astype(BF16)
        bwd = (taps[:, HYW:] * win * _lane_pick(zz, lane, HY_EMB)).astype(BF16)
        for bb in range(FFT_BB // 2):
            jj = hi * (FFT_BB // 2) + bb
            rows = slice(bb * na, (bb + 1) * na)
            obuf[0, jj] = jnp.dot(fm_ref[...], fwd[rows], preferred_element_type=F32)
            obuf[1, jj] = jnp.dot(fm_ref[...], bwd[rows], preferred_element_type=F32)

    for cp in copies(g):
        cp.start()

    @pl.when(g == pl.num_programs(0) - 1)
    def _():
        for cp in copies(g):
            cp.wait()


def _filter_taps_dft(zin, w1, b1, f1, w2, b2, f2, w3a, w3b, deltas, fmat):
    na = L // FFT_P
    tm = FFT_BB * na
    ln = V7X_LANES
    full = lambda shape: pl.BlockSpec(shape, lambda i: tuple(0 for _ in shape))
    return pl.pallas_call(
        _taps_dft_kernel,
        grid=(FFT_P // FFT_BB,),
        in_specs=[pl.BlockSpec((tm, ln), lambda i: (i, 0)),
                  full((2, 2 * ln, ln)), full((1, ln)), full((1, ln)),
                  full((2, ln, ln)), full((1, ln)), full((1, ln)),
                  full((2, ln, 2 * HYW)), full((2, ln, 2 * HYW)), full((1, HYW)), full((FFT_P, na))],
        out_specs=pl.BlockSpec(memory_space=pl.ANY),
        out_shape=jax.ShapeDtypeStruct((2, FFT_P, FFT_P, HYW), F32),
        scratch_shapes=[pltpu.VMEM((2, FFT_BB, FFT_P, HYW), F32), pltpu.SemaphoreType.DMA(())],
        compiler_params=_cparams(1, VMEM_LIMIT),
        name="filter_taps_dft",
    )(zin, w1, b1, f1, w2, b2, f2, w3a, w3b, deltas, fmat)


def _strided_dft_kernel(x_hbm, f_ref, o_hbm, xbuf, obuf, sem_in, sem_out):
    g = pl.program_id(0)
    ng = pl.num_programs(0)
    nb = FFT_P // FFT_BB

    def copies(grp, slot, inbound):
        n = grp // nb
        b0 = (grp % nb) * FFT_BB
        if inbound:
            return [pltpu.make_async_copy(x_hbm.at[n, :, b0 + jj, :], xbuf.at[slot, jj], sem_in.at[slot])
                    for jj in range(FFT_BB)]
        return [pltpu.make_async_copy(obuf.at[slot, jj], o_hbm.at[n, :, b0 + jj, :], sem_out.at[slot])
                for jj in range(FFT_BB)]

    def start(grp, slot, inbound):
        for cp in copies(grp, slot, inbound):
            cp.start()

    def wait(grp, slot, inbound):
        for cp in copies(grp, slot, inbound):
            cp.wait()

    @pl.when(g == 0)
    def _():
        start(0, 0, True)

    for slot in range(2):
        grp = 2 * g + slot
        if slot == 0:
            start(grp + 1, 1, True)
        else:
            @pl.when(g + 1 < ng)
            def _():
                start(grp + 1, 0, True)
        wait(grp, slot, True)

        @pl.when(g > 0)
        def _():
            wait(grp - 2, slot, False)

        for jj in range(FFT_BB):
            obuf[slot, jj] = jnp.dot(f_ref[...], xbuf[slot, jj].astype(BF16), preferred_element_type=F32)
        start(grp, slot, False)

    @pl.when(g + 1 == ng)
    def _():
        wait(2 * g, 0, False)
        wait(2 * g + 1, 1, False)


def _strided_dft(xv, fmat, name):
    n, kk = xv.shape[0], xv.shape[1]
    mm = fmat.shape[0]
    groups = n * (FFT_P // FFT_BB)
    return pl.pallas_call(
        _strided_dft_kernel,
        grid=(groups // 2,),
        in_specs=[pl.BlockSpec(memory_space=pl.ANY),
                  pl.BlockSpec((mm, kk), lambda g: (0, 0))],
        out_specs=pl.BlockSpec(memory_space=pl.ANY),
        out_shape=jax.ShapeDtypeStruct((n, mm, FFT_P, HYW), F32),
        scratch_shapes=[pltpu.VMEM((2, FFT_BB, kk, HYW), F32), pltpu.VMEM((2, FFT_BB, mm, HYW), F32),
                        pltpu.SemaphoreType.DMA((2,)), pltpu.SemaphoreType.DMA((2,))],
        compiler_params=_cparams(1, VMEM_LIMIT),
        name=name,
    )(xv, fmat)


def _cblock(mr, mi):
    return jnp.concatenate([jnp.concatenate([mr, -mi], axis=1), jnp.concatenate([mi, mr], axis=1)], axis=0)


FFT_H = FFT_P // 2
KSPEC_DD = 8
MID_DD = 4


def _twiddled(fr_ref, fi_ref, tw):
    twr = tw[0:1, :]
    twi = tw[1:2, :]
    fr = fr_ref[...]
    fi = fi_ref[...]
    return fr * twr - fi * twi, fr * twi + fi * twr


def _cmul_rows(x, kr, ki):
    xr, xi = x[:FFT_P], x[FFT_P:]
    return jnp.concatenate([xr * kr - xi * ki, xr * ki + xi * kr], axis=0).astype(BF16)


_TN_DIMS = (((0,), (0,)), ((), ()))

def _kspec_kernel(a_ref, fr_ref, fi_ref, tw_ref, twh_ref, o_ref, oh_ref):
    g = pl.program_id(0)
    dot = functools.partial(jnp.dot, preferred_element_type=F32)

    def combine(xf, xb, store):
        store(0, xf[:FFT_P] + xb[:FFT_P])
        store(1, xf[FFT_P:] - xb[FFT_P:])

    def general(u):
        def store(ri, val):
            o_ref[ri, u] = val
        rm = _cblock(*_twiddled(fr_ref, fi_ref, tw_ref[u])).astype(BF16)
        combine(dot(rm, a_ref[0, u].reshape(2 * FFT_P, HYW).astype(BF16)),
                dot(rm, a_ref[1, u].reshape(2 * FFT_P, HYW).astype(BF16)), store)

    @pl.when(g > 0)
    def _():
        general(0)

    @pl.when(g == 0)
    def _():
        def store_first(ri, val):
            o_ref[ri, 0] = val

        def store_half(ri, val):
            oh_ref[ri] = val
        for slot, tw, store in ((0, tw_ref[0], store_first), (1, twh_ref[...], store_half)):
            w = jnp.concatenate(_twiddled(fr_ref, fi_ref, tw), axis=0).astype(BF16)
            combine(dot(w, a_ref[0, 0, slot].astype(BF16)), dot(w, a_ref[1, 0, slot].astype(BF16)), store)

    for u in range(1, KSPEC_DD):
        general(u)


def _kernel_spectrum(ak, fr, fi, tw, twh):
    same = lambda shape: pl.BlockSpec(shape, lambda g: tuple(0 for _ in shape))
    return pl.pallas_call(
        _kspec_kernel,
        grid=(FFT_H // KSPEC_DD,),
        in_specs=[pl.BlockSpec((2, KSPEC_DD, 2, FFT_P, HYW), lambda g: (0, g, 0, 0, 0)),
                  same((FFT_P, FFT_P)), same((FFT_P, FFT_P)),
                  pl.BlockSpec((KSPEC_DD, 2, FFT_P), lambda g: (g, 0, 0)), same((2, FFT_P))],
        out_specs=(pl.BlockSpec((2, KSPEC_DD, FFT_P, HYW), lambda g: (0, g, 0, 0)),
                   pl.BlockSpec((2, FFT_P, HYW), lambda g: (0, 0, 0))),
        out_shape=(jax.ShapeDtypeStruct((2, FFT_H, FFT_P, HYW), F32),
                   jax.ShapeDtypeStruct((2, FFT_P, HYW), F32)),
        compiler_params=_cparams(1, VMEM_LIMIT),
        name="kernel_spectrum",
    )(ak, fr, fi, tw, twh)


def _mid_kernel(a_ref, kh_ref, khh_ref, fr_ref, fi_ref, tw_ref, twh_ref, o_ref):
    g = pl.program_id(0)
    dot = functools.partial(jnp.dot, preferred_element_type=F32)
    dot_t = lambda w, y: lax.dot_general(w, y, _TN_DIMS, preferred_element_type=F32)

    def general(u):
        rm = _cblock(*_twiddled(fr_ref, fi_ref, tw_ref[u])).astype(BF16)
        for n in range(B):
            x = dot(rm, a_ref[n, u].reshape(2 * FFT_P, HYW).astype(BF16))
            y = _cmul_rows(x, kh_ref[0, u], kh_ref[1, u])
            o_ref[n, u] = dot_t(rm, y).reshape(2, FFT_P, HYW)

    @pl.when(g > 0)
    def _():
        general(0)

    @pl.when(g == 0)
    def _():
        for slot, tw, kr, ki in ((0, tw_ref[0], kh_ref[0, 0], kh_ref[1, 0]), (1, twh_ref[...], khh_ref[0], khh_ref[1])):
            w = jnp.concatenate(_twiddled(fr_ref, fi_ref, tw), axis=0).astype(BF16)
            for n in range(B):
                y = _cmul_rows(dot(w, a_ref[n, 0, slot].astype(BF16)), kr, ki)
                o_ref[n, 0, slot] = dot_t(w, y)

    for u in range(1, MID_DD):
        general(u)


def _fft_mid(au, kh, khh, fr, fi, tw, twh):
    pair = pl.BlockSpec((B, MID_DD, 2, FFT_P, HYW), lambda g: (0, g, 0, 0, 0))
    same = lambda shape: pl.BlockSpec(shape, lambda g: tuple(0 for _ in shape))
    return pl.pallas_call(
        _mid_kernel,
        grid=(FFT_H // MID_DD,),
        in_specs=[pair,
                  pl.BlockSpec((2, MID_DD, FFT_P, HYW), lambda g: (0, g, 0, 0)),
                  same((2, FFT_P, HYW)), same((FFT_P, FFT_P)), same((FFT_P, FFT_P)),
                  pl.BlockSpec((MID_DD, 2, FFT_P), lambda g: (g, 0, 0)), same((2, FFT_P))],
        out_specs=pair,
        out_shape=jax.ShapeDtypeStruct((B, FFT_H, 2, FFT_P, HYW), F32),
        compiler_params=_cparams(1, VMEM_LIMIT),
        name="fft_mid",
    )(au, kh, khh, fr, fi, tw, twh)


def _dft_tables():
    na = L // FFT_P
    a = np.arange(na)
    dd = np.arange(FFT_H)
    ang = 2.0 * np.pi * np.outer(dd, a) / FFT_P
    re_rows = np.cos(ang)
    im_rows = -np.sin(ang)
    im_rows[0] = np.cos(np.pi * a)
    f_first = np.stack([re_rows, im_rows], axis=1).reshape(FFT_P, na)
    gre = 2.0 * np.cos(ang)
    gim = -2.0 * np.sin(ang)
    gre[0] = 1.0
    gim[0] = np.cos(np.pi * a)
    g_last = np.stack([gre, gim], axis=1).reshape(FFT_P, na).T / FFT_N
    b = np.arange(FFT_P)
    angf = 2.0 * np.pi * np.outer(b, b) / FFT_P
    ang2 = 2.0 * np.pi * np.outer(np.arange(FFT_H + 1), b) / FFT_N
    tw = np.stack([np.cos(ang2), -np.sin(ang2)], axis=1)
    f32 = lambda x: jnp.asarray(x.astype(np.float32))
    return (f32(f_first).astype(BF16), f32(g_last).astype(BF16), f32(np.cos(angf)), f32(-np.sin(angf)),
            f32(tw[:FFT_H]), f32(tw[FFT_H]))


def _merge_kernel(ya_ref, cv_ref, vx_ref, x0_ref, ga_ref, gb_ref, x_ref, er_ref, ec_ref, gt1_ref, hyd_ref,
                  wpa_ref, wpb_ref, wo_ref, g2_ref, sh2_ref, sc2_ref, wr_ref, br_ref,
                  x1_ref, t2_ref, lg_ref):
    vx = vx_ref[...]
    yb = x0_ref[...].astype(F32) * (cv_ref[...] + vx * hyd_ref[...])
    pa = jnp.dot(ya_ref[...].astype(BF16), wpa_ref[...], preferred_element_type=F32)
    pb = jnp.dot(yb.astype(BF16), wpb_ref[...], preferred_element_type=F32)
    mixed = (jax.nn.sigmoid(ga_ref[...].astype(F32)) * pa
             + jax.nn.sigmoid(gb_ref[...].astype(F32)) * pb)
    xm = jnp.dot(mixed.astype(BF16), wo_ref[...], preferred_element_type=F32)
    nrow = TM_MG // GRID_W
    row0 = (pl.program_id(0) % (L // TM_MG)) * nrow
    x1 = x_ref[...] + _pos_tile(er_ref, ec_ref, row0, nrow) + gt1_ref[...] * xm
    x1_ref[...] = x1
    t2 = _rms(x1) * g2_ref[...]
    t2 = t2 * (1.0 + sc2_ref[...]) + sh2_ref[...]
    t2_ref[...] = t2
    t_hi = t2.astype(BF16)
    t_lo = (t2 - t_hi.astype(F32)).astype(BF16)
    rr = (jnp.dot(t_hi, wr_ref[...], preferred_element_type=F32)
          + jnp.dot(t_lo, wr_ref[...], preferred_element_type=F32))
    lg_ref[...] = rr[:, :V7X_LANES] + rr[:, V7X_LANES:] + br_ref[...]


def _merge(ya, cv, vx, x0c, p, x2, er, ec, gt1, hyd, wpa, wpb, wo, g2, sh2, sc2, wr, br):
    tpb = L // TM_MG
    half = lambda: pl.BlockSpec((TM_MG, HYW), lambda i: (i, 0))
    full = lambda shape: pl.BlockSpec(shape, lambda i: tuple(0 for _ in shape))
    perb = lambda: pl.BlockSpec((None, 1, D), lambda i: (i // tpb, 0, 0))
    return pl.pallas_call(
        _merge_kernel,
        grid=(T // TM_MG,),
        in_specs=[half(), half(), half(), half(),
                  pl.BlockSpec((TM_MG, D), lambda i: (i, 3)),
                  pl.BlockSpec((TM_MG, D), lambda i: (i, 4)),
                  pl.BlockSpec((TM_MG, D), lambda i: (i, 0)),
                  full((L // GRID_W, D // 2)), full((GRID_W, D // 2)),
                  perb(), full((1, HYW)),
                  full((KD, D)), full((HYW, D)), full((D, D)),
                  full((1, D)), perb(), perb(),
                  full((D, 2 * V7X_LANES)), full((1, V7X_LANES))],
        out_specs=(pl.BlockSpec((TM_MG, D), lambda i: (i, 0)),
                   pl.BlockSpec((TM_MG, D), lambda i: (i, 0)),
                   pl.BlockSpec((TM_MG, V7X_LANES), lambda i: (i, 0))),
        out_shape=(jax.ShapeDtypeStruct((T, D), F32), jax.ShapeDtypeStruct((T, D), F32),
                   jax.ShapeDtypeStruct((T, V7X_LANES), F32)),
        compiler_params=_cparams(1, VMEM_LIMIT),
        name="merge",
    )(ya, cv, vx, x0c, p, p, x2, er, ec, gt1, hyd, wpa, wpb, wo, g2, sh2, sc2, wr, br)


def _route_kernel(lg_ref, info_ref, cnt_ref):
    @pl.when(pl.program_id(0) == 0)
    def _():
        cnt_ref[...] = jnp.zeros_like(cnt_ref)

    lg = lg_ref[...]
    lane = lax.broadcasted_iota(I32, lg.shape, 1)
    lanef = lane.astype(F32)
    neg = -1e30
    big = 1e9
    is_g = (lane >= NEXP) & (lane < NEXP + NGRP)
    gl = jnp.where(is_g, lg, neg)
    ge = jnp.where(is_g, jnp.exp(gl - jnp.max(gl, axis=-1, keepdims=True)), 0.0)
    pg = ge / jnp.sum(ge, axis=-1, keepdims=True)
    p_top_g = jnp.max(pg, axis=-1, keepdims=True)
    gidx = jnp.min(jnp.where(is_g & (pg == p_top_g), lanef, big), axis=-1, keepdims=True)
    g_sel = gidx.astype(I32) - NEXP
    emask = (lane < NEXP) & (jnp.right_shift(lane, NEPG.bit_length() - 1) == g_sel)
    el = jnp.where(emask, lg, neg)
    ee = jnp.where(emask, jnp.exp(el - jnp.max(el, axis=-1, keepdims=True)), 0.0)
    pe = ee / jnp.sum(ee, axis=-1, keepdims=True)
    p1 = jnp.max(jnp.where(emask, pe, -1.0), axis=-1, keepdims=True)
    i1 = jnp.min(jnp.where(emask & (pe == p1), lanef, big), axis=-1, keepdims=True)
    rest = emask & (lanef != i1)
    p2 = jnp.max(jnp.where(rest, pe, -1.0), axis=-1, keepdims=True)
    i2 = jnp.min(jnp.where(rest & (pe == p2), lanef, big), axis=-1, keepdims=True)
    wsum = p1 + p2
    w1 = p_top_g * p1 / wsum
    w2 = p_top_g * p2 / wsum
    sel1 = lanef == i1
    sel2 = lanef == i2
    oh = jnp.where(sel1 | sel2, 1.0, 0.0)
    r = lax.broadcasted_iota(I32, (TR, TR), 0)
    c = lax.broadcasted_iota(I32, (TR, TR), 1)
    stril = jnp.where(c < r, 1.0, 0.0).astype(BF16)
    before = jnp.dot(stril, oh.astype(BF16), preferred_element_type=F32) + cnt_ref[...]
    r1 = jnp.sum(jnp.where(sel1, before, 0.0), axis=-1, keepdims=True)
    r2 = jnp.sum(jnp.where(sel2, before, 0.0), axis=-1, keepdims=True)
    cnt_ref[...] += jnp.sum(oh, axis=0, keepdims=True)
    info = jnp.where(lane == 0, i1, jnp.where(lane == 1, r1, jnp.where(lane == 2, i2, jnp.where(
        lane == 3, r2, jnp.where(lane == 4, w1, jnp.where(lane == 5, w2, 0.0))))))
    info_ref[...] = info


def _route(lg):
    return pl.pallas_call(
        _route_kernel,
        grid=(T // TR,),
        in_specs=[pl.BlockSpec((TR, V7X_LANES), lambda i: (i, 0))],
        out_specs=(pl.BlockSpec((TR, V7X_LANES), lambda i: (i, 0)),
                   pl.BlockSpec((1, V7X_LANES), lambda i: (0, 0))),
        out_shape=(jax.ShapeDtypeStruct((T, V7X_LANES), F32), jax.ShapeDtypeStruct((1, V7X_LANES), F32)),
        compiler_params=_cparams(1, VMEM_LIMIT),
        name="route",
    )(lg)


def _positions_kernel(info_ref, st_ref, o_ref):
    info = info_ref[...]
    lane = lax.broadcasted_iota(I32, info.shape, 1)
    lanef = lane.astype(F32)
    st = st_ref[...]
    row = lambda e_lane, r_lane: (jnp.sum(jnp.where(lanef == _lane_pick(info, lane, e_lane), st, 0.0),
                                          axis=-1, keepdims=True) + _lane_pick(info, lane, r_lane))
    o_ref[...] = jnp.where(lane == 0, row(0, 1), jnp.where(lane == 1, row(2, 3), 0.0)).astype(I32)


def _positions(info, starts_row):
    return pl.pallas_call(
        _positions_kernel,
        grid=(T // TPOS,),
        in_specs=[pl.BlockSpec((TPOS, V7X_LANES), lambda i: (i, 0)),
                  pl.BlockSpec((1, V7X_LANES), lambda i: (0, 0))],
        out_specs=pl.BlockSpec((TPOS, V7X_LANES), lambda i: (i, 0)),
        out_shape=jax.ShapeDtypeStruct((T, V7X_LANES), I32),
        compiler_params=_cparams(1, VMEM_LIMIT),
        name="positions",
    )(info, starts_row)


def _scatter_kernel(pos_ref, zt_ref, t2_hbm, zeros_hbm, xs_hbm, tbuf, fsem, sem, zsem):
    i = pl.program_id(0)

    def zcopy(row):
        start = pl.multiple_of(jnp.maximum(row, 0), TE)
        return pltpu.make_async_copy(zeros_hbm, xs_hbm.at[pl.ds(start, TE)], zsem)

    @pl.when(i == 0)
    def _():
        def ztail(start, e, carry):
            @pl.when(zt_ref[0, e] >= 0)
            def _():
                cp = zcopy(zt_ref[0, e])
                cp.start() if start else cp.wait()
            return carry

        lax.fori_loop(0, NEXP, functools.partial(ztail, True), 0)
        lax.fori_loop(0, NEXP, functools.partial(ztail, False), 0)

        def zrest(start, tile, carry):
            cp = zcopy(tile * TE)
            cp.start() if start else cp.wait()
            return carry

        lax.fori_loop(zt_ref[0, NA_OFF], NT_EXP, functools.partial(zrest, True), 0)
        lax.fori_loop(zt_ref[0, NA_OFF], NT_EXP, functools.partial(zrest, False), 0)

    n = pl.num_programs(0)
    slot = i % SCATTER_SLOTS

    def fetch(tile, s):
        return pltpu.make_async_copy(t2_hbm.at[pl.ds(pl.multiple_of(tile * TS, TS), TS)], tbuf.at[s], fsem.at[s])

    def wait_rows(s):
        _wait_rows(TS, pltpu.make_async_copy(tbuf.at[s].at[pl.ds(0, 1)], xs_hbm.at[pl.ds(0, 1)], sem.at[s]))

    @pl.when(i == 0)
    def _():
        fetch(0, 0).start()

    @pl.when(i + 1 < n)
    def _():
        fetch(i + 1, (i + 1) % SCATTER_SLOTS).start()

    fetch(i, slot).wait()

    def row_copy(base, jj, kk):
        dst = pos_ref[0, 2 * (base + jj) + kk]
        return pltpu.make_async_copy(tbuf.at[slot].at[pl.ds(base, ROW_GROUP)].at[pl.ds(jj, 1)],
                                     xs_hbm.at[pl.ds(dst, 1)], sem.at[slot])

    _start_rows(TS, row_copy)

    @pl.when(i > 0)
    def _():
        wait_rows((i + SCATTER_SLOTS - 1) % SCATTER_SLOTS)

    @pl.when(i + 1 == n)
    def _():
        wait_rows(slot)


def _start_rows(n_rows, row_copy):
    def group(g, carry):
        base = pl.multiple_of(g * ROW_GROUP, ROW_GROUP)
        for jj in range(ROW_GROUP):
            for kk in range(2):
                row_copy(base, jj, kk).start(priority=kk)
        return carry

    lax.fori_loop(0, n_rows // ROW_GROUP, group, 0)


def _wait_rows(n_rows, one_row_copy):
    def drain(j, carry):
        one_row_copy.wait()
        one_row_copy.wait()
        return carry

    lax.fori_loop(0, n_rows, drain, 0, unroll=ROW_UNROLL)


def _scatter_rows(pos3, meta, t2, zeros_tile):
    return pl.pallas_call(
        _scatter_kernel,
        grid=(T // TS,),
        in_specs=[pl.BlockSpec((None, 1, 2 * TS), lambda i: (i, 0, 0), memory_space=pltpu.SMEM),
                  pl.BlockSpec(memory_space=pltpu.SMEM),
                  pl.BlockSpec(memory_space=pl.ANY),
                  pl.BlockSpec(memory_space=pl.ANY)],
        out_specs=pl.BlockSpec(memory_space=pl.ANY),
        out_shape=jax.ShapeDtypeStruct((NP_ROWS, D), F32),
        scratch_shapes=[pltpu.VMEM((SCATTER_SLOTS, TS, D), F32), pltpu.SemaphoreType.DMA((SCATTER_SLOTS,)),
                        pltpu.SemaphoreType.DMA((SCATTER_SLOTS,)), pltpu.SemaphoreType.DMA(())],
        compiler_params=_cparams(1, VMEM_LIMIT),
        name="scatter_rows",
    )(pos3, meta, t2, zeros_tile)


def _expert_kernel(te_ref, nx_ref, sl_ref, na_ref, xs_ref, wg_hbm, wu_hbm, wd_hbm, ys_ref,
                   wgs_ref, wus_ref, wds_ref, wgb_ref, wub_ref, wdb_ref, sem):
    i = pl.program_id(0)
    active = i < na_ref[0]
    first = active & ((i == 0) | (te_ref[i] != te_ref[jnp.maximum(i - 1, 0)]))

    def fetch(e, slot):
        return [pltpu.make_async_copy(wg_hbm.at[e], wgs_ref.at[slot], sem.at[slot]),
                pltpu.make_async_copy(wu_hbm.at[e], wus_ref.at[slot], sem.at[slot]),
                pltpu.make_async_copy(wd_hbm.at[e], wds_ref.at[slot], sem.at[slot])]

    for slot in range(2):
        @pl.when(first & (sl_ref[i] == slot))
        def _():
            @pl.when(i == 0)
            def _():
                for cp in fetch(te_ref[i], slot):
                    cp.start()

            for cp in fetch(te_ref[i], slot):
                cp.wait()
            wgb_ref[...] = wgs_ref[slot].astype(BF16)
            wub_ref[...] = wus_ref[slot].astype(BF16)
            wdb_ref[...] = wds_ref[slot].astype(BF16)

            @pl.when(nx_ref[i] >= 0)
            def _():
                for cp in fetch(nx_ref[i], 1 - slot):
                    cp.start()

    @pl.when(active)
    def _():
        x = xs_ref[...].astype(BF16)
        g = jnp.dot(x, wgb_ref[...], preferred_element_type=F32)
        u = jnp.dot(x, wub_ref[...], preferred_element_type=F32)
        hid = (g * jax.nn.sigmoid(g) * u).astype(BF16)
        ys_ref[...] = jnp.dot(hid, wdb_ref[...], preferred_element_type=F32)

    @pl.when(jnp.logical_not(active))
    def _():
        ys_ref[...] = jnp.zeros_like(ys_ref)


def _experts(tile_expert, tile_next, tile_slot, n_active, xs, wg, wu, wd):
    rows = lambda i, *_: (i, 0)
    rows_in = lambda i, te, nx, sl, na: (jnp.minimum(i, na[0] - 1), 0)
    any_space = pl.BlockSpec(memory_space=pl.ANY)
    grid_spec = pltpu.PrefetchScalarGridSpec(
        num_scalar_prefetch=4,
        grid=(NT_EXP,),
        in_specs=[pl.BlockSpec((TE, D), rows_in), any_space, any_space, any_space],
        out_specs=pl.BlockSpec((TE, D), rows),
        scratch_shapes=[pltpu.VMEM((2, D, DEXP), F32), pltpu.VMEM((2, D, DEXP), F32), pltpu.VMEM((2, DEXP, D), F32),
                        pltpu.VMEM((D, DEXP), BF16), pltpu.VMEM((D, DEXP), BF16), pltpu.VMEM((DEXP, D), BF16),
                        pltpu.SemaphoreType.DMA((2,))],
    )
    return pl.pallas_call(
        _expert_kernel,
        grid_spec=grid_spec,
        out_shape=jax.ShapeDtypeStruct((NP_ROWS, D), F32),
        compiler_params=_cparams(1, VMEM_LIMIT),
        name="experts",
    )(tile_expert, tile_next, tile_slot, n_active, xs, wg, wu, wd)


def _combine_kernel(pos_ref, posn_ref, info_ref, x1_ref, gt2_ref, fg_ref, ys_hbm, o_ref, buf, sem):
    g = pl.program_id(0)
    ng = pl.num_programs(0)

    def gather(p_ref, half, slot):
        def row_copy(base, jj, kk):
            src = p_ref[0, 2 * (half * TC + base + jj) + kk]
            return pltpu.make_async_copy(ys_hbm.at[pl.ds(src, 1)],
                                         buf.at[slot, kk].at[pl.ds(base, ROW_GROUP)].at[pl.ds(jj, 1)],
                                         sem.at[slot])
        _start_rows(TC, row_copy)

    def finish(half, slot):
        _wait_rows(TC, pltpu.make_async_copy(ys_hbm.at[pl.ds(0, 1)], buf.at[slot, 0].at[pl.ds(0, 1)],
                                             sem.at[slot]))
        rows = slice(half * TC, (half + 1) * TC)
        info = info_ref[rows, :]
        lane = lax.broadcasted_iota(I32, info.shape, 1)
        moe = _lane_pick(info, lane, 4) * buf[slot, 0] + _lane_pick(info, lane, 5) * buf[slot, 1]
        x2 = x1_ref[rows, :] + gt2_ref[...] * moe
        o_ref[rows, :] = _rms(x2) * fg_ref[...]

    @pl.when(g == 0)
    def _():
        gather(pos_ref, 0, 0)

    gather(pos_ref, 1, 1)
    finish(0, 0)

    @pl.when(g + 1 < ng)
    def _():
        gather(posn_ref, 0, 0)

    finish(1, 1)


def _combine(pos3, info, x1, gt2, fg, ys):
    step = 2 * TC
    tpb = L // step
    nsteps = T // step
    return pl.pallas_call(
        _combine_kernel,
        grid=(nsteps,),
        in_specs=[pl.BlockSpec((None, 1, 2 * step), lambda i: (i, 0, 0), memory_space=pltpu.SMEM),
                  pl.BlockSpec((None, 1, 2 * step), lambda i: (jnp.minimum(i + 1, nsteps - 1), 0, 0),
                               memory_space=pltpu.SMEM),
                  pl.BlockSpec((step, V7X_LANES), lambda i: (i, 0)),
                  pl.BlockSpec((step, D), lambda i: (i, 0)),
                  pl.BlockSpec((None, 1, D), lambda i: (i // tpb, 0, 0)),
                  pl.BlockSpec((1, D), lambda i: (0, 0)),
                  pl.BlockSpec(memory_space=pl.ANY)],
        out_specs=pl.BlockSpec((step, D), lambda i: (i, 0)),
        out_shape=jax.ShapeDtypeStruct((T, D), F32),
        scratch_shapes=[pltpu.VMEM((2, 2, TC, D), F32), pltpu.SemaphoreType.DMA((2,))],
        compiler_params=_cparams(1, VMEM_LIMIT),
        name="combine",
    )(pos3, pos3, info, x1, gt2, fg, ys)


def _pos_tables():
    rows = L // GRID_W
    quarter = D // 4
    omega = 1.0 / (10000.0 ** (jnp.arange(quarter, dtype=F32) / quarter))

    def axis_emb(pos):
        a = pos[:, None] * omega[None, :]
        return jnp.concatenate([jnp.sin(a), jnp.cos(a)], axis=-1)

    er = axis_emb(jnp.arange(rows, dtype=F32))
    ec = axis_emb(jnp.arange(GRID_W, dtype=F32))
    return er, ec


def _filter_features():
    z = np.zeros((L, V7X_LANES), np.float64)
    bands = (HY_EMB - 1) // 2
    ang = (2.0 * np.pi * np.arange(L) / L)[:, None] * np.linspace(1e-4, bands - 1, bands)[None, :]
    z[:, 0] = np.linspace(0.0, 1.0, L)
    z[:, 1:1 + bands] = np.cos(ang)
    z[:, 1 + bands:HY_EMB] = -np.sin(ang)
    z[1:, HY_EMB] = 1.0
    z = z.reshape(L // FFT_P, FFT_P, V7X_LANES).transpose(1, 0, 2).reshape(L, V7X_LANES)
    return jnp.asarray(z.astype(np.float32))


def _pad2(a, rows, cols):
    return jnp.pad(a, ((0, rows - a.shape[0]), (0, cols - a.shape[1])))


def kernel(x, c, ctx, c_ctx, ada_w, ada_b, norm1_g, norm2_g, w_in, hgrn_lb, hgrn_norm_g, hy_conv_w, hy_conv_b, hy_filt_w1, hy_filt_b1, hy_filt_freq1, hy_filt_w2, hy_filt_b2, hy_filt_freq2, hy_filt_w3, hy_d, w_proj_a, w_proj_b, w_out, moe_router_g_w, moe_router_g_b, moe_router_e_w, moe_router_e_b, moe_w_gate, moe_w_up, moe_w_down, final_norm_g):
    cvec = jnp.zeros((8, D), F32).at[0:B].set(c).at[B].set(c_ctx)
    mod = _adaln(cvec, ada_w[0], ada_b[0][None, :])
    m6 = mod.reshape(8, 6, D)
    sh1, sc1, gt1, sh2, sc2, gt2 = [m6[0:B, k][:, None, :] for k in range(6)]
    csh1, csc1 = m6[B:B + 1, 0], m6[B:B + 1, 1]

    lbs = jnp.cumsum(jax.nn.softmax(hgrn_lb.astype(F32), axis=0), axis=0)[0]
    g1 = norm1_g[0][None, :]
    er, ec = _pos_tables()
    x2 = x.reshape(T, D)

    w_ctx = w_in[0][:, KD:4 * KD].astype(BF16)
    s_f, s_b = _context_states(ctx, g1, csh1, csc1, w_ctx, lbs)

    pz, p = _in_projection(x2, er, ec, g1, sh1, sc1, w_in[0].astype(BF16))

    o_f = _hgrn_scan(pz, p, lbs[0:1], s_f, False)
    y_a = _hgrn_scan(pz, p, lbs[1:2], s_b, True, o_f=o_f, norm_g=hgrn_norm_g[0][None, :])

    vx, x0c = _hyena_pre(p, hy_conv_w[0], hy_conv_b[0][None, :])
    deltas = jnp.abs(jnp.linspace(math.log(HY_DECAY_TARGET) / HY_SLOW_PCT,
                                  math.log(HY_DECAY_TARGET) / HY_FAST_PCT, HYW, dtype=F32))[None, :]
    ln = V7X_LANES
    fh = hy_filt_w2.shape[-1]
    blockdiag = lambda m: jnp.concatenate([_pad2(m, m.shape[0], 2 * m.shape[1]),
                                           jnp.pad(m, ((0, 0), (m.shape[1], 0)))], axis=0)
    twice = lambda v: jnp.concatenate([v, v])[None, :]
    w3 = hy_filt_w3[0]
    f_first, g_last, fr, fi, tw, twh = _dft_tables()
    na = L // FFT_P
    ak = _filter_taps_dft(
        _filter_features(),
        _hi_lo(blockdiag(_pad2(hy_filt_w1[0], ln, fh))), twice(hy_filt_b1[0]), twice(hy_filt_freq1[0]),
        _hi_lo(blockdiag(hy_filt_w2[0])), twice(hy_filt_b2[0]), twice(hy_filt_freq2[0]),
        _hi_lo(_pad2(w3, ln, 2 * HYW)), _hi_lo(jnp.pad(w3, ((fh, 0), (0, 0)))), deltas, f_first)
    kh, khh = _kernel_spectrum(ak.reshape(2, FFT_H, 2, FFT_P, HYW), fr, fi, tw, twh)
    au = _strided_dft(vx.reshape(B, na, FFT_P, HYW), f_first, "dft_first")
    bp = _fft_mid(au.reshape(B, FFT_H, 2, FFT_P, HYW), kh, khh, fr, fi, tw, twh)
    conv = _strided_dft(bp.reshape(B, FFT_P, FFT_P, HYW), g_last, "dft_last").reshape(T, HYW)

    wr = jnp.concatenate([jnp.transpose(moe_router_e_w[0], (1, 0, 2)).reshape(D, NEXP),
                          moe_router_g_w[0], jnp.zeros((D, V7X_LANES - NEXP - NGRP), F32)], axis=1)
    wr_hi = wr.astype(BF16)
    wr = jnp.concatenate([wr_hi, (wr - wr_hi.astype(F32)).astype(BF16)], axis=1)
    br = jnp.concatenate([moe_router_e_b[0].reshape(NEXP), moe_router_g_b[0],
                          jnp.zeros((V7X_LANES - NEXP - NGRP,), F32)])[None, :]
    x1, t2, lg = _merge(y_a.reshape(T, KD), conv, vx, x0c, p, x2, er, ec, gt1, hy_d[0][None, :],
                        w_proj_a[0].astype(BF16), w_proj_b[0].astype(BF16), w_out[0].astype(BF16),
                        norm2_g[0][None, :], sh2, sc2, wr, br)

    info, counts = _route(lg)
    cnt = counts[0, :NEXP].astype(I32)
    pc = ((cnt + TE - 1) // TE) * TE
    ends = jnp.cumsum(pc)
    starts = ends - pc
    n_active = (ends[-1] // TE).astype(I32)[None]
    tile_rows = jnp.arange(NT_EXP, dtype=I32) * TE
    tile_expert = jnp.minimum(jnp.sum((ends[None, :] <= tile_rows[:, None]).astype(I32), axis=1), NEXP - 1)
    meta = jnp.concatenate([jnp.where(pc > 0, ends - TE, -1), starts, n_active]).astype(I32)[None, :]
    starts_row = jnp.pad(starts.astype(F32), (0, V7X_LANES - NEXP))[None, :]
    pos3 = _positions(info, starts_row)[:, :2].reshape(T // TS, 1, 2 * TS)

    xs = _scatter_rows(pos3, meta, t2, jnp.zeros((TE, D), F32))
    eid = jnp.arange(NEXP, dtype=I32)
    nonempty = pc > 0
    later = jnp.where(nonempty[None, :] & (eid[None, :] > eid[:, None]), eid[None, :], NEXP)
    next_e = jnp.min(later, axis=1)
    next_e = jnp.where(next_e < NEXP, next_e, -1).astype(I32)
    slot_e = ((jnp.cumsum(nonempty.astype(I32)) - 1) % 2).astype(I32)
    ys = _experts(tile_expert, next_e[tile_expert], slot_e[tile_expert], n_active, xs,
                  moe_w_gate[0].reshape(NEXP, D, DEXP), moe_w_up[0].reshape(NEXP, D, DEXP),
                  moe_w_down[0].reshape(NEXP, DEXP, D))
    out = _combine(pos3, info, x1, gt2, final_norm_g[None, :], ys)
    return out.reshape(B, L, D)
```
